```python
import jax, jax.numpy as jnp
from jax import lax
import numpy as np

D_MODEL = 1024
BATCH = 8
SEQ = 8192
DEPTH = 4

N_META = 16
BLOCK = 128
PAD = BLOCK - N_META
MLA_HEADS = 8
MLA_NOPE = 64
MLA_ROPE = 32
MLA_QK = MLA_NOPE + MLA_ROPE
MLA_V = 64
Q_LORA = 384
KV_LORA = 256
ROPE_BASE = 10000.0
FOX_HEADS = 8
FOX_DIM = 64
CONV_WIDTH = 3
D_FF = 4 * D_MODEL
EPS = 1e-6
NEG = -1e30
N_EVEN = (DEPTH + 1) // 2
N_ODD = DEPTH // 2
ATTN_SPLITS = (Q_LORA, KV_LORA, MLA_ROPE, FOX_HEADS * FOX_DIM, FOX_HEADS * FOX_DIM, FOX_HEADS * FOX_DIM, FOX_HEADS)
ATTN_IN = Q_LORA + KV_LORA + MLA_ROPE + 3 * FOX_HEADS * FOX_DIM + FOX_HEADS
MIX_OUT = MLA_HEADS * MLA_V + FOX_HEADS * FOX_DIM

kernel_name = "hybrid_mla_fox_shortconv_trunk"


def _offsets(sizes):
    out, acc = [], 0
    for s in sizes[:-1]:
        acc += s
        out.append(acc)
    return out


def rms_norm(x, g):
    xf = x.astype(jnp.float32)
    y = xf * lax.rsqrt(jnp.mean(xf * xf, axis=-1, keepdims=True) + EPS)
    return (y * g.astype(jnp.float32)).astype(x.dtype)


def rope_tables(length):
    pos = jnp.arange(length, dtype=jnp.float32)
    inv_freq = ROPE_BASE ** (-jnp.arange(0, MLA_ROPE, 2, dtype=jnp.float32) / MLA_ROPE)
    ang = pos[:, None] * inv_freq[None, :]
    return jnp.cos(ang), jnp.sin(ang)


def rope_tail(x, cos, sin):
    x_nope = x[..., :MLA_NOPE]
    xr = x[..., MLA_NOPE:].astype(jnp.float32)
    x1, x2 = xr[..., : MLA_ROPE // 2], xr[..., MLA_ROPE // 2:]
    c, s = cos[None, :, None, :], sin[None, :, None, :]
    rot = jnp.concatenate([x1 * c - x2 * s, x2 * c + x1 * s], axis=-1).astype(x.dtype)
    return jnp.concatenate([x_nope, rot], axis=-1)


def pad_front(x):
    return jnp.pad(x, [(0, 0), (PAD, 0)] + [(0, 0)] * (x.ndim - 2))


def blocked_causal_attention(q, k, v, scale, cum_log_f=None):
    b, lp, h, dk = q.shape
    nb = lp // BLOCK
    key_pos = jnp.arange(lp)
    qb = q.reshape(b, nb, BLOCK, h, dk).transpose(1, 0, 2, 3, 4)
    use_decay = cum_log_f is not None
    if use_decay:
        f_bh = cum_log_f.transpose(0, 2, 1)
        f_q = f_bh.reshape(b, h, nb, BLOCK).transpose(2, 0, 1, 3)
        xs = (jnp.arange(nb), qb, f_q)
    else:
        xs = (jnp.arange(nb), qb)

    def one_block(args):
        if use_decay:
            i, q_blk, fq = args
        else:
            i, q_blk = args
        s = jnp.einsum('bqhd,bkhd->bhqk', q_blk, k, preferred_element_type=jnp.float32) * scale
        if use_decay:
            s = s + fq[..., :, None] - f_bh[:, :, None, :]
        q_pos = i * BLOCK + jnp.arange(BLOCK)
        mask = (key_pos[None, :] <= q_pos[:, None]) & (key_pos[None, :] >= PAD)
        s = jnp.where(mask[None, None], s, NEG)
        p = jax.nn.softmax(s, axis=-1)
        return jnp.einsum('bhqk,bkhd->bqhd', p.astype(v.dtype), v)

    out = lax.map(one_block, xs)
    return out.transpose(1, 0, 2, 3, 4).reshape(b, lp, h, v.shape[-1])


def attention_mixer(h, cos, sin, w_in, g_cq, w_uq, g_ckv, w_ukv, g_q_mla, g_k_mla,
                    g_q_fox, g_k_fox, b_forget, w_out):
    b, l, _ = h.shape
    z = h @ w_in
    c_q, c_kv, k_pe, fq, fk, fv, f_logit = jnp.split(z, _offsets(ATTN_SPLITS), axis=-1)

    q = (rms_norm(c_q, g_cq) @ w_uq).reshape(b, l, MLA_HEADS, MLA_QK)
    kv = (rms_norm(c_kv, g_ckv) @ w_ukv).reshape(b, l, MLA_HEADS, MLA_NOPE + MLA_V)
    k_nope, v_mla = kv[..., :MLA_NOPE], kv[..., MLA_NOPE:]
    k_rope = jnp.broadcast_to(k_pe[:, :, None, :], (b, l, MLA_HEADS, MLA_ROPE))
    k = jnp.concatenate([k_nope, k_rope], axis=-1)
    q = rope_tail(rms_norm(q, g_q_mla), cos, sin)
    k = rope_tail(rms_norm(k, g_k_mla), cos, sin)
    o_mla = blocked_causal_attention(pad_front(q), pad_front(k), pad_front(v_mla),
                                     MLA_QK ** -0.5)[:, PAD:]
    o_mla = o_mla.reshape(b, l, MLA_HEADS * MLA_V)

    qf = rms_norm(fq.reshape(b, l, FOX_HEADS, FOX_DIM), g_q_fox)
    kf = rms_norm(fk.reshape(b, l, FOX_HEADS, FOX_DIM), g_k_fox)
    vf = fv.reshape(b, l, FOX_HEADS, FOX_DIM)
    log_f = jax.nn.log_sigmoid(f_logit.astype(jnp.float32) + b_forget.astype(jnp.float32))
    cum_log_f = jnp.cumsum(pad_front(log_f), axis=1)
    o_fox = blocked_causal_attention(pad_front(qf), pad_front(kf), pad_front(vf),
                                     FOX_DIM ** -0.5, cum_log_f)[:, PAD:]
    o_fox = o_fox.reshape(b, l, FOX_HEADS * FOX_DIM)

    return jnp.concatenate([o_mla, o_fox], axis=-1) @ w_out


def short_conv_mixer(h, w_in, conv_w, w_out):
    z = h @ w_in
    gate_b, gate_c, u = jnp.split(z, 3, axis=-1)
    g = gate_c * u
    y = lax.conv_general_dilated(
        g, conv_w[:, None, :].astype(g.dtype), window_strides=(1,),
        padding=[(CONV_WIDTH - 1, 0)], dimension_numbers=('NWC', 'WIO', 'NWC'),
        feature_group_count=D_MODEL)
    return (gate_b * y) @ w_out


def sq_relu_mlp(h, w_up, w_down):
    return jnp.square(jax.nn.relu(h @ w_up)) @ w_down


def _fwd_setup_inputs(seed: int = 0) -> dict:
    key = jax.random.key(seed)
    ks = iter(jax.random.split(key, 32))

    def nrm(shape, scale):
        return jax.random.normal(next(ks), shape, jnp.float32) * scale

    def gain(shape):
        return 1.0 + 0.02 * jax.random.normal(next(ks), shape, jnp.float32)

    out_scale = (2.0 * DEPTH) ** -0.5
    return {
        "x": nrm((BATCH, SEQ, D_MODEL), 1.0),
        "meta_tokens": nrm((N_META, D_MODEL), 1.0),
        "g_mix": gain((DEPTH, D_MODEL)),
        "g_mlp": gain((DEPTH, D_MODEL)),
        "w_in_attn": nrm((N_EVEN, D_MODEL, ATTN_IN), D_MODEL ** -0.5),
        "g_cq": gain((N_EVEN, Q_LORA)),
        "w_uq": nrm((N_EVEN, Q_LORA, MLA_HEADS * MLA_QK), Q_LORA ** -0.5),
        "g_ckv": gain((N_EVEN, KV_LORA)),
        "w_ukv": nrm((N_EVEN, KV_LORA, MLA_HEADS * (MLA_NOPE + MLA_V)), KV_LORA ** -0.5),
        "g_q_mla": gain((N_EVEN, MLA_QK)),
        "g_k_mla": gain((N_EVEN, MLA_QK)),
        "g_q_fox": gain((N_EVEN, FOX_DIM)),
        "g_k_fox": gain((N_EVEN, FOX_DIM)),
        "b_forget": 2.0 + nrm((N_EVEN, FOX_HEADS), 0.1),
        "w_out_attn": nrm((N_EVEN, MIX_OUT, D_MODEL), MIX_OUT ** -0.5 * out_scale),
        "w_in_conv": nrm((N_ODD, D_MODEL, 3 * D_MODEL), D_MODEL ** -0.5),
        "conv_w": nrm((N_ODD, CONV_WIDTH, D_MODEL), CONV_WIDTH ** -0.5),
        "w_out_conv": nrm((N_ODD, D_MODEL, D_MODEL), D_MODEL ** -0.5 * out_scale),
        "w_mlp_up": nrm((DEPTH, D_MODEL, D_FF), D_MODEL ** -0.5),
        "w_mlp_down": nrm((DEPTH, D_FF, D_MODEL), D_FF ** -0.5 * out_scale),
    }


def _fwd_reference(x, meta_tokens, g_mix, g_mlp, w_in_attn, g_cq, w_uq, g_ckv, w_ukv,
              g_q_mla, g_k_mla, g_q_fox, g_k_fox, b_forget, w_out_attn,
              w_in_conv, conv_w, w_out_conv, w_mlp_up, w_mlp_down):
    b = x.shape[0]
    meta = jnp.broadcast_to(meta_tokens.astype(x.dtype)[None], (b, N_META, D_MODEL))
    h = jnp.concatenate([meta, x], axis=1)
    cos, sin = rope_tables(h.shape[1])
    for layer in range(DEPTH):
        j = layer // 2
        hn = rms_norm(h, g_mix[layer])
        if layer % 2 == 0:
            h = h + attention_mixer(hn, cos, sin, w_in_attn[j], g_cq[j], w_uq[j], g_ckv[j],
                                    w_ukv[j], g_q_mla[j], g_k_mla[j], g_q_fox[j], g_k_fox[j],
                                    b_forget[j], w_out_attn[j])
        else:
            h = h + short_conv_mixer(hn, w_in_conv[j], conv_w[j], w_out_conv[j])
        h = h + sq_relu_mlp(rms_norm(h, g_mlp[layer]), w_mlp_up[layer], w_mlp_down[layer])
    return h[:, N_META:]


import jax as _jax
import jax.numpy as _jnp

TWIN_FORMAT = 'train_step'
FWD_PARAMS = ['x', 'meta_tokens', 'g_mix', 'g_mlp', 'w_in_attn', 'g_cq', 'w_uq', 'g_ckv', 'w_ukv', 'g_q_mla', 'g_k_mla', 'g_q_fox', 'g_k_fox', 'b_forget', 'w_out_attn', 'w_in_conv', 'conv_w', 'w_out_conv', 'w_mlp_up', 'w_mlp_down']
TWIN_WEIGHTS = ['meta_tokens', 'g_mix', 'g_mlp', 'w_in_attn', 'g_cq', 'w_uq', 'g_ckv', 'w_ukv', 'g_q_mla', 'g_k_mla', 'g_q_fox', 'g_k_fox', 'b_forget', 'w_out_attn', 'w_in_conv', 'conv_w', 'w_out_conv', 'w_mlp_up', 'w_mlp_down']
TWIN_DIFF_INPUT = 'x'
TWIN_INPUTS = ['x', 'meta_tokens', 'g_mix', 'g_mlp', 'w_in_attn', 'g_cq', 'w_uq', 'g_ckv', 'w_ukv', 'g_q_mla', 'g_k_mla', 'g_q_fox', 'g_k_fox', 'b_forget', 'w_out_attn', 'w_in_conv', 'conv_w', 'w_out_conv', 'w_mlp_up', 'w_mlp_down', 'loss_target', 'm_meta_tokens', 'm_g_mix', 'm_g_mlp', 'm_w_in_attn', 'm_g_cq', 'm_w_uq', 'm_g_ckv', 'm_w_ukv', 'm_g_q_mla', 'm_g_k_mla', 'm_g_q_fox', 'm_g_k_fox', 'm_b_forget', 'm_w_out_attn', 'm_w_in_conv', 'm_conv_w', 'm_w_out_conv', 'm_w_mlp_up', 'm_w_mlp_down', 'v_meta_tokens', 'v_g_mix', 'v_g_mlp', 'v_w_in_attn', 'v_g_cq', 'v_w_uq', 'v_g_ckv', 'v_w_ukv', 'v_g_q_mla', 'v_g_k_mla', 'v_g_q_fox', 'v_g_k_fox', 'v_b_forget', 'v_w_out_attn', 'v_w_in_conv', 'v_conv_w', 'v_w_out_conv', 'v_w_mlp_up', 'v_w_mlp_down']
TWIN_OUTPUTS = ['loss', 'grad_x', 'grad_meta_tokens', 'grad_g_mix', 'grad_g_mlp', 'grad_w_in_attn', 'grad_g_cq', 'grad_w_uq', 'grad_g_ckv', 'grad_w_ukv', 'grad_g_q_mla', 'grad_g_k_mla', 'grad_g_q_fox', 'grad_g_k_fox', 'grad_b_forget', 'grad_w_out_attn', 'grad_w_in_conv', 'grad_conv_w', 'grad_w_out_conv', 'grad_w_mlp_up', 'grad_w_mlp_down', 'delta_meta_tokens', 'delta_g_mix', 'delta_g_mlp', 'delta_w_in_attn', 'delta_g_cq', 'delta_w_uq', 'delta_g_ckv', 'delta_w_ukv', 'delta_g_q_mla', 'delta_g_k_mla', 'delta_g_q_fox', 'delta_g_k_fox', 'delta_b_forget', 'delta_w_out_attn', 'delta_w_in_conv', 'delta_conv_w', 'delta_w_out_conv', 'delta_w_mlp_up', 'delta_w_mlp_down', 'new_m_meta_tokens', 'new_m_g_mix', 'new_m_g_mlp', 'new_m_w_in_attn', 'new_m_g_cq', 'new_m_w_uq', 'new_m_g_ckv', 'new_m_w_ukv', 'new_m_g_q_mla', 'new_m_g_k_mla', 'new_m_g_q_fox', 'new_m_g_k_fox', 'new_m_b_forget', 'new_m_w_out_attn', 'new_m_w_in_conv', 'new_m_conv_w', 'new_m_w_out_conv', 'new_m_w_mlp_up', 'new_m_w_mlp_down', 'new_v_meta_tokens', 'new_v_g_mix', 'new_v_g_mlp', 'new_v_w_in_attn', 'new_v_g_cq', 'new_v_w_uq', 'new_v_g_ckv', 'new_v_w_ukv', 'new_v_g_q_mla', 'new_v_g_k_mla', 'new_v_g_q_fox', 'new_v_g_k_fox', 'new_v_b_forget', 'new_v_w_out_attn', 'new_v_w_in_conv', 'new_v_conv_w', 'new_v_w_out_conv', 'new_v_w_mlp_up', 'new_v_w_mlp_down']
TWIN_LEAF_KINDS = {'loss': 'loss', 'grad_x': 'grad_x', 'grad_meta_tokens': 'grad_w', 'grad_g_mix': 'grad_w', 'grad_g_mlp': 'grad_w', 'grad_w_in_attn': 'grad_w', 'grad_g_cq': 'grad_w', 'grad_w_uq': 'grad_w', 'grad_g_ckv': 'grad_w', 'grad_w_ukv': 'grad_w', 'grad_g_q_mla': 'grad_w', 'grad_g_k_mla': 'grad_w', 'grad_g_q_fox': 'grad_w', 'grad_g_k_fox': 'grad_w', 'grad_b_forget': 'grad_w', 'grad_w_out_attn': 'grad_w', 'grad_w_in_conv': 'grad_w', 'grad_conv_w': 'grad_w', 'grad_w_out_conv': 'grad_w', 'grad_w_mlp_up': 'grad_w', 'grad_w_mlp_down': 'grad_w', 'delta_meta_tokens': 'delta_w', 'delta_g_mix': 'delta_w', 'delta_g_mlp': 'delta_w', 'delta_w_in_attn': 'delta_w', 'delta_g_cq': 'delta_w', 'delta_w_uq': 'delta_w', 'delta_g_ckv': 'delta_w', 'delta_w_ukv': 'delta_w', 'delta_g_q_mla': 'delta_w', 'delta_g_k_mla': 'delta_w', 'delta_g_q_fox': 'delta_w', 'delta_g_k_fox': 'delta_w', 'delta_b_forget': 'delta_w', 'delta_w_out_attn': 'delta_w', 'delta_w_in_conv': 'delta_w', 'delta_conv_w': 'delta_w', 'delta_w_out_conv': 'delta_w', 'delta_w_mlp_up': 'delta_w', 'delta_w_mlp_down': 'delta_w', 'new_m_meta_tokens': 'new_m', 'new_m_g_mix': 'new_m', 'new_m_g_mlp': 'new_m', 'new_m_w_in_attn': 'new_m', 'new_m_g_cq': 'new_m', 'new_m_w_uq': 'new_m', 'new_m_g_ckv': 'new_m', 'new_m_w_ukv': 'new_m', 'new_m_g_q_mla': 'new_m', 'new_m_g_k_mla': 'new_m', 'new_m_g_q_fox': 'new_m', 'new_m_g_k_fox': 'new_m', 'new_m_b_forget': 'new_m', 'new_m_w_out_attn': 'new_m', 'new_m_w_in_conv': 'new_m', 'new_m_conv_w': 'new_m', 'new_m_w_out_conv': 'new_m', 'new_m_w_mlp_up': 'new_m', 'new_m_w_mlp_down': 'new_m', 'new_v_meta_tokens': 'new_v', 'new_v_g_mix': 'new_v', 'new_v_g_mlp': 'new_v', 'new_v_w_in_attn': 'new_v', 'new_v_g_cq': 'new_v', 'new_v_w_uq': 'new_v', 'new_v_g_ckv': 'new_v', 'new_v_w_ukv': 'new_v', 'new_v_g_q_mla': 'new_v', 'new_v_g_k_mla': 'new_v', 'new_v_g_q_fox': 'new_v', 'new_v_g_k_fox': 'new_v', 'new_v_b_forget': 'new_v', 'new_v_w_out_attn': 'new_v', 'new_v_w_in_conv': 'new_v', 'new_v_conv_w': 'new_v', 'new_v_w_out_conv': 'new_v', 'new_v_w_mlp_up': 'new_v', 'new_v_w_mlp_down': 'new_v'}


def _forward(args):
    return _fwd_reference(*[args[k] for k in FWD_PARAMS])


def _output_shape():
    def fwd():
        inp = _fwd_setup_inputs(0)
        return _fwd_reference(*[inp[k] for k in FWD_PARAMS])
    out = _jax.eval_shape(fwd)
    return out.shape, out.dtype

N_MICROBATCH = 1
ADAM_LR = 0.001
ADAM_B1 = 0.9
ADAM_B2 = 0.999
ADAM_EPS = 1e-08
ADAM_WD = 0.01
ADAM_STEP = 10
PER_EXAMPLE_BATCH_AXIS = {'x': 0, 'loss_target': 0}
SHARED_INPUTS = []
_WEIGHT_DTYPES = {'meta_tokens': _jnp.float32, 'g_mix': _jnp.float32, 'g_mlp': _jnp.float32, 'w_in_attn': _jnp.float32, 'g_cq': _jnp.float32, 'w_uq': _jnp.float32, 'g_ckv': _jnp.float32, 'w_ukv': _jnp.float32, 'g_q_mla': _jnp.float32, 'g_k_mla': _jnp.float32, 'g_q_fox': _jnp.float32, 'g_k_fox': _jnp.float32, 'b_forget': _jnp.float32, 'w_out_attn': _jnp.float32, 'w_in_conv': _jnp.float32, 'conv_w': _jnp.float32, 'w_out_conv': _jnp.float32, 'w_mlp_up': _jnp.float32, 'w_mlp_down': _jnp.float32}
MOMENT_SCALE = {'meta_tokens': 2.791894e-02, 'g_mix': 1.699998e+01, 'g_mlp': 2.451846e+01, 'w_in_attn': 8.738770e-01, 'g_cq': 5.255007e-02, 'w_uq': 3.981757e-02, 'g_ckv': 1.725794e+00, 'w_ukv': 8.830604e-01, 'g_q_mla': 1.919079e-01, 'g_k_mla': 1.923636e-01, 'g_q_fox': 4.065855e+00, 'g_k_fox': 4.059655e+00, 'b_forget': 2.660310e+01, 'w_out_attn': 3.507671e+00, 'w_in_conv': 4.620199e-01, 'conv_w': 4.548123e+00, 'w_out_conv': 8.770296e-01, 'w_mlp_up': 8.753895e-01, 'w_mlp_down': 1.427796e+01}


def _to_microbatches(a, axis):
    t = _jnp.moveaxis(a, axis, 0)
    t = t.reshape((N_MICROBATCH, t.shape[0] // N_MICROBATCH) + t.shape[1:])
    return _jnp.moveaxis(t, 1, axis + 1)


def setup_inputs(seed: int = 0) -> dict:
    inp = _fwd_setup_inputs(seed)
    key = _jax.random.fold_in(_jax.random.key(seed), 7919)
    shape, _ = _output_shape()
    out = dict(inp)
    out["loss_target"] = _jax.random.normal(_jax.random.fold_in(key, 0), shape, _jnp.float32)
    for i, name in enumerate(TWIN_WEIGHTS):
        w = inp[name].astype(_jnp.float32)
        if MOMENT_SCALE is None:
            s = _jnp.sqrt(_jnp.mean(_jnp.square(w)) + 1e-30)
        else:
            s = MOMENT_SCALE[name]
        km, kv = _jax.random.split(_jax.random.fold_in(key, i + 1))
        out[name] = w
        out["m_" + name] = s * _jax.random.normal(km, w.shape, _jnp.float32)
        out["v_" + name] = (s * s) * _jax.random.uniform(kv, w.shape, _jnp.float32, 0.5, 1.5)
    if N_MICROBATCH > 1:
        for name, axis in PER_EXAMPLE_BATCH_AXIS.items():
            out[name] = _to_microbatches(out[name], axis)
    return {'x': out['x'], 'meta_tokens': out['meta_tokens'], 'g_mix': out['g_mix'], 'g_mlp': out['g_mlp'], 'w_in_attn': out['w_in_attn'], 'g_cq': out['g_cq'], 'w_uq': out['w_uq'], 'g_ckv': out['g_ckv'], 'w_ukv': out['w_ukv'], 'g_q_mla': out['g_q_mla'], 'g_k_mla': out['g_k_mla'], 'g_q_fox': out['g_q_fox'], 'g_k_fox': out['g_k_fox'], 'b_forget': out['b_forget'], 'w_out_attn': out['w_out_attn'], 'w_in_conv': out['w_in_conv'], 'conv_w': out['conv_w'], 'w_out_conv': out['w_out_conv'], 'w_mlp_up': out['w_mlp_up'], 'w_mlp_down': out['w_mlp_down'], 'loss_target': out['loss_target'], 'm_meta_tokens': out['m_meta_tokens'], 'm_g_mix': out['m_g_mix'], 'm_g_mlp': out['m_g_mlp'], 'm_w_in_attn': out['m_w_in_attn'], 'm_g_cq': out['m_g_cq'], 'm_w_uq': out['m_w_uq'], 'm_g_ckv': out['m_g_ckv'], 'm_w_ukv': out['m_w_ukv'], 'm_g_q_mla': out['m_g_q_mla'], 'm_g_k_mla': out['m_g_k_mla'], 'm_g_q_fox': out['m_g_q_fox'], 'm_g_k_fox': out['m_g_k_fox'], 'm_b_forget': out['m_b_forget'], 'm_w_out_attn': out['m_w_out_attn'], 'm_w_in_conv': out['m_w_in_conv'], 'm_conv_w': out['m_conv_w'], 'm_w_out_conv': out['m_w_out_conv'], 'm_w_mlp_up': out['m_w_mlp_up'], 'm_w_mlp_down': out['m_w_mlp_down'], 'v_meta_tokens': out['v_meta_tokens'], 'v_g_mix': out['v_g_mix'], 'v_g_mlp': out['v_g_mlp'], 'v_w_in_attn': out['v_w_in_attn'], 'v_g_cq': out['v_g_cq'], 'v_w_uq': out['v_w_uq'], 'v_g_ckv': out['v_g_ckv'], 'v_w_ukv': out['v_w_ukv'], 'v_g_q_mla': out['v_g_q_mla'], 'v_g_k_mla': out['v_g_k_mla'], 'v_g_q_fox': out['v_g_q_fox'], 'v_g_k_fox': out['v_g_k_fox'], 'v_b_forget': out['v_b_forget'], 'v_w_out_attn': out['v_w_out_attn'], 'v_w_in_conv': out['v_w_in_conv'], 'v_conv_w': out['v_conv_w'], 'v_w_out_conv': out['v_w_out_conv'], 'v_w_mlp_up': out['v_w_mlp_up'], 'v_w_mlp_down': out['v_w_mlp_down']}


def _loss(weights, diff, rest, loss_target):
    with _jax.named_scope("forward"):
        args = {**rest, TWIN_DIFF_INPUT: diff, **{k: w.astype(_WEIGHT_DTYPES[k]) for k, w in weights.items()}}
        y = _forward(args)
    with _jax.named_scope("loss_head"):
        err = _jnp.square(y.astype(_jnp.float32) - loss_target)
        return 0.5 * _jnp.sum(_jnp.mean(err, axis=-1)) if err.ndim else 0.5 * err


def _adamw(w, g, m, v):
    m = ADAM_B1 * m + (1.0 - ADAM_B1) * g
    v = ADAM_B2 * v + (1.0 - ADAM_B2) * _jnp.square(g)
    m_hat = m / (1.0 - ADAM_B1 ** ADAM_STEP)
    v_hat = v / (1.0 - ADAM_B2 ** ADAM_STEP)
    delta = -ADAM_LR * (m_hat / (_jnp.sqrt(v_hat) + ADAM_EPS) + ADAM_WD * w)
    return delta, m, v


def reference(x, meta_tokens, g_mix, g_mlp, w_in_attn, g_cq, w_uq, g_ckv, w_ukv, g_q_mla, g_k_mla, g_q_fox, g_k_fox, b_forget, w_out_attn, w_in_conv, conv_w, w_out_conv, w_mlp_up, w_mlp_down, loss_target, m_meta_tokens, m_g_mix, m_g_mlp, m_w_in_attn, m_g_cq, m_w_uq, m_g_ckv, m_w_ukv, m_g_q_mla, m_g_k_mla, m_g_q_fox, m_g_k_fox, m_b_forget, m_w_out_attn, m_w_in_conv, m_conv_w, m_w_out_conv, m_w_mlp_up, m_w_mlp_down, v_meta_tokens, v_g_mix, v_g_mlp, v_w_in_attn, v_g_cq, v_w_uq, v_g_ckv, v_w_ukv, v_g_q_mla, v_g_k_mla, v_g_q_fox, v_g_k_fox, v_b_forget, v_w_out_attn, v_w_in_conv, v_conv_w, v_w_out_conv, v_w_mlp_up, v_w_mlp_down):
    given = dict(x=x, meta_tokens=meta_tokens, g_mix=g_mix, g_mlp=g_mlp, w_in_attn=w_in_attn, g_cq=g_cq, w_uq=w_uq, g_ckv=g_ckv, w_ukv=w_ukv, g_q_mla=g_q_mla, g_k_mla=g_k_mla, g_q_fox=g_q_fox, g_k_fox=g_k_fox, b_forget=b_forget, w_out_attn=w_out_attn, w_in_conv=w_in_conv, conv_w=conv_w, w_out_conv=w_out_conv, w_mlp_up=w_mlp_up, w_mlp_down=w_mlp_down, loss_target=loss_target, m_meta_tokens=m_meta_tokens, m_g_mix=m_g_mix, m_g_mlp=m_g_mlp, m_w_in_attn=m_w_in_attn, m_g_cq=m_g_cq, m_w_uq=m_w_uq, m_g_ckv=m_g_ckv, m_w_ukv=m_w_ukv, m_g_q_mla=m_g_q_mla, m_g_k_mla=m_g_k_mla, m_g_q_fox=m_g_q_fox, m_g_k_fox=m_g_k_fox, m_b_forget=m_b_forget, m_w_out_attn=m_w_out_attn, m_w_in_conv=m_w_in_conv, m_conv_w=m_conv_w, m_w_out_conv=m_w_out_conv, m_w_mlp_up=m_w_mlp_up, m_w_mlp_down=m_w_mlp_down, v_meta_tokens=v_meta_tokens, v_g_mix=v_g_mix, v_g_mlp=v_g_mlp, v_w_in_attn=v_w_in_attn, v_g_cq=v_g_cq, v_w_uq=v_w_uq, v_g_ckv=v_g_ckv, v_w_ukv=v_w_ukv, v_g_q_mla=v_g_q_mla, v_g_k_mla=v_g_k_mla, v_g_q_fox=v_g_q_fox, v_g_k_fox=v_g_k_fox, v_b_forget=v_b_forget, v_w_out_attn=v_w_out_attn, v_w_in_conv=v_w_in_conv, v_conv_w=v_conv_w, v_w_out_conv=v_w_out_conv, v_w_mlp_up=v_w_mlp_up, v_w_mlp_down=v_w_mlp_down)
    weights = {n: given[n] for n in TWIN_WEIGHTS}
    shared = {n: given[n] for n in SHARED_INPUTS}
    per_example = {n: given[n] for n in ['x']}
    grad_fn = _jax.value_and_grad(_loss, argnums=(0, 1))

    def one_microbatch(ex, loss_target):
        ex = dict(ex)
        diff = ex.pop(TWIN_DIFF_INPUT)
        return grad_fn(weights, diff, {**shared, **ex}, loss_target)

    if N_MICROBATCH == 1:
        loss, (grad_w, grad_x) = one_microbatch(per_example, given["loss_target"])
    else:
        def body(carry, xs):
            loss_sum, grad_sum = carry
            l_k, (gw_k, gx_k) = one_microbatch(xs[0], xs[1])
            with _jax.named_scope("update"):
                return (loss_sum + l_k, _jax.tree.map(_jnp.add, grad_sum, gw_k)), gx_k

        init = (_jnp.zeros((), _jnp.float32), _jax.tree.map(_jnp.zeros_like, weights))
        (loss, grad_w), grad_x = _jax.lax.scan(body, init, (per_example, given["loss_target"]))
    with _jax.named_scope("update"):
        delta_w, new_m, new_v = {}, {}, {}
        for n in TWIN_WEIGHTS:
            delta_w[n], new_m[n], new_v[n] = _adamw(weights[n], grad_w[n], given["m_" + n], given["v_" + n])
    return (loss, grad_x, *[grad_w[n] for n in TWIN_WEIGHTS], *[delta_w[n] for n in TWIN_WEIGHTS],
            *[new_m[n] for n in TWIN_WEIGHTS], *[new_v[n] for n in TWIN_WEIGHTS])
```

```python
import functools

import jax
import jax.numpy as jnp
from jax import lax
from jax.experimental import pallas as pl
from jax.experimental.pallas import tpu as pltpu

F32 = jnp.float32
MXU_DTYPE = jnp.bfloat16

D_MODEL = 1024
N_META = 16
LANES = 128
PAD = LANES - N_META
HEADS = 8
Q_LORA = 384
KV_LORA = 256
MLA_NOPE = 64
MLA_ROPE = 32
MLA_QK = MLA_NOPE + MLA_ROPE
HEAD_V = 64
FOX_DIM = 64
ROPE_BASE = 10000.0
D_FF = 4 * D_MODEL
DEPTH = 4
EPS = 1e-6
NEG = -1e30
ATTN_IN = Q_LORA + KV_LORA + MLA_ROPE + 3 * HEADS * FOX_DIM + HEADS

ZA_CQ = 0
ZA_CKV = Q_LORA
ZA_FQ = ZA_CKV + KV_LORA
ZA_FK = ZA_FQ + HEADS * FOX_DIM
ZA_FV = ZA_FK + HEADS * FOX_DIM
ZA_LAST = ZA_FV + HEADS * FOX_DIM
ZA_W = ZA_LAST + LANES
KPE_LANE = MLA_NOPE

ADAM_LR = 0.001
ADAM_B1 = 0.9
ADAM_B2 = 0.999
ADAM_EPS = 1e-08
ADAM_WD = 0.01
ADAM_STEP = 10

VMEM_LIMIT_BYTES = 52 * 1024 * 1024
ROW_TILE = 640
PREP_TILE = 320
ATTN_BLOCK = 640
LOSS_TILE = 128
MAX_TILE = 1536

MESH_AXES = ("x", "y", "c")
MESH = pl.DeviceIdType.MESH


def _params(*sem):
    return pltpu.CompilerParams(dimension_semantics=sem, vmem_limit_bytes=VMEM_LIMIT_BYTES)


def _tile(n, cap=None):
    cap = MAX_TILE if cap is None else cap
    if n <= cap:
        return n
    best = None
    for t in range(LANES, cap + 1, LANES):
        if n % t == 0:
            best = t
    assert best is not None, n
    return best


def _iota(shape, dim):
    return lax.broadcasted_iota(jnp.int32, shape, dim)


def _matmul(a, b, *, ta=False, tb=False, extras=(), epilogue=None, out_dtypes=(F32,), name, b_tiles=None,
            out_into=None, tm=None, tn=None):
    if ta:
        kdim, m = a.shape
    else:
        m, kdim = a.shape
    row_tile = min(ROW_TILE, m)
    if ta:
        tm_auto, tk = _tile(m), min(ROW_TILE, kdim)
    else:
        tm_auto, tk = (row_tile if m % row_tile == 0 else _tile(m)), _tile(kdim)
    tm = tm_auto if tm is None else tm
    if b_tiles is None:
        n = b.shape[0] if tb else b.shape[1]
        assert (b.shape[1] if tb else b.shape[0]) == kdim, (a.shape, b.shape, ta, tb)
        tn = _tile(n) if tn is None else tn
        b_spec = pl.BlockSpec((tn, tk), lambda i, j, k: (j, k)) if tb else pl.BlockSpec((tk, tn), lambda i, j, k: (k, j))
    else:
        n, tn, tk, b_spec = b_tiles["n"], b_tiles["tn"], b_tiles["tk"], b_tiles["spec"]
    nm, nn, nk = m // tm, n // tn, kdim // tk
    assert nm * tm == m and nn * tn == n and nk * tk == kdim, (m, n, kdim, tm, tn, tk)
    n_ex, n_out = len(extras), len(out_dtypes)
    n_alias = 0 if out_into is None else 1
    assert n_out == 1 or out_into is None
    dims = (((0 if ta else 1,), (1 if tb else 0,)), ((), ()))
    if epilogue is None:
        epilogue = lambda acc: (acc,)

    def body(a_ref, b_ref, *rest):
        ex_refs, out_refs, acc_ref = rest[:n_ex], rest[n_ex + n_alias:n_ex + n_alias + n_out], rest[-1]
        k = pl.program_id(2)

        @pl.when(k == 0)
        def _():
            acc_ref[...] = jnp.zeros_like(acc_ref)

        acc_ref[...] += lax.dot_general(a_ref[...].astype(MXU_DTYPE), b_ref[...].astype(MXU_DTYPE), dims,
                                        preferred_element_type=F32)

        @pl.when(k == nk - 1)
        def _():
            res = epilogue(acc_ref[...], *[e[...] for e in ex_refs])
            for o_ref, r in zip(out_refs, res):
                o_ref[...] = r.astype(o_ref.dtype)

    a_spec = pl.BlockSpec((tk, tm), lambda i, j, k: (k, i)) if ta else pl.BlockSpec((tm, tk), lambda i, j, k: (i, k))
    mn_spec = pl.BlockSpec((tm, tn), lambda i, j, k: (i, j))
    if out_into is None:
        outs = pl.pallas_call(
            body, name=name, grid=(nm, nn, nk),
            in_specs=[a_spec, b_spec] + [mn_spec] * n_ex,
            out_specs=[mn_spec] * n_out,
            out_shape=[jax.ShapeDtypeStruct((m, n), dt) for dt in out_dtypes],
            scratch_shapes=[pltpu.VMEM((tm, tn), F32)],
            compiler_params=_params("parallel", "parallel", "arbitrary"),
        )(a, b, *extras)
        return outs[0] if n_out == 1 else outs
    buf, buf_spec = out_into
    return pl.pallas_call(
        body, name=name, grid=(nm, nn, nk),
        in_specs=[a_spec, b_spec] + [mn_spec] * n_ex + [ANY],
        out_specs=buf_spec,
        out_shape=jax.ShapeDtypeStruct(buf.shape, buf.dtype),
        input_output_aliases={2 + n_ex: 0},
        scratch_shapes=[pltpu.VMEM((tm, tn), F32)],
        compiler_params=_params("parallel", "parallel", "arbitrary"),
    )(a, b, *extras, buf)


def _rmsnorm_fwd(x, g, *, name="rmsnorm_fwd"):
    t, d = x.shape
    bt = min(ROW_TILE, t)

    def body(x_ref, g_ref, o_ref):
        xv = x_ref[...]
        r = lax.rsqrt(jnp.mean(xv * xv, axis=-1, keepdims=True) + EPS)
        o_ref[...] = (xv * r * g_ref[...]).astype(o_ref.dtype)

    return pl.pallas_call(
        body, name=name, grid=(t // bt,),
        in_specs=[pl.BlockSpec((bt, d), lambda i: (i, 0)), pl.BlockSpec((1, d), lambda i: (0, 0))],
        out_specs=pl.BlockSpec((bt, d), lambda i: (i, 0)),
        out_shape=jax.ShapeDtypeStruct((t, d), MXU_DTYPE),
        compiler_params=_params("parallel"),
    )(x, g.reshape(1, d))


def _rmsnorm_bwd(x, g, dy, dres, *, name="rmsnorm_bwd"):
    t, d = x.shape
    bt = min(ROW_TILE, t)

    def body(x_ref, g_ref, dy_ref, dres_ref, dx_ref, dg_ref):
        @pl.when(pl.program_id(0) == 0)
        def _():
            dg_ref[...] = jnp.zeros_like(dg_ref)

        xv, dyv = x_ref[...], dy_ref[...].astype(F32)
        r = lax.rsqrt(jnp.mean(xv * xv, axis=-1, keepdims=True) + EPS)
        xhat = xv * r
        dxhat = dyv * g_ref[...]
        dx = r * (dxhat - xhat * jnp.mean(dxhat * xhat, axis=-1, keepdims=True))
        dx_ref[...] = dres_ref[...] + dx
        dg_ref[...] += jnp.sum(dyv * xhat, axis=0, keepdims=True)

    row = pl.BlockSpec((bt, d), lambda i: (i, 0))
    vec = pl.BlockSpec((1, d), lambda i: (0, 0))
    dx, dg = pl.pallas_call(
        body, name=name, grid=(t // bt,),
        in_specs=[row, vec, row, row], out_specs=[row, vec],
        out_shape=[jax.ShapeDtypeStruct((t, d), F32), jax.ShapeDtypeStruct((1, d), F32)],
        compiler_params=_params("arbitrary"),
    )(x, g.reshape(1, d), dy, dres)
    return dx, dg.reshape(d)


def _pair_rms(x, lo):
    x2 = x * x
    s_lo = jnp.sum(jnp.where(lo, x2, 0.0), axis=-1, keepdims=True)
    s_hi = jnp.sum(jnp.where(lo, 0.0, x2), axis=-1, keepdims=True)
    return jnp.where(lo, lax.rsqrt(s_lo / FOX_DIM + EPS), lax.rsqrt(s_hi / FOX_DIM + EPS))


def _pair_sum(x, lo):
    s_lo = jnp.sum(jnp.where(lo, x, 0.0), axis=-1, keepdims=True)
    s_hi = jnp.sum(jnp.where(lo, 0.0, x), axis=-1, keepdims=True)
    return jnp.where(lo, s_lo, s_hi)


def _prep_a_fwd(z, g_cq, g_ckv, g_qf, g_kf, b_f):
    t = z.shape[0]
    bt = min(PREP_TILE, t)
    hw = HEADS * FOX_DIM

    def body(z_ref, gcq_ref, gckv_ref, gqf_ref, gkf_ref, bf_ref, cqn_ref, ckvn_ref, qf_ref, kf_ref, vf_ref, logf_ref):
        i = pl.program_id(0)
        cq = z_ref[:, ZA_CQ:ZA_CQ + Q_LORA]
        cqn_ref[...] = (cq * lax.rsqrt(jnp.mean(cq * cq, axis=-1, keepdims=True) + EPS) * gcq_ref[...]).astype(cqn_ref.dtype)
        ckv = z_ref[:, ZA_CKV:ZA_CKV + KV_LORA]
        ckvn_ref[...] = (ckv * lax.rsqrt(jnp.mean(ckv * ckv, axis=-1, keepdims=True) + EPS) * gckv_ref[...]).astype(ckvn_ref.dtype)
        lo = _iota((1, LANES), 1) < FOX_DIM
        for p in range(HEADS // 2):
            sl = slice(p * LANES, (p + 1) * LANES)
            xq = z_ref[:, ZA_FQ + p * LANES:ZA_FQ + (p + 1) * LANES]
            qf_ref[:, sl] = (xq * _pair_rms(xq, lo) * gqf_ref[...]).astype(qf_ref.dtype)
            xk = z_ref[:, ZA_FK + p * LANES:ZA_FK + (p + 1) * LANES]
            kf_ref[:, sl] = (xk * _pair_rms(xk, lo) * gkf_ref[...]).astype(kf_ref.dtype)
        vf_ref[...] = z_ref[:, ZA_FV:ZA_FV + hw].astype(vf_ref.dtype)
        xl = z_ref[:, ZA_LAST:ZA_LAST + LANES] + bf_ref[...]
        logf = jnp.minimum(xl, 0.0) - jnp.log(1.0 + jnp.exp(-jnp.abs(xl)))
        row = i * bt + _iota((bt, LANES), 0)
        lane = _iota((bt, LANES), 1)
        logf_ref[...] = jnp.where((lane < HEADS) & (row >= PAD), logf, 0.0)

    def vec(w):
        return pl.BlockSpec((1, w), lambda i: (0, 0))

    def row(w):
        return pl.BlockSpec((bt, w), lambda i: (i, 0))

    return pl.pallas_call(
        body, name="attn_prep_a_fwd", grid=(t // bt,),
        in_specs=[row(ZA_W), vec(Q_LORA), vec(KV_LORA), vec(LANES), vec(LANES), vec(LANES)],
        out_specs=[row(Q_LORA), row(KV_LORA), row(hw), row(hw), row(hw), row(LANES)],
        out_shape=[jax.ShapeDtypeStruct((t, Q_LORA), MXU_DTYPE), jax.ShapeDtypeStruct((t, KV_LORA), MXU_DTYPE),
                   jax.ShapeDtypeStruct((t, hw), MXU_DTYPE), jax.ShapeDtypeStruct((t, hw), MXU_DTYPE),
                   jax.ShapeDtypeStruct((t, hw), MXU_DTYPE), jax.ShapeDtypeStruct((t, LANES), F32)],
        compiler_params=_params("parallel"),
    )(z, g_cq, g_ckv, g_qf, g_kf, b_f)


def _prep_a_bwd(z, g_cq, g_ckv, g_qf, g_kf, b_f, dcqn, dckvn, dqf, dkf, dvf, dlogf, dkpe):
    t = z.shape[0]
    bt = min(PREP_TILE, t)
    hw = HEADS * FOX_DIM

    def norm_bwd(x, g, dy):
        r = lax.rsqrt(jnp.mean(x * x, axis=-1, keepdims=True) + EPS)
        xhat = x * r
        dxhat = dy * g
        dx = r * (dxhat - xhat * jnp.mean(dxhat * xhat, axis=-1, keepdims=True))
        return dx, jnp.sum(dy * xhat, axis=0, keepdims=True)

    def body(z_ref, gcq_ref, gckv_ref, gqf_ref, gkf_ref, bf_ref, dcqn_ref, dckvn_ref, dqf_ref, dkf_ref, dvf_ref,
             dlogf_ref, dkpe_ref, dz_ref, dgcq_ref, dgckv_ref, dgqf_ref, dgkf_ref, dbf_ref):
        i = pl.program_id(0)

        @pl.when(i == 0)
        def _():
            for r in (dgcq_ref, dgckv_ref, dgqf_ref, dgkf_ref, dbf_ref):
                r[...] = jnp.zeros_like(r)

        dx, dg = norm_bwd(z_ref[:, ZA_CQ:ZA_CQ + Q_LORA], gcq_ref[...], dcqn_ref[...])
        dz_ref[:, ZA_CQ:ZA_CQ + Q_LORA] = dx.astype(dz_ref.dtype)
        dgcq_ref[...] += dg
        dx, dg = norm_bwd(z_ref[:, ZA_CKV:ZA_CKV + KV_LORA], gckv_ref[...], dckvn_ref[...])
        dz_ref[:, ZA_CKV:ZA_CKV + KV_LORA] = dx.astype(dz_ref.dtype)
        dgckv_ref[...] += dg
        lo = _iota((1, LANES), 1) < FOX_DIM
        for base, g_ref, dy_ref, dg_ref in ((ZA_FQ, gqf_ref, dqf_ref, dgqf_ref), (ZA_FK, gkf_ref, dkf_ref, dgkf_ref)):
            for p in range(HEADS // 2):
                x = z_ref[:, base + p * LANES:base + (p + 1) * LANES]
                dy = dy_ref[:, p * LANES:(p + 1) * LANES]
                r = _pair_rms(x, lo)
                xhat = x * r
                dxhat = dy * g_ref[...]
                dx = r * (dxhat - xhat * _pair_sum(dxhat * xhat, lo) / FOX_DIM)
                dz_ref[:, base + p * LANES:base + (p + 1) * LANES] = dx.astype(dz_ref.dtype)
                dg_ref[...] += jnp.sum(dy * xhat, axis=0, keepdims=True)
        dz_ref[:, ZA_FV:ZA_FV + hw] = dvf_ref[...].astype(dz_ref.dtype)
        xl = z_ref[:, ZA_LAST:ZA_LAST + LANES] + bf_ref[...]
        row = i * bt + _iota((bt, LANES), 0)
        lane = _iota((bt, LANES), 1)
        dfl = jnp.where((lane < HEADS) & (row >= PAD), dlogf_ref[...] / (1.0 + jnp.exp(xl)), 0.0)
        dbf_ref[...] += jnp.sum(dfl, axis=0, keepdims=True)
        dz_ref[:, ZA_LAST:ZA_LAST + LANES] = (dfl + dkpe_ref[...]).astype(dz_ref.dtype)

    def vec(w):
        return pl.BlockSpec((1, w), lambda i: (0, 0))

    def row(w):
        return pl.BlockSpec((bt, w), lambda i: (i, 0))

    return pl.pallas_call(
        body, name="attn_prep_a_bwd", grid=(t // bt,),
        in_specs=[row(ZA_W), vec(Q_LORA), vec(KV_LORA), vec(LANES), vec(LANES), vec(LANES),
                  row(Q_LORA), row(KV_LORA), row(hw), row(hw), row(hw), row(LANES), row(LANES)],
        out_specs=[row(ZA_W), vec(Q_LORA), vec(KV_LORA), vec(LANES), vec(LANES), vec(LANES)],
        out_shape=[jax.ShapeDtypeStruct((t, ZA_W), MXU_DTYPE), jax.ShapeDtypeStruct((1, Q_LORA), F32),
                   jax.ShapeDtypeStruct((1, KV_LORA), F32), jax.ShapeDtypeStruct((1, LANES), F32),
                   jax.ShapeDtypeStruct((1, LANES), F32), jax.ShapeDtypeStruct((1, LANES), F32)],
        compiler_params=_params("arbitrary"),
    )(z, g_cq, g_ckv, g_qf, g_kf, b_f, dcqn, dckvn, dqf, dkf, dvf, dlogf, dkpe)


def _cumsum_rows(x, *, reverse, name):
    t = x.shape[0]
    nblk = t // LANES

    def body(x_ref, f_ref, ft_ref, carry_ref):
        r = _iota((LANES, LANES), 0)
        c = _iota((LANES, LANES), 1)
        tri = jnp.where((c >= r) if reverse else (c <= r), 1.0, 0.0).astype(F32)
        carry_ref[...] = jnp.zeros_like(carry_ref)

        def step(s, _):
            b = (nblk - 1 - s) if reverse else s
            start = pl.multiple_of(b * LANES, LANES)
            blk = x_ref[pl.ds(start, LANES), :]
            cs = jnp.dot(tri, blk, precision=lax.Precision.HIGHEST, preferred_element_type=F32) + carry_ref[0:1, :]
            f_ref[pl.ds(start, LANES), :] = cs
            ft_ref[:, pl.ds(start, LANES)] = cs.T
            carry_ref[0:1, :] = cs[0:1, :] if reverse else cs[LANES - 1:LANES, :]
            return 0

        lax.fori_loop(0, nblk, step, 0)

    return pl.pallas_call(
        body, name=name, grid=(1,),
        in_specs=[pl.BlockSpec((t, LANES), lambda i: (0, 0))],
        out_specs=[pl.BlockSpec((t, LANES), lambda i: (0, 0)), pl.BlockSpec((LANES, t), lambda i: (0, 0))],
        out_shape=[jax.ShapeDtypeStruct((t, LANES), F32), jax.ShapeDtypeStruct((LANES, t), F32)],
        scratch_shapes=[pltpu.VMEM((8, LANES), F32)],
        compiler_params=_params("arbitrary"),
    )(x)


def _rope_partner(x, lane):
    half = MLA_ROPE // 2
    swapped = jnp.where(lane < KPE_LANE + half, pltpu.roll(x, LANES - half, axis=1), pltpu.roll(x, half, axis=1))
    return jnp.where((lane >= KPE_LANE) & (lane < KPE_LANE + MLA_ROPE), swapped, 0.0)


def _rope_tables(t):
    pos = (jnp.arange(t, dtype=jnp.int32) - PAD).astype(F32)
    inv_freq = ROPE_BASE ** (-jnp.arange(0, MLA_ROPE, 2, dtype=F32) / MLA_ROPE)
    ang = pos[:, None] * inv_freq[None, :]
    cos, sin = jnp.cos(ang), jnp.sin(ang)
    ones = jnp.ones((t, KPE_LANE), F32)
    tail = jnp.zeros((t, LANES - KPE_LANE - MLA_ROPE), F32)
    c_tab = jnp.concatenate([ones, cos, cos, tail + 1.0], axis=1)
    s_tab = jnp.concatenate([ones * 0.0, -sin, sin, tail], axis=1)
    return c_tab, s_tab


def _prep_b_fwd(q_raw, kv_raw, z, c_tab, s_tab, g_q, g_k):
    t = q_raw.shape[0]
    bt = min(PREP_TILE, t)
    qw = HEADS * LANES
    vw = HEADS * HEAD_V

    def body(q_ref, kv_ref, zl_ref, c_ref, s_ref, gq_ref, gk_ref, qn_ref, kn_ref, v_ref):
        lane = _iota((1, LANES), 1)
        kpe = jnp.where((lane >= KPE_LANE) & (lane < KPE_LANE + MLA_ROPE), zl_ref[...], 0.0)
        cv, sv = c_ref[...], s_ref[...]
        for h in range(HEADS):
            sl = slice(h * LANES, (h + 1) * LANES)
            for x, g_ref, o_ref in ((q_ref[:, sl], gq_ref, qn_ref), (kv_ref[:, sl] + kpe, gk_ref, kn_ref)):
                r = lax.rsqrt(jnp.sum(x * x, axis=-1, keepdims=True) / MLA_QK + EPS)
                xn = x * r * g_ref[...]
                o_ref[:, sl] = (xn * cv + _rope_partner(xn, lane) * sv).astype(o_ref.dtype)
        v_ref[...] = kv_ref[:, qw:qw + vw].astype(v_ref.dtype)

    def row(w):
        return pl.BlockSpec((bt, w), lambda i: (i, 0))

    vec = pl.BlockSpec((1, LANES), lambda i: (0, 0))
    return pl.pallas_call(
        body, name="attn_prep_b_fwd", grid=(t // bt,),
        in_specs=[row(qw), row(qw + vw), pl.BlockSpec((bt, LANES), lambda i: (i, ZA_LAST // LANES)),
                  row(LANES), row(LANES), vec, vec],
        out_specs=[row(qw), row(qw), row(vw)],
        out_shape=[jax.ShapeDtypeStruct((t, qw), MXU_DTYPE), jax.ShapeDtypeStruct((t, qw), MXU_DTYPE),
                   jax.ShapeDtypeStruct((t, vw), MXU_DTYPE)],
        compiler_params=_params("parallel"),
    )(q_raw, kv_raw, z, c_tab, s_tab, g_q, g_k)


def _prep_b_bwd(q_raw, kv_raw, z, c_tab, s_tab, g_q, g_k, dqn, dkn, dv):
    t = q_raw.shape[0]
    bt = min(PREP_TILE, t)
    qw = HEADS * LANES
    vw = HEADS * HEAD_V

    def body(q_ref, kv_ref, zl_ref, c_ref, s_ref, gq_ref, gk_ref, dqn_ref, dkn_ref, dv_ref,
             dq_ref, dkv_ref, dkpe_ref, dgq_ref, dgk_ref):
        @pl.when(pl.program_id(0) == 0)
        def _():
            dgq_ref[...] = jnp.zeros_like(dgq_ref)
            dgk_ref[...] = jnp.zeros_like(dgk_ref)

        lane = _iota((1, LANES), 1)
        rope_lanes = (lane >= KPE_LANE) & (lane < KPE_LANE + MLA_ROPE)
        kpe = jnp.where(rope_lanes, zl_ref[...], 0.0)
        cv, sv = c_ref[...], s_ref[...]
        dkpe = jnp.zeros((bt, LANES), F32)
        for h in range(HEADS):
            sl = slice(h * LANES, (h + 1) * LANES)
            for is_k, x, g_ref, dout, dg_ref in ((False, q_ref[:, sl], gq_ref, dqn_ref[:, sl], dgq_ref),
                                                  (True, kv_ref[:, sl] + kpe, gk_ref, dkn_ref[:, sl], dgk_ref)):
                r = lax.rsqrt(jnp.sum(x * x, axis=-1, keepdims=True) / MLA_QK + EPS)
                xhat = x * r
                dxn = dout * cv + _rope_partner(dout * sv, lane)
                dg_ref[...] += jnp.sum(dxn * xhat, axis=0, keepdims=True)
                dxhat = dxn * g_ref[...]
                dx = r * (dxhat - xhat * (jnp.sum(dxhat * xhat, axis=-1, keepdims=True) / MLA_QK))
                if is_k:
                    dkv_ref[:, sl] = jnp.where(lane < KPE_LANE, dx, 0.0).astype(dkv_ref.dtype)
                    dkpe = dkpe + jnp.where(rope_lanes, dx, 0.0)
                else:
                    dq_ref[:, sl] = dx.astype(dq_ref.dtype)
        dkv_ref[:, qw:qw + vw] = dv_ref[...].astype(dkv_ref.dtype)
        dkpe_ref[...] = dkpe

    def row(w):
        return pl.BlockSpec((bt, w), lambda i: (i, 0))

    vec = pl.BlockSpec((1, LANES), lambda i: (0, 0))
    return pl.pallas_call(
        body, name="attn_prep_b_bwd", grid=(t // bt,),
        in_specs=[row(qw), row(qw + vw), pl.BlockSpec((bt, LANES), lambda i: (i, ZA_LAST // LANES)),
                  row(LANES), row(LANES), vec, vec, row(qw), row(qw), row(vw)],
        out_specs=[row(qw), row(qw + vw), row(LANES), vec, vec],
        out_shape=[jax.ShapeDtypeStruct((t, qw), MXU_DTYPE), jax.ShapeDtypeStruct((t, qw + vw), MXU_DTYPE),
                   jax.ShapeDtypeStruct((t, LANES), F32), jax.ShapeDtypeStruct((1, LANES), F32),
                   jax.ShapeDtypeStruct((1, LANES), F32)],
        compiler_params=_params("arbitrary"),
    )(q_raw, kv_raw, z, c_tab, s_tab, g_q, g_k, dqn, dkn, dv)


NT_DIMS = (((1,), (1,)), ((), ()))
TN_DIMS = (((0,), (0,)), ((), ()))


def _head_qk(q_ref, k_ref, e, mla, lo):
    if mla:
        return q_ref[:, e * LANES:(e + 1) * LANES], k_ref[:, e * LANES:(e + 1) * LANES]
    q = q_ref[...]
    return jnp.where(lo if e == 0 else jnp.logical_not(lo), q, jnp.zeros_like(q)), k_ref[...]


def _scores(q_ref, k_ref, fq_ref, fk_ref, e, pair, mla, scale, lo, mask):
    qe, ke = _head_qk(q_ref, k_ref, e, mla, lo)
    s = lax.dot_general(qe, ke, NT_DIMS, preferred_element_type=F32) * scale
    if not mla:
        head = 2 * pair + e
        fq = fq_ref[...]
        fqc = jnp.sum(jnp.where(_iota(fq.shape, 1) == head, fq, 0.0), axis=1, keepdims=True)
        fk = fk_ref[...]
        fkr = jnp.sum(jnp.where(_iota(fk.shape, 0) == head, fk, 0.0), axis=0, keepdims=True)
        s = s + fqc - fkr
    return (s if mask is None else jnp.where(mask, s, NEG)), qe, ke


def _causal_mask(qb, kb, blk):
    rows = qb * blk + _iota((blk, blk), 0)
    cols = kb * blk + _iota((blk, blk), 1)
    return (cols <= rows) & (cols >= PAD)


def _masked_and_plain(active, qb, kb, blk, step):
    assert blk >= PAD
    needs_mask = (qb == kb) | (kb == 0)

    @pl.when(active & needs_mask)
    def _():
        step(_causal_mask(qb, kb, blk))

    @pl.when(active & jnp.logical_not(needs_mask))
    def _():
        step(None)


def _attn_specs(mla, blk, q_map, k_map):
    w = 2 * LANES if mla else LANES
    q_spec = pl.BlockSpec((blk, w), lambda j, a, b: (q_map(a, b), j))
    k_spec = pl.BlockSpec((blk, w), lambda j, a, b: (k_map(a, b), j))
    qv_spec = pl.BlockSpec((blk, LANES), lambda j, a, b: (q_map(a, b), j))
    kv_spec = pl.BlockSpec((blk, LANES), lambda j, a, b: (k_map(a, b), j))
    fq_spec = pl.BlockSpec((blk, LANES), lambda j, a, b: (q_map(a, b), 0))
    fk_spec = pl.BlockSpec((8, blk), lambda j, a, b: (0, k_map(a, b)))
    return q_spec, k_spec, qv_spec, kv_spec, fq_spec, fk_spec


def _flash_fwd(q, k, v, f, f_t, *, mla, scale, name):
    t = q.shape[0]
    blk = min(ATTN_BLOCK, t)
    nb = t // blk
    pairs = HEADS // 2
    q_spec, k_spec, qv_spec, kv_spec, fq_spec, fk_spec = _attn_specs(
        mla, blk, lambda i, kk: i, lambda i, kk: jnp.minimum(kk, i))

    def body(*refs):
        if mla:
            q_ref, k_ref, v_ref, o_ref, lse_ref, m_s, l_s, acc_s = refs
            fq_ref = fk_ref = None
        else:
            q_ref, k_ref, v_ref, fq_ref, fk_ref, o_ref, lse_ref, m_s, l_s, acc_s = refs
        j, i, kk = pl.program_id(0), pl.program_id(1), pl.program_id(2)
        lo = _iota((1, LANES), 1) < HEAD_V

        @pl.when(kk == 0)
        def _():
            m_s[...] = jnp.full_like(m_s, NEG)
            l_s[...] = jnp.zeros_like(l_s)
            acc_s[...] = jnp.zeros_like(acc_s)

        def step(mask):
            vv = v_ref[...]
            for e in range(2):
                s, _, _ = _scores(q_ref, k_ref, fq_ref, fk_ref, e, j, mla, scale, lo, mask)
                m_prev = m_s[e]
                m_new = jnp.maximum(m_prev, jnp.max(s, axis=1, keepdims=True))
                alpha = jnp.exp(m_prev - m_new)
                p = jnp.exp(s - m_new)
                if mask is not None:
                    p = jnp.where(mask, p, 0.0)
                l_s[e] = alpha * l_s[e] + jnp.sum(p, axis=1, keepdims=True)
                acc_s[e] = alpha * acc_s[e] + jnp.dot(p.astype(MXU_DTYPE), vv, preferred_element_type=F32)
                m_s[e] = m_new

        _masked_and_plain(kk <= i, i, kk, blk, step)

        @pl.when(kk == nb - 1)
        def _():
            valid = (i * blk + _iota((blk, 1), 0)) >= PAD
            outs, lses = [], []
            for e in range(2):
                l = l_s[e]
                outs.append(acc_s[e] * jnp.where(l > 0.0, 1.0 / jnp.where(l > 0.0, l, 1.0), 0.0))
                lses.append(m_s[e] + jnp.log(jnp.where(l > 0.0, l, 1.0)))
            o = jnp.where(lo, outs[0], outs[1])
            o_ref[...] = jnp.where(valid, o, 0.0).astype(o_ref.dtype)
            lane = _iota((1, LANES), 1)
            lse_ref[...] = jnp.where(lane == 0, lses[0], jnp.where(lane == 1, lses[1], 0.0))

    in_specs = [q_spec, k_spec, kv_spec] + ([] if mla else [fq_spec, fk_spec])
    args = (q, k, v) + (() if mla else (f, f_t))
    hv = HEADS * HEAD_V
    return pl.pallas_call(
        body, name=name, grid=(pairs, nb, nb),
        in_specs=in_specs, out_specs=[qv_spec, qv_spec],
        out_shape=[jax.ShapeDtypeStruct((t, hv), F32), jax.ShapeDtypeStruct((t, hv), F32)],
        scratch_shapes=[pltpu.VMEM((2, blk, 1), F32), pltpu.VMEM((2, blk, 1), F32), pltpu.VMEM((2, blk, LANES), F32)],
        compiler_params=_params("parallel", "parallel", "arbitrary"),
    )(*args)


def _bwd_tile(q_ref, k_ref, v_ref, o_ref, do_ref, lse_ref, fq_ref, fk_ref, e, pair, mla, scale, lo, mask):
    s, qe, ke = _scores(q_ref, k_ref, fq_ref, fk_ref, e, pair, mla, scale, lo, mask)
    p = jnp.exp(s - lse_ref[:, e:e + 1])
    if mask is not None:
        p = jnp.where(mask, p, 0.0)
    do = do_ref[...]
    doe = jnp.where(lo if e == 0 else jnp.logical_not(lo), do, jnp.zeros_like(do))
    dp = lax.dot_general(doe, v_ref[...], NT_DIMS, preferred_element_type=F32)
    delta = jnp.sum(doe.astype(F32) * o_ref[...].astype(F32), axis=1, keepdims=True)
    return p, p * (dp - delta), qe, ke


def _flash_bwd_dq(q, k, v, o, do, lse, f, f_t, *, mla, scale, col0, name):
    t = q.shape[0]
    blk = min(ATTN_BLOCK, t)
    nb = t // blk
    pairs = HEADS // 2
    w = 2 * LANES if mla else LANES
    q_spec, k_spec, qv_spec, kv_spec, fq_spec, fk_spec = _attn_specs(
        mla, blk, lambda i, kk: i, lambda i, kk: jnp.minimum(kk, i))
    od_spec = pl.BlockSpec((blk, LANES), lambda j, i, kk: (i, col0 + j))

    def body(*refs):
        if mla:
            q_ref, k_ref, v_ref, o_ref, do_ref, lse_ref, dq_ref, dq_s = refs
            fq_ref = fk_ref = rs_ref = rs_s = None
        else:
            q_ref, k_ref, v_ref, o_ref, do_ref, lse_ref, fq_ref, fk_ref, dq_ref, rs_ref, dq_s, rs_s = refs
        j, i, kk = pl.program_id(0), pl.program_id(1), pl.program_id(2)
        lo = _iota((1, LANES), 1) < HEAD_V

        @pl.when(kk == 0)
        def _():
            dq_s[...] = jnp.zeros_like(dq_s)
            if not mla:
                rs_s[...] = jnp.zeros_like(rs_s)

        def step(mask):
            for e in range(2):
                _, ds, _, ke = _bwd_tile(q_ref, k_ref, v_ref, o_ref, do_ref, lse_ref, fq_ref, fk_ref, e, j, mla, scale,
                                         lo, mask)
                dq_s[e] += jnp.dot(ds.astype(MXU_DTYPE), ke, preferred_element_type=F32)
                if not mla:
                    rs_s[e] += jnp.sum(ds, axis=1, keepdims=True)

        _masked_and_plain(kk <= i, i, kk, blk, step)

        @pl.when(kk == nb - 1)
        def _():
            if mla:
                dq_ref[:, 0:LANES] = dq_s[0] * scale
                dq_ref[:, LANES:2 * LANES] = dq_s[1] * scale
            else:
                dq_ref[...] = jnp.where(lo, dq_s[0], dq_s[1]) * scale
                lane = _iota((1, LANES), 1)
                rs_ref[...] = jnp.where(lane == 0, rs_s[0], jnp.where(lane == 1, rs_s[1], 0.0))

    in_specs = [q_spec, k_spec, kv_spec, od_spec, od_spec, qv_spec] + ([] if mla else [fq_spec, fk_spec])
    args = (q, k, v, o, do, lse) + (() if mla else (f, f_t))
    out_specs = [q_spec] + ([] if mla else [qv_spec])
    out_shape = [jax.ShapeDtypeStruct((t, pairs * w), F32)]
    scratch = [pltpu.VMEM((2, blk, LANES), F32)]
    if not mla:
        out_shape.append(jax.ShapeDtypeStruct((t, pairs * LANES), F32))
        scratch.append(pltpu.VMEM((2, blk, 1), F32))
    outs = pl.pallas_call(
        body, name=name, grid=(pairs, nb, nb),
        in_specs=in_specs, out_specs=out_specs, out_shape=out_shape, scratch_shapes=scratch,
        compiler_params=_params("parallel", "parallel", "arbitrary"),
    )(*args)
    return outs[0] if mla else outs


def _flash_bwd_dkv(q, k, v, o, do, lse, f, f_t, *, mla, scale, col0, name):
    t = q.shape[0]
    blk = min(ATTN_BLOCK, t)
    nb = t // blk
    pairs = HEADS // 2
    w = 2 * LANES if mla else LANES
    q_spec, k_spec, qv_spec, kv_spec, fq_spec, fk_spec = _attn_specs(
        mla, blk, lambda a, b: jnp.maximum(a, b), lambda a, b: a)
    od_spec = pl.BlockSpec((blk, LANES), lambda j, a, b: (jnp.maximum(a, b), col0 + j))
    cs_spec = pl.BlockSpec((8, blk), lambda j, a, b: (j, a))

    def body(*refs):
        if mla:
            q_ref, k_ref, v_ref, o_ref, do_ref, lse_ref, dk_ref, dv_ref, dk_s, dv_s = refs
            fq_ref = fk_ref = cs_ref = cs_s = None
        else:
            q_ref, k_ref, v_ref, o_ref, do_ref, lse_ref, fq_ref, fk_ref, dk_ref, dv_ref, cs_ref, dk_s, dv_s, cs_s = refs
        j, kb, qb = pl.program_id(0), pl.program_id(1), pl.program_id(2)
        lo = _iota((1, LANES), 1) < HEAD_V

        @pl.when(qb == 0)
        def _():
            dk_s[...] = jnp.zeros_like(dk_s)
            dv_s[...] = jnp.zeros_like(dv_s)
            if not mla:
                cs_s[...] = jnp.zeros_like(cs_s)

        def step(mask):
            do = do_ref[...]
            for e in range(2):
                p, ds, _, _ = _bwd_tile(q_ref, k_ref, v_ref, o_ref, do_ref, lse_ref, fq_ref, fk_ref, e, j, mla, scale,
                                        lo, mask)
                dv_s[e] += lax.dot_general(p.astype(MXU_DTYPE), do, TN_DIMS, preferred_element_type=F32)
                q_src = q_ref[:, e * LANES:(e + 1) * LANES] if mla else q_ref[...]
                dk_s[e] += lax.dot_general(ds.astype(MXU_DTYPE), q_src, TN_DIMS, preferred_element_type=F32)
                if not mla:
                    cs_s[e] += jnp.sum(ds, axis=0, keepdims=True)

        _masked_and_plain(qb >= kb, qb, kb, blk, step)

        @pl.when(qb == nb - 1)
        def _():
            dv_ref[...] = jnp.where(lo, dv_s[0], dv_s[1])
            if mla:
                dk_ref[:, 0:LANES] = dk_s[0] * scale
                dk_ref[:, LANES:2 * LANES] = dk_s[1] * scale
            else:
                dk_ref[...] = jnp.where(lo, dk_s[0], dk_s[1]) * scale
                sub = _iota((8, 1), 0)
                cs_ref[...] = jnp.where(sub == 0, cs_s[0], jnp.where(sub == 1, cs_s[1], 0.0))

    in_specs = [q_spec, k_spec, kv_spec, od_spec, od_spec, qv_spec] + ([] if mla else [fq_spec, fk_spec])
    args = (q, k, v, o, do, lse) + (() if mla else (f, f_t))
    out_specs = [k_spec, kv_spec] + ([] if mla else [cs_spec])
    out_shape = [jax.ShapeDtypeStruct((t, pairs * w), F32), jax.ShapeDtypeStruct((t, HEADS * HEAD_V), F32)]
    scratch = [pltpu.VMEM((2, blk, LANES), F32), pltpu.VMEM((2, blk, LANES), F32)]
    if not mla:
        out_shape.append(jax.ShapeDtypeStruct((pairs * 8, t), F32))
        scratch.append(pltpu.VMEM((2, 1, blk), F32))
    return pl.pallas_call(
        body, name=name, grid=(pairs, nb, nb),
        in_specs=in_specs, out_specs=out_specs, out_shape=out_shape, scratch_shapes=scratch,
        compiler_params=_params("parallel", "parallel", "arbitrary"),
    )(*args)


def _shift_down(x, halo, n):
    rows = x.shape[0]
    r = _iota((rows, 1), 0)
    out = pltpu.roll(x, n, axis=0)
    for s in range(n):
        out = jnp.where(r == s, halo[8 - n + s:8 - n + s + 1, :], out)
    return out


def _shift_up(x, halo, n):
    rows = x.shape[0]
    r = _iota((rows, 1), 0)
    out = pltpu.roll(x, rows - n, axis=0)
    for s in range(n):
        out = jnp.where(r == rows - n + s, halo[s:s + 1, :], out)
    return out


def _conv_specs(bt, nblk):
    d = D_MODEL
    per8 = bt // 8
    z_spec = pl.BlockSpec((bt, 3 * d), lambda i: (i, 0))
    prev_spec = pl.BlockSpec((8, 3 * d), lambda i: (jnp.maximum(i * per8 - 1, 0), 0))
    next_z = pl.BlockSpec((8, 3 * d), lambda i: (jnp.minimum((i + 1) * per8, nblk * per8 - 1), 0))
    next_d = pl.BlockSpec((8, d), lambda i: (jnp.minimum((i + 1) * per8, nblk * per8 - 1), 0))
    w_spec = pl.BlockSpec((8, d), lambda i: (0, 0))
    row_spec = pl.BlockSpec((bt, d), lambda i: (i, 0))
    return z_spec, prev_spec, next_z, next_d, w_spec, row_spec


def _conv_taps(z_ref, prev_ref, i):
    d = D_MODEL
    g = z_ref[:, d:2 * d] * z_ref[:, 2 * d:3 * d]
    gh = jnp.where(i > 0, prev_ref[:, d:2 * d] * prev_ref[:, 2 * d:3 * d], 0.0)
    return g, _shift_down(g, gh, 1), _shift_down(g, gh, 2)


def _conv_fwd(z, conv_w8):
    t = z.shape[0]
    bt = min(PREP_TILE, t)
    nblk = t // bt
    d = D_MODEL
    z_spec, prev_spec, _, _, w_spec, row_spec = _conv_specs(bt, nblk)

    def body(z_ref, prev_ref, w_ref, v_ref):
        g, g1, g2 = _conv_taps(z_ref, prev_ref, pl.program_id(0))
        y = w_ref[0:1, :] * g2 + w_ref[1:2, :] * g1 + w_ref[2:3, :] * g
        v_ref[...] = (z_ref[:, 0:d] * y).astype(v_ref.dtype)

    return pl.pallas_call(
        body, name="conv_fwd", grid=(nblk,),
        in_specs=[z_spec, prev_spec, w_spec], out_specs=row_spec,
        out_shape=jax.ShapeDtypeStruct((t, d), MXU_DTYPE),
        compiler_params=_params("parallel"),
    )(z, z, conv_w8)


def _conv_bwd(z, conv_w8, dv):
    t = z.shape[0]
    bt = min(PREP_TILE, t)
    nblk = t // bt
    d = D_MODEL
    z_spec, prev_spec, next_z, next_d, w_spec, row_spec = _conv_specs(bt, nblk)

    def body(z_ref, prev_ref, nz_ref, dv_ref, ndv_ref, w_ref, dz_ref, dw_ref):
        i = pl.program_id(0)

        @pl.when(i == 0)
        def _():
            dw_ref[...] = jnp.zeros_like(dw_ref)

        g, g1, g2 = _conv_taps(z_ref, prev_ref, i)
        w0, w1, w2 = w_ref[0:1, :], w_ref[1:2, :], w_ref[2:3, :]
        y = w0 * g2 + w1 * g1 + w2 * g
        dvv = dv_ref[...].astype(F32)
        gate_b = z_ref[:, 0:d]
        dy = dvv * gate_b
        dyn = jnp.where(i < nblk - 1, ndv_ref[...].astype(F32) * nz_ref[:, 0:d], 0.0)
        dg = w2 * dy + w1 * _shift_up(dy, dyn, 1) + w0 * _shift_up(dy, dyn, 2)
        dz_ref[:, 0:d] = (dvv * y).astype(dz_ref.dtype)
        dz_ref[:, d:2 * d] = (dg * z_ref[:, 2 * d:3 * d]).astype(dz_ref.dtype)
        dz_ref[:, 2 * d:3 * d] = (dg * z_ref[:, d:2 * d]).astype(dz_ref.dtype)
        sub = _iota((8, 1), 0)
        s0 = jnp.sum(dy * g2, axis=0, keepdims=True)
        s1 = jnp.sum(dy * g1, axis=0, keepdims=True)
        s2 = jnp.sum(dy * g, axis=0, keepdims=True)
        dw_ref[...] += jnp.where(sub == 0, s0, jnp.where(sub == 1, s1, jnp.where(sub == 2, s2, 0.0)))

    return pl.pallas_call(
        body, name="conv_bwd", grid=(nblk,),
        in_specs=[z_spec, prev_spec, next_z, row_spec, next_d, w_spec], out_specs=[z_spec, w_spec],
        out_shape=[jax.ShapeDtypeStruct((t, 3 * d), MXU_DTYPE), jax.ShapeDtypeStruct((8, d), F32)],
        compiler_params=_params("arbitrary"),
    )(z, z, z, dv, dv, conv_w8)


def _loss_head(h, target):
    t, d = h.shape
    bt = LOSS_TILE
    assert LANES % bt == 0 or bt == LANES
    off = LANES // bt

    def body(h_ref, y_ref, dh_ref, acc_ref):
        i = pl.program_id(0)

        @pl.when(i == 0)
        def _():
            acc_ref[...] = jnp.zeros_like(acc_ref)

        @pl.when(i < off)
        def _():
            dh_ref[...] = jnp.zeros_like(dh_ref)

        @pl.when(i >= off)
        def _():
            err = h_ref[...] - y_ref[...]
            dh_ref[...] = err / d
            acc_ref[...] += jnp.sum(err * err)

    dh, acc = pl.pallas_call(
        body, name="loss_head", grid=(t // bt,),
        in_specs=[pl.BlockSpec((bt, d), lambda i: (i, 0)), pl.BlockSpec((bt, d), lambda i: (jnp.maximum(i - off, 0), 0))],
        out_specs=[pl.BlockSpec((bt, d), lambda i: (i, 0)), pl.BlockSpec((8, LANES), lambda i: (0, 0))],
        out_shape=[jax.ShapeDtypeStruct((t, d), F32), jax.ShapeDtypeStruct((8, LANES), F32)],
        compiler_params=_params("arbitrary"),
    )(h, target)
    return dh, acc[0, 0] * (0.5 / d)


def _common_tile(rows, row_off, cap=512, align=8):
    for b in range(min(cap, rows) // align * align, 0, -align):
        if rows % b == 0 and row_off % b == 0:
            return b
    raise ValueError((rows, row_off))


def _round_up(n, m):
    return -(-n // m) * m


def _adamw(w, m, v, g_buf, row_off, col_off):
    rows, width = w.shape
    wpad = _round_up(width, LANES)
    assert col_off % wpad == 0
    bt = _common_tile(rows, row_off)

    def body(w_ref, m_ref, v_ref, g_ref, go_ref, d_ref, nm_ref, nv_ref):
        gv = g_ref[...]
        if wpad != width:
            gv = gv[:, :width]
        m_new = ADAM_B1 * m_ref[...] + (1.0 - ADAM_B1) * gv
        v_new = ADAM_B2 * v_ref[...] + (1.0 - ADAM_B2) * jnp.square(gv)
        m_hat = m_new / (1.0 - ADAM_B1 ** ADAM_STEP)
        v_hat = v_new / (1.0 - ADAM_B2 ** ADAM_STEP)
        go_ref[...] = gv
        d_ref[...] = -ADAM_LR * (m_hat / (jnp.sqrt(v_hat) + ADAM_EPS) + ADAM_WD * w_ref[...])
        nm_ref[...] = m_new
        nv_ref[...] = v_new

    spec = pl.BlockSpec((bt, width), lambda i: (i, 0))
    g_spec = pl.BlockSpec((bt, wpad), lambda i: (row_off // bt + i, col_off // wpad))
    return pl.pallas_call(
        body, name="adamw", grid=(rows // bt,),
        in_specs=[spec] * 3 + [g_spec], out_specs=[spec] * 4,
        out_shape=[jax.ShapeDtypeStruct((rows, width), F32)] * 4,
        compiler_params=_params("parallel"),
    )(w, m, v, g_buf)


def _add2(a, b, *, name):
    rows, width = a.shape
    bt = next(x for x in range(min(rows, 640), 0, -8) if rows % x == 0)

    def body(a_ref, b_ref, o_ref):
        o_ref[...] = a_ref[...] + b_ref[...]

    spec = pl.BlockSpec((bt, width), lambda i: (i, 0))
    return pl.pallas_call(
        body, name=name, grid=(rows // bt,), in_specs=[spec, spec], out_specs=spec,
        out_shape=jax.ShapeDtypeStruct((rows, width), F32), compiler_params=_params("parallel"),
    )(a, b)


def _sum4(parts, *, name):
    _, rows, width = parts.shape
    bt = next(x for x in range(min(rows, 640), 0, -8) if rows % x == 0)

    def body(p_ref, o_ref):
        o_ref[...] = ((p_ref[0] + p_ref[1]) + p_ref[2]) + p_ref[3]

    return pl.pallas_call(
        body, name=name, grid=(rows // bt,),
        in_specs=[pl.BlockSpec((4, bt, width), lambda i: (0, i, 0))],
        out_specs=pl.BlockSpec((bt, width), lambda i: (i, 0)),
        out_shape=jax.ShapeDtypeStruct((rows, width), F32), compiler_params=_params("parallel"),
    )(parts)


ANY = pl.BlockSpec(memory_space=pl.ANY)
CHIP_FLIPS = ((1, 0), (0, 1), (1, 1))


def _place():
    return lax.axis_index("x"), lax.axis_index("y"), lax.axis_index("c")


def _flip(v, f):
    return 1 - v if f else v


def _allgather_chips(pack):
    rows, width = pack.shape
    half = rows // 2

    def body(pack_ref, out_ref, send_sems, recv_sems, local_sem):
        x, y, c = _place()
        me = 2 * x + y
        sibling = (x, y, 1 - c)
        mine = pltpu.make_async_copy(pack_ref, out_ref.at[me], local_sem)
        mine.start()
        my_rows = pl.ds(pl.multiple_of(c * half, 8), half)
        sib_rows = pl.ds(pl.multiple_of((1 - c) * half, 8), half)
        first, passed = [], []
        for n, (fx, fy) in enumerate(CHIP_FLIPS):
            px, py = _flip(x, fx), _flip(y, fy)
            peer = 2 * px + py
            first.append(pltpu.make_async_remote_copy(
                src_ref=pack_ref.at[my_rows], dst_ref=out_ref.at[me, my_rows],
                send_sem=send_sems.at[n], recv_sem=recv_sems.at[n], device_id=(px, py, c), device_id_type=MESH))
            passed.append(pltpu.make_async_remote_copy(
                src_ref=out_ref.at[peer, my_rows], dst_ref=out_ref.at[peer, my_rows],
                send_sem=send_sems.at[3 + n], recv_sem=recv_sems.at[3 + n], device_id=sibling, device_id_type=MESH))
        for cp in first:
            cp.start()
        for n, (fx, fy) in enumerate(CHIP_FLIPS):
            peer = 2 * _flip(x, fx) + _flip(y, fy)
            pltpu.make_async_remote_copy(
                src_ref=pack_ref.at[my_rows], dst_ref=out_ref.at[peer, my_rows],
                send_sem=send_sems.at[n], recv_sem=recv_sems.at[n], device_id=sibling, device_id_type=MESH).wait_recv()
            passed[n].start()
        for n, (fx, fy) in enumerate(CHIP_FLIPS):
            peer = 2 * _flip(x, fx) + _flip(y, fy)
            pltpu.make_async_remote_copy(
                src_ref=pack_ref.at[sib_rows], dst_ref=out_ref.at[peer, sib_rows],
                send_sem=send_sems.at[3 + n], recv_sem=recv_sems.at[3 + n], device_id=sibling,
                device_id_type=MESH).wait_recv()
        for cp in first + passed:
            cp.wait_send()
        mine.wait()

    return pl.pallas_call(
        body, name="allgather_weights",
        in_specs=[ANY], out_specs=ANY,
        out_shape=jax.ShapeDtypeStruct((4, rows, width), pack.dtype),
        scratch_shapes=[pltpu.SemaphoreType.DMA((6,)), pltpu.SemaphoreType.DMA((6,)), pltpu.SemaphoreType.DMA],
    )(pack)


def _swap_halves(g):
    _, rows, width = g.shape
    half = rows // 2

    def body(g_ref, got_ref, send_sem, recv_sem):
        x, y, c = _place()
        away = pl.ds(pl.multiple_of((1 - c) * half, 8), half)
        cp = pltpu.make_async_remote_copy(
            src_ref=g_ref.at[:, away], dst_ref=got_ref, send_sem=send_sem, recv_sem=recv_sem,
            device_id=(x, y, 1 - c), device_id_type=MESH)
        cp.start()
        cp.wait()

    return pl.pallas_call(
        body, name="grad_swap_halves",
        in_specs=[ANY], out_specs=ANY,
        out_shape=jax.ShapeDtypeStruct((4, half, width), g.dtype),
        scratch_shapes=[pltpu.SemaphoreType.DMA, pltpu.SemaphoreType.DMA],
    )(g)


def _scatter_chips(s):
    _, rows, width = s.shape

    def body(s_ref, out_ref, send_sems, recv_sems, local_sem):
        x, y, c = _place()
        me = 2 * x + y
        mine = pltpu.make_async_copy(s_ref.at[me], out_ref.at[me], local_sem)
        mine.start()
        copies = []
        for n, (fx, fy) in enumerate(CHIP_FLIPS):
            px, py = _flip(x, fx), _flip(y, fy)
            copies.append(pltpu.make_async_remote_copy(
                src_ref=s_ref.at[2 * px + py], dst_ref=out_ref.at[me],
                send_sem=send_sems.at[n], recv_sem=recv_sems.at[n], device_id=(px, py, c), device_id_type=MESH))
        for cp in copies:
            cp.start()
        for n, (fx, fy) in enumerate(CHIP_FLIPS):
            peer = 2 * _flip(x, fx) + _flip(y, fy)
            pltpu.make_async_remote_copy(
                src_ref=s_ref.at[me], dst_ref=out_ref.at[peer],
                send_sem=send_sems.at[n], recv_sem=recv_sems.at[n], device_id=(x, y, c), device_id_type=MESH).wait_recv()
        for cp in copies:
            cp.wait_send()
        mine.wait()

    return pl.pallas_call(
        body, name="grad_scatter_chips",
        in_specs=[ANY], out_specs=ANY,
        out_shape=jax.ShapeDtypeStruct((4, rows, width), s.dtype),
        scratch_shapes=[pltpu.SemaphoreType.DMA((3,)), pltpu.SemaphoreType.DMA((3,)), pltpu.SemaphoreType.DMA],
    )(s)


def _join_halves(tot):
    rows, width = tot.shape

    def body(t_ref, out_ref, send_sem, recv_sem, local_sem):
        x, y, c = _place()
        mine = pltpu.make_async_copy(t_ref, out_ref.at[c], local_sem)
        mine.start()
        cp = pltpu.make_async_remote_copy(
            src_ref=t_ref, dst_ref=out_ref.at[c], send_sem=send_sem, recv_sem=recv_sem,
            device_id=(x, y, 1 - c), device_id_type=MESH)
        cp.start()
        pltpu.make_async_remote_copy(
            src_ref=t_ref, dst_ref=out_ref.at[1 - c], send_sem=send_sem, recv_sem=recv_sem,
            device_id=(x, y, 1 - c), device_id_type=MESH).wait_recv()
        cp.wait_send()
        mine.wait()

    return pl.pallas_call(
        body, name="grad_join_halves",
        in_specs=[ANY], out_specs=ANY,
        out_shape=jax.ShapeDtypeStruct((2, rows, width), tot.dtype),
        scratch_shapes=[pltpu.SemaphoreType.DMA, pltpu.SemaphoreType.DMA, pltpu.SemaphoreType.DMA],
    )(tot)


PACK_W = 1024
REPLICATED = ("g_mix", "g_mlp", "g_cq", "g_ckv", "g_q_mla", "g_k_mla", "g_q_fox", "g_k_fox", "b_forget")
WEIGHT_ORDER = ("meta_tokens", "g_mix", "g_mlp", "w_in_attn", "g_cq", "w_uq", "g_ckv", "w_ukv", "g_q_mla", "g_k_mla",
                "g_q_fox", "g_k_fox", "b_forget", "w_out_attn", "w_in_conv", "conv_w", "w_out_conv", "w_mlp_up",
                "w_mlp_down")
N_EVEN = 2
N_ODD = 2
SHARD_IN = ATTN_IN // 4
SHARD_MIX = D_MODEL // 4
SHARD_UQ = HEADS * MLA_QK // 4
SHARD_UKV = HEADS * (MLA_NOPE + HEAD_V) // 4
SHARD_CONV = 3 * D_MODEL // 4
SIDE_W = 256
PK_UP = (0, 0)
PK_DOWN = (4096, 0)
PK_CONV_IN = (8192, 0)
PK_ATTN_IN = (10240, 0)
PK_OUT_ATTN = (12288, 0)
PK_OUT_CONV = (12800, 0)
PK_SMALL = (8192, 768)
PK_UQ = (10240, 768)
PK_UKV = (11008, 768)
PK_ROWS = 13312
SMALL_ROWS = 64
SMALL_META = 0
SMALL_CONV = 16
SMALL_REP = 24
SMALL_BITS_ROWS = 48
MATRIX_PLACES = (("w_mlp_up", PK_UP), ("w_mlp_down", PK_DOWN), ("w_in_conv", PK_CONV_IN), ("w_in_attn", PK_ATTN_IN),
                 ("w_out_attn", PK_OUT_ATTN), ("w_out_conv", PK_OUT_CONV), ("w_uq", PK_UQ), ("w_ukv", PK_UKV))


def _put(buf, x, place, *, name):
    row_off, col_off = place
    slabs = x.ndim == 3
    rows, w = x.shape[-2:]
    wpad = _round_up(w, LANES)
    assert col_off % wpad == 0
    bt = _common_tile(rows, row_off, align=16)

    def body(x_ref, _, o_ref):
        v = x_ref[...].astype(o_ref.dtype)
        if wpad != w:
            v = jnp.concatenate([v, jnp.zeros((bt, wpad - w), o_ref.dtype)], axis=1)
        o_ref[...] = v

    if slabs:
        grid = (4, rows // bt)
        x_spec = pl.BlockSpec((None, bt, w), lambda s, i: (s, i, 0))
        o_spec = pl.BlockSpec((None, bt, wpad), lambda s, i: (s, row_off // bt + i, col_off // wpad))
        sem = ("parallel", "parallel")
    else:
        grid = (rows // bt,)
        x_spec = pl.BlockSpec((bt, w), lambda i: (i, 0))
        o_spec = pl.BlockSpec((bt, wpad), lambda i: (row_off // bt + i, col_off // wpad))
        sem = ("parallel",)
    return pl.pallas_call(
        body, name=name, grid=grid, in_specs=[x_spec, ANY], out_specs=o_spec,
        out_shape=jax.ShapeDtypeStruct(buf.shape, buf.dtype), input_output_aliases={1: 0},
        compiler_params=_params(*sem),
    )(x, buf)


def _w_cols(place, layer, rows, width):
    base = (place[0] + layer * rows) // rows
    return dict(n=4 * width, tn=width, tk=rows, spec=pl.BlockSpec((None, rows, width), lambda i, j, k: (j, base, 0)))


def _w_cols_t(place, layer, rows, width):
    base = (place[0] + layer * rows) // rows
    return dict(n=rows, tn=rows, tk=width, spec=pl.BlockSpec((None, rows, width), lambda i, j, k: (k, base, 0)))


def _w_rows(place, layer, rows):
    base = (place[0] + layer * rows) // rows
    return dict(n=D_MODEL, tn=D_MODEL, tk=rows, spec=pl.BlockSpec((None, rows, D_MODEL), lambda i, j, k: (k, base, 0)))


def _w_rows_t(place, layer, rows):
    base = (place[0] + layer * rows) // rows
    return dict(n=4 * rows, tn=rows, tk=D_MODEL, spec=pl.BlockSpec((None, rows, D_MODEL), lambda i, j, k: (j, base, 0)))


def _g_cols(g, place, layer, rows, width):
    base = (place[0] + layer * rows) // rows
    return g, pl.BlockSpec((None, rows, width), lambda i, j, k: (j, base, 0))


def _g_rows(g, place, layer, rows):
    base = (place[0] + layer * rows) // rows
    return g, pl.BlockSpec((None, rows, D_MODEL), lambda i, j, k: (i, base, 0))


IN_PADW = _round_up(SHARD_IN, LANES)
IN_TAIL = ZA_FQ - SHARD_IN
IN_FL = SHARD_IN - HEADS
ZA_KPE = ZA_LAST + KPE_LANE


def _assemble_attn_in(gathered, layer):
    bt = 256
    base = (PK_ATTN_IN[0] + layer * D_MODEL) // bt
    assert 2 * SHARD_IN > ZA_FQ + MLA_ROPE and 3 * SHARD_IN < ATTN_IN - HEADS

    def body(s0, s1, s2, s3, o_ref):
        dt = o_ref.dtype
        z = lambda n: jnp.zeros((bt, n), dt)
        o_ref[...] = jnp.concatenate(
            [s0[:, :SHARD_IN], s1[:, :IN_TAIL], s1[:, IN_TAIL + MLA_ROPE:SHARD_IN], s2[:, :SHARD_IN], s3[:, :IN_FL],
             s3[:, IN_FL:SHARD_IN], z(KPE_LANE - HEADS), s1[:, IN_TAIL:IN_TAIL + MLA_ROPE],
             z(LANES - KPE_LANE - MLA_ROPE)], axis=1).astype(dt)

    def spec(s):
        return pl.BlockSpec((None, bt, IN_PADW), lambda i: (s, base + i, 0))

    return pl.pallas_call(
        body, name="assemble_attn_in", grid=(D_MODEL // bt,),
        in_specs=[spec(s) for s in range(4)], out_specs=pl.BlockSpec((bt, ZA_W), lambda i: (i, 0)),
        out_shape=jax.ShapeDtypeStruct((D_MODEL, ZA_W), MXU_DTYPE), compiler_params=_params("parallel"),
    )(gathered, gathered, gathered, gathered)


def _scatter_attn_in(g, dwa, layer):
    bt = 256
    base = (PK_ATTN_IN[0] + layer * D_MODEL) // bt
    fq1 = ZA_FQ + SHARD_IN - IN_TAIL - MLA_ROPE

    def body(d_ref, _, o_ref):
        pad = jnp.zeros((bt, IN_PADW - SHARD_IN), F32)
        pieces = (
            (d_ref[:, 0:SHARD_IN],),
            (d_ref[:, SHARD_IN:ZA_FQ], d_ref[:, ZA_KPE:ZA_KPE + MLA_ROPE], d_ref[:, ZA_FQ:fq1]),
            (d_ref[:, fq1:fq1 + SHARD_IN],),
            (d_ref[:, fq1 + SHARD_IN:ZA_LAST], d_ref[:, ZA_LAST:ZA_LAST + HEADS]),
        )
        for s in range(4):
            @pl.when(pl.program_id(0) == s)
            def _(s=s):
                o_ref[...] = jnp.concatenate(list(pieces[s]) + [pad], axis=1)

    return pl.pallas_call(
        body, name="scatter_attn_in", grid=(4, D_MODEL // bt),
        in_specs=[pl.BlockSpec((bt, ZA_W), lambda s, i: (i, 0)), ANY],
        out_specs=pl.BlockSpec((None, bt, IN_PADW), lambda s, i: (s, base + i, 0)),
        out_shape=jax.ShapeDtypeStruct(g.shape, g.dtype), input_output_aliases={1: 0},
        compiler_params=_params("parallel", "parallel"),
    )(dwa, g)


def _assemble_uq(gathered, layer):
    bt = 128
    base = (PK_UQ[0] + layer * Q_LORA) // bt
    col = PK_UQ[1] // SIDE_W

    def body(s0, s1, s2, s3, o_ref):
        dt = o_ref.dtype
        z = jnp.zeros((bt, LANES - MLA_QK), dt)
        parts = []
        for s_ref in (s0, s1, s2, s3):
            parts += [s_ref[:, 0:MLA_QK], z, s_ref[:, MLA_QK:2 * MLA_QK], z]
        o_ref[...] = jnp.concatenate(parts, axis=1).astype(dt)

    def spec(s):
        return pl.BlockSpec((None, bt, SIDE_W), lambda i: (s, base + i, col))

    return pl.pallas_call(
        body, name="assemble_uq", grid=(Q_LORA // bt,),
        in_specs=[spec(s) for s in range(4)], out_specs=pl.BlockSpec((bt, HEADS * LANES), lambda i: (i, 0)),
        out_shape=jax.ShapeDtypeStruct((Q_LORA, HEADS * LANES), MXU_DTYPE), compiler_params=_params("parallel"),
    )(gathered, gathered, gathered, gathered)


def _scatter_uq(g, dw, layer):
    bt = 128
    base = (PK_UQ[0] + layer * Q_LORA) // bt
    col = PK_UQ[1] // SIDE_W

    def body(d_ref, _, o_ref):
        o_ref[...] = jnp.concatenate([d_ref[:, 0:MLA_QK], d_ref[:, LANES:LANES + MLA_QK],
                                      jnp.zeros((bt, SIDE_W - 2 * MLA_QK), F32)], axis=1)

    return pl.pallas_call(
        body, name="scatter_uq", grid=(4, Q_LORA // bt),
        in_specs=[pl.BlockSpec((bt, 2 * LANES), lambda s, i: (i, s)), ANY],
        out_specs=pl.BlockSpec((None, bt, SIDE_W), lambda s, i: (s, base + i, col)),
        out_shape=jax.ShapeDtypeStruct(g.shape, g.dtype), input_output_aliases={1: 0},
        compiler_params=_params("parallel", "parallel"),
    )(dw, g)


def _assemble_ukv(gathered, layer):
    bt = KV_LORA
    base = (PK_UKV[0] + layer * KV_LORA) // bt
    col = PK_UKV[1] // SIDE_W
    hd = MLA_NOPE + HEAD_V

    def body(s0, s1, s2, s3, o_ref):
        dt = o_ref.dtype
        z = jnp.zeros((bt, LANES - MLA_NOPE), dt)
        keys, vals = [], []
        for s_ref in (s0, s1, s2, s3):
            for e in range(2):
                keys += [s_ref[:, e * hd:e * hd + MLA_NOPE], z]
                vals.append(s_ref[:, e * hd + MLA_NOPE:(e + 1) * hd])
        o_ref[...] = jnp.concatenate(keys + vals, axis=1).astype(dt)

    def spec(s):
        return pl.BlockSpec((None, bt, SIDE_W), lambda i: (s, base + i, col))

    return pl.pallas_call(
        body, name="assemble_ukv", grid=(1,),
        in_specs=[spec(s) for s in range(4)],
        out_specs=pl.BlockSpec((bt, HEADS * (LANES + HEAD_V)), lambda i: (i, 0)),
        out_shape=jax.ShapeDtypeStruct((KV_LORA, HEADS * (LANES + HEAD_V)), MXU_DTYPE), compiler_params=_params("parallel"),
    )(gathered, gathered, gathered, gathered)


def _scatter_ukv(g, dw, layer):
    bt = KV_LORA
    base = (PK_UKV[0] + layer * KV_LORA) // bt
    col = PK_UKV[1] // SIDE_W

    def body(k_ref, v_ref, _, o_ref):
        o_ref[...] = jnp.concatenate([k_ref[:, 0:MLA_NOPE], v_ref[:, 0:HEAD_V], k_ref[:, LANES:LANES + MLA_NOPE],
                                      v_ref[:, HEAD_V:2 * HEAD_V]], axis=1)

    return pl.pallas_call(
        body, name="scatter_ukv", grid=(4,),
        in_specs=[pl.BlockSpec((bt, 2 * LANES), lambda s: (0, s)),
                  pl.BlockSpec((bt, 2 * HEAD_V), lambda s: (0, HEADS * LANES // (2 * HEAD_V) + s)), ANY],
        out_specs=pl.BlockSpec((None, bt, SIDE_W), lambda s: (s, base, col)),
        out_shape=jax.ShapeDtypeStruct(g.shape, g.dtype), input_output_aliases={2: 0},
        compiler_params=_params("parallel"),
    )(dw, dw, g)


def _pad_lanes(v, n=LANES):
    return jnp.pad(v, (0, n - v.shape[0])).reshape(1, n)


def _relu2_up(acc):
    r = jnp.maximum(acc, 0.0)
    return acc, r * r


def _relu2_bwd(acc, u):
    return (acc * (2.0 * jnp.maximum(u, 0.0)),)


def _add_res(acc, res):
    return (acc + res,)


def _local_step(x, target, meta, small, gathered):
    seq = x.shape[0]
    t = seq + LANES
    d = D_MODEL
    h = jnp.concatenate([jnp.zeros((PAD, d), F32), meta.astype(F32), x], axis=0)
    c_tab, s_tab = _rope_tables(t)
    scale_mla, scale_fox = MLA_QK ** -0.5, FOX_DIM ** -0.5
    grads = {}
    saved = []
    g = jnp.zeros((4, PK_ROWS, PACK_W), F32)

    for layer in range(DEPTH):
        j = layer // 2
        sv = {"h_in": h}
        hn = _rmsnorm_fwd(h, small["g_mix"][layer])
        sv["hn"] = hn
        if layer % 2 == 0:
            w_in = _assemble_attn_in(gathered, j)
            w_uq = _assemble_uq(gathered, j)
            w_ukv = _assemble_ukv(gathered, j)
            out_place = PK_OUT_ATTN
            vecs = dict(
                g_cq=small["g_cq"][j].reshape(1, Q_LORA), g_ckv=small["g_ckv"][j].reshape(1, KV_LORA),
                g_qf=jnp.tile(small["g_q_fox"][j], 2).reshape(1, LANES), g_kf=jnp.tile(small["g_k_fox"][j], 2).reshape(1, LANES),
                b_f=_pad_lanes(small["b_forget"][j]), g_q=_pad_lanes(small["g_q_mla"][j]), g_k=_pad_lanes(small["g_k_mla"][j]))
            z = _matmul(hn, w_in, name="mm_attn_in")
            cqn, ckvn, qf, kf, vf, logf = _prep_a_fwd(z, vecs["g_cq"], vecs["g_ckv"], vecs["g_qf"], vecs["g_kf"], vecs["b_f"])
            f_cum, f_cum_t = _cumsum_rows(logf, reverse=False, name="cumsum_fwd")
            q_raw = _matmul(cqn, w_uq, name="mm_uq")
            kv_raw = _matmul(ckvn, w_ukv, name="mm_ukv")
            qn, kn, v_mla = _prep_b_fwd(q_raw, kv_raw, z, c_tab, s_tab, vecs["g_q"], vecs["g_k"])
            o_mla, lse_mla = _flash_fwd(qn, kn, v_mla, None, None, mla=True, scale=scale_mla, name="flash_fwd_mla")
            o_fox, lse_fox = _flash_fwd(qf, kf, vf, f_cum, f_cum_t, mla=False, scale=scale_fox, name="flash_fwd_fox")
            o = jnp.concatenate([o_mla, o_fox], axis=1)
            h = _matmul(o, gathered, b_tiles=_w_rows(out_place, j, SHARD_MIX), extras=(h,), epilogue=_add_res,
                        name="mm_mix_out")
            sv.update(w_in=w_in, w_uq=w_uq, w_ukv=w_ukv, out_place=out_place, vecs=vecs, z=z, cqn=cqn, ckvn=ckvn, qf=qf, kf=kf,
                      vf=vf, f_cum=f_cum, f_cum_t=f_cum_t, q_raw=q_raw, kv_raw=kv_raw, qn=qn, kn=kn, v_mla=v_mla, o=o,
                      lse_mla=lse_mla, lse_fox=lse_fox)
        else:
            out_place = PK_OUT_CONV
            conv_w8 = jnp.pad(small["conv_w"][j], ((0, 5), (0, 0)))
            z = _matmul(hn, gathered, b_tiles=_w_cols(PK_CONV_IN, j, d, SHARD_CONV), name="mm_conv_in")
            vmix = _conv_fwd(z, conv_w8)
            h = _matmul(vmix, gathered, b_tiles=_w_rows(out_place, j, SHARD_MIX), extras=(h,), epilogue=_add_res,
                        name="mm_mix_out")
            sv.update(out_place=out_place, conv_w8=conv_w8, z=z, vmix=vmix)
        sv["h_mid"] = h
        hn2 = _rmsnorm_fwd(h, small["g_mlp"][layer])
        u, a = _matmul(hn2, gathered, b_tiles=_w_cols(PK_UP, layer, d, d), epilogue=_relu2_up,
                       out_dtypes=(F32, MXU_DTYPE), name="mm_mlp_up")
        h = _matmul(a, gathered, b_tiles=_w_rows(PK_DOWN, layer, d), extras=(h,), epilogue=_add_res, name="mm_mlp_down")
        sv.update(hn2=hn2, u=u, a=a)
        saved.append(sv)

    dh, loss_local = _loss_head(h, target)

    dg_mix, dg_mlp = [None] * DEPTH, [None] * DEPTH
    per_even = {k: [None, None] for k in ("g_cq", "g_ckv", "g_q_mla", "g_k_mla", "g_q_fox", "g_k_fox", "b_forget")}
    per_odd = {"conv_w": [None, None]}
    for layer in reversed(range(DEPTH)):
        j = layer // 2
        sv = saved[layer]
        du = _matmul(dh, gathered, tb=True, b_tiles=_w_rows_t(PK_DOWN, layer, d), extras=(sv["u"],),
                     epilogue=_relu2_bwd, out_dtypes=(MXU_DTYPE,), name="mm_mlp_da")
        g = _matmul(sv["a"], dh, ta=True, out_into=_g_rows(g, PK_DOWN, layer, d), name="mm_dw_down")
        g = _matmul(sv["hn2"], du, ta=True, out_into=_g_cols(g, PK_UP, layer, d, d), name="mm_dw_up")
        dhn2 = _matmul(du, gathered, tb=True, b_tiles=_w_cols_t(PK_UP, layer, d, d), name="mm_mlp_dhn")
        dh, dg_mlp[layer] = _rmsnorm_bwd(sv["h_mid"], small["g_mlp"][layer], dhn2, dh)
        do = _matmul(dh, gathered, tb=True, b_tiles=_w_rows_t(sv["out_place"], j, SHARD_MIX), out_dtypes=(MXU_DTYPE,),
                     name="mm_mix_do")
        if layer % 2 == 0:
            vecs = sv["vecs"]
            g = _matmul(sv["o"], dh, ta=True, tm=SHARD_MIX, out_into=_g_rows(g, PK_OUT_ATTN, j, SHARD_MIX),
                        name="mm_dw_out")
            mla_args = (sv["qn"], sv["kn"], sv["v_mla"], sv["o"], do, sv["lse_mla"], None, None)
            dqn = _flash_bwd_dq(*mla_args, mla=True, scale=scale_mla, col0=0, name="flash_dq_mla")
            dkn, dv_mla = _flash_bwd_dkv(*mla_args, mla=True, scale=scale_mla, col0=0, name="flash_dkv_mla")
            fox_args = (sv["qf"], sv["kf"], sv["vf"], sv["o"], do, sv["lse_fox"], sv["f_cum"], sv["f_cum_t"])
            dqf, rs = _flash_bwd_dq(*fox_args, mla=False, scale=scale_fox, col0=HEADS // 2, name="flash_dq_fox")
            dkf, dvf, cs_t = _flash_bwd_dkv(*fox_args, mla=False, scale=scale_fox, col0=HEADS // 2, name="flash_dkv_fox")
            d_f = rs.reshape(t, HEADS // 2, LANES)[:, :, :2].reshape(t, HEADS)
            d_f = d_f - cs_t.reshape(HEADS // 2, 8, t)[:, :2, :].reshape(HEADS, t).T
            d_f = jnp.pad(d_f, ((0, 0), (0, LANES - HEADS)))
            dlogf, _ = _cumsum_rows(d_f, reverse=True, name="cumsum_bwd")
            dq_raw, dkv_raw, dkpe, dg_q, dg_k = _prep_b_bwd(sv["q_raw"], sv["kv_raw"], sv["z"], c_tab, s_tab, vecs["g_q"],
                                                            vecs["g_k"], dqn, dkn, dv_mla)
            g = _scatter_uq(g, _matmul(sv["cqn"], dq_raw, ta=True, name="mm_dw_uq"), j)
            g = _scatter_ukv(g, _matmul(sv["ckvn"], dkv_raw, ta=True, name="mm_dw_ukv"), j)
            dcqn = _matmul(dq_raw, sv["w_uq"], tb=True, name="mm_dcqn")
            dckvn = _matmul(dkv_raw, sv["w_ukv"], tb=True, name="mm_dckvn")
            dz, dg_cq, dg_ckv, dg_qf, dg_kf, db_f = _prep_a_bwd(
                sv["z"], vecs["g_cq"], vecs["g_ckv"], vecs["g_qf"], vecs["g_kf"], vecs["b_f"], dcqn, dckvn, dqf, dkf, dvf,
                dlogf, dkpe)
            g = _scatter_attn_in(g, _matmul(sv["hn"], dz, ta=True, name="mm_dw_attn_in"), j)
            per_even["g_cq"][j] = dg_cq[0]
            per_even["g_ckv"][j] = dg_ckv[0]
            per_even["g_q_mla"][j] = dg_q[0, :MLA_QK]
            per_even["g_k_mla"][j] = dg_k[0, :MLA_QK]
            per_even["g_q_fox"][j] = dg_qf[0, :FOX_DIM] + dg_qf[0, FOX_DIM:]
            per_even["g_k_fox"][j] = dg_kf[0, :FOX_DIM] + dg_kf[0, FOX_DIM:]
            per_even["b_forget"][j] = db_f[0, :HEADS]
            dhn = _matmul(dz, sv["w_in"], tb=True, name="mm_attn_dhn")
        else:
            g = _matmul(sv["vmix"], dh, ta=True, tm=SHARD_MIX, out_into=_g_rows(g, PK_OUT_CONV, j, SHARD_MIX),
                        name="mm_dw_out")
            dz, dcw = _conv_bwd(sv["z"], sv["conv_w8"], do)
            per_odd["conv_w"][j] = dcw[:3]
            g = _matmul(sv["hn"], dz, ta=True, tn=SHARD_CONV, out_into=_g_cols(g, PK_CONV_IN, j, d, SHARD_CONV),
                        name="mm_dw_conv_in")
            dhn = _matmul(dz, gathered, tb=True, b_tiles=_w_cols_t(PK_CONV_IN, j, d, SHARD_CONV), name="mm_conv_dhn")
        dh, dg_mix[layer] = _rmsnorm_bwd(sv["h_in"], small["g_mix"][layer], dhn, dh)

    grads["meta_tokens"] = dh[PAD:LANES]
    grads["g_mix"] = jnp.stack(dg_mix)
    grads["g_mlp"] = jnp.stack(dg_mlp)
    for k, v in list(per_even.items()) + list(per_odd.items()):
        grads[k] = jnp.stack(v)
    return loss_local, dh[LANES:], g, grads


def kernel(x, meta_tokens, g_mix, g_mlp, w_in_attn, g_cq, w_uq, g_ckv, w_ukv, g_q_mla, g_k_mla, g_q_fox, g_k_fox, b_forget, w_out_attn, w_in_conv, conv_w, w_out_conv, w_mlp_up, w_mlp_down, loss_target, m_meta_tokens, m_g_mix, m_g_mlp, m_w_in_attn, m_g_cq, m_w_uq, m_g_ckv, m_w_ukv, m_g_q_mla, m_g_k_mla, m_g_q_fox, m_g_k_fox, m_b_forget, m_w_out_attn, m_w_in_conv, m_conv_w, m_w_out_conv, m_w_mlp_up, m_w_mlp_down, v_meta_tokens, v_g_mix, v_g_mlp, v_w_in_attn, v_g_cq, v_w_uq, v_g_ckv, v_w_ukv, v_g_q_mla, v_g_k_mla, v_g_q_fox, v_g_k_fox, v_b_forget, v_w_out_attn, v_w_in_conv, v_conv_w, v_w_out_conv, v_w_mlp_up, v_w_mlp_down):
    args = dict(locals())
    weights = {n: args[n] for n in WEIGHT_ORDER}
    mom_m = {n: args["m_" + n] for n in WEIGHT_ORDER}
    mom_v = {n: args["v_" + n] for n in WEIGHT_ORDER}

    wire = jnp.bfloat16
    buf = jnp.zeros((PK_ROWS, PACK_W), wire)
    for name, place in MATRIX_PLACES:
        w = weights[name]
        buf = _put(buf, w.reshape(-1, w.shape[-1]), place, name="pack_weights")
    meta_bits = lax.bitcast_convert_type(meta_tokens, wire).reshape(2 * N_META, SIDE_W)
    conv_bits = lax.bitcast_convert_type(conv_w, wire).reshape(2 * N_ODD * 3, SIDE_W)
    bits = jnp.concatenate([meta_bits, conv_bits, jnp.zeros((SMALL_BITS_ROWS - 2 * N_META - 2 * N_ODD * 3, SIDE_W), wire)])
    buf = _put(buf, bits, PK_SMALL, name="pack_weights")
    gathered = _allgather_chips(buf)
    got_bits = gathered[:, PK_SMALL[0]:PK_SMALL[0] + SMALL_BITS_ROWS, PK_SMALL[1]:PK_SMALL[1] + SIDE_W]
    meta_full = lax.bitcast_convert_type(got_bits[:, :2 * N_META].reshape(4, N_META, SIDE_W, 2), F32)
    meta_full = meta_full.transpose(1, 0, 2).reshape(N_META, D_MODEL)
    conv_full = lax.bitcast_convert_type(
        got_bits[:, 2 * N_META:2 * N_META + 2 * N_ODD * 3].reshape(4, N_ODD, 3, SIDE_W, 2), F32)
    small = {n: weights[n] for n in REPLICATED}
    small["conv_w"] = conv_full.transpose(1, 2, 0, 3).reshape(N_ODD, 3, D_MODEL)

    loss_local, grad_x, g, grads = _local_step(x[0], loss_target[0], meta_full, small, gathered)
    loss = lax.psum(loss_local, MESH_AXES)

    rep = jnp.concatenate([grads[n].reshape(-1) for n in REPLICATED])
    rep = jnp.pad(rep, (0, (SMALL_ROWS - SMALL_REP) * SIDE_W - rep.shape[0])).reshape(SMALL_ROWS - SMALL_REP, SIDE_W)
    g_meta = grads["meta_tokens"].reshape(N_META, 4, SIDE_W).transpose(1, 0, 2)
    g_conv = grads["conv_w"].reshape(N_ODD * 3, 4, SIDE_W).transpose(1, 0, 2)
    small4 = jnp.concatenate([g_meta, g_conv, jnp.zeros((4, SMALL_REP - SMALL_CONV - N_ODD * 3, SIDE_W), F32),
                              jnp.broadcast_to(rep[None], (4,) + rep.shape)], axis=1)
    g = _put(g, small4, PK_SMALL, name="pack_small_grads")
    half = PK_ROWS // 2
    c = lax.axis_index("c")
    got = _swap_halves(g)
    kept = lax.dynamic_slice_in_dim(g, c * half, half, axis=1)
    pair = _add2(kept.reshape(4 * half, PACK_W), got.reshape(4 * half, PACK_W), name="grad_pair_sum").reshape(4, half, PACK_W)
    total_half = _sum4(_scatter_chips(pair), name="grad_chip_sum")
    g_tot = _join_halves(total_half).reshape(PK_ROWS, PACK_W)

    out = {}
    for name, place in MATRIX_PLACES:
        shape = weights[name].shape
        two_d = lambda a: a.reshape(-1, shape[-1])
        res = _adamw(two_d(weights[name]), two_d(mom_m[name]), two_d(mom_v[name]), g_tot, place[0], place[1])
        out[name] = [r.reshape(shape) for r in res]

    def small_pack(src):
        flat = jnp.concatenate([src[n].reshape(-1) for n in REPLICATED])
        flat = jnp.pad(flat, (0, (SMALL_ROWS - SMALL_REP) * SIDE_W - flat.shape[0])).reshape(SMALL_ROWS - SMALL_REP, SIDE_W)
        return jnp.concatenate([src["meta_tokens"], src["conv_w"].reshape(N_ODD * 3, SIDE_W),
                                jnp.zeros((SMALL_REP - SMALL_CONV - N_ODD * 3, SIDE_W), F32), flat])

    res = _adamw(small_pack(weights), small_pack(mom_m), small_pack(mom_v), g_tot, PK_SMALL[0], PK_SMALL[1])
    for name in ("meta_tokens", "conv_w") + REPLICATED:
        out[name] = []
    for r in res:
        out["meta_tokens"].append(r[SMALL_META:SMALL_META + N_META])
        out["conv_w"].append(r[SMALL_CONV:SMALL_CONV + N_ODD * 3].reshape(N_ODD, 3, SIDE_W))
        flat, off = r[SMALL_REP:].reshape(-1), 0
        for name in REPLICATED:
            n = weights[name].size
            out[name].append(flat[off:off + n].reshape(weights[name].shape))
            off += n
    return (loss, grad_x[None], *[out[n][0] for n in WEIGHT_ORDER], *[out[n][1] for n in WEIGHT_ORDER],
            *[out[n][2] for n in WEIGHT_ORDER], *[out[n][3] for n in WEIGHT_ORDER])
```

```python
import functools

import jax
import jax.numpy as jnp
from jax import lax
from jax.experimental import pallas as pl
from jax.experimental.pallas import tpu as pltpu

F32 = jnp.float32
MXU_DTYPE = jnp.bfloat16

D_MODEL = 1024
N_META = 16
LANES = 128
PAD = LANES - N_META
HEADS = 8
Q_LORA = 384
KV_LORA = 256
MLA_NOPE = 64
MLA_ROPE = 32
MLA_QK = MLA_NOPE + MLA_ROPE
HEAD_V = 64
FOX_DIM = 64
ROPE_BASE = 10000.0
D_FF = 4 * D_MODEL
DEPTH = 4
EPS = 1e-6
NEG = -1e30
ATTN_IN = Q_LORA + KV_LORA + MLA_ROPE + 3 * HEADS * FOX_DIM + HEADS

ZA_CQ = 0
ZA_CKV = Q_LORA
ZA_FQ = ZA_CKV + KV_LORA
ZA_FK = ZA_FQ + HEADS * FOX_DIM
ZA_FV = ZA_FK + HEADS * FOX_DIM
ZA_LAST = ZA_FV + HEADS * FOX_DIM
ZA_W = ZA_LAST + LANES
KPE_LANE = MLA_NOPE

ADAM_LR = 0.001
ADAM_B1 = 0.9
ADAM_B2 = 0.999
ADAM_EPS = 1e-08
ADAM_WD = 0.01
ADAM_STEP = 10

VMEM_LIMIT_BYTES = 52 * 1024 * 1024
ROW_TILE = 640
PREP_TILE = 320
ATTN_BLOCK = 640
LOSS_TILE = 128
MAX_TILE = 1536

MESH_AXES = ("x", "y", "c")
MESH = pl.DeviceIdType.MESH


def _params(*sem):
    return pltpu.CompilerParams(dimension_semantics=sem, vmem_limit_bytes=VMEM_LIMIT_BYTES)


def _tile(n, cap=None):
    cap = MAX_TILE if cap is None else cap
    if n <= cap:
        return n
    best = None
    for t in range(LANES, cap + 1, LANES):
        if n % t == 0:
            best = t
    assert best is not None, n
    return best


def _iota(shape, dim):
    return lax.broadcasted_iota(jnp.int32, shape, dim)


def _matmul(a, b, *, ta=False, tb=False, extras=(), epilogue=None, out_dtypes=(F32,), name, b_tiles=None,
            out_into=None, tm=None, tn=None):
    if ta:
        kdim, m = a.shape
    else:
        m, kdim = a.shape
    row_tile = min(ROW_TILE, m)
    if ta:
        tm_auto, tk = _tile(m), min(ROW_TILE, kdim)
    else:
        tm_auto, tk = (row_tile if m % row_tile == 0 else _tile(m)), _tile(kdim)
    tm = tm_auto if tm is None else tm
    if b_tiles is None:
        n = b.shape[0] if tb else b.shape[1]
        assert (b.shape[1] if tb else b.shape[0]) == kdim, (a.shape, b.shape, ta, tb)
        tn = _tile(n) if tn is None else tn
        b_spec = pl.BlockSpec((tn, tk), lambda i, j, k: (j, k)) if tb else pl.BlockSpec((tk, tn), lambda i, j, k: (k, j))
    else:
        n, tn, tk, b_spec = b_tiles["n"], b_tiles["tn"], b_tiles["tk"], b_tiles["spec"]
    nm, nn, nk = m // tm, n // tn, kdim // tk
    assert nm * tm == m and nn * tn == n and nk * tk == kdim, (m, n, kdim, tm, tn, tk)
    n_ex, n_out = len(extras), len(out_dtypes)
    n_alias = 0 if out_into is None else 1
    assert n_out == 1 or out_into is None
    dims = (((0 if ta else 1,), (1 if tb else 0,)), ((), ()))
    if epilogue is None:
        epilogue = lambda acc: (acc,)

    def body(a_ref, b_ref, *rest):
        ex_refs, out_refs, acc_ref = rest[:n_ex], rest[n_ex + n_alias:n_ex + n_alias + n_out], rest[-1]
        k = pl.program_id(2)

        @pl.when(k == 0)
        def _():
            acc_ref[...] = jnp.zeros_like(acc_ref)

        acc_ref[...] += lax.dot_general(a_ref[...].astype(MXU_DTYPE), b_ref[...].astype(MXU_DTYPE), dims,
                                        preferred_element_type=F32)

        @pl.when(k == nk - 1)
        def _():
            res = epilogue(acc_ref[...], *[e[...] for e in ex_refs])
            for o_ref, r in zip(out_refs, res):
                o_ref[...] = r.astype(o_ref.dtype)

    a_spec = pl.BlockSpec((tk, tm), lambda i, j, k: (k, i)) if ta else pl.BlockSpec((tm, tk), lambda i, j, k: (i, k))
    mn_spec = pl.BlockSpec((tm, tn), lambda i, j, k: (i, j))
    if out_into is None:
        outs = pl.pallas_call(
            body, name=name, grid=(nm, nn, nk),
            in_specs=[a_spec, b_spec] + [mn_spec] * n_ex,
            out_specs=[mn_spec] * n_out,
            out_shape=[jax.ShapeDtypeStruct((m, n), dt) for dt in out_dtypes],
            scratch_shapes=[pltpu.VMEM((tm, tn), F32)],
            compiler_params=_params("parallel", "parallel", "arbitrary"),
        )(a, b, *extras)
        return outs[0] if n_out == 1 else outs
    buf, buf_spec = out_into
    return pl.pallas_call(
        body, name=name, grid=(nm, nn, nk),
        in_specs=[a_spec, b_spec] + [mn_spec] * n_ex + [ANY],
        out_specs=buf_spec,
        out_shape=jax.ShapeDtypeStruct(buf.shape, buf.dtype),
        input_output_aliases={2 + n_ex: 0},
        scratch_shapes=[pltpu.VMEM((tm, tn), F32)],
        compiler_params=_params("parallel", "parallel", "arbitrary"),
    )(a, b, *extras, buf)


def _rmsnorm_fwd(x, g, *, name="rmsnorm_fwd"):
    t, d = x.shape
    bt = min(ROW_TILE, t)

    def body(x_ref, g_ref, o_ref):
        xv = x_ref[...]
        r = lax.rsqrt(jnp.mean(xv * xv, axis=-1, keepdims=True) + EPS)
        o_ref[...] = (xv * r * g_ref[...]).astype(o_ref.dtype)

    return pl.pallas_call(
        body, name=name, grid=(t // bt,),
        in_specs=[pl.BlockSpec((bt, d), lambda i: (i, 0)), pl.BlockSpec((1, d), lambda i: (0, 0))],
        out_specs=pl.BlockSpec((bt, d), lambda i: (i, 0)),
        out_shape=jax.ShapeDtypeStruct((t, d), MXU_DTYPE),
        compiler_params=_params("parallel"),
    )(x, g.reshape(1, d))


def _rmsnorm_bwd(x, g, dy, dres, *, name="rmsnorm_bwd"):
    t, d = x.shape
    bt = min(ROW_TILE, t)

    def body(x_ref, g_ref, dy_ref, dres_ref, dx_ref, dg_ref):
        @pl.when(pl.program_id(0) == 0)
        def _():
            dg_ref[...] = jnp.zeros_like(dg_ref)

        xv, dyv = x_ref[...], dy_ref[...].astype(F32)
        r = lax.rsqrt(jnp.mean(xv * xv, axis=-1, keepdims=True) + EPS)
        xhat = xv * r
        dxhat = dyv * g_ref[...]
        dx = r * (dxhat - xhat * jnp.mean(dxhat * xhat, axis=-1, keepdims=True))
        dx_ref[...] = dres_ref[...] + dx
        dg_ref[...] += jnp.sum(dyv * xhat, axis=0, keepdims=True)

    row = pl.BlockSpec((bt, d), lambda i: (i, 0))
    vec = pl.BlockSpec((1, d), lambda i: (0, 0))
    dx, dg = pl.pallas_call(
        body, name=name, grid=(t // bt,),
        in_specs=[row, vec, row, row], out_specs=[row, vec],
        out_shape=[jax.ShapeDtypeStruct((t, d), F32), jax.ShapeDtypeStruct((1, d), F32)],
        compiler_params=_params("arbitrary"),
    )(x, g.reshape(1, d), dy, dres)
    return dx, dg.reshape(d)


def _pair_rms(x, lo):
    x2 = x * x
    s_lo = jnp.sum(jnp.where(lo, x2, 0.0), axis=-1, keepdims=True)
    s_hi = jnp.sum(jnp.where(lo, 0.0, x2), axis=-1, keepdims=True)
    return jnp.where(lo, lax.rsqrt(s_lo / FOX_DIM + EPS), lax.rsqrt(s_hi / FOX_DIM + EPS))


def _pair_sum(x, lo):
    s_lo = jnp.sum(jnp.where(lo, x, 0.0), axis=-1, keepdims=True)
    s_hi = jnp.sum(jnp.where(lo, 0.0, x), axis=-1, keepdims=True)
    return jnp.where(lo, s_lo, s_hi)


def _prep_a_fwd(z, g_cq, g_ckv, g_qf, g_kf, b_f):
    t = z.shape[0]
    bt = min(PREP_TILE, t)
    hw = HEADS * FOX_DIM

    def body(z_ref, gcq_ref, gckv_ref, gqf_ref, gkf_ref, bf_ref, cqn_ref, ckvn_ref, qf_ref, kf_ref, vf_ref, logf_ref):
        i = pl.program_id(0)
        cq = z_ref[:, ZA_CQ:ZA_CQ + Q_LORA]
        cqn_ref[...] = (cq * lax.rsqrt(jnp.mean(cq * cq, axis=-1, keepdims=True) + EPS) * gcq_ref[...]).astype(cqn_ref.dtype)
        ckv = z_ref[:, ZA_CKV:ZA_CKV + KV_LORA]
        ckvn_ref[...] = (ckv * lax.rsqrt(jnp.mean(ckv * ckv, axis=-1, keepdims=True) + EPS) * gckv_ref[...]).astype(ckvn_ref.dtype)
        lo = _iota((1, LANES), 1) < FOX_DIM
        for p in range(HEADS // 2):
            sl = slice(p * LANES, (p + 1) * LANES)
            xq = z_ref[:, ZA_FQ + p * LANES:ZA_FQ + (p + 1) * LANES]
            qf_ref[:, sl] = (xq * _pair_rms(xq, lo) * gqf_ref[...]).astype(qf_ref.dtype)
            xk = z_ref[:, ZA_FK + p * LANES:ZA_FK + (p + 1) * LANES]
            kf_ref[:, sl] = (xk * _pair_rms(xk, lo) * gkf_ref[...]).astype(kf_ref.dtype)
        vf_ref[...] = z_ref[:, ZA_FV:ZA_FV + hw].astype(vf_ref.dtype)
        xl = z_ref[:, ZA_LAST:ZA_LAST + LANES] + bf_ref[...]
        logf = jnp.minimum(xl, 0.0) - jnp.log(1.0 + jnp.exp(-jnp.abs(xl)))
        row = i * bt + _iota((bt, LANES), 0)
        lane = _iota((bt, LANES), 1)
        logf_ref[...] = jnp.where((lane < HEADS) & (row >= PAD), logf, 0.0)

    def vec(w):
        return pl.BlockSpec((1, w), lambda i: (0, 0))

    def row(w):
        return pl.BlockSpec((bt, w), lambda i: (i, 0))

    return pl.pallas_call(
        body, name="attn_prep_a_fwd", grid=(t // bt,),
        in_specs=[row(ZA_W), vec(Q_LORA), vec(KV_LORA), vec(LANES), vec(LANES), vec(LANES)],
        out_specs=[row(Q_LORA), row(KV_LORA), row(hw), row(hw), row(hw), row(LANES)],
        out_shape=[jax.ShapeDtypeStruct((t, Q_LORA), MXU_DTYPE), jax.ShapeDtypeStruct((t, KV_LORA), MXU_DTYPE),
                   jax.ShapeDtypeStruct((t, hw), MXU_DTYPE), jax.ShapeDtypeStruct((t, hw), MXU_DTYPE),
                   jax.ShapeDtypeStruct((t, hw), MXU_DTYPE), jax.ShapeDtypeStruct((t, LANES), F32)],
        compiler_params=_params("parallel"),
    )(z, g_cq, g_ckv, g_qf, g_kf, b_f)


def _prep_a_bwd(z, g_cq, g_ckv, g_qf, g_kf, b_f, dcqn, dckvn, dqf, dkf, dvf, dlogf, dkpe):
    t = z.shape[0]
    bt = min(PREP_TILE, t)
    hw = HEADS * FOX_DIM

    def norm_bwd(x, g, dy):
        r = lax.rsqrt(jnp.mean(x * x, axis=-1, keepdims=True) + EPS)
        xhat = x * r
        dxhat = dy * g
        dx = r * (dxhat - xhat * jnp.mean(dxhat * xhat, axis=-1, keepdims=True))
        return dx, jnp.sum(dy * xhat, axis=0, keepdims=True)

    def body(z_ref, gcq_ref, gckv_ref, gqf_ref, gkf_ref, bf_ref, dcqn_ref, dckvn_ref, dqf_ref, dkf_ref, dvf_ref,
             dlogf_ref, dkpe_ref, dz_ref, dgcq_ref, dgckv_ref, dgqf_ref, dgkf_ref, dbf_ref):
        i = pl.program_id(0)

        @pl.when(i == 0)
        def _():
            for r in (dgcq_ref, dgckv_ref, dgqf_ref, dgkf_ref, dbf_ref):
                r[...] = jnp.zeros_like(r)

        dx, dg = norm_bwd(z_ref[:, ZA_CQ:ZA_CQ + Q_LORA], gcq_ref[...], dcqn_ref[...])
        dz_ref[:, ZA_CQ:ZA_CQ + Q_LORA] = dx.astype(dz_ref.dtype)
        dgcq_ref[...] += dg
        dx, dg = norm_bwd(z_ref[:, ZA_CKV:ZA_CKV + KV_LORA], gckv_ref[...], dckvn_ref[...])
        dz_ref[:, ZA_CKV:ZA_CKV + KV_LORA] = dx.astype(dz_ref.dtype)
        dgckv_ref[...] += dg
        lo = _iota((1, LANES), 1) < FOX_DIM
        for base, g_ref, dy_ref, dg_ref in ((ZA_FQ, gqf_ref, dqf_ref, dgqf_ref), (ZA_FK, gkf_ref, dkf_ref, dgkf_ref)):
            for p in range(HEADS // 2):
                x = z_ref[:, base + p * LANES:base + (p + 1) * LANES]
                dy = dy_ref[:, p * LANES:(p + 1) * LANES]
                r = _pair_rms(x, lo)
                xhat = x * r
                dxhat = dy * g_ref[...]
                dx = r * (dxhat - xhat * _pair_sum(dxhat * xhat, lo) / FOX_DIM)
                dz_ref[:, base + p * LANES:base + (p + 1) * LANES] = dx.astype(dz_ref.dtype)
                dg_ref[...] += jnp.sum(dy * xhat, axis=0, keepdims=True)
        dz_ref[:, ZA_FV:ZA_FV + hw] = dvf_ref[...].astype(dz_ref.dtype)
        xl = z_ref[:, ZA_LAST:ZA_LAST + LANES] + bf_ref[...]
        row = i * bt + _iota((bt, LANES), 0)
        lane = _iota((bt, LANES), 1)
        dfl = jnp.where((lane < HEADS) & (row >= PAD), dlogf_ref[...] / (1.0 + jnp.exp(xl)), 0.0)
        dbf_ref[...] += jnp.sum(dfl, axis=0, keepdims=True)
        dz_ref[:, ZA_LAST:ZA_LAST + LANES] = (dfl + dkpe_ref[...]).astype(dz_ref.dtype)

    def vec(w):
        return pl.BlockSpec((1, w), lambda i: (0, 0))

    def row(w):
        return pl.BlockSpec((bt, w), lambda i: (i, 0))

    return pl.pallas_call(
        body, name="attn_prep_a_bwd", grid=(t // bt,),
        in_specs=[row(ZA_W), vec(Q_LORA), vec(KV_LORA), vec(LANES), vec(LANES), vec(LANES),
                  row(Q_LORA), row(KV_LORA), row(hw), row(hw), row(hw), row(LANES), row(LANES)],
        out_specs=[row(ZA_W), vec(Q_LORA), vec(KV_LORA), vec(LANES), vec(LANES), vec(LANES)],
        out_shape=[jax.ShapeDtypeStruct((t, ZA_W), MXU_DTYPE), jax.ShapeDtypeStruct((1, Q_LORA), F32),
                   jax.ShapeDtypeStruct((1, KV_LORA), F32), jax.ShapeDtypeStruct((1, LANES), F32),
                   jax.ShapeDtypeStruct((1, LANES), F32), jax.ShapeDtypeStruct((1, LANES), F32)],
        compiler_params=_params("arbitrary"),
    )(z, g_cq, g_ckv, g_qf, g_kf, b_f, dcqn, dckvn, dqf, dkf, dvf, dlogf, dkpe)


def _cumsum_rows(x, *, reverse, name):
    t = x.shape[0]
    nblk = t // LANES

    def body(x_ref, f_ref, ft_ref, carry_ref):
        r = _iota((LANES, LANES), 0)
        c = _iota((LANES, LANES), 1)
        tri = jnp.where((c >= r) if reverse else (c <= r), 1.0, 0.0).astype(F32)
        carry_ref[...] = jnp.zeros_like(carry_ref)

        def step(s, _):
            b = (nblk - 1 - s) if reverse else s
            start = pl.multiple_of(b * LANES, LANES)
            blk = x_ref[pl.ds(start, LANES), :]
            cs = jnp.dot(tri, blk, precision=lax.Precision.HIGHEST, preferred_element_type=F32) + carry_ref[0:1, :]
            f_ref[pl.ds(start, LANES), :] = cs
            ft_ref[:, pl.ds(start, LANES)] = cs.T
            carry_ref[0:1, :] = cs[0:1, :] if reverse else cs[LANES - 1:LANES, :]
            return 0

        lax.fori_loop(0, nblk, step, 0)

    return pl.pallas_call(
        body, name=name, grid=(1,),
        in_specs=[pl.BlockSpec((t, LANES), lambda i: (0, 0))],
        out_specs=[pl.BlockSpec((t, LANES), lambda i: (0, 0)), pl.BlockSpec((LANES, t), lambda i: (0, 0))],
        out_shape=[jax.ShapeDtypeStruct((t, LANES), F32), jax.ShapeDtypeStruct((LANES, t), F32)],
        scratch_shapes=[pltpu.VMEM((8, LANES), F32)],
        compiler_params=_params("arbitrary"),
    )(x)


def _rope_partner(x, lane):
    half = MLA_ROPE // 2
    swapped = jnp.where(lane < KPE_LANE + half, pltpu.roll(x, LANES - half, axis=1), pltpu.roll(x, half, axis=1))
    return jnp.where((lane >= KPE_LANE) & (lane < KPE_LANE + MLA_ROPE), swapped, 0.0)


def _rope_tables(t):
    pos = (jnp.arange(t, dtype=jnp.int32) - PAD).astype(F32)
    inv_freq = ROPE_BASE ** (-jnp.arange(0, MLA_ROPE, 2, dtype=F32) / MLA_ROPE)
    ang = pos[:, None] * inv_freq[None, :]
    cos, sin = jnp.cos(ang), jnp.sin(ang)
    ones = jnp.ones((t, KPE_LANE), F32)
    tail = jnp.zeros((t, LANES - KPE_LANE - MLA_ROPE), F32)
    c_tab = jnp.concatenate([ones, cos, cos, tail + 1.0], axis=1)
    s_tab = jnp.concatenate([ones * 0.0, -sin, sin, tail], axis=1)
    return c_tab, s_tab


def _prep_b_fwd(q_raw, kv_raw, z, c_tab, s_tab, g_q, g_k):
    t = q_raw.shape[0]
    bt = min(PREP_TILE, t)
    qw = HEADS * LANES
    vw = HEADS * HEAD_V

    def body(q_ref, kv_ref, zl_ref, c_ref, s_ref, gq_ref, gk_ref, qn_ref, kn_ref, v_ref):
        lane = _iota((1, LANES), 1)
        kpe = jnp.where((lane >= KPE_LANE) & (lane < KPE_LANE + MLA_ROPE), zl_ref[...], 0.0)
        cv, sv = c_ref[...], s_ref[...]
        for h in range(HEADS):
            sl = slice(h * LANES, (h + 1) * LANES)
            for x, g_ref, o_ref in ((q_ref[:, sl], gq_ref, qn_ref), (kv_ref[:, sl] + kpe, gk_ref, kn_ref)):
                r = lax.rsqrt(jnp.sum(x * x, axis=-1, keepdims=True) / MLA_QK + EPS)
                xn = x * r * g_ref[...]
                o_ref[:, sl] = (xn * cv + _rope_partner(xn, lane) * sv).astype(o_ref.dtype)
        v_ref[...] = kv_ref[:, qw:qw + vw].astype(v_ref.dtype)

    def row(w):
        return pl.BlockSpec((bt, w), lambda i: (i, 0))

    vec = pl.BlockSpec((1, LANES), lambda i: (0, 0))
    return pl.pallas_call(
        body, name="attn_prep_b_fwd", grid=(t // bt,),
        in_specs=[row(qw), row(qw + vw), pl.BlockSpec((bt, LANES), lambda i: (i, ZA_LAST // LANES)),
                  row(LANES), row(LANES), vec, vec],
        out_specs=[row(qw), row(qw), row(vw)],
        out_shape=[jax.ShapeDtypeStruct((t, qw), MXU_DTYPE), jax.ShapeDtypeStruct((t, qw), MXU_DTYPE),
                   jax.ShapeDtypeStruct((t, vw), MXU_DTYPE)],
        compiler_params=_params("parallel"),
    )(q_raw, kv_raw, z, c_tab, s_tab, g_q, g_k)


def _prep_b_bwd(q_raw, kv_raw, z, c_tab, s_tab, g_q, g_k, dqn, dkn, dv):
    t = q_raw.shape[0]
    bt = min(PREP_TILE, t)
    qw = HEADS * LANES
    vw = HEADS * HEAD_V

    def body(q_ref, kv_ref, zl_ref, c_ref, s_ref, gq_ref, gk_ref, dqn_ref, dkn_ref, dv_ref,
             dq_ref, dkv_ref, dkpe_ref, dgq_ref, dgk_ref):
        @pl.when(pl.program_id(0) == 0)
        def _():
            dgq_ref[...] = jnp.zeros_like(dgq_ref)
            dgk_ref[...] = jnp.zeros_like(dgk_ref)

        lane = _iota((1, LANES), 1)
        rope_lanes = (lane >= KPE_LANE) & (lane < KPE_LANE + MLA_ROPE)
        kpe = jnp.where(rope_lanes, zl_ref[...], 0.0)
        cv, sv = c_ref[...], s_ref[...]
        dkpe = jnp.zeros((bt, LANES), F32)
        for h in range(HEADS):
            sl = slice(h * LANES, (h + 1) * LANES)
            for is_k, x, g_ref, dout, dg_ref in ((False, q_ref[:, sl], gq_ref, dqn_ref[:, sl], dgq_ref),
                                                  (True, kv_ref[:, sl] + kpe, gk_ref, dkn_ref[:, sl], dgk_ref)):
                r = lax.rsqrt(jnp.sum(x * x, axis=-1, keepdims=True) / MLA_QK + EPS)
                xhat = x * r
                dxn = dout * cv + _rope_partner(dout * sv, lane)
                dg_ref[...] += jnp.sum(dxn * xhat, axis=0, keepdims=True)
                dxhat = dxn * g_ref[...]
                dx = r * (dxhat - xhat * (jnp.sum(dxhat * xhat, axis=-1, keepdims=True) / MLA_QK))
                if is_k:
                    dkv_ref[:, sl] = jnp.where(lane < KPE_LANE, dx, 0.0).astype(dkv_ref.dtype)
                    dkpe = dkpe + jnp.where(rope_lanes, dx, 0.0)
                else:
                    dq_ref[:, sl] = dx.astype(dq_ref.dtype)
        dkv_ref[:, qw:qw + vw] = dv_ref[...].astype(dkv_ref.dtype)
        dkpe_ref[...] = dkpe

    def row(w):
        return pl.BlockSpec((bt, w), lambda i: (i, 0))

    vec = pl.BlockSpec((1, LANES), lambda i: (0, 0))
    return pl.pallas_call(
        body, name="attn_prep_b_bwd", grid=(t // bt,),
        in_specs=[row(qw), row(qw + vw), pl.BlockSpec((bt, LANES), lambda i: (i, ZA_LAST // LANES)),
                  row(LANES), row(LANES), vec, vec, row(qw), row(qw), row(vw)],
        out_specs=[row(qw), row(qw + vw), row(LANES), vec, vec],
        out_shape=[jax.ShapeDtypeStruct((t, qw), MXU_DTYPE), jax.ShapeDtypeStruct((t, qw + vw), MXU_DTYPE),
                   jax.ShapeDtypeStruct((t, LANES), F32), jax.ShapeDtypeStruct((1, LANES), F32),
                   jax.ShapeDtypeStruct((1, LANES), F32)],
        compiler_params=_params("arbitrary"),
    )(q_raw, kv_raw, z, c_tab, s_tab, g_q, g_k, dqn, dkn, dv)


NT_DIMS = (((1,), (1,)), ((), ()))
TN_DIMS = (((0,), (0,)), ((), ()))


def _head_qk(q_ref, k_ref, e, mla, lo):
    if mla:
        return q_ref[:, e * LANES:(e + 1) * LANES], k_ref[:, e * LANES:(e + 1) * LANES]
    q = q_ref[...]
    return jnp.where(lo if e == 0 else jnp.logical_not(lo), q, jnp.zeros_like(q)), k_ref[...]


def _attn_specs(mla, blk, q_map, k_map):
    w = 2 * LANES if mla else LANES
    q_spec = pl.BlockSpec((blk, w), lambda j, a, b: (q_map(a, b), j))
    k_spec = pl.BlockSpec((blk, w), lambda j, a, b: (k_map(a, b), j))
    qv_spec = pl.BlockSpec((blk, LANES), lambda j, a, b: (q_map(a, b), j))
    kv_spec = pl.BlockSpec((blk, LANES), lambda j, a, b: (k_map(a, b), j))
    fq_spec = pl.BlockSpec((blk, LANES), lambda j, a, b: (q_map(a, b), 0))
    fk_spec = pl.BlockSpec((8, blk), lambda j, a, b: (0, k_map(a, b)))
    return q_spec, k_spec, qv_spec, kv_spec, fq_spec, fk_spec


def _flash_fwd(q, k, v, f, f_t, *, mla, scale, name):
    t = q.shape[0]
    blk = min(ATTN_BLOCK, t)
    nb = t // blk
    pairs = HEADS // 2
    q_spec, k_spec, qv_spec, kv_spec, fq_spec, fk_spec = _attn_specs(
        mla, blk, lambda i, kk: i, lambda i, kk: jnp.minimum(kk, i))

    def body(*refs):
        if mla:
            q_ref, k_ref, v_ref, o_ref, lse_ref, m_s, l_s, acc_s = refs
            fq_ref = fk_ref = None
        else:
            q_ref, k_ref, v_ref, fq_ref, fk_ref, o_ref, lse_ref, m_s, l_s, acc_s = refs
        j, i, kk = pl.program_id(0), pl.program_id(1), pl.program_id(2)
        lo = _iota((1, LANES), 1) < HEAD_V

        @pl.when(kk == 0)
        def _():
            m_s[...] = jnp.full_like(m_s, NEG)
            l_s[...] = jnp.zeros_like(l_s)
            acc_s[...] = jnp.zeros_like(acc_s)

        def step(mask):
            vv = v_ref[...]
            for e in range(2):
                s, _, _ = _scores(q_ref, k_ref, fq_ref, fk_ref, e, j, mla, scale, lo, mask)
                m_prev = m_s[e]
                m_new = jnp.maximum(m_prev, jnp.max(s, axis=1, keepdims=True))
                alpha = jnp.exp(m_prev - m_new)
                p = jnp.exp(s - m_new)
                if mask is not None:
                    p = jnp.where(mask, p, 0.0)
                l_s[e] = alpha * l_s[e] + jnp.sum(p, axis=1, keepdims=True)
                acc_s[e] = alpha * acc_s[e] + jnp.dot(p.astype(MXU_DTYPE), vv, preferred_element_type=F32)
                m_s[e] = m_new

        _masked_and_plain(kk <= i, i, kk, blk, step)

        @pl.when(kk == nb - 1)
        def _():
            valid = (i * blk + _iota((blk, 1), 0)) >= PAD
            outs, lses = [], []
            for e in range(2):
                l = l_s[e]
                outs.append(acc_s[e] * jnp.where(l > 0.0, 1.0 / jnp.where(l > 0.0, l, 1.0), 0.0))
                lses.append(m_s[e] + jnp.log(jnp.where(l > 0.0, l, 1.0)))
            o = jnp.where(lo, outs[0], outs[1])
            o_ref[...] = jnp.where(valid, o, 0.0).astype(o_ref.dtype)
            lane = _iota((1, LANES), 1)
            lse_ref[...] = jnp.where(lane == 0, lses[0], jnp.where(lane == 1, lses[1], 0.0))

    in_specs = [q_spec, k_spec, kv_spec] + ([] if mla else [fq_spec, fk_spec])
    args = (q, k, v) + (() if mla else (f, f_t))
    hv = HEADS * HEAD_V
    return pl.pallas_call(
        body, name=name, grid=(pairs, nb, nb),
        in_specs=in_specs, out_specs=[qv_spec, qv_spec],
        out_shape=[jax.ShapeDtypeStruct((t, hv), F32), jax.ShapeDtypeStruct((t, hv), F32)],
        scratch_shapes=[pltpu.VMEM((2, blk, 1), F32), pltpu.VMEM((2, blk, 1), F32), pltpu.VMEM((2, blk, LANES), F32)],
        compiler_params=_params("parallel", "parallel", "arbitrary"),
    )(*args)


def _bwd_tile(q_ref, k_ref, v_ref, o_ref, do_ref, lse_ref, fq_ref, fk_ref, e, pair, mla, scale, lo, mask):
    s, qe, ke = _scores(q_ref, k_ref, fq_ref, fk_ref, e, pair, mla, scale, lo, mask)
    p = jnp.exp(s - lse_ref[:, e:e + 1])
    if mask is not None:
        p = jnp.where(mask, p, 0.0)
    do = do_ref[...]
    doe = jnp.where(lo if e == 0 else jnp.logical_not(lo), do, jnp.zeros_like(do))
    dp = lax.dot_general(doe, v_ref[...], NT_DIMS, preferred_element_type=F32)
    delta = jnp.sum(doe.astype(F32) * o_ref[...].astype(F32), axis=1, keepdims=True)
    return p, p * (dp - delta), qe, ke


def _flash_bwd_dq(q, k, v, o, do, lse, f, f_t, *, mla, scale, col0, name):
    t = q.shape[0]
    blk = min(ATTN_BLOCK, t)
    nb = t // blk
    pairs = HEADS // 2
    w = 2 * LANES if mla else LANES
    q_spec, k_spec, qv_spec, kv_spec, fq_spec, fk_spec = _attn_specs(
        mla, blk, lambda i, kk: i, lambda i, kk: jnp.minimum(kk, i))
    od_spec = pl.BlockSpec((blk, LANES), lambda j, i, kk: (i, col0 + j))

    def body(*refs):
        if mla:
            q_ref, k_ref, v_ref, o_ref, do_ref, lse_ref, dq_ref, dq_s = refs
            fq_ref = fk_ref = rs_ref = rs_s = None
        else:
            q_ref, k_ref, v_ref, o_ref, do_ref, lse_ref, fq_ref, fk_ref, dq_ref, rs_ref, dq_s, rs_s = refs
        j, i, kk = pl.program_id(0), pl.program_id(1), pl.program_id(2)
        lo = _iota((1, LANES), 1) < HEAD_V

        @pl.when(kk == 0)
        def _():
            dq_s[...] = jnp.zeros_like(dq_s)
            if not mla:
                rs_s[...] = jnp.zeros_like(rs_s)

        def step(mask):
            for e in range(2):
                _, ds, _, ke = _bwd_tile(q_ref, k_ref, v_ref, o_ref, do_ref, lse_ref, fq_ref, fk_ref, e, j, mla, scale,
                                         lo, mask)
                dq_s[e] += jnp.dot(ds.astype(MXU_DTYPE), ke, preferred_element_type=F32)
                if not mla:
                    rs_s[e] += jnp.sum(ds, axis=1, keepdims=True)

        _masked_and_plain(kk <= i, i, kk, blk, step)

        @pl.when(kk == nb - 1)
        def _():
            if mla:
                dq_ref[:, 0:LANES] = dq_s[0] * scale
                dq_ref[:, LANES:2 * LANES] = dq_s[1] * scale
            else:
                dq_ref[...] = jnp.where(lo, dq_s[0], dq_s[1]) * scale
                lane = _iota((1, LANES), 1)
                rs_ref[...] = jnp.where(lane == 0, rs_s[0], jnp.where(lane == 1, rs_s[1], 0.0))

    in_specs = [q_spec, k_spec, kv_spec, od_spec, od_spec, qv_spec] + ([] if mla else [fq_spec, fk_spec])
    args = (q, k, v, o, do, lse) + (() if mla else (f, f_t))
    out_specs = [q_spec] + ([] if mla else [qv_spec])
    out_shape = [jax.ShapeDtypeStruct((t, pairs * w), F32)]
    scratch = [pltpu.VMEM((2, blk, LANES), F32)]
    if not mla:
        out_shape.append(jax.ShapeDtypeStruct((t, pairs * LANES), F32))
        scratch.append(pltpu.VMEM((2, blk, 1), F32))
    outs = pl.pallas_call(
        body, name=name, grid=(pairs, nb, nb),
        in_specs=in_specs, out_specs=out_specs, out_shape=out_shape, scratch_shapes=scratch,
        compiler_params=_params("parallel", "parallel", "arbitrary"),
    )(*args)
    return outs[0] if mla else outs


def _flash_bwd_dkv(q, k, v, o, do, lse, f, f_t, *, mla, scale, col0, name):
    t = q.shape[0]
    blk = min(ATTN_BLOCK, t)
    nb = t // blk
    pairs = HEADS // 2
    w = 2 * LANES if mla else LANES
    q_spec, k_spec, qv_spec, kv_spec, fq_spec, fk_spec = _attn_specs(
        mla, blk, lambda a, b: jnp.maximum(a, b), lambda a, b: a)
    od_spec = pl.BlockSpec((blk, LANES), lambda j, a, b: (jnp.maximum(a, b), col0 + j))
    cs_spec = pl.BlockSpec((8, blk), lambda j, a, b: (j, a))

    def body(*refs):
        if mla:
            q_ref, k_ref, v_ref, o_ref, do_ref, lse_ref, dk_ref, dv_ref, dk_s, dv_s = refs
            fq_ref = fk_ref = cs_ref = cs_s = None
        else:
            q_ref, k_ref, v_ref, o_ref, do_ref, lse_ref, fq_ref, fk_ref, dk_ref, dv_ref, cs_ref, dk_s, dv_s, cs_s = refs
        j, kb, qb = pl.program_id(0), pl.program_id(1), pl.program_id(2)
        lo = _iota((1, LANES), 1) < HEAD_V

        @pl.when(qb == 0)
        def _():
            dk_s[...] = jnp.zeros_like(dk_s)
            dv_s[...] = jnp.zeros_like(dv_s)
            if not mla:
                cs_s[...] = jnp.zeros_like(cs_s)

        def step(mask):
            do = do_ref[...]
            for e in range(2):
                p, ds, _, _ = _bwd_tile(q_ref, k_ref, v_ref, o_ref, do_ref, lse_ref, fq_ref, fk_ref, e, j, mla, scale,
                                        lo, mask)
                dv_s[e] += lax.dot_general(p.astype(MXU_DTYPE), do, TN_DIMS, preferred_element_type=F32)
                q_src = q_ref[:, e * LANES:(e + 1) * LANES] if mla else q_ref[...]
                dk_s[e] += lax.dot_general(ds.astype(MXU_DTYPE), q_src, TN_DIMS, preferred_element_type=F32)
                if not mla:
                    cs_s[e] += jnp.sum(ds, axis=0, keepdims=True)

        _masked_and_plain(qb >= kb, qb, kb, blk, step)

        @pl.when(qb == nb - 1)
        def _():
            dv_ref[...] = jnp.where(lo, dv_s[0], dv_s[1])
            if mla:
                dk_ref[:, 0:LANES] = dk_s[0] * scale
                dk_ref[:, LANES:2 * LANES] = dk_s[1] * scale
            else:
                dk_ref[...] = jnp.where(lo, dk_s[0], dk_s[1]) * scale
                sub = _iota((8, 1), 0)
                cs_ref[...] = jnp.where(sub == 0, cs_s[0], jnp.where(sub == 1, cs_s[1], 0.0))

    in_specs = [q_spec, k_spec, kv_spec, od_spec, od_spec, qv_spec] + ([] if mla else [fq_spec, fk_spec])
    args = (q, k, v, o, do, lse) + (() if mla else (f, f_t))
    out_specs = [k_spec, kv_spec] + ([] if mla else [cs_spec])
    out_shape = [jax.ShapeDtypeStruct((t, pairs * w), F32), jax.ShapeDtypeStruct((t, HEADS * HEAD_V), F32)]
    scratch = [pltpu.VMEM((2, blk, LANES), F32), pltpu.VMEM((2, blk, LANES), F32)]
    if not mla:
        out_shape.append(jax.ShapeDtypeStruct((pairs * 8, t), F32))
        scratch.append(pltpu.VMEM((2, 1, blk), F32))
    return pl.pallas_call(
        body, name=name, grid=(pairs, nb, nb),
        in_specs=in_specs, out_specs=out_specs, out_shape=out_shape, scratch_shapes=scratch,
        compiler_params=_params("parallel", "parallel", "arbitrary"),
    )(*args)


ATTN_CHUNK = 16


def _select_lane(x, idx):
    return jnp.sum(jnp.where(_iota(x.shape, 1) == idx, x, 0.0), axis=1, keepdims=True)


def _select_row(x, idx):
    return jnp.sum(jnp.where(_iota(x.shape, 0) == idx, x, 0.0), axis=0, keepdims=True)


def _flash_fwd_chunked(q, k, v, f, f_t, *, mla, scale, name):
    t = q.shape[0]
    blk = min(ATTN_BLOCK, t)
    nb = t // blk
    pairs = HEADS // 2
    ch = ATTN_CHUNK
    q_spec, k_spec, qv_spec, kv_spec, fq_spec, fk_spec = _attn_specs(
        mla, blk, lambda i, kk: i, lambda i, kk: jnp.minimum(kk, i))
    lse_spec = pl.BlockSpec((8, blk), lambda j, i, kk: (j, i))

    def body(*refs):
        if mla:
            q_ref, k_ref, v_ref, o_ref, lse_ref, m_s, l_s, a_s, acc_s, s_s, p_s = refs
            fq_ref = fk_ref = None
        else:
            q_ref, k_ref, v_ref, fq_ref, fk_ref, o_ref, lse_ref, m_s, l_s, a_s, acc_s, s_s, p_s = refs
        j, i, kk = pl.program_id(0), pl.program_id(1), pl.program_id(2)
        lo = _iota((1, LANES), 1) < HEAD_V

        @pl.when(kk == 0)
        def _():
            m_s[...] = jnp.full_like(m_s, NEG)
            l_s[...] = jnp.zeros_like(l_s)
            acc_s[...] = jnp.zeros_like(acc_s)

        def step(masked):
            vv = v_ref[...]
            for e in range(2):
                qe, ke = _head_qk(q_ref, k_ref, e, mla, lo)
                s_s[...] = lax.dot_general(qe, ke, NT_DIMS, preferred_element_type=F32)
                fkr = None if mla else _select_row(fk_ref[...], 2 * j + e)

                def chunk(c, _, e=e, fkr=fkr):
                    r0 = pl.multiple_of(c * ch, ch)
                    rows = pl.ds(r0, ch)
                    s = s_s[rows, :] * scale
                    if not mla:
                        s = s + _select_lane(fq_ref[rows, :], 2 * j + e) - fkr
                    if masked:
                        rpos = i * blk + r0 + _iota((ch, blk), 0)
                        cpos = kk * blk + _iota((ch, blk), 1)
                        mask = (cpos <= rpos) & (cpos >= PAD)
                        s = jnp.where(mask, s, NEG)
                    m_prev = m_s[e, rows, :]
                    m_new = jnp.maximum(m_prev, jnp.max(s, axis=1, keepdims=True))
                    alpha = jnp.exp(m_prev - m_new)
                    p = jnp.exp(s - m_new)
                    if masked:
                        p = jnp.where(mask, p, 0.0)
                    l_s[e, rows, :] = alpha * l_s[e, rows, :] + jnp.sum(p, axis=1, keepdims=True)
                    m_s[e, rows, :] = m_new
                    a_s[rows, :] = alpha
                    p_s[rows, :] = p.astype(p_s.dtype)
                    return 0

                lax.fori_loop(0, blk // ch, chunk, 0)
                acc_s[e] = a_s[...] * acc_s[e] + jnp.dot(p_s[...], vv, preferred_element_type=F32)

        needs_mask = (kk == i) | (kk == 0)

        @pl.when((kk <= i) & needs_mask)
        def _():
            step(True)

        @pl.when((kk <= i) & jnp.logical_not(needs_mask))
        def _():
            step(False)

        @pl.when(kk == nb - 1)
        def _():
            valid = (i * blk + _iota((blk, 1), 0)) >= PAD
            outs, lses = [], []
            for e in range(2):
                l = l_s[e]
                outs.append(acc_s[e] * jnp.where(l > 0.0, 1.0 / jnp.where(l > 0.0, l, 1.0), 0.0))
                lses.append(m_s[e] + jnp.log(jnp.where(l > 0.0, l, 1.0)))
            o = jnp.where(lo, outs[0], outs[1])
            o_ref[...] = jnp.where(valid, o, 0.0).astype(o_ref.dtype)
            lane = _iota((1, LANES), 1)
            lse_cols = jnp.where(lane == 0, lses[0], jnp.where(lane == 1, lses[1], 0.0))
            lse_ref[...] = lse_cols.T[0:8, :]

    in_specs = [q_spec, k_spec, kv_spec] + ([] if mla else [fq_spec, fk_spec])
    args = (q, k, v) + (() if mla else (f, f_t))
    hv = HEADS * HEAD_V
    return pl.pallas_call(
        body, name=name, grid=(pairs, nb, nb),
        in_specs=in_specs, out_specs=[qv_spec, lse_spec],
        out_shape=[jax.ShapeDtypeStruct((t, hv), F32), jax.ShapeDtypeStruct((pairs * 8, t), F32)],
        scratch_shapes=[pltpu.VMEM((2, blk, 1), F32), pltpu.VMEM((2, blk, 1), F32), pltpu.VMEM((blk, 1), F32),
                        pltpu.VMEM((2, blk, LANES), F32), pltpu.VMEM((blk, blk), F32), pltpu.VMEM((blk, blk), MXU_DTYPE)],
        compiler_params=_params("parallel", "parallel", "arbitrary"),
    )(*args)


def _delta_rows(do, o):
    t, width = o.shape
    bt = min(ROW_TILE, t)
    n_heads = width // HEAD_V

    def body(do_ref, o_ref, d_ref):
        prod = do_ref[...].astype(F32) * o_ref[...]
        col = _iota((width, LANES), 0)
        first = _iota((width, LANES), 1) * HEAD_V
        sel = jnp.where((col >= first) & (col < first + HEAD_V), 1.0, 0.0).astype(F32)
        per_head = jnp.dot(prod, sel, precision=lax.Precision.HIGHEST, preferred_element_type=F32)
        d_ref[...] = per_head.T[0:n_heads, :]

    return pl.pallas_call(
        body, name="attn_delta", grid=(t // bt,),
        in_specs=[pl.BlockSpec((bt, width), lambda i: (i, 0)), pl.BlockSpec((bt, width), lambda i: (i, 0))],
        out_specs=pl.BlockSpec((n_heads, bt), lambda i: (0, i)),
        out_shape=jax.ShapeDtypeStruct((n_heads, t), F32),
        compiler_params=_params("parallel"),
    )(do, o)


def _flash_bwd_fused(q, k, v, do, lse_t, delta_t, f, f_t, *, mla, scale, col0, name):
    t = q.shape[0]
    blk = min(ATTN_BLOCK, t)
    nb = t // blk
    pairs = HEADS // 2
    ch = ATTN_CHUNK
    w = 2 * LANES if mla else LANES
    last = nb - 1
    qmap = lambda a, b: jnp.maximum(a, b)
    q_spec = pl.BlockSpec((blk, w), lambda j, a, b: (qmap(a, b), j))
    k_spec = pl.BlockSpec((blk, w), lambda j, a, b: (a, j))
    v_spec = pl.BlockSpec((blk, LANES), lambda j, a, b: (a, j))
    do_spec = pl.BlockSpec((blk, LANES), lambda j, a, b: (qmap(a, b), col0 + j))
    lse_spec = pl.BlockSpec((8, blk), lambda j, a, b: (j, qmap(a, b)))
    delta_spec = pl.BlockSpec((8, blk), lambda j, a, b: (col0 // (HEADS // 2), qmap(a, b)))
    fq_spec = pl.BlockSpec((8, blk), lambda j, a, b: (0, qmap(a, b)))
    fk_spec = pl.BlockSpec((blk, LANES), lambda j, a, b: (a, 0))
    dq_spec = pl.BlockSpec((blk, w), lambda j, a, b: (jnp.where(a == last, b, 0), j))
    rs_spec = pl.BlockSpec((8, blk), lambda j, a, b: (j, jnp.where(a == last, b, 0)))
    cs_spec = pl.BlockSpec((blk, LANES), lambda j, a, b: (a, j))

    def body(*refs):
        if mla:
            (q_ref, k_ref, v_ref, do_ref, lse_ref, delta_ref, dq_ref, dk_ref, dv_ref,
             dq_s, dk_s, dv_s, st_s, dpt_s, pt_s, dst_s) = refs
            fq_ref = fk_ref = rs_ref = cs_ref = rs_s = cs_s = None
        else:
            (q_ref, k_ref, v_ref, do_ref, lse_ref, delta_ref, fq_ref, fk_ref, dq_ref, dk_ref, dv_ref, rs_ref, cs_ref,
             dq_s, dk_s, dv_s, st_s, dpt_s, pt_s, dst_s, rs_s, cs_s) = refs
        j, kb, qb = pl.program_id(0), pl.program_id(1), pl.program_id(2)
        lo = _iota((1, LANES), 1) < HEAD_V

        @pl.when((kb == 0) & (qb == 0))
        def _():
            dq_s[...] = jnp.zeros_like(dq_s)
            if not mla:
                rs_s[...] = jnp.zeros_like(rs_s)

        @pl.when(qb == 0)
        def _():
            dk_s[...] = jnp.zeros_like(dk_s)
            dv_s[...] = jnp.zeros_like(dv_s)
            if not mla:
                cs_s[...] = jnp.zeros_like(cs_s)

        def step(masked):
            do = do_ref[...]
            vv = v_ref[...]
            for e in range(2):
                half = lo if e == 0 else jnp.logical_not(lo)
                qe, ke = _head_qk(q_ref, k_ref, e, mla, lo)
                doe = jnp.where(half, do, jnp.zeros_like(do))
                st_s[...] = lax.dot_general(ke, qe, NT_DIMS, preferred_element_type=F32)
                dpt_s[...] = lax.dot_general(vv, doe, NT_DIMS, preferred_element_type=F32)
                head = 2 * j + e
                lse_row = _select_row(lse_ref[...], e)
                delta_row = _select_row(delta_ref[...], head)
                fq_row = None if mla else _select_row(fq_ref[...], head)

                def chunk(c, _, e=e, lse_row=lse_row, delta_row=delta_row, fq_row=fq_row, head=head):
                    r0 = pl.multiple_of(c * ch, ch)
                    rows = pl.ds(r0, ch)
                    s = st_s[rows, :] * scale
                    if not mla:
                        s = s + fq_row - _select_lane(fk_ref[rows, :], head)
                    p = jnp.exp(s - lse_row)
                    if masked:
                        kpos = kb * blk + r0 + _iota((ch, blk), 0)
                        qpos = qb * blk + _iota((ch, blk), 1)
                        p = jnp.where((kpos <= qpos) & (kpos >= PAD), p, 0.0)
                    ds = p * (dpt_s[rows, :] - delta_row)
                    pt_s[rows, :] = p.astype(pt_s.dtype)
                    dst_s[rows, :] = ds.astype(dst_s.dtype)
                    if not mla:
                        cs_s[e, rows, :] += jnp.sum(ds, axis=1, keepdims=True)
                        rs_s[qb, e] += jnp.sum(ds, axis=0, keepdims=True)
                    return 0

                lax.fori_loop(0, blk // ch, chunk, 0)
                dv_s[e] += jnp.dot(pt_s[...], do, preferred_element_type=F32)
                q_src = qe if mla else q_ref[...]
                dk_s[e] += jnp.dot(dst_s[...], q_src, preferred_element_type=F32)
                dq_s[qb, e] += lax.dot_general(dst_s[...], ke, TN_DIMS, preferred_element_type=F32)

        needs_mask = (qb == kb) | (kb == 0)

        @pl.when((qb >= kb) & needs_mask)
        def _():
            step(True)

        @pl.when((qb >= kb) & jnp.logical_not(needs_mask))
        def _():
            step(False)

        @pl.when(qb == last)
        def _():
            dv_ref[...] = jnp.where(lo, dv_s[0], dv_s[1])
            if mla:
                dk_ref[:, 0:LANES] = dk_s[0] * scale
                dk_ref[:, LANES:2 * LANES] = dk_s[1] * scale
            else:
                dk_ref[...] = jnp.where(lo, dk_s[0], dk_s[1]) * scale
                lane = _iota((1, LANES), 1)
                cs_ref[...] = jnp.where(lane == 0, cs_s[0], jnp.where(lane == 1, cs_s[1], 0.0))

        @pl.when(kb == last)
        def _():
            if mla:
                dq_ref[:, 0:LANES] = dq_s[qb, 0] * scale
                dq_ref[:, LANES:2 * LANES] = dq_s[qb, 1] * scale
            else:
                dq_ref[...] = jnp.where(lo, dq_s[qb, 0], dq_s[qb, 1]) * scale
                sub = _iota((8, 1), 0)
                rs_ref[...] = jnp.where(sub == 0, rs_s[qb, 0], jnp.where(sub == 1, rs_s[qb, 1], 0.0))

    in_specs = [q_spec, k_spec, v_spec, do_spec, lse_spec, delta_spec] + ([] if mla else [fq_spec, fk_spec])
    args = (q, k, v, do, lse_t, delta_t) + (() if mla else (f_t, f))
    hv = HEADS * HEAD_V
    out_specs = [dq_spec, k_spec, v_spec]
    out_shape = [jax.ShapeDtypeStruct((t, pairs * w), F32), jax.ShapeDtypeStruct((t, pairs * w), F32),
                 jax.ShapeDtypeStruct((t, hv), F32)]
    scratch = [pltpu.VMEM((nb, 2, blk, LANES), F32), pltpu.VMEM((2, blk, LANES), F32), pltpu.VMEM((2, blk, LANES), F32),
               pltpu.VMEM((blk, blk), F32), pltpu.VMEM((blk, blk), F32), pltpu.VMEM((blk, blk), MXU_DTYPE),
               pltpu.VMEM((blk, blk), MXU_DTYPE)]
    if not mla:
        out_specs += [rs_spec, cs_spec]
        out_shape += [jax.ShapeDtypeStruct((pairs * 8, t), F32), jax.ShapeDtypeStruct((t, hv), F32)]
        scratch += [pltpu.VMEM((nb, 2, 1, blk), F32), pltpu.VMEM((2, blk, 1), F32)]
    return pl.pallas_call(
        body, name=name, grid=(pairs, nb, nb),
        in_specs=in_specs, out_specs=out_specs, out_shape=out_shape, scratch_shapes=scratch,
        compiler_params=_params("parallel", "arbitrary", "arbitrary"),
    )(*args)


def _shift_down(x, halo, n):
    rows = x.shape[0]
    r = _iota((rows, 1), 0)
    out = pltpu.roll(x, n, axis=0)
    for s in range(n):
        out = jnp.where(r == s, halo[8 - n + s:8 - n + s + 1, :], out)
    return out


def _shift_up(x, halo, n):
    rows = x.shape[0]
    r = _iota((rows, 1), 0)
    out = pltpu.roll(x, rows - n, axis=0)
    for s in range(n):
        out = jnp.where(r == rows - n + s, halo[s:s + 1, :], out)
    return out


def _conv_specs(bt, nblk):
    d = D_MODEL
    per8 = bt // 8
    z_spec = pl.BlockSpec((bt, 3 * d), lambda i: (i, 0))
    prev_spec = pl.BlockSpec((8, 3 * d), lambda i: (jnp.maximum(i * per8 - 1, 0), 0))
    next_z = pl.BlockSpec((8, 3 * d), lambda i: (jnp.minimum((i + 1) * per8, nblk * per8 - 1), 0))
    next_d = pl.BlockSpec((8, d), lambda i: (jnp.minimum((i + 1) * per8, nblk * per8 - 1), 0))
    w_spec = pl.BlockSpec((8, d), lambda i: (0, 0))
    row_spec = pl.BlockSpec((bt, d), lambda i: (i, 0))
    return z_spec, prev_spec, next_z, next_d, w_spec, row_spec


def _conv_taps(z_ref, prev_ref, i):
    d = D_MODEL
    g = z_ref[:, d:2 * d] * z_ref[:, 2 * d:3 * d]
    gh = jnp.where(i > 0, prev_ref[:, d:2 * d] * prev_ref[:, 2 * d:3 * d], 0.0)
    return g, _shift_down(g, gh, 1), _shift_down(g, gh, 2)


def _conv_fwd(z, conv_w8):
    t = z.shape[0]
    bt = min(PREP_TILE, t)
    nblk = t // bt
    d = D_MODEL
    z_spec, prev_spec, _, _, w_spec, row_spec = _conv_specs(bt, nblk)

    def body(z_ref, prev_ref, w_ref, v_ref):
        g, g1, g2 = _conv_taps(z_ref, prev_ref, pl.program_id(0))
        y = w_ref[0:1, :] * g2 + w_ref[1:2, :] * g1 + w_ref[2:3, :] * g
        v_ref[...] = (z_ref[:, 0:d] * y).astype(v_ref.dtype)

    return pl.pallas_call(
        body, name="conv_fwd", grid=(nblk,),
        in_specs=[z_spec, prev_spec, w_spec], out_specs=row_spec,
        out_shape=jax.ShapeDtypeStruct((t, d), MXU_DTYPE),
        compiler_params=_params("parallel"),
    )(z, z, conv_w8)


def _conv_bwd(z, conv_w8, dv):
    t = z.shape[0]
    bt = min(PREP_TILE, t)
    nblk = t // bt
    d = D_MODEL
    z_spec, prev_spec, next_z, next_d, w_spec, row_spec = _conv_specs(bt, nblk)

    def body(z_ref, prev_ref, nz_ref, dv_ref, ndv_ref, w_ref, dz_ref, dw_ref):
        i = pl.program_id(0)

        @pl.when(i == 0)
        def _():
            dw_ref[...] = jnp.zeros_like(dw_ref)

        g, g1, g2 = _conv_taps(z_ref, prev_ref, i)
        w0, w1, w2 = w_ref[0:1, :], w_ref[1:2, :], w_ref[2:3, :]
        y = w0 * g2 + w1 * g1 + w2 * g
        dvv = dv_ref[...].astype(F32)
        gate_b = z_ref[:, 0:d]
        dy = dvv * gate_b
        dyn = jnp.where(i < nblk - 1, ndv_ref[...].astype(F32) * nz_ref[:, 0:d], 0.0)
        dg = w2 * dy + w1 * _shift_up(dy, dyn, 1) + w0 * _shift_up(dy, dyn, 2)
        dz_ref[:, 0:d] = (dvv * y).astype(dz_ref.dtype)
        dz_ref[:, d:2 * d] = (dg * z_ref[:, 2 * d:3 * d]).astype(dz_ref.dtype)
        dz_ref[:, 2 * d:3 * d] = (dg * z_ref[:, d:2 * d]).astype(dz_ref.dtype)
        sub = _iota((8, 1), 0)
        s0 = jnp.sum(dy * g2, axis=0, keepdims=True)
        s1 = jnp.sum(dy * g1, axis=0, keepdims=True)
        s2 = jnp.sum(dy * g, axis=0, keepdims=True)
        dw_ref[...] += jnp.where(sub == 0, s0, jnp.where(sub == 1, s1, jnp.where(sub == 2, s2, 0.0)))

    return pl.pallas_call(
        body, name="conv_bwd", grid=(nblk,),
        in_specs=[z_spec, prev_spec, next_z, row_spec, next_d, w_spec], out_specs=[z_spec, w_spec],
        out_shape=[jax.ShapeDtypeStruct((t, 3 * d), MXU_DTYPE), jax.ShapeDtypeStruct((8, d), F32)],
        compiler_params=_params("arbitrary"),
    )(z, z, z, dv, dv, conv_w8)


def _loss_head(h, target):
    t, d = h.shape
    bt = LOSS_TILE
    assert LANES % bt == 0 or bt == LANES
    off = LANES // bt

    def body(h_ref, y_ref, dh_ref, acc_ref):
        i = pl.program_id(0)

        @pl.when(i == 0)
        def _():
            acc_ref[...] = jnp.zeros_like(acc_ref)

        @pl.when(i < off)
        def _():
            dh_ref[...] = jnp.zeros_like(dh_ref)

        @pl.when(i >= off)
        def _():
            err = h_ref[...] - y_ref[...]
            dh_ref[...] = err / d
            acc_ref[...] += jnp.sum(err * err)

    dh, acc = pl.pallas_call(
        body, name="loss_head", grid=(t // bt,),
        in_specs=[pl.BlockSpec((bt, d), lambda i: (i, 0)), pl.BlockSpec((bt, d), lambda i: (jnp.maximum(i - off, 0), 0))],
        out_specs=[pl.BlockSpec((bt, d), lambda i: (i, 0)), pl.BlockSpec((8, LANES), lambda i: (0, 0))],
        out_shape=[jax.ShapeDtypeStruct((t, d), F32), jax.ShapeDtypeStruct((8, LANES), F32)],
        compiler_params=_params("arbitrary"),
    )(h, target)
    return dh, acc[0, 0] * (0.5 / d)


def _common_tile(rows, row_off, cap=512, align=8):
    for b in range(min(cap, rows) // align * align, 0, -align):
        if rows % b == 0 and row_off % b == 0:
            return b
    raise ValueError((rows, row_off))


def _round_up(n, m):
    return -(-n // m) * m


def _adamw(w, m, v, g_buf, row_off, col_off):
    rows, width = w.shape
    wpad = _round_up(width, LANES)
    assert col_off % wpad == 0
    bt = _common_tile(rows, row_off)

    def body(w_ref, m_ref, v_ref, g_ref, go_ref, d_ref, nm_ref, nv_ref):
        gv = g_ref[...]
        if wpad != width:
            gv = gv[:, :width]
        m_new = ADAM_B1 * m_ref[...] + (1.0 - ADAM_B1) * gv
        v_new = ADAM_B2 * v_ref[...] + (1.0 - ADAM_B2) * jnp.square(gv)
        m_hat = m_new / (1.0 - ADAM_B1 ** ADAM_STEP)
        v_hat = v_new / (1.0 - ADAM_B2 ** ADAM_STEP)
        go_ref[...] = gv
        d_ref[...] = -ADAM_LR * (m_hat / (jnp.sqrt(v_hat) + ADAM_EPS) + ADAM_WD * w_ref[...])
        nm_ref[...] = m_new
        nv_ref[...] = v_new

    spec = pl.BlockSpec((bt, width), lambda i: (i, 0))
    g_spec = pl.BlockSpec((bt, wpad), lambda i: (row_off // bt + i, col_off // wpad))
    return pl.pallas_call(
        body, name="adamw", grid=(rows // bt,),
        in_specs=[spec] * 3 + [g_spec], out_specs=[spec] * 4,
        out_shape=[jax.ShapeDtypeStruct((rows, width), F32)] * 4,
        compiler_params=_params("parallel"),
    )(w, m, v, g_buf)


def _add2(a, b, *, out_dtype, name):
    rows, width = a.shape
    bt = next(x for x in range(min(rows, 640), 0, -16) if rows % x == 0)

    def body(a_ref, b_ref, o_ref):
        o_ref[...] = (a_ref[...] + b_ref[...]).astype(o_ref.dtype)

    spec = pl.BlockSpec((bt, width), lambda i: (i, 0))
    return pl.pallas_call(
        body, name=name, grid=(rows // bt,), in_specs=[spec, spec], out_specs=spec,
        out_shape=jax.ShapeDtypeStruct((rows, width), out_dtype), compiler_params=_params("parallel"),
    )(a, b)


def _sum4(parts, *, name):
    _, rows, width = parts.shape
    bt = next(x for x in range(min(rows, 640), 0, -16) if rows % x == 0)

    def body(p_ref, o_ref):
        p = [p_ref[n].astype(F32) for n in range(4)]
        o_ref[...] = ((p[0] + p[1]) + p[2]) + p[3]

    return pl.pallas_call(
        body, name=name, grid=(rows // bt,),
        in_specs=[pl.BlockSpec((4, bt, width), lambda i: (0, i, 0))],
        out_specs=pl.BlockSpec((bt, width), lambda i: (i, 0)),
        out_shape=jax.ShapeDtypeStruct((rows, width), F32), compiler_params=_params("parallel"),
    )(parts)


ANY = pl.BlockSpec(memory_space=pl.ANY)
CHIP_FLIPS = ((1, 0), (0, 1), (1, 1))


def _place():
    return lax.axis_index("x"), lax.axis_index("y"), lax.axis_index("c")


def _flip(v, f):
    return 1 - v if f else v


def _allgather_chips(pack):
    rows, width = pack.shape
    half = rows // 2

    def body(pack_ref, out_ref, send_sems, recv_sems, local_sem):
        x, y, c = _place()
        me = 2 * x + y
        sibling = (x, y, 1 - c)
        mine = pltpu.make_async_copy(pack_ref, out_ref.at[me], local_sem)
        mine.start()
        my_rows = pl.ds(pl.multiple_of(c * half, 8), half)
        sib_rows = pl.ds(pl.multiple_of((1 - c) * half, 8), half)
        first, passed = [], []
        for n, (fx, fy) in enumerate(CHIP_FLIPS):
            px, py = _flip(x, fx), _flip(y, fy)
            peer = 2 * px + py
            first.append(pltpu.make_async_remote_copy(
                src_ref=pack_ref.at[my_rows], dst_ref=out_ref.at[me, my_rows],
                send_sem=send_sems.at[n], recv_sem=recv_sems.at[n], device_id=(px, py, c), device_id_type=MESH))
            passed.append(pltpu.make_async_remote_copy(
                src_ref=out_ref.at[peer, my_rows], dst_ref=out_ref.at[peer, my_rows],
                send_sem=send_sems.at[3 + n], recv_sem=recv_sems.at[3 + n], device_id=sibling, device_id_type=MESH))
        for cp in first:
            cp.start()
        for n, (fx, fy) in enumerate(CHIP_FLIPS):
            peer = 2 * _flip(x, fx) + _flip(y, fy)
            pltpu.make_async_remote_copy(
                src_ref=pack_ref.at[my_rows], dst_ref=out_ref.at[peer, my_rows],
                send_sem=send_sems.at[n], recv_sem=recv_sems.at[n], device_id=sibling, device_id_type=MESH).wait_recv()
            passed[n].start()
        for n, (fx, fy) in enumerate(CHIP_FLIPS):
            peer = 2 * _flip(x, fx) + _flip(y, fy)
            pltpu.make_async_remote_copy(
                src_ref=pack_ref.at[sib_rows], dst_ref=out_ref.at[peer, sib_rows],
                send_sem=send_sems.at[3 + n], recv_sem=recv_sems.at[3 + n], device_id=sibling,
                device_id_type=MESH).wait_recv()
        for cp in first + passed:
            cp.wait_send()
        mine.wait()

    return pl.pallas_call(
        body, name="allgather_weights",
        in_specs=[ANY], out_specs=ANY,
        out_shape=jax.ShapeDtypeStruct((4, rows, width), pack.dtype),
        scratch_shapes=[pltpu.SemaphoreType.DMA((6,)), pltpu.SemaphoreType.DMA((6,)), pltpu.SemaphoreType.DMA],
    )(pack)


def _swap_halves(g):
    _, rows, width = g.shape
    half = rows // 2

    def body(g_ref, got_ref, send_sem, recv_sem):
        x, y, c = _place()
        away = pl.ds(pl.multiple_of((1 - c) * half, 8), half)
        cp = pltpu.make_async_remote_copy(
            src_ref=g_ref.at[:, away], dst_ref=got_ref, send_sem=send_sem, recv_sem=recv_sem,
            device_id=(x, y, 1 - c), device_id_type=MESH)
        cp.start()
        cp.wait()

    return pl.pallas_call(
        body, name="grad_swap_halves",
        in_specs=[ANY], out_specs=ANY,
        out_shape=jax.ShapeDtypeStruct((4, half, width), g.dtype),
        scratch_shapes=[pltpu.SemaphoreType.DMA, pltpu.SemaphoreType.DMA],
    )(g)


def _scatter_chips(s):
    _, rows, width = s.shape

    def body(s_ref, out_ref, send_sems, recv_sems, local_sem):
        x, y, c = _place()
        me = 2 * x + y
        mine = pltpu.make_async_copy(s_ref.at[me], out_ref.at[me], local_sem)
        mine.start()
        copies = []
        for n, (fx, fy) in enumerate(CHIP_FLIPS):
            px, py = _flip(x, fx), _flip(y, fy)
            copies.append(pltpu.make_async_remote_copy(
                src_ref=s_ref.at[2 * px + py], dst_ref=out_ref.at[me],
                send_sem=send_sems.at[n], recv_sem=recv_sems.at[n], device_id=(px, py, c), device_id_type=MESH))
        for cp in copies:
            cp.start()
        for n, (fx, fy) in enumerate(CHIP_FLIPS):
            peer = 2 * _flip(x, fx) + _flip(y, fy)
            pltpu.make_async_remote_copy(
                src_ref=s_ref.at[me], dst_ref=out_ref.at[peer],
                send_sem=send_sems.at[n], recv_sem=recv_sems.at[n], device_id=(x, y, c), device_id_type=MESH).wait_recv()
        for cp in copies:
            cp.wait_send()
        mine.wait()

    return pl.pallas_call(
        body, name="grad_scatter_chips",
        in_specs=[ANY], out_specs=ANY,
        out_shape=jax.ShapeDtypeStruct((4, rows, width), s.dtype),
        scratch_shapes=[pltpu.SemaphoreType.DMA((3,)), pltpu.SemaphoreType.DMA((3,)), pltpu.SemaphoreType.DMA],
    )(s)


def _join_halves(tot):
    rows, width = tot.shape

    def body(t_ref, out_ref, send_sem, recv_sem, local_sem):
        x, y, c = _place()
        mine = pltpu.make_async_copy(t_ref, out_ref.at[c], local_sem)
        mine.start()
        cp = pltpu.make_async_remote_copy(
            src_ref=t_ref, dst_ref=out_ref.at[c], send_sem=send_sem, recv_sem=recv_sem,
            device_id=(x, y, 1 - c), device_id_type=MESH)
        cp.start()
        pltpu.make_async_remote_copy(
            src_ref=t_ref, dst_ref=out_ref.at[1 - c], send_sem=send_sem, recv_sem=recv_sem,
            device_id=(x, y, 1 - c), device_id_type=MESH).wait_recv()
        cp.wait_send()
        mine.wait()

    return pl.pallas_call(
        body, name="grad_join_halves",
        in_specs=[ANY], out_specs=ANY,
        out_shape=jax.ShapeDtypeStruct((2, rows, width), tot.dtype),
        scratch_shapes=[pltpu.SemaphoreType.DMA, pltpu.SemaphoreType.DMA, pltpu.SemaphoreType.DMA],
    )(tot)


PACK_W = 1024
REPLICATED = ("g_mix", "g_mlp", "g_cq", "g_ckv", "g_q_mla", "g_k_mla", "g_q_fox", "g_k_fox", "b_forget")
WEIGHT_ORDER = ("meta_tokens", "g_mix", "g_mlp", "w_in_attn", "g_cq", "w_uq", "g_ckv", "w_ukv", "g_q_mla", "g_k_mla",
                "g_q_fox", "g_k_fox", "b_forget", "w_out_attn", "w_in_conv", "conv_w", "w_out_conv", "w_mlp_up",
                "w_mlp_down")
N_EVEN = 2
N_ODD = 2
SHARD_IN = ATTN_IN // 4
SHARD_MIX = D_MODEL // 4
SHARD_UQ = HEADS * MLA_QK // 4
SHARD_UKV = HEADS * (MLA_NOPE + HEAD_V) // 4
SHARD_CONV = 3 * D_MODEL // 4
SIDE_W = 256
PK_UP = (0, 0)
PK_DOWN = (4096, 0)
PK_CONV_IN = (8192, 0)
PK_ATTN_IN = (10240, 0)
PK_OUT_ATTN = (12288, 0)
PK_OUT_CONV = (12800, 0)
PK_SMALL = (8192, 768)
PK_UQ = (10240, 768)
PK_UKV = (11008, 768)
PK_ROWS = 13312
SMALL_ROWS = 64
SMALL_META = 0
SMALL_CONV = 16
SMALL_REP = 24
SMALL_BITS_ROWS = 48
MATRIX_PLACES = (("w_mlp_up", PK_UP), ("w_mlp_down", PK_DOWN), ("w_in_conv", PK_CONV_IN), ("w_in_attn", PK_ATTN_IN),
                 ("w_out_attn", PK_OUT_ATTN), ("w_out_conv", PK_OUT_CONV), ("w_uq", PK_UQ), ("w_ukv", PK_UKV))


def _put(buf, x, place, *, name):
    row_off, col_off = place
    slabs = x.ndim == 3
    rows, w = x.shape[-2:]
    wpad = _round_up(w, LANES)
    assert col_off % wpad == 0
    bt = _common_tile(rows, row_off, align=16)

    def body(x_ref, _, o_ref):
        v = x_ref[...].astype(o_ref.dtype)
        if wpad != w:
            v = jnp.concatenate([v, jnp.zeros((bt, wpad - w), o_ref.dtype)], axis=1)
        o_ref[...] = v

    if slabs:
        grid = (4, rows // bt)
        x_spec = pl.BlockSpec((None, bt, w), lambda s, i: (s, i, 0))
        o_spec = pl.BlockSpec((None, bt, wpad), lambda s, i: (s, row_off // bt + i, col_off // wpad))
        sem = ("parallel", "parallel")
    else:
        grid = (rows // bt,)
        x_spec = pl.BlockSpec((bt, w), lambda i: (i, 0))
        o_spec = pl.BlockSpec((bt, wpad), lambda i: (row_off // bt + i, col_off // wpad))
        sem = ("parallel",)
    return pl.pallas_call(
        body, name=name, grid=grid, in_specs=[x_spec, ANY], out_specs=o_spec,
        out_shape=jax.ShapeDtypeStruct(buf.shape, buf.dtype), input_output_aliases={1: 0},
        compiler_params=_params(*sem),
    )(x, buf)


def _w_cols(place, layer, rows, width):
    base = (place[0] + layer * rows) // rows
    return dict(n=4 * width, tn=width, tk=rows, spec=pl.BlockSpec((None, rows, width), lambda i, j, k: (j, base, 0)))


def _w_cols_t(place, layer, rows, width):
    base = (place[0] + layer * rows) // rows
    return dict(n=rows, tn=rows, tk=width, spec=pl.BlockSpec((None, rows, width), lambda i, j, k: (k, base, 0)))


def _w_rows(place, layer, rows):
    base = (place[0] + layer * rows) // rows
    return dict(n=D_MODEL, tn=D_MODEL, tk=rows, spec=pl.BlockSpec((None, rows, D_MODEL), lambda i, j, k: (k, base, 0)))


def _w_rows_t(place, layer, rows):
    base = (place[0] + layer * rows) // rows
    return dict(n=4 * rows, tn=rows, tk=D_MODEL, spec=pl.BlockSpec((None, rows, D_MODEL), lambda i, j, k: (j, base, 0)))


def _g_cols(g, place, layer, rows, width):
    base = (place[0] + layer * rows) // rows
    return g, pl.BlockSpec((None, rows, width), lambda i, j, k: (j, base, 0))


def _g_rows(g, place, layer, rows):
    base = (place[0] + layer * rows) // rows
    return g, pl.BlockSpec((None, rows, D_MODEL), lambda i, j, k: (i, base, 0))


IN_PADW = _round_up(SHARD_IN, LANES)
IN_TAIL = ZA_FQ - SHARD_IN
IN_FL = SHARD_IN - HEADS
ZA_KPE = ZA_LAST + KPE_LANE


def _assemble_attn_in(gathered, layer):
    bt = 256
    base = (PK_ATTN_IN[0] + layer * D_MODEL) // bt
    assert 2 * SHARD_IN > ZA_FQ + MLA_ROPE and 3 * SHARD_IN < ATTN_IN - HEADS

    def body(s0, s1, s2, s3, o_ref):
        dt = o_ref.dtype
        z = lambda n: jnp.zeros((bt, n), dt)
        o_ref[...] = jnp.concatenate(
            [s0[:, :SHARD_IN], s1[:, :IN_TAIL], s1[:, IN_TAIL + MLA_ROPE:SHARD_IN], s2[:, :SHARD_IN], s3[:, :IN_FL],
             s3[:, IN_FL:SHARD_IN], z(KPE_LANE - HEADS), s1[:, IN_TAIL:IN_TAIL + MLA_ROPE],
             z(LANES - KPE_LANE - MLA_ROPE)], axis=1).astype(dt)

    def spec(s):
        return pl.BlockSpec((None, bt, IN_PADW), lambda i: (s, base + i, 0))

    return pl.pallas_call(
        body, name="assemble_attn_in", grid=(D_MODEL // bt,),
        in_specs=[spec(s) for s in range(4)], out_specs=pl.BlockSpec((bt, ZA_W), lambda i: (i, 0)),
        out_shape=jax.ShapeDtypeStruct((D_MODEL, ZA_W), MXU_DTYPE), compiler_params=_params("parallel"),
    )(gathered, gathered, gathered, gathered)


def _scatter_attn_in(g, dwa, layer):
    bt = 256
    base = (PK_ATTN_IN[0] + layer * D_MODEL) // bt
    fq1 = ZA_FQ + SHARD_IN - IN_TAIL - MLA_ROPE

    def body(d_ref, _, o_ref):
        pad = jnp.zeros((bt, IN_PADW - SHARD_IN), F32)
        pieces = (
            (d_ref[:, 0:SHARD_IN],),
            (d_ref[:, SHARD_IN:ZA_FQ], d_ref[:, ZA_KPE:ZA_KPE + MLA_ROPE], d_ref[:, ZA_FQ:fq1]),
            (d_ref[:, fq1:fq1 + SHARD_IN],),
            (d_ref[:, fq1 + SHARD_IN:ZA_LAST], d_ref[:, ZA_LAST:ZA_LAST + HEADS]),
        )
        for s in range(4):
            @pl.when(pl.program_id(0) == s)
            def _(s=s):
                o_ref[...] = jnp.concatenate(list(pieces[s]) + [pad], axis=1)

    return pl.pallas_call(
        body, name="scatter_attn_in", grid=(4, D_MODEL // bt),
        in_specs=[pl.BlockSpec((bt, ZA_W), lambda s, i: (i, 0)), ANY],
        out_specs=pl.BlockSpec((None, bt, IN_PADW), lambda s, i: (s, base + i, 0)),
        out_shape=jax.ShapeDtypeStruct(g.shape, g.dtype), input_output_aliases={1: 0},
        compiler_params=_params("parallel", "parallel"),
    )(dwa, g)


def _assemble_uq(gathered, layer):
    bt = 128
    base = (PK_UQ[0] + layer * Q_LORA) // bt
    col = PK_UQ[1] // SIDE_W

    def body(s0, s1, s2, s3, o_ref):
        dt = o_ref.dtype
        z = jnp.zeros((bt, LANES - MLA_QK), dt)
        parts = []
        for s_ref in (s0, s1, s2, s3):
            parts += [s_ref[:, 0:MLA_QK], z, s_ref[:, MLA_QK:2 * MLA_QK], z]
        o_ref[...] = jnp.concatenate(parts, axis=1).astype(dt)

    def spec(s):
        return pl.BlockSpec((None, bt, SIDE_W), lambda i: (s, base + i, col))

    return pl.pallas_call(
        body, name="assemble_uq", grid=(Q_LORA // bt,),
        in_specs=[spec(s) for s in range(4)], out_specs=pl.BlockSpec((bt, HEADS * LANES), lambda i: (i, 0)),
        out_shape=jax.ShapeDtypeStruct((Q_LORA, HEADS * LANES), MXU_DTYPE), compiler_params=_params("parallel"),
    )(gathered, gathered, gathered, gathered)


def _scatter_uq(g, dw, layer):
    bt = 128
    base = (PK_UQ[0] + layer * Q_LORA) // bt
    col = PK_UQ[1] // SIDE_W

    def body(d_ref, _, o_ref):
        o_ref[...] = jnp.concatenate([d_ref[:, 0:MLA_QK], d_ref[:, LANES:LANES + MLA_QK],
                                      jnp.zeros((bt, SIDE_W - 2 * MLA_QK), F32)], axis=1)

    return pl.pallas_call(
        body, name="scatter_uq", grid=(4, Q_LORA // bt),
        in_specs=[pl.BlockSpec((bt, 2 * LANES), lambda s, i: (i, s)), ANY],
        out_specs=pl.BlockSpec((None, bt, SIDE_W), lambda s, i: (s, base + i, col)),
        out_shape=jax.ShapeDtypeStruct(g.shape, g.dtype), input_output_aliases={1: 0},
        compiler_params=_params("parallel", "parallel"),
    )(dw, g)


def _assemble_ukv(gathered, layer):
    bt = KV_LORA
    base = (PK_UKV[0] + layer * KV_LORA) // bt
    col = PK_UKV[1] // SIDE_W
    hd = MLA_NOPE + HEAD_V

    def body(s0, s1, s2, s3, o_ref):
        dt = o_ref.dtype
        z = jnp.zeros((bt, LANES - MLA_NOPE), dt)
        keys, vals = [], []
        for s_ref in (s0, s1, s2, s3):
            for e in range(2):
                keys += [s_ref[:, e * hd:e * hd + MLA_NOPE], z]
                vals.append(s_ref[:, e * hd + MLA_NOPE:(e + 1) * hd])
        o_ref[...] = jnp.concatenate(keys + vals, axis=1).astype(dt)

    def spec(s):
        return pl.BlockSpec((None, bt, SIDE_W), lambda i: (s, base + i, col))

    return pl.pallas_call(
        body, name="assemble_ukv", grid=(1,),
        in_specs=[spec(s) for s in range(4)],
        out_specs=pl.BlockSpec((bt, HEADS * (LANES + HEAD_V)), lambda i: (i, 0)),
        out_shape=jax.ShapeDtypeStruct((KV_LORA, HEADS * (LANES + HEAD_V)), MXU_DTYPE), compiler_params=_params("parallel"),
    )(gathered, gathered, gathered, gathered)


def _scatter_ukv(g, dw, layer):
    bt = KV_LORA
    base = (PK_UKV[0] + layer * KV_LORA) // bt
    col = PK_UKV[1] // SIDE_W

    def body(k_ref, v_ref, _, o_ref):
        o_ref[...] = jnp.concatenate([k_ref[:, 0:MLA_NOPE], v_ref[:, 0:HEAD_V], k_ref[:, LANES:LANES + MLA_NOPE],
                                      v_ref[:, HEAD_V:2 * HEAD_V]], axis=1)

    return pl.pallas_call(
        body, name="scatter_ukv", grid=(4,),
        in_specs=[pl.BlockSpec((bt, 2 * LANES), lambda s: (0, s)),
                  pl.BlockSpec((bt, 2 * HEAD_V), lambda s: (0, HEADS * LANES // (2 * HEAD_V) + s)), ANY],
        out_specs=pl.BlockSpec((None, bt, SIDE_W), lambda s: (s, base, col)),
        out_shape=jax.ShapeDtypeStruct(g.shape, g.dtype), input_output_aliases={2: 0},
        compiler_params=_params("parallel"),
    )(dw, dw, g)


def _pad_lanes(v, n=LANES):
    return jnp.pad(v, (0, n - v.shape[0])).reshape(1, n)


def _relu2_up(acc):
    r = jnp.maximum(acc, 0.0)
    return acc, r * r


def _relu2_bwd(acc, u):
    return (acc * (2.0 * jnp.maximum(u, 0.0)),)


def _add_res(acc, res):
    return (acc + res,)


def _local_step(x, target, meta, small, gathered):
    seq = x.shape[0]
    t = seq + LANES
    d = D_MODEL
    h = jnp.concatenate([jnp.zeros((PAD, d), F32), meta.astype(F32), x], axis=0)
    c_tab, s_tab = _rope_tables(t)
    scale_mla, scale_fox = MLA_QK ** -0.5, FOX_DIM ** -0.5
    grads = {}
    saved = []
    g = jnp.zeros((4, PK_ROWS, PACK_W), F32)

    for layer in range(DEPTH):
        j = layer // 2
        sv = {"h_in": h}
        hn = _rmsnorm_fwd(h, small["g_mix"][layer])
        sv["hn"] = hn
        if layer % 2 == 0:
            w_in = _assemble_attn_in(gathered, j)
            w_uq = _assemble_uq(gathered, j)
            w_ukv = _assemble_ukv(gathered, j)
            out_place = PK_OUT_ATTN
            vecs = dict(
                g_cq=small["g_cq"][j].reshape(1, Q_LORA), g_ckv=small["g_ckv"][j].reshape(1, KV_LORA),
                g_qf=jnp.tile(small["g_q_fox"][j], 2).reshape(1, LANES), g_kf=jnp.tile(small["g_k_fox"][j], 2).reshape(1, LANES),
                b_f=_pad_lanes(small["b_forget"][j]), g_q=_pad_lanes(small["g_q_mla"][j]), g_k=_pad_lanes(small["g_k_mla"][j]))
            z = _matmul(hn, w_in, name="mm_attn_in")
            cqn, ckvn, qf, kf, vf, logf = _prep_a_fwd(z, vecs["g_cq"], vecs["g_ckv"], vecs["g_qf"], vecs["g_kf"], vecs["b_f"])
            f_cum, f_cum_t = _cumsum_rows(logf, reverse=False, name="cumsum_fwd")
            q_raw = _matmul(cqn, w_uq, name="mm_uq")
            kv_raw = _matmul(ckvn, w_ukv, name="mm_ukv")
            qn, kn, v_mla = _prep_b_fwd(q_raw, kv_raw, z, c_tab, s_tab, vecs["g_q"], vecs["g_k"])
            o_mla, lse_mla = _flash_fwd_chunked(qn, kn, v_mla, None, None, mla=True, scale=scale_mla, name="flash_fwd_mla")
            o_fox, lse_fox = _flash_fwd_chunked(qf, kf, vf, f_cum, f_cum_t, mla=False, scale=scale_fox,
                                                name="flash_fwd_fox")
            o = jnp.concatenate([o_mla, o_fox], axis=1)
            h = _matmul(o, gathered, b_tiles=_w_rows(out_place, j, SHARD_MIX), extras=(h,), epilogue=_add_res,
                        name="mm_mix_out")
            sv.update(w_in=w_in, w_uq=w_uq, w_ukv=w_ukv, out_place=out_place, vecs=vecs, z=z, cqn=cqn, ckvn=ckvn, qf=qf, kf=kf,
                      vf=vf, f_cum=f_cum, f_cum_t=f_cum_t, q_raw=q_raw, kv_raw=kv_raw, qn=qn, kn=kn, v_mla=v_mla, o=o,
                      lse_mla=lse_mla, lse_fox=lse_fox)
        else:
            out_place = PK_OUT_CONV
            conv_w8 = jnp.pad(small["conv_w"][j], ((0, 5), (0, 0)))
            z = _matmul(hn, gathered, b_tiles=_w_cols(PK_CONV_IN, j, d, SHARD_CONV), name="mm_conv_in")
            vmix = _conv_fwd(z, conv_w8)
            h = _matmul(vmix, gathered, b_tiles=_w_rows(out_place, j, SHARD_MIX), extras=(h,), epilogue=_add_res,
                        name="mm_mix_out")
            sv.update(out_place=out_place, conv_w8=conv_w8, z=z, vmix=vmix)
        sv["h_mid"] = h
        hn2 = _rmsnorm_fwd(h, small["g_mlp"][layer])
        u, a = _matmul(hn2, gathered, b_tiles=_w_cols(PK_UP, layer, d, d), epilogue=_relu2_up,
                       out_dtypes=(F32, MXU_DTYPE), name="mm_mlp_up")
        h = _matmul(a, gathered, b_tiles=_w_rows(PK_DOWN, layer, d), extras=(h,), epilogue=_add_res, name="mm_mlp_down")
        sv.update(hn2=hn2, u=u, a=a)
        saved.append(sv)

    dh, loss_local = _loss_head(h, target)

    dg_mix, dg_mlp = [None] * DEPTH, [None] * DEPTH
    per_even = {k: [None, None] for k in ("g_cq", "g_ckv", "g_q_mla", "g_k_mla", "g_q_fox", "g_k_fox", "b_forget")}
    per_odd = {"conv_w": [None, None]}
    for layer in reversed(range(DEPTH)):
        j = layer // 2
        sv = saved[layer]
        du = _matmul(dh, gathered, tb=True, b_tiles=_w_rows_t(PK_DOWN, layer, d), extras=(sv["u"],),
                     epilogue=_relu2_bwd, out_dtypes=(MXU_DTYPE,), name="mm_mlp_da")
        g = _matmul(sv["a"], dh, ta=True, out_into=_g_rows(g, PK_DOWN, layer, d), name="mm_dw_down")
        g = _matmul(sv["hn2"], du, ta=True, out_into=_g_cols(g, PK_UP, layer, d, d), name="mm_dw_up")
        dhn2 = _matmul(du, gathered, tb=True, b_tiles=_w_cols_t(PK_UP, layer, d, d), name="mm_mlp_dhn")
        dh, dg_mlp[layer] = _rmsnorm_bwd(sv["h_mid"], small["g_mlp"][layer], dhn2, dh)
        do = _matmul(dh, gathered, tb=True, b_tiles=_w_rows_t(sv["out_place"], j, SHARD_MIX), out_dtypes=(MXU_DTYPE,),
                     name="mm_mix_do")
        if layer % 2 == 0:
            vecs = sv["vecs"]
            g = _matmul(sv["o"], dh, ta=True, tm=SHARD_MIX, out_into=_g_rows(g, PK_OUT_ATTN, j, SHARD_MIX),
                        name="mm_dw_out")
            delta_t = _delta_rows(do, sv["o"])
            dqn, dkn, dv_mla = _flash_bwd_fused(sv["qn"], sv["kn"], sv["v_mla"], do, sv["lse_mla"], delta_t, None, None,
                                                mla=True, scale=scale_mla, col0=0, name="flash_bwd_mla")
            dqf, dkf, dvf, rs_t, cs = _flash_bwd_fused(sv["qf"], sv["kf"], sv["vf"], do, sv["lse_fox"], delta_t,
                                                       sv["f_cum"], sv["f_cum_t"], mla=False, scale=scale_fox,
                                                       col0=HEADS // 2, name="flash_bwd_fox")
            d_f = rs_t.reshape(HEADS // 2, 8, t)[:, :2, :].reshape(HEADS, t).T
            d_f = d_f - cs.reshape(t, HEADS // 2, LANES)[:, :, :2].reshape(t, HEADS)
            d_f = jnp.pad(d_f, ((0, 0), (0, LANES - HEADS)))
            dlogf, _ = _cumsum_rows(d_f, reverse=True, name="cumsum_bwd")
            dq_raw, dkv_raw, dkpe, dg_q, dg_k = _prep_b_bwd(sv["q_raw"], sv["kv_raw"], sv["z"], c_tab, s_tab, vecs["g_q"],
                                                            vecs["g_k"], dqn, dkn, dv_mla)
            g = _scatter_uq(g, _matmul(sv["cqn"], dq_raw, ta=True, name="mm_dw_uq"), j)
            g = _scatter_ukv(g, _matmul(sv["ckvn"], dkv_raw, ta=True, name="mm_dw_ukv"), j)
            dcqn = _matmul(dq_raw, sv["w_uq"], tb=True, name="mm_dcqn")
            dckvn = _matmul(dkv_raw, sv["w_ukv"], tb=True, name="mm_dckvn")
            dz, dg_cq, dg_ckv, dg_qf, dg_kf, db_f = _prep_a_bwd(
                sv["z"], vecs["g_cq"], vecs["g_ckv"], vecs["g_qf"], vecs["g_kf"], vecs["b_f"], dcqn, dckvn, dqf, dkf, dvf,
                dlogf, dkpe)
            g = _scatter_attn_in(g, _matmul(sv["hn"], dz, ta=True, name="mm_dw_attn_in"), j)
            per_even["g_cq"][j] = dg_cq[0]
            per_even["g_ckv"][j] = dg_ckv[0]
            per_even["g_q_mla"][j] = dg_q[0, :MLA_QK]
            per_even["g_k_mla"][j] = dg_k[0, :MLA_QK]
            per_even["g_q_fox"][j] = dg_qf[0, :FOX_DIM] + dg_qf[0, FOX_DIM:]
            per_even["g_k_fox"][j] = dg_kf[0, :FOX_DIM] + dg_kf[0, FOX_DIM:]
            per_even["b_forget"][j] = db_f[0, :HEADS]
            dhn = _matmul(dz, sv["w_in"], tb=True, name="mm_attn_dhn")
        else:
            g = _matmul(sv["vmix"], dh, ta=True, tm=SHARD_MIX, out_into=_g_rows(g, PK_OUT_CONV, j, SHARD_MIX),
                        name="mm_dw_out")
            dz, dcw = _conv_bwd(sv["z"], sv["conv_w8"], do)
            per_odd["conv_w"][j] = dcw[:3]
            g = _matmul(sv["hn"], dz, ta=True, tn=SHARD_CONV, out_into=_g_cols(g, PK_CONV_IN, j, d, SHARD_CONV),
                        name="mm_dw_conv_in")
            dhn = _matmul(dz, gathered, tb=True, b_tiles=_w_cols_t(PK_CONV_IN, j, d, SHARD_CONV), name="mm_conv_dhn")
        dh, dg_mix[layer] = _rmsnorm_bwd(sv["h_in"], small["g_mix"][layer], dhn, dh)

    grads["meta_tokens"] = dh[PAD:LANES]
    grads["g_mix"] = jnp.stack(dg_mix)
    grads["g_mlp"] = jnp.stack(dg_mlp)
    for k, v in list(per_even.items()) + list(per_odd.items()):
        grads[k] = jnp.stack(v)
    return loss_local, dh[LANES:], g, grads


def kernel(x, meta_tokens, g_mix, g_mlp, w_in_attn, g_cq, w_uq, g_ckv, w_ukv, g_q_mla, g_k_mla, g_q_fox, g_k_fox, b_forget, w_out_attn, w_in_conv, conv_w, w_out_conv, w_mlp_up, w_mlp_down, loss_target, m_meta_tokens, m_g_mix, m_g_mlp, m_w_in_attn, m_g_cq, m_w_uq, m_g_ckv, m_w_ukv, m_g_q_mla, m_g_k_mla, m_g_q_fox, m_g_k_fox, m_b_forget, m_w_out_attn, m_w_in_conv, m_conv_w, m_w_out_conv, m_w_mlp_up, m_w_mlp_down, v_meta_tokens, v_g_mix, v_g_mlp, v_w_in_attn, v_g_cq, v_w_uq, v_g_ckv, v_w_ukv, v_g_q_mla, v_g_k_mla, v_g_q_fox, v_g_k_fox, v_b_forget, v_w_out_attn, v_w_in_conv, v_conv_w, v_w_out_conv, v_w_mlp_up, v_w_mlp_down):
    args = dict(locals())
    weights = {n: args[n] for n in WEIGHT_ORDER}
    mom_m = {n: args["m_" + n] for n in WEIGHT_ORDER}
    mom_v = {n: args["v_" + n] for n in WEIGHT_ORDER}

    wire = jnp.bfloat16
    buf = jnp.zeros((PK_ROWS, PACK_W), wire)
    for name, place in MATRIX_PLACES:
        w = weights[name]
        buf = _put(buf, w.reshape(-1, w.shape[-1]), place, name="pack_weights")
    meta_bits = lax.bitcast_convert_type(meta_tokens, wire).reshape(2 * N_META, SIDE_W)
    conv_bits = lax.bitcast_convert_type(conv_w, wire).reshape(2 * N_ODD * 3, SIDE_W)
    bits = jnp.concatenate([meta_bits, conv_bits, jnp.zeros((SMALL_BITS_ROWS - 2 * N_META - 2 * N_ODD * 3, SIDE_W), wire)])
    buf = _put(buf, bits, PK_SMALL, name="pack_weights")
    gathered = _allgather_chips(buf)
    got_bits = gathered[:, PK_SMALL[0]:PK_SMALL[0] + SMALL_BITS_ROWS, PK_SMALL[1]:PK_SMALL[1] + SIDE_W]
    meta_full = lax.bitcast_convert_type(got_bits[:, :2 * N_META].reshape(4, N_META, SIDE_W, 2), F32)
    meta_full = meta_full.transpose(1, 0, 2).reshape(N_META, D_MODEL)
    conv_full = lax.bitcast_convert_type(
        got_bits[:, 2 * N_META:2 * N_META + 2 * N_ODD * 3].reshape(4, N_ODD, 3, SIDE_W, 2), F32)
    small = {n: weights[n] for n in REPLICATED}
    small["conv_w"] = conv_full.transpose(1, 2, 0, 3).reshape(N_ODD, 3, D_MODEL)

    loss_local, grad_x, g, grads = _local_step(x[0], loss_target[0], meta_full, small, gathered)
    loss = lax.psum(loss_local, MESH_AXES)

    rep = jnp.concatenate([grads[n].reshape(-1) for n in REPLICATED])
    rep = jnp.pad(rep, (0, (SMALL_ROWS - SMALL_REP) * SIDE_W - rep.shape[0])).reshape(SMALL_ROWS - SMALL_REP, SIDE_W)
    g_meta = grads["meta_tokens"].reshape(N_META, 4, SIDE_W).transpose(1, 0, 2)
    g_conv = grads["conv_w"].reshape(N_ODD * 3, 4, SIDE_W).transpose(1, 0, 2)
    small4 = jnp.concatenate([g_meta, g_conv, jnp.zeros((4, SMALL_REP - SMALL_CONV - N_ODD * 3, SIDE_W), F32),
                              jnp.broadcast_to(rep[None], (4,) + rep.shape)], axis=1)
    g = _put(g, small4, PK_SMALL, name="pack_small_grads")
    half = PK_ROWS // 2
    c = lax.axis_index("c")
    got = _swap_halves(g)
    kept = lax.dynamic_slice_in_dim(g, c * half, half, axis=1)
    pair = _add2(kept.reshape(4 * half, PACK_W), got.reshape(4 * half, PACK_W), out_dtype=jnp.bfloat16,
                 name="grad_pair_sum").reshape(4, half, PACK_W)
    total_half = _sum4(_scatter_chips(pair), name="grad_chip_sum")
    g_tot = _join_halves(total_half).reshape(PK_ROWS, PACK_W)

    out = {}
    for name, place in MATRIX_PLACES:
        shape = weights[name].shape
        two_d = lambda a: a.reshape(-1, shape[-1])
        res = _adamw(two_d(weights[name]), two_d(mom_m[name]), two_d(mom_v[name]), g_tot, place[0], place[1])
        out[name] = [r.reshape(shape) for r in res]

    def small_pack(src):
        flat = jnp.concatenate([src[n].reshape(-1) for n in REPLICATED])
        flat = jnp.pad(flat, (0, (SMALL_ROWS - SMALL_REP) * SIDE_W - flat.shape[0])).reshape(SMALL_ROWS - SMALL_REP, SIDE_W)
        return jnp.concatenate([src["meta_tokens"], src["conv_w"].reshape(N_ODD * 3, SIDE_W),
                                jnp.zeros((SMALL_REP - SMALL_CONV - N_ODD * 3, SIDE_W), F32), flat])

    res = _adamw(small_pack(weights), small_pack(mom_m), small_pack(mom_v), g_tot, PK_SMALL[0], PK_SMALL[1])
    for name in ("meta_tokens", "conv_w") + REPLICATED:
        out[name] = []
    for r in res:
        out["meta_tokens"].append(r[SMALL_META:SMALL_META + N_META])
        out["conv_w"].append(r[SMALL_CONV:SMALL_CONV + N_ODD * 3].reshape(N_ODD, 3, SIDE_W))
        flat, off = r[SMALL_REP:].reshape(-1), 0
        for name in REPLICATED:
            n = weights[name].size
            out[name].append(flat[off:off + n].reshape(weights[name].shape))
            off += n
    return (loss, grad_x[None], *[out[n][0] for n in WEIGHT_ORDER], *[out[n][1] for n in WEIGHT_ORDER],
            *[out[n][2] for n in WEIGHT_ORDER], *[out[n][3] for n in WEIGHT_ORDER])
```

```python
import functools

import jax
import jax.numpy as jnp
from jax import lax
from jax.experimental import pallas as pl
from jax.experimental.pallas import tpu as pltpu

F32 = jnp.float32
MXU_DTYPE = jnp.bfloat16

D_MODEL = 1024
N_META = 16
LANES = 128
PAD = LANES - N_META
HEADS = 8
Q_LORA = 384
KV_LORA = 256
MLA_NOPE = 64
MLA_ROPE = 32
MLA_QK = MLA_NOPE + MLA_ROPE
HEAD_V = 64
FOX_DIM = 64
ROPE_BASE = 10000.0
D_FF = 4 * D_MODEL
DEPTH = 4
EPS = 1e-6
NEG = -1e30
ATTN_IN = Q_LORA + KV_LORA + MLA_ROPE + 3 * HEADS * FOX_DIM + HEADS

ZA_CQ = 0
ZA_CKV = Q_LORA
ZA_FQ = ZA_CKV + KV_LORA
ZA_FK = ZA_FQ + HEADS * FOX_DIM
ZA_FV = ZA_FK + HEADS * FOX_DIM
ZA_LAST = ZA_FV + HEADS * FOX_DIM
ZA_W = ZA_LAST + LANES
KPE_LANE = MLA_NOPE

ADAM_LR = 0.001
ADAM_B1 = 0.9
ADAM_B2 = 0.999
ADAM_EPS = 1e-08
ADAM_WD = 0.01
ADAM_STEP = 10

VMEM_LIMIT_BYTES = 52 * 1024 * 1024
ROW_TILE = 640
PREP_TILE = 320
ATTN_BLOCK = 640
LOSS_TILE = 128
MAX_TILE = 1536

MESH_AXES = ("x", "y", "c")
MESH = pl.DeviceIdType.MESH


def _params(*sem):
    return pltpu.CompilerParams(dimension_semantics=sem, vmem_limit_bytes=VMEM_LIMIT_BYTES)


def _tile(n, cap=None):
    cap = MAX_TILE if cap is None else cap
    if n <= cap:
        return n
    best = None
    for t in range(LANES, cap + 1, LANES):
        if n % t == 0:
            best = t
    assert best is not None, n
    return best


def _iota(shape, dim):
    return lax.broadcasted_iota(jnp.int32, shape, dim)


def _matmul(a, b, *, ta=False, tb=False, extras=(), epilogue=None, out_dtypes=(F32,), name, b_tiles=None,
            out_into=None, tm=None, tn=None):
    if ta:
        kdim, m = a.shape
    else:
        m, kdim = a.shape
    row_tile = min(ROW_TILE, m)
    if ta:
        tm_auto, tk = _tile(m), min(ROW_TILE, kdim)
    else:
        tm_auto, tk = (row_tile if m % row_tile == 0 else _tile(m)), _tile(kdim)
    tm = tm_auto if tm is None else tm
    if b_tiles is None:
        n = b.shape[0] if tb else b.shape[1]
        assert (b.shape[1] if tb else b.shape[0]) == kdim, (a.shape, b.shape, ta, tb)
        tn = _tile(n) if tn is None else tn
        b_spec = pl.BlockSpec((tn, tk), lambda i, j, k: (j, k)) if tb else pl.BlockSpec((tk, tn), lambda i, j, k: (k, j))
    else:
        n, tn, tk, b_spec = b_tiles["n"], b_tiles["tn"], b_tiles["tk"], b_tiles["spec"]
    nm, nn, nk = m // tm, n // tn, kdim // tk
    assert nm * tm == m and nn * tn == n and nk * tk == kdim, (m, n, kdim, tm, tn, tk)
    n_ex, n_out = len(extras), len(out_dtypes)
    n_alias = 0 if out_into is None else 1
    assert n_out == 1 or out_into is None
    dims = (((0 if ta else 1,), (1 if tb else 0,)), ((), ()))
    if epilogue is None:
        epilogue = lambda acc: (acc,)

    def body(a_ref, b_ref, *rest):
        ex_refs, out_refs, acc_ref = rest[:n_ex], rest[n_ex + n_alias:n_ex + n_alias + n_out], rest[-1]
        k = pl.program_id(2)

        @pl.when(k == 0)
        def _():
            acc_ref[...] = jnp.zeros_like(acc_ref)

        acc_ref[...] += lax.dot_general(a_ref[...].astype(MXU_DTYPE), b_ref[...].astype(MXU_DTYPE), dims,
                                        preferred_element_type=F32)

        @pl.when(k == nk - 1)
        def _():
            res = epilogue(acc_ref[...], *[e[...] for e in ex_refs])
            for o_ref, r in zip(out_refs, res):
                o_ref[...] = r.astype(o_ref.dtype)

    a_spec = pl.BlockSpec((tk, tm), lambda i, j, k: (k, i)) if ta else pl.BlockSpec((tm, tk), lambda i, j, k: (i, k))
    mn_spec = pl.BlockSpec((tm, tn), lambda i, j, k: (i, j))
    if out_into is None:
        outs = pl.pallas_call(
            body, name=name, grid=(nm, nn, nk),
            in_specs=[a_spec, b_spec] + [mn_spec] * n_ex,
            out_specs=[mn_spec] * n_out,
            out_shape=[jax.ShapeDtypeStruct((m, n), dt) for dt in out_dtypes],
            scratch_shapes=[pltpu.VMEM((tm, tn), F32)],
            compiler_params=_params("parallel", "parallel", "arbitrary"),
        )(a, b, *extras)
        return outs[0] if n_out == 1 else outs
    buf, buf_spec = out_into
    return pl.pallas_call(
        body, name=name, grid=(nm, nn, nk),
        in_specs=[a_spec, b_spec] + [mn_spec] * n_ex + [ANY],
        out_specs=buf_spec,
        out_shape=jax.ShapeDtypeStruct(buf.shape, buf.dtype),
        input_output_aliases={2 + n_ex: 0},
        scratch_shapes=[pltpu.VMEM((tm, tn), F32)],
        compiler_params=_params("parallel", "parallel", "arbitrary"),
    )(a, b, *extras, buf)


def _rmsnorm_fwd(x, g, *, name="rmsnorm_fwd"):
    t, d = x.shape
    bt = min(ROW_TILE, t)

    def body(x_ref, g_ref, o_ref):
        xv = x_ref[...]
        r = lax.rsqrt(jnp.mean(xv * xv, axis=-1, keepdims=True) + EPS)
        o_ref[...] = (xv * r * g_ref[...]).astype(o_ref.dtype)

    return pl.pallas_call(
        body, name=name, grid=(t // bt,),
        in_specs=[pl.BlockSpec((bt, d), lambda i: (i, 0)), pl.BlockSpec((1, d), lambda i: (0, 0))],
        out_specs=pl.BlockSpec((bt, d), lambda i: (i, 0)),
        out_shape=jax.ShapeDtypeStruct((t, d), MXU_DTYPE),
        compiler_params=_params("parallel"),
    )(x, g.reshape(1, d))


def _rmsnorm_bwd(x, g, dy, dres, *, name="rmsnorm_bwd"):
    t, d = x.shape
    bt = min(ROW_TILE, t)

    def body(x_ref, g_ref, dy_ref, dres_ref, dx_ref, dg_ref):
        @pl.when(pl.program_id(0) == 0)
        def _():
            dg_ref[...] = jnp.zeros_like(dg_ref)

        xv, dyv = x_ref[...], dy_ref[...].astype(F32)
        r = lax.rsqrt(jnp.mean(xv * xv, axis=-1, keepdims=True) + EPS)
        xhat = xv * r
        dxhat = dyv * g_ref[...]
        dx = r * (dxhat - xhat * jnp.mean(dxhat * xhat, axis=-1, keepdims=True))
        dx_ref[...] = dres_ref[...] + dx
        dg_ref[...] += jnp.sum(dyv * xhat, axis=0, keepdims=True)

    row = pl.BlockSpec((bt, d), lambda i: (i, 0))
    vec = pl.BlockSpec((1, d), lambda i: (0, 0))
    dx, dg = pl.pallas_call(
        body, name=name, grid=(t // bt,),
        in_specs=[row, vec, row, row], out_specs=[row, vec],
        out_shape=[jax.ShapeDtypeStruct((t, d), F32), jax.ShapeDtypeStruct((1, d), F32)],
        compiler_params=_params("arbitrary"),
    )(x, g.reshape(1, d), dy, dres)
    return dx, dg.reshape(d)


def _pair_rms(x, lo):
    x2 = x * x
    s_lo = jnp.sum(jnp.where(lo, x2, 0.0), axis=-1, keepdims=True)
    s_hi = jnp.sum(jnp.where(lo, 0.0, x2), axis=-1, keepdims=True)
    return jnp.where(lo, lax.rsqrt(s_lo / FOX_DIM + EPS), lax.rsqrt(s_hi / FOX_DIM + EPS))


def _pair_sum(x, lo):
    s_lo = jnp.sum(jnp.where(lo, x, 0.0), axis=-1, keepdims=True)
    s_hi = jnp.sum(jnp.where(lo, 0.0, x), axis=-1, keepdims=True)
    return jnp.where(lo, s_lo, s_hi)


def _prep_a_fwd(z, g_cq, g_ckv, g_qf, g_kf, b_f):
    t = z.shape[0]
    bt = min(PREP_TILE, t)
    hw = HEADS * FOX_DIM

    def body(z_ref, gcq_ref, gckv_ref, gqf_ref, gkf_ref, bf_ref, cqn_ref, ckvn_ref, qf_ref, kf_ref, vf_ref, logf_ref):
        i = pl.program_id(0)
        cq = z_ref[:, ZA_CQ:ZA_CQ + Q_LORA]
        cqn_ref[...] = (cq * lax.rsqrt(jnp.mean(cq * cq, axis=-1, keepdims=True) + EPS) * gcq_ref[...]).astype(cqn_ref.dtype)
        ckv = z_ref[:, ZA_CKV:ZA_CKV + KV_LORA]
        ckvn_ref[...] = (ckv * lax.rsqrt(jnp.mean(ckv * ckv, axis=-1, keepdims=True) + EPS) * gckv_ref[...]).astype(ckvn_ref.dtype)
        lo = _iota((1, LANES), 1) < FOX_DIM
        for p in range(HEADS // 2):
            sl = slice(p * LANES, (p + 1) * LANES)
            xq = z_ref[:, ZA_FQ + p * LANES:ZA_FQ + (p + 1) * LANES]
            qf_ref[:, sl] = (xq * _pair_rms(xq, lo) * gqf_ref[...]).astype(qf_ref.dtype)
            xk = z_ref[:, ZA_FK + p * LANES:ZA_FK + (p + 1) * LANES]
            kf_ref[:, sl] = (xk * _pair_rms(xk, lo) * gkf_ref[...]).astype(kf_ref.dtype)
        vf_ref[...] = z_ref[:, ZA_FV:ZA_FV + hw].astype(vf_ref.dtype)
        xl = z_ref[:, ZA_LAST:ZA_LAST + LANES] + bf_ref[...]
        logf = jnp.minimum(xl, 0.0) - jnp.log(1.0 + jnp.exp(-jnp.abs(xl)))
        row = i * bt + _iota((bt, LANES), 0)
        lane = _iota((bt, LANES), 1)
        logf_ref[...] = jnp.where((lane < HEADS) & (row >= PAD), logf, 0.0)

    def vec(w):
        return pl.BlockSpec((1, w), lambda i: (0, 0))

    def row(w):
        return pl.BlockSpec((bt, w), lambda i: (i, 0))

    return pl.pallas_call(
        body, name="attn_prep_a_fwd", grid=(t // bt,),
        in_specs=[row(ZA_W), vec(Q_LORA), vec(KV_LORA), vec(LANES), vec(LANES), vec(LANES)],
        out_specs=[row(Q_LORA), row(KV_LORA), row(hw), row(hw), row(hw), row(LANES)],
        out_shape=[jax.ShapeDtypeStruct((t, Q_LORA), MXU_DTYPE), jax.ShapeDtypeStruct((t, KV_LORA), MXU_DTYPE),
                   jax.ShapeDtypeStruct((t, hw), MXU_DTYPE), jax.ShapeDtypeStruct((t, hw), MXU_DTYPE),
                   jax.ShapeDtypeStruct((t, hw), MXU_DTYPE), jax.ShapeDtypeStruct((t, LANES), F32)],
        compiler_params=_params("parallel"),
    )(z, g_cq, g_ckv, g_qf, g_kf, b_f)


def _prep_a_bwd(z, g_cq, g_ckv, g_qf, g_kf, b_f, dcqn, dckvn, dqf, dkf, dvf, dlogf, dkpe):
    t = z.shape[0]
    bt = min(PREP_TILE, t)
    hw = HEADS * FOX_DIM

    def norm_bwd(x, g, dy):
        r = lax.rsqrt(jnp.mean(x * x, axis=-1, keepdims=True) + EPS)
        xhat = x * r
        dxhat = dy * g
        dx = r * (dxhat - xhat * jnp.mean(dxhat * xhat, axis=-1, keepdims=True))
        return dx, jnp.sum(dy * xhat, axis=0, keepdims=True)

    def body(z_ref, gcq_ref, gckv_ref, gqf_ref, gkf_ref, bf_ref, dcqn_ref, dckvn_ref, dqf_ref, dkf_ref, dvf_ref,
             dlogf_ref, dkpe_ref, dz_ref, dgcq_ref, dgckv_ref, dgqf_ref, dgkf_ref, dbf_ref):
        i = pl.program_id(0)

        @pl.when(i == 0)
        def _():
            for r in (dgcq_ref, dgckv_ref, dgqf_ref, dgkf_ref, dbf_ref):
                r[...] = jnp.zeros_like(r)

        dx, dg = norm_bwd(z_ref[:, ZA_CQ:ZA_CQ + Q_LORA], gcq_ref[...], dcqn_ref[...])
        dz_ref[:, ZA_CQ:ZA_CQ + Q_LORA] = dx.astype(dz_ref.dtype)
        dgcq_ref[...] += dg
        dx, dg = norm_bwd(z_ref[:, ZA_CKV:ZA_CKV + KV_LORA], gckv_ref[...], dckvn_ref[...])
        dz_ref[:, ZA_CKV:ZA_CKV + KV_LORA] = dx.astype(dz_ref.dtype)
        dgckv_ref[...] += dg
        lo = _iota((1, LANES), 1) < FOX_DIM
        for base, g_ref, dy_ref, dg_ref in ((ZA_FQ, gqf_ref, dqf_ref, dgqf_ref), (ZA_FK, gkf_ref, dkf_ref, dgkf_ref)):
            for p in range(HEADS // 2):
                x = z_ref[:, base + p * LANES:base + (p + 1) * LANES]
                dy = dy_ref[:, p * LANES:(p + 1) * LANES]
                r = _pair_rms(x, lo)
                xhat = x * r
                dxhat = dy * g_ref[...]
                dx = r * (dxhat - xhat * _pair_sum(dxhat * xhat, lo) / FOX_DIM)
                dz_ref[:, base + p * LANES:base + (p + 1) * LANES] = dx.astype(dz_ref.dtype)
                dg_ref[...] += jnp.sum(dy * xhat, axis=0, keepdims=True)
        dz_ref[:, ZA_FV:ZA_FV + hw] = dvf_ref[...].astype(dz_ref.dtype)
        xl = z_ref[:, ZA_LAST:ZA_LAST + LANES] + bf_ref[...]
        row = i * bt + _iota((bt, LANES), 0)
        lane = _iota((bt, LANES), 1)
        dfl = jnp.where((lane < HEADS) & (row >= PAD), dlogf_ref[...] / (1.0 + jnp.exp(xl)), 0.0)
        dbf_ref[...] += jnp.sum(dfl, axis=0, keepdims=True)
        dz_ref[:, ZA_LAST:ZA_LAST + LANES] = (dfl + dkpe_ref[...]).astype(dz_ref.dtype)

    def vec(w):
        return pl.BlockSpec((1, w), lambda i: (0, 0))

    def row(w):
        return pl.BlockSpec((bt, w), lambda i: (i, 0))

    return pl.pallas_call(
        body, name="attn_prep_a_bwd", grid=(t // bt,),
        in_specs=[row(ZA_W), vec(Q_LORA), vec(KV_LORA), vec(LANES), vec(LANES), vec(LANES),
                  row(Q_LORA), row(KV_LORA), row(hw), row(hw), row(hw), row(LANES), row(LANES)],
        out_specs=[row(ZA_W), vec(Q_LORA), vec(KV_LORA), vec(LANES), vec(LANES), vec(LANES)],
        out_shape=[jax.ShapeDtypeStruct((t, ZA_W), MXU_DTYPE), jax.ShapeDtypeStruct((1, Q_LORA), F32),
                   jax.ShapeDtypeStruct((1, KV_LORA), F32), jax.ShapeDtypeStruct((1, LANES), F32),
                   jax.ShapeDtypeStruct((1, LANES), F32), jax.ShapeDtypeStruct((1, LANES), F32)],
        compiler_params=_params("arbitrary"),
    )(z, g_cq, g_ckv, g_qf, g_kf, b_f, dcqn, dckvn, dqf, dkf, dvf, dlogf, dkpe)


def _cumsum_rows(x, *, reverse, name):
    t = x.shape[0]
    nblk = t // LANES

    def body(x_ref, f_ref, ft_ref, carry_ref):
        r = _iota((LANES, LANES), 0)
        c = _iota((LANES, LANES), 1)
        tri = jnp.where((c >= r) if reverse else (c <= r), 1.0, 0.0).astype(F32)
        carry_ref[...] = jnp.zeros_like(carry_ref)

        def step(s, _):
            b = (nblk - 1 - s) if reverse else s
            start = pl.multiple_of(b * LANES, LANES)
            blk = x_ref[pl.ds(start, LANES), :]
            cs = jnp.dot(tri, blk, precision=lax.Precision.HIGHEST, preferred_element_type=F32) + carry_ref[0:1, :]
            f_ref[pl.ds(start, LANES), :] = cs
            ft_ref[:, pl.ds(start, LANES)] = cs.T
            carry_ref[0:1, :] = cs[0:1, :] if reverse else cs[LANES - 1:LANES, :]
            return 0

        lax.fori_loop(0, nblk, step, 0)

    return pl.pallas_call(
        body, name=name, grid=(1,),
        in_specs=[pl.BlockSpec((t, LANES), lambda i: (0, 0))],
        out_specs=[pl.BlockSpec((t, LANES), lambda i: (0, 0)), pl.BlockSpec((LANES, t), lambda i: (0, 0))],
        out_shape=[jax.ShapeDtypeStruct((t, LANES), F32), jax.ShapeDtypeStruct((LANES, t), F32)],
        scratch_shapes=[pltpu.VMEM((8, LANES), F32)],
        compiler_params=_params("arbitrary"),
    )(x)


def _rope_partner(x, lane):
    half = MLA_ROPE // 2
    swapped = jnp.where(lane < KPE_LANE + half, pltpu.roll(x, LANES - half, axis=1), pltpu.roll(x, half, axis=1))
    return jnp.where((lane >= KPE_LANE) & (lane < KPE_LANE + MLA_ROPE), swapped, 0.0)


def _rope_tables(t):
    pos = (jnp.arange(t, dtype=jnp.int32) - PAD).astype(F32)
    inv_freq = ROPE_BASE ** (-jnp.arange(0, MLA_ROPE, 2, dtype=F32) / MLA_ROPE)
    ang = pos[:, None] * inv_freq[None, :]
    cos, sin = jnp.cos(ang), jnp.sin(ang)
    ones = jnp.ones((t, KPE_LANE), F32)
    tail = jnp.zeros((t, LANES - KPE_LANE - MLA_ROPE), F32)
    c_tab = jnp.concatenate([ones, cos, cos, tail + 1.0], axis=1)
    s_tab = jnp.concatenate([ones * 0.0, -sin, sin, tail], axis=1)
    return c_tab, s_tab


def _prep_b_fwd(q_raw, kv_raw, z, c_tab, s_tab, g_q, g_k):
    t = q_raw.shape[0]
    bt = min(PREP_TILE, t)
    qw = HEADS * LANES
    vw = HEADS * HEAD_V

    def body(q_ref, kv_ref, zl_ref, c_ref, s_ref, gq_ref, gk_ref, qn_ref, kn_ref, v_ref):
        lane = _iota((1, LANES), 1)
        kpe = jnp.where((lane >= KPE_LANE) & (lane < KPE_LANE + MLA_ROPE), zl_ref[...], 0.0)
        cv, sv = c_ref[...], s_ref[...]
        for h in range(HEADS):
            sl = slice(h * LANES, (h + 1) * LANES)
            for x, g_ref, o_ref in ((q_ref[:, sl], gq_ref, qn_ref), (kv_ref[:, sl] + kpe, gk_ref, kn_ref)):
                r = lax.rsqrt(jnp.sum(x * x, axis=-1, keepdims=True) / MLA_QK + EPS)
                xn = x * r * g_ref[...]
                o_ref[:, sl] = (xn * cv + _rope_partner(xn, lane) * sv).astype(o_ref.dtype)
        v_ref[...] = kv_ref[:, qw:qw + vw].astype(v_ref.dtype)

    def row(w):
        return pl.BlockSpec((bt, w), lambda i: (i, 0))

    vec = pl.BlockSpec((1, LANES), lambda i: (0, 0))
    return pl.pallas_call(
        body, name="attn_prep_b_fwd", grid=(t // bt,),
        in_specs=[row(qw), row(qw + vw), pl.BlockSpec((bt, LANES), lambda i: (i, ZA_LAST // LANES)),
                  row(LANES), row(LANES), vec, vec],
        out_specs=[row(qw), row(qw), row(vw)],
        out_shape=[jax.ShapeDtypeStruct((t, qw), MXU_DTYPE), jax.ShapeDtypeStruct((t, qw), MXU_DTYPE),
                   jax.ShapeDtypeStruct((t, vw), MXU_DTYPE)],
        compiler_params=_params("parallel"),
    )(q_raw, kv_raw, z, c_tab, s_tab, g_q, g_k)


def _prep_b_bwd(q_raw, kv_raw, z, c_tab, s_tab, g_q, g_k, dqn, dkn, dv):
    t = q_raw.shape[0]
    bt = min(PREP_TILE, t)
    qw = HEADS * LANES
    vw = HEADS * HEAD_V

    def body(q_ref, kv_ref, zl_ref, c_ref, s_ref, gq_ref, gk_ref, dqn_ref, dkn_ref, dv_ref,
             dq_ref, dkv_ref, dkpe_ref, dgq_ref, dgk_ref):
        @pl.when(pl.program_id(0) == 0)
        def _():
            dgq_ref[...] = jnp.zeros_like(dgq_ref)
            dgk_ref[...] = jnp.zeros_like(dgk_ref)

        lane = _iota((1, LANES), 1)
        rope_lanes = (lane >= KPE_LANE) & (lane < KPE_LANE + MLA_ROPE)
        kpe = jnp.where(rope_lanes, zl_ref[...], 0.0)
        cv, sv = c_ref[...], s_ref[...]
        dkpe = jnp.zeros((bt, LANES), F32)
        for h in range(HEADS):
            sl = slice(h * LANES, (h + 1) * LANES)
            for is_k, x, g_ref, dout, dg_ref in ((False, q_ref[:, sl], gq_ref, dqn_ref[:, sl], dgq_ref),
                                                  (True, kv_ref[:, sl] + kpe, gk_ref, dkn_ref[:, sl], dgk_ref)):
                r = lax.rsqrt(jnp.sum(x * x, axis=-1, keepdims=True) / MLA_QK + EPS)
                xhat = x * r
                dxn = dout * cv + _rope_partner(dout * sv, lane)
                dg_ref[...] += jnp.sum(dxn * xhat, axis=0, keepdims=True)
                dxhat = dxn * g_ref[...]
                dx = r * (dxhat - xhat * (jnp.sum(dxhat * xhat, axis=-1, keepdims=True) / MLA_QK))
                if is_k:
                    dkv_ref[:, sl] = jnp.where(lane < KPE_LANE, dx, 0.0).astype(dkv_ref.dtype)
                    dkpe = dkpe + jnp.where(rope_lanes, dx, 0.0)
                else:
                    dq_ref[:, sl] = dx.astype(dq_ref.dtype)
        dkv_ref[:, qw:qw + vw] = dv_ref[...].astype(dkv_ref.dtype)
        dkpe_ref[...] = dkpe

    def row(w):
        return pl.BlockSpec((bt, w), lambda i: (i, 0))

    vec = pl.BlockSpec((1, LANES), lambda i: (0, 0))
    return pl.pallas_call(
        body, name="attn_prep_b_bwd", grid=(t // bt,),
        in_specs=[row(qw), row(qw + vw), pl.BlockSpec((bt, LANES), lambda i: (i, ZA_LAST // LANES)),
                  row(LANES), row(LANES), vec, vec, row(qw), row(qw), row(vw)],
        out_specs=[row(qw), row(qw + vw), row(LANES), vec, vec],
        out_shape=[jax.ShapeDtypeStruct((t, qw), MXU_DTYPE), jax.ShapeDtypeStruct((t, qw + vw), MXU_DTYPE),
                   jax.ShapeDtypeStruct((t, LANES), F32), jax.ShapeDtypeStruct((1, LANES), F32),
                   jax.ShapeDtypeStruct((1, LANES), F32)],
        compiler_params=_params("arbitrary"),
    )(q_raw, kv_raw, z, c_tab, s_tab, g_q, g_k, dqn, dkn, dv)


NT_DIMS = (((1,), (1,)), ((), ()))
TN_DIMS = (((0,), (0,)), ((), ()))


def _head_qk(q_ref, k_ref, e, mla, lo):
    if mla:
        return q_ref[:, e * LANES:(e + 1) * LANES], k_ref[:, e * LANES:(e + 1) * LANES]
    q = q_ref[...]
    return jnp.where(lo if e == 0 else jnp.logical_not(lo), q, jnp.zeros_like(q)), k_ref[...]


def _attn_specs(mla, blk, q_map, k_map):
    w = 2 * LANES if mla else LANES
    q_spec = pl.BlockSpec((blk, w), lambda j, a, b: (q_map(a, b), j))
    k_spec = pl.BlockSpec((blk, w), lambda j, a, b: (k_map(a, b), j))
    qv_spec = pl.BlockSpec((blk, LANES), lambda j, a, b: (q_map(a, b), j))
    kv_spec = pl.BlockSpec((blk, LANES), lambda j, a, b: (k_map(a, b), j))
    fq_spec = pl.BlockSpec((blk, LANES), lambda j, a, b: (q_map(a, b), 0))
    fk_spec = pl.BlockSpec((8, blk), lambda j, a, b: (0, k_map(a, b)))
    return q_spec, k_spec, qv_spec, kv_spec, fq_spec, fk_spec


def _flash_fwd(q, k, v, f, f_t, *, mla, scale, name):
    t = q.shape[0]
    blk = min(ATTN_BLOCK, t)
    nb = t // blk
    pairs = HEADS // 2
    q_spec, k_spec, qv_spec, kv_spec, fq_spec, fk_spec = _attn_specs(
        mla, blk, lambda i, kk: i, lambda i, kk: jnp.minimum(kk, i))

    def body(*refs):
        if mla:
            q_ref, k_ref, v_ref, o_ref, lse_ref, m_s, l_s, acc_s = refs
            fq_ref = fk_ref = None
        else:
            q_ref, k_ref, v_ref, fq_ref, fk_ref, o_ref, lse_ref, m_s, l_s, acc_s = refs
        j, i, kk = pl.program_id(0), pl.program_id(1), pl.program_id(2)
        lo = _iota((1, LANES), 1) < HEAD_V

        @pl.when(kk == 0)
        def _():
            m_s[...] = jnp.full_like(m_s, NEG)
            l_s[...] = jnp.zeros_like(l_s)
            acc_s[...] = jnp.zeros_like(acc_s)

        def step(mask):
            vv = v_ref[...]
            for e in range(2):
                s, _, _ = _scores(q_ref, k_ref, fq_ref, fk_ref, e, j, mla, scale, lo, mask)
                m_prev = m_s[e]
                m_new = jnp.maximum(m_prev, jnp.max(s, axis=1, keepdims=True))
                alpha = jnp.exp(m_prev - m_new)
                p = jnp.exp(s - m_new)
                if mask is not None:
                    p = jnp.where(mask, p, 0.0)
                l_s[e] = alpha * l_s[e] + jnp.sum(p, axis=1, keepdims=True)
                acc_s[e] = alpha * acc_s[e] + jnp.dot(p.astype(MXU_DTYPE), vv, preferred_element_type=F32)
                m_s[e] = m_new

        _masked_and_plain(kk <= i, i, kk, blk, step)

        @pl.when(kk == nb - 1)
        def _():
            valid = (i * blk + _iota((blk, 1), 0)) >= PAD
            outs, lses = [], []
            for e in range(2):
                l = l_s[e]
                outs.append(acc_s[e] * jnp.where(l > 0.0, 1.0 / jnp.where(l > 0.0, l, 1.0), 0.0))
                lses.append(m_s[e] + jnp.log(jnp.where(l > 0.0, l, 1.0)))
            o = jnp.where(lo, outs[0], outs[1])
            o_ref[...] = jnp.where(valid, o, 0.0).astype(o_ref.dtype)
            lane = _iota((1, LANES), 1)
            lse_ref[...] = jnp.where(lane == 0, lses[0], jnp.where(lane == 1, lses[1], 0.0))

    in_specs = [q_spec, k_spec, kv_spec] + ([] if mla else [fq_spec, fk_spec])
    args = (q, k, v) + (() if mla else (f, f_t))
    hv = HEADS * HEAD_V
    return pl.pallas_call(
        body, name=name, grid=(pairs, nb, nb),
        in_specs=in_specs, out_specs=[qv_spec, qv_spec],
        out_shape=[jax.ShapeDtypeStruct((t, hv), F32), jax.ShapeDtypeStruct((t, hv), F32)],
        scratch_shapes=[pltpu.VMEM((2, blk, 1), F32), pltpu.VMEM((2, blk, 1), F32), pltpu.VMEM((2, blk, LANES), F32)],
        compiler_params=_params("parallel", "parallel", "arbitrary"),
    )(*args)


def _bwd_tile(q_ref, k_ref, v_ref, o_ref, do_ref, lse_ref, fq_ref, fk_ref, e, pair, mla, scale, lo, mask):
    s, qe, ke = _scores(q_ref, k_ref, fq_ref, fk_ref, e, pair, mla, scale, lo, mask)
    p = jnp.exp(s - lse_ref[:, e:e + 1])
    if mask is not None:
        p = jnp.where(mask, p, 0.0)
    do = do_ref[...]
    doe = jnp.where(lo if e == 0 else jnp.logical_not(lo), do, jnp.zeros_like(do))
    dp = lax.dot_general(doe, v_ref[...], NT_DIMS, preferred_element_type=F32)
    delta = jnp.sum(doe.astype(F32) * o_ref[...].astype(F32), axis=1, keepdims=True)
    return p, p * (dp - delta), qe, ke


def _flash_bwd_dq(q, k, v, o, do, lse, f, f_t, *, mla, scale, col0, name):
    t = q.shape[0]
    blk = min(ATTN_BLOCK, t)
    nb = t // blk
    pairs = HEADS // 2
    w = 2 * LANES if mla else LANES
    q_spec, k_spec, qv_spec, kv_spec, fq_spec, fk_spec = _attn_specs(
        mla, blk, lambda i, kk: i, lambda i, kk: jnp.minimum(kk, i))
    od_spec = pl.BlockSpec((blk, LANES), lambda j, i, kk: (i, col0 + j))

    def body(*refs):
        if mla:
            q_ref, k_ref, v_ref, o_ref, do_ref, lse_ref, dq_ref, dq_s = refs
            fq_ref = fk_ref = rs_ref = rs_s = None
        else:
            q_ref, k_ref, v_ref, o_ref, do_ref, lse_ref, fq_ref, fk_ref, dq_ref, rs_ref, dq_s, rs_s = refs
        j, i, kk = pl.program_id(0), pl.program_id(1), pl.program_id(2)
        lo = _iota((1, LANES), 1) < HEAD_V

        @pl.when(kk == 0)
        def _():
            dq_s[...] = jnp.zeros_like(dq_s)
            if not mla:
                rs_s[...] = jnp.zeros_like(rs_s)

        def step(mask):
            for e in range(2):
                _, ds, _, ke = _bwd_tile(q_ref, k_ref, v_ref, o_ref, do_ref, lse_ref, fq_ref, fk_ref, e, j, mla, scale,
                                         lo, mask)
                dq_s[e] += jnp.dot(ds.astype(MXU_DTYPE), ke, preferred_element_type=F32)
                if not mla:
                    rs_s[e] += jnp.sum(ds, axis=1, keepdims=True)

        _masked_and_plain(kk <= i, i, kk, blk, step)

        @pl.when(kk == nb - 1)
        def _():
            if mla:
                dq_ref[:, 0:LANES] = dq_s[0] * scale
                dq_ref[:, LANES:2 * LANES] = dq_s[1] * scale
            else:
                dq_ref[...] = jnp.where(lo, dq_s[0], dq_s[1]) * scale
                lane = _iota((1, LANES), 1)
                rs_ref[...] = jnp.where(lane == 0, rs_s[0], jnp.where(lane == 1, rs_s[1], 0.0))

    in_specs = [q_spec, k_spec, kv_spec, od_spec, od_spec, qv_spec] + ([] if mla else [fq_spec, fk_spec])
    args = (q, k, v, o, do, lse) + (() if mla else (f, f_t))
    out_specs = [q_spec] + ([] if mla else [qv_spec])
    out_shape = [jax.ShapeDtypeStruct((t, pairs * w), F32)]
    scratch = [pltpu.VMEM((2, blk, LANES), F32)]
    if not mla:
        out_shape.append(jax.ShapeDtypeStruct((t, pairs * LANES), F32))
        scratch.append(pltpu.VMEM((2, blk, 1), F32))
    outs = pl.pallas_call(
        body, name=name, grid=(pairs, nb, nb),
        in_specs=in_specs, out_specs=out_specs, out_shape=out_shape, scratch_shapes=scratch,
        compiler_params=_params("parallel", "parallel", "arbitrary"),
    )(*args)
    return outs[0] if mla else outs


def _flash_bwd_dkv(q, k, v, o, do, lse, f, f_t, *, mla, scale, col0, name):
    t = q.shape[0]
    blk = min(ATTN_BLOCK, t)
    nb = t // blk
    pairs = HEADS // 2
    w = 2 * LANES if mla else LANES
    q_spec, k_spec, qv_spec, kv_spec, fq_spec, fk_spec = _attn_specs(
        mla, blk, lambda a, b: jnp.maximum(a, b), lambda a, b: a)
    od_spec = pl.BlockSpec((blk, LANES), lambda j, a, b: (jnp.maximum(a, b), col0 + j))
    cs_spec = pl.BlockSpec((8, blk), lambda j, a, b: (j, a))

    def body(*refs):
        if mla:
            q_ref, k_ref, v_ref, o_ref, do_ref, lse_ref, dk_ref, dv_ref, dk_s, dv_s = refs
            fq_ref = fk_ref = cs_ref = cs_s = None
        else:
            q_ref, k_ref, v_ref, o_ref, do_ref, lse_ref, fq_ref, fk_ref, dk_ref, dv_ref, cs_ref, dk_s, dv_s, cs_s = refs
        j, kb, qb = pl.program_id(0), pl.program_id(1), pl.program_id(2)
        lo = _iota((1, LANES), 1) < HEAD_V

        @pl.when(qb == 0)
        def _():
            dk_s[...] = jnp.zeros_like(dk_s)
            dv_s[...] = jnp.zeros_like(dv_s)
            if not mla:
                cs_s[...] = jnp.zeros_like(cs_s)

        def step(mask):
            do = do_ref[...]
            for e in range(2):
                p, ds, _, _ = _bwd_tile(q_ref, k_ref, v_ref, o_ref, do_ref, lse_ref, fq_ref, fk_ref, e, j, mla, scale,
                                        lo, mask)
                dv_s[e] += lax.dot_general(p.astype(MXU_DTYPE), do, TN_DIMS, preferred_element_type=F32)
                q_src = q_ref[:, e * LANES:(e + 1) * LANES] if mla else q_ref[...]
                dk_s[e] += lax.dot_general(ds.astype(MXU_DTYPE), q_src, TN_DIMS, preferred_element_type=F32)
                if not mla:
                    cs_s[e] += jnp.sum(ds, axis=0, keepdims=True)

        _masked_and_plain(qb >= kb, qb, kb, blk, step)

        @pl.when(qb == nb - 1)
        def _():
            dv_ref[...] = jnp.where(lo, dv_s[0], dv_s[1])
            if mla:
                dk_ref[:, 0:LANES] = dk_s[0] * scale
                dk_ref[:, LANES:2 * LANES] = dk_s[1] * scale
            else:
                dk_ref[...] = jnp.where(lo, dk_s[0], dk_s[1]) * scale
                sub = _iota((8, 1), 0)
                cs_ref[...] = jnp.where(sub == 0, cs_s[0], jnp.where(sub == 1, cs_s[1], 0.0))

    in_specs = [q_spec, k_spec, kv_spec, od_spec, od_spec, qv_spec] + ([] if mla else [fq_spec, fk_spec])
    args = (q, k, v, o, do, lse) + (() if mla else (f, f_t))
    out_specs = [k_spec, kv_spec] + ([] if mla else [cs_spec])
    out_shape = [jax.ShapeDtypeStruct((t, pairs * w), F32), jax.ShapeDtypeStruct((t, HEADS * HEAD_V), F32)]
    scratch = [pltpu.VMEM((2, blk, LANES), F32), pltpu.VMEM((2, blk, LANES), F32)]
    if not mla:
        out_shape.append(jax.ShapeDtypeStruct((pairs * 8, t), F32))
        scratch.append(pltpu.VMEM((2, 1, blk), F32))
    return pl.pallas_call(
        body, name=name, grid=(pairs, nb, nb),
        in_specs=in_specs, out_specs=out_specs, out_shape=out_shape, scratch_shapes=scratch,
        compiler_params=_params("parallel", "parallel", "arbitrary"),
    )(*args)


ATTN_CHUNK = 640


def _for_chunks(n, ch, body):
    if n == 1:
        body(0)
    else:
        lax.fori_loop(0, n, lambda c, carry: (body(pl.multiple_of(c * ch, ch)), carry)[1], 0)


def _select_lane(x, idx):
    return jnp.sum(jnp.where(_iota(x.shape, 1) == idx, x, 0.0), axis=1, keepdims=True)


def _select_row(x, idx):
    return jnp.sum(jnp.where(_iota(x.shape, 0) == idx, x, 0.0), axis=0, keepdims=True)


def _flash_fwd_chunked(q, k, v, f, f_t, *, mla, scale, name):
    t = q.shape[0]
    blk = min(ATTN_BLOCK, t)
    nb = t // blk
    pairs = HEADS // 2
    ch = min(ATTN_CHUNK, blk)
    assert blk % ch == 0
    q_spec, k_spec, qv_spec, kv_spec, fq_spec, fk_spec = _attn_specs(
        mla, blk, lambda i, kk: i, lambda i, kk: jnp.minimum(kk, i))
    lse_spec = pl.BlockSpec((8, blk), lambda j, i, kk: (j, i))

    def body(*refs):
        if mla:
            q_ref, k_ref, v_ref, o_ref, lse_ref, m_s, l_s, a_s, acc_s, s_s, p_s = refs
            fq_ref = fk_ref = None
        else:
            q_ref, k_ref, v_ref, fq_ref, fk_ref, o_ref, lse_ref, m_s, l_s, a_s, acc_s, s_s, p_s = refs
        j, i, kk = pl.program_id(0), pl.program_id(1), pl.program_id(2)
        lo = _iota((1, LANES), 1) < HEAD_V

        @pl.when(kk == 0)
        def _():
            m_s[...] = jnp.full_like(m_s, NEG)
            l_s[...] = jnp.zeros_like(l_s)
            acc_s[...] = jnp.zeros_like(acc_s)

        def step(masked):
            vv = v_ref[...]
            for e in range(2):
                qe, ke = _head_qk(q_ref, k_ref, e, mla, lo)
                s_s[...] = lax.dot_general(qe, ke, NT_DIMS, preferred_element_type=F32)
                fkr = None if mla else _select_row(fk_ref[...], 2 * j + e)

                def chunk(r0, e=e, fkr=fkr):
                    rows = pl.ds(r0, ch)
                    s = s_s[rows, :] * scale
                    if not mla:
                        s = s + _select_lane(fq_ref[rows, :], 2 * j + e) - fkr
                    if masked:
                        rpos = i * blk + r0 + _iota((ch, blk), 0)
                        cpos = kk * blk + _iota((ch, blk), 1)
                        mask = (cpos <= rpos) & (cpos >= PAD)
                        s = jnp.where(mask, s, NEG)
                    m_prev = m_s[e, rows, :]
                    m_new = jnp.maximum(m_prev, jnp.max(s, axis=1, keepdims=True))
                    alpha = jnp.exp(m_prev - m_new)
                    p = jnp.exp(s - m_new)
                    if masked:
                        p = jnp.where(mask, p, 0.0)
                    l_s[e, rows, :] = alpha * l_s[e, rows, :] + jnp.sum(p, axis=1, keepdims=True)
                    m_s[e, rows, :] = m_new
                    a_s[rows, :] = alpha
                    p_s[rows, :] = p.astype(p_s.dtype)

                _for_chunks(blk // ch, ch, chunk)
                acc_s[e] = a_s[...] * acc_s[e] + jnp.dot(p_s[...], vv, preferred_element_type=F32)

        needs_mask = (kk == i) | (kk == 0)

        @pl.when((kk <= i) & needs_mask)
        def _():
            step(True)

        @pl.when((kk <= i) & jnp.logical_not(needs_mask))
        def _():
            step(False)

        @pl.when(kk == nb - 1)
        def _():
            valid = (i * blk + _iota((blk, 1), 0)) >= PAD
            outs, lses = [], []
            for e in range(2):
                l = l_s[e]
                outs.append(acc_s[e] * jnp.where(l > 0.0, 1.0 / jnp.where(l > 0.0, l, 1.0), 0.0))
                lses.append(m_s[e] + jnp.log(jnp.where(l > 0.0, l, 1.0)))
            o = jnp.where(lo, outs[0], outs[1])
            o_ref[...] = jnp.where(valid, o, 0.0).astype(o_ref.dtype)
            lane = _iota((1, LANES), 1)
            lse_cols = jnp.where(lane == 0, lses[0], jnp.where(lane == 1, lses[1], 0.0))
            lse_ref[...] = lse_cols.T[0:8, :]

    in_specs = [q_spec, k_spec, kv_spec] + ([] if mla else [fq_spec, fk_spec])
    args = (q, k, v) + (() if mla else (f, f_t))
    hv = HEADS * HEAD_V
    return pl.pallas_call(
        body, name=name, grid=(pairs, nb, nb),
        in_specs=in_specs, out_specs=[qv_spec, lse_spec],
        out_shape=[jax.ShapeDtypeStruct((t, hv), F32), jax.ShapeDtypeStruct((pairs * 8, t), F32)],
        scratch_shapes=[pltpu.VMEM((2, blk, 1), F32), pltpu.VMEM((2, blk, 1), F32), pltpu.VMEM((blk, 1), F32),
                        pltpu.VMEM((2, blk, LANES), F32), pltpu.VMEM((blk, blk), F32), pltpu.VMEM((blk, blk), MXU_DTYPE)],
        compiler_params=_params("parallel", "parallel", "arbitrary"),
    )(*args)


def _delta_rows(do, o):
    t, width = o.shape
    bt = min(ROW_TILE, t)
    n_heads = width // HEAD_V

    def body(do_ref, o_ref, d_ref):
        prod = do_ref[...].astype(F32) * o_ref[...]
        col = _iota((width, LANES), 0)
        first = _iota((width, LANES), 1) * HEAD_V
        sel = jnp.where((col >= first) & (col < first + HEAD_V), 1.0, 0.0).astype(F32)
        per_head = jnp.dot(prod, sel, precision=lax.Precision.HIGHEST, preferred_element_type=F32)
        d_ref[...] = per_head.T[0:n_heads, :]

    return pl.pallas_call(
        body, name="attn_delta", grid=(t // bt,),
        in_specs=[pl.BlockSpec((bt, width), lambda i: (i, 0)), pl.BlockSpec((bt, width), lambda i: (i, 0))],
        out_specs=pl.BlockSpec((n_heads, bt), lambda i: (0, i)),
        out_shape=jax.ShapeDtypeStruct((n_heads, t), F32),
        compiler_params=_params("parallel"),
    )(do, o)


def _flash_bwd_fused(q, k, v, do, lse_t, delta_t, f, f_t, *, mla, scale, col0, name):
    t = q.shape[0]
    blk = min(ATTN_BLOCK, t)
    nb = t // blk
    pairs = HEADS // 2
    ch = min(ATTN_CHUNK, blk)
    assert blk % ch == 0
    w = 2 * LANES if mla else LANES
    last = nb - 1
    qmap = lambda a, b: jnp.maximum(a, b)
    q_spec = pl.BlockSpec((blk, w), lambda j, a, b: (qmap(a, b), j))
    k_spec = pl.BlockSpec((blk, w), lambda j, a, b: (a, j))
    v_spec = pl.BlockSpec((blk, LANES), lambda j, a, b: (a, j))
    do_spec = pl.BlockSpec((blk, LANES), lambda j, a, b: (qmap(a, b), col0 + j))
    lse_spec = pl.BlockSpec((8, blk), lambda j, a, b: (j, qmap(a, b)))
    delta_spec = pl.BlockSpec((8, blk), lambda j, a, b: (col0 // (HEADS // 2), qmap(a, b)))
    fq_spec = pl.BlockSpec((8, blk), lambda j, a, b: (0, qmap(a, b)))
    fk_spec = pl.BlockSpec((blk, LANES), lambda j, a, b: (a, 0))
    dq_spec = pl.BlockSpec((blk, w), lambda j, a, b: (jnp.where(a == last, b, 0), j))
    rs_spec = pl.BlockSpec((8, blk), lambda j, a, b: (j, jnp.where(a == last, b, 0)))
    cs_spec = pl.BlockSpec((blk, LANES), lambda j, a, b: (a, j))

    def body(*refs):
        if mla:
            (q_ref, k_ref, v_ref, do_ref, lse_ref, delta_ref, dq_ref, dk_ref, dv_ref,
             dq_s, dk_s, dv_s, st_s, dpt_s, pt_s, dst_s) = refs
            fq_ref = fk_ref = rs_ref = cs_ref = rs_s = cs_s = None
        else:
            (q_ref, k_ref, v_ref, do_ref, lse_ref, delta_ref, fq_ref, fk_ref, dq_ref, dk_ref, dv_ref, rs_ref, cs_ref,
             dq_s, dk_s, dv_s, st_s, dpt_s, pt_s, dst_s, rs_s, cs_s) = refs
        j, kb, qb = pl.program_id(0), pl.program_id(1), pl.program_id(2)
        lo = _iota((1, LANES), 1) < HEAD_V

        @pl.when((kb == 0) & (qb == 0))
        def _():
            dq_s[...] = jnp.zeros_like(dq_s)
            if not mla:
                rs_s[...] = jnp.zeros_like(rs_s)

        @pl.when(qb == 0)
        def _():
            dk_s[...] = jnp.zeros_like(dk_s)
            dv_s[...] = jnp.zeros_like(dv_s)
            if not mla:
                cs_s[...] = jnp.zeros_like(cs_s)

        def step(masked):
            do = do_ref[...]
            vv = v_ref[...]
            for e in range(2):
                half = lo if e == 0 else jnp.logical_not(lo)
                qe, ke = _head_qk(q_ref, k_ref, e, mla, lo)
                doe = jnp.where(half, do, jnp.zeros_like(do))
                st_s[...] = lax.dot_general(ke, qe, NT_DIMS, preferred_element_type=F32)
                dpt_s[...] = lax.dot_general(vv, doe, NT_DIMS, preferred_element_type=F32)
                head = 2 * j + e
                lse_row = _select_row(lse_ref[...], e)
                delta_row = _select_row(delta_ref[...], head)
                fq_row = None if mla else _select_row(fq_ref[...], head)

                def chunk(r0, e=e, lse_row=lse_row, delta_row=delta_row, fq_row=fq_row, head=head):
                    rows = pl.ds(r0, ch)
                    s = st_s[rows, :] * scale
                    if not mla:
                        s = s + fq_row - _select_lane(fk_ref[rows, :], head)
                    p = jnp.exp(s - lse_row)
                    if masked:
                        kpos = kb * blk + r0 + _iota((ch, blk), 0)
                        qpos = qb * blk + _iota((ch, blk), 1)
                        p = jnp.where((kpos <= qpos) & (kpos >= PAD), p, 0.0)
                    ds = p * (dpt_s[rows, :] - delta_row)
                    pt_s[rows, :] = p.astype(pt_s.dtype)
                    dst_s[rows, :] = ds.astype(dst_s.dtype)
                    if not mla:
                        cs_s[e, rows, :] += jnp.sum(ds, axis=1, keepdims=True)
                        rs_s[qb, e] += jnp.sum(ds, axis=0, keepdims=True)

                _for_chunks(blk // ch, ch, chunk)
                dv_s[e] += jnp.dot(pt_s[...], do, preferred_element_type=F32)
                q_src = qe if mla else q_ref[...]
                dk_s[e] += jnp.dot(dst_s[...], q_src, preferred_element_type=F32)
                dq_s[qb, e] += lax.dot_general(dst_s[...], ke, TN_DIMS, preferred_element_type=F32)

        needs_mask = (qb == kb) | (kb == 0)

        @pl.when((qb >= kb) & needs_mask)
        def _():
            step(True)

        @pl.when((qb >= kb) & jnp.logical_not(needs_mask))
        def _():
            step(False)

        @pl.when(qb == last)
        def _():
            dv_ref[...] = jnp.where(lo, dv_s[0], dv_s[1])
            if mla:
                dk_ref[:, 0:LANES] = dk_s[0] * scale
                dk_ref[:, LANES:2 * LANES] = dk_s[1] * scale
            else:
                dk_ref[...] = jnp.where(lo, dk_s[0], dk_s[1]) * scale
                lane = _iota((1, LANES), 1)
                cs_ref[...] = jnp.where(lane == 0, cs_s[0], jnp.where(lane == 1, cs_s[1], 0.0))

        @pl.when(kb == last)
        def _():
            if mla:
                dq_ref[:, 0:LANES] = dq_s[qb, 0] * scale
                dq_ref[:, LANES:2 * LANES] = dq_s[qb, 1] * scale
            else:
                dq_ref[...] = jnp.where(lo, dq_s[qb, 0], dq_s[qb, 1]) * scale
                sub = _iota((8, 1), 0)
                rs_ref[...] = jnp.where(sub == 0, rs_s[qb, 0], jnp.where(sub == 1, rs_s[qb, 1], 0.0))

    in_specs = [q_spec, k_spec, v_spec, do_spec, lse_spec, delta_spec] + ([] if mla else [fq_spec, fk_spec])
    args = (q, k, v, do, lse_t, delta_t) + (() if mla else (f_t, f))
    hv = HEADS * HEAD_V
    out_specs = [dq_spec, k_spec, v_spec]
    out_shape = [jax.ShapeDtypeStruct((t, pairs * w), F32), jax.ShapeDtypeStruct((t, pairs * w), F32),
                 jax.ShapeDtypeStruct((t, hv), F32)]
    scratch = [pltpu.VMEM((nb, 2, blk, LANES), F32), pltpu.VMEM((2, blk, LANES), F32), pltpu.VMEM((2, blk, LANES), F32),
               pltpu.VMEM((blk, blk), F32), pltpu.VMEM((blk, blk), F32), pltpu.VMEM((blk, blk), MXU_DTYPE),
               pltpu.VMEM((blk, blk), MXU_DTYPE)]
    if not mla:
        out_specs += [rs_spec, cs_spec]
        out_shape += [jax.ShapeDtypeStruct((pairs * 8, t), F32), jax.ShapeDtypeStruct((t, hv), F32)]
        scratch += [pltpu.VMEM((nb, 2, 1, blk), F32), pltpu.VMEM((2, blk, 1), F32)]
    return pl.pallas_call(
        body, name=name, grid=(pairs, nb, nb),
        in_specs=in_specs, out_specs=out_specs, out_shape=out_shape, scratch_shapes=scratch,
        compiler_params=_params("parallel", "arbitrary", "arbitrary"),
    )(*args)


def _shift_down(x, halo, n):
    rows = x.shape[0]
    r = _iota((rows, 1), 0)
    out = pltpu.roll(x, n, axis=0)
    for s in range(n):
        out = jnp.where(r == s, halo[8 - n + s:8 - n + s + 1, :], out)
    return out


def _shift_up(x, halo, n):
    rows = x.shape[0]
    r = _iota((rows, 1), 0)
    out = pltpu.roll(x, rows - n, axis=0)
    for s in range(n):
        out = jnp.where(r == rows - n + s, halo[s:s + 1, :], out)
    return out


def _conv_specs(bt, nblk):
    d = D_MODEL
    per8 = bt // 8
    z_spec = pl.BlockSpec((bt, 3 * d), lambda i: (i, 0))
    prev_spec = pl.BlockSpec((8, 3 * d), lambda i: (jnp.maximum(i * per8 - 1, 0), 0))
    next_z = pl.BlockSpec((8, 3 * d), lambda i: (jnp.minimum((i + 1) * per8, nblk * per8 - 1), 0))
    next_d = pl.BlockSpec((8, d), lambda i: (jnp.minimum((i + 1) * per8, nblk * per8 - 1), 0))
    w_spec = pl.BlockSpec((8, d), lambda i: (0, 0))
    row_spec = pl.BlockSpec((bt, d), lambda i: (i, 0))
    return z_spec, prev_spec, next_z, next_d, w_spec, row_spec


def _conv_taps(z_ref, prev_ref, i):
    d = D_MODEL
    g = z_ref[:, d:2 * d] * z_ref[:, 2 * d:3 * d]
    gh = jnp.where(i > 0, prev_ref[:, d:2 * d] * prev_ref[:, 2 * d:3 * d], 0.0)
    return g, _shift_down(g, gh, 1), _shift_down(g, gh, 2)


def _conv_fwd(z, conv_w8):
    t = z.shape[0]
    bt = min(PREP_TILE, t)
    nblk = t // bt
    d = D_MODEL
    z_spec, prev_spec, _, _, w_spec, row_spec = _conv_specs(bt, nblk)

    def body(z_ref, prev_ref, w_ref, v_ref):
        g, g1, g2 = _conv_taps(z_ref, prev_ref, pl.program_id(0))
        y = w_ref[0:1, :] * g2 + w_ref[1:2, :] * g1 + w_ref[2:3, :] * g
        v_ref[...] = (z_ref[:, 0:d] * y).astype(v_ref.dtype)

    return pl.pallas_call(
        body, name="conv_fwd", grid=(nblk,),
        in_specs=[z_spec, prev_spec, w_spec], out_specs=row_spec,
        out_shape=jax.ShapeDtypeStruct((t, d), MXU_DTYPE),
        compiler_params=_params("parallel"),
    )(z, z, conv_w8)


def _conv_bwd(z, conv_w8, dv):
    t = z.shape[0]
    bt = min(PREP_TILE, t)
    nblk = t // bt
    d = D_MODEL
    z_spec, prev_spec, next_z, next_d, w_spec, row_spec = _conv_specs(bt, nblk)

    def body(z_ref, prev_ref, nz_ref, dv_ref, ndv_ref, w_ref, dz_ref, dw_ref):
        i = pl.program_id(0)

        @pl.when(i == 0)
        def _():
            dw_ref[...] = jnp.zeros_like(dw_ref)

        g, g1, g2 = _conv_taps(z_ref, prev_ref, i)
        w0, w1, w2 = w_ref[0:1, :], w_ref[1:2, :], w_ref[2:3, :]
        y = w0 * g2 + w1 * g1 + w2 * g
        dvv = dv_ref[...].astype(F32)
        gate_b = z_ref[:, 0:d]
        dy = dvv * gate_b
        dyn = jnp.where(i < nblk - 1, ndv_ref[...].astype(F32) * nz_ref[:, 0:d], 0.0)
        dg = w2 * dy + w1 * _shift_up(dy, dyn, 1) + w0 * _shift_up(dy, dyn, 2)
        dz_ref[:, 0:d] = (dvv * y).astype(dz_ref.dtype)
        dz_ref[:, d:2 * d] = (dg * z_ref[:, 2 * d:3 * d]).astype(dz_ref.dtype)
        dz_ref[:, 2 * d:3 * d] = (dg * z_ref[:, d:2 * d]).astype(dz_ref.dtype)
        sub = _iota((8, 1), 0)
        s0 = jnp.sum(dy * g2, axis=0, keepdims=True)
        s1 = jnp.sum(dy * g1, axis=0, keepdims=True)
        s2 = jnp.sum(dy * g, axis=0, keepdims=True)
        dw_ref[...] += jnp.where(sub == 0, s0, jnp.where(sub == 1, s1, jnp.where(sub == 2, s2, 0.0)))

    return pl.pallas_call(
        body, name="conv_bwd", grid=(nblk,),
        in_specs=[z_spec, prev_spec, next_z, row_spec, next_d, w_spec], out_specs=[z_spec, w_spec],
        out_shape=[jax.ShapeDtypeStruct((t, 3 * d), MXU_DTYPE), jax.ShapeDtypeStruct((8, d), F32)],
        compiler_params=_params("arbitrary"),
    )(z, z, z, dv, dv, conv_w8)


def _loss_head(h, target):
    t, d = h.shape
    bt = LOSS_TILE
    assert LANES % bt == 0 or bt == LANES
    off = LANES // bt

    def body(h_ref, y_ref, dh_ref, acc_ref):
        i = pl.program_id(0)

        @pl.when(i == 0)
        def _():
            acc_ref[...] = jnp.zeros_like(acc_ref)

        @pl.when(i < off)
        def _():
            dh_ref[...] = jnp.zeros_like(dh_ref)

        @pl.when(i >= off)
        def _():
            err = h_ref[...] - y_ref[...]
            dh_ref[...] = err / d
            acc_ref[...] += jnp.sum(err * err)

    dh, acc = pl.pallas_call(
        body, name="loss_head", grid=(t // bt,),
        in_specs=[pl.BlockSpec((bt, d), lambda i: (i, 0)), pl.BlockSpec((bt, d), lambda i: (jnp.maximum(i - off, 0), 0))],
        out_specs=[pl.BlockSpec((bt, d), lambda i: (i, 0)), pl.BlockSpec((8, LANES), lambda i: (0, 0))],
        out_shape=[jax.ShapeDtypeStruct((t, d), F32), jax.ShapeDtypeStruct((8, LANES), F32)],
        compiler_params=_params("arbitrary"),
    )(h, target)
    return dh, acc[0, 0] * (0.5 / d)


def _common_tile(rows, row_off, cap=512, align=8):
    for b in range(min(cap, rows) // align * align, 0, -align):
        if rows % b == 0 and row_off % b == 0:
            return b
    raise ValueError((rows, row_off))


def _round_up(n, m):
    return -(-n // m) * m


def _adamw(w, m, v, g_buf, row_off, col_off):
    rows, width = w.shape
    wpad = _round_up(width, LANES)
    assert col_off % wpad == 0
    bt = _common_tile(rows, row_off)

    def body(w_ref, m_ref, v_ref, g_ref, go_ref, d_ref, nm_ref, nv_ref):
        gv = g_ref[...]
        if wpad != width:
            gv = gv[:, :width]
        m_new = ADAM_B1 * m_ref[...] + (1.0 - ADAM_B1) * gv
        v_new = ADAM_B2 * v_ref[...] + (1.0 - ADAM_B2) * jnp.square(gv)
        m_hat = m_new / (1.0 - ADAM_B1 ** ADAM_STEP)
        v_hat = v_new / (1.0 - ADAM_B2 ** ADAM_STEP)
        go_ref[...] = gv
        d_ref[...] = -ADAM_LR * (m_hat / (jnp.sqrt(v_hat) + ADAM_EPS) + ADAM_WD * w_ref[...])
        nm_ref[...] = m_new
        nv_ref[...] = v_new

    spec = pl.BlockSpec((bt, width), lambda i: (i, 0))
    g_spec = pl.BlockSpec((bt, wpad), lambda i: (row_off // bt + i, col_off // wpad))
    return pl.pallas_call(
        body, name="adamw", grid=(rows // bt,),
        in_specs=[spec] * 3 + [g_spec], out_specs=[spec] * 4,
        out_shape=[jax.ShapeDtypeStruct((rows, width), F32)] * 4,
        compiler_params=_params("parallel"),
    )(w, m, v, g_buf)


def _add2(a, b, *, out_dtype, name):
    rows, width = a.shape
    bt = next(x for x in range(min(rows, 640), 0, -16) if rows % x == 0)

    def body(a_ref, b_ref, o_ref):
        o_ref[...] = (a_ref[...] + b_ref[...]).astype(o_ref.dtype)

    spec = pl.BlockSpec((bt, width), lambda i: (i, 0))
    return pl.pallas_call(
        body, name=name, grid=(rows // bt,), in_specs=[spec, spec], out_specs=spec,
        out_shape=jax.ShapeDtypeStruct((rows, width), out_dtype), compiler_params=_params("parallel"),
    )(a, b)


def _sum4(parts, *, name):
    _, rows, width = parts.shape
    bt = next(x for x in range(min(rows, 640), 0, -16) if rows % x == 0)

    def body(p_ref, o_ref):
        p = [p_ref[n].astype(F32) for n in range(4)]
        o_ref[...] = ((p[0] + p[1]) + p[2]) + p[3]

    return pl.pallas_call(
        body, name=name, grid=(rows // bt,),
        in_specs=[pl.BlockSpec((4, bt, width), lambda i: (0, i, 0))],
        out_specs=pl.BlockSpec((bt, width), lambda i: (i, 0)),
        out_shape=jax.ShapeDtypeStruct((rows, width), F32), compiler_params=_params("parallel"),
    )(parts)


ANY = pl.BlockSpec(memory_space=pl.ANY)
CHIP_FLIPS = ((1, 0), (0, 1), (1, 1))


def _place():
    return lax.axis_index("x"), lax.axis_index("y"), lax.axis_index("c")


def _flip(v, f):
    return 1 - v if f else v


def _allgather_chips(pack):
    rows, width = pack.shape
    half = rows // 2

    def body(pack_ref, out_ref, send_sems, recv_sems, local_sem):
        x, y, c = _place()
        me = 2 * x + y
        sibling = (x, y, 1 - c)
        mine = pltpu.make_async_copy(pack_ref, out_ref.at[me], local_sem)
        mine.start()
        my_rows = pl.ds(pl.multiple_of(c * half, 8), half)
        sib_rows = pl.ds(pl.multiple_of((1 - c) * half, 8), half)
        first, passed = [], []
        for n, (fx, fy) in enumerate(CHIP_FLIPS):
            px, py = _flip(x, fx), _flip(y, fy)
            peer = 2 * px + py
            first.append(pltpu.make_async_remote_copy(
                src_ref=pack_ref.at[my_rows], dst_ref=out_ref.at[me, my_rows],
                send_sem=send_sems.at[n], recv_sem=recv_sems.at[n], device_id=(px, py, c), device_id_type=MESH))
            passed.append(pltpu.make_async_remote_copy(
                src_ref=out_ref.at[peer, my_rows], dst_ref=out_ref.at[peer, my_rows],
                send_sem=send_sems.at[3 + n], recv_sem=recv_sems.at[3 + n], device_id=sibling, device_id_type=MESH))
        for cp in first:
            cp.start()
        for n, (fx, fy) in enumerate(CHIP_FLIPS):
            peer = 2 * _flip(x, fx) + _flip(y, fy)
            pltpu.make_async_remote_copy(
                src_ref=pack_ref.at[my_rows], dst_ref=out_ref.at[peer, my_rows],
                send_sem=send_sems.at[n], recv_sem=recv_sems.at[n], device_id=sibling, device_id_type=MESH).wait_recv()
            passed[n].start()
        for n, (fx, fy) in enumerate(CHIP_FLIPS):
            peer = 2 * _flip(x, fx) + _flip(y, fy)
            pltpu.make_async_remote_copy(
                src_ref=pack_ref.at[sib_rows], dst_ref=out_ref.at[peer, sib_rows],
                send_sem=send_sems.at[3 + n], recv_sem=recv_sems.at[3 + n], device_id=sibling,
                device_id_type=MESH).wait_recv()
        for cp in first + passed:
            cp.wait_send()
        mine.wait()

    return pl.pallas_call(
        body, name="allgather_weights",
        in_specs=[ANY], out_specs=ANY,
        out_shape=jax.ShapeDtypeStruct((4, rows, width), pack.dtype),
        scratch_shapes=[pltpu.SemaphoreType.DMA((6,)), pltpu.SemaphoreType.DMA((6,)), pltpu.SemaphoreType.DMA],
    )(pack)


def _swap_halves(g):
    _, rows, width = g.shape
    half = rows // 2

    def body(g_ref, got_ref, send_sem, recv_sem):
        x, y, c = _place()
        away = pl.ds(pl.multiple_of((1 - c) * half, 8), half)
        cp = pltpu.make_async_remote_copy(
            src_ref=g_ref.at[:, away], dst_ref=got_ref, send_sem=send_sem, recv_sem=recv_sem,
            device_id=(x, y, 1 - c), device_id_type=MESH)
        cp.start()
        cp.wait()

    return pl.pallas_call(
        body, name="grad_swap_halves",
        in_specs=[ANY], out_specs=ANY,
        out_shape=jax.ShapeDtypeStruct((4, half, width), g.dtype),
        scratch_shapes=[pltpu.SemaphoreType.DMA, pltpu.SemaphoreType.DMA],
    )(g)


def _scatter_chips(s):
    _, rows, width = s.shape

    def body(s_ref, out_ref, send_sems, recv_sems, local_sem):
        x, y, c = _place()
        me = 2 * x + y
        mine = pltpu.make_async_copy(s_ref.at[me], out_ref.at[me], local_sem)
        mine.start()
        copies = []
        for n, (fx, fy) in enumerate(CHIP_FLIPS):
            px, py = _flip(x, fx), _flip(y, fy)
            copies.append(pltpu.make_async_remote_copy(
                src_ref=s_ref.at[2 * px + py], dst_ref=out_ref.at[me],
                send_sem=send_sems.at[n], recv_sem=recv_sems.at[n], device_id=(px, py, c), device_id_type=MESH))
        for cp in copies:
            cp.start()
        for n, (fx, fy) in enumerate(CHIP_FLIPS):
            peer = 2 * _flip(x, fx) + _flip(y, fy)
            pltpu.make_async_remote_copy(
                src_ref=s_ref.at[me], dst_ref=out_ref.at[peer],
                send_sem=send_sems.at[n], recv_sem=recv_sems.at[n], device_id=(x, y, c), device_id_type=MESH).wait_recv()
        for cp in copies:
            cp.wait_send()
        mine.wait()

    return pl.pallas_call(
        body, name="grad_scatter_chips",
        in_specs=[ANY], out_specs=ANY,
        out_shape=jax.ShapeDtypeStruct((4, rows, width), s.dtype),
        scratch_shapes=[pltpu.SemaphoreType.DMA((3,)), pltpu.SemaphoreType.DMA((3,)), pltpu.SemaphoreType.DMA],
    )(s)


def _join_halves(tot):
    rows, width = tot.shape

    def body(t_ref, out_ref, send_sem, recv_sem, local_sem):
        x, y, c = _place()
        mine = pltpu.make_async_copy(t_ref, out_ref.at[c], local_sem)
        mine.start()
        cp = pltpu.make_async_remote_copy(
            src_ref=t_ref, dst_ref=out_ref.at[c], send_sem=send_sem, recv_sem=recv_sem,
            device_id=(x, y, 1 - c), device_id_type=MESH)
        cp.start()
        pltpu.make_async_remote_copy(
            src_ref=t_ref, dst_ref=out_ref.at[1 - c], send_sem=send_sem, recv_sem=recv_sem,
            device_id=(x, y, 1 - c), device_id_type=MESH).wait_recv()
        cp.wait_send()
        mine.wait()

    return pl.pallas_call(
        body, name="grad_join_halves",
        in_specs=[ANY], out_specs=ANY,
        out_shape=jax.ShapeDtypeStruct((2, rows, width), tot.dtype),
        scratch_shapes=[pltpu.SemaphoreType.DMA, pltpu.SemaphoreType.DMA, pltpu.SemaphoreType.DMA],
    )(tot)


PACK_W = 1024
REPLICATED = ("g_mix", "g_mlp", "g_cq", "g_ckv", "g_q_mla", "g_k_mla", "g_q_fox", "g_k_fox", "b_forget")
WEIGHT_ORDER = ("meta_tokens", "g_mix", "g_mlp", "w_in_attn", "g_cq", "w_uq", "g_ckv", "w_ukv", "g_q_mla", "g_k_mla",
                "g_q_fox", "g_k_fox", "b_forget", "w_out_attn", "w_in_conv", "conv_w", "w_out_conv", "w_mlp_up",
                "w_mlp_down")
N_EVEN = 2
N_ODD = 2
SHARD_IN = ATTN_IN // 4
SHARD_MIX = D_MODEL // 4
SHARD_UQ = HEADS * MLA_QK // 4
SHARD_UKV = HEADS * (MLA_NOPE + HEAD_V) // 4
SHARD_CONV = 3 * D_MODEL // 4
SIDE_W = 256
PK_UP = (0, 0)
PK_DOWN = (4096, 0)
PK_CONV_IN = (8192, 0)
PK_ATTN_IN = (10240, 0)
PK_OUT_ATTN = (12288, 0)
PK_OUT_CONV = (12800, 0)
PK_SMALL = (8192, 768)
PK_UQ = (10240, 768)
PK_UKV = (11008, 768)
PK_ROWS = 13312
SMALL_ROWS = 64
SMALL_META = 0
SMALL_CONV = 16
SMALL_REP = 24
SMALL_BITS_ROWS = 48
MATRIX_PLACES = (("w_mlp_up", PK_UP), ("w_mlp_down", PK_DOWN), ("w_in_conv", PK_CONV_IN), ("w_in_attn", PK_ATTN_IN),
                 ("w_out_attn", PK_OUT_ATTN), ("w_out_conv", PK_OUT_CONV), ("w_uq", PK_UQ), ("w_ukv", PK_UKV))


def _put(buf, x, place, *, name):
    row_off, col_off = place
    slabs = x.ndim == 3
    rows, w = x.shape[-2:]
    wpad = _round_up(w, LANES)
    assert col_off % wpad == 0
    bt = _common_tile(rows, row_off, align=16)

    def body(x_ref, _, o_ref):
        v = x_ref[...].astype(o_ref.dtype)
        if wpad != w:
            v = jnp.concatenate([v, jnp.zeros((bt, wpad - w), o_ref.dtype)], axis=1)
        o_ref[...] = v

    if slabs:
        grid = (4, rows // bt)
        x_spec = pl.BlockSpec((None, bt, w), lambda s, i: (s, i, 0))
        o_spec = pl.BlockSpec((None, bt, wpad), lambda s, i: (s, row_off // bt + i, col_off // wpad))
        sem = ("parallel", "parallel")
    else:
        grid = (rows // bt,)
        x_spec = pl.BlockSpec((bt, w), lambda i: (i, 0))
        o_spec = pl.BlockSpec((bt, wpad), lambda i: (row_off // bt + i, col_off // wpad))
        sem = ("parallel",)
    return pl.pallas_call(
        body, name=name, grid=grid, in_specs=[x_spec, ANY], out_specs=o_spec,
        out_shape=jax.ShapeDtypeStruct(buf.shape, buf.dtype), input_output_aliases={1: 0},
        compiler_params=_params(*sem),
    )(x, buf)


def _w_cols(place, layer, rows, width):
    base = (place[0] + layer * rows) // rows
    return dict(n=4 * width, tn=width, tk=rows, spec=pl.BlockSpec((None, rows, width), lambda i, j, k: (j, base, 0)))


def _w_cols_t(place, layer, rows, width):
    base = (place[0] + layer * rows) // rows
    return dict(n=rows, tn=rows, tk=width, spec=pl.BlockSpec((None, rows, width), lambda i, j, k: (k, base, 0)))


def _w_rows(place, layer, rows):
    base = (place[0] + layer * rows) // rows
    return dict(n=D_MODEL, tn=D_MODEL, tk=rows, spec=pl.BlockSpec((None, rows, D_MODEL), lambda i, j, k: (k, base, 0)))


def _w_rows_t(place, layer, rows):
    base = (place[0] + layer * rows) // rows
    return dict(n=4 * rows, tn=rows, tk=D_MODEL, spec=pl.BlockSpec((None, rows, D_MODEL), lambda i, j, k: (j, base, 0)))


def _g_cols(g, place, layer, rows, width):
    base = (place[0] + layer * rows) // rows
    return g, pl.BlockSpec((None, rows, width), lambda i, j, k: (j, base, 0))


def _g_rows(g, place, layer, rows):
    base = (place[0] + layer * rows) // rows
    return g, pl.BlockSpec((None, rows, D_MODEL), lambda i, j, k: (i, base, 0))


IN_PADW = _round_up(SHARD_IN, LANES)
IN_TAIL = ZA_FQ - SHARD_IN
IN_FL = SHARD_IN - HEADS
ZA_KPE = ZA_LAST + KPE_LANE


def _assemble_attn_in(gathered, layer):
    bt = 256
    base = (PK_ATTN_IN[0] + layer * D_MODEL) // bt
    assert 2 * SHARD_IN > ZA_FQ + MLA_ROPE and 3 * SHARD_IN < ATTN_IN - HEADS

    def body(s0, s1, s2, s3, o_ref):
        dt = o_ref.dtype
        z = lambda n: jnp.zeros((bt, n), dt)
        o_ref[...] = jnp.concatenate(
            [s0[:, :SHARD_IN], s1[:, :IN_TAIL], s1[:, IN_TAIL + MLA_ROPE:SHARD_IN], s2[:, :SHARD_IN], s3[:, :IN_FL],
             s3[:, IN_FL:SHARD_IN], z(KPE_LANE - HEADS), s1[:, IN_TAIL:IN_TAIL + MLA_ROPE],
             z(LANES - KPE_LANE - MLA_ROPE)], axis=1).astype(dt)

    def spec(s):
        return pl.BlockSpec((None, bt, IN_PADW), lambda i: (s, base + i, 0))

    return pl.pallas_call(
        body, name="assemble_attn_in", grid=(D_MODEL // bt,),
        in_specs=[spec(s) for s in range(4)], out_specs=pl.BlockSpec((bt, ZA_W), lambda i: (i, 0)),
        out_shape=jax.ShapeDtypeStruct((D_MODEL, ZA_W), MXU_DTYPE), compiler_params=_params("parallel"),
    )(gathered, gathered, gathered, gathered)


def _scatter_attn_in(g, dwa, layer):
    bt = 256
    base = (PK_ATTN_IN[0] + layer * D_MODEL) // bt
    fq1 = ZA_FQ + SHARD_IN - IN_TAIL - MLA_ROPE

    def body(d_ref, _, o_ref):
        pad = jnp.zeros((bt, IN_PADW - SHARD_IN), F32)
        pieces = (
            (d_ref[:, 0:SHARD_IN],),
            (d_ref[:, SHARD_IN:ZA_FQ], d_ref[:, ZA_KPE:ZA_KPE + MLA_ROPE], d_ref[:, ZA_FQ:fq1]),
            (d_ref[:, fq1:fq1 + SHARD_IN],),
            (d_ref[:, fq1 + SHARD_IN:ZA_LAST], d_ref[:, ZA_LAST:ZA_LAST + HEADS]),
        )
        for s in range(4):
            @pl.when(pl.program_id(0) == s)
            def _(s=s):
                o_ref[...] = jnp.concatenate(list(pieces[s]) + [pad], axis=1)

    return pl.pallas_call(
        body, name="scatter_attn_in", grid=(4, D_MODEL // bt),
        in_specs=[pl.BlockSpec((bt, ZA_W), lambda s, i: (i, 0)), ANY],
        out_specs=pl.BlockSpec((None, bt, IN_PADW), lambda s, i: (s, base + i, 0)),
        out_shape=jax.ShapeDtypeStruct(g.shape, g.dtype), input_output_aliases={1: 0},
        compiler_params=_params("parallel", "parallel"),
    )(dwa, g)


def _assemble_uq(gathered, layer):
    bt = 128
    base = (PK_UQ[0] + layer * Q_LORA) // bt
    col = PK_UQ[1] // SIDE_W

    def body(s0, s1, s2, s3, o_ref):
        dt = o_ref.dtype
        z = jnp.zeros((bt, LANES - MLA_QK), dt)
        parts = []
        for s_ref in (s0, s1, s2, s3):
            parts += [s_ref[:, 0:MLA_QK], z, s_ref[:, MLA_QK:2 * MLA_QK], z]
        o_ref[...] = jnp.concatenate(parts, axis=1).astype(dt)

    def spec(s):
        return pl.BlockSpec((None, bt, SIDE_W), lambda i: (s, base + i, col))

    return pl.pallas_call(
        body, name="assemble_uq", grid=(Q_LORA // bt,),
        in_specs=[spec(s) for s in range(4)], out_specs=pl.BlockSpec((bt, HEADS * LANES), lambda i: (i, 0)),
        out_shape=jax.ShapeDtypeStruct((Q_LORA, HEADS * LANES), MXU_DTYPE), compiler_params=_params("parallel"),
    )(gathered, gathered, gathered, gathered)


def _scatter_uq(g, dw, layer):
    bt = 128
    base = (PK_UQ[0] + layer * Q_LORA) // bt
    col = PK_UQ[1] // SIDE_W

    def body(d_ref, _, o_ref):
        o_ref[...] = jnp.concatenate([d_ref[:, 0:MLA_QK], d_ref[:, LANES:LANES + MLA_QK],
                                      jnp.zeros((bt, SIDE_W - 2 * MLA_QK), F32)], axis=1)

    return pl.pallas_call(
        body, name="scatter_uq", grid=(4, Q_LORA // bt),
        in_specs=[pl.BlockSpec((bt, 2 * LANES), lambda s, i: (i, s)), ANY],
        out_specs=pl.BlockSpec((None, bt, SIDE_W), lambda s, i: (s, base + i, col)),
        out_shape=jax.ShapeDtypeStruct(g.shape, g.dtype), input_output_aliases={1: 0},
        compiler_params=_params("parallel", "parallel"),
    )(dw, g)


def _assemble_ukv(gathered, layer):
    bt = KV_LORA
    base = (PK_UKV[0] + layer * KV_LORA) // bt
    col = PK_UKV[1] // SIDE_W
    hd = MLA_NOPE + HEAD_V

    def body(s0, s1, s2, s3, o_ref):
        dt = o_ref.dtype
        z = jnp.zeros((bt, LANES - MLA_NOPE), dt)
        keys, vals = [], []
        for s_ref in (s0, s1, s2, s3):
            for e in range(2):
                keys += [s_ref[:, e * hd:e * hd + MLA_NOPE], z]
                vals.append(s_ref[:, e * hd + MLA_NOPE:(e + 1) * hd])
        o_ref[...] = jnp.concatenate(keys + vals, axis=1).astype(dt)

    def spec(s):
        return pl.BlockSpec((None, bt, SIDE_W), lambda i: (s, base + i, col))

    return pl.pallas_call(
        body, name="assemble_ukv", grid=(1,),
        in_specs=[spec(s) for s in range(4)],
        out_specs=pl.BlockSpec((bt, HEADS * (LANES + HEAD_V)), lambda i: (i, 0)),
        out_shape=jax.ShapeDtypeStruct((KV_LORA, HEADS * (LANES + HEAD_V)), MXU_DTYPE), compiler_params=_params("parallel"),
    )(gathered, gathered, gathered, gathered)


def _scatter_ukv(g, dw, layer):
    bt = KV_LORA
    base = (PK_UKV[0] + layer * KV_LORA) // bt
    col = PK_UKV[1] // SIDE_W

    def body(k_ref, v_ref, _, o_ref):
        o_ref[...] = jnp.concatenate([k_ref[:, 0:MLA_NOPE], v_ref[:, 0:HEAD_V], k_ref[:, LANES:LANES + MLA_NOPE],
                                      v_ref[:, HEAD_V:2 * HEAD_V]], axis=1)

    return pl.pallas_call(
        body, name="scatter_ukv", grid=(4,),
        in_specs=[pl.BlockSpec((bt, 2 * LANES), lambda s: (0, s)),
                  pl.BlockSpec((bt, 2 * HEAD_V), lambda s: (0, HEADS * LANES // (2 * HEAD_V) + s)), ANY],
        out_specs=pl.BlockSpec((None, bt, SIDE_W), lambda s: (s, base, col)),
        out_shape=jax.ShapeDtypeStruct(g.shape, g.dtype), input_output_aliases={2: 0},
        compiler_params=_params("parallel"),
    )(dw, dw, g)


def _pad_lanes(v, n=LANES):
    return jnp.pad(v, (0, n - v.shape[0])).reshape(1, n)


def _relu2_up(acc):
    r = jnp.maximum(acc, 0.0)
    return acc, r * r


def _relu2_bwd(acc, u):
    return (acc * (2.0 * jnp.maximum(u, 0.0)),)


def _add_res(acc, res):
    return (acc + res,)


def _local_step(x, target, meta, small, gathered):
    seq = x.shape[0]
    t = seq + LANES
    d = D_MODEL
    h = jnp.concatenate([jnp.zeros((PAD, d), F32), meta.astype(F32), x], axis=0)
    c_tab, s_tab = _rope_tables(t)
    scale_mla, scale_fox = MLA_QK ** -0.5, FOX_DIM ** -0.5
    grads = {}
    saved = []
    g = jnp.zeros((4, PK_ROWS, PACK_W), F32)

    for layer in range(DEPTH):
        j = layer // 2
        sv = {"h_in": h}
        hn = _rmsnorm_fwd(h, small["g_mix"][layer])
        sv["hn"] = hn
        if layer % 2 == 0:
            w_in = _assemble_attn_in(gathered, j)
            w_uq = _assemble_uq(gathered, j)
            w_ukv = _assemble_ukv(gathered, j)
            out_place = PK_OUT_ATTN
            vecs = dict(
                g_cq=small["g_cq"][j].reshape(1, Q_LORA), g_ckv=small["g_ckv"][j].reshape(1, KV_LORA),
                g_qf=jnp.tile(small["g_q_fox"][j], 2).reshape(1, LANES), g_kf=jnp.tile(small["g_k_fox"][j], 2).reshape(1, LANES),
                b_f=_pad_lanes(small["b_forget"][j]), g_q=_pad_lanes(small["g_q_mla"][j]), g_k=_pad_lanes(small["g_k_mla"][j]))
            z = _matmul(hn, w_in, name="mm_attn_in")
            cqn, ckvn, qf, kf, vf, logf = _prep_a_fwd(z, vecs["g_cq"], vecs["g_ckv"], vecs["g_qf"], vecs["g_kf"], vecs["b_f"])
            f_cum, f_cum_t = _cumsum_rows(logf, reverse=False, name="cumsum_fwd")
            q_raw = _matmul(cqn, w_uq, name="mm_uq")
            kv_raw = _matmul(ckvn, w_ukv, name="mm_ukv")
            qn, kn, v_mla = _prep_b_fwd(q_raw, kv_raw, z, c_tab, s_tab, vecs["g_q"], vecs["g_k"])
            o_mla, lse_mla = _flash_fwd_chunked(qn, kn, v_mla, None, None, mla=True, scale=scale_mla, name="flash_fwd_mla")
            o_fox, lse_fox = _flash_fwd_chunked(qf, kf, vf, f_cum, f_cum_t, mla=False, scale=scale_fox,
                                                name="flash_fwd_fox")
            o = jnp.concatenate([o_mla, o_fox], axis=1)
            h = _matmul(o, gathered, b_tiles=_w_rows(out_place, j, SHARD_MIX), extras=(h,), epilogue=_add_res,
                        name="mm_mix_out")
            sv.update(w_in=w_in, w_uq=w_uq, w_ukv=w_ukv, out_place=out_place, vecs=vecs, z=z, cqn=cqn, ckvn=ckvn, qf=qf, kf=kf,
                      vf=vf, f_cum=f_cum, f_cum_t=f_cum_t, q_raw=q_raw, kv_raw=kv_raw, qn=qn, kn=kn, v_mla=v_mla, o=o,
                      lse_mla=lse_mla, lse_fox=lse_fox)
        else:
            out_place = PK_OUT_CONV
            conv_w8 = jnp.pad(small["conv_w"][j], ((0, 5), (0, 0)))
            z = _matmul(hn, gathered, b_tiles=_w_cols(PK_CONV_IN, j, d, SHARD_CONV), name="mm_conv_in")
            vmix = _conv_fwd(z, conv_w8)
            h = _matmul(vmix, gathered, b_tiles=_w_rows(out_place, j, SHARD_MIX), extras=(h,), epilogue=_add_res,
                        name="mm_mix_out")
            sv.update(out_place=out_place, conv_w8=conv_w8, z=z, vmix=vmix)
        sv["h_mid"] = h
        hn2 = _rmsnorm_fwd(h, small["g_mlp"][layer])
        u, a = _matmul(hn2, gathered, b_tiles=_w_cols(PK_UP, layer, d, d), epilogue=_relu2_up,
                       out_dtypes=(F32, MXU_DTYPE), name="mm_mlp_up")
        h = _matmul(a, gathered, b_tiles=_w_rows(PK_DOWN, layer, d), extras=(h,), epilogue=_add_res, name="mm_mlp_down")
        sv.update(hn2=hn2, u=u, a=a)
        saved.append(sv)

    dh, loss_local = _loss_head(h, target)

    dg_mix, dg_mlp = [None] * DEPTH, [None] * DEPTH
    per_even = {k: [None, None] for k in ("g_cq", "g_ckv", "g_q_mla", "g_k_mla", "g_q_fox", "g_k_fox", "b_forget")}
    per_odd = {"conv_w": [None, None]}
    for layer in reversed(range(DEPTH)):
        j = layer // 2
        sv = saved[layer]
        du = _matmul(dh, gathered, tb=True, b_tiles=_w_rows_t(PK_DOWN, layer, d), extras=(sv["u"],),
                     epilogue=_relu2_bwd, out_dtypes=(MXU_DTYPE,), name="mm_mlp_da")
        g = _matmul(sv["a"], dh, ta=True, out_into=_g_rows(g, PK_DOWN, layer, d), name="mm_dw_down")
        g = _matmul(sv["hn2"], du, ta=True, out_into=_g_cols(g, PK_UP, layer, d, d), name="mm_dw_up")
        dhn2 = _matmul(du, gathered, tb=True, b_tiles=_w_cols_t(PK_UP, layer, d, d), name="mm_mlp_dhn")
        dh, dg_mlp[layer] = _rmsnorm_bwd(sv["h_mid"], small["g_mlp"][layer], dhn2, dh)
        do = _matmul(dh, gathered, tb=True, b_tiles=_w_rows_t(sv["out_place"], j, SHARD_MIX), out_dtypes=(MXU_DTYPE,),
                     name="mm_mix_do")
        if layer % 2 == 0:
            vecs = sv["vecs"]
            g = _matmul(sv["o"], dh, ta=True, tm=SHARD_MIX, out_into=_g_rows(g, PK_OUT_ATTN, j, SHARD_MIX),
                        name="mm_dw_out")
            delta_t = _delta_rows(do, sv["o"])
            dqn, dkn, dv_mla = _flash_bwd_fused(sv["qn"], sv["kn"], sv["v_mla"], do, sv["lse_mla"], delta_t, None, None,
                                                mla=True, scale=scale_mla, col0=0, name="flash_bwd_mla")
            dqf, dkf, dvf, rs_t, cs = _flash_bwd_fused(sv["qf"], sv["kf"], sv["vf"], do, sv["lse_fox"], delta_t,
                                                       sv["f_cum"], sv["f_cum_t"], mla=False, scale=scale_fox,
                                                       col0=HEADS // 2, name="flash_bwd_fox")
            d_f = rs_t.reshape(HEADS // 2, 8, t)[:, :2, :].reshape(HEADS, t).T
            d_f = d_f - cs.reshape(t, HEADS // 2, LANES)[:, :, :2].reshape(t, HEADS)
            d_f = jnp.pad(d_f, ((0, 0), (0, LANES - HEADS)))
            dlogf, _ = _cumsum_rows(d_f, reverse=True, name="cumsum_bwd")
            dq_raw, dkv_raw, dkpe, dg_q, dg_k = _prep_b_bwd(sv["q_raw"], sv["kv_raw"], sv["z"], c_tab, s_tab, vecs["g_q"],
                                                            vecs["g_k"], dqn, dkn, dv_mla)
            g = _scatter_uq(g, _matmul(sv["cqn"], dq_raw, ta=True, name="mm_dw_uq"), j)
            g = _scatter_ukv(g, _matmul(sv["ckvn"], dkv_raw, ta=True, name="mm_dw_ukv"), j)
            dcqn = _matmul(dq_raw, sv["w_uq"], tb=True, name="mm_dcqn")
            dckvn = _matmul(dkv_raw, sv["w_ukv"], tb=True, name="mm_dckvn")
            dz, dg_cq, dg_ckv, dg_qf, dg_kf, db_f = _prep_a_bwd(
                sv["z"], vecs["g_cq"], vecs["g_ckv"], vecs["g_qf"], vecs["g_kf"], vecs["b_f"], dcqn, dckvn, dqf, dkf, dvf,
                dlogf, dkpe)
            g = _scatter_attn_in(g, _matmul(sv["hn"], dz, ta=True, name="mm_dw_attn_in"), j)
            per_even["g_cq"][j] = dg_cq[0]
            per_even["g_ckv"][j] = dg_ckv[0]
            per_even["g_q_mla"][j] = dg_q[0, :MLA_QK]
            per_even["g_k_mla"][j] = dg_k[0, :MLA_QK]
            per_even["g_q_fox"][j] = dg_qf[0, :FOX_DIM] + dg_qf[0, FOX_DIM:]
            per_even["g_k_fox"][j] = dg_kf[0, :FOX_DIM] + dg_kf[0, FOX_DIM:]
            per_even["b_forget"][j] = db_f[0, :HEADS]
            dhn = _matmul(dz, sv["w_in"], tb=True, name="mm_attn_dhn")
        else:
            g = _matmul(sv["vmix"], dh, ta=True, tm=SHARD_MIX, out_into=_g_rows(g, PK_OUT_CONV, j, SHARD_MIX),
                        name="mm_dw_out")
            dz, dcw = _conv_bwd(sv["z"], sv["conv_w8"], do)
            per_odd["conv_w"][j] = dcw[:3]
            g = _matmul(sv["hn"], dz, ta=True, tn=SHARD_CONV, out_into=_g_cols(g, PK_CONV_IN, j, d, SHARD_CONV),
                        name="mm_dw_conv_in")
            dhn = _matmul(dz, gathered, tb=True, b_tiles=_w_cols_t(PK_CONV_IN, j, d, SHARD_CONV), name="mm_conv_dhn")
        dh, dg_mix[layer] = _rmsnorm_bwd(sv["h_in"], small["g_mix"][layer], dhn, dh)

    grads["meta_tokens"] = dh[PAD:LANES]
    grads["g_mix"] = jnp.stack(dg_mix)
    grads["g_mlp"] = jnp.stack(dg_mlp)
    for k, v in list(per_even.items()) + list(per_odd.items()):
        grads[k] = jnp.stack(v)
    return loss_local, dh[LANES:], g, grads


def kernel(x, meta_tokens, g_mix, g_mlp, w_in_attn, g_cq, w_uq, g_ckv, w_ukv, g_q_mla, g_k_mla, g_q_fox, g_k_fox, b_forget, w_out_attn, w_in_conv, conv_w, w_out_conv, w_mlp_up, w_mlp_down, loss_target, m_meta_tokens, m_g_mix, m_g_mlp, m_w_in_attn, m_g_cq, m_w_uq, m_g_ckv, m_w_ukv, m_g_q_mla, m_g_k_mla, m_g_q_fox, m_g_k_fox, m_b_forget, m_w_out_attn, m_w_in_conv, m_conv_w, m_w_out_conv, m_w_mlp_up, m_w_mlp_down, v_meta_tokens, v_g_mix, v_g_mlp, v_w_in_attn, v_g_cq, v_w_uq, v_g_ckv, v_w_ukv, v_g_q_mla, v_g_k_mla, v_g_q_fox, v_g_k_fox, v_b_forget, v_w_out_attn, v_w_in_conv, v_conv_w, v_w_out_conv, v_w_mlp_up, v_w_mlp_down):
    args = dict(locals())
    weights = {n: args[n] for n in WEIGHT_ORDER}
    mom_m = {n: args["m_" + n] for n in WEIGHT_ORDER}
    mom_v = {n: args["v_" + n] for n in WEIGHT_ORDER}

    wire = jnp.bfloat16
    buf = jnp.zeros((PK_ROWS, PACK_W), wire)
    for name, place in MATRIX_PLACES:
        w = weights[name]
        buf = _put(buf, w.reshape(-1, w.shape[-1]), place, name="pack_weights")
    meta_bits = lax.bitcast_convert_type(meta_tokens, wire).reshape(2 * N_META, SIDE_W)
    conv_bits = lax.bitcast_convert_type(conv_w, wire).reshape(2 * N_ODD * 3, SIDE_W)
    bits = jnp.concatenate([meta_bits, conv_bits, jnp.zeros((SMALL_BITS_ROWS - 2 * N_META - 2 * N_ODD * 3, SIDE_W), wire)])
    buf = _put(buf, bits, PK_SMALL, name="pack_weights")
    gathered = _allgather_chips(buf)
    got_bits = gathered[:, PK_SMALL[0]:PK_SMALL[0] + SMALL_BITS_ROWS, PK_SMALL[1]:PK_SMALL[1] + SIDE_W]
    meta_full = lax.bitcast_convert_type(got_bits[:, :2 * N_META].reshape(4, N_META, SIDE_W, 2), F32)
    meta_full = meta_full.transpose(1, 0, 2).reshape(N_META, D_MODEL)
    conv_full = lax.bitcast_convert_type(
        got_bits[:, 2 * N_META:2 * N_META + 2 * N_ODD * 3].reshape(4, N_ODD, 3, SIDE_W, 2), F32)
    small = {n: weights[n] for n in REPLICATED}
    small["conv_w"] = conv_full.transpose(1, 2, 0, 3).reshape(N_ODD, 3, D_MODEL)

    loss_local, grad_x, g, grads = _local_step(x[0], loss_target[0], meta_full, small, gathered)
    loss = lax.psum(loss_local, MESH_AXES)

    rep = jnp.concatenate([grads[n].reshape(-1) for n in REPLICATED])
    rep = jnp.pad(rep, (0, (SMALL_ROWS - SMALL_REP) * SIDE_W - rep.shape[0])).reshape(SMALL_ROWS - SMALL_REP, SIDE_W)
    g_meta = grads["meta_tokens"].reshape(N_META, 4, SIDE_W).transpose(1, 0, 2)
    g_conv = grads["conv_w"].reshape(N_ODD * 3, 4, SIDE_W).transpose(1, 0, 2)
    small4 = jnp.concatenate([g_meta, g_conv, jnp.zeros((4, SMALL_REP - SMALL_CONV - N_ODD * 3, SIDE_W), F32),
                              jnp.broadcast_to(rep[None], (4,) + rep.shape)], axis=1)
    g = _put(g, small4, PK_SMALL, name="pack_small_grads")
    half = PK_ROWS // 2
    c = lax.axis_index("c")
    got = _swap_halves(g)
    kept = lax.dynamic_slice_in_dim(g, c * half, half, axis=1)
    pair = _add2(kept.reshape(4 * half, PACK_W), got.reshape(4 * half, PACK_W), out_dtype=jnp.bfloat16,
                 name="grad_pair_sum").reshape(4, half, PACK_W)
    total_half = _sum4(_scatter_chips(pair), name="grad_chip_sum")
    g_tot = _join_halves(total_half).reshape(PK_ROWS, PACK_W)

    out = {}
    for name, place in MATRIX_PLACES:
        shape = weights[name].shape
        two_d = lambda a: a.reshape(-1, shape[-1])
        res = _adamw(two_d(weights[name]), two_d(mom_m[name]), two_d(mom_v[name]), g_tot, place[0], place[1])
        out[name] = [r.reshape(shape) for r in res]

    def small_pack(src):
        flat = jnp.concatenate([src[n].reshape(-1) for n in REPLICATED])
        flat = jnp.pad(flat, (0, (SMALL_ROWS - SMALL_REP) * SIDE_W - flat.shape[0])).reshape(SMALL_ROWS - SMALL_REP, SIDE_W)
        return jnp.concatenate([src["meta_tokens"], src["conv_w"].reshape(N_ODD * 3, SIDE_W),
                                jnp.zeros((SMALL_REP - SMALL_CONV - N_ODD * 3, SIDE_W), F32), flat])

    res = _adamw(small_pack(weights), small_pack(mom_m), small_pack(mom_v), g_tot, PK_SMALL[0], PK_SMALL[1])
    for name in ("meta_tokens", "conv_w") + REPLICATED:
        out[name] = []
    for r in res:
        out["meta_tokens"].append(r[SMALL_META:SMALL_META + N_META])
        out["conv_w"].append(r[SMALL_CONV:SMALL_CONV + N_ODD * 3].reshape(N_ODD, 3, SIDE_W))
        flat, off = r[SMALL_REP:].reshape(-1), 0
        for name in REPLICATED:
            n = weights[name].size
            out[name].append(flat[off:off + n].reshape(weights[name].shape))
            off += n
    return (loss, grad_x[None], *[out[n][0] for n in WEIGHT_ORDER], *[out[n][1] for n in WEIGHT_ORDER],
            *[out[n][2] for n in WEIGHT_ORDER], *[out[n][3] for n in WEIGHT_ORDER])
```

```python
import functools

import jax
import jax.numpy as jnp
from jax import lax
from jax.experimental import pallas as pl
from jax.experimental.pallas import tpu as pltpu

F32 = jnp.float32
MXU_DTYPE = jnp.bfloat16

D_MODEL = 1024
N_META = 16
LANES = 128
PAD = LANES - N_META
HEADS = 8
Q_LORA = 384
KV_LORA = 256
MLA_NOPE = 64
MLA_ROPE = 32
MLA_QK = MLA_NOPE + MLA_ROPE
HEAD_V = 64
FOX_DIM = 64
ROPE_BASE = 10000.0
D_FF = 4 * D_MODEL
DEPTH = 4
EPS = 1e-6
NEG = -1e30
ATTN_IN = Q_LORA + KV_LORA + MLA_ROPE + 3 * HEADS * FOX_DIM + HEADS

ZA_CQ = 0
ZA_CKV = Q_LORA
ZA_FQ = ZA_CKV + KV_LORA
ZA_FK = ZA_FQ + HEADS * FOX_DIM
ZA_FV = ZA_FK + HEADS * FOX_DIM
ZA_LAST = ZA_FV + HEADS * FOX_DIM
ZA_W = ZA_LAST + LANES
KPE_LANE = MLA_NOPE

ADAM_LR = 0.001
ADAM_B1 = 0.9
ADAM_B2 = 0.999
ADAM_EPS = 1e-08
ADAM_WD = 0.01
ADAM_STEP = 10

VMEM_LIMIT_BYTES = 52 * 1024 * 1024
ROW_TILE = 640
PREP_TILE = 320
ATTN_BLOCK = 640
LOSS_TILE = 128
MAX_TILE = 1536

MESH_AXES = ("x", "y", "c")
MESH = pl.DeviceIdType.MESH


def _params(*sem):
    return pltpu.CompilerParams(dimension_semantics=sem, vmem_limit_bytes=VMEM_LIMIT_BYTES)


def _tile(n, cap=None):
    cap = MAX_TILE if cap is None else cap
    if n <= cap:
        return n
    best = None
    for t in range(LANES, cap + 1, LANES):
        if n % t == 0:
            best = t
    assert best is not None, n
    return best


def _iota(shape, dim):
    return lax.broadcasted_iota(jnp.int32, shape, dim)


def _matmul(a, b, *, ta=False, tb=False, extras=(), epilogue=None, out_dtypes=(F32,), name, b_tiles=None,
            out_into=None, tm=None, tn=None):
    if ta:
        kdim, m = a.shape
    else:
        m, kdim = a.shape
    row_tile = min(ROW_TILE, m)
    if ta:
        tm_auto, tk = _tile(m), min(ROW_TILE, kdim)
    else:
        tm_auto, tk = (row_tile if m % row_tile == 0 else _tile(m)), _tile(kdim)
    tm = tm_auto if tm is None else tm
    if b_tiles is None:
        n = b.shape[0] if tb else b.shape[1]
        assert (b.shape[1] if tb else b.shape[0]) == kdim, (a.shape, b.shape, ta, tb)
        tn = _tile(n) if tn is None else tn
        b_spec = pl.BlockSpec((tn, tk), lambda i, j, k: (j, k)) if tb else pl.BlockSpec((tk, tn), lambda i, j, k: (k, j))
    else:
        n, tn, tk, b_spec = b_tiles["n"], b_tiles["tn"], b_tiles["tk"], b_tiles["spec"]
    nm, nn, nk = m // tm, n // tn, kdim // tk
    assert nm * tm == m and nn * tn == n and nk * tk == kdim, (m, n, kdim, tm, tn, tk)
    n_ex, n_out = len(extras), len(out_dtypes)
    n_alias = 0 if out_into is None else 1
    assert n_out == 1 or out_into is None
    dims = (((0 if ta else 1,), (1 if tb else 0,)), ((), ()))
    if epilogue is None:
        epilogue = lambda acc: (acc,)

    def body(a_ref, b_ref, *rest):
        ex_refs, out_refs, acc_ref = rest[:n_ex], rest[n_ex + n_alias:n_ex + n_alias + n_out], rest[-1]
        k = pl.program_id(2)

        @pl.when(k == 0)
        def _():
            acc_ref[...] = jnp.zeros_like(acc_ref)

        acc_ref[...] += lax.dot_general(a_ref[...].astype(MXU_DTYPE), b_ref[...].astype(MXU_DTYPE), dims,
                                        preferred_element_type=F32)

        @pl.when(k == nk - 1)
        def _():
            res = epilogue(acc_ref[...], *[e[...] for e in ex_refs])
            for o_ref, r in zip(out_refs, res):
                o_ref[...] = r.astype(o_ref.dtype)

    a_spec = pl.BlockSpec((tk, tm), lambda i, j, k: (k, i)) if ta else pl.BlockSpec((tm, tk), lambda i, j, k: (i, k))
    mn_spec = pl.BlockSpec((tm, tn), lambda i, j, k: (i, j))
    if out_into is None:
        outs = pl.pallas_call(
            body, name=name, grid=(nm, nn, nk),
            in_specs=[a_spec, b_spec] + [mn_spec] * n_ex,
            out_specs=[mn_spec] * n_out,
            out_shape=[jax.ShapeDtypeStruct((m, n), dt) for dt in out_dtypes],
            scratch_shapes=[pltpu.VMEM((tm, tn), F32)],
            compiler_params=_params("parallel", "parallel", "arbitrary"),
        )(a, b, *extras)
        return outs[0] if n_out == 1 else outs
    buf, buf_spec = out_into
    return pl.pallas_call(
        body, name=name, grid=(nm, nn, nk),
        in_specs=[a_spec, b_spec] + [mn_spec] * n_ex + [ANY],
        out_specs=buf_spec,
        out_shape=jax.ShapeDtypeStruct(buf.shape, buf.dtype),
        input_output_aliases={2 + n_ex: 0},
        scratch_shapes=[pltpu.VMEM((tm, tn), F32)],
        compiler_params=_params("parallel", "parallel", "arbitrary"),
    )(a, b, *extras, buf)


def _rmsnorm_fwd(x, g, *, name="rmsnorm_fwd"):
    t, d = x.shape
    bt = min(ROW_TILE, t)

    def body(x_ref, g_ref, o_ref):
        xv = x_ref[...]
        r = lax.rsqrt(jnp.mean(xv * xv, axis=-1, keepdims=True) + EPS)
        o_ref[...] = (xv * r * g_ref[...]).astype(o_ref.dtype)

    return pl.pallas_call(
        body, name=name, grid=(t // bt,),
        in_specs=[pl.BlockSpec((bt, d), lambda i: (i, 0)), pl.BlockSpec((1, d), lambda i: (0, 0))],
        out_specs=pl.BlockSpec((bt, d), lambda i: (i, 0)),
        out_shape=jax.ShapeDtypeStruct((t, d), MXU_DTYPE),
        compiler_params=_params("parallel"),
    )(x, g.reshape(1, d))


def _rmsnorm_bwd(x, g, dy, dres, *, name="rmsnorm_bwd"):
    t, d = x.shape
    bt = min(ROW_TILE, t)

    def body(x_ref, g_ref, dy_ref, dres_ref, dx_ref, dg_ref):
        @pl.when(pl.program_id(0) == 0)
        def _():
            dg_ref[...] = jnp.zeros_like(dg_ref)

        xv, dyv = x_ref[...], dy_ref[...].astype(F32)
        r = lax.rsqrt(jnp.mean(xv * xv, axis=-1, keepdims=True) + EPS)
        xhat = xv * r
        dxhat = dyv * g_ref[...]
        dx = r * (dxhat - xhat * jnp.mean(dxhat * xhat, axis=-1, keepdims=True))
        dx_ref[...] = dres_ref[...] + dx
        dg_ref[...] += jnp.sum(dyv * xhat, axis=0, keepdims=True)

    row = pl.BlockSpec((bt, d), lambda i: (i, 0))
    vec = pl.BlockSpec((1, d), lambda i: (0, 0))
    dx, dg = pl.pallas_call(
        body, name=name, grid=(t // bt,),
        in_specs=[row, vec, row, row], out_specs=[row, vec],
        out_shape=[jax.ShapeDtypeStruct((t, d), F32), jax.ShapeDtypeStruct((1, d), F32)],
        compiler_params=_params("arbitrary"),
    )(x, g.reshape(1, d), dy, dres)
    return dx, dg.reshape(d)


def _pair_rms(x, lo):
    x2 = x * x
    s_lo = jnp.sum(jnp.where(lo, x2, 0.0), axis=-1, keepdims=True)
    s_hi = jnp.sum(jnp.where(lo, 0.0, x2), axis=-1, keepdims=True)
    return jnp.where(lo, lax.rsqrt(s_lo / FOX_DIM + EPS), lax.rsqrt(s_hi / FOX_DIM + EPS))


def _pair_sum(x, lo):
    s_lo = jnp.sum(jnp.where(lo, x, 0.0), axis=-1, keepdims=True)
    s_hi = jnp.sum(jnp.where(lo, 0.0, x), axis=-1, keepdims=True)
    return jnp.where(lo, s_lo, s_hi)


def _prep_a_fwd(z, g_cq, g_ckv, g_qf, g_kf, b_f):
    t = z.shape[0]
    bt = min(PREP_TILE, t)
    hw = HEADS * FOX_DIM

    def body(z_ref, gcq_ref, gckv_ref, gqf_ref, gkf_ref, bf_ref, cqn_ref, ckvn_ref, qf_ref, kf_ref, vf_ref, logf_ref):
        i = pl.program_id(0)
        cq = z_ref[:, ZA_CQ:ZA_CQ + Q_LORA]
        cqn_ref[...] = (cq * lax.rsqrt(jnp.mean(cq * cq, axis=-1, keepdims=True) + EPS) * gcq_ref[...]).astype(cqn_ref.dtype)
        ckv = z_ref[:, ZA_CKV:ZA_CKV + KV_LORA]
        ckvn_ref[...] = (ckv * lax.rsqrt(jnp.mean(ckv * ckv, axis=-1, keepdims=True) + EPS) * gckv_ref[...]).astype(ckvn_ref.dtype)
        lo = _iota((1, LANES), 1) < FOX_DIM
        for p in range(HEADS // 2):
            sl = slice(p * LANES, (p + 1) * LANES)
            xq = z_ref[:, ZA_FQ + p * LANES:ZA_FQ + (p + 1) * LANES]
            qf_ref[:, sl] = (xq * _pair_rms(xq, lo) * gqf_ref[...]).astype(qf_ref.dtype)
            xk = z_ref[:, ZA_FK + p * LANES:ZA_FK + (p + 1) * LANES]
            kf_ref[:, sl] = (xk * _pair_rms(xk, lo) * gkf_ref[...]).astype(kf_ref.dtype)
        vf_ref[...] = z_ref[:, ZA_FV:ZA_FV + hw].astype(vf_ref.dtype)
        xl = z_ref[:, ZA_LAST:ZA_LAST + LANES] + bf_ref[...]
        logf = jnp.minimum(xl, 0.0) - jnp.log(1.0 + jnp.exp(-jnp.abs(xl)))
        row = i * bt + _iota((bt, LANES), 0)
        lane = _iota((bt, LANES), 1)
        logf_ref[...] = jnp.where((lane < HEADS) & (row >= PAD), logf, 0.0)

    def vec(w):
        return pl.BlockSpec((1, w), lambda i: (0, 0))

    def row(w):
        return pl.BlockSpec((bt, w), lambda i: (i, 0))

    return pl.pallas_call(
        body, name="attn_prep_a_fwd", grid=(t // bt,),
        in_specs=[row(ZA_W), vec(Q_LORA), vec(KV_LORA), vec(LANES), vec(LANES), vec(LANES)],
        out_specs=[row(Q_LORA), row(KV_LORA), row(hw), row(hw), row(hw), row(LANES)],
        out_shape=[jax.ShapeDtypeStruct((t, Q_LORA), MXU_DTYPE), jax.ShapeDtypeStruct((t, KV_LORA), MXU_DTYPE),
                   jax.ShapeDtypeStruct((t, hw), MXU_DTYPE), jax.ShapeDtypeStruct((t, hw), MXU_DTYPE),
                   jax.ShapeDtypeStruct((t, hw), MXU_DTYPE), jax.ShapeDtypeStruct((t, LANES), F32)],
        compiler_params=_params("parallel"),
    )(z, g_cq, g_ckv, g_qf, g_kf, b_f)


def _prep_a_bwd(z, g_cq, g_ckv, g_qf, g_kf, b_f, dcqn, dckvn, dqf, dkf, dvf, dlogf, dkpe):
    t = z.shape[0]
    bt = min(PREP_TILE, t)
    hw = HEADS * FOX_DIM

    def norm_bwd(x, g, dy):
        r = lax.rsqrt(jnp.mean(x * x, axis=-1, keepdims=True) + EPS)
        xhat = x * r
        dxhat = dy * g
        dx = r * (dxhat - xhat * jnp.mean(dxhat * xhat, axis=-1, keepdims=True))
        return dx, jnp.sum(dy * xhat, axis=0, keepdims=True)

    def body(z_ref, gcq_ref, gckv_ref, gqf_ref, gkf_ref, bf_ref, dcqn_ref, dckvn_ref, dqf_ref, dkf_ref, dvf_ref,
             dlogf_ref, dkpe_ref, dz_ref, dgcq_ref, dgckv_ref, dgqf_ref, dgkf_ref, dbf_ref):
        i = pl.program_id(0)

        @pl.when(i == 0)
        def _():
            for r in (dgcq_ref, dgckv_ref, dgqf_ref, dgkf_ref, dbf_ref):
                r[...] = jnp.zeros_like(r)

        dx, dg = norm_bwd(z_ref[:, ZA_CQ:ZA_CQ + Q_LORA], gcq_ref[...], dcqn_ref[...])
        dz_ref[:, ZA_CQ:ZA_CQ + Q_LORA] = dx.astype(dz_ref.dtype)
        dgcq_ref[...] += dg
        dx, dg = norm_bwd(z_ref[:, ZA_CKV:ZA_CKV + KV_LORA], gckv_ref[...], dckvn_ref[...])
        dz_ref[:, ZA_CKV:ZA_CKV + KV_LORA] = dx.astype(dz_ref.dtype)
        dgckv_ref[...] += dg
        lo = _iota((1, LANES), 1) < FOX_DIM
        for base, g_ref, dy_ref, dg_ref in ((ZA_FQ, gqf_ref, dqf_ref, dgqf_ref), (ZA_FK, gkf_ref, dkf_ref, dgkf_ref)):
            for p in range(HEADS // 2):
                x = z_ref[:, base + p * LANES:base + (p + 1) * LANES]
                dy = dy_ref[:, p * LANES:(p + 1) * LANES]
                r = _pair_rms(x, lo)
                xhat = x * r
                dxhat = dy * g_ref[...]
                dx = r * (dxhat - xhat * _pair_sum(dxhat * xhat, lo) / FOX_DIM)
                dz_ref[:, base + p * LANES:base + (p + 1) * LANES] = dx.astype(dz_ref.dtype)
                dg_ref[...] += jnp.sum(dy * xhat, axis=0, keepdims=True)
        dz_ref[:, ZA_FV:ZA_FV + hw] = dvf_ref[...].astype(dz_ref.dtype)
        xl = z_ref[:, ZA_LAST:ZA_LAST + LANES] + bf_ref[...]
        row = i * bt + _iota((bt, LANES), 0)
        lane = _iota((bt, LANES), 1)
        dfl = jnp.where((lane < HEADS) & (row >= PAD), dlogf_ref[...] / (1.0 + jnp.exp(xl)), 0.0)
        dbf_ref[...] += jnp.sum(dfl, axis=0, keepdims=True)
        dz_ref[:, ZA_LAST:ZA_LAST + LANES] = (dfl + dkpe_ref[...]).astype(dz_ref.dtype)

    def vec(w):
        return pl.BlockSpec((1, w), lambda i: (0, 0))

    def row(w):
        return pl.BlockSpec((bt, w), lambda i: (i, 0))

    return pl.pallas_call(
        body, name="attn_prep_a_bwd", grid=(t // bt,),
        in_specs=[row(ZA_W), vec(Q_LORA), vec(KV_LORA), vec(LANES), vec(LANES), vec(LANES),
                  row(Q_LORA), row(KV_LORA), row(hw), row(hw), row(hw), row(LANES), row(LANES)],
        out_specs=[row(ZA_W), vec(Q_LORA), vec(KV_LORA), vec(LANES), vec(LANES), vec(LANES)],
        out_shape=[jax.ShapeDtypeStruct((t, ZA_W), MXU_DTYPE), jax.ShapeDtypeStruct((1, Q_LORA), F32),
                   jax.ShapeDtypeStruct((1, KV_LORA), F32), jax.ShapeDtypeStruct((1, LANES), F32),
                   jax.ShapeDtypeStruct((1, LANES), F32), jax.ShapeDtypeStruct((1, LANES), F32)],
        compiler_params=_params("arbitrary"),
    )(z, g_cq, g_ckv, g_qf, g_kf, b_f, dcqn, dckvn, dqf, dkf, dvf, dlogf, dkpe)


def _cumsum_rows(x, *, reverse, name):
    t = x.shape[0]
    nblk = t // LANES

    def body(x_ref, f_ref, ft_ref, carry_ref):
        r = _iota((LANES, LANES), 0)
        c = _iota((LANES, LANES), 1)
        tri = jnp.where((c >= r) if reverse else (c <= r), 1.0, 0.0).astype(F32)
        carry_ref[...] = jnp.zeros_like(carry_ref)

        def step(s, _):
            b = (nblk - 1 - s) if reverse else s
            start = pl.multiple_of(b * LANES, LANES)
            blk = x_ref[pl.ds(start, LANES), :]
            cs = jnp.dot(tri, blk, precision=lax.Precision.HIGHEST, preferred_element_type=F32) + carry_ref[0:1, :]
            f_ref[pl.ds(start, LANES), :] = cs
            ft_ref[:, pl.ds(start, LANES)] = cs.T
            carry_ref[0:1, :] = cs[0:1, :] if reverse else cs[LANES - 1:LANES, :]
            return 0

        lax.fori_loop(0, nblk, step, 0)

    return pl.pallas_call(
        body, name=name, grid=(1,),
        in_specs=[pl.BlockSpec((t, LANES), lambda i: (0, 0))],
        out_specs=[pl.BlockSpec((t, LANES), lambda i: (0, 0)), pl.BlockSpec((LANES, t), lambda i: (0, 0))],
        out_shape=[jax.ShapeDtypeStruct((t, LANES), F32), jax.ShapeDtypeStruct((LANES, t), F32)],
        scratch_shapes=[pltpu.VMEM((8, LANES), F32)],
        compiler_params=_params("arbitrary"),
    )(x)


def _rope_partner(x, lane):
    half = MLA_ROPE // 2
    swapped = jnp.where(lane < KPE_LANE + half, pltpu.roll(x, LANES - half, axis=1), pltpu.roll(x, half, axis=1))
    return jnp.where((lane >= KPE_LANE) & (lane < KPE_LANE + MLA_ROPE), swapped, 0.0)


def _rope_tables(t):
    pos = (jnp.arange(t, dtype=jnp.int32) - PAD).astype(F32)
    inv_freq = ROPE_BASE ** (-jnp.arange(0, MLA_ROPE, 2, dtype=F32) / MLA_ROPE)
    ang = pos[:, None] * inv_freq[None, :]
    cos, sin = jnp.cos(ang), jnp.sin(ang)
    ones = jnp.ones((t, KPE_LANE), F32)
    tail = jnp.zeros((t, LANES - KPE_LANE - MLA_ROPE), F32)
    c_tab = jnp.concatenate([ones, cos, cos, tail + 1.0], axis=1)
    s_tab = jnp.concatenate([ones * 0.0, -sin, sin, tail], axis=1)
    return c_tab, s_tab


def _prep_b_fwd(q_raw, kv_raw, z, c_tab, s_tab, g_q, g_k):
    t = q_raw.shape[0]
    bt = min(PREP_TILE, t)
    qw = HEADS * LANES
    vw = HEADS * HEAD_V

    def body(q_ref, kv_ref, zl_ref, c_ref, s_ref, gq_ref, gk_ref, qn_ref, kn_ref, v_ref):
        lane = _iota((1, LANES), 1)
        kpe = jnp.where((lane >= KPE_LANE) & (lane < KPE_LANE + MLA_ROPE), zl_ref[...], 0.0)
        cv, sv = c_ref[...], s_ref[...]
        for h in range(HEADS):
            sl = slice(h * LANES, (h + 1) * LANES)
            for x, g_ref, o_ref in ((q_ref[:, sl], gq_ref, qn_ref), (kv_ref[:, sl] + kpe, gk_ref, kn_ref)):
                r = lax.rsqrt(jnp.sum(x * x, axis=-1, keepdims=True) / MLA_QK + EPS)
                xn = x * r * g_ref[...]
                o_ref[:, sl] = (xn * cv + _rope_partner(xn, lane) * sv).astype(o_ref.dtype)
        v_ref[...] = kv_ref[:, qw:qw + vw].astype(v_ref.dtype)

    def row(w):
        return pl.BlockSpec((bt, w), lambda i: (i, 0))

    vec = pl.BlockSpec((1, LANES), lambda i: (0, 0))
    return pl.pallas_call(
        body, name="attn_prep_b_fwd", grid=(t // bt,),
        in_specs=[row(qw), row(qw + vw), pl.BlockSpec((bt, LANES), lambda i: (i, ZA_LAST // LANES)),
                  row(LANES), row(LANES), vec, vec],
        out_specs=[row(qw), row(qw), row(vw)],
        out_shape=[jax.ShapeDtypeStruct((t, qw), MXU_DTYPE), jax.ShapeDtypeStruct((t, qw), MXU_DTYPE),
                   jax.ShapeDtypeStruct((t, vw), MXU_DTYPE)],
        compiler_params=_params("parallel"),
    )(q_raw, kv_raw, z, c_tab, s_tab, g_q, g_k)


def _prep_b_bwd(q_raw, kv_raw, z, c_tab, s_tab, g_q, g_k, dqn, dkn, dv):
    t = q_raw.shape[0]
    bt = min(PREP_TILE, t)
    qw = HEADS * LANES
    vw = HEADS * HEAD_V

    def body(q_ref, kv_ref, zl_ref, c_ref, s_ref, gq_ref, gk_ref, dqn_ref, dkn_ref, dv_ref,
             dq_ref, dkv_ref, dkpe_ref, dgq_ref, dgk_ref):
        @pl.when(pl.program_id(0) == 0)
        def _():
            dgq_ref[...] = jnp.zeros_like(dgq_ref)
            dgk_ref[...] = jnp.zeros_like(dgk_ref)

        lane = _iota((1, LANES), 1)
        rope_lanes = (lane >= KPE_LANE) & (lane < KPE_LANE + MLA_ROPE)
        kpe = jnp.where(rope_lanes, zl_ref[...], 0.0)
        cv, sv = c_ref[...], s_ref[...]
        dkpe = jnp.zeros((bt, LANES), F32)
        for h in range(HEADS):
            sl = slice(h * LANES, (h + 1) * LANES)
            for is_k, x, g_ref, dout, dg_ref in ((False, q_ref[:, sl], gq_ref, dqn_ref[:, sl], dgq_ref),
                                                  (True, kv_ref[:, sl] + kpe, gk_ref, dkn_ref[:, sl], dgk_ref)):
                r = lax.rsqrt(jnp.sum(x * x, axis=-1, keepdims=True) / MLA_QK + EPS)
                xhat = x * r
                dxn = dout * cv + _rope_partner(dout * sv, lane)
                dg_ref[...] += jnp.sum(dxn * xhat, axis=0, keepdims=True)
                dxhat = dxn * g_ref[...]
                dx = r * (dxhat - xhat * (jnp.sum(dxhat * xhat, axis=-1, keepdims=True) / MLA_QK))
                if is_k:
                    dkv_ref[:, sl] = jnp.where(lane < KPE_LANE, dx, 0.0).astype(dkv_ref.dtype)
                    dkpe = dkpe + jnp.where(rope_lanes, dx, 0.0)
                else:
                    dq_ref[:, sl] = dx.astype(dq_ref.dtype)
        dkv_ref[:, qw:qw + vw] = dv_ref[...].astype(dkv_ref.dtype)
        dkpe_ref[...] = dkpe

    def row(w):
        return pl.BlockSpec((bt, w), lambda i: (i, 0))

    vec = pl.BlockSpec((1, LANES), lambda i: (0, 0))
    return pl.pallas_call(
        body, name="attn_prep_b_bwd", grid=(t // bt,),
        in_specs=[row(qw), row(qw + vw), pl.BlockSpec((bt, LANES), lambda i: (i, ZA_LAST // LANES)),
                  row(LANES), row(LANES), vec, vec, row(qw), row(qw), row(vw)],
        out_specs=[row(qw), row(qw + vw), row(LANES), vec, vec],
        out_shape=[jax.ShapeDtypeStruct((t, qw), MXU_DTYPE), jax.ShapeDtypeStruct((t, qw + vw), MXU_DTYPE),
                   jax.ShapeDtypeStruct((t, LANES), F32), jax.ShapeDtypeStruct((1, LANES), F32),
                   jax.ShapeDtypeStruct((1, LANES), F32)],
        compiler_params=_params("arbitrary"),
    )(q_raw, kv_raw, z, c_tab, s_tab, g_q, g_k, dqn, dkn, dv)


NT_DIMS = (((1,), (1,)), ((), ()))
TN_DIMS = (((0,), (0,)), ((), ()))


def _head_qk(q_ref, k_ref, e, mla, lo):
    if mla:
        return q_ref[:, e * LANES:(e + 1) * LANES], k_ref[:, e * LANES:(e + 1) * LANES]
    q = q_ref[...]
    return jnp.where(lo if e == 0 else jnp.logical_not(lo), q, jnp.zeros_like(q)), k_ref[...]


def _attn_specs(mla, blk, q_map, k_map):
    w = 2 * LANES if mla else LANES
    q_spec = pl.BlockSpec((blk, w), lambda j, a, b: (q_map(a, b), j))
    k_spec = pl.BlockSpec((blk, w), lambda j, a, b: (k_map(a, b), j))
    qv_spec = pl.BlockSpec((blk, LANES), lambda j, a, b: (q_map(a, b), j))
    kv_spec = pl.BlockSpec((blk, LANES), lambda j, a, b: (k_map(a, b), j))
    fq_spec = pl.BlockSpec((blk, LANES), lambda j, a, b: (q_map(a, b), 0))
    fk_spec = pl.BlockSpec((8, blk), lambda j, a, b: (0, k_map(a, b)))
    return q_spec, k_spec, qv_spec, kv_spec, fq_spec, fk_spec


def _flash_fwd(q, k, v, f, f_t, *, mla, scale, name):
    t = q.shape[0]
    blk = min(ATTN_BLOCK, t)
    nb = t // blk
    pairs = HEADS // 2
    q_spec, k_spec, qv_spec, kv_spec, fq_spec, fk_spec = _attn_specs(
        mla, blk, lambda i, kk: i, lambda i, kk: jnp.minimum(kk, i))

    def body(*refs):
        if mla:
            q_ref, k_ref, v_ref, o_ref, lse_ref, m_s, l_s, acc_s = refs
            fq_ref = fk_ref = None
        else:
            q_ref, k_ref, v_ref, fq_ref, fk_ref, o_ref, lse_ref, m_s, l_s, acc_s = refs
        j, i, kk = pl.program_id(0), pl.program_id(1), pl.program_id(2)
        lo = _iota((1, LANES), 1) < HEAD_V

        @pl.when(kk == 0)
        def _():
            m_s[...] = jnp.full_like(m_s, NEG)
            l_s[...] = jnp.zeros_like(l_s)
            acc_s[...] = jnp.zeros_like(acc_s)

        def step(mask):
            vv = v_ref[...]
            for e in range(2):
                s, _, _ = _scores(q_ref, k_ref, fq_ref, fk_ref, e, j, mla, scale, lo, mask)
                m_prev = m_s[e]
                m_new = jnp.maximum(m_prev, jnp.max(s, axis=1, keepdims=True))
                alpha = jnp.exp(m_prev - m_new)
                p = jnp.exp(s - m_new)
                if mask is not None:
                    p = jnp.where(mask, p, 0.0)
                l_s[e] = alpha * l_s[e] + jnp.sum(p, axis=1, keepdims=True)
                acc_s[e] = alpha * acc_s[e] + jnp.dot(p.astype(MXU_DTYPE), vv, preferred_element_type=F32)
                m_s[e] = m_new

        _masked_and_plain(kk <= i, i, kk, blk, step)

        @pl.when(kk == nb - 1)
        def _():
            valid = (i * blk + _iota((blk, 1), 0)) >= PAD
            outs, lses = [], []
            for e in range(2):
                l = l_s[e]
                outs.append(acc_s[e] * jnp.where(l > 0.0, 1.0 / jnp.where(l > 0.0, l, 1.0), 0.0))
                lses.append(m_s[e] + jnp.log(jnp.where(l > 0.0, l, 1.0)))
            o = jnp.where(lo, outs[0], outs[1])
            o_ref[...] = jnp.where(valid, o, 0.0).astype(o_ref.dtype)
            lane = _iota((1, LANES), 1)
            lse_ref[...] = jnp.where(lane == 0, lses[0], jnp.where(lane == 1, lses[1], 0.0))

    in_specs = [q_spec, k_spec, kv_spec] + ([] if mla else [fq_spec, fk_spec])
    args = (q, k, v) + (() if mla else (f, f_t))
    hv = HEADS * HEAD_V
    return pl.pallas_call(
        body, name=name, grid=(pairs, nb, nb),
        in_specs=in_specs, out_specs=[qv_spec, qv_spec],
        out_shape=[jax.ShapeDtypeStruct((t, hv), F32), jax.ShapeDtypeStruct((t, hv), F32)],
        scratch_shapes=[pltpu.VMEM((2, blk, 1), F32), pltpu.VMEM((2, blk, 1), F32), pltpu.VMEM((2, blk, LANES), F32)],
        compiler_params=_params("parallel", "parallel", "arbitrary"),
    )(*args)


def _bwd_tile(q_ref, k_ref, v_ref, o_ref, do_ref, lse_ref, fq_ref, fk_ref, e, pair, mla, scale, lo, mask):
    s, qe, ke = _scores(q_ref, k_ref, fq_ref, fk_ref, e, pair, mla, scale, lo, mask)
    p = jnp.exp(s - lse_ref[:, e:e + 1])
    if mask is not None:
        p = jnp.where(mask, p, 0.0)
    do = do_ref[...]
    doe = jnp.where(lo if e == 0 else jnp.logical_not(lo), do, jnp.zeros_like(do))
    dp = lax.dot_general(doe, v_ref[...], NT_DIMS, preferred_element_type=F32)
    delta = jnp.sum(doe.astype(F32) * o_ref[...].astype(F32), axis=1, keepdims=True)
    return p, p * (dp - delta), qe, ke


def _flash_bwd_dq(q, k, v, o, do, lse, f, f_t, *, mla, scale, col0, name):
    t = q.shape[0]
    blk = min(ATTN_BLOCK, t)
    nb = t // blk
    pairs = HEADS // 2
    w = 2 * LANES if mla else LANES
    q_spec, k_spec, qv_spec, kv_spec, fq_spec, fk_spec = _attn_specs(
        mla, blk, lambda i, kk: i, lambda i, kk: jnp.minimum(kk, i))
    od_spec = pl.BlockSpec((blk, LANES), lambda j, i, kk: (i, col0 + j))

    def body(*refs):
        if mla:
            q_ref, k_ref, v_ref, o_ref, do_ref, lse_ref, dq_ref, dq_s = refs
            fq_ref = fk_ref = rs_ref = rs_s = None
        else:
            q_ref, k_ref, v_ref, o_ref, do_ref, lse_ref, fq_ref, fk_ref, dq_ref, rs_ref, dq_s, rs_s = refs
        j, i, kk = pl.program_id(0), pl.program_id(1), pl.program_id(2)
        lo = _iota((1, LANES), 1) < HEAD_V

        @pl.when(kk == 0)
        def _():
            dq_s[...] = jnp.zeros_like(dq_s)
            if not mla:
                rs_s[...] = jnp.zeros_like(rs_s)

        def step(mask):
            for e in range(2):
                _, ds, _, ke = _bwd_tile(q_ref, k_ref, v_ref, o_ref, do_ref, lse_ref, fq_ref, fk_ref, e, j, mla, scale,
                                         lo, mask)
                dq_s[e] += jnp.dot(ds.astype(MXU_DTYPE), ke, preferred_element_type=F32)
                if not mla:
                    rs_s[e] += jnp.sum(ds, axis=1, keepdims=True)

        _masked_and_plain(kk <= i, i, kk, blk, step)

        @pl.when(kk == nb - 1)
        def _():
            if mla:
                dq_ref[:, 0:LANES] = dq_s[0] * scale
                dq_ref[:, LANES:2 * LANES] = dq_s[1] * scale
            else:
                dq_ref[...] = jnp.where(lo, dq_s[0], dq_s[1]) * scale
                lane = _iota((1, LANES), 1)
                rs_ref[...] = jnp.where(lane == 0, rs_s[0], jnp.where(lane == 1, rs_s[1], 0.0))

    in_specs = [q_spec, k_spec, kv_spec, od_spec, od_spec, qv_spec] + ([] if mla else [fq_spec, fk_spec])
    args = (q, k, v, o, do, lse) + (() if mla else (f, f_t))
    out_specs = [q_spec] + ([] if mla else [qv_spec])
    out_shape = [jax.ShapeDtypeStruct((t, pairs * w), F32)]
    scratch = [pltpu.VMEM((2, blk, LANES), F32)]
    if not mla:
        out_shape.append(jax.ShapeDtypeStruct((t, pairs * LANES), F32))
        scratch.append(pltpu.VMEM((2, blk, 1), F32))
    outs = pl.pallas_call(
        body, name=name, grid=(pairs, nb, nb),
        in_specs=in_specs, out_specs=out_specs, out_shape=out_shape, scratch_shapes=scratch,
        compiler_params=_params("parallel", "parallel", "arbitrary"),
    )(*args)
    return outs[0] if mla else outs


def _flash_bwd_dkv(q, k, v, o, do, lse, f, f_t, *, mla, scale, col0, name):
    t = q.shape[0]
    blk = min(ATTN_BLOCK, t)
    nb = t // blk
    pairs = HEADS // 2
    w = 2 * LANES if mla else LANES
    q_spec, k_spec, qv_spec, kv_spec, fq_spec, fk_spec = _attn_specs(
        mla, blk, lambda a, b: jnp.maximum(a, b), lambda a, b: a)
    od_spec = pl.BlockSpec((blk, LANES), lambda j, a, b: (jnp.maximum(a, b), col0 + j))
    cs_spec = pl.BlockSpec((8, blk), lambda j, a, b: (j, a))

    def body(*refs):
        if mla:
            q_ref, k_ref, v_ref, o_ref, do_ref, lse_ref, dk_ref, dv_ref, dk_s, dv_s = refs
            fq_ref = fk_ref = cs_ref = cs_s = None
        else:
            q_ref, k_ref, v_ref, o_ref, do_ref, lse_ref, fq_ref, fk_ref, dk_ref, dv_ref, cs_ref, dk_s, dv_s, cs_s = refs
        j, kb, qb = pl.program_id(0), pl.program_id(1), pl.program_id(2)
        lo = _iota((1, LANES), 1) < HEAD_V

        @pl.when(qb == 0)
        def _():
            dk_s[...] = jnp.zeros_like(dk_s)
            dv_s[...] = jnp.zeros_like(dv_s)
            if not mla:
                cs_s[...] = jnp.zeros_like(cs_s)

        def step(mask):
            do = do_ref[...]
            for e in range(2):
                p, ds, _, _ = _bwd_tile(q_ref, k_ref, v_ref, o_ref, do_ref, lse_ref, fq_ref, fk_ref, e, j, mla, scale,
                                        lo, mask)
                dv_s[e] += lax.dot_general(p.astype(MXU_DTYPE), do, TN_DIMS, preferred_element_type=F32)
                q_src = q_ref[:, e * LANES:(e + 1) * LANES] if mla else q_ref[...]
                dk_s[e] += lax.dot_general(ds.astype(MXU_DTYPE), q_src, TN_DIMS, preferred_element_type=F32)
                if not mla:
                    cs_s[e] += jnp.sum(ds, axis=0, keepdims=True)

        _masked_and_plain(qb >= kb, qb, kb, blk, step)

        @pl.when(qb == nb - 1)
        def _():
            dv_ref[...] = jnp.where(lo, dv_s[0], dv_s[1])
            if mla:
                dk_ref[:, 0:LANES] = dk_s[0] * scale
                dk_ref[:, LANES:2 * LANES] = dk_s[1] * scale
            else:
                dk_ref[...] = jnp.where(lo, dk_s[0], dk_s[1]) * scale
                sub = _iota((8, 1), 0)
                cs_ref[...] = jnp.where(sub == 0, cs_s[0], jnp.where(sub == 1, cs_s[1], 0.0))

    in_specs = [q_spec, k_spec, kv_spec, od_spec, od_spec, qv_spec] + ([] if mla else [fq_spec, fk_spec])
    args = (q, k, v, o, do, lse) + (() if mla else (f, f_t))
    out_specs = [k_spec, kv_spec] + ([] if mla else [cs_spec])
    out_shape = [jax.ShapeDtypeStruct((t, pairs * w), F32), jax.ShapeDtypeStruct((t, HEADS * HEAD_V), F32)]
    scratch = [pltpu.VMEM((2, blk, LANES), F32), pltpu.VMEM((2, blk, LANES), F32)]
    if not mla:
        out_shape.append(jax.ShapeDtypeStruct((pairs * 8, t), F32))
        scratch.append(pltpu.VMEM((2, 1, blk), F32))
    return pl.pallas_call(
        body, name=name, grid=(pairs, nb, nb),
        in_specs=in_specs, out_specs=out_specs, out_shape=out_shape, scratch_shapes=scratch,
        compiler_params=_params("parallel", "parallel", "arbitrary"),
    )(*args)


ATTN_CHUNK = 640


def _for_chunks(n, ch, body):
    if n == 1:
        body(0)
    else:
        lax.fori_loop(0, n, lambda c, carry: (body(pl.multiple_of(c * ch, ch)), carry)[1], 0)


def _select_lane(x, idx):
    return jnp.sum(jnp.where(_iota(x.shape, 1) == idx, x, 0.0), axis=1, keepdims=True)


def _select_row(x, idx):
    return jnp.sum(jnp.where(_iota(x.shape, 0) == idx, x, 0.0), axis=0, keepdims=True)


def _flash_fwd_chunked(q, k, v, f, f_t, *, mla, scale, name):
    t = q.shape[0]
    blk = min(ATTN_BLOCK, t)
    nb = t // blk
    pairs = HEADS // 2
    ch = min(ATTN_CHUNK, blk)
    assert blk % ch == 0
    q_spec, k_spec, qv_spec, kv_spec, fq_spec, fk_spec = _attn_specs(
        mla, blk, lambda i, kk: i, lambda i, kk: jnp.minimum(kk, i))
    lse_spec = pl.BlockSpec((8, blk), lambda j, i, kk: (j, i))

    def body(*refs):
        if mla:
            q_ref, k_ref, v_ref, o_ref, lse_ref, m_s, l_s, a_s, acc_s, s_s, p_s = refs
            fq_ref = fk_ref = None
        else:
            q_ref, k_ref, v_ref, fq_ref, fk_ref, o_ref, lse_ref, m_s, l_s, a_s, acc_s, s_s, p_s = refs
        j, i, kk = pl.program_id(0), pl.program_id(1), pl.program_id(2)
        lo = _iota((1, LANES), 1) < HEAD_V

        @pl.when(kk == 0)
        def _():
            m_s[...] = jnp.full_like(m_s, NEG)
            l_s[...] = jnp.zeros_like(l_s)
            acc_s[...] = jnp.zeros_like(acc_s)

        def step(masked):
            vv = v_ref[...]
            for e in range(2):
                qe, ke = _head_qk(q_ref, k_ref, e, mla, lo)
                s_s[...] = lax.dot_general(qe, ke, NT_DIMS, preferred_element_type=F32)
                fkr = None if mla else _select_row(fk_ref[...], 2 * j + e)

                def chunk(r0, e=e, fkr=fkr):
                    rows = pl.ds(r0, ch)
                    s = s_s[rows, :] * scale
                    if not mla:
                        s = s + _select_lane(fq_ref[rows, :], 2 * j + e) - fkr
                    if masked:
                        rpos = i * blk + r0 + _iota((ch, blk), 0)
                        cpos = kk * blk + _iota((ch, blk), 1)
                        mask = (cpos <= rpos) & (cpos >= PAD)
                        s = jnp.where(mask, s, NEG)
                    m_prev = m_s[e, rows, :]
                    m_new = jnp.maximum(m_prev, jnp.max(s, axis=1, keepdims=True))
                    alpha = jnp.exp(m_prev - m_new)
                    p = jnp.exp(s - m_new)
                    if masked:
                        p = jnp.where(mask, p, 0.0)
                    l_s[e, rows, :] = alpha * l_s[e, rows, :] + jnp.sum(p, axis=1, keepdims=True)
                    m_s[e, rows, :] = m_new
                    a_s[rows, :] = alpha
                    p_s[rows, :] = p.astype(p_s.dtype)

                _for_chunks(blk // ch, ch, chunk)
                acc_s[e] = a_s[...] * acc_s[e] + jnp.dot(p_s[...], vv, preferred_element_type=F32)

        needs_mask = (kk == i) | (kk == 0)

        @pl.when((kk <= i) & needs_mask)
        def _():
            step(True)

        @pl.when((kk <= i) & jnp.logical_not(needs_mask))
        def _():
            step(False)

        @pl.when(kk == nb - 1)
        def _():
            valid = (i * blk + _iota((blk, 1), 0)) >= PAD
            outs, lses = [], []
            for e in range(2):
                l = l_s[e]
                outs.append(acc_s[e] * jnp.where(l > 0.0, 1.0 / jnp.where(l > 0.0, l, 1.0), 0.0))
                lses.append(m_s[e] + jnp.log(jnp.where(l > 0.0, l, 1.0)))
            o = jnp.where(lo, outs[0], outs[1])
            o_ref[...] = jnp.where(valid, o, 0.0).astype(o_ref.dtype)
            lane = _iota((1, LANES), 1)
            lse_cols = jnp.where(lane == 0, lses[0], jnp.where(lane == 1, lses[1], 0.0))
            lse_ref[...] = lse_cols.T[0:8, :]

    in_specs = [q_spec, k_spec, kv_spec] + ([] if mla else [fq_spec, fk_spec])
    args = (q, k, v) + (() if mla else (f, f_t))
    hv = HEADS * HEAD_V
    return pl.pallas_call(
        body, name=name, grid=(pairs, nb, nb),
        in_specs=in_specs, out_specs=[qv_spec, lse_spec],
        out_shape=[jax.ShapeDtypeStruct((t, hv), F32), jax.ShapeDtypeStruct((pairs * 8, t), F32)],
        scratch_shapes=[pltpu.VMEM((2, blk, 1), F32), pltpu.VMEM((2, blk, 1), F32), pltpu.VMEM((blk, 1), F32),
                        pltpu.VMEM((2, blk, LANES), F32), pltpu.VMEM((blk, blk), F32), pltpu.VMEM((blk, blk), MXU_DTYPE)],
        compiler_params=_params("parallel", "parallel", "arbitrary"),
    )(*args)


def _delta_rows(do, o):
    t, width = o.shape
    bt = min(ROW_TILE, t)
    n_heads = width // HEAD_V

    def body(do_ref, o_ref, d_ref):
        prod = do_ref[...].astype(F32) * o_ref[...]
        col = _iota((width, LANES), 0)
        first = _iota((width, LANES), 1) * HEAD_V
        sel = jnp.where((col >= first) & (col < first + HEAD_V), 1.0, 0.0).astype(F32)
        per_head = jnp.dot(prod, sel, precision=lax.Precision.HIGHEST, preferred_element_type=F32)
        d_ref[...] = per_head.T[0:n_heads, :]

    return pl.pallas_call(
        body, name="attn_delta", grid=(t // bt,),
        in_specs=[pl.BlockSpec((bt, width), lambda i: (i, 0)), pl.BlockSpec((bt, width), lambda i: (i, 0))],
        out_specs=pl.BlockSpec((n_heads, bt), lambda i: (0, i)),
        out_shape=jax.ShapeDtypeStruct((n_heads, t), F32),
        compiler_params=_params("parallel"),
    )(do, o)


def _flash_bwd_fused(q, k, v, do, lse_t, delta_t, f, f_t, *, mla, scale, col0, name):
    t = q.shape[0]
    blk = min(ATTN_BLOCK, t)
    nb = t // blk
    pairs = HEADS // 2
    ch = min(ATTN_CHUNK, blk)
    assert blk % ch == 0
    w = 2 * LANES if mla else LANES
    last = nb - 1
    qmap = lambda a, b: jnp.maximum(a, b)
    q_spec = pl.BlockSpec((blk, w), lambda j, a, b: (qmap(a, b), j))
    k_spec = pl.BlockSpec((blk, w), lambda j, a, b: (a, j))
    v_spec = pl.BlockSpec((blk, LANES), lambda j, a, b: (a, j))
    do_spec = pl.BlockSpec((blk, LANES), lambda j, a, b: (qmap(a, b), col0 + j))
    lse_spec = pl.BlockSpec((8, blk), lambda j, a, b: (j, qmap(a, b)))
    delta_spec = pl.BlockSpec((8, blk), lambda j, a, b: (col0 // (HEADS // 2), qmap(a, b)))
    fq_spec = pl.BlockSpec((8, blk), lambda j, a, b: (0, qmap(a, b)))
    fk_spec = pl.BlockSpec((blk, LANES), lambda j, a, b: (a, 0))
    dq_spec = pl.BlockSpec((blk, w), lambda j, a, b: (jnp.where(a == last, b, 0), j))
    rs_spec = pl.BlockSpec((8, blk), lambda j, a, b: (j, jnp.where(a == last, b, 0)))
    cs_spec = pl.BlockSpec((blk, LANES), lambda j, a, b: (a, j))

    def body(*refs):
        if mla:
            (q_ref, k_ref, v_ref, do_ref, lse_ref, delta_ref, dq_ref, dk_ref, dv_ref,
             dq_s, dk_s, dv_s, st_s, dpt_s, pt_s, dst_s) = refs
            fq_ref = fk_ref = rs_ref = cs_ref = rs_s = cs_s = None
        else:
            (q_ref, k_ref, v_ref, do_ref, lse_ref, delta_ref, fq_ref, fk_ref, dq_ref, dk_ref, dv_ref, rs_ref, cs_ref,
             dq_s, dk_s, dv_s, st_s, dpt_s, pt_s, dst_s, rs_s, cs_s) = refs
        j, kb, qb = pl.program_id(0), pl.program_id(1), pl.program_id(2)
        lo = _iota((1, LANES), 1) < HEAD_V

        @pl.when((kb == 0) & (qb == 0))
        def _():
            dq_s[...] = jnp.zeros_like(dq_s)
            if not mla:
                rs_s[...] = jnp.zeros_like(rs_s)

        @pl.when(qb == 0)
        def _():
            dk_s[...] = jnp.zeros_like(dk_s)
            dv_s[...] = jnp.zeros_like(dv_s)
            if not mla:
                cs_s[...] = jnp.zeros_like(cs_s)

        def step(masked):
            do = do_ref[...]
            vv = v_ref[...]
            for e in range(2):
                half = lo if e == 0 else jnp.logical_not(lo)
                qe, ke = _head_qk(q_ref, k_ref, e, mla, lo)
                doe = jnp.where(half, do, jnp.zeros_like(do))
                st_s[...] = lax.dot_general(ke, qe, NT_DIMS, preferred_element_type=F32)
                dpt_s[...] = lax.dot_general(vv, doe, NT_DIMS, preferred_element_type=F32)
                head = 2 * j + e
                lse_row = _select_row(lse_ref[...], e)
                delta_row = _select_row(delta_ref[...], head)
                fq_row = None if mla else _select_row(fq_ref[...], head)

                def chunk(r0, e=e, lse_row=lse_row, delta_row=delta_row, fq_row=fq_row, head=head):
                    rows = pl.ds(r0, ch)
                    s = st_s[rows, :] * scale
                    if not mla:
                        s = s + fq_row - _select_lane(fk_ref[rows, :], head)
                    p = jnp.exp(s - lse_row)
                    if masked:
                        kpos = kb * blk + r0 + _iota((ch, blk), 0)
                        qpos = qb * blk + _iota((ch, blk), 1)
                        p = jnp.where((kpos <= qpos) & (kpos >= PAD), p, 0.0)
                    ds = p * (dpt_s[rows, :] - delta_row)
                    pt_s[rows, :] = p.astype(pt_s.dtype)
                    dst_s[rows, :] = ds.astype(dst_s.dtype)
                    if not mla:
                        cs_s[e, rows, :] += jnp.sum(ds, axis=1, keepdims=True)
                        rs_s[qb, e] += jnp.sum(ds, axis=0, keepdims=True)

                _for_chunks(blk // ch, ch, chunk)
                dv_s[e] += jnp.dot(pt_s[...], do, preferred_element_type=F32)
                q_src = qe if mla else q_ref[...]
                dk_s[e] += jnp.dot(dst_s[...], q_src, preferred_element_type=F32)
                dq_s[qb, e] += lax.dot_general(dst_s[...], ke, TN_DIMS, preferred_element_type=F32)

        needs_mask = (qb == kb) | (kb == 0)

        @pl.when((qb >= kb) & needs_mask)
        def _():
            step(True)

        @pl.when((qb >= kb) & jnp.logical_not(needs_mask))
        def _():
            step(False)

        @pl.when(qb == last)
        def _():
            dv_ref[...] = jnp.where(lo, dv_s[0], dv_s[1])
            if mla:
                dk_ref[:, 0:LANES] = dk_s[0] * scale
                dk_ref[:, LANES:2 * LANES] = dk_s[1] * scale
            else:
                dk_ref[...] = jnp.where(lo, dk_s[0], dk_s[1]) * scale
                lane = _iota((1, LANES), 1)
                cs_ref[...] = jnp.where(lane == 0, cs_s[0], jnp.where(lane == 1, cs_s[1], 0.0))

        @pl.when(kb == last)
        def _():
            if mla:
                dq_ref[:, 0:LANES] = dq_s[qb, 0] * scale
                dq_ref[:, LANES:2 * LANES] = dq_s[qb, 1] * scale
            else:
                dq_ref[...] = jnp.where(lo, dq_s[qb, 0], dq_s[qb, 1]) * scale
                sub = _iota((8, 1), 0)
                rs_ref[...] = jnp.where(sub == 0, rs_s[qb, 0], jnp.where(sub == 1, rs_s[qb, 1], 0.0))

    in_specs = [q_spec, k_spec, v_spec, do_spec, lse_spec, delta_spec] + ([] if mla else [fq_spec, fk_spec])
    args = (q, k, v, do, lse_t, delta_t) + (() if mla else (f_t, f))
    hv = HEADS * HEAD_V
    out_specs = [dq_spec, k_spec, v_spec]
    out_shape = [jax.ShapeDtypeStruct((t, pairs * w), F32), jax.ShapeDtypeStruct((t, pairs * w), F32),
                 jax.ShapeDtypeStruct((t, hv), F32)]
    scratch = [pltpu.VMEM((nb, 2, blk, LANES), F32), pltpu.VMEM((2, blk, LANES), F32), pltpu.VMEM((2, blk, LANES), F32),
               pltpu.VMEM((blk, blk), F32), pltpu.VMEM((blk, blk), F32), pltpu.VMEM((blk, blk), MXU_DTYPE),
               pltpu.VMEM((blk, blk), MXU_DTYPE)]
    if not mla:
        out_specs += [rs_spec, cs_spec]
        out_shape += [jax.ShapeDtypeStruct((pairs * 8, t), F32), jax.ShapeDtypeStruct((t, hv), F32)]
        scratch += [pltpu.VMEM((nb, 2, 1, blk), F32), pltpu.VMEM((2, blk, 1), F32)]
    return pl.pallas_call(
        body, name=name, grid=(pairs, nb, nb),
        in_specs=in_specs, out_specs=out_specs, out_shape=out_shape, scratch_shapes=scratch,
        compiler_params=_params("parallel", "arbitrary", "arbitrary"),
    )(*args)


def _shift_down(x, halo, n):
    rows = x.shape[0]
    r = _iota((rows, 1), 0)
    out = pltpu.roll(x, n, axis=0)
    for s in range(n):
        out = jnp.where(r == s, halo[8 - n + s:8 - n + s + 1, :], out)
    return out


def _shift_up(x, halo, n):
    rows = x.shape[0]
    r = _iota((rows, 1), 0)
    out = pltpu.roll(x, rows - n, axis=0)
    for s in range(n):
        out = jnp.where(r == rows - n + s, halo[s:s + 1, :], out)
    return out


def _conv_specs(bt, nblk):
    d = D_MODEL
    per8 = bt // 8
    z_spec = pl.BlockSpec((bt, 3 * d), lambda i: (i, 0))
    prev_spec = pl.BlockSpec((8, 3 * d), lambda i: (jnp.maximum(i * per8 - 1, 0), 0))
    next_z = pl.BlockSpec((8, 3 * d), lambda i: (jnp.minimum((i + 1) * per8, nblk * per8 - 1), 0))
    next_d = pl.BlockSpec((8, d), lambda i: (jnp.minimum((i + 1) * per8, nblk * per8 - 1), 0))
    w_spec = pl.BlockSpec((8, d), lambda i: (0, 0))
    row_spec = pl.BlockSpec((bt, d), lambda i: (i, 0))
    return z_spec, prev_spec, next_z, next_d, w_spec, row_spec


def _conv_taps(z_ref, prev_ref, i):
    d = D_MODEL
    g = z_ref[:, d:2 * d] * z_ref[:, 2 * d:3 * d]
    gh = jnp.where(i > 0, prev_ref[:, d:2 * d] * prev_ref[:, 2 * d:3 * d], 0.0)
    return g, _shift_down(g, gh, 1), _shift_down(g, gh, 2)


def _conv_fwd(z, conv_w8):
    t = z.shape[0]
    bt = min(PREP_TILE, t)
    nblk = t // bt
    d = D_MODEL
    z_spec, prev_spec, _, _, w_spec, row_spec = _conv_specs(bt, nblk)

    def body(z_ref, prev_ref, w_ref, v_ref):
        g, g1, g2 = _conv_taps(z_ref, prev_ref, pl.program_id(0))
        y = w_ref[0:1, :] * g2 + w_ref[1:2, :] * g1 + w_ref[2:3, :] * g
        v_ref[...] = (z_ref[:, 0:d] * y).astype(v_ref.dtype)

    return pl.pallas_call(
        body, name="conv_fwd", grid=(nblk,),
        in_specs=[z_spec, prev_spec, w_spec], out_specs=row_spec,
        out_shape=jax.ShapeDtypeStruct((t, d), MXU_DTYPE),
        compiler_params=_params("parallel"),
    )(z, z, conv_w8)


def _conv_bwd(z, conv_w8, dv):
    t = z.shape[0]
    bt = min(PREP_TILE, t)
    nblk = t // bt
    d = D_MODEL
    z_spec, prev_spec, next_z, next_d, w_spec, row_spec = _conv_specs(bt, nblk)

    def body(z_ref, prev_ref, nz_ref, dv_ref, ndv_ref, w_ref, dz_ref, dw_ref):
        i = pl.program_id(0)

        @pl.when(i == 0)
        def _():
            dw_ref[...] = jnp.zeros_like(dw_ref)

        g, g1, g2 = _conv_taps(z_ref, prev_ref, i)
        w0, w1, w2 = w_ref[0:1, :], w_ref[1:2, :], w_ref[2:3, :]
        y = w0 * g2 + w1 * g1 + w2 * g
        dvv = dv_ref[...].astype(F32)
        gate_b = z_ref[:, 0:d]
        dy = dvv * gate_b
        dyn = jnp.where(i < nblk - 1, ndv_ref[...].astype(F32) * nz_ref[:, 0:d], 0.0)
        dg = w2 * dy + w1 * _shift_up(dy, dyn, 1) + w0 * _shift_up(dy, dyn, 2)
        dz_ref[:, 0:d] = (dvv * y).astype(dz_ref.dtype)
        dz_ref[:, d:2 * d] = (dg * z_ref[:, 2 * d:3 * d]).astype(dz_ref.dtype)
        dz_ref[:, 2 * d:3 * d] = (dg * z_ref[:, d:2 * d]).astype(dz_ref.dtype)
        sub = _iota((8, 1), 0)
        s0 = jnp.sum(dy * g2, axis=0, keepdims=True)
        s1 = jnp.sum(dy * g1, axis=0, keepdims=True)
        s2 = jnp.sum(dy * g, axis=0, keepdims=True)
        dw_ref[...] += jnp.where(sub == 0, s0, jnp.where(sub == 1, s1, jnp.where(sub == 2, s2, 0.0)))

    return pl.pallas_call(
        body, name="conv_bwd", grid=(nblk,),
        in_specs=[z_spec, prev_spec, next_z, row_spec, next_d, w_spec], out_specs=[z_spec, w_spec],
        out_shape=[jax.ShapeDtypeStruct((t, 3 * d), MXU_DTYPE), jax.ShapeDtypeStruct((8, d), F32)],
        compiler_params=_params("arbitrary"),
    )(z, z, z, dv, dv, conv_w8)


def _loss_head(h, target):
    t, d = h.shape
    bt = LOSS_TILE
    assert LANES % bt == 0 or bt == LANES
    off = LANES // bt

    def body(h_ref, y_ref, dh_ref, acc_ref):
        i = pl.program_id(0)

        @pl.when(i == 0)
        def _():
            acc_ref[...] = jnp.zeros_like(acc_ref)

        @pl.when(i < off)
        def _():
            dh_ref[...] = jnp.zeros_like(dh_ref)

        @pl.when(i >= off)
        def _():
            err = h_ref[...] - y_ref[...]
            dh_ref[...] = err / d
            acc_ref[...] += jnp.sum(err * err)

    dh, acc = pl.pallas_call(
        body, name="loss_head", grid=(t // bt,),
        in_specs=[pl.BlockSpec((bt, d), lambda i: (i, 0)), pl.BlockSpec((bt, d), lambda i: (jnp.maximum(i - off, 0), 0))],
        out_specs=[pl.BlockSpec((bt, d), lambda i: (i, 0)), pl.BlockSpec((8, LANES), lambda i: (0, 0))],
        out_shape=[jax.ShapeDtypeStruct((t, d), F32), jax.ShapeDtypeStruct((8, LANES), F32)],
        compiler_params=_params("arbitrary"),
    )(h, target)
    return dh, acc[0, 0] * (0.5 / d)


def _common_tile(rows, row_off, cap=512, align=8):
    for b in range(min(cap, rows) // align * align, 0, -align):
        if rows % b == 0 and row_off % b == 0:
            return b
    raise ValueError((rows, row_off))


def _round_up(n, m):
    return -(-n // m) * m


def _adamw(w, m, v, g_buf, row_off, col_off):
    rows, width = w.shape
    wpad = _round_up(width, LANES)
    assert col_off % wpad == 0
    bt = _common_tile(rows, row_off)

    def body(w_ref, m_ref, v_ref, g_ref, go_ref, d_ref, nm_ref, nv_ref):
        gv = g_ref[...]
        if wpad != width:
            gv = gv[:, :width]
        m_new = ADAM_B1 * m_ref[...] + (1.0 - ADAM_B1) * gv
        v_new = ADAM_B2 * v_ref[...] + (1.0 - ADAM_B2) * jnp.square(gv)
        m_hat = m_new / (1.0 - ADAM_B1 ** ADAM_STEP)
        v_hat = v_new / (1.0 - ADAM_B2 ** ADAM_STEP)
        go_ref[...] = gv
        d_ref[...] = -ADAM_LR * (m_hat / (jnp.sqrt(v_hat) + ADAM_EPS) + ADAM_WD * w_ref[...])
        nm_ref[...] = m_new
        nv_ref[...] = v_new

    spec = pl.BlockSpec((bt, width), lambda i: (i, 0))
    g_spec = pl.BlockSpec((bt, wpad), lambda i: (row_off // bt + i, col_off // wpad))
    return pl.pallas_call(
        body, name="adamw", grid=(rows // bt,),
        in_specs=[spec] * 3 + [g_spec], out_specs=[spec] * 4,
        out_shape=[jax.ShapeDtypeStruct((rows, width), F32)] * 4,
        compiler_params=_params("parallel"),
    )(w, m, v, g_buf)


def _add2(a, b, *, out_dtype, name):
    rows, width = a.shape
    bt = next(x for x in range(min(rows, 640), 0, -16) if rows % x == 0)

    def body(a_ref, b_ref, o_ref):
        o_ref[...] = (a_ref[...] + b_ref[...]).astype(o_ref.dtype)

    spec = pl.BlockSpec((bt, width), lambda i: (i, 0))
    return pl.pallas_call(
        body, name=name, grid=(rows // bt,), in_specs=[spec, spec], out_specs=spec,
        out_shape=jax.ShapeDtypeStruct((rows, width), out_dtype), compiler_params=_params("parallel"),
    )(a, b)


def _sum4(parts, slot, *, name):
    _, rows, width = parts.shape
    bt = next(x for x in range(min(rows, 640), 0, -16) if rows % x == 0)

    def body(slot_ref, p_ref, o_ref):
        p = [p_ref[n].astype(F32) for n in range(4)]
        o_ref[...] = ((p[0] + p[1]) + p[2]) + p[3]

    grid_spec = pltpu.PrefetchScalarGridSpec(
        num_scalar_prefetch=1, grid=(rows // bt,),
        in_specs=[pl.BlockSpec((4, bt, width), lambda i, s: (0, i, 0))],
        out_specs=pl.BlockSpec((None, bt, width), lambda i, s: (s[0], i, 0)))
    return pl.pallas_call(
        body, name=name, grid_spec=grid_spec,
        out_shape=jax.ShapeDtypeStruct((2, rows, width), F32), compiler_params=_params("parallel"),
    )(jnp.reshape(slot, (1,)).astype(jnp.int32), parts)


ANY = pl.BlockSpec(memory_space=pl.ANY)
CHIP_FLIPS = ((1, 0), (0, 1), (1, 1))


def _place():
    return lax.axis_index("x"), lax.axis_index("y"), lax.axis_index("c")


def _flip(v, f):
    return 1 - v if f else v


def _allgather_chips(slabs):
    _, rows, width = slabs.shape
    half = rows // 2

    def body(_, out_ref, send_sems, recv_sems):
        x, y, c = _place()
        me = 2 * x + y
        sibling = (x, y, 1 - c)
        my_rows = pl.ds(pl.multiple_of(c * half, 8), half)
        sib_rows = pl.ds(pl.multiple_of((1 - c) * half, 8), half)
        first, passed = [], []
        for n, (fx, fy) in enumerate(CHIP_FLIPS):
            px, py = _flip(x, fx), _flip(y, fy)
            peer = 2 * px + py
            first.append(pltpu.make_async_remote_copy(
                src_ref=out_ref.at[me, my_rows], dst_ref=out_ref.at[me, my_rows],
                send_sem=send_sems.at[n], recv_sem=recv_sems.at[n], device_id=(px, py, c), device_id_type=MESH))
            passed.append(pltpu.make_async_remote_copy(
                src_ref=out_ref.at[peer, my_rows], dst_ref=out_ref.at[peer, my_rows],
                send_sem=send_sems.at[3 + n], recv_sem=recv_sems.at[3 + n], device_id=sibling, device_id_type=MESH))
        for cp in first:
            cp.start()
        for n, (fx, fy) in enumerate(CHIP_FLIPS):
            peer = 2 * _flip(x, fx) + _flip(y, fy)
            pltpu.make_async_remote_copy(
                src_ref=out_ref.at[me, my_rows], dst_ref=out_ref.at[peer, my_rows],
                send_sem=send_sems.at[n], recv_sem=recv_sems.at[n], device_id=sibling, device_id_type=MESH).wait_recv()
            passed[n].start()
        for n, (fx, fy) in enumerate(CHIP_FLIPS):
            peer = 2 * _flip(x, fx) + _flip(y, fy)
            pltpu.make_async_remote_copy(
                src_ref=out_ref.at[me, sib_rows], dst_ref=out_ref.at[peer, sib_rows],
                send_sem=send_sems.at[3 + n], recv_sem=recv_sems.at[3 + n], device_id=sibling,
                device_id_type=MESH).wait_recv()
        for cp in first + passed:
            cp.wait_send()

    return pl.pallas_call(
        body, name="allgather_weights",
        in_specs=[ANY], out_specs=ANY,
        out_shape=jax.ShapeDtypeStruct(slabs.shape, slabs.dtype), input_output_aliases={0: 0},
        scratch_shapes=[pltpu.SemaphoreType.DMA((6,)), pltpu.SemaphoreType.DMA((6,))],
    )(slabs)


def _swap_halves(g):
    _, rows, width = g.shape
    half = rows // 2

    def body(g_ref, got_ref, send_sem, recv_sem):
        x, y, c = _place()
        away = pl.ds(pl.multiple_of((1 - c) * half, 8), half)
        cp = pltpu.make_async_remote_copy(
            src_ref=g_ref.at[:, away], dst_ref=got_ref, send_sem=send_sem, recv_sem=recv_sem,
            device_id=(x, y, 1 - c), device_id_type=MESH)
        cp.start()
        cp.wait()

    return pl.pallas_call(
        body, name="grad_swap_halves",
        in_specs=[ANY], out_specs=ANY,
        out_shape=jax.ShapeDtypeStruct((4, half, width), g.dtype),
        scratch_shapes=[pltpu.SemaphoreType.DMA, pltpu.SemaphoreType.DMA],
    )(g)


def _scatter_chips(s):
    _, rows, width = s.shape

    def body(s_ref, out_ref, send_sems, recv_sems, local_sem):
        x, y, c = _place()
        me = 2 * x + y
        mine = pltpu.make_async_copy(s_ref.at[me], out_ref.at[me], local_sem)
        mine.start()
        copies = []
        for n, (fx, fy) in enumerate(CHIP_FLIPS):
            px, py = _flip(x, fx), _flip(y, fy)
            copies.append(pltpu.make_async_remote_copy(
                src_ref=s_ref.at[2 * px + py], dst_ref=out_ref.at[me],
                send_sem=send_sems.at[n], recv_sem=recv_sems.at[n], device_id=(px, py, c), device_id_type=MESH))
        for cp in copies:
            cp.start()
        for n, (fx, fy) in enumerate(CHIP_FLIPS):
            peer = 2 * _flip(x, fx) + _flip(y, fy)
            pltpu.make_async_remote_copy(
                src_ref=s_ref.at[me], dst_ref=out_ref.at[peer],
                send_sem=send_sems.at[n], recv_sem=recv_sems.at[n], device_id=(x, y, c), device_id_type=MESH).wait_recv()
        for cp in copies:
            cp.wait_send()
        mine.wait()

    return pl.pallas_call(
        body, name="grad_scatter_chips",
        in_specs=[ANY], out_specs=ANY,
        out_shape=jax.ShapeDtypeStruct((4, rows, width), s.dtype),
        scratch_shapes=[pltpu.SemaphoreType.DMA((3,)), pltpu.SemaphoreType.DMA((3,)), pltpu.SemaphoreType.DMA],
    )(s)


def _join_halves(halves):
    def body(_, out_ref, send_sem, recv_sem):
        x, y, c = _place()
        cp = pltpu.make_async_remote_copy(
            src_ref=out_ref.at[c], dst_ref=out_ref.at[c], send_sem=send_sem, recv_sem=recv_sem,
            device_id=(x, y, 1 - c), device_id_type=MESH)
        cp.start()
        pltpu.make_async_remote_copy(
            src_ref=out_ref.at[c], dst_ref=out_ref.at[1 - c], send_sem=send_sem, recv_sem=recv_sem,
            device_id=(x, y, 1 - c), device_id_type=MESH).wait_recv()
        cp.wait_send()

    return pl.pallas_call(
        body, name="grad_join_halves",
        in_specs=[ANY], out_specs=ANY,
        out_shape=jax.ShapeDtypeStruct(halves.shape, halves.dtype), input_output_aliases={0: 0},
        scratch_shapes=[pltpu.SemaphoreType.DMA, pltpu.SemaphoreType.DMA],
    )(halves)


PACK_W = 1024
REPLICATED = ("g_mix", "g_mlp", "g_cq", "g_ckv", "g_q_mla", "g_k_mla", "g_q_fox", "g_k_fox", "b_forget")
WEIGHT_ORDER = ("meta_tokens", "g_mix", "g_mlp", "w_in_attn", "g_cq", "w_uq", "g_ckv", "w_ukv", "g_q_mla", "g_k_mla",
                "g_q_fox", "g_k_fox", "b_forget", "w_out_attn", "w_in_conv", "conv_w", "w_out_conv", "w_mlp_up",
                "w_mlp_down")
N_EVEN = 2
N_ODD = 2
SHARD_IN = ATTN_IN // 4
SHARD_MIX = D_MODEL // 4
SHARD_UQ = HEADS * MLA_QK // 4
SHARD_UKV = HEADS * (MLA_NOPE + HEAD_V) // 4
SHARD_CONV = 3 * D_MODEL // 4
SIDE_W = 256
PK_UP = (0, 0)
PK_DOWN = (4096, 0)
PK_CONV_IN = (8192, 0)
PK_ATTN_IN = (10240, 0)
PK_OUT_ATTN = (12288, 0)
PK_OUT_CONV = (12800, 0)
PK_SMALL = (8192, 768)
PK_UQ = (10240, 768)
PK_UKV = (11008, 768)
PK_ROWS = 13312
SMALL_ROWS = 64
SMALL_META = 0
SMALL_CONV = 16
SMALL_REP = 24
SMALL_BITS_ROWS = 48
MATRIX_PLACES = (("w_mlp_up", PK_UP), ("w_mlp_down", PK_DOWN), ("w_in_conv", PK_CONV_IN), ("w_in_attn", PK_ATTN_IN),
                 ("w_out_attn", PK_OUT_ATTN), ("w_out_conv", PK_OUT_CONV), ("w_uq", PK_UQ), ("w_ukv", PK_UKV))


def _put(buf, x, place, *, name, slab=None):
    row_off, col_off = place
    slabs = x.ndim == 3
    rows, w = x.shape[-2:]
    wpad = _round_up(w, LANES)
    assert col_off % wpad == 0
    bt = _common_tile(rows, row_off, align=16)

    def fill(x_ref, o_ref):
        v = x_ref[...].astype(o_ref.dtype)
        if wpad != w:
            v = jnp.concatenate([v, jnp.zeros((bt, wpad - w), o_ref.dtype)], axis=1)
        o_ref[...] = v

    def body(x_ref, _, o_ref):
        fill(x_ref, o_ref)

    if slab is not None:
        grid_spec = pltpu.PrefetchScalarGridSpec(
            num_scalar_prefetch=1, grid=(rows // bt,),
            in_specs=[pl.BlockSpec((bt, w), lambda i, s: (i, 0)), ANY],
            out_specs=pl.BlockSpec((None, bt, wpad), lambda i, s: (s[0], row_off // bt + i, col_off // wpad)))
        return pl.pallas_call(
            lambda s_ref, x_ref, _, o_ref: fill(x_ref, o_ref), name=name, grid_spec=grid_spec,
            out_shape=jax.ShapeDtypeStruct(buf.shape, buf.dtype), input_output_aliases={2: 0},
            compiler_params=_params("parallel"),
        )(jnp.reshape(slab, (1,)).astype(jnp.int32), x, buf)
    if slabs:
        grid = (4, rows // bt)
        x_spec = pl.BlockSpec((None, bt, w), lambda s, i: (s, i, 0))
        o_spec = pl.BlockSpec((None, bt, wpad), lambda s, i: (s, row_off // bt + i, col_off // wpad))
        sem = ("parallel", "parallel")
    else:
        grid = (rows // bt,)
        x_spec = pl.BlockSpec((bt, w), lambda i: (i, 0))
        o_spec = pl.BlockSpec((bt, wpad), lambda i: (row_off // bt + i, col_off // wpad))
        sem = ("parallel",)
    return pl.pallas_call(
        body, name=name, grid=grid, in_specs=[x_spec, ANY], out_specs=o_spec,
        out_shape=jax.ShapeDtypeStruct(buf.shape, buf.dtype), input_output_aliases={1: 0},
        compiler_params=_params(*sem),
    )(x, buf)


def _w_cols(place, layer, rows, width):
    base = (place[0] + layer * rows) // rows
    return dict(n=4 * width, tn=width, tk=rows, spec=pl.BlockSpec((None, rows, width), lambda i, j, k: (j, base, 0)))


def _w_cols_t(place, layer, rows, width):
    base = (place[0] + layer * rows) // rows
    return dict(n=rows, tn=rows, tk=width, spec=pl.BlockSpec((None, rows, width), lambda i, j, k: (k, base, 0)))


def _w_rows(place, layer, rows):
    base = (place[0] + layer * rows) // rows
    return dict(n=D_MODEL, tn=D_MODEL, tk=rows, spec=pl.BlockSpec((None, rows, D_MODEL), lambda i, j, k: (k, base, 0)))


def _w_rows_t(place, layer, rows):
    base = (place[0] + layer * rows) // rows
    return dict(n=4 * rows, tn=rows, tk=D_MODEL, spec=pl.BlockSpec((None, rows, D_MODEL), lambda i, j, k: (j, base, 0)))


def _g_cols(g, place, layer, rows, width):
    base = (place[0] + layer * rows) // rows
    return g, pl.BlockSpec((None, rows, width), lambda i, j, k: (j, base, 0))


def _g_rows(g, place, layer, rows):
    base = (place[0] + layer * rows) // rows
    return g, pl.BlockSpec((None, rows, D_MODEL), lambda i, j, k: (i, base, 0))


IN_PADW = _round_up(SHARD_IN, LANES)
IN_TAIL = ZA_FQ - SHARD_IN
IN_FL = SHARD_IN - HEADS
ZA_KPE = ZA_LAST + KPE_LANE


def _assemble_attn_in(gathered, layer):
    bt = 256
    base = (PK_ATTN_IN[0] + layer * D_MODEL) // bt
    assert 2 * SHARD_IN > ZA_FQ + MLA_ROPE and 3 * SHARD_IN < ATTN_IN - HEADS

    def body(s0, s1, s2, s3, o_ref):
        dt = o_ref.dtype
        z = lambda n: jnp.zeros((bt, n), dt)
        o_ref[...] = jnp.concatenate(
            [s0[:, :SHARD_IN], s1[:, :IN_TAIL], s1[:, IN_TAIL + MLA_ROPE:SHARD_IN], s2[:, :SHARD_IN], s3[:, :IN_FL],
             s3[:, IN_FL:SHARD_IN], z(KPE_LANE - HEADS), s1[:, IN_TAIL:IN_TAIL + MLA_ROPE],
             z(LANES - KPE_LANE - MLA_ROPE)], axis=1).astype(dt)

    def spec(s):
        return pl.BlockSpec((None, bt, IN_PADW), lambda i: (s, base + i, 0))

    return pl.pallas_call(
        body, name="assemble_attn_in", grid=(D_MODEL // bt,),
        in_specs=[spec(s) for s in range(4)], out_specs=pl.BlockSpec((bt, ZA_W), lambda i: (i, 0)),
        out_shape=jax.ShapeDtypeStruct((D_MODEL, ZA_W), MXU_DTYPE), compiler_params=_params("parallel"),
    )(gathered, gathered, gathered, gathered)


def _scatter_attn_in(g, dwa, layer):
    bt = 256
    base = (PK_ATTN_IN[0] + layer * D_MODEL) // bt
    fq1 = ZA_FQ + SHARD_IN - IN_TAIL - MLA_ROPE

    def body(d_ref, _, o_ref):
        pad = jnp.zeros((bt, IN_PADW - SHARD_IN), F32)
        pieces = (
            (d_ref[:, 0:SHARD_IN],),
            (d_ref[:, SHARD_IN:ZA_FQ], d_ref[:, ZA_KPE:ZA_KPE + MLA_ROPE], d_ref[:, ZA_FQ:fq1]),
            (d_ref[:, fq1:fq1 + SHARD_IN],),
            (d_ref[:, fq1 + SHARD_IN:ZA_LAST], d_ref[:, ZA_LAST:ZA_LAST + HEADS]),
        )
        for s in range(4):
            @pl.when(pl.program_id(0) == s)
            def _(s=s):
                o_ref[...] = jnp.concatenate(list(pieces[s]) + [pad], axis=1)

    return pl.pallas_call(
        body, name="scatter_attn_in", grid=(4, D_MODEL // bt),
        in_specs=[pl.BlockSpec((bt, ZA_W), lambda s, i: (i, 0)), ANY],
        out_specs=pl.BlockSpec((None, bt, IN_PADW), lambda s, i: (s, base + i, 0)),
        out_shape=jax.ShapeDtypeStruct(g.shape, g.dtype), input_output_aliases={1: 0},
        compiler_params=_params("parallel", "parallel"),
    )(dwa, g)


def _assemble_uq(gathered, layer):
    bt = 128
    base = (PK_UQ[0] + layer * Q_LORA) // bt
    col = PK_UQ[1] // SIDE_W

    def body(s0, s1, s2, s3, o_ref):
        dt = o_ref.dtype
        z = jnp.zeros((bt, LANES - MLA_QK), dt)
        parts = []
        for s_ref in (s0, s1, s2, s3):
            parts += [s_ref[:, 0:MLA_QK], z, s_ref[:, MLA_QK:2 * MLA_QK], z]
        o_ref[...] = jnp.concatenate(parts, axis=1).astype(dt)

    def spec(s):
        return pl.BlockSpec((None, bt, SIDE_W), lambda i: (s, base + i, col))

    return pl.pallas_call(
        body, name="assemble_uq", grid=(Q_LORA // bt,),
        in_specs=[spec(s) for s in range(4)], out_specs=pl.BlockSpec((bt, HEADS * LANES), lambda i: (i, 0)),
        out_shape=jax.ShapeDtypeStruct((Q_LORA, HEADS * LANES), MXU_DTYPE), compiler_params=_params("parallel"),
    )(gathered, gathered, gathered, gathered)


def _scatter_uq(g, dw, layer):
    bt = 128
    base = (PK_UQ[0] + layer * Q_LORA) // bt
    col = PK_UQ[1] // SIDE_W

    def body(d_ref, _, o_ref):
        o_ref[...] = jnp.concatenate([d_ref[:, 0:MLA_QK], d_ref[:, LANES:LANES + MLA_QK],
                                      jnp.zeros((bt, SIDE_W - 2 * MLA_QK), F32)], axis=1)

    return pl.pallas_call(
        body, name="scatter_uq", grid=(4, Q_LORA // bt),
        in_specs=[pl.BlockSpec((bt, 2 * LANES), lambda s, i: (i, s)), ANY],
        out_specs=pl.BlockSpec((None, bt, SIDE_W), lambda s, i: (s, base + i, col)),
        out_shape=jax.ShapeDtypeStruct(g.shape, g.dtype), input_output_aliases={1: 0},
        compiler_params=_params("parallel", "parallel"),
    )(dw, g)


def _assemble_ukv(gathered, layer):
    bt = KV_LORA
    base = (PK_UKV[0] + layer * KV_LORA) // bt
    col = PK_UKV[1] // SIDE_W
    hd = MLA_NOPE + HEAD_V

    def body(s0, s1, s2, s3, o_ref):
        dt = o_ref.dtype
        z = jnp.zeros((bt, LANES - MLA_NOPE), dt)
        keys, vals = [], []
        for s_ref in (s0, s1, s2, s3):
            for e in range(2):
                keys += [s_ref[:, e * hd:e * hd + MLA_NOPE], z]
                vals.append(s_ref[:, e * hd + MLA_NOPE:(e + 1) * hd])
        o_ref[...] = jnp.concatenate(keys + vals, axis=1).astype(dt)

    def spec(s):
        return pl.BlockSpec((None, bt, SIDE_W), lambda i: (s, base + i, col))

    return pl.pallas_call(
        body, name="assemble_ukv", grid=(1,),
        in_specs=[spec(s) for s in range(4)],
        out_specs=pl.BlockSpec((bt, HEADS * (LANES + HEAD_V)), lambda i: (i, 0)),
        out_shape=jax.ShapeDtypeStruct((KV_LORA, HEADS * (LANES + HEAD_V)), MXU_DTYPE), compiler_params=_params("parallel"),
    )(gathered, gathered, gathered, gathered)


def _scatter_ukv(g, dw, layer):
    bt = KV_LORA
    base = (PK_UKV[0] + layer * KV_LORA) // bt
    col = PK_UKV[1] // SIDE_W

    def body(k_ref, v_ref, _, o_ref):
        o_ref[...] = jnp.concatenate([k_ref[:, 0:MLA_NOPE], v_ref[:, 0:HEAD_V], k_ref[:, LANES:LANES + MLA_NOPE],
                                      v_ref[:, HEAD_V:2 * HEAD_V]], axis=1)

    return pl.pallas_call(
        body, name="scatter_ukv", grid=(4,),
        in_specs=[pl.BlockSpec((bt, 2 * LANES), lambda s: (0, s)),
                  pl.BlockSpec((bt, 2 * HEAD_V), lambda s: (0, HEADS * LANES // (2 * HEAD_V) + s)), ANY],
        out_specs=pl.BlockSpec((None, bt, SIDE_W), lambda s: (s, base, col)),
        out_shape=jax.ShapeDtypeStruct(g.shape, g.dtype), input_output_aliases={2: 0},
        compiler_params=_params("parallel"),
    )(dw, dw, g)


def _pad_lanes(v, n=LANES):
    return jnp.pad(v, (0, n - v.shape[0])).reshape(1, n)


def _relu2_up(acc):
    r = jnp.maximum(acc, 0.0)
    return acc, r * r


def _relu2_bwd(acc, u):
    return (acc * (2.0 * jnp.maximum(u, 0.0)),)


def _add_res(acc, res):
    return (acc + res,)


def _local_step(x, target, meta, small, gathered):
    seq = x.shape[0]
    t = seq + LANES
    d = D_MODEL
    h = jnp.concatenate([jnp.zeros((PAD, d), F32), meta.astype(F32), x], axis=0)
    c_tab, s_tab = _rope_tables(t)
    scale_mla, scale_fox = MLA_QK ** -0.5, FOX_DIM ** -0.5
    grads = {}
    saved = []
    g = jnp.zeros((4, PK_ROWS, PACK_W), F32)

    for layer in range(DEPTH):
        j = layer // 2
        sv = {"h_in": h}
        hn = _rmsnorm_fwd(h, small["g_mix"][layer])
        sv["hn"] = hn
        if layer % 2 == 0:
            w_in = _assemble_attn_in(gathered, j)
            w_uq = _assemble_uq(gathered, j)
            w_ukv = _assemble_ukv(gathered, j)
            out_place = PK_OUT_ATTN
            vecs = dict(
                g_cq=small["g_cq"][j].reshape(1, Q_LORA), g_ckv=small["g_ckv"][j].reshape(1, KV_LORA),
                g_qf=jnp.tile(small["g_q_fox"][j], 2).reshape(1, LANES), g_kf=jnp.tile(small["g_k_fox"][j], 2).reshape(1, LANES),
                b_f=_pad_lanes(small["b_forget"][j]), g_q=_pad_lanes(small["g_q_mla"][j]), g_k=_pad_lanes(small["g_k_mla"][j]))
            z = _matmul(hn, w_in, name="mm_attn_in")
            cqn, ckvn, qf, kf, vf, logf = _prep_a_fwd(z, vecs["g_cq"], vecs["g_ckv"], vecs["g_qf"], vecs["g_kf"], vecs["b_f"])
            f_cum, f_cum_t = _cumsum_rows(logf, reverse=False, name="cumsum_fwd")
            q_raw = _matmul(cqn, w_uq, name="mm_uq")
            kv_raw = _matmul(ckvn, w_ukv, name="mm_ukv")
            qn, kn, v_mla = _prep_b_fwd(q_raw, kv_raw, z, c_tab, s_tab, vecs["g_q"], vecs["g_k"])
            o_mla, lse_mla = _flash_fwd_chunked(qn, kn, v_mla, None, None, mla=True, scale=scale_mla, name="flash_fwd_mla")
            o_fox, lse_fox = _flash_fwd_chunked(qf, kf, vf, f_cum, f_cum_t, mla=False, scale=scale_fox,
                                                name="flash_fwd_fox")
            o = jnp.concatenate([o_mla, o_fox], axis=1)
            h = _matmul(o, gathered, b_tiles=_w_rows(out_place, j, SHARD_MIX), extras=(h,), epilogue=_add_res,
                        name="mm_mix_out")
            sv.update(w_in=w_in, w_uq=w_uq, w_ukv=w_ukv, out_place=out_place, vecs=vecs, z=z, cqn=cqn, ckvn=ckvn, qf=qf, kf=kf,
                      vf=vf, f_cum=f_cum, f_cum_t=f_cum_t, q_raw=q_raw, kv_raw=kv_raw, qn=qn, kn=kn, v_mla=v_mla, o=o,
                      lse_mla=lse_mla, lse_fox=lse_fox)
        else:
            out_place = PK_OUT_CONV
            conv_w8 = jnp.pad(small["conv_w"][j], ((0, 5), (0, 0)))
            z = _matmul(hn, gathered, b_tiles=_w_cols(PK_CONV_IN, j, d, SHARD_CONV), name="mm_conv_in")
            vmix = _conv_fwd(z, conv_w8)
            h = _matmul(vmix, gathered, b_tiles=_w_rows(out_place, j, SHARD_MIX), extras=(h,), epilogue=_add_res,
                        name="mm_mix_out")
            sv.update(out_place=out_place, conv_w8=conv_w8, z=z, vmix=vmix)
        sv["h_mid"] = h
        hn2 = _rmsnorm_fwd(h, small["g_mlp"][layer])
        u, a = _matmul(hn2, gathered, b_tiles=_w_cols(PK_UP, layer, d, d), epilogue=_relu2_up,
                       out_dtypes=(F32, MXU_DTYPE), name="mm_mlp_up")
        h = _matmul(a, gathered, b_tiles=_w_rows(PK_DOWN, layer, d), extras=(h,), epilogue=_add_res, name="mm_mlp_down")
        sv.update(hn2=hn2, u=u, a=a)
        saved.append(sv)

    dh, loss_local = _loss_head(h, target)

    dg_mix, dg_mlp = [None] * DEPTH, [None] * DEPTH
    per_even = {k: [None, None] for k in ("g_cq", "g_ckv", "g_q_mla", "g_k_mla", "g_q_fox", "g_k_fox", "b_forget")}
    per_odd = {"conv_w": [None, None]}
    for layer in reversed(range(DEPTH)):
        j = layer // 2
        sv = saved[layer]
        du = _matmul(dh, gathered, tb=True, b_tiles=_w_rows_t(PK_DOWN, layer, d), extras=(sv["u"],),
                     epilogue=_relu2_bwd, out_dtypes=(MXU_DTYPE,), name="mm_mlp_da")
        g = _matmul(sv["a"], dh, ta=True, out_into=_g_rows(g, PK_DOWN, layer, d), name="mm_dw_down")
        g = _matmul(sv["hn2"], du, ta=True, out_into=_g_cols(g, PK_UP, layer, d, d), name="mm_dw_up")
        dhn2 = _matmul(du, gathered, tb=True, b_tiles=_w_cols_t(PK_UP, layer, d, d), name="mm_mlp_dhn")
        dh, dg_mlp[layer] = _rmsnorm_bwd(sv["h_mid"], small["g_mlp"][layer], dhn2, dh)
        do = _matmul(dh, gathered, tb=True, b_tiles=_w_rows_t(sv["out_place"], j, SHARD_MIX), out_dtypes=(MXU_DTYPE,),
                     name="mm_mix_do")
        if layer % 2 == 0:
            vecs = sv["vecs"]
            g = _matmul(sv["o"], dh, ta=True, tm=SHARD_MIX, out_into=_g_rows(g, PK_OUT_ATTN, j, SHARD_MIX),
                        name="mm_dw_out")
            delta_t = _delta_rows(do, sv["o"])
            dqn, dkn, dv_mla = _flash_bwd_fused(sv["qn"], sv["kn"], sv["v_mla"], do, sv["lse_mla"], delta_t, None, None,
                                                mla=True, scale=scale_mla, col0=0, name="flash_bwd_mla")
            dqf, dkf, dvf, rs_t, cs = _flash_bwd_fused(sv["qf"], sv["kf"], sv["vf"], do, sv["lse_fox"], delta_t,
                                                       sv["f_cum"], sv["f_cum_t"], mla=False, scale=scale_fox,
                                                       col0=HEADS // 2, name="flash_bwd_fox")
            d_f = rs_t.reshape(HEADS // 2, 8, t)[:, :2, :].reshape(HEADS, t).T
            d_f = d_f - cs.reshape(t, HEADS // 2, LANES)[:, :, :2].reshape(t, HEADS)
            d_f = jnp.pad(d_f, ((0, 0), (0, LANES - HEADS)))
            dlogf, _ = _cumsum_rows(d_f, reverse=True, name="cumsum_bwd")
            dq_raw, dkv_raw, dkpe, dg_q, dg_k = _prep_b_bwd(sv["q_raw"], sv["kv_raw"], sv["z"], c_tab, s_tab, vecs["g_q"],
                                                            vecs["g_k"], dqn, dkn, dv_mla)
            g = _scatter_uq(g, _matmul(sv["cqn"], dq_raw, ta=True, name="mm_dw_uq"), j)
            g = _scatter_ukv(g, _matmul(sv["ckvn"], dkv_raw, ta=True, name="mm_dw_ukv"), j)
            dcqn = _matmul(dq_raw, sv["w_uq"], tb=True, name="mm_dcqn")
            dckvn = _matmul(dkv_raw, sv["w_ukv"], tb=True, name="mm_dckvn")
            dz, dg_cq, dg_ckv, dg_qf, dg_kf, db_f = _prep_a_bwd(
                sv["z"], vecs["g_cq"], vecs["g_ckv"], vecs["g_qf"], vecs["g_kf"], vecs["b_f"], dcqn, dckvn, dqf, dkf, dvf,
                dlogf, dkpe)
            g = _scatter_attn_in(g, _matmul(sv["hn"], dz, ta=True, name="mm_dw_attn_in"), j)
            per_even["g_cq"][j] = dg_cq[0]
            per_even["g_ckv"][j] = dg_ckv[0]
            per_even["g_q_mla"][j] = dg_q[0, :MLA_QK]
            per_even["g_k_mla"][j] = dg_k[0, :MLA_QK]
            per_even["g_q_fox"][j] = dg_qf[0, :FOX_DIM] + dg_qf[0, FOX_DIM:]
            per_even["g_k_fox"][j] = dg_kf[0, :FOX_DIM] + dg_kf[0, FOX_DIM:]
            per_even["b_forget"][j] = db_f[0, :HEADS]
            dhn = _matmul(dz, sv["w_in"], tb=True, name="mm_attn_dhn")
        else:
            g = _matmul(sv["vmix"], dh, ta=True, tm=SHARD_MIX, out_into=_g_rows(g, PK_OUT_CONV, j, SHARD_MIX),
                        name="mm_dw_out")
            dz, dcw = _conv_bwd(sv["z"], sv["conv_w8"], do)
            per_odd["conv_w"][j] = dcw[:3]
            g = _matmul(sv["hn"], dz, ta=True, tn=SHARD_CONV, out_into=_g_cols(g, PK_CONV_IN, j, d, SHARD_CONV),
                        name="mm_dw_conv_in")
            dhn = _matmul(dz, gathered, tb=True, b_tiles=_w_cols_t(PK_CONV_IN, j, d, SHARD_CONV), name="mm_conv_dhn")
        dh, dg_mix[layer] = _rmsnorm_bwd(sv["h_in"], small["g_mix"][layer], dhn, dh)

    grads["meta_tokens"] = dh[PAD:LANES]
    grads["g_mix"] = jnp.stack(dg_mix)
    grads["g_mlp"] = jnp.stack(dg_mlp)
    for k, v in list(per_even.items()) + list(per_odd.items()):
        grads[k] = jnp.stack(v)
    return loss_local, dh[LANES:], g, grads


def kernel(x, meta_tokens, g_mix, g_mlp, w_in_attn, g_cq, w_uq, g_ckv, w_ukv, g_q_mla, g_k_mla, g_q_fox, g_k_fox, b_forget, w_out_attn, w_in_conv, conv_w, w_out_conv, w_mlp_up, w_mlp_down, loss_target, m_meta_tokens, m_g_mix, m_g_mlp, m_w_in_attn, m_g_cq, m_w_uq, m_g_ckv, m_w_ukv, m_g_q_mla, m_g_k_mla, m_g_q_fox, m_g_k_fox, m_b_forget, m_w_out_attn, m_w_in_conv, m_conv_w, m_w_out_conv, m_w_mlp_up, m_w_mlp_down, v_meta_tokens, v_g_mix, v_g_mlp, v_w_in_attn, v_g_cq, v_w_uq, v_g_ckv, v_w_ukv, v_g_q_mla, v_g_k_mla, v_g_q_fox, v_g_k_fox, v_b_forget, v_w_out_attn, v_w_in_conv, v_conv_w, v_w_out_conv, v_w_mlp_up, v_w_mlp_down):
    args = dict(locals())
    weights = {n: args[n] for n in WEIGHT_ORDER}
    mom_m = {n: args["m_" + n] for n in WEIGHT_ORDER}
    mom_v = {n: args["v_" + n] for n in WEIGHT_ORDER}

    wire = jnp.bfloat16
    me = 2 * lax.axis_index("x") + lax.axis_index("y")
    buf = jnp.zeros((4, PK_ROWS, PACK_W), wire)
    for name, place in MATRIX_PLACES:
        w = weights[name]
        buf = _put(buf, w.reshape(-1, w.shape[-1]), place, name="pack_weights", slab=me)
    meta_bits = lax.bitcast_convert_type(meta_tokens, wire).reshape(2 * N_META, SIDE_W)
    conv_bits = lax.bitcast_convert_type(conv_w, wire).reshape(2 * N_ODD * 3, SIDE_W)
    bits = jnp.concatenate([meta_bits, conv_bits, jnp.zeros((SMALL_BITS_ROWS - 2 * N_META - 2 * N_ODD * 3, SIDE_W), wire)])
    buf = _put(buf, bits, PK_SMALL, name="pack_weights", slab=me)
    gathered = _allgather_chips(buf)
    got_bits = gathered[:, PK_SMALL[0]:PK_SMALL[0] + SMALL_BITS_ROWS, PK_SMALL[1]:PK_SMALL[1] + SIDE_W]
    meta_full = lax.bitcast_convert_type(got_bits[:, :2 * N_META].reshape(4, N_META, SIDE_W, 2), F32)
    meta_full = meta_full.transpose(1, 0, 2).reshape(N_META, D_MODEL)
    conv_full = lax.bitcast_convert_type(
        got_bits[:, 2 * N_META:2 * N_META + 2 * N_ODD * 3].reshape(4, N_ODD, 3, SIDE_W, 2), F32)
    small = {n: weights[n] for n in REPLICATED}
    small["conv_w"] = conv_full.transpose(1, 2, 0, 3).reshape(N_ODD, 3, D_MODEL)

    loss_local, grad_x, g, grads = _local_step(x[0], loss_target[0], meta_full, small, gathered)
    loss = lax.psum(loss_local, MESH_AXES)

    rep = jnp.concatenate([grads[n].reshape(-1) for n in REPLICATED])
    rep = jnp.pad(rep, (0, (SMALL_ROWS - SMALL_REP) * SIDE_W - rep.shape[0])).reshape(SMALL_ROWS - SMALL_REP, SIDE_W)
    g_meta = grads["meta_tokens"].reshape(N_META, 4, SIDE_W).transpose(1, 0, 2)
    g_conv = grads["conv_w"].reshape(N_ODD * 3, 4, SIDE_W).transpose(1, 0, 2)
    small4 = jnp.concatenate([g_meta, g_conv, jnp.zeros((4, SMALL_REP - SMALL_CONV - N_ODD * 3, SIDE_W), F32),
                              jnp.broadcast_to(rep[None], (4,) + rep.shape)], axis=1)
    g = _put(g, small4, PK_SMALL, name="pack_small_grads")
    half = PK_ROWS // 2
    c = lax.axis_index("c")
    got = _swap_halves(g)
    kept = lax.dynamic_slice_in_dim(g, c * half, half, axis=1)
    pair = _add2(kept.reshape(4 * half, PACK_W), got.reshape(4 * half, PACK_W), out_dtype=jnp.bfloat16,
                 name="grad_pair_sum").reshape(4, half, PACK_W)
    total = _sum4(_scatter_chips(pair), c, name="grad_chip_sum")
    g_tot = _join_halves(total).reshape(PK_ROWS, PACK_W)

    out = {}
    for name, place in MATRIX_PLACES:
        shape = weights[name].shape
        two_d = lambda a: a.reshape(-1, shape[-1])
        res = _adamw(two_d(weights[name]), two_d(mom_m[name]), two_d(mom_v[name]), g_tot, place[0], place[1])
        out[name] = [r.reshape(shape) for r in res]

    def small_pack(src):
        flat = jnp.concatenate([src[n].reshape(-1) for n in REPLICATED])
        flat = jnp.pad(flat, (0, (SMALL_ROWS - SMALL_REP) * SIDE_W - flat.shape[0])).reshape(SMALL_ROWS - SMALL_REP, SIDE_W)
        return jnp.concatenate([src["meta_tokens"], src["conv_w"].reshape(N_ODD * 3, SIDE_W),
                                jnp.zeros((SMALL_REP - SMALL_CONV - N_ODD * 3, SIDE_W), F32), flat])

    res = _adamw(small_pack(weights), small_pack(mom_m), small_pack(mom_v), g_tot, PK_SMALL[0], PK_SMALL[1])
    for name in ("meta_tokens", "conv_w") + REPLICATED:
        out[name] = []
    for r in res:
        out["meta_tokens"].append(r[SMALL_META:SMALL_META + N_META])
        out["conv_w"].append(r[SMALL_CONV:SMALL_CONV + N_ODD * 3].reshape(N_ODD, 3, SIDE_W))
        flat, off = r[SMALL_REP:].reshape(-1), 0
        for name in REPLICATED:
            n = weights[name].size
            out[name].append(flat[off:off + n].reshape(weights[name].shape))
            off += n
    return (loss, grad_x[None], *[out[n][0] for n in WEIGHT_ORDER], *[out[n][1] for n in WEIGHT_ORDER],
            *[out[n][2] for n in WEIGHT_ORDER], *[out[n][3] for n in WEIGHT_ORDER])
```

```python
import functools

import jax
import jax.numpy as jnp
from jax import lax
from jax.experimental import pallas as pl
from jax.experimental.pallas import tpu as pltpu

F32 = jnp.float32
MXU_DTYPE = jnp.bfloat16

D_MODEL = 1024
N_META = 16
LANES = 128
PAD = LANES - N_META
HEADS = 8
Q_LORA = 384
KV_LORA = 256
MLA_NOPE = 64
MLA_ROPE = 32
MLA_QK = MLA_NOPE + MLA_ROPE
HEAD_V = 64
FOX_DIM = 64
ROPE_BASE = 10000.0
D_FF = 4 * D_MODEL
DEPTH = 4
EPS = 1e-6
NEG = -1e30
ATTN_IN = Q_LORA + KV_LORA + MLA_ROPE + 3 * HEADS * FOX_DIM + HEADS

ZA_CQ = 0
ZA_CKV = Q_LORA
ZA_FQ = ZA_CKV + KV_LORA
ZA_FK = ZA_FQ + HEADS * FOX_DIM
ZA_FV = ZA_FK + HEADS * FOX_DIM
ZA_LAST = ZA_FV + HEADS * FOX_DIM
ZA_W = ZA_LAST + LANES
KPE_LANE = MLA_NOPE

ADAM_LR = 0.001
ADAM_B1 = 0.9
ADAM_B2 = 0.999
ADAM_EPS = 1e-08
ADAM_WD = 0.01
ADAM_STEP = 10

VMEM_LIMIT_BYTES = 52 * 1024 * 1024
ROW_TILE = 1040
PREP_TILE = 320
ATTN_BLOCK = 640
LOSS_TILE = 128
MAX_TILE = 1536

MESH_AXES = ("x", "y", "c")
MESH = pl.DeviceIdType.MESH


def _params(*sem):
    return pltpu.CompilerParams(dimension_semantics=sem, vmem_limit_bytes=VMEM_LIMIT_BYTES)


def _tile(n, cap=None):
    cap = MAX_TILE if cap is None else cap
    if n <= cap:
        return n
    best = None
    for t in range(LANES, cap + 1, LANES):
        if n % t == 0:
            best = t
    assert best is not None, n
    return best


def _iota(shape, dim):
    return lax.broadcasted_iota(jnp.int32, shape, dim)


def _matmul(a, b, *, ta=False, tb=False, extras=(), epilogue=None, out_dtypes=(F32,), name, b_tiles=None,
            out_into=None, tm=None, tn=None):
    if ta:
        kdim, m = a.shape
    else:
        m, kdim = a.shape
    row_tile = min(ROW_TILE, m)
    if ta:
        tm_auto, tk = _tile(m), min(ATTN_BLOCK, kdim)
    else:
        tm_auto, tk = (row_tile if m % row_tile == 0 else _tile(m)), _tile(kdim)
    tm = tm_auto if tm is None else tm
    if b_tiles is None:
        n = b.shape[0] if tb else b.shape[1]
        assert (b.shape[1] if tb else b.shape[0]) == kdim, (a.shape, b.shape, ta, tb)
        tn = _tile(n) if tn is None else tn
        b_spec = pl.BlockSpec((tn, tk), lambda i, j, k: (j, k)) if tb else pl.BlockSpec((tk, tn), lambda i, j, k: (k, j))
    else:
        n, tn, tk, b_spec = b_tiles["n"], b_tiles["tn"], b_tiles["tk"], b_tiles["spec"]
    nm, nn, nk = m // tm, n // tn, kdim // tk
    assert nm * tm == m and nn * tn == n and nk * tk == kdim, (m, n, kdim, tm, tn, tk)
    n_ex, n_out = len(extras), len(out_dtypes)
    n_alias = 0 if out_into is None else 1
    assert n_out == 1 or out_into is None
    dims = (((0 if ta else 1,), (1 if tb else 0,)), ((), ()))
    if epilogue is None:
        epilogue = lambda acc: (acc,)

    def body(a_ref, b_ref, *rest):
        ex_refs, out_refs, acc_ref = rest[:n_ex], rest[n_ex + n_alias:n_ex + n_alias + n_out], rest[-1]
        k = pl.program_id(2)

        @pl.when(k == 0)
        def _():
            acc_ref[...] = jnp.zeros_like(acc_ref)

        acc_ref[...] += lax.dot_general(a_ref[...].astype(MXU_DTYPE), b_ref[...].astype(MXU_DTYPE), dims,
                                        preferred_element_type=F32)

        @pl.when(k == nk - 1)
        def _():
            res = epilogue(acc_ref[...], *[e[...] for e in ex_refs])
            for o_ref, r in zip(out_refs, res):
                o_ref[...] = r.astype(o_ref.dtype)

    a_spec = pl.BlockSpec((tk, tm), lambda i, j, k: (k, i)) if ta else pl.BlockSpec((tm, tk), lambda i, j, k: (i, k))
    mn_spec = pl.BlockSpec((tm, tn), lambda i, j, k: (i, j))
    if out_into is None:
        outs = pl.pallas_call(
            body, name=name, grid=(nm, nn, nk),
            in_specs=[a_spec, b_spec] + [mn_spec] * n_ex,
            out_specs=[mn_spec] * n_out,
            out_shape=[jax.ShapeDtypeStruct((m, n), dt) for dt in out_dtypes],
            scratch_shapes=[pltpu.VMEM((tm, tn), F32)],
            compiler_params=_params("parallel", "parallel", "arbitrary"),
        )(a, b, *extras)
        return outs[0] if n_out == 1 else outs
    buf, buf_spec = out_into
    return pl.pallas_call(
        body, name=name, grid=(nm, nn, nk),
        in_specs=[a_spec, b_spec] + [mn_spec] * n_ex + [ANY],
        out_specs=buf_spec,
        out_shape=jax.ShapeDtypeStruct(buf.shape, buf.dtype),
        input_output_aliases={2 + n_ex: 0},
        scratch_shapes=[pltpu.VMEM((tm, tn), F32)],
        compiler_params=_params("parallel", "parallel", "arbitrary"),
    )(a, b, *extras, buf)


def _rmsnorm_fwd(x, g, *, name="rmsnorm_fwd"):
    t, d = x.shape
    bt = min(ROW_TILE, t)

    def body(x_ref, g_ref, o_ref):
        xv = x_ref[...]
        r = lax.rsqrt(jnp.mean(xv * xv, axis=-1, keepdims=True) + EPS)
        o_ref[...] = (xv * r * g_ref[...]).astype(o_ref.dtype)

    return pl.pallas_call(
        body, name=name, grid=(t // bt,),
        in_specs=[pl.BlockSpec((bt, d), lambda i: (i, 0)), pl.BlockSpec((1, d), lambda i: (0, 0))],
        out_specs=pl.BlockSpec((bt, d), lambda i: (i, 0)),
        out_shape=jax.ShapeDtypeStruct((t, d), MXU_DTYPE),
        compiler_params=_params("parallel"),
    )(x, g.reshape(1, d))


def _rmsnorm_bwd(x, g, dy, dres, *, name="rmsnorm_bwd"):
    t, d = x.shape
    bt = min(ROW_TILE, t)

    def body(x_ref, g_ref, dy_ref, dres_ref, dx_ref, dg_ref):
        @pl.when(pl.program_id(0) == 0)
        def _():
            dg_ref[...] = jnp.zeros_like(dg_ref)

        xv, dyv = x_ref[...], dy_ref[...].astype(F32)
        r = lax.rsqrt(jnp.mean(xv * xv, axis=-1, keepdims=True) + EPS)
        xhat = xv * r
        dxhat = dyv * g_ref[...]
        dx = r * (dxhat - xhat * jnp.mean(dxhat * xhat, axis=-1, keepdims=True))
        dx_ref[...] = dres_ref[...] + dx
        dg_ref[...] += jnp.sum(dyv * xhat, axis=0, keepdims=True)

    row = pl.BlockSpec((bt, d), lambda i: (i, 0))
    vec = pl.BlockSpec((1, d), lambda i: (0, 0))
    dx, dg = pl.pallas_call(
        body, name=name, grid=(t // bt,),
        in_specs=[row, vec, row, row], out_specs=[row, vec],
        out_shape=[jax.ShapeDtypeStruct((t, d), F32), jax.ShapeDtypeStruct((1, d), F32)],
        compiler_params=_params("arbitrary"),
    )(x, g.reshape(1, d), dy, dres)
    return dx, dg.reshape(d)


def _pair_rms(x, lo):
    x2 = x * x
    s_lo = jnp.sum(jnp.where(lo, x2, 0.0), axis=-1, keepdims=True)
    s_hi = jnp.sum(jnp.where(lo, 0.0, x2), axis=-1, keepdims=True)
    return jnp.where(lo, lax.rsqrt(s_lo / FOX_DIM + EPS), lax.rsqrt(s_hi / FOX_DIM + EPS))


def _pair_sum(x, lo):
    s_lo = jnp.sum(jnp.where(lo, x, 0.0), axis=-1, keepdims=True)
    s_hi = jnp.sum(jnp.where(lo, 0.0, x), axis=-1, keepdims=True)
    return jnp.where(lo, s_lo, s_hi)


def _prep_a_fwd(z, g_cq, g_ckv, g_qf, g_kf, b_f):
    t = z.shape[0]
    bt = min(PREP_TILE, t)
    hw = HEADS * FOX_DIM

    def body(z_ref, gcq_ref, gckv_ref, gqf_ref, gkf_ref, bf_ref, cqn_ref, ckvn_ref, qf_ref, kf_ref, vf_ref, logf_ref):
        i = pl.program_id(0)
        cq = z_ref[:, ZA_CQ:ZA_CQ + Q_LORA]
        cqn_ref[...] = (cq * lax.rsqrt(jnp.mean(cq * cq, axis=-1, keepdims=True) + EPS) * gcq_ref[...]).astype(cqn_ref.dtype)
        ckv = z_ref[:, ZA_CKV:ZA_CKV + KV_LORA]
        ckvn_ref[...] = (ckv * lax.rsqrt(jnp.mean(ckv * ckv, axis=-1, keepdims=True) + EPS) * gckv_ref[...]).astype(ckvn_ref.dtype)
        lo = _iota((1, LANES), 1) < FOX_DIM
        for p in range(HEADS // 2):
            sl = slice(p * LANES, (p + 1) * LANES)
            xq = z_ref[:, ZA_FQ + p * LANES:ZA_FQ + (p + 1) * LANES]
            qf_ref[:, sl] = (xq * _pair_rms(xq, lo) * gqf_ref[...]).astype(qf_ref.dtype)
            xk = z_ref[:, ZA_FK + p * LANES:ZA_FK + (p + 1) * LANES]
            kf_ref[:, sl] = (xk * _pair_rms(xk, lo) * gkf_ref[...]).astype(kf_ref.dtype)
        vf_ref[...] = z_ref[:, ZA_FV:ZA_FV + hw].astype(vf_ref.dtype)
        xl = z_ref[:, ZA_LAST:ZA_LAST + LANES] + bf_ref[...]
        logf = jnp.minimum(xl, 0.0) - jnp.log(1.0 + jnp.exp(-jnp.abs(xl)))
        row = i * bt + _iota((bt, LANES), 0)
        lane = _iota((bt, LANES), 1)
        logf_ref[...] = jnp.where((lane < HEADS) & (row >= PAD), logf, 0.0)

    def vec(w):
        return pl.BlockSpec((1, w), lambda i: (0, 0))

    def row(w):
        return pl.BlockSpec((bt, w), lambda i: (i, 0))

    return pl.pallas_call(
        body, name="attn_prep_a_fwd", grid=(t // bt,),
        in_specs=[row(ZA_W), vec(Q_LORA), vec(KV_LORA), vec(LANES), vec(LANES), vec(LANES)],
        out_specs=[row(Q_LORA), row(KV_LORA), row(hw), row(hw), row(hw), row(LANES)],
        out_shape=[jax.ShapeDtypeStruct((t, Q_LORA), MXU_DTYPE), jax.ShapeDtypeStruct((t, KV_LORA), MXU_DTYPE),
                   jax.ShapeDtypeStruct((t, hw), MXU_DTYPE), jax.ShapeDtypeStruct((t, hw), MXU_DTYPE),
                   jax.ShapeDtypeStruct((t, hw), MXU_DTYPE), jax.ShapeDtypeStruct((t, LANES), F32)],
        compiler_params=_params("parallel"),
    )(z, g_cq, g_ckv, g_qf, g_kf, b_f)


def _prep_a_bwd(z, g_cq, g_ckv, g_qf, g_kf, b_f, dcqn, dckvn, dqf, dkf, dvf, dlogf, dkpe):
    t = z.shape[0]
    bt = min(PREP_TILE, t)
    hw = HEADS * FOX_DIM

    def norm_bwd(x, g, dy):
        r = lax.rsqrt(jnp.mean(x * x, axis=-1, keepdims=True) + EPS)
        xhat = x * r
        dxhat = dy * g
        dx = r * (dxhat - xhat * jnp.mean(dxhat * xhat, axis=-1, keepdims=True))
        return dx, jnp.sum(dy * xhat, axis=0, keepdims=True)

    def body(z_ref, gcq_ref, gckv_ref, gqf_ref, gkf_ref, bf_ref, dcqn_ref, dckvn_ref, dqf_ref, dkf_ref, dvf_ref,
             dlogf_ref, dkpe_ref, dz_ref, dgcq_ref, dgckv_ref, dgqf_ref, dgkf_ref, dbf_ref):
        i = pl.program_id(0)

        @pl.when(i == 0)
        def _():
            for r in (dgcq_ref, dgckv_ref, dgqf_ref, dgkf_ref, dbf_ref):
                r[...] = jnp.zeros_like(r)

        dx, dg = norm_bwd(z_ref[:, ZA_CQ:ZA_CQ + Q_LORA], gcq_ref[...], dcqn_ref[...])
        dz_ref[:, ZA_CQ:ZA_CQ + Q_LORA] = dx.astype(dz_ref.dtype)
        dgcq_ref[...] += dg
        dx, dg = norm_bwd(z_ref[:, ZA_CKV:ZA_CKV + KV_LORA], gckv_ref[...], dckvn_ref[...])
        dz_ref[:, ZA_CKV:ZA_CKV + KV_LORA] = dx.astype(dz_ref.dtype)
        dgckv_ref[...] += dg
        lo = _iota((1, LANES), 1) < FOX_DIM
        for base, g_ref, dy_ref, dg_ref in ((ZA_FQ, gqf_ref, dqf_ref, dgqf_ref), (ZA_FK, gkf_ref, dkf_ref, dgkf_ref)):
            for p in range(HEADS // 2):
                x = z_ref[:, base + p * LANES:base + (p + 1) * LANES]
                dy = dy_ref[:, p * LANES:(p + 1) * LANES]
                r = _pair_rms(x, lo)
                xhat = x * r
                dxhat = dy * g_ref[...]
                dx = r * (dxhat - xhat * _pair_sum(dxhat * xhat, lo) / FOX_DIM)
                dz_ref[:, base + p * LANES:base + (p + 1) * LANES] = dx.astype(dz_ref.dtype)
                dg_ref[...] += jnp.sum(dy * xhat, axis=0, keepdims=True)
        dz_ref[:, ZA_FV:ZA_FV + hw] = dvf_ref[...].astype(dz_ref.dtype)
        xl = z_ref[:, ZA_LAST:ZA_LAST + LANES] + bf_ref[...]
        row = i * bt + _iota((bt, LANES), 0)
        lane = _iota((bt, LANES), 1)
        dfl = jnp.where((lane < HEADS) & (row >= PAD), dlogf_ref[...] / (1.0 + jnp.exp(xl)), 0.0)
        dbf_ref[...] += jnp.sum(dfl, axis=0, keepdims=True)
        dz_ref[:, ZA_LAST:ZA_LAST + LANES] = (dfl + dkpe_ref[...]).astype(dz_ref.dtype)

    def vec(w):
        return pl.BlockSpec((1, w), lambda i: (0, 0))

    def row(w):
        return pl.BlockSpec((bt, w), lambda i: (i, 0))

    return pl.pallas_call(
        body, name="attn_prep_a_bwd", grid=(t // bt,),
        in_specs=[row(ZA_W), vec(Q_LORA), vec(KV_LORA), vec(LANES), vec(LANES), vec(LANES),
                  row(Q_LORA), row(KV_LORA), row(hw), row(hw), row(hw), row(LANES), row(LANES)],
        out_specs=[row(ZA_W), vec(Q_LORA), vec(KV_LORA), vec(LANES), vec(LANES), vec(LANES)],
        out_shape=[jax.ShapeDtypeStruct((t, ZA_W), MXU_DTYPE), jax.ShapeDtypeStruct((1, Q_LORA), F32),
                   jax.ShapeDtypeStruct((1, KV_LORA), F32), jax.ShapeDtypeStruct((1, LANES), F32),
                   jax.ShapeDtypeStruct((1, LANES), F32), jax.ShapeDtypeStruct((1, LANES), F32)],
        compiler_params=_params("arbitrary"),
    )(z, g_cq, g_ckv, g_qf, g_kf, b_f, dcqn, dckvn, dqf, dkf, dvf, dlogf, dkpe)


def _cumsum_rows(x, *, reverse, name):
    t = x.shape[0]
    nblk = t // LANES

    def body(x_ref, f_ref, ft_ref, carry_ref):
        r = _iota((LANES, LANES), 0)
        c = _iota((LANES, LANES), 1)
        tri = jnp.where((c >= r) if reverse else (c <= r), 1.0, 0.0).astype(F32)
        carry_ref[...] = jnp.zeros_like(carry_ref)

        def step(s, _):
            b = (nblk - 1 - s) if reverse else s
            start = pl.multiple_of(b * LANES, LANES)
            blk = x_ref[pl.ds(start, LANES), :]
            cs = jnp.dot(tri, blk, precision=lax.Precision.HIGHEST, preferred_element_type=F32) + carry_ref[0:1, :]
            f_ref[pl.ds(start, LANES), :] = cs
            ft_ref[:, pl.ds(start, LANES)] = cs.T
            carry_ref[0:1, :] = cs[0:1, :] if reverse else cs[LANES - 1:LANES, :]
            return 0

        lax.fori_loop(0, nblk, step, 0)

    return pl.pallas_call(
        body, name=name, grid=(1,),
        in_specs=[pl.BlockSpec((t, LANES), lambda i: (0, 0))],
        out_specs=[pl.BlockSpec((t, LANES), lambda i: (0, 0)), pl.BlockSpec((LANES, t), lambda i: (0, 0))],
        out_shape=[jax.ShapeDtypeStruct((t, LANES), F32), jax.ShapeDtypeStruct((LANES, t), F32)],
        scratch_shapes=[pltpu.VMEM((8, LANES), F32)],
        compiler_params=_params("arbitrary"),
    )(x)


def _rope_partner(x, lane):
    half = MLA_ROPE // 2
    swapped = jnp.where(lane < KPE_LANE + half, pltpu.roll(x, LANES - half, axis=1), pltpu.roll(x, half, axis=1))
    return jnp.where((lane >= KPE_LANE) & (lane < KPE_LANE + MLA_ROPE), swapped, 0.0)


def _rope_tables(t):
    pos = (jnp.arange(t, dtype=jnp.int32) - PAD).astype(F32)
    inv_freq = ROPE_BASE ** (-jnp.arange(0, MLA_ROPE, 2, dtype=F32) / MLA_ROPE)
    ang = pos[:, None] * inv_freq[None, :]
    cos, sin = jnp.cos(ang), jnp.sin(ang)
    ones = jnp.ones((t, KPE_LANE), F32)
    tail = jnp.zeros((t, LANES - KPE_LANE - MLA_ROPE), F32)
    c_tab = jnp.concatenate([ones, cos, cos, tail + 1.0], axis=1)
    s_tab = jnp.concatenate([ones * 0.0, -sin, sin, tail], axis=1)
    return c_tab, s_tab


def _prep_b_fwd(q_raw, kv_raw, z, c_tab, s_tab, g_q, g_k):
    t = q_raw.shape[0]
    bt = min(PREP_TILE, t)
    qw = HEADS * LANES
    vw = HEADS * HEAD_V

    def body(q_ref, kv_ref, zl_ref, c_ref, s_ref, gq_ref, gk_ref, qn_ref, kn_ref, v_ref):
        lane = _iota((1, LANES), 1)
        kpe = jnp.where((lane >= KPE_LANE) & (lane < KPE_LANE + MLA_ROPE), zl_ref[...], 0.0)
        cv, sv = c_ref[...], s_ref[...]
        for h in range(HEADS):
            sl = slice(h * LANES, (h + 1) * LANES)
            for x, g_ref, o_ref in ((q_ref[:, sl], gq_ref, qn_ref), (kv_ref[:, sl] + kpe, gk_ref, kn_ref)):
                r = lax.rsqrt(jnp.sum(x * x, axis=-1, keepdims=True) / MLA_QK + EPS)
                xn = x * r * g_ref[...]
                o_ref[:, sl] = (xn * cv + _rope_partner(xn, lane) * sv).astype(o_ref.dtype)
        v_ref[...] = kv_ref[:, qw:qw + vw].astype(v_ref.dtype)

    def row(w):
        return pl.BlockSpec((bt, w), lambda i: (i, 0))

    vec = pl.BlockSpec((1, LANES), lambda i: (0, 0))
    return pl.pallas_call(
        body, name="attn_prep_b_fwd", grid=(t // bt,),
        in_specs=[row(qw), row(qw + vw), pl.BlockSpec((bt, LANES), lambda i: (i, ZA_LAST // LANES)),
                  row(LANES), row(LANES), vec, vec],
        out_specs=[row(qw), row(qw), row(vw)],
        out_shape=[jax.ShapeDtypeStruct((t, qw), MXU_DTYPE), jax.ShapeDtypeStruct((t, qw), MXU_DTYPE),
                   jax.ShapeDtypeStruct((t, vw), MXU_DTYPE)],
        compiler_params=_params("parallel"),
    )(q_raw, kv_raw, z, c_tab, s_tab, g_q, g_k)


def _prep_b_bwd(q_raw, kv_raw, z, c_tab, s_tab, g_q, g_k, dqn, dkn, dv):
    t = q_raw.shape[0]
    bt = min(PREP_TILE, t)
    qw = HEADS * LANES
    vw = HEADS * HEAD_V

    def body(q_ref, kv_ref, zl_ref, c_ref, s_ref, gq_ref, gk_ref, dqn_ref, dkn_ref, dv_ref,
             dq_ref, dkv_ref, dkpe_ref, dgq_ref, dgk_ref):
        @pl.when(pl.program_id(0) == 0)
        def _():
            dgq_ref[...] = jnp.zeros_like(dgq_ref)
            dgk_ref[...] = jnp.zeros_like(dgk_ref)

        lane = _iota((1, LANES), 1)
        rope_lanes = (lane >= KPE_LANE) & (lane < KPE_LANE + MLA_ROPE)
        kpe = jnp.where(rope_lanes, zl_ref[...], 0.0)
        cv, sv = c_ref[...], s_ref[...]
        dkpe = jnp.zeros((bt, LANES), F32)
        for h in range(HEADS):
            sl = slice(h * LANES, (h + 1) * LANES)
            for is_k, x, g_ref, dout, dg_ref in ((False, q_ref[:, sl], gq_ref, dqn_ref[:, sl], dgq_ref),
                                                  (True, kv_ref[:, sl] + kpe, gk_ref, dkn_ref[:, sl], dgk_ref)):
                r = lax.rsqrt(jnp.sum(x * x, axis=-1, keepdims=True) / MLA_QK + EPS)
                xhat = x * r
                dxn = dout * cv + _rope_partner(dout * sv, lane)
                dg_ref[...] += jnp.sum(dxn * xhat, axis=0, keepdims=True)
                dxhat = dxn * g_ref[...]
                dx = r * (dxhat - xhat * (jnp.sum(dxhat * xhat, axis=-1, keepdims=True) / MLA_QK))
                if is_k:
                    dkv_ref[:, sl] = jnp.where(lane < KPE_LANE, dx, 0.0).astype(dkv_ref.dtype)
                    dkpe = dkpe + jnp.where(rope_lanes, dx, 0.0)
                else:
                    dq_ref[:, sl] = dx.astype(dq_ref.dtype)
        dkv_ref[:, qw:qw + vw] = dv_ref[...].astype(dkv_ref.dtype)
        dkpe_ref[...] = dkpe

    def row(w):
        return pl.BlockSpec((bt, w), lambda i: (i, 0))

    vec = pl.BlockSpec((1, LANES), lambda i: (0, 0))
    return pl.pallas_call(
        body, name="attn_prep_b_bwd", grid=(t // bt,),
        in_specs=[row(qw), row(qw + vw), pl.BlockSpec((bt, LANES), lambda i: (i, ZA_LAST // LANES)),
                  row(LANES), row(LANES), vec, vec, row(qw), row(qw), row(vw)],
        out_specs=[row(qw), row(qw + vw), row(LANES), vec, vec],
        out_shape=[jax.ShapeDtypeStruct((t, qw), MXU_DTYPE), jax.ShapeDtypeStruct((t, qw + vw), MXU_DTYPE),
                   jax.ShapeDtypeStruct((t, LANES), F32), jax.ShapeDtypeStruct((1, LANES), F32),
                   jax.ShapeDtypeStruct((1, LANES), F32)],
        compiler_params=_params("arbitrary"),
    )(q_raw, kv_raw, z, c_tab, s_tab, g_q, g_k, dqn, dkn, dv)


NT_DIMS = (((1,), (1,)), ((), ()))
TN_DIMS = (((0,), (0,)), ((), ()))


def _head_qk(q_ref, k_ref, e, mla, lo):
    if mla:
        return q_ref[:, e * LANES:(e + 1) * LANES], k_ref[:, e * LANES:(e + 1) * LANES]
    q = q_ref[...]
    return jnp.where(lo if e == 0 else jnp.logical_not(lo), q, jnp.zeros_like(q)), k_ref[...]


def _attn_specs(mla, blk, q_map, k_map):
    w = 2 * LANES if mla else LANES
    q_spec = pl.BlockSpec((blk, w), lambda j, a, b: (q_map(a, b), j))
    k_spec = pl.BlockSpec((blk, w), lambda j, a, b: (k_map(a, b), j))
    qv_spec = pl.BlockSpec((blk, LANES), lambda j, a, b: (q_map(a, b), j))
    kv_spec = pl.BlockSpec((blk, LANES), lambda j, a, b: (k_map(a, b), j))
    fq_spec = pl.BlockSpec((blk, LANES), lambda j, a, b: (q_map(a, b), 0))
    fk_spec = pl.BlockSpec((8, blk), lambda j, a, b: (0, k_map(a, b)))
    return q_spec, k_spec, qv_spec, kv_spec, fq_spec, fk_spec


def _flash_fwd(q, k, v, f, f_t, *, mla, scale, name):
    t = q.shape[0]
    blk = min(ATTN_BLOCK, t)
    nb = t // blk
    pairs = HEADS // 2
    q_spec, k_spec, qv_spec, kv_spec, fq_spec, fk_spec = _attn_specs(
        mla, blk, lambda i, kk: i, lambda i, kk: jnp.minimum(kk, i))

    def body(*refs):
        if mla:
            q_ref, k_ref, v_ref, o_ref, lse_ref, m_s, l_s, acc_s = refs
            fq_ref = fk_ref = None
        else:
            q_ref, k_ref, v_ref, fq_ref, fk_ref, o_ref, lse_ref, m_s, l_s, acc_s = refs
        j, i, kk = pl.program_id(0), pl.program_id(1), pl.program_id(2)
        lo = _iota((1, LANES), 1) < HEAD_V

        @pl.when(kk == 0)
        def _():
            m_s[...] = jnp.full_like(m_s, NEG)
            l_s[...] = jnp.zeros_like(l_s)
            acc_s[...] = jnp.zeros_like(acc_s)

        def step(mask):
            vv = v_ref[...]
            for e in range(2):
                s, _, _ = _scores(q_ref, k_ref, fq_ref, fk_ref, e, j, mla, scale, lo, mask)
                m_prev = m_s[e]
                m_new = jnp.maximum(m_prev, jnp.max(s, axis=1, keepdims=True))
                alpha = jnp.exp(m_prev - m_new)
                p = jnp.exp(s - m_new)
                if mask is not None:
                    p = jnp.where(mask, p, 0.0)
                l_s[e] = alpha * l_s[e] + jnp.sum(p, axis=1, keepdims=True)
                acc_s[e] = alpha * acc_s[e] + jnp.dot(p.astype(MXU_DTYPE), vv, preferred_element_type=F32)
                m_s[e] = m_new

        _masked_and_plain(kk <= i, i, kk, blk, step)

        @pl.when(kk == nb - 1)
        def _():
            valid = (i * blk + _iota((blk, 1), 0)) >= PAD
            outs, lses = [], []
            for e in range(2):
                l = l_s[e]
                outs.append(acc_s[e] * jnp.where(l > 0.0, 1.0 / jnp.where(l > 0.0, l, 1.0), 0.0))
                lses.append(m_s[e] + jnp.log(jnp.where(l > 0.0, l, 1.0)))
            o = jnp.where(lo, outs[0], outs[1])
            o_ref[...] = jnp.where(valid, o, 0.0).astype(o_ref.dtype)
            lane = _iota((1, LANES), 1)
            lse_ref[...] = jnp.where(lane == 0, lses[0], jnp.where(lane == 1, lses[1], 0.0))

    in_specs = [q_spec, k_spec, kv_spec] + ([] if mla else [fq_spec, fk_spec])
    args = (q, k, v) + (() if mla else (f, f_t))
    hv = HEADS * HEAD_V
    return pl.pallas_call(
        body, name=name, grid=(pairs, nb, nb),
        in_specs=in_specs, out_specs=[qv_spec, qv_spec],
        out_shape=[jax.ShapeDtypeStruct((t, hv), F32), jax.ShapeDtypeStruct((t, hv), F32)],
        scratch_shapes=[pltpu.VMEM((2, blk, 1), F32), pltpu.VMEM((2, blk, 1), F32), pltpu.VMEM((2, blk, LANES), F32)],
        compiler_params=_params("parallel", "parallel", "arbitrary"),
    )(*args)


def _bwd_tile(q_ref, k_ref, v_ref, o_ref, do_ref, lse_ref, fq_ref, fk_ref, e, pair, mla, scale, lo, mask):
    s, qe, ke = _scores(q_ref, k_ref, fq_ref, fk_ref, e, pair, mla, scale, lo, mask)
    p = jnp.exp(s - lse_ref[:, e:e + 1])
    if mask is not None:
        p = jnp.where(mask, p, 0.0)
    do = do_ref[...]
    doe = jnp.where(lo if e == 0 else jnp.logical_not(lo), do, jnp.zeros_like(do))
    dp = lax.dot_general(doe, v_ref[...], NT_DIMS, preferred_element_type=F32)
    delta = jnp.sum(doe.astype(F32) * o_ref[...].astype(F32), axis=1, keepdims=True)
    return p, p * (dp - delta), qe, ke


def _flash_bwd_dq(q, k, v, o, do, lse, f, f_t, *, mla, scale, col0, name):
    t = q.shape[0]
    blk = min(ATTN_BLOCK, t)
    nb = t // blk
    pairs = HEADS // 2
    w = 2 * LANES if mla else LANES
    q_spec, k_spec, qv_spec, kv_spec, fq_spec, fk_spec = _attn_specs(
        mla, blk, lambda i, kk: i, lambda i, kk: jnp.minimum(kk, i))
    od_spec = pl.BlockSpec((blk, LANES), lambda j, i, kk: (i, col0 + j))

    def body(*refs):
        if mla:
            q_ref, k_ref, v_ref, o_ref, do_ref, lse_ref, dq_ref, dq_s = refs
            fq_ref = fk_ref = rs_ref = rs_s = None
        else:
            q_ref, k_ref, v_ref, o_ref, do_ref, lse_ref, fq_ref, fk_ref, dq_ref, rs_ref, dq_s, rs_s = refs
        j, i, kk = pl.program_id(0), pl.program_id(1), pl.program_id(2)
        lo = _iota((1, LANES), 1) < HEAD_V

        @pl.when(kk == 0)
        def _():
            dq_s[...] = jnp.zeros_like(dq_s)
            if not mla:
                rs_s[...] = jnp.zeros_like(rs_s)

        def step(mask):
            for e in range(2):
                _, ds, _, ke = _bwd_tile(q_ref, k_ref, v_ref, o_ref, do_ref, lse_ref, fq_ref, fk_ref, e, j, mla, scale,
                                         lo, mask)
                dq_s[e] += jnp.dot(ds.astype(MXU_DTYPE), ke, preferred_element_type=F32)
                if not mla:
                    rs_s[e] += jnp.sum(ds, axis=1, keepdims=True)

        _masked_and_plain(kk <= i, i, kk, blk, step)

        @pl.when(kk == nb - 1)
        def _():
            if mla:
                dq_ref[:, 0:LANES] = dq_s[0] * scale
                dq_ref[:, LANES:2 * LANES] = dq_s[1] * scale
            else:
                dq_ref[...] = jnp.where(lo, dq_s[0], dq_s[1]) * scale
                lane = _iota((1, LANES), 1)
                rs_ref[...] = jnp.where(lane == 0, rs_s[0], jnp.where(lane == 1, rs_s[1], 0.0))

    in_specs = [q_spec, k_spec, kv_spec, od_spec, od_spec, qv_spec] + ([] if mla else [fq_spec, fk_spec])
    args = (q, k, v, o, do, lse) + (() if mla else (f, f_t))
    out_specs = [q_spec] + ([] if mla else [qv_spec])
    out_shape = [jax.ShapeDtypeStruct((t, pairs * w), F32)]
    scratch = [pltpu.VMEM((2, blk, LANES), F32)]
    if not mla:
        out_shape.append(jax.ShapeDtypeStruct((t, pairs * LANES), F32))
        scratch.append(pltpu.VMEM((2, blk, 1), F32))
    outs = pl.pallas_call(
        body, name=name, grid=(pairs, nb, nb),
        in_specs=in_specs, out_specs=out_specs, out_shape=out_shape, scratch_shapes=scratch,
        compiler_params=_params("parallel", "parallel", "arbitrary"),
    )(*args)
    return outs[0] if mla else outs


def _flash_bwd_dkv(q, k, v, o, do, lse, f, f_t, *, mla, scale, col0, name):
    t = q.shape[0]
    blk = min(ATTN_BLOCK, t)
    nb = t // blk
    pairs = HEADS // 2
    w = 2 * LANES if mla else LANES
    q_spec, k_spec, qv_spec, kv_spec, fq_spec, fk_spec = _attn_specs(
        mla, blk, lambda a, b: jnp.maximum(a, b), lambda a, b: a)
    od_spec = pl.BlockSpec((blk, LANES), lambda j, a, b: (jnp.maximum(a, b), col0 + j))
    cs_spec = pl.BlockSpec((8, blk), lambda j, a, b: (j, a))

    def body(*refs):
        if mla:
            q_ref, k_ref, v_ref, o_ref, do_ref, lse_ref, dk_ref, dv_ref, dk_s, dv_s = refs
            fq_ref = fk_ref = cs_ref = cs_s = None
        else:
            q_ref, k_ref, v_ref, o_ref, do_ref, lse_ref, fq_ref, fk_ref, dk_ref, dv_ref, cs_ref, dk_s, dv_s, cs_s = refs
        j, kb, qb = pl.program_id(0), pl.program_id(1), pl.program_id(2)
        lo = _iota((1, LANES), 1) < HEAD_V

        @pl.when(qb == 0)
        def _():
            dk_s[...] = jnp.zeros_like(dk_s)
            dv_s[...] = jnp.zeros_like(dv_s)
            if not mla:
                cs_s[...] = jnp.zeros_like(cs_s)

        def step(mask):
            do = do_ref[...]
            for e in range(2):
                p, ds, _, _ = _bwd_tile(q_ref, k_ref, v_ref, o_ref, do_ref, lse_ref, fq_ref, fk_ref, e, j, mla, scale,
                                        lo, mask)
                dv_s[e] += lax.dot_general(p.astype(MXU_DTYPE), do, TN_DIMS, preferred_element_type=F32)
                q_src = q_ref[:, e * LANES:(e + 1) * LANES] if mla else q_ref[...]
                dk_s[e] += lax.dot_general(ds.astype(MXU_DTYPE), q_src, TN_DIMS, preferred_element_type=F32)
                if not mla:
                    cs_s[e] += jnp.sum(ds, axis=0, keepdims=True)

        _masked_and_plain(qb >= kb, qb, kb, blk, step)

        @pl.when(qb == nb - 1)
        def _():
            dv_ref[...] = jnp.where(lo, dv_s[0], dv_s[1])
            if mla:
                dk_ref[:, 0:LANES] = dk_s[0] * scale
                dk_ref[:, LANES:2 * LANES] = dk_s[1] * scale
            else:
                dk_ref[...] = jnp.where(lo, dk_s[0], dk_s[1]) * scale
                sub = _iota((8, 1), 0)
                cs_ref[...] = jnp.where(sub == 0, cs_s[0], jnp.where(sub == 1, cs_s[1], 0.0))

    in_specs = [q_spec, k_spec, kv_spec, od_spec, od_spec, qv_spec] + ([] if mla else [fq_spec, fk_spec])
    args = (q, k, v, o, do, lse) + (() if mla else (f, f_t))
    out_specs = [k_spec, kv_spec] + ([] if mla else [cs_spec])
    out_shape = [jax.ShapeDtypeStruct((t, pairs * w), F32), jax.ShapeDtypeStruct((t, HEADS * HEAD_V), F32)]
    scratch = [pltpu.VMEM((2, blk, LANES), F32), pltpu.VMEM((2, blk, LANES), F32)]
    if not mla:
        out_shape.append(jax.ShapeDtypeStruct((pairs * 8, t), F32))
        scratch.append(pltpu.VMEM((2, 1, blk), F32))
    return pl.pallas_call(
        body, name=name, grid=(pairs, nb, nb),
        in_specs=in_specs, out_specs=out_specs, out_shape=out_shape, scratch_shapes=scratch,
        compiler_params=_params("parallel", "parallel", "arbitrary"),
    )(*args)


ATTN_CHUNK = 640


def _for_chunks(n, ch, body):
    for c in range(n):
        body(c * ch)


def _select_lane(x, idx):
    return jnp.sum(jnp.where(_iota(x.shape, 1) == idx, x, 0.0), axis=1, keepdims=True)


def _select_row(x, idx):
    return jnp.sum(jnp.where(_iota(x.shape, 0) == idx, x, 0.0), axis=0, keepdims=True)


def _flash_fwd_chunked(q, k, v, f, f_t, *, mla, scale, name):
    t = q.shape[0]
    blk = min(ATTN_BLOCK, t)
    nb = t // blk
    pairs = HEADS // 2
    ch = min(ATTN_CHUNK, blk)
    assert blk % ch == 0
    q_spec, k_spec, qv_spec, kv_spec, fq_spec, fk_spec = _attn_specs(
        mla, blk, lambda i, kk: i, lambda i, kk: jnp.minimum(kk, i))
    lse_spec = pl.BlockSpec((8, blk), lambda j, i, kk: (j, i))

    def body(*refs):
        if mla:
            q_ref, k_ref, v_ref, o_ref, lse_ref, m_s, l_s, a_s, acc_s, s_s, p_s = refs
            fq_ref = fk_ref = None
        else:
            q_ref, k_ref, v_ref, fq_ref, fk_ref, o_ref, lse_ref, m_s, l_s, a_s, acc_s, s_s, p_s = refs
        j, i, kk = pl.program_id(0), pl.program_id(1), pl.program_id(2)
        lo = _iota((1, LANES), 1) < HEAD_V

        @pl.when(kk == 0)
        def _():
            m_s[...] = jnp.full_like(m_s, NEG)
            l_s[...] = jnp.zeros_like(l_s)
            acc_s[...] = jnp.zeros_like(acc_s)

        def step(masked):
            vv = v_ref[...]
            for e in range(2):
                qe, ke = _head_qk(q_ref, k_ref, e, mla, lo)
                s_s[...] = lax.dot_general(qe, ke, NT_DIMS, preferred_element_type=F32)
                fkr = None if mla else _select_row(fk_ref[...], 2 * j + e)

                def chunk(r0, e=e, fkr=fkr):
                    rows = pl.ds(r0, ch)
                    s = s_s[rows, :] * scale
                    if not mla:
                        s = s + _select_lane(fq_ref[rows, :], 2 * j + e) - fkr
                    if masked:
                        rpos = i * blk + r0 + _iota((ch, blk), 0)
                        cpos = kk * blk + _iota((ch, blk), 1)
                        mask = (cpos <= rpos) & (cpos >= PAD)
                        s = jnp.where(mask, s, NEG)
                    m_prev = m_s[e, rows, :]
                    m_new = jnp.maximum(m_prev, jnp.max(s, axis=1, keepdims=True))
                    alpha = jnp.exp(m_prev - m_new)
                    p = jnp.exp(s - m_new)
                    if masked:
                        p = jnp.where(mask, p, 0.0)
                    l_s[e, rows, :] = alpha * l_s[e, rows, :] + jnp.sum(p, axis=1, keepdims=True)
                    m_s[e, rows, :] = m_new
                    a_s[rows, :] = alpha
                    p_s[rows, :] = p.astype(p_s.dtype)

                _for_chunks(blk // ch, ch, chunk)
                acc_s[e] = a_s[...] * acc_s[e] + jnp.dot(p_s[...], vv, preferred_element_type=F32)

        needs_mask = (kk == i) | (kk == 0)

        @pl.when((kk <= i) & needs_mask)
        def _():
            step(True)

        @pl.when((kk <= i) & jnp.logical_not(needs_mask))
        def _():
            step(False)

        @pl.when(kk == nb - 1)
        def _():
            valid = (i * blk + _iota((blk, 1), 0)) >= PAD
            outs, lses = [], []
            for e in range(2):
                l = l_s[e]
                outs.append(acc_s[e] * jnp.where(l > 0.0, 1.0 / jnp.where(l > 0.0, l, 1.0), 0.0))
                lses.append(m_s[e] + jnp.log(jnp.where(l > 0.0, l, 1.0)))
            o = jnp.where(lo, outs[0], outs[1])
            o_ref[...] = jnp.where(valid, o, 0.0).astype(o_ref.dtype)
            lane = _iota((1, LANES), 1)
            lse_cols = jnp.where(lane == 0, lses[0], jnp.where(lane == 1, lses[1], 0.0))
            lse_ref[...] = lse_cols.T[0:8, :]

    in_specs = [q_spec, k_spec, kv_spec] + ([] if mla else [fq_spec, fk_spec])
    args = (q, k, v) + (() if mla else (f, f_t))
    hv = HEADS * HEAD_V
    return pl.pallas_call(
        body, name=name, grid=(pairs, nb, nb),
        in_specs=in_specs, out_specs=[qv_spec, lse_spec],
        out_shape=[jax.ShapeDtypeStruct((t, hv), F32), jax.ShapeDtypeStruct((pairs * 8, t), F32)],
        scratch_shapes=[pltpu.VMEM((2, blk, 1), F32), pltpu.VMEM((2, blk, 1), F32), pltpu.VMEM((blk, 1), F32),
                        pltpu.VMEM((2, blk, LANES), F32), pltpu.VMEM((blk, blk), F32), pltpu.VMEM((blk, blk), MXU_DTYPE)],
        compiler_params=_params("parallel", "parallel", "arbitrary"),
    )(*args)


def _delta_rows(do, o):
    t, width = o.shape
    bt = min(ATTN_BLOCK, t)
    n_heads = width // HEAD_V

    def body(do_ref, o_ref, d_ref):
        prod = do_ref[...].astype(F32) * o_ref[...]
        col = _iota((width, LANES), 0)
        first = _iota((width, LANES), 1) * HEAD_V
        sel = jnp.where((col >= first) & (col < first + HEAD_V), 1.0, 0.0).astype(F32)
        per_head = jnp.dot(prod, sel, precision=lax.Precision.HIGHEST, preferred_element_type=F32)
        d_ref[...] = per_head.T[0:n_heads, :]

    return pl.pallas_call(
        body, name="attn_delta", grid=(t // bt,),
        in_specs=[pl.BlockSpec((bt, width), lambda i: (i, 0)), pl.BlockSpec((bt, width), lambda i: (i, 0))],
        out_specs=pl.BlockSpec((n_heads, bt), lambda i: (0, i)),
        out_shape=jax.ShapeDtypeStruct((n_heads, t), F32),
        compiler_params=_params("parallel"),
    )(do, o)


def _flash_bwd_fused(q, k, v, do, lse_t, delta_t, f, f_t, *, mla, scale, col0, name):
    t = q.shape[0]
    blk = min(ATTN_BLOCK, t)
    nb = t // blk
    pairs = HEADS // 2
    ch = min(ATTN_CHUNK, blk)
    assert blk % ch == 0
    w = 2 * LANES if mla else LANES
    last = nb - 1
    qmap = lambda a, b: jnp.maximum(a, b)
    q_spec = pl.BlockSpec((blk, w), lambda j, a, b: (qmap(a, b), j))
    k_spec = pl.BlockSpec((blk, w), lambda j, a, b: (a, j))
    v_spec = pl.BlockSpec((blk, LANES), lambda j, a, b: (a, j))
    do_spec = pl.BlockSpec((blk, LANES), lambda j, a, b: (qmap(a, b), col0 + j))
    lse_spec = pl.BlockSpec((8, blk), lambda j, a, b: (j, qmap(a, b)))
    delta_spec = pl.BlockSpec((8, blk), lambda j, a, b: (col0 // (HEADS // 2), qmap(a, b)))
    fq_spec = pl.BlockSpec((8, blk), lambda j, a, b: (0, qmap(a, b)))
    fk_spec = pl.BlockSpec((blk, LANES), lambda j, a, b: (a, 0))
    dq_spec = pl.BlockSpec((blk, w), lambda j, a, b: (jnp.where(a == last, b, 0), j))
    rs_spec = pl.BlockSpec((8, blk), lambda j, a, b: (j, jnp.where(a == last, b, 0)))
    cs_spec = pl.BlockSpec((blk, LANES), lambda j, a, b: (a, j))

    def body(*refs):
        if mla:
            (q_ref, k_ref, v_ref, do_ref, lse_ref, delta_ref, dq_ref, dk_ref, dv_ref,
             dq_s, dk_s, dv_s, st_s, dpt_s, pt_s, dst_s) = refs
            fq_ref = fk_ref = rs_ref = cs_ref = rs_s = cs_s = None
        else:
            (q_ref, k_ref, v_ref, do_ref, lse_ref, delta_ref, fq_ref, fk_ref, dq_ref, dk_ref, dv_ref, rs_ref, cs_ref,
             dq_s, dk_s, dv_s, st_s, dpt_s, pt_s, dst_s, rs_s, cs_s) = refs
        j, kb, qb = pl.program_id(0), pl.program_id(1), pl.program_id(2)
        lo = _iota((1, LANES), 1) < HEAD_V

        @pl.when((kb == 0) & (qb == 0))
        def _():
            dq_s[...] = jnp.zeros_like(dq_s)
            if not mla:
                rs_s[...] = jnp.zeros_like(rs_s)

        @pl.when(qb == 0)
        def _():
            dk_s[...] = jnp.zeros_like(dk_s)
            dv_s[...] = jnp.zeros_like(dv_s)
            if not mla:
                cs_s[...] = jnp.zeros_like(cs_s)

        def step(masked):
            do = do_ref[...]
            vv = v_ref[...]
            for e in range(2):
                half = lo if e == 0 else jnp.logical_not(lo)
                qe, ke = _head_qk(q_ref, k_ref, e, mla, lo)
                doe = jnp.where(half, do, jnp.zeros_like(do))
                st_s[...] = lax.dot_general(ke, qe, NT_DIMS, preferred_element_type=F32)
                dpt_s[...] = lax.dot_general(vv, doe, NT_DIMS, preferred_element_type=F32)
                head = 2 * j + e
                lse_row = _select_row(lse_ref[...], e)
                delta_row = _select_row(delta_ref[...], head)
                fq_row = None if mla else _select_row(fq_ref[...], head)

                def chunk(r0, e=e, lse_row=lse_row, delta_row=delta_row, fq_row=fq_row, head=head):
                    rows = pl.ds(r0, ch)
                    s = st_s[rows, :] * scale
                    if not mla:
                        s = s + fq_row - _select_lane(fk_ref[rows, :], head)
                    p = jnp.exp(s - lse_row)
                    if masked:
                        kpos = kb * blk + r0 + _iota((ch, blk), 0)
                        qpos = qb * blk + _iota((ch, blk), 1)
                        p = jnp.where((kpos <= qpos) & (kpos >= PAD), p, 0.0)
                    ds = p * (dpt_s[rows, :] - delta_row)
                    pt_s[rows, :] = p.astype(pt_s.dtype)
                    dst_s[rows, :] = ds.astype(dst_s.dtype)
                    if not mla:
                        cs_s[e, rows, :] += jnp.sum(ds, axis=1, keepdims=True)
                        rs_s[qb, e] += jnp.sum(ds, axis=0, keepdims=True)

                _for_chunks(blk // ch, ch, chunk)
                dv_s[e] += jnp.dot(pt_s[...], do, preferred_element_type=F32)
                q_src = qe if mla else q_ref[...]
                dk_s[e] += jnp.dot(dst_s[...], q_src, preferred_element_type=F32)
                dq_s[qb, e] += lax.dot_general(dst_s[...], ke, TN_DIMS, preferred_element_type=F32)

        needs_mask = (qb == kb) | (kb == 0)

        @pl.when((qb >= kb) & needs_mask)
        def _():
            step(True)

        @pl.when((qb >= kb) & jnp.logical_not(needs_mask))
        def _():
            step(False)

        @pl.when(qb == last)
        def _():
            dv_ref[...] = jnp.where(lo, dv_s[0], dv_s[1])
            if mla:
                dk_ref[:, 0:LANES] = dk_s[0] * scale
                dk_ref[:, LANES:2 * LANES] = dk_s[1] * scale
            else:
                dk_ref[...] = jnp.where(lo, dk_s[0], dk_s[1]) * scale
                lane = _iota((1, LANES), 1)
                cs_ref[...] = jnp.where(lane == 0, cs_s[0], jnp.where(lane == 1, cs_s[1], 0.0))

        @pl.when(kb == last)
        def _():
            if mla:
                dq_ref[:, 0:LANES] = dq_s[qb, 0] * scale
                dq_ref[:, LANES:2 * LANES] = dq_s[qb, 1] * scale
            else:
                dq_ref[...] = jnp.where(lo, dq_s[qb, 0], dq_s[qb, 1]) * scale
                sub = _iota((8, 1), 0)
                rs_ref[...] = jnp.where(sub == 0, rs_s[qb, 0], jnp.where(sub == 1, rs_s[qb, 1], 0.0))

    in_specs = [q_spec, k_spec, v_spec, do_spec, lse_spec, delta_spec] + ([] if mla else [fq_spec, fk_spec])
    args = (q, k, v, do, lse_t, delta_t) + (() if mla else (f_t, f))
    hv = HEADS * HEAD_V
    out_specs = [dq_spec, k_spec, v_spec]
    out_shape = [jax.ShapeDtypeStruct((t, pairs * w), F32), jax.ShapeDtypeStruct((t, pairs * w), F32),
                 jax.ShapeDtypeStruct((t, hv), F32)]
    scratch = [pltpu.VMEM((nb, 2, blk, LANES), F32), pltpu.VMEM((2, blk, LANES), F32), pltpu.VMEM((2, blk, LANES), F32),
               pltpu.VMEM((blk, blk), F32), pltpu.VMEM((blk, blk), F32), pltpu.VMEM((blk, blk), MXU_DTYPE),
               pltpu.VMEM((blk, blk), MXU_DTYPE)]
    if not mla:
        out_specs += [rs_spec, cs_spec]
        out_shape += [jax.ShapeDtypeStruct((pairs * 8, t), F32), jax.ShapeDtypeStruct((t, hv), F32)]
        scratch += [pltpu.VMEM((nb, 2, 1, blk), F32), pltpu.VMEM((2, blk, 1), F32)]
    return pl.pallas_call(
        body, name=name, grid=(pairs, nb, nb),
        in_specs=in_specs, out_specs=out_specs, out_shape=out_shape, scratch_shapes=scratch,
        compiler_params=_params("parallel", "arbitrary", "arbitrary"),
    )(*args)


def _shift_down(x, halo, n):
    rows = x.shape[0]
    r = _iota((rows, 1), 0)
    out = pltpu.roll(x, n, axis=0)
    for s in range(n):
        out = jnp.where(r == s, halo[8 - n + s:8 - n + s + 1, :], out)
    return out


def _shift_up(x, halo, n):
    rows = x.shape[0]
    r = _iota((rows, 1), 0)
    out = pltpu.roll(x, rows - n, axis=0)
    for s in range(n):
        out = jnp.where(r == rows - n + s, halo[s:s + 1, :], out)
    return out


def _conv_specs(bt, nblk):
    d = D_MODEL
    per8 = bt // 8
    z_spec = pl.BlockSpec((bt, 3 * d), lambda i: (i, 0))
    prev_spec = pl.BlockSpec((8, 3 * d), lambda i: (jnp.maximum(i * per8 - 1, 0), 0))
    next_z = pl.BlockSpec((8, 3 * d), lambda i: (jnp.minimum((i + 1) * per8, nblk * per8 - 1), 0))
    next_d = pl.BlockSpec((8, d), lambda i: (jnp.minimum((i + 1) * per8, nblk * per8 - 1), 0))
    w_spec = pl.BlockSpec((8, d), lambda i: (0, 0))
    row_spec = pl.BlockSpec((bt, d), lambda i: (i, 0))
    return z_spec, prev_spec, next_z, next_d, w_spec, row_spec


def _conv_taps(z_ref, prev_ref, i):
    d = D_MODEL
    g = z_ref[:, d:2 * d] * z_ref[:, 2 * d:3 * d]
    gh = jnp.where(i > 0, prev_ref[:, d:2 * d] * prev_ref[:, 2 * d:3 * d], 0.0)
    return g, _shift_down(g, gh, 1), _shift_down(g, gh, 2)


def _conv_fwd(z, conv_w8):
    t = z.shape[0]
    bt = min(PREP_TILE, t)
    nblk = t // bt
    d = D_MODEL
    z_spec, prev_spec, _, _, w_spec, row_spec = _conv_specs(bt, nblk)

    def body(z_ref, prev_ref, w_ref, v_ref):
        g, g1, g2 = _conv_taps(z_ref, prev_ref, pl.program_id(0))
        y = w_ref[0:1, :] * g2 + w_ref[1:2, :] * g1 + w_ref[2:3, :] * g
        v_ref[...] = (z_ref[:, 0:d] * y).astype(v_ref.dtype)

    return pl.pallas_call(
        body, name="conv_fwd", grid=(nblk,),
        in_specs=[z_spec, prev_spec, w_spec], out_specs=row_spec,
        out_shape=jax.ShapeDtypeStruct((t, d), MXU_DTYPE),
        compiler_params=_params("parallel"),
    )(z, z, conv_w8)


def _conv_bwd(z, conv_w8, dv):
    t = z.shape[0]
    bt = min(PREP_TILE, t)
    nblk = t // bt
    d = D_MODEL
    z_spec, prev_spec, next_z, next_d, w_spec, row_spec = _conv_specs(bt, nblk)

    def body(z_ref, prev_ref, nz_ref, dv_ref, ndv_ref, w_ref, dz_ref, dw_ref):
        i = pl.program_id(0)

        @pl.when(i == 0)
        def _():
            dw_ref[...] = jnp.zeros_like(dw_ref)

        g, g1, g2 = _conv_taps(z_ref, prev_ref, i)
        w0, w1, w2 = w_ref[0:1, :], w_ref[1:2, :], w_ref[2:3, :]
        y = w0 * g2 + w1 * g1 + w2 * g
        dvv = dv_ref[...].astype(F32)
        gate_b = z_ref[:, 0:d]
        dy = dvv * gate_b
        dyn = jnp.where(i < nblk - 1, ndv_ref[...].astype(F32) * nz_ref[:, 0:d], 0.0)
        dg = w2 * dy + w1 * _shift_up(dy, dyn, 1) + w0 * _shift_up(dy, dyn, 2)
        dz_ref[:, 0:d] = (dvv * y).astype(dz_ref.dtype)
        dz_ref[:, d:2 * d] = (dg * z_ref[:, 2 * d:3 * d]).astype(dz_ref.dtype)
        dz_ref[:, 2 * d:3 * d] = (dg * z_ref[:, d:2 * d]).astype(dz_ref.dtype)
        sub = _iota((8, 1), 0)
        s0 = jnp.sum(dy * g2, axis=0, keepdims=True)
        s1 = jnp.sum(dy * g1, axis=0, keepdims=True)
        s2 = jnp.sum(dy * g, axis=0, keepdims=True)
        dw_ref[...] += jnp.where(sub == 0, s0, jnp.where(sub == 1, s1, jnp.where(sub == 2, s2, 0.0)))

    return pl.pallas_call(
        body, name="conv_bwd", grid=(nblk,),
        in_specs=[z_spec, prev_spec, next_z, row_spec, next_d, w_spec], out_specs=[z_spec, w_spec],
        out_shape=[jax.ShapeDtypeStruct((t, 3 * d), MXU_DTYPE), jax.ShapeDtypeStruct((8, d), F32)],
        compiler_params=_params("arbitrary"),
    )(z, z, z, dv, dv, conv_w8)


def _loss_head(h, target):
    t, d = h.shape
    bt = LOSS_TILE
    assert LANES % bt == 0 or bt == LANES
    off = LANES // bt

    def body(h_ref, y_ref, dh_ref, acc_ref):
        i = pl.program_id(0)

        @pl.when(i == 0)
        def _():
            acc_ref[...] = jnp.zeros_like(acc_ref)

        @pl.when(i < off)
        def _():
            dh_ref[...] = jnp.zeros_like(dh_ref)

        @pl.when(i >= off)
        def _():
            err = h_ref[...] - y_ref[...]
            dh_ref[...] = err / d
            acc_ref[...] += jnp.sum(err * err)

    dh, acc = pl.pallas_call(
        body, name="loss_head", grid=(t // bt,),
        in_specs=[pl.BlockSpec((bt, d), lambda i: (i, 0)), pl.BlockSpec((bt, d), lambda i: (jnp.maximum(i - off, 0), 0))],
        out_specs=[pl.BlockSpec((bt, d), lambda i: (i, 0)), pl.BlockSpec((8, LANES), lambda i: (0, 0))],
        out_shape=[jax.ShapeDtypeStruct((t, d), F32), jax.ShapeDtypeStruct((8, LANES), F32)],
        compiler_params=_params("arbitrary"),
    )(h, target)
    return dh, acc[0, 0] * (0.5 / d)


def _common_tile(rows, row_off, cap=512, align=8):
    for b in range(min(cap, rows) // align * align, 0, -align):
        if rows % b == 0 and row_off % b == 0:
            return b
    raise ValueError((rows, row_off))


def _round_up(n, m):
    return -(-n // m) * m


def _adamw(w, m, v, g_buf, row_off, col_off):
    rows, width = w.shape
    wpad = _round_up(width, LANES)
    assert col_off % wpad == 0
    bt = _common_tile(rows, row_off)

    def body(w_ref, m_ref, v_ref, g_ref, go_ref, d_ref, nm_ref, nv_ref):
        gv = g_ref[...]
        if wpad != width:
            gv = gv[:, :width]
        m_new = ADAM_B1 * m_ref[...] + (1.0 - ADAM_B1) * gv
        v_new = ADAM_B2 * v_ref[...] + (1.0 - ADAM_B2) * jnp.square(gv)
        m_hat = m_new / (1.0 - ADAM_B1 ** ADAM_STEP)
        v_hat = v_new / (1.0 - ADAM_B2 ** ADAM_STEP)
        go_ref[...] = gv
        d_ref[...] = -ADAM_LR * (m_hat / (jnp.sqrt(v_hat) + ADAM_EPS) + ADAM_WD * w_ref[...])
        nm_ref[...] = m_new
        nv_ref[...] = v_new

    spec = pl.BlockSpec((bt, width), lambda i: (i, 0))
    g_spec = pl.BlockSpec((bt, wpad), lambda i: (row_off // bt + i, col_off // wpad))
    return pl.pallas_call(
        body, name="adamw", grid=(rows // bt,),
        in_specs=[spec] * 3 + [g_spec], out_specs=[spec] * 4,
        out_shape=[jax.ShapeDtypeStruct((rows, width), F32)] * 4,
        compiler_params=_params("parallel"),
    )(w, m, v, g_buf)


def _add2(a, b, *, out_dtype, name):
    rows, width = a.shape
    bt = next(x for x in range(min(rows, 640), 0, -16) if rows % x == 0)

    def body(a_ref, b_ref, o_ref):
        o_ref[...] = (a_ref[...] + b_ref[...]).astype(o_ref.dtype)

    spec = pl.BlockSpec((bt, width), lambda i: (i, 0))
    return pl.pallas_call(
        body, name=name, grid=(rows // bt,), in_specs=[spec, spec], out_specs=spec,
        out_shape=jax.ShapeDtypeStruct((rows, width), out_dtype), compiler_params=_params("parallel"),
    )(a, b)


def _sum4(parts, slot, *, name):
    _, rows, width = parts.shape
    bt = next(x for x in range(min(rows, 640), 0, -16) if rows % x == 0)

    def body(slot_ref, p_ref, o_ref):
        p = [p_ref[n].astype(F32) for n in range(4)]
        o_ref[...] = ((p[0] + p[1]) + p[2]) + p[3]

    grid_spec = pltpu.PrefetchScalarGridSpec(
        num_scalar_prefetch=1, grid=(rows // bt,),
        in_specs=[pl.BlockSpec((4, bt, width), lambda i, s: (0, i, 0))],
        out_specs=pl.BlockSpec((None, bt, width), lambda i, s: (s[0], i, 0)))
    return pl.pallas_call(
        body, name=name, grid_spec=grid_spec,
        out_shape=jax.ShapeDtypeStruct((2, rows, width), F32), compiler_params=_params("parallel"),
    )(jnp.reshape(slot, (1,)).astype(jnp.int32), parts)


ANY = pl.BlockSpec(memory_space=pl.ANY)
CHIP_FLIPS = ((1, 0), (0, 1), (1, 1))


def _place():
    return lax.axis_index("x"), lax.axis_index("y"), lax.axis_index("c")


def _flip(v, f):
    return 1 - v if f else v


def _allgather_chips(slabs):
    _, rows, width = slabs.shape
    half = rows // 2

    def body(_, out_ref, send_sems, recv_sems):
        x, y, c = _place()
        me = 2 * x + y
        sibling = (x, y, 1 - c)
        my_rows = pl.ds(pl.multiple_of(c * half, 8), half)
        sib_rows = pl.ds(pl.multiple_of((1 - c) * half, 8), half)
        first, passed = [], []
        for n, (fx, fy) in enumerate(CHIP_FLIPS):
            px, py = _flip(x, fx), _flip(y, fy)
            peer = 2 * px + py
            first.append(pltpu.make_async_remote_copy(
                src_ref=out_ref.at[me, my_rows], dst_ref=out_ref.at[me, my_rows],
                send_sem=send_sems.at[n], recv_sem=recv_sems.at[n], device_id=(px, py, c), device_id_type=MESH))
            passed.append(pltpu.make_async_remote_copy(
                src_ref=out_ref.at[peer, my_rows], dst_ref=out_ref.at[peer, my_rows],
                send_sem=send_sems.at[3 + n], recv_sem=recv_sems.at[3 + n], device_id=sibling, device_id_type=MESH))
        for cp in first:
            cp.start()
        for n, (fx, fy) in enumerate(CHIP_FLIPS):
            peer = 2 * _flip(x, fx) + _flip(y, fy)
            pltpu.make_async_remote_copy(
                src_ref=out_ref.at[me, my_rows], dst_ref=out_ref.at[peer, my_rows],
                send_sem=send_sems.at[n], recv_sem=recv_sems.at[n], device_id=sibling, device_id_type=MESH).wait_recv()
            passed[n].start()
        for n, (fx, fy) in enumerate(CHIP_FLIPS):
            peer = 2 * _flip(x, fx) + _flip(y, fy)
            pltpu.make_async_remote_copy(
                src_ref=out_ref.at[me, sib_rows], dst_ref=out_ref.at[peer, sib_rows],
                send_sem=send_sems.at[3 + n], recv_sem=recv_sems.at[3 + n], device_id=sibling,
                device_id_type=MESH).wait_recv()
        for cp in first + passed:
            cp.wait_send()

    return pl.pallas_call(
        body, name="allgather_weights",
        in_specs=[ANY], out_specs=ANY,
        out_shape=jax.ShapeDtypeStruct(slabs.shape, slabs.dtype), input_output_aliases={0: 0},
        scratch_shapes=[pltpu.SemaphoreType.DMA((6,)), pltpu.SemaphoreType.DMA((6,))],
    )(slabs)


def _swap_halves(g):
    _, rows, width = g.shape
    half = rows // 2

    def body(g_ref, got_ref, send_sem, recv_sem):
        x, y, c = _place()
        away = pl.ds(pl.multiple_of((1 - c) * half, 8), half)
        cp = pltpu.make_async_remote_copy(
            src_ref=g_ref.at[:, away], dst_ref=got_ref, send_sem=send_sem, recv_sem=recv_sem,
            device_id=(x, y, 1 - c), device_id_type=MESH)
        cp.start()
        cp.wait()

    return pl.pallas_call(
        body, name="grad_swap_halves",
        in_specs=[ANY], out_specs=ANY,
        out_shape=jax.ShapeDtypeStruct((4, half, width), g.dtype),
        scratch_shapes=[pltpu.SemaphoreType.DMA, pltpu.SemaphoreType.DMA],
    )(g)


def _scatter_chips(s):
    _, rows, width = s.shape

    def body(s_ref, out_ref, send_sems, recv_sems, local_sem):
        x, y, c = _place()
        me = 2 * x + y
        mine = pltpu.make_async_copy(s_ref.at[me], out_ref.at[me], local_sem)
        mine.start()
        copies = []
        for n, (fx, fy) in enumerate(CHIP_FLIPS):
            px, py = _flip(x, fx), _flip(y, fy)
            copies.append(pltpu.make_async_remote_copy(
                src_ref=s_ref.at[2 * px + py], dst_ref=out_ref.at[me],
                send_sem=send_sems.at[n], recv_sem=recv_sems.at[n], device_id=(px, py, c), device_id_type=MESH))
        for cp in copies:
            cp.start()
        for n, (fx, fy) in enumerate(CHIP_FLIPS):
            peer = 2 * _flip(x, fx) + _flip(y, fy)
            pltpu.make_async_remote_copy(
                src_ref=s_ref.at[me], dst_ref=out_ref.at[peer],
                send_sem=send_sems.at[n], recv_sem=recv_sems.at[n], device_id=(x, y, c), device_id_type=MESH).wait_recv()
        for cp in copies:
            cp.wait_send()
        mine.wait()

    return pl.pallas_call(
        body, name="grad_scatter_chips",
        in_specs=[ANY], out_specs=ANY,
        out_shape=jax.ShapeDtypeStruct((4, rows, width), s.dtype),
        scratch_shapes=[pltpu.SemaphoreType.DMA((3,)), pltpu.SemaphoreType.DMA((3,)), pltpu.SemaphoreType.DMA],
    )(s)


def _join_halves(halves):
    def body(_, out_ref, send_sem, recv_sem):
        x, y, c = _place()
        cp = pltpu.make_async_remote_copy(
            src_ref=out_ref.at[c], dst_ref=out_ref.at[c], send_sem=send_sem, recv_sem=recv_sem,
            device_id=(x, y, 1 - c), device_id_type=MESH)
        cp.start()
        pltpu.make_async_remote_copy(
            src_ref=out_ref.at[c], dst_ref=out_ref.at[1 - c], send_sem=send_sem, recv_sem=recv_sem,
            device_id=(x, y, 1 - c), device_id_type=MESH).wait_recv()
        cp.wait_send()

    return pl.pallas_call(
        body, name="grad_join_halves",
        in_specs=[ANY], out_specs=ANY,
        out_shape=jax.ShapeDtypeStruct(halves.shape, halves.dtype), input_output_aliases={0: 0},
        scratch_shapes=[pltpu.SemaphoreType.DMA, pltpu.SemaphoreType.DMA],
    )(halves)


PACK_W = 1024
REPLICATED = ("g_mix", "g_mlp", "g_cq", "g_ckv", "g_q_mla", "g_k_mla", "g_q_fox", "g_k_fox", "b_forget")
WEIGHT_ORDER = ("meta_tokens", "g_mix", "g_mlp", "w_in_attn", "g_cq", "w_uq", "g_ckv", "w_ukv", "g_q_mla", "g_k_mla",
                "g_q_fox", "g_k_fox", "b_forget", "w_out_attn", "w_in_conv", "conv_w", "w_out_conv", "w_mlp_up",
                "w_mlp_down")
N_EVEN = 2
N_ODD = 2
SHARD_IN = ATTN_IN // 4
SHARD_MIX = D_MODEL // 4
SHARD_UQ = HEADS * MLA_QK // 4
SHARD_UKV = HEADS * (MLA_NOPE + HEAD_V) // 4
SHARD_CONV = 3 * D_MODEL // 4
SIDE_W = 256
PK_UP = (0, 0)
PK_DOWN = (4096, 0)
PK_CONV_IN = (8192, 0)
PK_ATTN_IN = (10240, 0)
PK_OUT_ATTN = (12288, 0)
PK_OUT_CONV = (12800, 0)
PK_SMALL = (8192, 768)
PK_UQ = (10240, 768)
PK_UKV = (11008, 768)
PK_ROWS = 13312
SMALL_ROWS = 64
SMALL_META = 0
SMALL_CONV = 16
SMALL_REP = 24
SMALL_BITS_ROWS = 48
MATRIX_PLACES = (("w_mlp_up", PK_UP), ("w_mlp_down", PK_DOWN), ("w_in_conv", PK_CONV_IN), ("w_in_attn", PK_ATTN_IN),
                 ("w_out_attn", PK_OUT_ATTN), ("w_out_conv", PK_OUT_CONV), ("w_uq", PK_UQ), ("w_ukv", PK_UKV))


def _put(buf, x, place, *, name, slab=None):
    row_off, col_off = place
    slabs = x.ndim == 3
    rows, w = x.shape[-2:]
    wpad = _round_up(w, LANES)
    assert col_off % wpad == 0
    bt = _common_tile(rows, row_off, align=16)

    def fill(x_ref, o_ref):
        v = x_ref[...].astype(o_ref.dtype)
        if wpad != w:
            v = jnp.concatenate([v, jnp.zeros((bt, wpad - w), o_ref.dtype)], axis=1)
        o_ref[...] = v

    def body(x_ref, _, o_ref):
        fill(x_ref, o_ref)

    if slab is not None:
        grid_spec = pltpu.PrefetchScalarGridSpec(
            num_scalar_prefetch=1, grid=(rows // bt,),
            in_specs=[pl.BlockSpec((bt, w), lambda i, s: (i, 0)), ANY],
            out_specs=pl.BlockSpec((None, bt, wpad), lambda i, s: (s[0], row_off // bt + i, col_off // wpad)))
        return pl.pallas_call(
            lambda s_ref, x_ref, _, o_ref: fill(x_ref, o_ref), name=name, grid_spec=grid_spec,
            out_shape=jax.ShapeDtypeStruct(buf.shape, buf.dtype), input_output_aliases={2: 0},
            compiler_params=_params("parallel"),
        )(jnp.reshape(slab, (1,)).astype(jnp.int32), x, buf)
    if slabs:
        grid = (4, rows // bt)
        x_spec = pl.BlockSpec((None, bt, w), lambda s, i: (s, i, 0))
        o_spec = pl.BlockSpec((None, bt, wpad), lambda s, i: (s, row_off // bt + i, col_off // wpad))
        sem = ("parallel", "parallel")
    else:
        grid = (rows // bt,)
        x_spec = pl.BlockSpec((bt, w), lambda i: (i, 0))
        o_spec = pl.BlockSpec((bt, wpad), lambda i: (row_off // bt + i, col_off // wpad))
        sem = ("parallel",)
    return pl.pallas_call(
        body, name=name, grid=grid, in_specs=[x_spec, ANY], out_specs=o_spec,
        out_shape=jax.ShapeDtypeStruct(buf.shape, buf.dtype), input_output_aliases={1: 0},
        compiler_params=_params(*sem),
    )(x, buf)


def _w_cols(place, layer, rows, width):
    base = (place[0] + layer * rows) // rows
    return dict(n=4 * width, tn=width, tk=rows, spec=pl.BlockSpec((None, rows, width), lambda i, j, k: (j, base, 0)))


def _w_cols_t(place, layer, rows, width):
    base = (place[0] + layer * rows) // rows
    return dict(n=rows, tn=rows, tk=width, spec=pl.BlockSpec((None, rows, width), lambda i, j, k: (k, base, 0)))


def _w_rows(place, layer, rows):
    base = (place[0] + layer * rows) // rows
    return dict(n=D_MODEL, tn=D_MODEL, tk=rows, spec=pl.BlockSpec((None, rows, D_MODEL), lambda i, j, k: (k, base, 0)))


def _w_rows_t(place, layer, rows):
    base = (place[0] + layer * rows) // rows
    return dict(n=4 * rows, tn=rows, tk=D_MODEL, spec=pl.BlockSpec((None, rows, D_MODEL), lambda i, j, k: (j, base, 0)))


def _g_cols(g, place, layer, rows, width):
    base = (place[0] + layer * rows) // rows
    return g, pl.BlockSpec((None, rows, width), lambda i, j, k: (j, base, 0))


def _g_rows(g, place, layer, rows):
    base = (place[0] + layer * rows) // rows
    return g, pl.BlockSpec((None, rows, D_MODEL), lambda i, j, k: (i, base, 0))


IN_PADW = _round_up(SHARD_IN, LANES)
IN_TAIL = ZA_FQ - SHARD_IN
IN_FL = SHARD_IN - HEADS
ZA_KPE = ZA_LAST + KPE_LANE


def _assemble_attn_in(gathered, layer):
    bt = 256
    base = (PK_ATTN_IN[0] + layer * D_MODEL) // bt
    assert 2 * SHARD_IN > ZA_FQ + MLA_ROPE and 3 * SHARD_IN < ATTN_IN - HEADS

    def body(s0, s1, s2, s3, o_ref):
        dt = o_ref.dtype
        z = lambda n: jnp.zeros((bt, n), dt)
        o_ref[...] = jnp.concatenate(
            [s0[:, :SHARD_IN], s1[:, :IN_TAIL], s1[:, IN_TAIL + MLA_ROPE:SHARD_IN], s2[:, :SHARD_IN], s3[:, :IN_FL],
             s3[:, IN_FL:SHARD_IN], z(KPE_LANE - HEADS), s1[:, IN_TAIL:IN_TAIL + MLA_ROPE],
             z(LANES - KPE_LANE - MLA_ROPE)], axis=1).astype(dt)

    def spec(s):
        return pl.BlockSpec((None, bt, IN_PADW), lambda i: (s, base + i, 0))

    return pl.pallas_call(
        body, name="assemble_attn_in", grid=(D_MODEL // bt,),
        in_specs=[spec(s) for s in range(4)], out_specs=pl.BlockSpec((bt, ZA_W), lambda i: (i, 0)),
        out_shape=jax.ShapeDtypeStruct((D_MODEL, ZA_W), MXU_DTYPE), compiler_params=_params("parallel"),
    )(gathered, gathered, gathered, gathered)


def _scatter_attn_in(g, dwa, layer):
    bt = 256
    base = (PK_ATTN_IN[0] + layer * D_MODEL) // bt
    fq1 = ZA_FQ + SHARD_IN - IN_TAIL - MLA_ROPE

    def body(d_ref, _, o_ref):
        pad = jnp.zeros((bt, IN_PADW - SHARD_IN), F32)
        pieces = (
            (d_ref[:, 0:SHARD_IN],),
            (d_ref[:, SHARD_IN:ZA_FQ], d_ref[:, ZA_KPE:ZA_KPE + MLA_ROPE], d_ref[:, ZA_FQ:fq1]),
            (d_ref[:, fq1:fq1 + SHARD_IN],),
            (d_ref[:, fq1 + SHARD_IN:ZA_LAST], d_ref[:, ZA_LAST:ZA_LAST + HEADS]),
        )
        for s in range(4):
            @pl.when(pl.program_id(0) == s)
            def _(s=s):
                o_ref[...] = jnp.concatenate(list(pieces[s]) + [pad], axis=1)

    return pl.pallas_call(
        body, name="scatter_attn_in", grid=(4, D_MODEL // bt),
        in_specs=[pl.BlockSpec((bt, ZA_W), lambda s, i: (i, 0)), ANY],
        out_specs=pl.BlockSpec((None, bt, IN_PADW), lambda s, i: (s, base + i, 0)),
        out_shape=jax.ShapeDtypeStruct(g.shape, g.dtype), input_output_aliases={1: 0},
        compiler_params=_params("parallel", "parallel"),
    )(dwa, g)


def _assemble_uq(gathered, layer):
    bt = 128
    base = (PK_UQ[0] + layer * Q_LORA) // bt
    col = PK_UQ[1] // SIDE_W

    def body(s0, s1, s2, s3, o_ref):
        dt = o_ref.dtype
        z = jnp.zeros((bt, LANES - MLA_QK), dt)
        parts = []
        for s_ref in (s0, s1, s2, s3):
            parts += [s_ref[:, 0:MLA_QK], z, s_ref[:, MLA_QK:2 * MLA_QK], z]
        o_ref[...] = jnp.concatenate(parts, axis=1).astype(dt)

    def spec(s):
        return pl.BlockSpec((None, bt, SIDE_W), lambda i: (s, base + i, col))

    return pl.pallas_call(
        body, name="assemble_uq", grid=(Q_LORA // bt,),
        in_specs=[spec(s) for s in range(4)], out_specs=pl.BlockSpec((bt, HEADS * LANES), lambda i: (i, 0)),
        out_shape=jax.ShapeDtypeStruct((Q_LORA, HEADS * LANES), MXU_DTYPE), compiler_params=_params("parallel"),
    )(gathered, gathered, gathered, gathered)


def _scatter_uq(g, dw, layer):
    bt = 128
    base = (PK_UQ[0] + layer * Q_LORA) // bt
    col = PK_UQ[1] // SIDE_W

    def body(d_ref, _, o_ref):
        o_ref[...] = jnp.concatenate([d_ref[:, 0:MLA_QK], d_ref[:, LANES:LANES + MLA_QK],
                                      jnp.zeros((bt, SIDE_W - 2 * MLA_QK), F32)], axis=1)

    return pl.pallas_call(
        body, name="scatter_uq", grid=(4, Q_LORA // bt),
        in_specs=[pl.BlockSpec((bt, 2 * LANES), lambda s, i: (i, s)), ANY],
        out_specs=pl.BlockSpec((None, bt, SIDE_W), lambda s, i: (s, base + i, col)),
        out_shape=jax.ShapeDtypeStruct(g.shape, g.dtype), input_output_aliases={1: 0},
        compiler_params=_params("parallel", "parallel"),
    )(dw, g)


def _assemble_ukv(gathered, layer):
    bt = KV_LORA
    base = (PK_UKV[0] + layer * KV_LORA) // bt
    col = PK_UKV[1] // SIDE_W
    hd = MLA_NOPE + HEAD_V

    def body(s0, s1, s2, s3, o_ref):
        dt = o_ref.dtype
        z = jnp.zeros((bt, LANES - MLA_NOPE), dt)
        keys, vals = [], []
        for s_ref in (s0, s1, s2, s3):
            for e in range(2):
                keys += [s_ref[:, e * hd:e * hd + MLA_NOPE], z]
                vals.append(s_ref[:, e * hd + MLA_NOPE:(e + 1) * hd])
        o_ref[...] = jnp.concatenate(keys + vals, axis=1).astype(dt)

    def spec(s):
        return pl.BlockSpec((None, bt, SIDE_W), lambda i: (s, base + i, col))

    return pl.pallas_call(
        body, name="assemble_ukv", grid=(1,),
        in_specs=[spec(s) for s in range(4)],
        out_specs=pl.BlockSpec((bt, HEADS * (LANES + HEAD_V)), lambda i: (i, 0)),
        out_shape=jax.ShapeDtypeStruct((KV_LORA, HEADS * (LANES + HEAD_V)), MXU_DTYPE), compiler_params=_params("parallel"),
    )(gathered, gathered, gathered, gathered)


def _scatter_ukv(g, dw, layer):
    bt = KV_LORA
    base = (PK_UKV[0] + layer * KV_LORA) // bt
    col = PK_UKV[1] // SIDE_W

    def body(k_ref, v_ref, _, o_ref):
        o_ref[...] = jnp.concatenate([k_ref[:, 0:MLA_NOPE], v_ref[:, 0:HEAD_V], k_ref[:, LANES:LANES + MLA_NOPE],
                                      v_ref[:, HEAD_V:2 * HEAD_V]], axis=1)

    return pl.pallas_call(
        body, name="scatter_ukv", grid=(4,),
        in_specs=[pl.BlockSpec((bt, 2 * LANES), lambda s: (0, s)),
                  pl.BlockSpec((bt, 2 * HEAD_V), lambda s: (0, HEADS * LANES // (2 * HEAD_V) + s)), ANY],
        out_specs=pl.BlockSpec((None, bt, SIDE_W), lambda s: (s, base, col)),
        out_shape=jax.ShapeDtypeStruct(g.shape, g.dtype), input_output_aliases={2: 0},
        compiler_params=_params("parallel"),
    )(dw, dw, g)


def _pad_lanes(v, n=LANES):
    return jnp.pad(v, (0, n - v.shape[0])).reshape(1, n)


def _relu2_up(acc):
    r = jnp.maximum(acc, 0.0)
    return acc, r * r


def _relu2_bwd(acc, u):
    return (acc * (2.0 * jnp.maximum(u, 0.0)),)


def _add_res(acc, res):
    return (acc + res,)


def _local_step(x, target, meta, small, gathered):
    seq = x.shape[0]
    t = seq + LANES
    d = D_MODEL
    h = jnp.concatenate([jnp.zeros((PAD, d), F32), meta.astype(F32), x], axis=0)
    c_tab, s_tab = _rope_tables(t)
    scale_mla, scale_fox = MLA_QK ** -0.5, FOX_DIM ** -0.5
    grads = {}
    saved = []
    g = jnp.zeros((4, PK_ROWS, PACK_W), F32)

    for layer in range(DEPTH):
        j = layer // 2
        sv = {"h_in": h}
        hn = _rmsnorm_fwd(h, small["g_mix"][layer])
        sv["hn"] = hn
        if layer % 2 == 0:
            w_in = _assemble_attn_in(gathered, j)
            w_uq = _assemble_uq(gathered, j)
            w_ukv = _assemble_ukv(gathered, j)
            out_place = PK_OUT_ATTN
            vecs = dict(
                g_cq=small["g_cq"][j].reshape(1, Q_LORA), g_ckv=small["g_ckv"][j].reshape(1, KV_LORA),
                g_qf=jnp.tile(small["g_q_fox"][j], 2).reshape(1, LANES), g_kf=jnp.tile(small["g_k_fox"][j], 2).reshape(1, LANES),
                b_f=_pad_lanes(small["b_forget"][j]), g_q=_pad_lanes(small["g_q_mla"][j]), g_k=_pad_lanes(small["g_k_mla"][j]))
            z = _matmul(hn, w_in, name="mm_attn_in")
            cqn, ckvn, qf, kf, vf, logf = _prep_a_fwd(z, vecs["g_cq"], vecs["g_ckv"], vecs["g_qf"], vecs["g_kf"], vecs["b_f"])
            f_cum, f_cum_t = _cumsum_rows(logf, reverse=False, name="cumsum_fwd")
            q_raw = _matmul(cqn, w_uq, name="mm_uq")
            kv_raw = _matmul(ckvn, w_ukv, name="mm_ukv")
            qn, kn, v_mla = _prep_b_fwd(q_raw, kv_raw, z, c_tab, s_tab, vecs["g_q"], vecs["g_k"])
            o_mla, lse_mla = _flash_fwd_chunked(qn, kn, v_mla, None, None, mla=True, scale=scale_mla, name="flash_fwd_mla")
            o_fox, lse_fox = _flash_fwd_chunked(qf, kf, vf, f_cum, f_cum_t, mla=False, scale=scale_fox,
                                                name="flash_fwd_fox")
            o = jnp.concatenate([o_mla, o_fox], axis=1)
            h = _matmul(o, gathered, b_tiles=_w_rows(out_place, j, SHARD_MIX), extras=(h,), epilogue=_add_res,
                        name="mm_mix_out")
            sv.update(w_in=w_in, w_uq=w_uq, w_ukv=w_ukv, out_place=out_place, vecs=vecs, z=z, cqn=cqn, ckvn=ckvn, qf=qf, kf=kf,
                      vf=vf, f_cum=f_cum, f_cum_t=f_cum_t, q_raw=q_raw, kv_raw=kv_raw, qn=qn, kn=kn, v_mla=v_mla, o=o,
                      lse_mla=lse_mla, lse_fox=lse_fox)
        else:
            out_place = PK_OUT_CONV
            conv_w8 = jnp.pad(small["conv_w"][j], ((0, 5), (0, 0)))
            z = _matmul(hn, gathered, b_tiles=_w_cols(PK_CONV_IN, j, d, SHARD_CONV), name="mm_conv_in")
            vmix = _conv_fwd(z, conv_w8)
            h = _matmul(vmix, gathered, b_tiles=_w_rows(out_place, j, SHARD_MIX), extras=(h,), epilogue=_add_res,
                        name="mm_mix_out")
            sv.update(out_place=out_place, conv_w8=conv_w8, z=z, vmix=vmix)
        sv["h_mid"] = h
        hn2 = _rmsnorm_fwd(h, small["g_mlp"][layer])
        u, a = _matmul(hn2, gathered, b_tiles=_w_cols(PK_UP, layer, d, d), epilogue=_relu2_up,
                       out_dtypes=(F32, MXU_DTYPE), name="mm_mlp_up")
        h = _matmul(a, gathered, b_tiles=_w_rows(PK_DOWN, layer, d), extras=(h,), epilogue=_add_res, name="mm_mlp_down")
        sv.update(hn2=hn2, u=u, a=a)
        saved.append(sv)

    dh, loss_local = _loss_head(h, target)

    dg_mix, dg_mlp = [None] * DEPTH, [None] * DEPTH
    per_even = {k: [None, None] for k in ("g_cq", "g_ckv", "g_q_mla", "g_k_mla", "g_q_fox", "g_k_fox", "b_forget")}
    per_odd = {"conv_w": [None, None]}
    for layer in reversed(range(DEPTH)):
        j = layer // 2
        sv = saved[layer]
        du = _matmul(dh, gathered, tb=True, b_tiles=_w_rows_t(PK_DOWN, layer, d), extras=(sv["u"],),
                     epilogue=_relu2_bwd, out_dtypes=(MXU_DTYPE,), name="mm_mlp_da")
        g = _matmul(sv["a"], dh, ta=True, out_into=_g_rows(g, PK_DOWN, layer, d), name="mm_dw_down")
        g = _matmul(sv["hn2"], du, ta=True, out_into=_g_cols(g, PK_UP, layer, d, d), name="mm_dw_up")
        dhn2 = _matmul(du, gathered, tb=True, b_tiles=_w_cols_t(PK_UP, layer, d, d), name="mm_mlp_dhn")
        dh, dg_mlp[layer] = _rmsnorm_bwd(sv["h_mid"], small["g_mlp"][layer], dhn2, dh)
        do = _matmul(dh, gathered, tb=True, b_tiles=_w_rows_t(sv["out_place"], j, SHARD_MIX), out_dtypes=(MXU_DTYPE,),
                     name="mm_mix_do")
        if layer % 2 == 0:
            vecs = sv["vecs"]
            g = _matmul(sv["o"], dh, ta=True, tm=SHARD_MIX, out_into=_g_rows(g, PK_OUT_ATTN, j, SHARD_MIX),
                        name="mm_dw_out")
            delta_t = _delta_rows(do, sv["o"])
            dqn, dkn, dv_mla = _flash_bwd_fused(sv["qn"], sv["kn"], sv["v_mla"], do, sv["lse_mla"], delta_t, None, None,
                                                mla=True, scale=scale_mla, col0=0, name="flash_bwd_mla")
            dqf, dkf, dvf, rs_t, cs = _flash_bwd_fused(sv["qf"], sv["kf"], sv["vf"], do, sv["lse_fox"], delta_t,
                                                       sv["f_cum"], sv["f_cum_t"], mla=False, scale=scale_fox,
                                                       col0=HEADS // 2, name="flash_bwd_fox")
            d_f = rs_t.reshape(HEADS // 2, 8, t)[:, :2, :].reshape(HEADS, t).T
            d_f = d_f - cs.reshape(t, HEADS // 2, LANES)[:, :, :2].reshape(t, HEADS)
            d_f = jnp.pad(d_f, ((0, 0), (0, LANES - HEADS)))
            dlogf, _ = _cumsum_rows(d_f, reverse=True, name="cumsum_bwd")
            dq_raw, dkv_raw, dkpe, dg_q, dg_k = _prep_b_bwd(sv["q_raw"], sv["kv_raw"], sv["z"], c_tab, s_tab, vecs["g_q"],
                                                            vecs["g_k"], dqn, dkn, dv_mla)
            g = _scatter_uq(g, _matmul(sv["cqn"], dq_raw, ta=True, name="mm_dw_uq"), j)
            g = _scatter_ukv(g, _matmul(sv["ckvn"], dkv_raw, ta=True, name="mm_dw_ukv"), j)
            dcqn = _matmul(dq_raw, sv["w_uq"], tb=True, name="mm_dcqn")
            dckvn = _matmul(dkv_raw, sv["w_ukv"], tb=True, name="mm_dckvn")
            dz, dg_cq, dg_ckv, dg_qf, dg_kf, db_f = _prep_a_bwd(
                sv["z"], vecs["g_cq"], vecs["g_ckv"], vecs["g_qf"], vecs["g_kf"], vecs["b_f"], dcqn, dckvn, dqf, dkf, dvf,
                dlogf, dkpe)
            g = _scatter_attn_in(g, _matmul(sv["hn"], dz, ta=True, name="mm_dw_attn_in"), j)
            per_even["g_cq"][j] = dg_cq[0]
            per_even["g_ckv"][j] = dg_ckv[0]
            per_even["g_q_mla"][j] = dg_q[0, :MLA_QK]
            per_even["g_k_mla"][j] = dg_k[0, :MLA_QK]
            per_even["g_q_fox"][j] = dg_qf[0, :FOX_DIM] + dg_qf[0, FOX_DIM:]
            per_even["g_k_fox"][j] = dg_kf[0, :FOX_DIM] + dg_kf[0, FOX_DIM:]
            per_even["b_forget"][j] = db_f[0, :HEADS]
            dhn = _matmul(dz, sv["w_in"], tb=True, name="mm_attn_dhn")
        else:
            g = _matmul(sv["vmix"], dh, ta=True, tm=SHARD_MIX, out_into=_g_rows(g, PK_OUT_CONV, j, SHARD_MIX),
                        name="mm_dw_out")
            dz, dcw = _conv_bwd(sv["z"], sv["conv_w8"], do)
            per_odd["conv_w"][j] = dcw[:3]
            g = _matmul(sv["hn"], dz, ta=True, tn=SHARD_CONV, out_into=_g_cols(g, PK_CONV_IN, j, d, SHARD_CONV),
                        name="mm_dw_conv_in")
            dhn = _matmul(dz, gathered, tb=True, b_tiles=_w_cols_t(PK_CONV_IN, j, d, SHARD_CONV), name="mm_conv_dhn")
        dh, dg_mix[layer] = _rmsnorm_bwd(sv["h_in"], small["g_mix"][layer], dhn, dh)

    grads["meta_tokens"] = dh[PAD:LANES]
    grads["g_mix"] = jnp.stack(dg_mix)
    grads["g_mlp"] = jnp.stack(dg_mlp)
    for k, v in list(per_even.items()) + list(per_odd.items()):
        grads[k] = jnp.stack(v)
    return loss_local, dh[LANES:], g, grads


def kernel(x, meta_tokens, g_mix, g_mlp, w_in_attn, g_cq, w_uq, g_ckv, w_ukv, g_q_mla, g_k_mla, g_q_fox, g_k_fox, b_forget, w_out_attn, w_in_conv, conv_w, w_out_conv, w_mlp_up, w_mlp_down, loss_target, m_meta_tokens, m_g_mix, m_g_mlp, m_w_in_attn, m_g_cq, m_w_uq, m_g_ckv, m_w_ukv, m_g_q_mla, m_g_k_mla, m_g_q_fox, m_g_k_fox, m_b_forget, m_w_out_attn, m_w_in_conv, m_conv_w, m_w_out_conv, m_w_mlp_up, m_w_mlp_down, v_meta_tokens, v_g_mix, v_g_mlp, v_w_in_attn, v_g_cq, v_w_uq, v_g_ckv, v_w_ukv, v_g_q_mla, v_g_k_mla, v_g_q_fox, v_g_k_fox, v_b_forget, v_w_out_attn, v_w_in_conv, v_conv_w, v_w_out_conv, v_w_mlp_up, v_w_mlp_down):
    args = dict(locals())
    weights = {n: args[n] for n in WEIGHT_ORDER}
    mom_m = {n: args["m_" + n] for n in WEIGHT_ORDER}
    mom_v = {n: args["v_" + n] for n in WEIGHT_ORDER}

    wire = jnp.bfloat16
    me = 2 * lax.axis_index("x") + lax.axis_index("y")
    buf = jnp.zeros((4, PK_ROWS, PACK_W), wire)
    for name, place in MATRIX_PLACES:
        w = weights[name]
        buf = _put(buf, w.reshape(-1, w.shape[-1]), place, name="pack_weights", slab=me)
    meta_bits = lax.bitcast_convert_type(meta_tokens, wire).reshape(2 * N_META, SIDE_W)
    conv_bits = lax.bitcast_convert_type(conv_w, wire).reshape(2 * N_ODD * 3, SIDE_W)
    bits = jnp.concatenate([meta_bits, conv_bits, jnp.zeros((SMALL_BITS_ROWS - 2 * N_META - 2 * N_ODD * 3, SIDE_W), wire)])
    buf = _put(buf, bits, PK_SMALL, name="pack_weights", slab=me)
    gathered = _allgather_chips(buf)
    got_bits = gathered[:, PK_SMALL[0]:PK_SMALL[0] + SMALL_BITS_ROWS, PK_SMALL[1]:PK_SMALL[1] + SIDE_W]
    meta_full = lax.bitcast_convert_type(got_bits[:, :2 * N_META].reshape(4, N_META, SIDE_W, 2), F32)
    meta_full = meta_full.transpose(1, 0, 2).reshape(N_META, D_MODEL)
    conv_full = lax.bitcast_convert_type(
        got_bits[:, 2 * N_META:2 * N_META + 2 * N_ODD * 3].reshape(4, N_ODD, 3, SIDE_W, 2), F32)
    small = {n: weights[n] for n in REPLICATED}
    small["conv_w"] = conv_full.transpose(1, 2, 0, 3).reshape(N_ODD, 3, D_MODEL)

    loss_local, grad_x, g, grads = _local_step(x[0], loss_target[0], meta_full, small, gathered)
    loss = lax.psum(loss_local, MESH_AXES)

    rep = jnp.concatenate([grads[n].reshape(-1) for n in REPLICATED])
    rep = jnp.pad(rep, (0, (SMALL_ROWS - SMALL_REP) * SIDE_W - rep.shape[0])).reshape(SMALL_ROWS - SMALL_REP, SIDE_W)
    g_meta = grads["meta_tokens"].reshape(N_META, 4, SIDE_W).transpose(1, 0, 2)
    g_conv = grads["conv_w"].reshape(N_ODD * 3, 4, SIDE_W).transpose(1, 0, 2)
    small4 = jnp.concatenate([g_meta, g_conv, jnp.zeros((4, SMALL_REP - SMALL_CONV - N_ODD * 3, SIDE_W), F32),
                              jnp.broadcast_to(rep[None], (4,) + rep.shape)], axis=1)
    g = _put(g, small4, PK_SMALL, name="pack_small_grads")
    half = PK_ROWS // 2
    c = lax.axis_index("c")
    got = _swap_halves(g)
    kept = lax.dynamic_slice_in_dim(g, c * half, half, axis=1)
    pair = _add2(kept.reshape(4 * half, PACK_W), got.reshape(4 * half, PACK_W), out_dtype=jnp.bfloat16,
                 name="grad_pair_sum").reshape(4, half, PACK_W)
    total = _sum4(_scatter_chips(pair), c, name="grad_chip_sum")
    g_tot = _join_halves(total).reshape(PK_ROWS, PACK_W)

    out = {}
    for name, place in MATRIX_PLACES:
        shape = weights[name].shape
        two_d = lambda a: a.reshape(-1, shape[-1])
        res = _adamw(two_d(weights[name]), two_d(mom_m[name]), two_d(mom_v[name]), g_tot, place[0], place[1])
        out[name] = [r.reshape(shape) for r in res]

    def small_pack(src):
        flat = jnp.concatenate([src[n].reshape(-1) for n in REPLICATED])
        flat = jnp.pad(flat, (0, (SMALL_ROWS - SMALL_REP) * SIDE_W - flat.shape[0])).reshape(SMALL_ROWS - SMALL_REP, SIDE_W)
        return jnp.concatenate([src["meta_tokens"], src["conv_w"].reshape(N_ODD * 3, SIDE_W),
                                jnp.zeros((SMALL_REP - SMALL_CONV - N_ODD * 3, SIDE_W), F32), flat])

    res = _adamw(small_pack(weights), small_pack(mom_m), small_pack(mom_v), g_tot, PK_SMALL[0], PK_SMALL[1])
    for name in ("meta_tokens", "conv_w") + REPLICATED:
        out[name] = []
    for r in res:
        out["meta_tokens"].append(r[SMALL_META:SMALL_META + N_META])
        out["conv_w"].append(r[SMALL_CONV:SMALL_CONV + N_ODD * 3].reshape(N_ODD, 3, SIDE_W))
        flat, off = r[SMALL_REP:].reshape(-1), 0
        for name in REPLICATED:
            n = weights[name].size
            out[name].append(flat[off:off + n].reshape(weights[name].shape))
            off += n
    return (loss, grad_x[None], *[out[n][0] for n in WEIGHT_ORDER], *[out[n][1] for n in WEIGHT_ORDER],
            *[out[n][2] for n in WEIGHT_ORDER], *[out[n][3] for n in WEIGHT_ORDER])
```

```python
import functools

import jax
import jax.numpy as jnp
from jax import lax
from jax.experimental import pallas as pl
from jax.experimental.pallas import tpu as pltpu

F32 = jnp.float32
MXU_DTYPE = jnp.bfloat16

D_MODEL = 1024
N_META = 16
LANES = 128
PAD = LANES - N_META
HEADS = 8
Q_LORA = 384
KV_LORA = 256
MLA_NOPE = 64
MLA_ROPE = 32
MLA_QK = MLA_NOPE + MLA_ROPE
HEAD_V = 64
FOX_DIM = 64
ROPE_BASE = 10000.0
D_FF = 4 * D_MODEL
DEPTH = 4
EPS = 1e-6
NEG = -1e30
ATTN_IN = Q_LORA + KV_LORA + MLA_ROPE + 3 * HEADS * FOX_DIM + HEADS

ZA_CQ = 0
ZA_CKV = Q_LORA
ZA_FQ = ZA_CKV + KV_LORA
ZA_FK = ZA_FQ + HEADS * FOX_DIM
ZA_FV = ZA_FK + HEADS * FOX_DIM
ZA_LAST = ZA_FV + HEADS * FOX_DIM
ZA_W = ZA_LAST + LANES
KPE_LANE = MLA_NOPE

ADAM_LR = 0.001
ADAM_B1 = 0.9
ADAM_B2 = 0.999
ADAM_EPS = 1e-08
ADAM_WD = 0.01
ADAM_STEP = 10

VMEM_LIMIT_BYTES = 52 * 1024 * 1024
ROW_TILE = 1040
PREP_TILE = 320
ATTN_BLOCK = 640
LOSS_TILE = 128
MAX_TILE = 1536

MESH_AXES = ("x", "y", "c")
MESH = pl.DeviceIdType.MESH


def _params(*sem):
    return pltpu.CompilerParams(dimension_semantics=sem, vmem_limit_bytes=VMEM_LIMIT_BYTES)


def _tile(n, cap=None):
    cap = MAX_TILE if cap is None else cap
    if n <= cap:
        return n
    best = None
    for t in range(LANES, cap + 1, LANES):
        if n % t == 0:
            best = t
    assert best is not None, n
    return best


def _iota(shape, dim):
    return lax.broadcasted_iota(jnp.int32, shape, dim)


def _matmul(a, b, *, ta=False, tb=False, extras=(), epilogue=None, out_dtypes=(F32,), name, b_tiles=None,
            out_into=None, tm=None, tn=None):
    if ta:
        kdim, m = a.shape
    else:
        m, kdim = a.shape
    row_tile = min(ROW_TILE, m)
    if ta:
        tm_auto, tk = _tile(m), min(ATTN_BLOCK, kdim)
    else:
        tm_auto, tk = (row_tile if m % row_tile == 0 else _tile(m)), _tile(kdim)
    tm = tm_auto if tm is None else tm
    if b_tiles is None:
        n = b.shape[0] if tb else b.shape[1]
        assert (b.shape[1] if tb else b.shape[0]) == kdim, (a.shape, b.shape, ta, tb)
        tn = _tile(n) if tn is None else tn
        b_spec = pl.BlockSpec((tn, tk), lambda i, j, k: (j, k)) if tb else pl.BlockSpec((tk, tn), lambda i, j, k: (k, j))
    else:
        n, tn, tk, b_spec = b_tiles["n"], b_tiles["tn"], b_tiles["tk"], b_tiles["spec"]
    nm, nn, nk = m // tm, n // tn, kdim // tk
    assert nm * tm == m and nn * tn == n and nk * tk == kdim, (m, n, kdim, tm, tn, tk)
    n_ex, n_out = len(extras), len(out_dtypes)
    n_alias = 0 if out_into is None else 1
    assert n_out == 1 or out_into is None
    dims = (((0 if ta else 1,), (1 if tb else 0,)), ((), ()))
    if epilogue is None:
        epilogue = lambda acc: (acc,)

    def body(a_ref, b_ref, *rest):
        ex_refs, out_refs, acc_ref = rest[:n_ex], rest[n_ex + n_alias:n_ex + n_alias + n_out], rest[-1]
        k = pl.program_id(2)

        @pl.when(k == 0)
        def _():
            acc_ref[...] = jnp.zeros_like(acc_ref)

        acc_ref[...] += lax.dot_general(a_ref[...].astype(MXU_DTYPE), b_ref[...].astype(MXU_DTYPE), dims,
                                        preferred_element_type=F32)

        @pl.when(k == nk - 1)
        def _():
            res = epilogue(acc_ref[...], *[e[...] for e in ex_refs])
            for o_ref, r in zip(out_refs, res):
                o_ref[...] = r.astype(o_ref.dtype)

    a_spec = pl.BlockSpec((tk, tm), lambda i, j, k: (k, i)) if ta else pl.BlockSpec((tm, tk), lambda i, j, k: (i, k))
    mn_spec = pl.BlockSpec((tm, tn), lambda i, j, k: (i, j))
    if out_into is None:
        outs = pl.pallas_call(
            body, name=name, grid=(nm, nn, nk),
            in_specs=[a_spec, b_spec] + [mn_spec] * n_ex,
            out_specs=[mn_spec] * n_out,
            out_shape=[jax.ShapeDtypeStruct((m, n), dt) for dt in out_dtypes],
            scratch_shapes=[pltpu.VMEM((tm, tn), F32)],
            compiler_params=_params("parallel", "parallel", "arbitrary"),
        )(a, b, *extras)
        return outs[0] if n_out == 1 else outs
    buf, buf_spec = out_into
    return pl.pallas_call(
        body, name=name, grid=(nm, nn, nk),
        in_specs=[a_spec, b_spec] + [mn_spec] * n_ex + [ANY],
        out_specs=buf_spec,
        out_shape=jax.ShapeDtypeStruct(buf.shape, buf.dtype),
        input_output_aliases={2 + n_ex: 0},
        scratch_shapes=[pltpu.VMEM((tm, tn), F32)],
        compiler_params=_params("parallel", "parallel", "arbitrary"),
    )(a, b, *extras, buf)


def _rmsnorm_fwd(x, g, *, name="rmsnorm_fwd"):
    t, d = x.shape
    bt = min(ROW_TILE, t)

    def body(x_ref, g_ref, o_ref):
        xv = x_ref[...]
        r = lax.rsqrt(jnp.mean(xv * xv, axis=-1, keepdims=True) + EPS)
        o_ref[...] = (xv * r * g_ref[...]).astype(o_ref.dtype)

    return pl.pallas_call(
        body, name=name, grid=(t // bt,),
        in_specs=[pl.BlockSpec((bt, d), lambda i: (i, 0)), pl.BlockSpec((1, d), lambda i: (0, 0))],
        out_specs=pl.BlockSpec((bt, d), lambda i: (i, 0)),
        out_shape=jax.ShapeDtypeStruct((t, d), MXU_DTYPE),
        compiler_params=_params("parallel"),
    )(x, g.reshape(1, d))


def _rmsnorm_bwd(x, g, dy, dres, *, name="rmsnorm_bwd"):
    t, d = x.shape
    bt = min(ROW_TILE, t)

    def body(x_ref, g_ref, dy_ref, dres_ref, dx_ref, dg_ref):
        @pl.when(pl.program_id(0) == 0)
        def _():
            dg_ref[...] = jnp.zeros_like(dg_ref)

        xv, dyv = x_ref[...], dy_ref[...].astype(F32)
        r = lax.rsqrt(jnp.mean(xv * xv, axis=-1, keepdims=True) + EPS)
        xhat = xv * r
        dxhat = dyv * g_ref[...]
        dx = r * (dxhat - xhat * jnp.mean(dxhat * xhat, axis=-1, keepdims=True))
        dx_ref[...] = dres_ref[...] + dx
        dg_ref[...] += jnp.sum(dyv * xhat, axis=0, keepdims=True)

    row = pl.BlockSpec((bt, d), lambda i: (i, 0))
    vec = pl.BlockSpec((1, d), lambda i: (0, 0))
    dx, dg = pl.pallas_call(
        body, name=name, grid=(t // bt,),
        in_specs=[row, vec, row, row], out_specs=[row, vec],
        out_shape=[jax.ShapeDtypeStruct((t, d), F32), jax.ShapeDtypeStruct((1, d), F32)],
        compiler_params=_params("arbitrary"),
    )(x, g.reshape(1, d), dy, dres)
    return dx, dg.reshape(d)


def _pair_rms(x, lo):
    x2 = x * x
    s_lo = jnp.sum(jnp.where(lo, x2, 0.0), axis=-1, keepdims=True)
    s_hi = jnp.sum(jnp.where(lo, 0.0, x2), axis=-1, keepdims=True)
    return jnp.where(lo, lax.rsqrt(s_lo / FOX_DIM + EPS), lax.rsqrt(s_hi / FOX_DIM + EPS))


def _pair_sum(x, lo):
    s_lo = jnp.sum(jnp.where(lo, x, 0.0), axis=-1, keepdims=True)
    s_hi = jnp.sum(jnp.where(lo, 0.0, x), axis=-1, keepdims=True)
    return jnp.where(lo, s_lo, s_hi)


def _prep_a_fwd(z, g_cq, g_ckv, g_qf, g_kf, b_f):
    t = z.shape[0]
    bt = min(PREP_TILE, t)
    hw = HEADS * FOX_DIM

    def body(z_ref, gcq_ref, gckv_ref, gqf_ref, gkf_ref, bf_ref, cqn_ref, ckvn_ref, qf_ref, kf_ref, vf_ref, logf_ref):
        i = pl.program_id(0)
        cq = z_ref[:, ZA_CQ:ZA_CQ + Q_LORA]
        cqn_ref[...] = (cq * lax.rsqrt(jnp.mean(cq * cq, axis=-1, keepdims=True) + EPS) * gcq_ref[...]).astype(cqn_ref.dtype)
        ckv = z_ref[:, ZA_CKV:ZA_CKV + KV_LORA]
        ckvn_ref[...] = (ckv * lax.rsqrt(jnp.mean(ckv * ckv, axis=-1, keepdims=True) + EPS) * gckv_ref[...]).astype(ckvn_ref.dtype)
        lo = _iota((1, LANES), 1) < FOX_DIM
        for p in range(HEADS // 2):
            sl = slice(p * LANES, (p + 1) * LANES)
            xq = z_ref[:, ZA_FQ + p * LANES:ZA_FQ + (p + 1) * LANES]
            qf_ref[:, sl] = (xq * _pair_rms(xq, lo) * gqf_ref[...]).astype(qf_ref.dtype)
            xk = z_ref[:, ZA_FK + p * LANES:ZA_FK + (p + 1) * LANES]
            kf_ref[:, sl] = (xk * _pair_rms(xk, lo) * gkf_ref[...]).astype(kf_ref.dtype)
        vf_ref[...] = z_ref[:, ZA_FV:ZA_FV + hw].astype(vf_ref.dtype)
        xl = z_ref[:, ZA_LAST:ZA_LAST + LANES] + bf_ref[...]
        logf = jnp.minimum(xl, 0.0) - jnp.log(1.0 + jnp.exp(-jnp.abs(xl)))
        row = i * bt + _iota((bt, LANES), 0)
        lane = _iota((bt, LANES), 1)
        logf_ref[...] = jnp.where((lane < HEADS) & (row >= PAD), logf, 0.0)

    def vec(w):
        return pl.BlockSpec((1, w), lambda i: (0, 0))

    def row(w):
        return pl.BlockSpec((bt, w), lambda i: (i, 0))

    return pl.pallas_call(
        body, name="attn_prep_a_fwd", grid=(t // bt,),
        in_specs=[row(ZA_W), vec(Q_LORA), vec(KV_LORA), vec(LANES), vec(LANES), vec(LANES)],
        out_specs=[row(Q_LORA), row(KV_LORA), row(hw), row(hw), row(hw), row(LANES)],
        out_shape=[jax.ShapeDtypeStruct((t, Q_LORA), MXU_DTYPE), jax.ShapeDtypeStruct((t, KV_LORA), MXU_DTYPE),
                   jax.ShapeDtypeStruct((t, hw), MXU_DTYPE), jax.ShapeDtypeStruct((t, hw), MXU_DTYPE),
                   jax.ShapeDtypeStruct((t, hw), MXU_DTYPE), jax.ShapeDtypeStruct((t, LANES), F32)],
        compiler_params=_params("parallel"),
    )(z, g_cq, g_ckv, g_qf, g_kf, b_f)


def _prep_a_bwd(z, g_cq, g_ckv, g_qf, g_kf, b_f, dcqn, dckvn, dqf, dkf, dvf, dlogf, dkpe):
    t = z.shape[0]
    bt = min(PREP_TILE, t)
    hw = HEADS * FOX_DIM

    def norm_bwd(x, g, dy):
        r = lax.rsqrt(jnp.mean(x * x, axis=-1, keepdims=True) + EPS)
        xhat = x * r
        dxhat = dy * g
        dx = r * (dxhat - xhat * jnp.mean(dxhat * xhat, axis=-1, keepdims=True))
        return dx, jnp.sum(dy * xhat, axis=0, keepdims=True)

    def body(z_ref, gcq_ref, gckv_ref, gqf_ref, gkf_ref, bf_ref, dcqn_ref, dckvn_ref, dqf_ref, dkf_ref, dvf_ref,
             dlogf_ref, dkpe_ref, dz_ref, dgcq_ref, dgckv_ref, dgqf_ref, dgkf_ref, dbf_ref):
        i = pl.program_id(0)

        @pl.when(i == 0)
        def _():
            for r in (dgcq_ref, dgckv_ref, dgqf_ref, dgkf_ref, dbf_ref):
                r[...] = jnp.zeros_like(r)

        dx, dg = norm_bwd(z_ref[:, ZA_CQ:ZA_CQ + Q_LORA], gcq_ref[...], dcqn_ref[...])
        dz_ref[:, ZA_CQ:ZA_CQ + Q_LORA] = dx.astype(dz_ref.dtype)
        dgcq_ref[...] += dg
        dx, dg = norm_bwd(z_ref[:, ZA_CKV:ZA_CKV + KV_LORA], gckv_ref[...], dckvn_ref[...])
        dz_ref[:, ZA_CKV:ZA_CKV + KV_LORA] = dx.astype(dz_ref.dtype)
        dgckv_ref[...] += dg
        lo = _iota((1, LANES), 1) < FOX_DIM
        for base, g_ref, dy_ref, dg_ref in ((ZA_FQ, gqf_ref, dqf_ref, dgqf_ref), (ZA_FK, gkf_ref, dkf_ref, dgkf_ref)):
            for p in range(HEADS // 2):
                x = z_ref[:, base + p * LANES:base + (p + 1) * LANES]
                dy = dy_ref[:, p * LANES:(p + 1) * LANES]
                r = _pair_rms(x, lo)
                xhat = x * r
                dxhat = dy * g_ref[...]
                dx = r * (dxhat - xhat * _pair_sum(dxhat * xhat, lo) / FOX_DIM)
                dz_ref[:, base + p * LANES:base + (p + 1) * LANES] = dx.astype(dz_ref.dtype)
                dg_ref[...] += jnp.sum(dy * xhat, axis=0, keepdims=True)
        dz_ref[:, ZA_FV:ZA_FV + hw] = dvf_ref[...].astype(dz_ref.dtype)
        xl = z_ref[:, ZA_LAST:ZA_LAST + LANES] + bf_ref[...]
        row = i * bt + _iota((bt, LANES), 0)
        lane = _iota((bt, LANES), 1)
        dfl = jnp.where((lane < HEADS) & (row >= PAD), dlogf_ref[...] / (1.0 + jnp.exp(xl)), 0.0)
        dbf_ref[...] += jnp.sum(dfl, axis=0, keepdims=True)
        dz_ref[:, ZA_LAST:ZA_LAST + LANES] = (dfl + dkpe_ref[...]).astype(dz_ref.dtype)

    def vec(w):
        return pl.BlockSpec((1, w), lambda i: (0, 0))

    def row(w):
        return pl.BlockSpec((bt, w), lambda i: (i, 0))

    return pl.pallas_call(
        body, name="attn_prep_a_bwd", grid=(t // bt,),
        in_specs=[row(ZA_W), vec(Q_LORA), vec(KV_LORA), vec(LANES), vec(LANES), vec(LANES),
                  row(Q_LORA), row(KV_LORA), row(hw), row(hw), row(hw), row(LANES), row(LANES)],
        out_specs=[row(ZA_W), vec(Q_LORA), vec(KV_LORA), vec(LANES), vec(LANES), vec(LANES)],
        out_shape=[jax.ShapeDtypeStruct((t, ZA_W), MXU_DTYPE), jax.ShapeDtypeStruct((1, Q_LORA), F32),
                   jax.ShapeDtypeStruct((1, KV_LORA), F32), jax.ShapeDtypeStruct((1, LANES), F32),
                   jax.ShapeDtypeStruct((1, LANES), F32), jax.ShapeDtypeStruct((1, LANES), F32)],
        compiler_params=_params("arbitrary"),
    )(z, g_cq, g_ckv, g_qf, g_kf, b_f, dcqn, dckvn, dqf, dkf, dvf, dlogf, dkpe)


def _cumsum_rows(x, *, reverse, name, out_scale=1.0):
    t = x.shape[0]
    nblk = t // LANES

    def body(x_ref, f_ref, ft_ref, carry_ref):
        r = _iota((LANES, LANES), 0)
        c = _iota((LANES, LANES), 1)
        tri = jnp.where((c >= r) if reverse else (c <= r), 1.0, 0.0).astype(F32)
        carry_ref[...] = jnp.zeros_like(carry_ref)

        def step(s, _):
            b = (nblk - 1 - s) if reverse else s
            start = pl.multiple_of(b * LANES, LANES)
            blk = x_ref[pl.ds(start, LANES), :]
            cs = jnp.dot(tri, blk, precision=lax.Precision.HIGHEST, preferred_element_type=F32) + carry_ref[0:1, :]
            scaled = cs if out_scale == 1.0 else cs * out_scale
            f_ref[pl.ds(start, LANES), :] = scaled
            ft_ref[:, pl.ds(start, LANES)] = scaled.T
            carry_ref[0:1, :] = cs[0:1, :] if reverse else cs[LANES - 1:LANES, :]
            return 0

        lax.fori_loop(0, nblk, step, 0)

    return pl.pallas_call(
        body, name=name, grid=(1,),
        in_specs=[pl.BlockSpec((t, LANES), lambda i: (0, 0))],
        out_specs=[pl.BlockSpec((t, LANES), lambda i: (0, 0)), pl.BlockSpec((LANES, t), lambda i: (0, 0))],
        out_shape=[jax.ShapeDtypeStruct((t, LANES), F32), jax.ShapeDtypeStruct((LANES, t), F32)],
        scratch_shapes=[pltpu.VMEM((8, LANES), F32)],
        compiler_params=_params("arbitrary"),
    )(x)


def _rope_partner(x, lane):
    half = MLA_ROPE // 2
    swapped = jnp.where(lane < KPE_LANE + half, pltpu.roll(x, LANES - half, axis=1), pltpu.roll(x, half, axis=1))
    return jnp.where((lane >= KPE_LANE) & (lane < KPE_LANE + MLA_ROPE), swapped, 0.0)


def _rope_tables(t):
    pos = (jnp.arange(t, dtype=jnp.int32) - PAD).astype(F32)
    inv_freq = ROPE_BASE ** (-jnp.arange(0, MLA_ROPE, 2, dtype=F32) / MLA_ROPE)
    ang = pos[:, None] * inv_freq[None, :]
    cos, sin = jnp.cos(ang), jnp.sin(ang)
    ones = jnp.ones((t, KPE_LANE), F32)
    tail = jnp.zeros((t, LANES - KPE_LANE - MLA_ROPE), F32)
    c_tab = jnp.concatenate([ones, cos, cos, tail + 1.0], axis=1)
    s_tab = jnp.concatenate([ones * 0.0, -sin, sin, tail], axis=1)
    return c_tab, s_tab


def _prep_b_fwd(q_raw, kv_raw, z, c_tab, s_tab, g_q, g_k):
    t = q_raw.shape[0]
    bt = min(PREP_TILE, t)
    qw = HEADS * LANES
    vw = HEADS * HEAD_V

    def body(q_ref, kv_ref, zl_ref, c_ref, s_ref, gq_ref, gk_ref, qn_ref, kn_ref, v_ref):
        lane = _iota((1, LANES), 1)
        kpe = jnp.where((lane >= KPE_LANE) & (lane < KPE_LANE + MLA_ROPE), zl_ref[...], 0.0)
        cv, sv = c_ref[...], s_ref[...]
        for h in range(HEADS):
            sl = slice(h * LANES, (h + 1) * LANES)
            for x, g_ref, o_ref in ((q_ref[:, sl], gq_ref, qn_ref), (kv_ref[:, sl] + kpe, gk_ref, kn_ref)):
                r = lax.rsqrt(jnp.sum(x * x, axis=-1, keepdims=True) / MLA_QK + EPS)
                xn = x * r * g_ref[...]
                o_ref[:, sl] = (xn * cv + _rope_partner(xn, lane) * sv).astype(o_ref.dtype)
        v_ref[...] = kv_ref[:, qw:qw + vw].astype(v_ref.dtype)

    def row(w):
        return pl.BlockSpec((bt, w), lambda i: (i, 0))

    vec = pl.BlockSpec((1, LANES), lambda i: (0, 0))
    return pl.pallas_call(
        body, name="attn_prep_b_fwd", grid=(t // bt,),
        in_specs=[row(qw), row(qw + vw), pl.BlockSpec((bt, LANES), lambda i: (i, ZA_LAST // LANES)),
                  row(LANES), row(LANES), vec, vec],
        out_specs=[row(qw), row(qw), row(vw)],
        out_shape=[jax.ShapeDtypeStruct((t, qw), MXU_DTYPE), jax.ShapeDtypeStruct((t, qw), MXU_DTYPE),
                   jax.ShapeDtypeStruct((t, vw), MXU_DTYPE)],
        compiler_params=_params("parallel"),
    )(q_raw, kv_raw, z, c_tab, s_tab, g_q, g_k)


def _prep_b_bwd(q_raw, kv_raw, z, c_tab, s_tab, g_q, g_k, dqn, dkn, dv):
    t = q_raw.shape[0]
    bt = min(PREP_TILE, t)
    qw = HEADS * LANES
    vw = HEADS * HEAD_V

    def body(q_ref, kv_ref, zl_ref, c_ref, s_ref, gq_ref, gk_ref, dqn_ref, dkn_ref, dv_ref,
             dq_ref, dkv_ref, dkpe_ref, dgq_ref, dgk_ref):
        @pl.when(pl.program_id(0) == 0)
        def _():
            dgq_ref[...] = jnp.zeros_like(dgq_ref)
            dgk_ref[...] = jnp.zeros_like(dgk_ref)

        lane = _iota((1, LANES), 1)
        rope_lanes = (lane >= KPE_LANE) & (lane < KPE_LANE + MLA_ROPE)
        kpe = jnp.where(rope_lanes, zl_ref[...], 0.0)
        cv, sv = c_ref[...], s_ref[...]
        dkpe = jnp.zeros((bt, LANES), F32)
        for h in range(HEADS):
            sl = slice(h * LANES, (h + 1) * LANES)
            for is_k, x, g_ref, dout, dg_ref in ((False, q_ref[:, sl], gq_ref, dqn_ref[:, sl], dgq_ref),
                                                  (True, kv_ref[:, sl] + kpe, gk_ref, dkn_ref[:, sl], dgk_ref)):
                r = lax.rsqrt(jnp.sum(x * x, axis=-1, keepdims=True) / MLA_QK + EPS)
                xhat = x * r
                dxn = dout * cv + _rope_partner(dout * sv, lane)
                dg_ref[...] += jnp.sum(dxn * xhat, axis=0, keepdims=True)
                dxhat = dxn * g_ref[...]
                dx = r * (dxhat - xhat * (jnp.sum(dxhat * xhat, axis=-1, keepdims=True) / MLA_QK))
                if is_k:
                    dkv_ref[:, sl] = jnp.where(lane < KPE_LANE, dx, 0.0).astype(dkv_ref.dtype)
                    dkpe = dkpe + jnp.where(rope_lanes, dx, 0.0)
                else:
                    dq_ref[:, sl] = dx.astype(dq_ref.dtype)
        dkv_ref[:, qw:qw + vw] = dv_ref[...].astype(dkv_ref.dtype)
        dkpe_ref[...] = dkpe

    def row(w):
        return pl.BlockSpec((bt, w), lambda i: (i, 0))

    vec = pl.BlockSpec((1, LANES), lambda i: (0, 0))
    return pl.pallas_call(
        body, name="attn_prep_b_bwd", grid=(t // bt,),
        in_specs=[row(qw), row(qw + vw), pl.BlockSpec((bt, LANES), lambda i: (i, ZA_LAST // LANES)),
                  row(LANES), row(LANES), vec, vec, row(qw), row(qw), row(vw)],
        out_specs=[row(qw), row(qw + vw), row(LANES), vec, vec],
        out_shape=[jax.ShapeDtypeStruct((t, qw), MXU_DTYPE), jax.ShapeDtypeStruct((t, qw + vw), MXU_DTYPE),
                   jax.ShapeDtypeStruct((t, LANES), F32), jax.ShapeDtypeStruct((1, LANES), F32),
                   jax.ShapeDtypeStruct((1, LANES), F32)],
        compiler_params=_params("arbitrary"),
    )(q_raw, kv_raw, z, c_tab, s_tab, g_q, g_k, dqn, dkn, dv)


NT_DIMS = (((1,), (1,)), ((), ()))
TN_DIMS = (((0,), (0,)), ((), ()))


def _head_qk(q_ref, k_ref, e, mla, lo):
    if mla:
        return q_ref[:, e * LANES:(e + 1) * LANES], k_ref[:, e * LANES:(e + 1) * LANES]
    q = q_ref[...]
    return jnp.where(lo if e == 0 else jnp.logical_not(lo), q, jnp.zeros_like(q)), k_ref[...]


def _attn_specs(mla, blk, q_map, k_map):
    w = 2 * LANES if mla else LANES
    q_spec = pl.BlockSpec((blk, w), lambda j, a, b: (q_map(a, b), j))
    k_spec = pl.BlockSpec((blk, w), lambda j, a, b: (k_map(a, b), j))
    qv_spec = pl.BlockSpec((blk, LANES), lambda j, a, b: (q_map(a, b), j))
    kv_spec = pl.BlockSpec((blk, LANES), lambda j, a, b: (k_map(a, b), j))
    fq_spec = pl.BlockSpec((blk, LANES), lambda j, a, b: (q_map(a, b), 0))
    fk_spec = pl.BlockSpec((8, blk), lambda j, a, b: (0, k_map(a, b)))
    return q_spec, k_spec, qv_spec, kv_spec, fq_spec, fk_spec


def _flash_fwd(q, k, v, f, f_t, *, mla, scale, name):
    t = q.shape[0]
    blk = min(ATTN_BLOCK, t)
    nb = t // blk
    pairs = HEADS // 2
    q_spec, k_spec, qv_spec, kv_spec, fq_spec, fk_spec = _attn_specs(
        mla, blk, lambda i, kk: i, lambda i, kk: jnp.minimum(kk, i))

    def body(*refs):
        if mla:
            q_ref, k_ref, v_ref, o_ref, lse_ref, m_s, l_s, acc_s = refs
            fq_ref = fk_ref = None
        else:
            q_ref, k_ref, v_ref, fq_ref, fk_ref, o_ref, lse_ref, m_s, l_s, acc_s = refs
        j, i, kk = pl.program_id(0), pl.program_id(1), pl.program_id(2)
        lo = _iota((1, LANES), 1) < HEAD_V

        @pl.when(kk == 0)
        def _():
            m_s[...] = jnp.full_like(m_s, NEG)
            l_s[...] = jnp.zeros_like(l_s)
            acc_s[...] = jnp.zeros_like(acc_s)

        def step(mask):
            vv = v_ref[...]
            for e in range(2):
                s, _, _ = _scores(q_ref, k_ref, fq_ref, fk_ref, e, j, mla, scale, lo, mask)
                m_prev = m_s[e]
                m_new = jnp.maximum(m_prev, jnp.max(s, axis=1, keepdims=True))
                alpha = jnp.exp(m_prev - m_new)
                p = jnp.exp(s - m_new)
                if mask is not None:
                    p = jnp.where(mask, p, 0.0)
                l_s[e] = alpha * l_s[e] + jnp.sum(p, axis=1, keepdims=True)
                acc_s[e] = alpha * acc_s[e] + jnp.dot(p.astype(MXU_DTYPE), vv, preferred_element_type=F32)
                m_s[e] = m_new

        _masked_and_plain(kk <= i, i, kk, blk, step)

        @pl.when(kk == nb - 1)
        def _():
            valid = (i * blk + _iota((blk, 1), 0)) >= PAD
            outs, lses = [], []
            for e in range(2):
                l = l_s[e]
                outs.append(acc_s[e] * jnp.where(l > 0.0, 1.0 / jnp.where(l > 0.0, l, 1.0), 0.0))
                lses.append(m_s[e] + jnp.log(jnp.where(l > 0.0, l, 1.0)))
            o = jnp.where(lo, outs[0], outs[1])
            o_ref[...] = jnp.where(valid, o, 0.0).astype(o_ref.dtype)
            lane = _iota((1, LANES), 1)
            lse_ref[...] = jnp.where(lane == 0, lses[0], jnp.where(lane == 1, lses[1], 0.0))

    in_specs = [q_spec, k_spec, kv_spec] + ([] if mla else [fq_spec, fk_spec])
    args = (q, k, v) + (() if mla else (f, f_t))
    hv = HEADS * HEAD_V
    return pl.pallas_call(
        body, name=name, grid=(pairs, nb, nb),
        in_specs=in_specs, out_specs=[qv_spec, qv_spec],
        out_shape=[jax.ShapeDtypeStruct((t, hv), F32), jax.ShapeDtypeStruct((t, hv), F32)],
        scratch_shapes=[pltpu.VMEM((2, blk, 1), F32), pltpu.VMEM((2, blk, 1), F32), pltpu.VMEM((2, blk, LANES), F32)],
        compiler_params=_params("parallel", "parallel", "arbitrary"),
    )(*args)


def _bwd_tile(q_ref, k_ref, v_ref, o_ref, do_ref, lse_ref, fq_ref, fk_ref, e, pair, mla, scale, lo, mask):
    s, qe, ke = _scores(q_ref, k_ref, fq_ref, fk_ref, e, pair, mla, scale, lo, mask)
    p = jnp.exp(s - lse_ref[:, e:e + 1])
    if mask is not None:
        p = jnp.where(mask, p, 0.0)
    do = do_ref[...]
    doe = jnp.where(lo if e == 0 else jnp.logical_not(lo), do, jnp.zeros_like(do))
    dp = lax.dot_general(doe, v_ref[...], NT_DIMS, preferred_element_type=F32)
    delta = jnp.sum(doe.astype(F32) * o_ref[...].astype(F32), axis=1, keepdims=True)
    return p, p * (dp - delta), qe, ke


def _flash_bwd_dq(q, k, v, o, do, lse, f, f_t, *, mla, scale, col0, name):
    t = q.shape[0]
    blk = min(ATTN_BLOCK, t)
    nb = t // blk
    pairs = HEADS // 2
    w = 2 * LANES if mla else LANES
    q_spec, k_spec, qv_spec, kv_spec, fq_spec, fk_spec = _attn_specs(
        mla, blk, lambda i, kk: i, lambda i, kk: jnp.minimum(kk, i))
    od_spec = pl.BlockSpec((blk, LANES), lambda j, i, kk: (i, col0 + j))

    def body(*refs):
        if mla:
            q_ref, k_ref, v_ref, o_ref, do_ref, lse_ref, dq_ref, dq_s = refs
            fq_ref = fk_ref = rs_ref = rs_s = None
        else:
            q_ref, k_ref, v_ref, o_ref, do_ref, lse_ref, fq_ref, fk_ref, dq_ref, rs_ref, dq_s, rs_s = refs
        j, i, kk = pl.program_id(0), pl.program_id(1), pl.program_id(2)
        lo = _iota((1, LANES), 1) < HEAD_V

        @pl.when(kk == 0)
        def _():
            dq_s[...] = jnp.zeros_like(dq_s)
            if not mla:
                rs_s[...] = jnp.zeros_like(rs_s)

        def step(mask):
            for e in range(2):
                _, ds, _, ke = _bwd_tile(q_ref, k_ref, v_ref, o_ref, do_ref, lse_ref, fq_ref, fk_ref, e, j, mla, scale,
                                         lo, mask)
                dq_s[e] += jnp.dot(ds.astype(MXU_DTYPE), ke, preferred_element_type=F32)
                if not mla:
                    rs_s[e] += jnp.sum(ds, axis=1, keepdims=True)

        _masked_and_plain(kk <= i, i, kk, blk, step)

        @pl.when(kk == nb - 1)
        def _():
            if mla:
                dq_ref[:, 0:LANES] = dq_s[0] * scale
                dq_ref[:, LANES:2 * LANES] = dq_s[1] * scale
            else:
                dq_ref[...] = jnp.where(lo, dq_s[0], dq_s[1]) * scale
                lane = _iota((1, LANES), 1)
                rs_ref[...] = jnp.where(lane == 0, rs_s[0], jnp.where(lane == 1, rs_s[1], 0.0))

    in_specs = [q_spec, k_spec, kv_spec, od_spec, od_spec, qv_spec] + ([] if mla else [fq_spec, fk_spec])
    args = (q, k, v, o, do, lse) + (() if mla else (f, f_t))
    out_specs = [q_spec] + ([] if mla else [qv_spec])
    out_shape = [jax.ShapeDtypeStruct((t, pairs * w), F32)]
    scratch = [pltpu.VMEM((2, blk, LANES), F32)]
    if not mla:
        out_shape.append(jax.ShapeDtypeStruct((t, pairs * LANES), F32))
        scratch.append(pltpu.VMEM((2, blk, 1), F32))
    outs = pl.pallas_call(
        body, name=name, grid=(pairs, nb, nb),
        in_specs=in_specs, out_specs=out_specs, out_shape=out_shape, scratch_shapes=scratch,
        compiler_params=_params("parallel", "parallel", "arbitrary"),
    )(*args)
    return outs[0] if mla else outs


def _flash_bwd_dkv(q, k, v, o, do, lse, f, f_t, *, mla, scale, col0, name):
    t = q.shape[0]
    blk = min(ATTN_BLOCK, t)
    nb = t // blk
    pairs = HEADS // 2
    w = 2 * LANES if mla else LANES
    q_spec, k_spec, qv_spec, kv_spec, fq_spec, fk_spec = _attn_specs(
        mla, blk, lambda a, b: jnp.maximum(a, b), lambda a, b: a)
    od_spec = pl.BlockSpec((blk, LANES), lambda j, a, b: (jnp.maximum(a, b), col0 + j))
    cs_spec = pl.BlockSpec((8, blk), lambda j, a, b: (j, a))

    def body(*refs):
        if mla:
            q_ref, k_ref, v_ref, o_ref, do_ref, lse_ref, dk_ref, dv_ref, dk_s, dv_s = refs
            fq_ref = fk_ref = cs_ref = cs_s = None
        else:
            q_ref, k_ref, v_ref, o_ref, do_ref, lse_ref, fq_ref, fk_ref, dk_ref, dv_ref, cs_ref, dk_s, dv_s, cs_s = refs
        j, kb, qb = pl.program_id(0), pl.program_id(1), pl.program_id(2)
        lo = _iota((1, LANES), 1) < HEAD_V

        @pl.when(qb == 0)
        def _():
            dk_s[...] = jnp.zeros_like(dk_s)
            dv_s[...] = jnp.zeros_like(dv_s)
            if not mla:
                cs_s[...] = jnp.zeros_like(cs_s)

        def step(mask):
            do = do_ref[...]
            for e in range(2):
                p, ds, _, _ = _bwd_tile(q_ref, k_ref, v_ref, o_ref, do_ref, lse_ref, fq_ref, fk_ref, e, j, mla, scale,
                                        lo, mask)
                dv_s[e] += lax.dot_general(p.astype(MXU_DTYPE), do, TN_DIMS, preferred_element_type=F32)
                q_src = q_ref[:, e * LANES:(e + 1) * LANES] if mla else q_ref[...]
                dk_s[e] += lax.dot_general(ds.astype(MXU_DTYPE), q_src, TN_DIMS, preferred_element_type=F32)
                if not mla:
                    cs_s[e] += jnp.sum(ds, axis=0, keepdims=True)

        _masked_and_plain(qb >= kb, qb, kb, blk, step)

        @pl.when(qb == nb - 1)
        def _():
            dv_ref[...] = jnp.where(lo, dv_s[0], dv_s[1])
            if mla:
                dk_ref[:, 0:LANES] = dk_s[0] * scale
                dk_ref[:, LANES:2 * LANES] = dk_s[1] * scale
            else:
                dk_ref[...] = jnp.where(lo, dk_s[0], dk_s[1]) * scale
                sub = _iota((8, 1), 0)
                cs_ref[...] = jnp.where(sub == 0, cs_s[0], jnp.where(sub == 1, cs_s[1], 0.0))

    in_specs = [q_spec, k_spec, kv_spec, od_spec, od_spec, qv_spec] + ([] if mla else [fq_spec, fk_spec])
    args = (q, k, v, o, do, lse) + (() if mla else (f, f_t))
    out_specs = [k_spec, kv_spec] + ([] if mla else [cs_spec])
    out_shape = [jax.ShapeDtypeStruct((t, pairs * w), F32), jax.ShapeDtypeStruct((t, HEADS * HEAD_V), F32)]
    scratch = [pltpu.VMEM((2, blk, LANES), F32), pltpu.VMEM((2, blk, LANES), F32)]
    if not mla:
        out_shape.append(jax.ShapeDtypeStruct((pairs * 8, t), F32))
        scratch.append(pltpu.VMEM((2, 1, blk), F32))
    return pl.pallas_call(
        body, name=name, grid=(pairs, nb, nb),
        in_specs=in_specs, out_specs=out_specs, out_shape=out_shape, scratch_shapes=scratch,
        compiler_params=_params("parallel", "parallel", "arbitrary"),
    )(*args)


ATTN_CHUNK = 640
LOG2E = 1.4426950408889634
LN2 = 0.6931471805599453


def _for_chunks(n, ch, body):
    for c in range(n):
        body(c * ch)


def _select_lane(x, idx):
    return jnp.sum(jnp.where(_iota(x.shape, 1) == idx, x, 0.0), axis=1, keepdims=True)


def _select_row(x, idx):
    return jnp.sum(jnp.where(_iota(x.shape, 0) == idx, x, 0.0), axis=0, keepdims=True)


def _flash_fwd_chunked(q, k, v, f, f_t, *, mla, name):
    t = q.shape[0]
    blk = min(ATTN_BLOCK, t)
    nb = t // blk
    pairs = HEADS // 2
    ch = min(ATTN_CHUNK, blk)
    assert blk % ch == 0
    q_spec, k_spec, qv_spec, kv_spec, fq_spec, fk_spec = _attn_specs(
        mla, blk, lambda i, kk: i, lambda i, kk: jnp.minimum(kk, i))
    lse_spec = pl.BlockSpec((8, blk), lambda j, i, kk: (j, i))

    def body(*refs):
        if mla:
            q_ref, k_ref, v_ref, o_ref, lse_ref, m_s, l_s, a_s, acc_s, s_s, p_s = refs
            fq_ref = fk_ref = None
        else:
            q_ref, k_ref, v_ref, fq_ref, fk_ref, o_ref, lse_ref, m_s, l_s, a_s, acc_s, s_s, p_s = refs
        j, i, kk = pl.program_id(0), pl.program_id(1), pl.program_id(2)
        lo = _iota((1, LANES), 1) < HEAD_V

        @pl.when(kk == 0)
        def _():
            m_s[...] = jnp.full_like(m_s, NEG)
            l_s[...] = jnp.zeros_like(l_s)
            acc_s[...] = jnp.zeros_like(acc_s)

        def step(masked):
            vv = v_ref[...]
            for e in range(2):
                qe, ke = _head_qk(q_ref, k_ref, e, mla, lo)
                s_s[...] = lax.dot_general(qe, ke, NT_DIMS, preferred_element_type=F32)
                fkr = None if mla else _select_row(fk_ref[...], 2 * j + e)

                def chunk(r0, e=e, fkr=fkr):
                    rows = pl.ds(r0, ch)
                    s = s_s[rows, :]
                    if not mla:
                        s = s + _select_lane(fq_ref[rows, :], 2 * j + e) - fkr
                    if masked:
                        rpos = i * blk + r0 + _iota((ch, blk), 0)
                        cpos = kk * blk + _iota((ch, blk), 1)
                        mask = (cpos <= rpos) & (cpos >= PAD)
                        s = jnp.where(mask, s, NEG)
                    m_prev = m_s[e, rows, :]
                    m_new = jnp.maximum(m_prev, jnp.max(s, axis=1, keepdims=True))
                    alpha = jnp.exp2(m_prev - m_new)
                    p = jnp.exp2(s - m_new)
                    if masked:
                        p = jnp.where(mask, p, 0.0)
                    l_s[e, rows, :] = alpha * l_s[e, rows, :] + jnp.sum(p, axis=1, keepdims=True)
                    m_s[e, rows, :] = m_new
                    a_s[rows, :] = alpha
                    p_s[rows, :] = p.astype(p_s.dtype)

                _for_chunks(blk // ch, ch, chunk)
                acc_s[e] = a_s[...] * acc_s[e] + jnp.dot(p_s[...], vv, preferred_element_type=F32)

        needs_mask = (kk == i) | (kk == 0)

        @pl.when((kk <= i) & needs_mask)
        def _():
            step(True)

        @pl.when((kk <= i) & jnp.logical_not(needs_mask))
        def _():
            step(False)

        @pl.when(kk == nb - 1)
        def _():
            valid = (i * blk + _iota((blk, 1), 0)) >= PAD
            outs, lses = [], []
            for e in range(2):
                l = l_s[e]
                outs.append(acc_s[e] * jnp.where(l > 0.0, 1.0 / jnp.where(l > 0.0, l, 1.0), 0.0))
                lses.append(m_s[e] + jnp.log(jnp.where(l > 0.0, l, 1.0)) * LOG2E)
            o = jnp.where(lo, outs[0], outs[1])
            o_ref[...] = jnp.where(valid, o, 0.0).astype(o_ref.dtype)
            lane = _iota((1, LANES), 1)
            lse_cols = jnp.where(lane == 0, lses[0], jnp.where(lane == 1, lses[1], 0.0))
            lse_ref[...] = lse_cols.T[0:8, :]

    in_specs = [q_spec, k_spec, kv_spec] + ([] if mla else [fq_spec, fk_spec])
    args = (q, k, v) + (() if mla else (f, f_t))
    hv = HEADS * HEAD_V
    return pl.pallas_call(
        body, name=name, grid=(pairs, nb, nb),
        in_specs=in_specs, out_specs=[qv_spec, lse_spec],
        out_shape=[jax.ShapeDtypeStruct((t, hv), F32), jax.ShapeDtypeStruct((pairs * 8, t), F32)],
        scratch_shapes=[pltpu.VMEM((2, blk, 1), F32), pltpu.VMEM((2, blk, 1), F32), pltpu.VMEM((blk, 1), F32),
                        pltpu.VMEM((2, blk, LANES), F32), pltpu.VMEM((blk, blk), F32), pltpu.VMEM((blk, blk), MXU_DTYPE)],
        compiler_params=_params("parallel", "parallel", "arbitrary"),
    )(*args)


def _delta_rows(do, o):
    t, width = o.shape
    bt = min(ATTN_BLOCK, t)
    n_heads = width // HEAD_V

    def body(do_ref, o_ref, d_ref):
        prod = do_ref[...].astype(F32) * o_ref[...]
        col = _iota((width, LANES), 0)
        first = _iota((width, LANES), 1) * HEAD_V
        sel = jnp.where((col >= first) & (col < first + HEAD_V), 1.0, 0.0).astype(F32)
        per_head = jnp.dot(prod, sel, precision=lax.Precision.HIGHEST, preferred_element_type=F32)
        d_ref[...] = per_head.T[0:n_heads, :]

    return pl.pallas_call(
        body, name="attn_delta", grid=(t // bt,),
        in_specs=[pl.BlockSpec((bt, width), lambda i: (i, 0)), pl.BlockSpec((bt, width), lambda i: (i, 0))],
        out_specs=pl.BlockSpec((n_heads, bt), lambda i: (0, i)),
        out_shape=jax.ShapeDtypeStruct((n_heads, t), F32),
        compiler_params=_params("parallel"),
    )(do, o)


def _flash_bwd_fused(q, k, v, do, lse_t, delta_t, f, f_t, *, mla, col0, name):
    t = q.shape[0]
    blk = min(ATTN_BLOCK, t)
    nb = t // blk
    pairs = HEADS // 2
    ch = min(ATTN_CHUNK, blk)
    assert blk % ch == 0
    w = 2 * LANES if mla else LANES
    last = nb - 1
    qmap = lambda a, b: jnp.maximum(a, b)
    q_spec = pl.BlockSpec((blk, w), lambda j, a, b: (qmap(a, b), j))
    k_spec = pl.BlockSpec((blk, w), lambda j, a, b: (a, j))
    v_spec = pl.BlockSpec((blk, LANES), lambda j, a, b: (a, j))
    do_spec = pl.BlockSpec((blk, LANES), lambda j, a, b: (qmap(a, b), col0 + j))
    lse_spec = pl.BlockSpec((8, blk), lambda j, a, b: (j, qmap(a, b)))
    delta_spec = pl.BlockSpec((8, blk), lambda j, a, b: (col0 // (HEADS // 2), qmap(a, b)))
    fq_spec = pl.BlockSpec((8, blk), lambda j, a, b: (0, qmap(a, b)))
    fk_spec = pl.BlockSpec((blk, LANES), lambda j, a, b: (a, 0))
    dq_spec = pl.BlockSpec((blk, w), lambda j, a, b: (jnp.where(a == last, b, 0), j))
    rs_spec = pl.BlockSpec((8, blk), lambda j, a, b: (j, jnp.where(a == last, b, 0)))
    cs_spec = pl.BlockSpec((blk, LANES), lambda j, a, b: (a, j))

    def body(*refs):
        if mla:
            (q_ref, k_ref, v_ref, do_ref, lse_ref, delta_ref, dq_ref, dk_ref, dv_ref,
             dq_s, dk_s, dv_s, st_s, dpt_s, pt_s, dst_s) = refs
            fq_ref = fk_ref = rs_ref = cs_ref = rs_s = cs_s = None
        else:
            (q_ref, k_ref, v_ref, do_ref, lse_ref, delta_ref, fq_ref, fk_ref, dq_ref, dk_ref, dv_ref, rs_ref, cs_ref,
             dq_s, dk_s, dv_s, st_s, dpt_s, pt_s, dst_s, rs_s, cs_s) = refs
        j, kb, qb = pl.program_id(0), pl.program_id(1), pl.program_id(2)
        lo = _iota((1, LANES), 1) < HEAD_V

        @pl.when((kb == 0) & (qb == 0))
        def _():
            dq_s[...] = jnp.zeros_like(dq_s)
            if not mla:
                rs_s[...] = jnp.zeros_like(rs_s)

        @pl.when(qb == 0)
        def _():
            dk_s[...] = jnp.zeros_like(dk_s)
            dv_s[...] = jnp.zeros_like(dv_s)
            if not mla:
                cs_s[...] = jnp.zeros_like(cs_s)

        def step(masked):
            do = do_ref[...]
            vv = v_ref[...]
            for e in range(2):
                half = lo if e == 0 else jnp.logical_not(lo)
                qe, ke = _head_qk(q_ref, k_ref, e, mla, lo)
                doe = jnp.where(half, do, jnp.zeros_like(do))
                st_s[...] = lax.dot_general(ke, qe, NT_DIMS, preferred_element_type=F32)
                dpt_s[...] = lax.dot_general(vv, doe, NT_DIMS, preferred_element_type=F32)
                head = 2 * j + e
                lse_row = _select_row(lse_ref[...], e)
                delta_row = _select_row(delta_ref[...], head)
                fq_row = None if mla else _select_row(fq_ref[...], head)

                def chunk(r0, e=e, lse_row=lse_row, delta_row=delta_row, fq_row=fq_row, head=head):
                    rows = pl.ds(r0, ch)
                    s = st_s[rows, :]
                    if not mla:
                        s = s + fq_row - _select_lane(fk_ref[rows, :], head)
                    p = jnp.exp2(s - lse_row)
                    if masked:
                        kpos = kb * blk + r0 + _iota((ch, blk), 0)
                        qpos = qb * blk + _iota((ch, blk), 1)
                        p = jnp.where((kpos <= qpos) & (kpos >= PAD), p, 0.0)
                    ds = p * (dpt_s[rows, :] - delta_row)
                    pt_s[rows, :] = p.astype(pt_s.dtype)
                    dst_s[rows, :] = ds.astype(dst_s.dtype)
                    if not mla:
                        cs_s[e, rows, :] += jnp.sum(ds, axis=1, keepdims=True)
                        rs_s[qb, e] += jnp.sum(ds, axis=0, keepdims=True)

                _for_chunks(blk // ch, ch, chunk)
                dv_s[e] += jnp.dot(pt_s[...], do, preferred_element_type=F32)
                q_src = qe if mla else q_ref[...]
                dk_s[e] += jnp.dot(dst_s[...], q_src, preferred_element_type=F32)
                dq_s[qb, e] += lax.dot_general(dst_s[...], ke, TN_DIMS, preferred_element_type=F32)

        needs_mask = (qb == kb) | (kb == 0)

        @pl.when((qb >= kb) & needs_mask)
        def _():
            step(True)

        @pl.when((qb >= kb) & jnp.logical_not(needs_mask))
        def _():
            step(False)

        @pl.when(qb == last)
        def _():
            dv_ref[...] = jnp.where(lo, dv_s[0], dv_s[1])
            if mla:
                dk_ref[:, 0:LANES] = dk_s[0] * LN2
                dk_ref[:, LANES:2 * LANES] = dk_s[1] * LN2
            else:
                dk_ref[...] = jnp.where(lo, dk_s[0], dk_s[1]) * LN2
                lane = _iota((1, LANES), 1)
                cs_ref[...] = jnp.where(lane == 0, cs_s[0], jnp.where(lane == 1, cs_s[1], 0.0))

        @pl.when(kb == last)
        def _():
            if mla:
                dq_ref[:, 0:LANES] = dq_s[qb, 0] * LN2
                dq_ref[:, LANES:2 * LANES] = dq_s[qb, 1] * LN2
            else:
                dq_ref[...] = jnp.where(lo, dq_s[qb, 0], dq_s[qb, 1]) * LN2
                sub = _iota((8, 1), 0)
                rs_ref[...] = jnp.where(sub == 0, rs_s[qb, 0], jnp.where(sub == 1, rs_s[qb, 1], 0.0))

    in_specs = [q_spec, k_spec, v_spec, do_spec, lse_spec, delta_spec] + ([] if mla else [fq_spec, fk_spec])
    args = (q, k, v, do, lse_t, delta_t) + (() if mla else (f_t, f))
    hv = HEADS * HEAD_V
    out_specs = [dq_spec, k_spec, v_spec]
    out_shape = [jax.ShapeDtypeStruct((t, pairs * w), F32), jax.ShapeDtypeStruct((t, pairs * w), F32),
                 jax.ShapeDtypeStruct((t, hv), F32)]
    scratch = [pltpu.VMEM((nb, 2, blk, LANES), F32), pltpu.VMEM((2, blk, LANES), F32), pltpu.VMEM((2, blk, LANES), F32),
               pltpu.VMEM((blk, blk), F32), pltpu.VMEM((blk, blk), F32), pltpu.VMEM((blk, blk), MXU_DTYPE),
               pltpu.VMEM((blk, blk), MXU_DTYPE)]
    if not mla:
        out_specs += [rs_spec, cs_spec]
        out_shape += [jax.ShapeDtypeStruct((pairs * 8, t), F32), jax.ShapeDtypeStruct((t, hv), F32)]
        scratch += [pltpu.VMEM((nb, 2, 1, blk), F32), pltpu.VMEM((2, blk, 1), F32)]
    return pl.pallas_call(
        body, name=name, grid=(pairs, nb, nb),
        in_specs=in_specs, out_specs=out_specs, out_shape=out_shape, scratch_shapes=scratch,
        compiler_params=_params("parallel", "arbitrary", "arbitrary"),
    )(*args)


def _shift_down(x, halo, n):
    rows = x.shape[0]
    r = _iota((rows, 1), 0)
    out = pltpu.roll(x, n, axis=0)
    for s in range(n):
        out = jnp.where(r == s, halo[8 - n + s:8 - n + s + 1, :], out)
    return out


def _shift_up(x, halo, n):
    rows = x.shape[0]
    r = _iota((rows, 1), 0)
    out = pltpu.roll(x, rows - n, axis=0)
    for s in range(n):
        out = jnp.where(r == rows - n + s, halo[s:s + 1, :], out)
    return out


def _conv_specs(bt, nblk):
    d = D_MODEL
    per8 = bt // 8
    z_spec = pl.BlockSpec((bt, 3 * d), lambda i: (i, 0))
    prev_spec = pl.BlockSpec((8, 3 * d), lambda i: (jnp.maximum(i * per8 - 1, 0), 0))
    next_z = pl.BlockSpec((8, 3 * d), lambda i: (jnp.minimum((i + 1) * per8, nblk * per8 - 1), 0))
    next_d = pl.BlockSpec((8, d), lambda i: (jnp.minimum((i + 1) * per8, nblk * per8 - 1), 0))
    w_spec = pl.BlockSpec((8, d), lambda i: (0, 0))
    row_spec = pl.BlockSpec((bt, d), lambda i: (i, 0))
    return z_spec, prev_spec, next_z, next_d, w_spec, row_spec


def _conv_taps(z_ref, prev_ref, i):
    d = D_MODEL
    g = z_ref[:, d:2 * d] * z_ref[:, 2 * d:3 * d]
    gh = jnp.where(i > 0, prev_ref[:, d:2 * d] * prev_ref[:, 2 * d:3 * d], 0.0)
    return g, _shift_down(g, gh, 1), _shift_down(g, gh, 2)


def _conv_fwd(z, conv_w8):
    t = z.shape[0]
    bt = min(PREP_TILE, t)
    nblk = t // bt
    d = D_MODEL
    z_spec, prev_spec, _, _, w_spec, row_spec = _conv_specs(bt, nblk)

    def body(z_ref, prev_ref, w_ref, v_ref):
        g, g1, g2 = _conv_taps(z_ref, prev_ref, pl.program_id(0))
        y = w_ref[0:1, :] * g2 + w_ref[1:2, :] * g1 + w_ref[2:3, :] * g
        v_ref[...] = (z_ref[:, 0:d] * y).astype(v_ref.dtype)

    return pl.pallas_call(
        body, name="conv_fwd", grid=(nblk,),
        in_specs=[z_spec, prev_spec, w_spec], out_specs=row_spec,
        out_shape=jax.ShapeDtypeStruct((t, d), MXU_DTYPE),
        compiler_params=_params("parallel"),
    )(z, z, conv_w8)


def _conv_bwd(z, conv_w8, dv):
    t = z.shape[0]
    bt = min(PREP_TILE, t)
    nblk = t // bt
    d = D_MODEL
    z_spec, prev_spec, next_z, next_d, w_spec, row_spec = _conv_specs(bt, nblk)

    def body(z_ref, prev_ref, nz_ref, dv_ref, ndv_ref, w_ref, dz_ref, dw_ref):
        i = pl.program_id(0)

        @pl.when(i == 0)
        def _():
            dw_ref[...] = jnp.zeros_like(dw_ref)

        g, g1, g2 = _conv_taps(z_ref, prev_ref, i)
        w0, w1, w2 = w_ref[0:1, :], w_ref[1:2, :], w_ref[2:3, :]
        y = w0 * g2 + w1 * g1 + w2 * g
        dvv = dv_ref[...].astype(F32)
        gate_b = z_ref[:, 0:d]
        dy = dvv * gate_b
        dyn = jnp.where(i < nblk - 1, ndv_ref[...].astype(F32) * nz_ref[:, 0:d], 0.0)
        dg = w2 * dy + w1 * _shift_up(dy, dyn, 1) + w0 * _shift_up(dy, dyn, 2)
        dz_ref[:, 0:d] = (dvv * y).astype(dz_ref.dtype)
        dz_ref[:, d:2 * d] = (dg * z_ref[:, 2 * d:3 * d]).astype(dz_ref.dtype)
        dz_ref[:, 2 * d:3 * d] = (dg * z_ref[:, d:2 * d]).astype(dz_ref.dtype)
        sub = _iota((8, 1), 0)
        s0 = jnp.sum(dy * g2, axis=0, keepdims=True)
        s1 = jnp.sum(dy * g1, axis=0, keepdims=True)
        s2 = jnp.sum(dy * g, axis=0, keepdims=True)
        dw_ref[...] += jnp.where(sub == 0, s0, jnp.where(sub == 1, s1, jnp.where(sub == 2, s2, 0.0)))

    return pl.pallas_call(
        body, name="conv_bwd", grid=(nblk,),
        in_specs=[z_spec, prev_spec, next_z, row_spec, next_d, w_spec], out_specs=[z_spec, w_spec],
        out_shape=[jax.ShapeDtypeStruct((t, 3 * d), MXU_DTYPE), jax.ShapeDtypeStruct((8, d), F32)],
        compiler_params=_params("arbitrary"),
    )(z, z, z, dv, dv, conv_w8)


def _loss_head(h, target):
    t, d = h.shape
    bt = LOSS_TILE
    assert LANES % bt == 0 or bt == LANES
    off = LANES // bt

    def body(h_ref, y_ref, dh_ref, acc_ref):
        i = pl.program_id(0)

        @pl.when(i == 0)
        def _():
            acc_ref[...] = jnp.zeros_like(acc_ref)

        @pl.when(i < off)
        def _():
            dh_ref[...] = jnp.zeros_like(dh_ref)

        @pl.when(i >= off)
        def _():
            err = h_ref[...] - y_ref[...]
            dh_ref[...] = err / d
            acc_ref[...] += jnp.sum(err * err)

    dh, acc = pl.pallas_call(
        body, name="loss_head", grid=(t // bt,),
        in_specs=[pl.BlockSpec((bt, d), lambda i: (i, 0)), pl.BlockSpec((bt, d), lambda i: (jnp.maximum(i - off, 0), 0))],
        out_specs=[pl.BlockSpec((bt, d), lambda i: (i, 0)), pl.BlockSpec((8, LANES), lambda i: (0, 0))],
        out_shape=[jax.ShapeDtypeStruct((t, d), F32), jax.ShapeDtypeStruct((8, LANES), F32)],
        compiler_params=_params("arbitrary"),
    )(h, target)
    return dh, acc[0, 0] * (0.5 / d)


def _common_tile(rows, row_off, cap=512, align=8):
    for b in range(min(cap, rows) // align * align, 0, -align):
        if rows % b == 0 and row_off % b == 0:
            return b
    raise ValueError((rows, row_off))


def _round_up(n, m):
    return -(-n // m) * m


def _adamw(w, m, v, g_buf, row_off, col_off):
    rows, width = w.shape
    wpad = _round_up(width, LANES)
    assert col_off % wpad == 0
    bt = _common_tile(rows, row_off)

    def body(w_ref, m_ref, v_ref, g_ref, go_ref, d_ref, nm_ref, nv_ref):
        gv = g_ref[...]
        if wpad != width:
            gv = gv[:, :width]
        m_new = ADAM_B1 * m_ref[...] + (1.0 - ADAM_B1) * gv
        v_new = ADAM_B2 * v_ref[...] + (1.0 - ADAM_B2) * jnp.square(gv)
        m_hat = m_new / (1.0 - ADAM_B1 ** ADAM_STEP)
        v_hat = v_new / (1.0 - ADAM_B2 ** ADAM_STEP)
        go_ref[...] = gv
        d_ref[...] = -ADAM_LR * (m_hat / (jnp.sqrt(v_hat) + ADAM_EPS) + ADAM_WD * w_ref[...])
        nm_ref[...] = m_new
        nv_ref[...] = v_new

    spec = pl.BlockSpec((bt, width), lambda i: (i, 0))
    g_spec = pl.BlockSpec((bt, wpad), lambda i: (row_off // bt + i, col_off // wpad))
    return pl.pallas_call(
        body, name="adamw", grid=(rows // bt,),
        in_specs=[spec] * 3 + [g_spec], out_specs=[spec] * 4,
        out_shape=[jax.ShapeDtypeStruct((rows, width), F32)] * 4,
        compiler_params=_params("parallel"),
    )(w, m, v, g_buf)


def _add2(a, b, *, out_dtype, name):
    rows, width = a.shape
    bt = next(x for x in range(min(rows, 640), 0, -16) if rows % x == 0)

    def body(a_ref, b_ref, o_ref):
        o_ref[...] = (a_ref[...] + b_ref[...]).astype(o_ref.dtype)

    spec = pl.BlockSpec((bt, width), lambda i: (i, 0))
    return pl.pallas_call(
        body, name=name, grid=(rows // bt,), in_specs=[spec, spec], out_specs=spec,
        out_shape=jax.ShapeDtypeStruct((rows, width), out_dtype), compiler_params=_params("parallel"),
    )(a, b)


def _sum4(parts, slot, *, name):
    _, rows, width = parts.shape
    bt = next(x for x in range(min(rows, 640), 0, -16) if rows % x == 0)

    def body(slot_ref, p_ref, o_ref):
        p = [p_ref[n].astype(F32) for n in range(4)]
        o_ref[...] = ((p[0] + p[1]) + p[2]) + p[3]

    grid_spec = pltpu.PrefetchScalarGridSpec(
        num_scalar_prefetch=1, grid=(rows // bt,),
        in_specs=[pl.BlockSpec((4, bt, width), lambda i, s: (0, i, 0))],
        out_specs=pl.BlockSpec((None, bt, width), lambda i, s: (s[0], i, 0)))
    return pl.pallas_call(
        body, name=name, grid_spec=grid_spec,
        out_shape=jax.ShapeDtypeStruct((2, rows, width), F32), compiler_params=_params("parallel"),
    )(jnp.reshape(slot, (1,)).astype(jnp.int32), parts)


ANY = pl.BlockSpec(memory_space=pl.ANY)
CHIP_FLIPS = ((1, 0), (0, 1), (1, 1))


def _place():
    return lax.axis_index("x"), lax.axis_index("y"), lax.axis_index("c")


def _flip(v, f):
    return 1 - v if f else v


def _allgather_chips(slabs):
    _, rows, width = slabs.shape
    half = rows // 2

    def body(_, out_ref, send_sems, recv_sems):
        x, y, c = _place()
        me = 2 * x + y
        sibling = (x, y, 1 - c)
        my_rows = pl.ds(pl.multiple_of(c * half, 8), half)
        sib_rows = pl.ds(pl.multiple_of((1 - c) * half, 8), half)
        first, passed = [], []
        for n, (fx, fy) in enumerate(CHIP_FLIPS):
            px, py = _flip(x, fx), _flip(y, fy)
            peer = 2 * px + py
            first.append(pltpu.make_async_remote_copy(
                src_ref=out_ref.at[me, my_rows], dst_ref=out_ref.at[me, my_rows],
                send_sem=send_sems.at[n], recv_sem=recv_sems.at[n], device_id=(px, py, c), device_id_type=MESH))
            passed.append(pltpu.make_async_remote_copy(
                src_ref=out_ref.at[peer, my_rows], dst_ref=out_ref.at[peer, my_rows],
                send_sem=send_sems.at[3 + n], recv_sem=recv_sems.at[3 + n], device_id=sibling, device_id_type=MESH))
        for cp in first:
            cp.start()
        for n, (fx, fy) in enumerate(CHIP_FLIPS):
            peer = 2 * _flip(x, fx) + _flip(y, fy)
            pltpu.make_async_remote_copy(
                src_ref=out_ref.at[me, my_rows], dst_ref=out_ref.at[peer, my_rows],
                send_sem=send_sems.at[n], recv_sem=recv_sems.at[n], device_id=sibling, device_id_type=MESH).wait_recv()
            passed[n].start()
        for n, (fx, fy) in enumerate(CHIP_FLIPS):
            peer = 2 * _flip(x, fx) + _flip(y, fy)
            pltpu.make_async_remote_copy(
                src_ref=out_ref.at[me, sib_rows], dst_ref=out_ref.at[peer, sib_rows],
                send_sem=send_sems.at[3 + n], recv_sem=recv_sems.at[3 + n], device_id=sibling,
                device_id_type=MESH).wait_recv()
        for cp in first + passed:
            cp.wait_send()

    return pl.pallas_call(
        body, name="allgather_weights",
        in_specs=[ANY], out_specs=ANY,
        out_shape=jax.ShapeDtypeStruct(slabs.shape, slabs.dtype), input_output_aliases={0: 0},
        scratch_shapes=[pltpu.SemaphoreType.DMA((6,)), pltpu.SemaphoreType.DMA((6,))],
    )(slabs)


def _swap_halves(g):
    _, rows, width = g.shape
    half = rows // 2

    def body(g_ref, got_ref, send_sem, recv_sem):
        x, y, c = _place()
        away = pl.ds(pl.multiple_of((1 - c) * half, 8), half)
        cp = pltpu.make_async_remote_copy(
            src_ref=g_ref.at[:, away], dst_ref=got_ref, send_sem=send_sem, recv_sem=recv_sem,
            device_id=(x, y, 1 - c), device_id_type=MESH)
        cp.start()
        cp.wait()

    return pl.pallas_call(
        body, name="grad_swap_halves",
        in_specs=[ANY], out_specs=ANY,
        out_shape=jax.ShapeDtypeStruct((4, half, width), g.dtype),
        scratch_shapes=[pltpu.SemaphoreType.DMA, pltpu.SemaphoreType.DMA],
    )(g)


def _scatter_chips(s):
    _, rows, width = s.shape

    def body(s_ref, out_ref, send_sems, recv_sems, local_sem):
        x, y, c = _place()
        me = 2 * x + y
        mine = pltpu.make_async_copy(s_ref.at[me], out_ref.at[me], local_sem)
        mine.start()
        copies = []
        for n, (fx, fy) in enumerate(CHIP_FLIPS):
            px, py = _flip(x, fx), _flip(y, fy)
            copies.append(pltpu.make_async_remote_copy(
                src_ref=s_ref.at[2 * px + py], dst_ref=out_ref.at[me],
                send_sem=send_sems.at[n], recv_sem=recv_sems.at[n], device_id=(px, py, c), device_id_type=MESH))
        for cp in copies:
            cp.start()
        for n, (fx, fy) in enumerate(CHIP_FLIPS):
            peer = 2 * _flip(x, fx) + _flip(y, fy)
            pltpu.make_async_remote_copy(
                src_ref=s_ref.at[me], dst_ref=out_ref.at[peer],
                send_sem=send_sems.at[n], recv_sem=recv_sems.at[n], device_id=(x, y, c), device_id_type=MESH).wait_recv()
        for cp in copies:
            cp.wait_send()
        mine.wait()

    return pl.pallas_call(
        body, name="grad_scatter_chips",
        in_specs=[ANY], out_specs=ANY,
        out_shape=jax.ShapeDtypeStruct((4, rows, width), s.dtype),
        scratch_shapes=[pltpu.SemaphoreType.DMA((3,)), pltpu.SemaphoreType.DMA((3,)), pltpu.SemaphoreType.DMA],
    )(s)


def _join_halves(halves):
    def body(_, out_ref, send_sem, recv_sem):
        x, y, c = _place()
        cp = pltpu.make_async_remote_copy(
            src_ref=out_ref.at[c], dst_ref=out_ref.at[c], send_sem=send_sem, recv_sem=recv_sem,
            device_id=(x, y, 1 - c), device_id_type=MESH)
        cp.start()
        pltpu.make_async_remote_copy(
            src_ref=out_ref.at[c], dst_ref=out_ref.at[1 - c], send_sem=send_sem, recv_sem=recv_sem,
            device_id=(x, y, 1 - c), device_id_type=MESH).wait_recv()
        cp.wait_send()

    return pl.pallas_call(
        body, name="grad_join_halves",
        in_specs=[ANY], out_specs=ANY,
        out_shape=jax.ShapeDtypeStruct(halves.shape, halves.dtype), input_output_aliases={0: 0},
        scratch_shapes=[pltpu.SemaphoreType.DMA, pltpu.SemaphoreType.DMA],
    )(halves)


PACK_W = 1024
REPLICATED = ("g_mix", "g_mlp", "g_cq", "g_ckv", "g_q_mla", "g_k_mla", "g_q_fox", "g_k_fox", "b_forget")
WEIGHT_ORDER = ("meta_tokens", "g_mix", "g_mlp", "w_in_attn", "g_cq", "w_uq", "g_ckv", "w_ukv", "g_q_mla", "g_k_mla",
                "g_q_fox", "g_k_fox", "b_forget", "w_out_attn", "w_in_conv", "conv_w", "w_out_conv", "w_mlp_up",
                "w_mlp_down")
N_EVEN = 2
N_ODD = 2
SHARD_IN = ATTN_IN // 4
SHARD_MIX = D_MODEL // 4
SHARD_UQ = HEADS * MLA_QK // 4
SHARD_UKV = HEADS * (MLA_NOPE + HEAD_V) // 4
SHARD_CONV = 3 * D_MODEL // 4
SIDE_W = 256
PK_UP = (0, 0)
PK_DOWN = (4096, 0)
PK_CONV_IN = (8192, 0)
PK_ATTN_IN = (10240, 0)
PK_OUT_ATTN = (12288, 0)
PK_OUT_CONV = (12800, 0)
PK_SMALL = (8192, 768)
PK_UQ = (10240, 768)
PK_UKV = (11008, 768)
PK_ROWS = 13312
SMALL_ROWS = 64
SMALL_META = 0
SMALL_CONV = 16
SMALL_REP = 24
SMALL_BITS_ROWS = 48
MATRIX_PLACES = (("w_mlp_up", PK_UP), ("w_mlp_down", PK_DOWN), ("w_in_conv", PK_CONV_IN), ("w_in_attn", PK_ATTN_IN),
                 ("w_out_attn", PK_OUT_ATTN), ("w_out_conv", PK_OUT_CONV), ("w_uq", PK_UQ), ("w_ukv", PK_UKV))


def _put(buf, x, place, *, name, slab=None):
    row_off, col_off = place
    slabs = x.ndim == 3
    rows, w = x.shape[-2:]
    wpad = _round_up(w, LANES)
    assert col_off % wpad == 0
    bt = _common_tile(rows, row_off, align=16)

    def fill(x_ref, o_ref):
        v = x_ref[...].astype(o_ref.dtype)
        if wpad != w:
            v = jnp.concatenate([v, jnp.zeros((bt, wpad - w), o_ref.dtype)], axis=1)
        o_ref[...] = v

    def body(x_ref, _, o_ref):
        fill(x_ref, o_ref)

    if slab is not None:
        grid_spec = pltpu.PrefetchScalarGridSpec(
            num_scalar_prefetch=1, grid=(rows // bt,),
            in_specs=[pl.BlockSpec((bt, w), lambda i, s: (i, 0)), ANY],
            out_specs=pl.BlockSpec((None, bt, wpad), lambda i, s: (s[0], row_off // bt + i, col_off // wpad)))
        return pl.pallas_call(
            lambda s_ref, x_ref, _, o_ref: fill(x_ref, o_ref), name=name, grid_spec=grid_spec,
            out_shape=jax.ShapeDtypeStruct(buf.shape, buf.dtype), input_output_aliases={2: 0},
            compiler_params=_params("parallel"),
        )(jnp.reshape(slab, (1,)).astype(jnp.int32), x, buf)
    if slabs:
        grid = (4, rows // bt)
        x_spec = pl.BlockSpec((None, bt, w), lambda s, i: (s, i, 0))
        o_spec = pl.BlockSpec((None, bt, wpad), lambda s, i: (s, row_off // bt + i, col_off // wpad))
        sem = ("parallel", "parallel")
    else:
        grid = (rows // bt,)
        x_spec = pl.BlockSpec((bt, w), lambda i: (i, 0))
        o_spec = pl.BlockSpec((bt, wpad), lambda i: (row_off // bt + i, col_off // wpad))
        sem = ("parallel",)
    return pl.pallas_call(
        body, name=name, grid=grid, in_specs=[x_spec, ANY], out_specs=o_spec,
        out_shape=jax.ShapeDtypeStruct(buf.shape, buf.dtype), input_output_aliases={1: 0},
        compiler_params=_params(*sem),
    )(x, buf)


def _w_cols(place, layer, rows, width):
    base = (place[0] + layer * rows) // rows
    return dict(n=4 * width, tn=width, tk=rows, spec=pl.BlockSpec((None, rows, width), lambda i, j, k: (j, base, 0)))


def _w_cols_t(place, layer, rows, width):
    base = (place[0] + layer * rows) // rows
    return dict(n=rows, tn=rows, tk=width, spec=pl.BlockSpec((None, rows, width), lambda i, j, k: (k, base, 0)))


def _w_rows(place, layer, rows):
    base = (place[0] + layer * rows) // rows
    return dict(n=D_MODEL, tn=D_MODEL, tk=rows, spec=pl.BlockSpec((None, rows, D_MODEL), lambda i, j, k: (k, base, 0)))


def _w_rows_t(place, layer, rows):
    base = (place[0] + layer * rows) // rows
    return dict(n=4 * rows, tn=rows, tk=D_MODEL, spec=pl.BlockSpec((None, rows, D_MODEL), lambda i, j, k: (j, base, 0)))


def _g_cols(g, place, layer, rows, width):
    base = (place[0] + layer * rows) // rows
    return g, pl.BlockSpec((None, rows, width), lambda i, j, k: (j, base, 0))


def _g_rows(g, place, layer, rows):
    base = (place[0] + layer * rows) // rows
    return g, pl.BlockSpec((None, rows, D_MODEL), lambda i, j, k: (i, base, 0))


IN_PADW = _round_up(SHARD_IN, LANES)
IN_TAIL = ZA_FQ - SHARD_IN
IN_FL = SHARD_IN - HEADS
ZA_KPE = ZA_LAST + KPE_LANE


def _assemble_attn_in(gathered, layer):
    bt = 256
    base = (PK_ATTN_IN[0] + layer * D_MODEL) // bt
    assert 2 * SHARD_IN > ZA_FQ + MLA_ROPE and 3 * SHARD_IN < ATTN_IN - HEADS

    def body(s0, s1, s2, s3, o_ref):
        dt = o_ref.dtype
        z = lambda n: jnp.zeros((bt, n), dt)
        o_ref[...] = jnp.concatenate(
            [s0[:, :SHARD_IN], s1[:, :IN_TAIL], s1[:, IN_TAIL + MLA_ROPE:SHARD_IN], s2[:, :SHARD_IN], s3[:, :IN_FL],
             s3[:, IN_FL:SHARD_IN], z(KPE_LANE - HEADS), s1[:, IN_TAIL:IN_TAIL + MLA_ROPE],
             z(LANES - KPE_LANE - MLA_ROPE)], axis=1).astype(dt)

    def spec(s):
        return pl.BlockSpec((None, bt, IN_PADW), lambda i: (s, base + i, 0))

    return pl.pallas_call(
        body, name="assemble_attn_in", grid=(D_MODEL // bt,),
        in_specs=[spec(s) for s in range(4)], out_specs=pl.BlockSpec((bt, ZA_W), lambda i: (i, 0)),
        out_shape=jax.ShapeDtypeStruct((D_MODEL, ZA_W), MXU_DTYPE), compiler_params=_params("parallel"),
    )(gathered, gathered, gathered, gathered)


def _scatter_attn_in(g, dwa, layer):
    bt = 256
    base = (PK_ATTN_IN[0] + layer * D_MODEL) // bt
    fq1 = ZA_FQ + SHARD_IN - IN_TAIL - MLA_ROPE

    def body(d_ref, _, o_ref):
        pad = jnp.zeros((bt, IN_PADW - SHARD_IN), F32)
        pieces = (
            (d_ref[:, 0:SHARD_IN],),
            (d_ref[:, SHARD_IN:ZA_FQ], d_ref[:, ZA_KPE:ZA_KPE + MLA_ROPE], d_ref[:, ZA_FQ:fq1]),
            (d_ref[:, fq1:fq1 + SHARD_IN],),
            (d_ref[:, fq1 + SHARD_IN:ZA_LAST], d_ref[:, ZA_LAST:ZA_LAST + HEADS]),
        )
        for s in range(4):
            @pl.when(pl.program_id(0) == s)
            def _(s=s):
                o_ref[...] = jnp.concatenate(list(pieces[s]) + [pad], axis=1)

    return pl.pallas_call(
        body, name="scatter_attn_in", grid=(4, D_MODEL // bt),
        in_specs=[pl.BlockSpec((bt, ZA_W), lambda s, i: (i, 0)), ANY],
        out_specs=pl.BlockSpec((None, bt, IN_PADW), lambda s, i: (s, base + i, 0)),
        out_shape=jax.ShapeDtypeStruct(g.shape, g.dtype), input_output_aliases={1: 0},
        compiler_params=_params("parallel", "parallel"),
    )(dwa, g)


def _assemble_uq(gathered, layer):
    bt = 128
    base = (PK_UQ[0] + layer * Q_LORA) // bt
    col = PK_UQ[1] // SIDE_W

    def body(s0, s1, s2, s3, o_ref):
        dt = o_ref.dtype
        z = jnp.zeros((bt, LANES - MLA_QK), dt)
        parts = []
        for s_ref in (s0, s1, s2, s3):
            parts += [s_ref[:, 0:MLA_QK], z, s_ref[:, MLA_QK:2 * MLA_QK], z]
        o_ref[...] = jnp.concatenate(parts, axis=1).astype(dt)

    def spec(s):
        return pl.BlockSpec((None, bt, SIDE_W), lambda i: (s, base + i, col))

    return pl.pallas_call(
        body, name="assemble_uq", grid=(Q_LORA // bt,),
        in_specs=[spec(s) for s in range(4)], out_specs=pl.BlockSpec((bt, HEADS * LANES), lambda i: (i, 0)),
        out_shape=jax.ShapeDtypeStruct((Q_LORA, HEADS * LANES), MXU_DTYPE), compiler_params=_params("parallel"),
    )(gathered, gathered, gathered, gathered)


def _scatter_uq(g, dw, layer):
    bt = 128
    base = (PK_UQ[0] + layer * Q_LORA) // bt
    col = PK_UQ[1] // SIDE_W

    def body(d_ref, _, o_ref):
        o_ref[...] = jnp.concatenate([d_ref[:, 0:MLA_QK], d_ref[:, LANES:LANES + MLA_QK],
                                      jnp.zeros((bt, SIDE_W - 2 * MLA_QK), F32)], axis=1)

    return pl.pallas_call(
        body, name="scatter_uq", grid=(4, Q_LORA // bt),
        in_specs=[pl.BlockSpec((bt, 2 * LANES), lambda s, i: (i, s)), ANY],
        out_specs=pl.BlockSpec((None, bt, SIDE_W), lambda s, i: (s, base + i, col)),
        out_shape=jax.ShapeDtypeStruct(g.shape, g.dtype), input_output_aliases={1: 0},
        compiler_params=_params("parallel", "parallel"),
    )(dw, g)


def _assemble_ukv(gathered, layer):
    bt = KV_LORA
    base = (PK_UKV[0] + layer * KV_LORA) // bt
    col = PK_UKV[1] // SIDE_W
    hd = MLA_NOPE + HEAD_V

    def body(s0, s1, s2, s3, o_ref):
        dt = o_ref.dtype
        z = jnp.zeros((bt, LANES - MLA_NOPE), dt)
        keys, vals = [], []
        for s_ref in (s0, s1, s2, s3):
            for e in range(2):
                keys += [s_ref[:, e * hd:e * hd + MLA_NOPE], z]
                vals.append(s_ref[:, e * hd + MLA_NOPE:(e + 1) * hd])
        o_ref[...] = jnp.concatenate(keys + vals, axis=1).astype(dt)

    def spec(s):
        return pl.BlockSpec((None, bt, SIDE_W), lambda i: (s, base + i, col))

    return pl.pallas_call(
        body, name="assemble_ukv", grid=(1,),
        in_specs=[spec(s) for s in range(4)],
        out_specs=pl.BlockSpec((bt, HEADS * (LANES + HEAD_V)), lambda i: (i, 0)),
        out_shape=jax.ShapeDtypeStruct((KV_LORA, HEADS * (LANES + HEAD_V)), MXU_DTYPE), compiler_params=_params("parallel"),
    )(gathered, gathered, gathered, gathered)


def _scatter_ukv(g, dw, layer):
    bt = KV_LORA
    base = (PK_UKV[0] + layer * KV_LORA) // bt
    col = PK_UKV[1] // SIDE_W

    def body(k_ref, v_ref, _, o_ref):
        o_ref[...] = jnp.concatenate([k_ref[:, 0:MLA_NOPE], v_ref[:, 0:HEAD_V], k_ref[:, LANES:LANES + MLA_NOPE],
                                      v_ref[:, HEAD_V:2 * HEAD_V]], axis=1)

    return pl.pallas_call(
        body, name="scatter_ukv", grid=(4,),
        in_specs=[pl.BlockSpec((bt, 2 * LANES), lambda s: (0, s)),
                  pl.BlockSpec((bt, 2 * HEAD_V), lambda s: (0, HEADS * LANES // (2 * HEAD_V) + s)), ANY],
        out_specs=pl.BlockSpec((None, bt, SIDE_W), lambda s: (s, base, col)),
        out_shape=jax.ShapeDtypeStruct(g.shape, g.dtype), input_output_aliases={2: 0},
        compiler_params=_params("parallel"),
    )(dw, dw, g)


def _pad_lanes(v, n=LANES):
    return jnp.pad(v, (0, n - v.shape[0])).reshape(1, n)


def _relu2_up(acc):
    r = jnp.maximum(acc, 0.0)
    return acc, r * r


def _relu2_bwd(acc, u):
    return (acc * (2.0 * jnp.maximum(u, 0.0)),)


def _add_res(acc, res):
    return (acc + res,)


def _local_step(x, target, meta, small, gathered):
    seq = x.shape[0]
    t = seq + LANES
    d = D_MODEL
    h = jnp.concatenate([jnp.zeros((PAD, d), F32), meta.astype(F32), x], axis=0)
    c_tab, s_tab = _rope_tables(t)
    scale_mla, scale_fox = MLA_QK ** -0.5 * LOG2E, FOX_DIM ** -0.5 * LOG2E
    grads = {}
    saved = []
    g = jnp.zeros((4, PK_ROWS, PACK_W), F32)

    for layer in range(DEPTH):
        j = layer // 2
        sv = {"h_in": h}
        hn = _rmsnorm_fwd(h, small["g_mix"][layer])
        sv["hn"] = hn
        if layer % 2 == 0:
            w_in = _assemble_attn_in(gathered, j)
            w_uq = _assemble_uq(gathered, j)
            w_ukv = _assemble_ukv(gathered, j)
            out_place = PK_OUT_ATTN
            vecs = dict(
                g_cq=small["g_cq"][j].reshape(1, Q_LORA), g_ckv=small["g_ckv"][j].reshape(1, KV_LORA),
                g_qf=jnp.tile(small["g_q_fox"][j] * scale_fox, 2).reshape(1, LANES),
                g_kf=jnp.tile(small["g_k_fox"][j], 2).reshape(1, LANES),
                b_f=_pad_lanes(small["b_forget"][j]), g_q=_pad_lanes(small["g_q_mla"][j] * scale_mla),
                g_k=_pad_lanes(small["g_k_mla"][j]))
            z = _matmul(hn, w_in, name="mm_attn_in")
            cqn, ckvn, qf, kf, vf, logf = _prep_a_fwd(z, vecs["g_cq"], vecs["g_ckv"], vecs["g_qf"], vecs["g_kf"], vecs["b_f"])
            f_cum, f_cum_t = _cumsum_rows(logf, reverse=False, name="cumsum_fwd", out_scale=LOG2E)
            q_raw = _matmul(cqn, w_uq, name="mm_uq")
            kv_raw = _matmul(ckvn, w_ukv, name="mm_ukv")
            qn, kn, v_mla = _prep_b_fwd(q_raw, kv_raw, z, c_tab, s_tab, vecs["g_q"], vecs["g_k"])
            o_mla, lse_mla = _flash_fwd_chunked(qn, kn, v_mla, None, None, mla=True, name="flash_fwd_mla")
            o_fox, lse_fox = _flash_fwd_chunked(qf, kf, vf, f_cum, f_cum_t, mla=False, name="flash_fwd_fox")
            o = jnp.concatenate([o_mla, o_fox], axis=1)
            h = _matmul(o, gathered, b_tiles=_w_rows(out_place, j, SHARD_MIX), extras=(h,), epilogue=_add_res,
                        name="mm_mix_out")
            sv.update(w_in=w_in, w_uq=w_uq, w_ukv=w_ukv, out_place=out_place, vecs=vecs, z=z, cqn=cqn, ckvn=ckvn, qf=qf, kf=kf,
                      vf=vf, f_cum=f_cum, f_cum_t=f_cum_t, q_raw=q_raw, kv_raw=kv_raw, qn=qn, kn=kn, v_mla=v_mla, o=o,
                      lse_mla=lse_mla, lse_fox=lse_fox)
        else:
            out_place = PK_OUT_CONV
            conv_w8 = jnp.pad(small["conv_w"][j], ((0, 5), (0, 0)))
            z = _matmul(hn, gathered, b_tiles=_w_cols(PK_CONV_IN, j, d, SHARD_CONV), name="mm_conv_in")
            vmix = _conv_fwd(z, conv_w8)
            h = _matmul(vmix, gathered, b_tiles=_w_rows(out_place, j, SHARD_MIX), extras=(h,), epilogue=_add_res,
                        name="mm_mix_out")
            sv.update(out_place=out_place, conv_w8=conv_w8, z=z, vmix=vmix)
        sv["h_mid"] = h
        hn2 = _rmsnorm_fwd(h, small["g_mlp"][layer])
        u, a = _matmul(hn2, gathered, b_tiles=_w_cols(PK_UP, layer, d, d), epilogue=_relu2_up,
                       out_dtypes=(F32, MXU_DTYPE), name="mm_mlp_up")
        h = _matmul(a, gathered, b_tiles=_w_rows(PK_DOWN, layer, d), extras=(h,), epilogue=_add_res, name="mm_mlp_down")
        sv.update(hn2=hn2, u=u, a=a)
        saved.append(sv)

    dh, loss_local = _loss_head(h, target)

    dg_mix, dg_mlp = [None] * DEPTH, [None] * DEPTH
    per_even = {k: [None, None] for k in ("g_cq", "g_ckv", "g_q_mla", "g_k_mla", "g_q_fox", "g_k_fox", "b_forget")}
    per_odd = {"conv_w": [None, None]}
    for layer in reversed(range(DEPTH)):
        j = layer // 2
        sv = saved[layer]
        du = _matmul(dh, gathered, tb=True, b_tiles=_w_rows_t(PK_DOWN, layer, d), extras=(sv["u"],),
                     epilogue=_relu2_bwd, out_dtypes=(MXU_DTYPE,), name="mm_mlp_da")
        g = _matmul(sv["a"], dh, ta=True, out_into=_g_rows(g, PK_DOWN, layer, d), name="mm_dw_down")
        g = _matmul(sv["hn2"], du, ta=True, out_into=_g_cols(g, PK_UP, layer, d, d), name="mm_dw_up")
        dhn2 = _matmul(du, gathered, tb=True, b_tiles=_w_cols_t(PK_UP, layer, d, d), name="mm_mlp_dhn")
        dh, dg_mlp[layer] = _rmsnorm_bwd(sv["h_mid"], small["g_mlp"][layer], dhn2, dh)
        do = _matmul(dh, gathered, tb=True, b_tiles=_w_rows_t(sv["out_place"], j, SHARD_MIX), out_dtypes=(MXU_DTYPE,),
                     name="mm_mix_do")
        if layer % 2 == 0:
            vecs = sv["vecs"]
            g = _matmul(sv["o"], dh, ta=True, tm=SHARD_MIX, out_into=_g_rows(g, PK_OUT_ATTN, j, SHARD_MIX),
                        name="mm_dw_out")
            delta_t = _delta_rows(do, sv["o"])
            dqn, dkn, dv_mla = _flash_bwd_fused(sv["qn"], sv["kn"], sv["v_mla"], do, sv["lse_mla"], delta_t, None, None,
                                                mla=True, col0=0, name="flash_bwd_mla")
            dqf, dkf, dvf, rs_t, cs = _flash_bwd_fused(sv["qf"], sv["kf"], sv["vf"], do, sv["lse_fox"], delta_t,
                                                       sv["f_cum"], sv["f_cum_t"], mla=False, col0=HEADS // 2,
                                                       name="flash_bwd_fox")
            d_f = rs_t.reshape(HEADS // 2, 8, t)[:, :2, :].reshape(HEADS, t).T
            d_f = d_f - cs.reshape(t, HEADS // 2, LANES)[:, :, :2].reshape(t, HEADS)
            d_f = jnp.pad(d_f, ((0, 0), (0, LANES - HEADS)))
            dlogf, _ = _cumsum_rows(d_f, reverse=True, name="cumsum_bwd")
            dq_raw, dkv_raw, dkpe, dg_q, dg_k = _prep_b_bwd(sv["q_raw"], sv["kv_raw"], sv["z"], c_tab, s_tab, vecs["g_q"],
                                                            vecs["g_k"], dqn, dkn, dv_mla)
            g = _scatter_uq(g, _matmul(sv["cqn"], dq_raw, ta=True, name="mm_dw_uq"), j)
            g = _scatter_ukv(g, _matmul(sv["ckvn"], dkv_raw, ta=True, name="mm_dw_ukv"), j)
            dcqn = _matmul(dq_raw, sv["w_uq"], tb=True, name="mm_dcqn")
            dckvn = _matmul(dkv_raw, sv["w_ukv"], tb=True, name="mm_dckvn")
            dz, dg_cq, dg_ckv, dg_qf, dg_kf, db_f = _prep_a_bwd(
                sv["z"], vecs["g_cq"], vecs["g_ckv"], vecs["g_qf"], vecs["g_kf"], vecs["b_f"], dcqn, dckvn, dqf, dkf, dvf,
                dlogf, dkpe)
            g = _scatter_attn_in(g, _matmul(sv["hn"], dz, ta=True, name="mm_dw_attn_in"), j)
            per_even["g_cq"][j] = dg_cq[0]
            per_even["g_ckv"][j] = dg_ckv[0]
            per_even["g_q_mla"][j] = dg_q[0, :MLA_QK] * scale_mla
            per_even["g_k_mla"][j] = dg_k[0, :MLA_QK]
            per_even["g_q_fox"][j] = (dg_qf[0, :FOX_DIM] + dg_qf[0, FOX_DIM:]) * scale_fox
            per_even["g_k_fox"][j] = dg_kf[0, :FOX_DIM] + dg_kf[0, FOX_DIM:]
            per_even["b_forget"][j] = db_f[0, :HEADS]
            dhn = _matmul(dz, sv["w_in"], tb=True, name="mm_attn_dhn")
        else:
            g = _matmul(sv["vmix"], dh, ta=True, tm=SHARD_MIX, out_into=_g_rows(g, PK_OUT_CONV, j, SHARD_MIX),
                        name="mm_dw_out")
            dz, dcw = _conv_bwd(sv["z"], sv["conv_w8"], do)
            per_odd["conv_w"][j] = dcw[:3]
            g = _matmul(sv["hn"], dz, ta=True, tn=SHARD_CONV, out_into=_g_cols(g, PK_CONV_IN, j, d, SHARD_CONV),
                        name="mm_dw_conv_in")
            dhn = _matmul(dz, gathered, tb=True, b_tiles=_w_cols_t(PK_CONV_IN, j, d, SHARD_CONV), name="mm_conv_dhn")
        dh, dg_mix[layer] = _rmsnorm_bwd(sv["h_in"], small["g_mix"][layer], dhn, dh)

    grads["meta_tokens"] = dh[PAD:LANES]
    grads["g_mix"] = jnp.stack(dg_mix)
    grads["g_mlp"] = jnp.stack(dg_mlp)
    for k, v in list(per_even.items()) + list(per_odd.items()):
        grads[k] = jnp.stack(v)
    return loss_local, dh[LANES:], g, grads


def kernel(x, meta_tokens, g_mix, g_mlp, w_in_attn, g_cq, w_uq, g_ckv, w_ukv, g_q_mla, g_k_mla, g_q_fox, g_k_fox, b_forget, w_out_attn, w_in_conv, conv_w, w_out_conv, w_mlp_up, w_mlp_down, loss_target, m_meta_tokens, m_g_mix, m_g_mlp, m_w_in_attn, m_g_cq, m_w_uq, m_g_ckv, m_w_ukv, m_g_q_mla, m_g_k_mla, m_g_q_fox, m_g_k_fox, m_b_forget, m_w_out_attn, m_w_in_conv, m_conv_w, m_w_out_conv, m_w_mlp_up, m_w_mlp_down, v_meta_tokens, v_g_mix, v_g_mlp, v_w_in_attn, v_g_cq, v_w_uq, v_g_ckv, v_w_ukv, v_g_q_mla, v_g_k_mla, v_g_q_fox, v_g_k_fox, v_b_forget, v_w_out_attn, v_w_in_conv, v_conv_w, v_w_out_conv, v_w_mlp_up, v_w_mlp_down):
    args = dict(locals())
    weights = {n: args[n] for n in WEIGHT_ORDER}
    mom_m = {n: args["m_" + n] for n in WEIGHT_ORDER}
    mom_v = {n: args["v_" + n] for n in WEIGHT_ORDER}

    wire = jnp.bfloat16
    me = 2 * lax.axis_index("x") + lax.axis_index("y")
    buf = jnp.zeros((4, PK_ROWS, PACK_W), wire)
    for name, place in MATRIX_PLACES:
        w = weights[name]
        buf = _put(buf, w.reshape(-1, w.shape[-1]), place, name="pack_weights", slab=me)
    meta_bits = lax.bitcast_convert_type(meta_tokens, wire).reshape(2 * N_META, SIDE_W)
    conv_bits = lax.bitcast_convert_type(conv_w, wire).reshape(2 * N_ODD * 3, SIDE_W)
    bits = jnp.concatenate([meta_bits, conv_bits, jnp.zeros((SMALL_BITS_ROWS - 2 * N_META - 2 * N_ODD * 3, SIDE_W), wire)])
    buf = _put(buf, bits, PK_SMALL, name="pack_weights", slab=me)
    gathered = _allgather_chips(buf)
    got_bits = gathered[:, PK_SMALL[0]:PK_SMALL[0] + SMALL_BITS_ROWS, PK_SMALL[1]:PK_SMALL[1] + SIDE_W]
    meta_full = lax.bitcast_convert_type(got_bits[:, :2 * N_META].reshape(4, N_META, SIDE_W, 2), F32)
    meta_full = meta_full.transpose(1, 0, 2).reshape(N_META, D_MODEL)
    conv_full = lax.bitcast_convert_type(
        got_bits[:, 2 * N_META:2 * N_META + 2 * N_ODD * 3].reshape(4, N_ODD, 3, SIDE_W, 2), F32)
    small = {n: weights[n] for n in REPLICATED}
    small["conv_w"] = conv_full.transpose(1, 2, 0, 3).reshape(N_ODD, 3, D_MODEL)

    loss_local, grad_x, g, grads = _local_step(x[0], loss_target[0], meta_full, small, gathered)
    loss = lax.psum(loss_local, MESH_AXES)

    rep = jnp.concatenate([grads[n].reshape(-1) for n in REPLICATED])
    rep = jnp.pad(rep, (0, (SMALL_ROWS - SMALL_REP) * SIDE_W - rep.shape[0])).reshape(SMALL_ROWS - SMALL_REP, SIDE_W)
    g_meta = grads["meta_tokens"].reshape(N_META, 4, SIDE_W).transpose(1, 0, 2)
    g_conv = grads["conv_w"].reshape(N_ODD * 3, 4, SIDE_W).transpose(1, 0, 2)
    small4 = jnp.concatenate([g_meta, g_conv, jnp.zeros((4, SMALL_REP - SMALL_CONV - N_ODD * 3, SIDE_W), F32),
                              jnp.broadcast_to(rep[None], (4,) + rep.shape)], axis=1)
    g = _put(g, small4, PK_SMALL, name="pack_small_grads")
    half = PK_ROWS // 2
    c = lax.axis_index("c")
    got = _swap_halves(g)
    kept = lax.dynamic_slice_in_dim(g, c * half, half, axis=1)
    pair = _add2(kept.reshape(4 * half, PACK_W), got.reshape(4 * half, PACK_W), out_dtype=jnp.bfloat16,
                 name="grad_pair_sum").reshape(4, half, PACK_W)
    total = _sum4(_scatter_chips(pair), c, name="grad_chip_sum")
    g_tot = _join_halves(total).reshape(PK_ROWS, PACK_W)

    out = {}
    for name, place in MATRIX_PLACES:
        shape = weights[name].shape
        two_d = lambda a: a.reshape(-1, shape[-1])
        res = _adamw(two_d(weights[name]), two_d(mom_m[name]), two_d(mom_v[name]), g_tot, place[0], place[1])
        out[name] = [r.reshape(shape) for r in res]

    def small_pack(src):
        flat = jnp.concatenate([src[n].reshape(-1) for n in REPLICATED])
        flat = jnp.pad(flat, (0, (SMALL_ROWS - SMALL_REP) * SIDE_W - flat.shape[0])).reshape(SMALL_ROWS - SMALL_REP, SIDE_W)
        return jnp.concatenate([src["meta_tokens"], src["conv_w"].reshape(N_ODD * 3, SIDE_W),
                                jnp.zeros((SMALL_REP - SMALL_CONV - N_ODD * 3, SIDE_W), F32), flat])

    res = _adamw(small_pack(weights), small_pack(mom_m), small_pack(mom_v), g_tot, PK_SMALL[0], PK_SMALL[1])
    for name in ("meta_tokens", "conv_w") + REPLICATED:
        out[name] = []
    for r in res:
        out["meta_tokens"].append(r[SMALL_META:SMALL_META + N_META])
        out["conv_w"].append(r[SMALL_CONV:SMALL_CONV + N_ODD * 3].reshape(N_ODD, 3, SIDE_W))
        flat, off = r[SMALL_REP:].reshape(-1), 0
        for name in REPLICATED:
            n = weights[name].size
            out[name].append(flat[off:off + n].reshape(weights[name].shape))
            off += n
    return (loss, grad_x[None], *[out[n][0] for n in WEIGHT_ORDER], *[out[n][1] for n in WEIGHT_ORDER],
            *[out[n][2] for n in WEIGHT_ORDER], *[out[n][3] for n in WEIGHT_ORDER])
```

```python
import functools

import jax
import jax.numpy as jnp
import numpy as np
from jax import lax
from jax.experimental import pallas as pl
from jax.experimental.pallas import tpu as pltpu

F32 = jnp.float32
MXU_DTYPE = jnp.bfloat16

D_MODEL = 1024
N_META = 16
LANES = 128
PAD = LANES - N_META
HEADS = 8
Q_LORA = 384
KV_LORA = 256
MLA_NOPE = 64
MLA_ROPE = 32
MLA_QK = MLA_NOPE + MLA_ROPE
HEAD_V = 64
FOX_DIM = 64
ROPE_BASE = 10000.0
D_FF = 4 * D_MODEL
DEPTH = 4
EPS = 1e-6
NEG = -1e30
ATTN_IN = Q_LORA + KV_LORA + MLA_ROPE + 3 * HEADS * FOX_DIM + HEADS

ZA_CQ = 0
ZA_CKV = Q_LORA
ZA_FQ = ZA_CKV + KV_LORA
ZA_FK = ZA_FQ + HEADS * FOX_DIM
ZA_FV = ZA_FK + HEADS * FOX_DIM
ZA_LAST = ZA_FV + HEADS * FOX_DIM
ZA_W = ZA_LAST + LANES
KPE_LANE = MLA_NOPE

ADAM_LR = 0.001
ADAM_B1 = 0.9
ADAM_B2 = 0.999
ADAM_EPS = 1e-08
ADAM_WD = 0.01
ADAM_STEP = 10

VMEM_LIMIT_BYTES = 52 * 1024 * 1024
ROW_TILE = 1040
PREP_TILE = 320
ATTN_BLOCK = 640
LOSS_TILE = 128
MAX_TILE = 1536

MESH_AXES = ("x", "y", "c")
MESH = pl.DeviceIdType.MESH


def _params(*sem):
    return pltpu.CompilerParams(dimension_semantics=sem, vmem_limit_bytes=VMEM_LIMIT_BYTES)


def _tile(n, cap=None):
    cap = MAX_TILE if cap is None else cap
    if n <= cap:
        return n
    best = None
    for t in range(LANES, cap + 1, LANES):
        if n % t == 0:
            best = t
    assert best is not None, n
    return best


def _iota(shape, dim):
    return lax.broadcasted_iota(jnp.int32, shape, dim)


def _matmul(a, b, *, ta=False, tb=False, extras=(), epilogue=None, out_dtypes=(F32,), name, b_tiles=None,
            out_into=None, tm=None, tn=None):
    if ta:
        kdim, m = a.shape
    else:
        m, kdim = a.shape
    row_tile = min(ROW_TILE, m)
    if ta:
        tm_auto, tk = _tile(m), min(ATTN_BLOCK, kdim)
    else:
        tm_auto, tk = (row_tile if m % row_tile == 0 else _tile(m)), _tile(kdim)
    tm = tm_auto if tm is None else tm
    if b_tiles is None:
        n = b.shape[0] if tb else b.shape[1]
        assert (b.shape[1] if tb else b.shape[0]) == kdim, (a.shape, b.shape, ta, tb)
        tn = _tile(n) if tn is None else tn
        b_spec = pl.BlockSpec((tn, tk), lambda i, j, k: (j, k)) if tb else pl.BlockSpec((tk, tn), lambda i, j, k: (k, j))
    else:
        n, tn, tk, b_spec = b_tiles["n"], b_tiles["tn"], b_tiles["tk"], b_tiles["spec"]
    nm, nn, nk = m // tm, n // tn, kdim // tk
    assert nm * tm == m and nn * tn == n and nk * tk == kdim, (m, n, kdim, tm, tn, tk)
    n_ex, n_out = len(extras), len(out_dtypes)
    n_alias = 0 if out_into is None else 1
    assert n_out == 1 or out_into is None
    dims = (((0 if ta else 1,), (1 if tb else 0,)), ((), ()))
    if epilogue is None:
        epilogue = lambda acc: (acc,)

    def body(a_ref, b_ref, *rest):
        ex_refs, out_refs, acc_ref = rest[:n_ex], rest[n_ex + n_alias:n_ex + n_alias + n_out], rest[-1]
        k = pl.program_id(2)

        @pl.when(k == 0)
        def _():
            acc_ref[...] = jnp.zeros_like(acc_ref)

        acc_ref[...] += lax.dot_general(a_ref[...].astype(MXU_DTYPE), b_ref[...].astype(MXU_DTYPE), dims,
                                        preferred_element_type=F32)

        @pl.when(k == nk - 1)
        def _():
            res = epilogue(acc_ref[...], *[e[...] for e in ex_refs])
            for o_ref, r in zip(out_refs, res):
                o_ref[...] = r.astype(o_ref.dtype)

    a_spec = pl.BlockSpec((tk, tm), lambda i, j, k: (k, i)) if ta else pl.BlockSpec((tm, tk), lambda i, j, k: (i, k))
    mn_spec = pl.BlockSpec((tm, tn), lambda i, j, k: (i, j))
    if out_into is None:
        outs = pl.pallas_call(
            body, name=name, grid=(nm, nn, nk),
            in_specs=[a_spec, b_spec] + [mn_spec] * n_ex,
            out_specs=[mn_spec] * n_out,
            out_shape=[jax.ShapeDtypeStruct((m, n), dt) for dt in out_dtypes],
            scratch_shapes=[pltpu.VMEM((tm, tn), F32)],
            compiler_params=_params("parallel", "parallel", "arbitrary"),
        )(a, b, *extras)
        return outs[0] if n_out == 1 else outs
    buf, buf_spec = out_into
    return pl.pallas_call(
        body, name=name, grid=(nm, nn, nk),
        in_specs=[a_spec, b_spec] + [mn_spec] * n_ex + [ANY],
        out_specs=buf_spec,
        out_shape=jax.ShapeDtypeStruct(buf.shape, buf.dtype),
        input_output_aliases={2 + n_ex: 0},
        scratch_shapes=[pltpu.VMEM((tm, tn), F32)],
        compiler_params=_params("parallel", "parallel", "arbitrary"),
    )(a, b, *extras, buf)


def _rmsnorm_fwd(x, g, *, name="rmsnorm_fwd"):
    t, d = x.shape
    bt = min(ROW_TILE, t)

    def body(x_ref, g_ref, o_ref):
        xv = x_ref[...]
        r = lax.rsqrt(jnp.mean(xv * xv, axis=-1, keepdims=True) + EPS)
        o_ref[...] = (xv * r * g_ref[...]).astype(o_ref.dtype)

    return pl.pallas_call(
        body, name=name, grid=(t // bt,),
        in_specs=[pl.BlockSpec((bt, d), lambda i: (i, 0)), pl.BlockSpec((1, d), lambda i: (0, 0))],
        out_specs=pl.BlockSpec((bt, d), lambda i: (i, 0)),
        out_shape=jax.ShapeDtypeStruct((t, d), MXU_DTYPE),
        compiler_params=_params("parallel"),
    )(x, g.reshape(1, d))


def _rmsnorm_bwd(x, g, dy, dres, *, name="rmsnorm_bwd"):
    t, d = x.shape
    bt = min(ROW_TILE, t)

    def body(x_ref, g_ref, dy_ref, dres_ref, dx_ref, dg_ref):
        @pl.when(pl.program_id(0) == 0)
        def _():
            dg_ref[...] = jnp.zeros_like(dg_ref)

        xv, dyv = x_ref[...], dy_ref[...].astype(F32)
        r = lax.rsqrt(jnp.mean(xv * xv, axis=-1, keepdims=True) + EPS)
        xhat = xv * r
        dxhat = dyv * g_ref[...]
        dx = r * (dxhat - xhat * jnp.mean(dxhat * xhat, axis=-1, keepdims=True))
        dx_ref[...] = dres_ref[...] + dx
        dg_ref[...] += jnp.sum(dyv * xhat, axis=0, keepdims=True)

    row = pl.BlockSpec((bt, d), lambda i: (i, 0))
    vec = pl.BlockSpec((1, d), lambda i: (0, 0))
    dx, dg = pl.pallas_call(
        body, name=name, grid=(t // bt,),
        in_specs=[row, vec, row, row], out_specs=[row, vec],
        out_shape=[jax.ShapeDtypeStruct((t, d), F32), jax.ShapeDtypeStruct((1, d), F32)],
        compiler_params=_params("arbitrary"),
    )(x, g.reshape(1, d), dy, dres)
    return dx, dg.reshape(d)


def _pair_rms(x, lo):
    x2 = x * x
    s_lo = jnp.sum(jnp.where(lo, x2, 0.0), axis=-1, keepdims=True)
    s_hi = jnp.sum(jnp.where(lo, 0.0, x2), axis=-1, keepdims=True)
    return jnp.where(lo, lax.rsqrt(s_lo / FOX_DIM + EPS), lax.rsqrt(s_hi / FOX_DIM + EPS))


def _pair_sum(x, lo):
    s_lo = jnp.sum(jnp.where(lo, x, 0.0), axis=-1, keepdims=True)
    s_hi = jnp.sum(jnp.where(lo, 0.0, x), axis=-1, keepdims=True)
    return jnp.where(lo, s_lo, s_hi)


def _prep_a_fwd(z, g_cq, g_ckv, g_qf, g_kf, b_f):
    t = z.shape[0]
    bt = min(PREP_TILE, t)
    hw = HEADS * FOX_DIM

    def body(z_ref, gcq_ref, gckv_ref, gqf_ref, gkf_ref, bf_ref, cqn_ref, ckvn_ref, qf_ref, kf_ref, vf_ref, logf_ref):
        i = pl.program_id(0)
        cq = z_ref[:, ZA_CQ:ZA_CQ + Q_LORA]
        cqn_ref[...] = (cq * lax.rsqrt(jnp.mean(cq * cq, axis=-1, keepdims=True) + EPS) * gcq_ref[...]).astype(cqn_ref.dtype)
        ckv = z_ref[:, ZA_CKV:ZA_CKV + KV_LORA]
        ckvn_ref[...] = (ckv * lax.rsqrt(jnp.mean(ckv * ckv, axis=-1, keepdims=True) + EPS) * gckv_ref[...]).astype(ckvn_ref.dtype)
        lo = _iota((1, LANES), 1) < FOX_DIM
        for p in range(HEADS // 2):
            sl = slice(p * LANES, (p + 1) * LANES)
            xq = z_ref[:, ZA_FQ + p * LANES:ZA_FQ + (p + 1) * LANES]
            qf_ref[:, sl] = (xq * _pair_rms(xq, lo) * gqf_ref[...]).astype(qf_ref.dtype)
            xk = z_ref[:, ZA_FK + p * LANES:ZA_FK + (p + 1) * LANES]
            kf_ref[:, sl] = (xk * _pair_rms(xk, lo) * gkf_ref[...]).astype(kf_ref.dtype)
        vf_ref[...] = z_ref[:, ZA_FV:ZA_FV + hw].astype(vf_ref.dtype)
        xl = z_ref[:, ZA_LAST:ZA_LAST + LANES] + bf_ref[...]
        logf = jnp.minimum(xl, 0.0) - jnp.log(1.0 + jnp.exp(-jnp.abs(xl)))
        row = i * bt + _iota((bt, LANES), 0)
        lane = _iota((bt, LANES), 1)
        logf_ref[...] = jnp.where((lane < HEADS) & (row >= PAD), logf, 0.0)

    def vec(w):
        return pl.BlockSpec((1, w), lambda i: (0, 0))

    def row(w):
        return pl.BlockSpec((bt, w), lambda i: (i, 0))

    return pl.pallas_call(
        body, name="attn_prep_a_fwd", grid=(t // bt,),
        in_specs=[row(ZA_W), vec(Q_LORA), vec(KV_LORA), vec(LANES), vec(LANES), vec(LANES)],
        out_specs=[row(Q_LORA), row(KV_LORA), row(hw), row(hw), row(hw), row(LANES)],
        out_shape=[jax.ShapeDtypeStruct((t, Q_LORA), MXU_DTYPE), jax.ShapeDtypeStruct((t, KV_LORA), MXU_DTYPE),
                   jax.ShapeDtypeStruct((t, hw), MXU_DTYPE), jax.ShapeDtypeStruct((t, hw), MXU_DTYPE),
                   jax.ShapeDtypeStruct((t, hw), MXU_DTYPE), jax.ShapeDtypeStruct((t, LANES), F32)],
        compiler_params=_params("parallel"),
    )(z, g_cq, g_ckv, g_qf, g_kf, b_f)


def _prep_a_bwd(z, g_cq, g_ckv, g_qf, g_kf, b_f, dcqn, dckvn, dqf, dkf, dvf, dlogf, dkpe):
    t = z.shape[0]
    bt = min(PREP_TILE, t)
    hw = HEADS * FOX_DIM

    def norm_bwd(x, g, dy):
        r = lax.rsqrt(jnp.mean(x * x, axis=-1, keepdims=True) + EPS)
        xhat = x * r
        dxhat = dy * g
        dx = r * (dxhat - xhat * jnp.mean(dxhat * xhat, axis=-1, keepdims=True))
        return dx, jnp.sum(dy * xhat, axis=0, keepdims=True)

    def body(z_ref, gcq_ref, gckv_ref, gqf_ref, gkf_ref, bf_ref, dcqn_ref, dckvn_ref, dqf_ref, dkf_ref, dvf_ref,
             dlogf_ref, dkpe_ref, dz_ref, dgcq_ref, dgckv_ref, dgqf_ref, dgkf_ref, dbf_ref):
        i = pl.program_id(0)

        @pl.when(i == 0)
        def _():
            for r in (dgcq_ref, dgckv_ref, dgqf_ref, dgkf_ref, dbf_ref):
                r[...] = jnp.zeros_like(r)

        dx, dg = norm_bwd(z_ref[:, ZA_CQ:ZA_CQ + Q_LORA], gcq_ref[...], dcqn_ref[...])
        dz_ref[:, ZA_CQ:ZA_CQ + Q_LORA] = dx.astype(dz_ref.dtype)
        dgcq_ref[...] += dg
        dx, dg = norm_bwd(z_ref[:, ZA_CKV:ZA_CKV + KV_LORA], gckv_ref[...], dckvn_ref[...])
        dz_ref[:, ZA_CKV:ZA_CKV + KV_LORA] = dx.astype(dz_ref.dtype)
        dgckv_ref[...] += dg
        lo = _iota((1, LANES), 1) < FOX_DIM
        for base, g_ref, dy_ref, dg_ref in ((ZA_FQ, gqf_ref, dqf_ref, dgqf_ref), (ZA_FK, gkf_ref, dkf_ref, dgkf_ref)):
            for p in range(HEADS // 2):
                x = z_ref[:, base + p * LANES:base + (p + 1) * LANES]
                dy = dy_ref[:, p * LANES:(p + 1) * LANES]
                r = _pair_rms(x, lo)
                xhat = x * r
                dxhat = dy * g_ref[...]
                dx = r * (dxhat - xhat * _pair_sum(dxhat * xhat, lo) / FOX_DIM)
                dz_ref[:, base + p * LANES:base + (p + 1) * LANES] = dx.astype(dz_ref.dtype)
                dg_ref[...] += jnp.sum(dy * xhat, axis=0, keepdims=True)
        dz_ref[:, ZA_FV:ZA_FV + hw] = dvf_ref[...].astype(dz_ref.dtype)
        xl = z_ref[:, ZA_LAST:ZA_LAST + LANES] + bf_ref[...]
        row = i * bt + _iota((bt, LANES), 0)
        lane = _iota((bt, LANES), 1)
        dfl = jnp.where((lane < HEADS) & (row >= PAD), dlogf_ref[...] / (1.0 + jnp.exp(xl)), 0.0)
        dbf_ref[...] += jnp.sum(dfl, axis=0, keepdims=True)
        dz_ref[:, ZA_LAST:ZA_LAST + LANES] = (dfl + dkpe_ref[...]).astype(dz_ref.dtype)

    def vec(w):
        return pl.BlockSpec((1, w), lambda i: (0, 0))

    def row(w):
        return pl.BlockSpec((bt, w), lambda i: (i, 0))

    return pl.pallas_call(
        body, name="attn_prep_a_bwd", grid=(t // bt,),
        in_specs=[row(ZA_W), vec(Q_LORA), vec(KV_LORA), vec(LANES), vec(LANES), vec(LANES),
                  row(Q_LORA), row(KV_LORA), row(hw), row(hw), row(hw), row(LANES), row(LANES)],
        out_specs=[row(ZA_W), vec(Q_LORA), vec(KV_LORA), vec(LANES), vec(LANES), vec(LANES)],
        out_shape=[jax.ShapeDtypeStruct((t, ZA_W), MXU_DTYPE), jax.ShapeDtypeStruct((1, Q_LORA), F32),
                   jax.ShapeDtypeStruct((1, KV_LORA), F32), jax.ShapeDtypeStruct((1, LANES), F32),
                   jax.ShapeDtypeStruct((1, LANES), F32), jax.ShapeDtypeStruct((1, LANES), F32)],
        compiler_params=_params("arbitrary"),
    )(z, g_cq, g_ckv, g_qf, g_kf, b_f, dcqn, dckvn, dqf, dkf, dvf, dlogf, dkpe)


def _cumsum_rows(x, *, reverse, name, out_scale=1.0):
    t = x.shape[0]
    nblk = t // LANES

    def body(x_ref, f_ref, ft_ref, carry_ref):
        r = _iota((LANES, LANES), 0)
        c = _iota((LANES, LANES), 1)
        tri = jnp.where((c >= r) if reverse else (c <= r), 1.0, 0.0).astype(F32)
        carry_ref[...] = jnp.zeros_like(carry_ref)

        def step(s, _):
            b = (nblk - 1 - s) if reverse else s
            start = pl.multiple_of(b * LANES, LANES)
            blk = x_ref[pl.ds(start, LANES), :]
            cs = jnp.dot(tri, blk, precision=lax.Precision.HIGHEST, preferred_element_type=F32) + carry_ref[0:1, :]
            scaled = cs if out_scale == 1.0 else cs * out_scale
            f_ref[pl.ds(start, LANES), :] = scaled
            ft_ref[:, pl.ds(start, LANES)] = scaled.T
            carry_ref[0:1, :] = cs[0:1, :] if reverse else cs[LANES - 1:LANES, :]
            return 0

        lax.fori_loop(0, nblk, step, 0)

    return pl.pallas_call(
        body, name=name, grid=(1,),
        in_specs=[pl.BlockSpec((t, LANES), lambda i: (0, 0))],
        out_specs=[pl.BlockSpec((t, LANES), lambda i: (0, 0)), pl.BlockSpec((LANES, t), lambda i: (0, 0))],
        out_shape=[jax.ShapeDtypeStruct((t, LANES), F32), jax.ShapeDtypeStruct((LANES, t), F32)],
        scratch_shapes=[pltpu.VMEM((8, LANES), F32)],
        compiler_params=_params("arbitrary"),
    )(x)


def _rope_partner(x, lane):
    half = MLA_ROPE // 2
    swapped = jnp.where(lane < KPE_LANE + half, pltpu.roll(x, LANES - half, axis=1), pltpu.roll(x, half, axis=1))
    return jnp.where((lane >= KPE_LANE) & (lane < KPE_LANE + MLA_ROPE), swapped, 0.0)


def _rope_tables(t):
    pos = (jnp.arange(t, dtype=jnp.int32) - PAD).astype(F32)
    inv_freq = ROPE_BASE ** (-jnp.arange(0, MLA_ROPE, 2, dtype=F32) / MLA_ROPE)
    ang = pos[:, None] * inv_freq[None, :]
    cos, sin = jnp.cos(ang), jnp.sin(ang)
    ones = jnp.ones((t, KPE_LANE), F32)
    tail = jnp.zeros((t, LANES - KPE_LANE - MLA_ROPE), F32)
    c_tab = jnp.concatenate([ones, cos, cos, tail + 1.0], axis=1)
    s_tab = jnp.concatenate([ones * 0.0, -sin, sin, tail], axis=1)
    return c_tab, s_tab


def _prep_b_fwd(q_raw, kv_raw, z, c_tab, s_tab, g_q, g_k):
    t = q_raw.shape[0]
    bt = min(PREP_TILE, t)
    qw = HEADS * LANES
    vw = HEADS * HEAD_V

    def body(q_ref, kv_ref, zl_ref, c_ref, s_ref, gq_ref, gk_ref, qn_ref, kn_ref, v_ref):
        lane = _iota((1, LANES), 1)
        kpe = jnp.where((lane >= KPE_LANE) & (lane < KPE_LANE + MLA_ROPE), zl_ref[...], 0.0)
        cv, sv = c_ref[...], s_ref[...]
        for h in range(HEADS):
            sl = slice(h * LANES, (h + 1) * LANES)
            for x, g_ref, o_ref in ((q_ref[:, sl], gq_ref, qn_ref), (kv_ref[:, sl] + kpe, gk_ref, kn_ref)):
                r = lax.rsqrt(jnp.sum(x * x, axis=-1, keepdims=True) / MLA_QK + EPS)
                xn = x * r * g_ref[...]
                o_ref[:, sl] = (xn * cv + _rope_partner(xn, lane) * sv).astype(o_ref.dtype)
        v_ref[...] = kv_ref[:, qw:qw + vw].astype(v_ref.dtype)

    def row(w):
        return pl.BlockSpec((bt, w), lambda i: (i, 0))

    vec = pl.BlockSpec((1, LANES), lambda i: (0, 0))
    return pl.pallas_call(
        body, name="attn_prep_b_fwd", grid=(t // bt,),
        in_specs=[row(qw), row(qw + vw), pl.BlockSpec((bt, LANES), lambda i: (i, ZA_LAST // LANES)),
                  row(LANES), row(LANES), vec, vec],
        out_specs=[row(qw), row(qw), row(vw)],
        out_shape=[jax.ShapeDtypeStruct((t, qw), MXU_DTYPE), jax.ShapeDtypeStruct((t, qw), MXU_DTYPE),
                   jax.ShapeDtypeStruct((t, vw), MXU_DTYPE)],
        compiler_params=_params("parallel"),
    )(q_raw, kv_raw, z, c_tab, s_tab, g_q, g_k)


def _prep_b_bwd(q_raw, kv_raw, z, c_tab, s_tab, g_q, g_k, dqn, dkn, dv):
    t = q_raw.shape[0]
    bt = min(PREP_TILE, t)
    qw = HEADS * LANES
    vw = HEADS * HEAD_V

    def body(q_ref, kv_ref, zl_ref, c_ref, s_ref, gq_ref, gk_ref, dqn_ref, dkn_ref, dv_ref,
             dq_ref, dkv_ref, dkpe_ref, dgq_ref, dgk_ref):
        @pl.when(pl.program_id(0) == 0)
        def _():
            dgq_ref[...] = jnp.zeros_like(dgq_ref)
            dgk_ref[...] = jnp.zeros_like(dgk_ref)

        lane = _iota((1, LANES), 1)
        rope_lanes = (lane >= KPE_LANE) & (lane < KPE_LANE + MLA_ROPE)
        kpe = jnp.where(rope_lanes, zl_ref[...], 0.0)
        cv, sv = c_ref[...], s_ref[...]
        dkpe = jnp.zeros((bt, LANES), F32)
        for h in range(HEADS):
            sl = slice(h * LANES, (h + 1) * LANES)
            for is_k, x, g_ref, dout, dg_ref in ((False, q_ref[:, sl], gq_ref, dqn_ref[:, sl], dgq_ref),
                                                  (True, kv_ref[:, sl] + kpe, gk_ref, dkn_ref[:, sl], dgk_ref)):
                r = lax.rsqrt(jnp.sum(x * x, axis=-1, keepdims=True) / MLA_QK + EPS)
                xhat = x * r
                dxn = dout * cv + _rope_partner(dout * sv, lane)
                dg_ref[...] += jnp.sum(dxn * xhat, axis=0, keepdims=True)
                dxhat = dxn * g_ref[...]
                dx = r * (dxhat - xhat * (jnp.sum(dxhat * xhat, axis=-1, keepdims=True) / MLA_QK))
                if is_k:
                    dkv_ref[:, sl] = jnp.where(lane < KPE_LANE, dx, 0.0).astype(dkv_ref.dtype)
                    dkpe = dkpe + jnp.where(rope_lanes, dx, 0.0)
                else:
                    dq_ref[:, sl] = dx.astype(dq_ref.dtype)
        dkv_ref[:, qw:qw + vw] = dv_ref[...].astype(dkv_ref.dtype)
        dkpe_ref[...] = dkpe

    def row(w):
        return pl.BlockSpec((bt, w), lambda i: (i, 0))

    vec = pl.BlockSpec((1, LANES), lambda i: (0, 0))
    return pl.pallas_call(
        body, name="attn_prep_b_bwd", grid=(t // bt,),
        in_specs=[row(qw), row(qw + vw), pl.BlockSpec((bt, LANES), lambda i: (i, ZA_LAST // LANES)),
                  row(LANES), row(LANES), vec, vec, row(qw), row(qw), row(vw)],
        out_specs=[row(qw), row(qw + vw), row(LANES), vec, vec],
        out_shape=[jax.ShapeDtypeStruct((t, qw), MXU_DTYPE), jax.ShapeDtypeStruct((t, qw + vw), MXU_DTYPE),
                   jax.ShapeDtypeStruct((t, LANES), F32), jax.ShapeDtypeStruct((1, LANES), F32),
                   jax.ShapeDtypeStruct((1, LANES), F32)],
        compiler_params=_params("arbitrary"),
    )(q_raw, kv_raw, z, c_tab, s_tab, g_q, g_k, dqn, dkn, dv)


NT_DIMS = (((1,), (1,)), ((), ()))
TN_DIMS = (((0,), (0,)), ((), ()))


def _head_qk(q_ref, k_ref, e, mla, lo):
    if mla:
        return q_ref[:, e * LANES:(e + 1) * LANES], k_ref[:, e * LANES:(e + 1) * LANES]
    q = q_ref[...]
    return jnp.where(lo if e == 0 else jnp.logical_not(lo), q, jnp.zeros_like(q)), k_ref[...]


def _attn_specs(mla, blk, q_map, k_map):
    w = 2 * LANES if mla else LANES
    q_spec = pl.BlockSpec((blk, w), lambda j, a, b: (q_map(a, b), j))
    k_spec = pl.BlockSpec((blk, w), lambda j, a, b: (k_map(a, b), j))
    qv_spec = pl.BlockSpec((blk, LANES), lambda j, a, b: (q_map(a, b), j))
    kv_spec = pl.BlockSpec((blk, LANES), lambda j, a, b: (k_map(a, b), j))
    fq_spec = pl.BlockSpec((blk, LANES), lambda j, a, b: (q_map(a, b), 0))
    fk_spec = pl.BlockSpec((8, blk), lambda j, a, b: (0, k_map(a, b)))
    return q_spec, k_spec, qv_spec, kv_spec, fq_spec, fk_spec


def _flash_fwd(q, k, v, f, f_t, *, mla, scale, name):
    t = q.shape[0]
    blk = min(ATTN_BLOCK, t)
    nb = t // blk
    pairs = HEADS // 2
    q_spec, k_spec, qv_spec, kv_spec, fq_spec, fk_spec = _attn_specs(
        mla, blk, lambda i, kk: i, lambda i, kk: jnp.minimum(kk, i))

    def body(*refs):
        if mla:
            q_ref, k_ref, v_ref, o_ref, lse_ref, m_s, l_s, acc_s = refs
            fq_ref = fk_ref = None
        else:
            q_ref, k_ref, v_ref, fq_ref, fk_ref, o_ref, lse_ref, m_s, l_s, acc_s = refs
        j, i, kk = pl.program_id(0), pl.program_id(1), pl.program_id(2)
        lo = _iota((1, LANES), 1) < HEAD_V

        @pl.when(kk == 0)
        def _():
            m_s[...] = jnp.full_like(m_s, NEG)
            l_s[...] = jnp.zeros_like(l_s)
            acc_s[...] = jnp.zeros_like(acc_s)

        def step(mask):
            vv = v_ref[...]
            for e in range(2):
                s, _, _ = _scores(q_ref, k_ref, fq_ref, fk_ref, e, j, mla, scale, lo, mask)
                m_prev = m_s[e]
                m_new = jnp.maximum(m_prev, jnp.max(s, axis=1, keepdims=True))
                alpha = jnp.exp(m_prev - m_new)
                p = jnp.exp(s - m_new)
                if mask is not None:
                    p = jnp.where(mask, p, 0.0)
                l_s[e] = alpha * l_s[e] + jnp.sum(p, axis=1, keepdims=True)
                acc_s[e] = alpha * acc_s[e] + jnp.dot(p.astype(MXU_DTYPE), vv, preferred_element_type=F32)
                m_s[e] = m_new

        _masked_and_plain(kk <= i, i, kk, blk, step)

        @pl.when(kk == nb - 1)
        def _():
            valid = (i * blk + _iota((blk, 1), 0)) >= PAD
            outs, lses = [], []
            for e in range(2):
                l = l_s[e]
                outs.append(acc_s[e] * jnp.where(l > 0.0, 1.0 / jnp.where(l > 0.0, l, 1.0), 0.0))
                lses.append(m_s[e] + jnp.log(jnp.where(l > 0.0, l, 1.0)))
            o = jnp.where(lo, outs[0], outs[1])
            o_ref[...] = jnp.where(valid, o, 0.0).astype(o_ref.dtype)
            lane = _iota((1, LANES), 1)
            lse_ref[...] = jnp.where(lane == 0, lses[0], jnp.where(lane == 1, lses[1], 0.0))

    in_specs = [q_spec, k_spec, kv_spec] + ([] if mla else [fq_spec, fk_spec])
    args = (q, k, v) + (() if mla else (f, f_t))
    hv = HEADS * HEAD_V
    return pl.pallas_call(
        body, name=name, grid=(pairs, nb, nb),
        in_specs=in_specs, out_specs=[qv_spec, qv_spec],
        out_shape=[jax.ShapeDtypeStruct((t, hv), F32), jax.ShapeDtypeStruct((t, hv), F32)],
        scratch_shapes=[pltpu.VMEM((2, blk, 1), F32), pltpu.VMEM((2, blk, 1), F32), pltpu.VMEM((2, blk, LANES), F32)],
        compiler_params=_params("parallel", "parallel", "arbitrary"),
    )(*args)


def _bwd_tile(q_ref, k_ref, v_ref, o_ref, do_ref, lse_ref, fq_ref, fk_ref, e, pair, mla, scale, lo, mask):
    s, qe, ke = _scores(q_ref, k_ref, fq_ref, fk_ref, e, pair, mla, scale, lo, mask)
    p = jnp.exp(s - lse_ref[:, e:e + 1])
    if mask is not None:
        p = jnp.where(mask, p, 0.0)
    do = do_ref[...]
    doe = jnp.where(lo if e == 0 else jnp.logical_not(lo), do, jnp.zeros_like(do))
    dp = lax.dot_general(doe, v_ref[...], NT_DIMS, preferred_element_type=F32)
    delta = jnp.sum(doe.astype(F32) * o_ref[...].astype(F32), axis=1, keepdims=True)
    return p, p * (dp - delta), qe, ke


def _flash_bwd_dq(q, k, v, o, do, lse, f, f_t, *, mla, scale, col0, name):
    t = q.shape[0]
    blk = min(ATTN_BLOCK, t)
    nb = t // blk
    pairs = HEADS // 2
    w = 2 * LANES if mla else LANES
    q_spec, k_spec, qv_spec, kv_spec, fq_spec, fk_spec = _attn_specs(
        mla, blk, lambda i, kk: i, lambda i, kk: jnp.minimum(kk, i))
    od_spec = pl.BlockSpec((blk, LANES), lambda j, i, kk: (i, col0 + j))

    def body(*refs):
        if mla:
            q_ref, k_ref, v_ref, o_ref, do_ref, lse_ref, dq_ref, dq_s = refs
            fq_ref = fk_ref = rs_ref = rs_s = None
        else:
            q_ref, k_ref, v_ref, o_ref, do_ref, lse_ref, fq_ref, fk_ref, dq_ref, rs_ref, dq_s, rs_s = refs
        j, i, kk = pl.program_id(0), pl.program_id(1), pl.program_id(2)
        lo = _iota((1, LANES), 1) < HEAD_V

        @pl.when(kk == 0)
        def _():
            dq_s[...] = jnp.zeros_like(dq_s)
            if not mla:
                rs_s[...] = jnp.zeros_like(rs_s)

        def step(mask):
            for e in range(2):
                _, ds, _, ke = _bwd_tile(q_ref, k_ref, v_ref, o_ref, do_ref, lse_ref, fq_ref, fk_ref, e, j, mla, scale,
                                         lo, mask)
                dq_s[e] += jnp.dot(ds.astype(MXU_DTYPE), ke, preferred_element_type=F32)
                if not mla:
                    rs_s[e] += jnp.sum(ds, axis=1, keepdims=True)

        _masked_and_plain(kk <= i, i, kk, blk, step)

        @pl.when(kk == nb - 1)
        def _():
            if mla:
                dq_ref[:, 0:LANES] = dq_s[0] * scale
                dq_ref[:, LANES:2 * LANES] = dq_s[1] * scale
            else:
                dq_ref[...] = jnp.where(lo, dq_s[0], dq_s[1]) * scale
                lane = _iota((1, LANES), 1)
                rs_ref[...] = jnp.where(lane == 0, rs_s[0], jnp.where(lane == 1, rs_s[1], 0.0))

    in_specs = [q_spec, k_spec, kv_spec, od_spec, od_spec, qv_spec] + ([] if mla else [fq_spec, fk_spec])
    args = (q, k, v, o, do, lse) + (() if mla else (f, f_t))
    out_specs = [q_spec] + ([] if mla else [qv_spec])
    out_shape = [jax.ShapeDtypeStruct((t, pairs * w), F32)]
    scratch = [pltpu.VMEM((2, blk, LANES), F32)]
    if not mla:
        out_shape.append(jax.ShapeDtypeStruct((t, pairs * LANES), F32))
        scratch.append(pltpu.VMEM((2, blk, 1), F32))
    outs = pl.pallas_call(
        body, name=name, grid=(pairs, nb, nb),
        in_specs=in_specs, out_specs=out_specs, out_shape=out_shape, scratch_shapes=scratch,
        compiler_params=_params("parallel", "parallel", "arbitrary"),
    )(*args)
    return outs[0] if mla else outs


def _flash_bwd_dkv(q, k, v, o, do, lse, f, f_t, *, mla, scale, col0, name):
    t = q.shape[0]
    blk = min(ATTN_BLOCK, t)
    nb = t // blk
    pairs = HEADS // 2
    w = 2 * LANES if mla else LANES
    q_spec, k_spec, qv_spec, kv_spec, fq_spec, fk_spec = _attn_specs(
        mla, blk, lambda a, b: jnp.maximum(a, b), lambda a, b: a)
    od_spec = pl.BlockSpec((blk, LANES), lambda j, a, b: (jnp.maximum(a, b), col0 + j))
    cs_spec = pl.BlockSpec((8, blk), lambda j, a, b: (j, a))

    def body(*refs):
        if mla:
            q_ref, k_ref, v_ref, o_ref, do_ref, lse_ref, dk_ref, dv_ref, dk_s, dv_s = refs
            fq_ref = fk_ref = cs_ref = cs_s = None
        else:
            q_ref, k_ref, v_ref, o_ref, do_ref, lse_ref, fq_ref, fk_ref, dk_ref, dv_ref, cs_ref, dk_s, dv_s, cs_s = refs
        j, kb, qb = pl.program_id(0), pl.program_id(1), pl.program_id(2)
        lo = _iota((1, LANES), 1) < HEAD_V

        @pl.when(qb == 0)
        def _():
            dk_s[...] = jnp.zeros_like(dk_s)
            dv_s[...] = jnp.zeros_like(dv_s)
            if not mla:
                cs_s[...] = jnp.zeros_like(cs_s)

        def step(mask):
            do = do_ref[...]
            for e in range(2):
                p, ds, _, _ = _bwd_tile(q_ref, k_ref, v_ref, o_ref, do_ref, lse_ref, fq_ref, fk_ref, e, j, mla, scale,
                                        lo, mask)
                dv_s[e] += lax.dot_general(p.astype(MXU_DTYPE), do, TN_DIMS, preferred_element_type=F32)
                q_src = q_ref[:, e * LANES:(e + 1) * LANES] if mla else q_ref[...]
                dk_s[e] += lax.dot_general(ds.astype(MXU_DTYPE), q_src, TN_DIMS, preferred_element_type=F32)
                if not mla:
                    cs_s[e] += jnp.sum(ds, axis=0, keepdims=True)

        _masked_and_plain(qb >= kb, qb, kb, blk, step)

        @pl.when(qb == nb - 1)
        def _():
            dv_ref[...] = jnp.where(lo, dv_s[0], dv_s[1])
            if mla:
                dk_ref[:, 0:LANES] = dk_s[0] * scale
                dk_ref[:, LANES:2 * LANES] = dk_s[1] * scale
            else:
                dk_ref[...] = jnp.where(lo, dk_s[0], dk_s[1]) * scale
                sub = _iota((8, 1), 0)
                cs_ref[...] = jnp.where(sub == 0, cs_s[0], jnp.where(sub == 1, cs_s[1], 0.0))

    in_specs = [q_spec, k_spec, kv_spec, od_spec, od_spec, qv_spec] + ([] if mla else [fq_spec, fk_spec])
    args = (q, k, v, o, do, lse) + (() if mla else (f, f_t))
    out_specs = [k_spec, kv_spec] + ([] if mla else [cs_spec])
    out_shape = [jax.ShapeDtypeStruct((t, pairs * w), F32), jax.ShapeDtypeStruct((t, HEADS * HEAD_V), F32)]
    scratch = [pltpu.VMEM((2, blk, LANES), F32), pltpu.VMEM((2, blk, LANES), F32)]
    if not mla:
        out_shape.append(jax.ShapeDtypeStruct((pairs * 8, t), F32))
        scratch.append(pltpu.VMEM((2, 1, blk), F32))
    return pl.pallas_call(
        body, name=name, grid=(pairs, nb, nb),
        in_specs=in_specs, out_specs=out_specs, out_shape=out_shape, scratch_shapes=scratch,
        compiler_params=_params("parallel", "parallel", "arbitrary"),
    )(*args)


ATTN_CHUNK = 640
LOG2E = 1.4426950408889634
LN2 = 0.6931471805599453


def _for_chunks(n, ch, body):
    for c in range(n):
        body(c * ch)


def _select_lane(x, idx):
    return jnp.sum(jnp.where(_iota(x.shape, 1) == idx, x, 0.0), axis=1, keepdims=True)


def _select_row(x, idx):
    return jnp.sum(jnp.where(_iota(x.shape, 0) == idx, x, 0.0), axis=0, keepdims=True)


def _flash_fwd_chunked(q, k, v, f, f_t, *, mla, name):
    t = q.shape[0]
    blk = min(ATTN_BLOCK, t)
    nb = t // blk
    pairs = HEADS // 2
    ch = min(ATTN_CHUNK, blk)
    assert blk % ch == 0 and blk >= PAD
    w = 2 * LANES if mla else LANES
    q_of, k_of = _causal_pairs(nb, by_key=False)
    q_spec = pl.BlockSpec((blk, w), lambda j, s, qt, kt: (qt[s], j))
    k_spec = pl.BlockSpec((blk, w), lambda j, s, qt, kt: (kt[s], j))
    kv_spec = pl.BlockSpec((blk, LANES), lambda j, s, qt, kt: (kt[s], j))
    qv_spec = pl.BlockSpec((blk, LANES), lambda j, s, qt, kt: (qt[s], j))
    fq_spec = pl.BlockSpec((blk, LANES), lambda j, s, qt, kt: (qt[s], 0))
    fk_spec = pl.BlockSpec((8, blk), lambda j, s, qt, kt: (0, kt[s]))
    lse_spec = pl.BlockSpec((8, blk), lambda j, s, qt, kt: (j, qt[s]))

    def body(qt_ref, kt_ref, *refs):
        if mla:
            q_ref, k_ref, v_ref, o_ref, lse_ref, m_s, l_s, a_s, acc_s, s_s, p_s = refs
            fq_ref = fk_ref = None
        else:
            q_ref, k_ref, v_ref, fq_ref, fk_ref, o_ref, lse_ref, m_s, l_s, a_s, acc_s, s_s, p_s = refs
        j, step_id = pl.program_id(0), pl.program_id(1)
        i, kk = qt_ref[step_id], kt_ref[step_id]
        lo = _iota((1, LANES), 1) < HEAD_V

        @pl.when(kk == 0)
        def _():
            m_s[...] = jnp.full_like(m_s, NEG)
            l_s[...] = jnp.zeros_like(l_s)
            acc_s[...] = jnp.zeros_like(acc_s)

        def step(masked):
            vv = v_ref[...]
            for e in range(2):
                qe, ke = _head_qk(q_ref, k_ref, e, mla, lo)
                s_s[...] = lax.dot_general(qe, ke, NT_DIMS, preferred_element_type=F32)
                fkr = None if mla else _select_row(fk_ref[...], 2 * j + e)

                def chunk(r0, e=e, fkr=fkr):
                    rows = pl.ds(r0, ch)
                    s = s_s[rows, :]
                    if not mla:
                        s = s + _select_lane(fq_ref[rows, :], 2 * j + e) - fkr
                    if masked:
                        rpos = i * blk + r0 + _iota((ch, blk), 0)
                        cpos = kk * blk + _iota((ch, blk), 1)
                        mask = (cpos <= rpos) & (cpos >= PAD)
                        s = jnp.where(mask, s, NEG)
                    m_prev = m_s[e, rows, :]
                    m_new = jnp.maximum(m_prev, jnp.max(s, axis=1, keepdims=True))
                    alpha = jnp.exp2(m_prev - m_new)
                    p = jnp.exp2(s - m_new)
                    if masked:
                        p = jnp.where(mask, p, 0.0)
                    l_s[e, rows, :] = alpha * l_s[e, rows, :] + jnp.sum(p, axis=1, keepdims=True)
                    m_s[e, rows, :] = m_new
                    a_s[rows, :] = alpha
                    p_s[rows, :] = p.astype(p_s.dtype)

                _for_chunks(blk // ch, ch, chunk)
                acc_s[e] = a_s[...] * acc_s[e] + jnp.dot(p_s[...], vv, preferred_element_type=F32)

        needs_mask = (kk == i) | (kk == 0)

        @pl.when(needs_mask)
        def _():
            step(True)

        @pl.when(jnp.logical_not(needs_mask))
        def _():
            step(False)

        @pl.when(kk == i)
        def _():
            valid = (i * blk + _iota((blk, 1), 0)) >= PAD
            outs, lses = [], []
            for e in range(2):
                l = l_s[e]
                outs.append(acc_s[e] * jnp.where(l > 0.0, 1.0 / jnp.where(l > 0.0, l, 1.0), 0.0))
                lses.append(m_s[e] + jnp.log(jnp.where(l > 0.0, l, 1.0)) * LOG2E)
            o = jnp.where(lo, outs[0], outs[1])
            o_ref[...] = jnp.where(valid, o, 0.0).astype(o_ref.dtype)
            lane = _iota((1, LANES), 1)
            lse_cols = jnp.where(lane == 0, lses[0], jnp.where(lane == 1, lses[1], 0.0))
            lse_ref[...] = lse_cols.T[0:8, :]

    in_specs = [q_spec, k_spec, kv_spec] + ([] if mla else [fq_spec, fk_spec])
    args = (q, k, v) + (() if mla else (f, f_t))
    hv = HEADS * HEAD_V
    grid_spec = pltpu.PrefetchScalarGridSpec(
        num_scalar_prefetch=2, grid=(pairs, len(q_of)), in_specs=in_specs, out_specs=[qv_spec, lse_spec],
        scratch_shapes=[pltpu.VMEM((2, blk, 1), F32), pltpu.VMEM((2, blk, 1), F32), pltpu.VMEM((blk, 1), F32),
                        pltpu.VMEM((2, blk, LANES), F32), pltpu.VMEM((blk, blk), F32), pltpu.VMEM((blk, blk), MXU_DTYPE)])
    return pl.pallas_call(
        body, name=name, grid_spec=grid_spec,
        out_shape=[jax.ShapeDtypeStruct((t, hv), F32), jax.ShapeDtypeStruct((pairs * 8, t), F32)],
        compiler_params=_params("parallel", "arbitrary"),
    )(jnp.asarray(q_of), jnp.asarray(k_of), *args)


def _causal_pairs(nb, *, by_key):
    if by_key:
        pairs = [(qb, kb) for kb in range(nb) for qb in range(kb, nb)]
    else:
        pairs = [(qb, kb) for qb in range(nb) for kb in range(qb + 1)]
    return (np.asarray([p[0] for p in pairs], np.int32), np.asarray([p[1] for p in pairs], np.int32))


def _delta_rows(do, o):
    t, width = o.shape
    bt = min(ATTN_BLOCK, t)
    n_heads = width // HEAD_V

    def body(do_ref, o_ref, d_ref):
        prod = do_ref[...].astype(F32) * o_ref[...]
        col = _iota((width, LANES), 0)
        first = _iota((width, LANES), 1) * HEAD_V
        sel = jnp.where((col >= first) & (col < first + HEAD_V), 1.0, 0.0).astype(F32)
        per_head = jnp.dot(prod, sel, precision=lax.Precision.HIGHEST, preferred_element_type=F32)
        d_ref[...] = per_head.T[0:n_heads, :]

    return pl.pallas_call(
        body, name="attn_delta", grid=(t // bt,),
        in_specs=[pl.BlockSpec((bt, width), lambda i: (i, 0)), pl.BlockSpec((bt, width), lambda i: (i, 0))],
        out_specs=pl.BlockSpec((n_heads, bt), lambda i: (0, i)),
        out_shape=jax.ShapeDtypeStruct((n_heads, t), F32),
        compiler_params=_params("parallel"),
    )(do, o)


def _flash_bwd_fused(q, k, v, do, lse_t, delta_t, f, f_t, *, mla, col0, name):
    t = q.shape[0]
    blk = min(ATTN_BLOCK, t)
    nb = t // blk
    pairs = HEADS // 2
    ch = min(ATTN_CHUNK, blk)
    assert blk % ch == 0
    w = 2 * LANES if mla else LANES
    last = nb - 1
    assert blk >= PAD
    q_of, k_of = _causal_pairs(nb, by_key=True)
    q_spec = pl.BlockSpec((blk, w), lambda j, s, qt, kt: (qt[s], j))
    k_spec = pl.BlockSpec((blk, w), lambda j, s, qt, kt: (kt[s], j))
    v_spec = pl.BlockSpec((blk, LANES), lambda j, s, qt, kt: (kt[s], j))
    do_spec = pl.BlockSpec((blk, LANES), lambda j, s, qt, kt: (qt[s], col0 + j))
    lse_spec = pl.BlockSpec((8, blk), lambda j, s, qt, kt: (j, qt[s]))
    delta_spec = pl.BlockSpec((8, blk), lambda j, s, qt, kt: (col0 // (HEADS // 2), qt[s]))
    fq_spec = pl.BlockSpec((8, blk), lambda j, s, qt, kt: (0, qt[s]))
    fk_spec = pl.BlockSpec((blk, LANES), lambda j, s, qt, kt: (kt[s], 0))
    dq_spec = pl.BlockSpec((blk, w), lambda j, s, qt, kt: (kt[s], j))
    rs_spec = pl.BlockSpec((8, blk), lambda j, s, qt, kt: (j, kt[s]))
    cs_spec = pl.BlockSpec((blk, LANES), lambda j, s, qt, kt: (kt[s], j))

    def body(qt_ref, kt_ref, *refs):
        if mla:
            (q_ref, k_ref, v_ref, do_ref, lse_ref, delta_ref, dq_ref, dk_ref, dv_ref,
             dq_s, dk_s, dv_s, st_s, dpt_s, pt_s, dst_s) = refs
            fq_ref = fk_ref = rs_ref = cs_ref = rs_s = cs_s = None
        else:
            (q_ref, k_ref, v_ref, do_ref, lse_ref, delta_ref, fq_ref, fk_ref, dq_ref, dk_ref, dv_ref, rs_ref, cs_ref,
             dq_s, dk_s, dv_s, st_s, dpt_s, pt_s, dst_s, rs_s, cs_s) = refs
        j, step_id = pl.program_id(0), pl.program_id(1)
        qb, kb = qt_ref[step_id], kt_ref[step_id]
        lo = _iota((1, LANES), 1) < HEAD_V

        @pl.when(step_id == 0)
        def _():
            dq_s[...] = jnp.zeros_like(dq_s)
            if not mla:
                rs_s[...] = jnp.zeros_like(rs_s)

        @pl.when(qb == kb)
        def _():
            dk_s[...] = jnp.zeros_like(dk_s)
            dv_s[...] = jnp.zeros_like(dv_s)
            if not mla:
                cs_s[...] = jnp.zeros_like(cs_s)

        def step(masked):
            do = do_ref[...]
            vv = v_ref[...]
            for e in range(2):
                half = lo if e == 0 else jnp.logical_not(lo)
                qe, ke = _head_qk(q_ref, k_ref, e, mla, lo)
                doe = jnp.where(half, do, jnp.zeros_like(do))
                st_s[...] = lax.dot_general(ke, qe, NT_DIMS, preferred_element_type=F32)
                dpt_s[...] = lax.dot_general(vv, doe, NT_DIMS, preferred_element_type=F32)
                head = 2 * j + e
                lse_row = _select_row(lse_ref[...], e)
                delta_row = _select_row(delta_ref[...], head)
                fq_row = None if mla else _select_row(fq_ref[...], head)

                def chunk(r0, e=e, lse_row=lse_row, delta_row=delta_row, fq_row=fq_row, head=head):
                    rows = pl.ds(r0, ch)
                    s = st_s[rows, :]
                    if not mla:
                        s = s + fq_row - _select_lane(fk_ref[rows, :], head)
                    p = jnp.exp2(s - lse_row)
                    if masked:
                        kpos = kb * blk + r0 + _iota((ch, blk), 0)
                        qpos = qb * blk + _iota((ch, blk), 1)
                        p = jnp.where((kpos <= qpos) & (kpos >= PAD), p, 0.0)
                    ds = p * (dpt_s[rows, :] - delta_row)
                    pt_s[rows, :] = p.astype(pt_s.dtype)
                    dst_s[rows, :] = ds.astype(dst_s.dtype)
                    if not mla:
                        cs_s[e, rows, :] += jnp.sum(ds, axis=1, keepdims=True)
                        rs_s[qb, e] += jnp.sum(ds, axis=0, keepdims=True)

                _for_chunks(blk // ch, ch, chunk)
                dv_s[e] += jnp.dot(pt_s[...], do, preferred_element_type=F32)
                q_src = qe if mla else q_ref[...]
                dk_s[e] += jnp.dot(dst_s[...], q_src, preferred_element_type=F32)
                dq_s[qb, e] += lax.dot_general(dst_s[...], ke, TN_DIMS, preferred_element_type=F32)

        needs_mask = (qb == kb) | (kb == 0)

        @pl.when(needs_mask)
        def _():
            step(True)

        @pl.when(jnp.logical_not(needs_mask))
        def _():
            step(False)

        @pl.when(qb == last)
        def _():
            dv_ref[...] = jnp.where(lo, dv_s[0], dv_s[1])
            if mla:
                dk_ref[:, 0:LANES] = dk_s[0] * LN2
                dk_ref[:, LANES:2 * LANES] = dk_s[1] * LN2
            else:
                dk_ref[...] = jnp.where(lo, dk_s[0], dk_s[1]) * LN2
                lane = _iota((1, LANES), 1)
                cs_ref[...] = jnp.where(lane == 0, cs_s[0], jnp.where(lane == 1, cs_s[1], 0.0))

        @pl.when(qb == kb)
        def _():
            if mla:
                dq_ref[:, 0:LANES] = dq_s[qb, 0] * LN2
                dq_ref[:, LANES:2 * LANES] = dq_s[qb, 1] * LN2
            else:
                dq_ref[...] = jnp.where(lo, dq_s[qb, 0], dq_s[qb, 1]) * LN2
                sub = _iota((8, 1), 0)
                rs_ref[...] = jnp.where(sub == 0, rs_s[qb, 0], jnp.where(sub == 1, rs_s[qb, 1], 0.0))

    in_specs = [q_spec, k_spec, v_spec, do_spec, lse_spec, delta_spec] + ([] if mla else [fq_spec, fk_spec])
    args = (q, k, v, do, lse_t, delta_t) + (() if mla else (f_t, f))
    hv = HEADS * HEAD_V
    out_specs = [dq_spec, k_spec, v_spec]
    out_shape = [jax.ShapeDtypeStruct((t, pairs * w), F32), jax.ShapeDtypeStruct((t, pairs * w), F32),
                 jax.ShapeDtypeStruct((t, hv), F32)]
    scratch = [pltpu.VMEM((nb, 2, blk, LANES), F32), pltpu.VMEM((2, blk, LANES), F32), pltpu.VMEM((2, blk, LANES), F32),
               pltpu.VMEM((blk, blk), F32), pltpu.VMEM((blk, blk), F32), pltpu.VMEM((blk, blk), MXU_DTYPE),
               pltpu.VMEM((blk, blk), MXU_DTYPE)]
    if not mla:
        out_specs += [rs_spec, cs_spec]
        out_shape += [jax.ShapeDtypeStruct((pairs * 8, t), F32), jax.ShapeDtypeStruct((t, hv), F32)]
        scratch += [pltpu.VMEM((nb, 2, 1, blk), F32), pltpu.VMEM((2, blk, 1), F32)]
    grid_spec = pltpu.PrefetchScalarGridSpec(
        num_scalar_prefetch=2, grid=(pairs, len(q_of)), in_specs=in_specs, out_specs=out_specs, scratch_shapes=scratch)
    return pl.pallas_call(
        body, name=name, grid_spec=grid_spec, out_shape=out_shape,
        compiler_params=_params("parallel", "arbitrary"),
    )(jnp.asarray(q_of), jnp.asarray(k_of), *args)


def _shift_down(x, halo, n):
    rows = x.shape[0]
    r = _iota((rows, 1), 0)
    out = pltpu.roll(x, n, axis=0)
    for s in range(n):
        out = jnp.where(r == s, halo[8 - n + s:8 - n + s + 1, :], out)
    return out


def _shift_up(x, halo, n):
    rows = x.shape[0]
    r = _iota((rows, 1), 0)
    out = pltpu.roll(x, rows - n, axis=0)
    for s in range(n):
        out = jnp.where(r == rows - n + s, halo[s:s + 1, :], out)
    return out


def _conv_specs(bt, nblk):
    d = D_MODEL
    per8 = bt // 8
    z_spec = pl.BlockSpec((bt, 3 * d), lambda i: (i, 0))
    prev_spec = pl.BlockSpec((8, 3 * d), lambda i: (jnp.maximum(i * per8 - 1, 0), 0))
    next_z = pl.BlockSpec((8, 3 * d), lambda i: (jnp.minimum((i + 1) * per8, nblk * per8 - 1), 0))
    next_d = pl.BlockSpec((8, d), lambda i: (jnp.minimum((i + 1) * per8, nblk * per8 - 1), 0))
    w_spec = pl.BlockSpec((8, d), lambda i: (0, 0))
    row_spec = pl.BlockSpec((bt, d), lambda i: (i, 0))
    return z_spec, prev_spec, next_z, next_d, w_spec, row_spec


def _conv_taps(z_ref, prev_ref, i):
    d = D_MODEL
    g = z_ref[:, d:2 * d] * z_ref[:, 2 * d:3 * d]
    gh = jnp.where(i > 0, prev_ref[:, d:2 * d] * prev_ref[:, 2 * d:3 * d], 0.0)
    return g, _shift_down(g, gh, 1), _shift_down(g, gh, 2)


def _conv_fwd(z, conv_w8):
    t = z.shape[0]
    bt = min(PREP_TILE, t)
    nblk = t // bt
    d = D_MODEL
    z_spec, prev_spec, _, _, w_spec, row_spec = _conv_specs(bt, nblk)

    def body(z_ref, prev_ref, w_ref, v_ref):
        g, g1, g2 = _conv_taps(z_ref, prev_ref, pl.program_id(0))
        y = w_ref[0:1, :] * g2 + w_ref[1:2, :] * g1 + w_ref[2:3, :] * g
        v_ref[...] = (z_ref[:, 0:d] * y).astype(v_ref.dtype)

    return pl.pallas_call(
        body, name="conv_fwd", grid=(nblk,),
        in_specs=[z_spec, prev_spec, w_spec], out_specs=row_spec,
        out_shape=jax.ShapeDtypeStruct((t, d), MXU_DTYPE),
        compiler_params=_params("parallel"),
    )(z, z, conv_w8)


def _conv_bwd(z, conv_w8, dv):
    t = z.shape[0]
    bt = min(PREP_TILE, t)
    nblk = t // bt
    d = D_MODEL
    z_spec, prev_spec, next_z, next_d, w_spec, row_spec = _conv_specs(bt, nblk)

    def body(z_ref, prev_ref, nz_ref, dv_ref, ndv_ref, w_ref, dz_ref, dw_ref):
        i = pl.program_id(0)

        @pl.when(i == 0)
        def _():
            dw_ref[...] = jnp.zeros_like(dw_ref)

        g, g1, g2 = _conv_taps(z_ref, prev_ref, i)
        w0, w1, w2 = w_ref[0:1, :], w_ref[1:2, :], w_ref[2:3, :]
        y = w0 * g2 + w1 * g1 + w2 * g
        dvv = dv_ref[...].astype(F32)
        gate_b = z_ref[:, 0:d]
        dy = dvv * gate_b
        dyn = jnp.where(i < nblk - 1, ndv_ref[...].astype(F32) * nz_ref[:, 0:d], 0.0)
        dg = w2 * dy + w1 * _shift_up(dy, dyn, 1) + w0 * _shift_up(dy, dyn, 2)
        dz_ref[:, 0:d] = (dvv * y).astype(dz_ref.dtype)
        dz_ref[:, d:2 * d] = (dg * z_ref[:, 2 * d:3 * d]).astype(dz_ref.dtype)
        dz_ref[:, 2 * d:3 * d] = (dg * z_ref[:, d:2 * d]).astype(dz_ref.dtype)
        sub = _iota((8, 1), 0)
        s0 = jnp.sum(dy * g2, axis=0, keepdims=True)
        s1 = jnp.sum(dy * g1, axis=0, keepdims=True)
        s2 = jnp.sum(dy * g, axis=0, keepdims=True)
        dw_ref[...] += jnp.where(sub == 0, s0, jnp.where(sub == 1, s1, jnp.where(sub == 2, s2, 0.0)))

    return pl.pallas_call(
        body, name="conv_bwd", grid=(nblk,),
        in_specs=[z_spec, prev_spec, next_z, row_spec, next_d, w_spec], out_specs=[z_spec, w_spec],
        out_shape=[jax.ShapeDtypeStruct((t, 3 * d), MXU_DTYPE), jax.ShapeDtypeStruct((8, d), F32)],
        compiler_params=_params("arbitrary"),
    )(z, z, z, dv, dv, conv_w8)


def _loss_head(h, target):
    t, d = h.shape
    bt = LOSS_TILE
    assert LANES % bt == 0 or bt == LANES
    off = LANES // bt

    def body(h_ref, y_ref, dh_ref, acc_ref):
        i = pl.program_id(0)

        @pl.when(i == 0)
        def _():
            acc_ref[...] = jnp.zeros_like(acc_ref)

        @pl.when(i < off)
        def _():
            dh_ref[...] = jnp.zeros_like(dh_ref)

        @pl.when(i >= off)
        def _():
            err = h_ref[...] - y_ref[...]
            dh_ref[...] = err / d
            acc_ref[...] += jnp.sum(err * err)

    dh, acc = pl.pallas_call(
        body, name="loss_head", grid=(t // bt,),
        in_specs=[pl.BlockSpec((bt, d), lambda i: (i, 0)), pl.BlockSpec((bt, d), lambda i: (jnp.maximum(i - off, 0), 0))],
        out_specs=[pl.BlockSpec((bt, d), lambda i: (i, 0)), pl.BlockSpec((8, LANES), lambda i: (0, 0))],
        out_shape=[jax.ShapeDtypeStruct((t, d), F32), jax.ShapeDtypeStruct((8, LANES), F32)],
        compiler_params=_params("arbitrary"),
    )(h, target)
    return dh, acc[0, 0] * (0.5 / d)


def _common_tile(rows, row_off, cap=512, align=8):
    for b in range(min(cap, rows) // align * align, 0, -align):
        if rows % b == 0 and row_off % b == 0:
            return b
    raise ValueError((rows, row_off))


def _round_up(n, m):
    return -(-n // m) * m


def _adamw(w, m, v, g_buf, row_off, col_off):
    rows, width = w.shape
    wpad = _round_up(width, LANES)
    assert col_off % wpad == 0
    bt = _common_tile(rows, row_off)

    def body(w_ref, m_ref, v_ref, g_ref, go_ref, d_ref, nm_ref, nv_ref):
        gv = g_ref[...]
        if wpad != width:
            gv = gv[:, :width]
        m_new = ADAM_B1 * m_ref[...] + (1.0 - ADAM_B1) * gv
        v_new = ADAM_B2 * v_ref[...] + (1.0 - ADAM_B2) * jnp.square(gv)
        m_hat = m_new / (1.0 - ADAM_B1 ** ADAM_STEP)
        v_hat = v_new / (1.0 - ADAM_B2 ** ADAM_STEP)
        go_ref[...] = gv
        d_ref[...] = -ADAM_LR * (m_hat / (jnp.sqrt(v_hat) + ADAM_EPS) + ADAM_WD * w_ref[...])
        nm_ref[...] = m_new
        nv_ref[...] = v_new

    spec = pl.BlockSpec((bt, width), lambda i: (i, 0))
    g_spec = pl.BlockSpec((bt, wpad), lambda i: (row_off // bt + i, col_off // wpad))
    return pl.pallas_call(
        body, name="adamw", grid=(rows // bt,),
        in_specs=[spec] * 3 + [g_spec], out_specs=[spec] * 4,
        out_shape=[jax.ShapeDtypeStruct((rows, width), F32)] * 4,
        compiler_params=_params("parallel"),
    )(w, m, v, g_buf)


def _add2(a, b, *, out_dtype, name):
    rows, width = a.shape
    bt = next(x for x in range(min(rows, 640), 0, -16) if rows % x == 0)

    def body(a_ref, b_ref, o_ref):
        o_ref[...] = (a_ref[...] + b_ref[...]).astype(o_ref.dtype)

    spec = pl.BlockSpec((bt, width), lambda i: (i, 0))
    return pl.pallas_call(
        body, name=name, grid=(rows // bt,), in_specs=[spec, spec], out_specs=spec,
        out_shape=jax.ShapeDtypeStruct((rows, width), out_dtype), compiler_params=_params("parallel"),
    )(a, b)


def _sum4(parts, slot, *, name):
    _, rows, width = parts.shape
    bt = next(x for x in range(min(rows, 640), 0, -16) if rows % x == 0)

    def body(slot_ref, p_ref, o_ref):
        p = [p_ref[n].astype(F32) for n in range(4)]
        o_ref[...] = ((p[0] + p[1]) + p[2]) + p[3]

    grid_spec = pltpu.PrefetchScalarGridSpec(
        num_scalar_prefetch=1, grid=(rows // bt,),
        in_specs=[pl.BlockSpec((4, bt, width), lambda i, s: (0, i, 0))],
        out_specs=pl.BlockSpec((None, bt, width), lambda i, s: (s[0], i, 0)))
    return pl.pallas_call(
        body, name=name, grid_spec=grid_spec,
        out_shape=jax.ShapeDtypeStruct((2, rows, width), F32), compiler_params=_params("parallel"),
    )(jnp.reshape(slot, (1,)).astype(jnp.int32), parts)


ANY = pl.BlockSpec(memory_space=pl.ANY)
CHIP_FLIPS = ((1, 0), (0, 1), (1, 1))


def _place():
    return lax.axis_index("x"), lax.axis_index("y"), lax.axis_index("c")


def _flip(v, f):
    return 1 - v if f else v


def _allgather_chips(slabs):
    _, rows, width = slabs.shape
    half = rows // 2

    def body(_, out_ref, send_sems, recv_sems):
        x, y, c = _place()
        me = 2 * x + y
        sibling = (x, y, 1 - c)
        my_rows = pl.ds(pl.multiple_of(c * half, 8), half)
        sib_rows = pl.ds(pl.multiple_of((1 - c) * half, 8), half)
        first, passed = [], []
        for n, (fx, fy) in enumerate(CHIP_FLIPS):
            px, py = _flip(x, fx), _flip(y, fy)
            peer = 2 * px + py
            first.append(pltpu.make_async_remote_copy(
                src_ref=out_ref.at[me, my_rows], dst_ref=out_ref.at[me, my_rows],
                send_sem=send_sems.at[n], recv_sem=recv_sems.at[n], device_id=(px, py, c), device_id_type=MESH))
            passed.append(pltpu.make_async_remote_copy(
                src_ref=out_ref.at[peer, my_rows], dst_ref=out_ref.at[peer, my_rows],
                send_sem=send_sems.at[3 + n], recv_sem=recv_sems.at[3 + n], device_id=sibling, device_id_type=MESH))
        for cp in first:
            cp.start()
        for n, (fx, fy) in enumerate(CHIP_FLIPS):
            peer = 2 * _flip(x, fx) + _flip(y, fy)
            pltpu.make_async_remote_copy(
                src_ref=out_ref.at[me, my_rows], dst_ref=out_ref.at[peer, my_rows],
                send_sem=send_sems.at[n], recv_sem=recv_sems.at[n], device_id=sibling, device_id_type=MESH).wait_recv()
            passed[n].start()
        for n, (fx, fy) in enumerate(CHIP_FLIPS):
            peer = 2 * _flip(x, fx) + _flip(y, fy)
            pltpu.make_async_remote_copy(
                src_ref=out_ref.at[me, sib_rows], dst_ref=out_ref.at[peer, sib_rows],
                send_sem=send_sems.at[3 + n], recv_sem=recv_sems.at[3 + n], device_id=sibling,
                device_id_type=MESH).wait_recv()
        for cp in first + passed:
            cp.wait_send()

    return pl.pallas_call(
        body, name="allgather_weights",
        in_specs=[ANY], out_specs=ANY,
        out_shape=jax.ShapeDtypeStruct(slabs.shape, slabs.dtype), input_output_aliases={0: 0},
        scratch_shapes=[pltpu.SemaphoreType.DMA((6,)), pltpu.SemaphoreType.DMA((6,))],
    )(slabs)


def _swap_halves(g):
    _, rows, width = g.shape
    half = rows // 2

    def body(g_ref, got_ref, send_sem, recv_sem):
        x, y, c = _place()
        away = pl.ds(pl.multiple_of((1 - c) * half, 8), half)
        cp = pltpu.make_async_remote_copy(
            src_ref=g_ref.at[:, away], dst_ref=got_ref, send_sem=send_sem, recv_sem=recv_sem,
            device_id=(x, y, 1 - c), device_id_type=MESH)
        cp.start()
        cp.wait()

    return pl.pallas_call(
        body, name="grad_swap_halves",
        in_specs=[ANY], out_specs=ANY,
        out_shape=jax.ShapeDtypeStruct((4, half, width), g.dtype),
        scratch_shapes=[pltpu.SemaphoreType.DMA, pltpu.SemaphoreType.DMA],
    )(g)


def _scatter_chips(s):
    _, rows, width = s.shape

    def body(s_ref, out_ref, send_sems, recv_sems, local_sem):
        x, y, c = _place()
        me = 2 * x + y
        mine = pltpu.make_async_copy(s_ref.at[me], out_ref.at[me], local_sem)
        mine.start()
        copies = []
        for n, (fx, fy) in enumerate(CHIP_FLIPS):
            px, py = _flip(x, fx), _flip(y, fy)
            copies.append(pltpu.make_async_remote_copy(
                src_ref=s_ref.at[2 * px + py], dst_ref=out_ref.at[me],
                send_sem=send_sems.at[n], recv_sem=recv_sems.at[n], device_id=(px, py, c), device_id_type=MESH))
        for cp in copies:
            cp.start()
        for n, (fx, fy) in enumerate(CHIP_FLIPS):
            peer = 2 * _flip(x, fx) + _flip(y, fy)
            pltpu.make_async_remote_copy(
                src_ref=s_ref.at[me], dst_ref=out_ref.at[peer],
                send_sem=send_sems.at[n], recv_sem=recv_sems.at[n], device_id=(x, y, c), device_id_type=MESH).wait_recv()
        for cp in copies:
            cp.wait_send()
        mine.wait()

    return pl.pallas_call(
        body, name="grad_scatter_chips",
        in_specs=[ANY], out_specs=ANY,
        out_shape=jax.ShapeDtypeStruct((4, rows, width), s.dtype),
        scratch_shapes=[pltpu.SemaphoreType.DMA((3,)), pltpu.SemaphoreType.DMA((3,)), pltpu.SemaphoreType.DMA],
    )(s)


def _join_halves(halves):
    def body(_, out_ref, send_sem, recv_sem):
        x, y, c = _place()
        cp = pltpu.make_async_remote_copy(
            src_ref=out_ref.at[c], dst_ref=out_ref.at[c], send_sem=send_sem, recv_sem=recv_sem,
            device_id=(x, y, 1 - c), device_id_type=MESH)
        cp.start()
        pltpu.make_async_remote_copy(
            src_ref=out_ref.at[c], dst_ref=out_ref.at[1 - c], send_sem=send_sem, recv_sem=recv_sem,
            device_id=(x, y, 1 - c), device_id_type=MESH).wait_recv()
        cp.wait_send()

    return pl.pallas_call(
        body, name="grad_join_halves",
        in_specs=[ANY], out_specs=ANY,
        out_shape=jax.ShapeDtypeStruct(halves.shape, halves.dtype), input_output_aliases={0: 0},
        scratch_shapes=[pltpu.SemaphoreType.DMA, pltpu.SemaphoreType.DMA],
    )(halves)


PACK_W = 1024
REPLICATED = ("g_mix", "g_mlp", "g_cq", "g_ckv", "g_q_mla", "g_k_mla", "g_q_fox", "g_k_fox", "b_forget")
WEIGHT_ORDER = ("meta_tokens", "g_mix", "g_mlp", "w_in_attn", "g_cq", "w_uq", "g_ckv", "w_ukv", "g_q_mla", "g_k_mla",
                "g_q_fox", "g_k_fox", "b_forget", "w_out_attn", "w_in_conv", "conv_w", "w_out_conv", "w_mlp_up",
                "w_mlp_down")
N_EVEN = 2
N_ODD = 2
SHARD_IN = ATTN_IN // 4
SHARD_MIX = D_MODEL // 4
SHARD_UQ = HEADS * MLA_QK // 4
SHARD_UKV = HEADS * (MLA_NOPE + HEAD_V) // 4
SHARD_CONV = 3 * D_MODEL // 4
SIDE_W = 256
PK_UP = (0, 0)
PK_DOWN = (4096, 0)
PK_CONV_IN = (8192, 0)
PK_ATTN_IN = (10240, 0)
PK_OUT_ATTN = (12288, 0)
PK_OUT_CONV = (12800, 0)
PK_SMALL = (8192, 768)
PK_UQ = (10240, 768)
PK_UKV = (11008, 768)
PK_ROWS = 13312
SMALL_ROWS = 64
SMALL_META = 0
SMALL_CONV = 16
SMALL_REP = 24
SMALL_BITS_ROWS = 48
MATRIX_PLACES = (("w_mlp_up", PK_UP), ("w_mlp_down", PK_DOWN), ("w_in_conv", PK_CONV_IN), ("w_in_attn", PK_ATTN_IN),
                 ("w_out_attn", PK_OUT_ATTN), ("w_out_conv", PK_OUT_CONV), ("w_uq", PK_UQ), ("w_ukv", PK_UKV))


def _put(buf, x, place, *, name, slab=None):
    row_off, col_off = place
    slabs = x.ndim == 3
    rows, w = x.shape[-2:]
    wpad = _round_up(w, LANES)
    assert col_off % wpad == 0
    bt = _common_tile(rows, row_off, align=16)

    def fill(x_ref, o_ref):
        v = x_ref[...].astype(o_ref.dtype)
        if wpad != w:
            v = jnp.concatenate([v, jnp.zeros((bt, wpad - w), o_ref.dtype)], axis=1)
        o_ref[...] = v

    def body(x_ref, _, o_ref):
        fill(x_ref, o_ref)

    if slab is not None:
        grid_spec = pltpu.PrefetchScalarGridSpec(
            num_scalar_prefetch=1, grid=(rows // bt,),
            in_specs=[pl.BlockSpec((bt, w), lambda i, s: (i, 0)), ANY],
            out_specs=pl.BlockSpec((None, bt, wpad), lambda i, s: (s[0], row_off // bt + i, col_off // wpad)))
        return pl.pallas_call(
            lambda s_ref, x_ref, _, o_ref: fill(x_ref, o_ref), name=name, grid_spec=grid_spec,
            out_shape=jax.ShapeDtypeStruct(buf.shape, buf.dtype), input_output_aliases={2: 0},
            compiler_params=_params("parallel"),
        )(jnp.reshape(slab, (1,)).astype(jnp.int32), x, buf)
    if slabs:
        grid = (4, rows // bt)
        x_spec = pl.BlockSpec((None, bt, w), lambda s, i: (s, i, 0))
        o_spec = pl.BlockSpec((None, bt, wpad), lambda s, i: (s, row_off // bt + i, col_off // wpad))
        sem = ("parallel", "parallel")
    else:
        grid = (rows // bt,)
        x_spec = pl.BlockSpec((bt, w), lambda i: (i, 0))
        o_spec = pl.BlockSpec((bt, wpad), lambda i: (row_off // bt + i, col_off // wpad))
        sem = ("parallel",)
    return pl.pallas_call(
        body, name=name, grid=grid, in_specs=[x_spec, ANY], out_specs=o_spec,
        out_shape=jax.ShapeDtypeStruct(buf.shape, buf.dtype), input_output_aliases={1: 0},
        compiler_params=_params(*sem),
    )(x, buf)


def _w_cols(place, layer, rows, width):
    base = (place[0] + layer * rows) // rows
    return dict(n=4 * width, tn=width, tk=rows, spec=pl.BlockSpec((None, rows, width), lambda i, j, k: (j, base, 0)))


def _w_cols_t(place, layer, rows, width):
    base = (place[0] + layer * rows) // rows
    return dict(n=rows, tn=rows, tk=width, spec=pl.BlockSpec((None, rows, width), lambda i, j, k: (k, base, 0)))


def _w_rows(place, layer, rows):
    base = (place[0] + layer * rows) // rows
    return dict(n=D_MODEL, tn=D_MODEL, tk=rows, spec=pl.BlockSpec((None, rows, D_MODEL), lambda i, j, k: (k, base, 0)))


def _w_rows_t(place, layer, rows):
    base = (place[0] + layer * rows) // rows
    return dict(n=4 * rows, tn=rows, tk=D_MODEL, spec=pl.BlockSpec((None, rows, D_MODEL), lambda i, j, k: (j, base, 0)))


def _g_cols(g, place, layer, rows, width):
    base = (place[0] + layer * rows) // rows
    return g, pl.BlockSpec((None, rows, width), lambda i, j, k: (j, base, 0))


def _g_rows(g, place, layer, rows):
    base = (place[0] + layer * rows) // rows
    return g, pl.BlockSpec((None, rows, D_MODEL), lambda i, j, k: (i, base, 0))


IN_PADW = _round_up(SHARD_IN, LANES)
IN_TAIL = ZA_FQ - SHARD_IN
IN_FL = SHARD_IN - HEADS
ZA_KPE = ZA_LAST + KPE_LANE


def _assemble_attn_in(gathered, layer):
    bt = 256
    base = (PK_ATTN_IN[0] + layer * D_MODEL) // bt
    assert 2 * SHARD_IN > ZA_FQ + MLA_ROPE and 3 * SHARD_IN < ATTN_IN - HEADS

    def body(s0, s1, s2, s3, o_ref):
        dt = o_ref.dtype
        z = lambda n: jnp.zeros((bt, n), dt)
        o_ref[...] = jnp.concatenate(
            [s0[:, :SHARD_IN], s1[:, :IN_TAIL], s1[:, IN_TAIL + MLA_ROPE:SHARD_IN], s2[:, :SHARD_IN], s3[:, :IN_FL],
             s3[:, IN_FL:SHARD_IN], z(KPE_LANE - HEADS), s1[:, IN_TAIL:IN_TAIL + MLA_ROPE],
             z(LANES - KPE_LANE - MLA_ROPE)], axis=1).astype(dt)

    def spec(s):
        return pl.BlockSpec((None, bt, IN_PADW), lambda i: (s, base + i, 0))

    return pl.pallas_call(
        body, name="assemble_attn_in", grid=(D_MODEL // bt,),
        in_specs=[spec(s) for s in range(4)], out_specs=pl.BlockSpec((bt, ZA_W), lambda i: (i, 0)),
        out_shape=jax.ShapeDtypeStruct((D_MODEL, ZA_W), MXU_DTYPE), compiler_params=_params("parallel"),
    )(gathered, gathered, gathered, gathered)


def _scatter_attn_in(g, dwa, layer):
    bt = 256
    base = (PK_ATTN_IN[0] + layer * D_MODEL) // bt
    fq1 = ZA_FQ + SHARD_IN - IN_TAIL - MLA_ROPE

    def body(d_ref, _, o_ref):
        pad = jnp.zeros((bt, IN_PADW - SHARD_IN), F32)
        pieces = (
            (d_ref[:, 0:SHARD_IN],),
            (d_ref[:, SHARD_IN:ZA_FQ], d_ref[:, ZA_KPE:ZA_KPE + MLA_ROPE], d_ref[:, ZA_FQ:fq1]),
            (d_ref[:, fq1:fq1 + SHARD_IN],),
            (d_ref[:, fq1 + SHARD_IN:ZA_LAST], d_ref[:, ZA_LAST:ZA_LAST + HEADS]),
        )
        for s in range(4):
            @pl.when(pl.program_id(0) == s)
            def _(s=s):
                o_ref[...] = jnp.concatenate(list(pieces[s]) + [pad], axis=1)

    return pl.pallas_call(
        body, name="scatter_attn_in", grid=(4, D_MODEL // bt),
        in_specs=[pl.BlockSpec((bt, ZA_W), lambda s, i: (i, 0)), ANY],
        out_specs=pl.BlockSpec((None, bt, IN_PADW), lambda s, i: (s, base + i, 0)),
        out_shape=jax.ShapeDtypeStruct(g.shape, g.dtype), input_output_aliases={1: 0},
        compiler_params=_params("parallel", "parallel"),
    )(dwa, g)


def _assemble_uq(gathered, layer):
    bt = 128
    base = (PK_UQ[0] + layer * Q_LORA) // bt
    col = PK_UQ[1] // SIDE_W

    def body(s0, s1, s2, s3, o_ref):
        dt = o_ref.dtype
        z = jnp.zeros((bt, LANES - MLA_QK), dt)
        parts = []
        for s_ref in (s0, s1, s2, s3):
            parts += [s_ref[:, 0:MLA_QK], z, s_ref[:, MLA_QK:2 * MLA_QK], z]
        o_ref[...] = jnp.concatenate(parts, axis=1).astype(dt)

    def spec(s):
        return pl.BlockSpec((None, bt, SIDE_W), lambda i: (s, base + i, col))

    return pl.pallas_call(
        body, name="assemble_uq", grid=(Q_LORA // bt,),
        in_specs=[spec(s) for s in range(4)], out_specs=pl.BlockSpec((bt, HEADS * LANES), lambda i: (i, 0)),
        out_shape=jax.ShapeDtypeStruct((Q_LORA, HEADS * LANES), MXU_DTYPE), compiler_params=_params("parallel"),
    )(gathered, gathered, gathered, gathered)


def _scatter_uq(g, dw, layer):
    bt = 128
    base = (PK_UQ[0] + layer * Q_LORA) // bt
    col = PK_UQ[1] // SIDE_W

    def body(d_ref, _, o_ref):
        o_ref[...] = jnp.concatenate([d_ref[:, 0:MLA_QK], d_ref[:, LANES:LANES + MLA_QK],
                                      jnp.zeros((bt, SIDE_W - 2 * MLA_QK), F32)], axis=1)

    return pl.pallas_call(
        body, name="scatter_uq", grid=(4, Q_LORA // bt),
        in_specs=[pl.BlockSpec((bt, 2 * LANES), lambda s, i: (i, s)), ANY],
        out_specs=pl.BlockSpec((None, bt, SIDE_W), lambda s, i: (s, base + i, col)),
        out_shape=jax.ShapeDtypeStruct(g.shape, g.dtype), input_output_aliases={1: 0},
        compiler_params=_params("parallel", "parallel"),
    )(dw, g)


def _assemble_ukv(gathered, layer):
    bt = KV_LORA
    base = (PK_UKV[0] + layer * KV_LORA) // bt
    col = PK_UKV[1] // SIDE_W
    hd = MLA_NOPE + HEAD_V

    def body(s0, s1, s2, s3, o_ref):
        dt = o_ref.dtype
        z = jnp.zeros((bt, LANES - MLA_NOPE), dt)
        keys, vals = [], []
        for s_ref in (s0, s1, s2, s3):
            for e in range(2):
                keys += [s_ref[:, e * hd:e * hd + MLA_NOPE], z]
                vals.append(s_ref[:, e * hd + MLA_NOPE:(e + 1) * hd])
        o_ref[...] = jnp.concatenate(keys + vals, axis=1).astype(dt)

    def spec(s):
        return pl.BlockSpec((None, bt, SIDE_W), lambda i: (s, base + i, col))

    return pl.pallas_call(
        body, name="assemble_ukv", grid=(1,),
        in_specs=[spec(s) for s in range(4)],
        out_specs=pl.BlockSpec((bt, HEADS * (LANES + HEAD_V)), lambda i: (i, 0)),
        out_shape=jax.ShapeDtypeStruct((KV_LORA, HEADS * (LANES + HEAD_V)), MXU_DTYPE), compiler_params=_params("parallel"),
    )(gathered, gathered, gathered, gathered)


def _scatter_ukv(g, dw, layer):
    bt = KV_LORA
    base = (PK_UKV[0] + layer * KV_LORA) // bt
    col = PK_UKV[1] // SIDE_W

    def body(k_ref, v_ref, _, o_ref):
        o_ref[...] = jnp.concatenate([k_ref[:, 0:MLA_NOPE], v_ref[:, 0:HEAD_V], k_ref[:, LANES:LANES + MLA_NOPE],
                                      v_ref[:, HEAD_V:2 * HEAD_V]], axis=1)

    return pl.pallas_call(
        body, name="scatter_ukv", grid=(4,),
        in_specs=[pl.BlockSpec((bt, 2 * LANES), lambda s: (0, s)),
                  pl.BlockSpec((bt, 2 * HEAD_V), lambda s: (0, HEADS * LANES // (2 * HEAD_V) + s)), ANY],
        out_specs=pl.BlockSpec((None, bt, SIDE_W), lambda s: (s, base, col)),
        out_shape=jax.ShapeDtypeStruct(g.shape, g.dtype), input_output_aliases={2: 0},
        compiler_params=_params("parallel"),
    )(dw, dw, g)


def _pad_lanes(v, n=LANES):
    return jnp.pad(v, (0, n - v.shape[0])).reshape(1, n)


def _relu2_up(acc):
    r = jnp.maximum(acc, 0.0)
    return acc, r * r


def _relu2_bwd(acc, u):
    return (acc * (2.0 * jnp.maximum(u, 0.0)),)


def _add_res(acc, res):
    return (acc + res,)


def _local_step(x, target, meta, small, gathered):
    seq = x.shape[0]
    t = seq + LANES
    d = D_MODEL
    h = jnp.concatenate([jnp.zeros((PAD, d), F32), meta.astype(F32), x], axis=0)
    c_tab, s_tab = _rope_tables(t)
    scale_mla, scale_fox = MLA_QK ** -0.5 * LOG2E, FOX_DIM ** -0.5 * LOG2E
    grads = {}
    saved = []
    g = jnp.zeros((4, PK_ROWS, PACK_W), F32)

    for layer in range(DEPTH):
        j = layer // 2
        sv = {"h_in": h}
        hn = _rmsnorm_fwd(h, small["g_mix"][layer])
        sv["hn"] = hn
        if layer % 2 == 0:
            w_in = _assemble_attn_in(gathered, j)
            w_uq = _assemble_uq(gathered, j)
            w_ukv = _assemble_ukv(gathered, j)
            out_place = PK_OUT_ATTN
            vecs = dict(
                g_cq=small["g_cq"][j].reshape(1, Q_LORA), g_ckv=small["g_ckv"][j].reshape(1, KV_LORA),
                g_qf=jnp.tile(small["g_q_fox"][j] * scale_fox, 2).reshape(1, LANES),
                g_kf=jnp.tile(small["g_k_fox"][j], 2).reshape(1, LANES),
                b_f=_pad_lanes(small["b_forget"][j]), g_q=_pad_lanes(small["g_q_mla"][j] * scale_mla),
                g_k=_pad_lanes(small["g_k_mla"][j]))
            z = _matmul(hn, w_in, name="mm_attn_in")
            cqn, ckvn, qf, kf, vf, logf = _prep_a_fwd(z, vecs["g_cq"], vecs["g_ckv"], vecs["g_qf"], vecs["g_kf"], vecs["b_f"])
            f_cum, f_cum_t = _cumsum_rows(logf, reverse=False, name="cumsum_fwd", out_scale=LOG2E)
            q_raw = _matmul(cqn, w_uq, name="mm_uq")
            kv_raw = _matmul(ckvn, w_ukv, name="mm_ukv")
            qn, kn, v_mla = _prep_b_fwd(q_raw, kv_raw, z, c_tab, s_tab, vecs["g_q"], vecs["g_k"])
            o_mla, lse_mla = _flash_fwd_chunked(qn, kn, v_mla, None, None, mla=True, name="flash_fwd_mla")
            o_fox, lse_fox = _flash_fwd_chunked(qf, kf, vf, f_cum, f_cum_t, mla=False, name="flash_fwd_fox")
            o = jnp.concatenate([o_mla, o_fox], axis=1)
            h = _matmul(o, gathered, b_tiles=_w_rows(out_place, j, SHARD_MIX), extras=(h,), epilogue=_add_res,
                        name="mm_mix_out")
            sv.update(w_in=w_in, w_uq=w_uq, w_ukv=w_ukv, out_place=out_place, vecs=vecs, z=z, cqn=cqn, ckvn=ckvn, qf=qf, kf=kf,
                      vf=vf, f_cum=f_cum, f_cum_t=f_cum_t, q_raw=q_raw, kv_raw=kv_raw, qn=qn, kn=kn, v_mla=v_mla, o=o,
                      lse_mla=lse_mla, lse_fox=lse_fox)
        else:
            out_place = PK_OUT_CONV
            conv_w8 = jnp.pad(small["conv_w"][j], ((0, 5), (0, 0)))
            z = _matmul(hn, gathered, b_tiles=_w_cols(PK_CONV_IN, j, d, SHARD_CONV), name="mm_conv_in")
            vmix = _conv_fwd(z, conv_w8)
            h = _matmul(vmix, gathered, b_tiles=_w_rows(out_place, j, SHARD_MIX), extras=(h,), epilogue=_add_res,
                        name="mm_mix_out")
            sv.update(out_place=out_place, conv_w8=conv_w8, z=z, vmix=vmix)
        sv["h_mid"] = h
        hn2 = _rmsnorm_fwd(h, small["g_mlp"][layer])
        u, a = _matmul(hn2, gathered, b_tiles=_w_cols(PK_UP, layer, d, d), epilogue=_relu2_up,
                       out_dtypes=(F32, MXU_DTYPE), name="mm_mlp_up")
        h = _matmul(a, gathered, b_tiles=_w_rows(PK_DOWN, layer, d), extras=(h,), epilogue=_add_res, name="mm_mlp_down")
        sv.update(hn2=hn2, u=u, a=a)
        saved.append(sv)

    dh, loss_local = _loss_head(h, target)

    dg_mix, dg_mlp = [None] * DEPTH, [None] * DEPTH
    per_even = {k: [None, None] for k in ("g_cq", "g_ckv", "g_q_mla", "g_k_mla", "g_q_fox", "g_k_fox", "b_forget")}
    per_odd = {"conv_w": [None, None]}
    for layer in reversed(range(DEPTH)):
        j = layer // 2
        sv = saved[layer]
        du = _matmul(dh, gathered, tb=True, b_tiles=_w_rows_t(PK_DOWN, layer, d), extras=(sv["u"],),
                     epilogue=_relu2_bwd, out_dtypes=(MXU_DTYPE,), name="mm_mlp_da")
        g = _matmul(sv["a"], dh, ta=True, out_into=_g_rows(g, PK_DOWN, layer, d), name="mm_dw_down")
        g = _matmul(sv["hn2"], du, ta=True, out_into=_g_cols(g, PK_UP, layer, d, d), name="mm_dw_up")
        dhn2 = _matmul(du, gathered, tb=True, b_tiles=_w_cols_t(PK_UP, layer, d, d), name="mm_mlp_dhn")
        dh, dg_mlp[layer] = _rmsnorm_bwd(sv["h_mid"], small["g_mlp"][layer], dhn2, dh)
        do = _matmul(dh, gathered, tb=True, b_tiles=_w_rows_t(sv["out_place"], j, SHARD_MIX), out_dtypes=(MXU_DTYPE,),
                     name="mm_mix_do")
        if layer % 2 == 0:
            vecs = sv["vecs"]
            g = _matmul(sv["o"], dh, ta=True, tm=SHARD_MIX, out_into=_g_rows(g, PK_OUT_ATTN, j, SHARD_MIX),
                        name="mm_dw_out")
            delta_t = _delta_rows(do, sv["o"])
            dqn, dkn, dv_mla = _flash_bwd_fused(sv["qn"], sv["kn"], sv["v_mla"], do, sv["lse_mla"], delta_t, None, None,
                                                mla=True, col0=0, name="flash_bwd_mla")
            dqf, dkf, dvf, rs_t, cs = _flash_bwd_fused(sv["qf"], sv["kf"], sv["vf"], do, sv["lse_fox"], delta_t,
                                                       sv["f_cum"], sv["f_cum_t"], mla=False, col0=HEADS // 2,
                                                       name="flash_bwd_fox")
            d_f = rs_t.reshape(HEADS // 2, 8, t)[:, :2, :].reshape(HEADS, t).T
            d_f = d_f - cs.reshape(t, HEADS // 2, LANES)[:, :, :2].reshape(t, HEADS)
            d_f = jnp.pad(d_f, ((0, 0), (0, LANES - HEADS)))
            dlogf, _ = _cumsum_rows(d_f, reverse=True, name="cumsum_bwd")
            dq_raw, dkv_raw, dkpe, dg_q, dg_k = _prep_b_bwd(sv["q_raw"], sv["kv_raw"], sv["z"], c_tab, s_tab, vecs["g_q"],
                                                            vecs["g_k"], dqn, dkn, dv_mla)
            g = _scatter_uq(g, _matmul(sv["cqn"], dq_raw, ta=True, name="mm_dw_uq"), j)
            g = _scatter_ukv(g, _matmul(sv["ckvn"], dkv_raw, ta=True, name="mm_dw_ukv"), j)
            dcqn = _matmul(dq_raw, sv["w_uq"], tb=True, name="mm_dcqn")
            dckvn = _matmul(dkv_raw, sv["w_ukv"], tb=True, name="mm_dckvn")
            dz, dg_cq, dg_ckv, dg_qf, dg_kf, db_f = _prep_a_bwd(
                sv["z"], vecs["g_cq"], vecs["g_ckv"], vecs["g_qf"], vecs["g_kf"], vecs["b_f"], dcqn, dckvn, dqf, dkf, dvf,
                dlogf, dkpe)
            g = _scatter_attn_in(g, _matmul(sv["hn"], dz, ta=True, name="mm_dw_attn_in"), j)
            per_even["g_cq"][j] = dg_cq[0]
            per_even["g_ckv"][j] = dg_ckv[0]
            per_even["g_q_mla"][j] = dg_q[0, :MLA_QK] * scale_mla
            per_even["g_k_mla"][j] = dg_k[0, :MLA_QK]
            per_even["g_q_fox"][j] = (dg_qf[0, :FOX_DIM] + dg_qf[0, FOX_DIM:]) * scale_fox
            per_even["g_k_fox"][j] = dg_kf[0, :FOX_DIM] + dg_kf[0, FOX_DIM:]
            per_even["b_forget"][j] = db_f[0, :HEADS]
            dhn = _matmul(dz, sv["w_in"], tb=True, name="mm_attn_dhn")
        else:
            g = _matmul(sv["vmix"], dh, ta=True, tm=SHARD_MIX, out_into=_g_rows(g, PK_OUT_CONV, j, SHARD_MIX),
                        name="mm_dw_out")
            dz, dcw = _conv_bwd(sv["z"], sv["conv_w8"], do)
            per_odd["conv_w"][j] = dcw[:3]
            g = _matmul(sv["hn"], dz, ta=True, tn=SHARD_CONV, out_into=_g_cols(g, PK_CONV_IN, j, d, SHARD_CONV),
                        name="mm_dw_conv_in")
            dhn = _matmul(dz, gathered, tb=True, b_tiles=_w_cols_t(PK_CONV_IN, j, d, SHARD_CONV), name="mm_conv_dhn")
        dh, dg_mix[layer] = _rmsnorm_bwd(sv["h_in"], small["g_mix"][layer], dhn, dh)

    grads["meta_tokens"] = dh[PAD:LANES]
    grads["g_mix"] = jnp.stack(dg_mix)
    grads["g_mlp"] = jnp.stack(dg_mlp)
    for k, v in list(per_even.items()) + list(per_odd.items()):
        grads[k] = jnp.stack(v)
    return loss_local, dh[LANES:], g, grads


def kernel(x, meta_tokens, g_mix, g_mlp, w_in_attn, g_cq, w_uq, g_ckv, w_ukv, g_q_mla, g_k_mla, g_q_fox, g_k_fox, b_forget, w_out_attn, w_in_conv, conv_w, w_out_conv, w_mlp_up, w_mlp_down, loss_target, m_meta_tokens, m_g_mix, m_g_mlp, m_w_in_attn, m_g_cq, m_w_uq, m_g_ckv, m_w_ukv, m_g_q_mla, m_g_k_mla, m_g_q_fox, m_g_k_fox, m_b_forget, m_w_out_attn, m_w_in_conv, m_conv_w, m_w_out_conv, m_w_mlp_up, m_w_mlp_down, v_meta_tokens, v_g_mix, v_g_mlp, v_w_in_attn, v_g_cq, v_w_uq, v_g_ckv, v_w_ukv, v_g_q_mla, v_g_k_mla, v_g_q_fox, v_g_k_fox, v_b_forget, v_w_out_attn, v_w_in_conv, v_conv_w, v_w_out_conv, v_w_mlp_up, v_w_mlp_down):
    args = dict(locals())
    weights = {n: args[n] for n in WEIGHT_ORDER}
    mom_m = {n: args["m_" + n] for n in WEIGHT_ORDER}
    mom_v = {n: args["v_" + n] for n in WEIGHT_ORDER}

    wire = jnp.bfloat16
    me = 2 * lax.axis_index("x") + lax.axis_index("y")
    buf = jnp.zeros((4, PK_ROWS, PACK_W), wire)
    for name, place in MATRIX_PLACES:
        w = weights[name]
        buf = _put(buf, w.reshape(-1, w.shape[-1]), place, name="pack_weights", slab=me)
    meta_bits = lax.bitcast_convert_type(meta_tokens, wire).reshape(2 * N_META, SIDE_W)
    conv_bits = lax.bitcast_convert_type(conv_w, wire).reshape(2 * N_ODD * 3, SIDE_W)
    bits = jnp.concatenate([meta_bits, conv_bits, jnp.zeros((SMALL_BITS_ROWS - 2 * N_META - 2 * N_ODD * 3, SIDE_W), wire)])
    buf = _put(buf, bits, PK_SMALL, name="pack_weights", slab=me)
    gathered = _allgather_chips(buf)
    got_bits = gathered[:, PK_SMALL[0]:PK_SMALL[0] + SMALL_BITS_ROWS, PK_SMALL[1]:PK_SMALL[1] + SIDE_W]
    meta_full = lax.bitcast_convert_type(got_bits[:, :2 * N_META].reshape(4, N_META, SIDE_W, 2), F32)
    meta_full = meta_full.transpose(1, 0, 2).reshape(N_META, D_MODEL)
    conv_full = lax.bitcast_convert_type(
        got_bits[:, 2 * N_META:2 * N_META + 2 * N_ODD * 3].reshape(4, N_ODD, 3, SIDE_W, 2), F32)
    small = {n: weights[n] for n in REPLICATED}
    small["conv_w"] = conv_full.transpose(1, 2, 0, 3).reshape(N_ODD, 3, D_MODEL)

    loss_local, grad_x, g, grads = _local_step(x[0], loss_target[0], meta_full, small, gathered)
    loss = lax.psum(loss_local, MESH_AXES)

    rep = jnp.concatenate([grads[n].reshape(-1) for n in REPLICATED])
    rep = jnp.pad(rep, (0, (SMALL_ROWS - SMALL_REP) * SIDE_W - rep.shape[0])).reshape(SMALL_ROWS - SMALL_REP, SIDE_W)
    g_meta = grads["meta_tokens"].reshape(N_META, 4, SIDE_W).transpose(1, 0, 2)
    g_conv = grads["conv_w"].reshape(N_ODD * 3, 4, SIDE_W).transpose(1, 0, 2)
    small4 = jnp.concatenate([g_meta, g_conv, jnp.zeros((4, SMALL_REP - SMALL_CONV - N_ODD * 3, SIDE_W), F32),
                              jnp.broadcast_to(rep[None], (4,) + rep.shape)], axis=1)
    g = _put(g, small4, PK_SMALL, name="pack_small_grads")
    half = PK_ROWS // 2
    c = lax.axis_index("c")
    got = _swap_halves(g)
    kept = lax.dynamic_slice_in_dim(g, c * half, half, axis=1)
    pair = _add2(kept.reshape(4 * half, PACK_W), got.reshape(4 * half, PACK_W), out_dtype=jnp.bfloat16,
                 name="grad_pair_sum").reshape(4, half, PACK_W)
    total = _sum4(_scatter_chips(pair), c, name="grad_chip_sum")
    g_tot = _join_halves(total).reshape(PK_ROWS, PACK_W)

    out = {}
    for name, place in MATRIX_PLACES:
        shape = weights[name].shape
        two_d = lambda a: a.reshape(-1, shape[-1])
        res = _adamw(two_d(weights[name]), two_d(mom_m[name]), two_d(mom_v[name]), g_tot, place[0], place[1])
        out[name] = [r.reshape(shape) for r in res]

    def small_pack(src):
        flat = jnp.concatenate([src[n].reshape(-1) for n in REPLICATED])
        flat = jnp.pad(flat, (0, (SMALL_ROWS - SMALL_REP) * SIDE_W - flat.shape[0])).reshape(SMALL_ROWS - SMALL_REP, SIDE_W)
        return jnp.concatenate([src["meta_tokens"], src["conv_w"].reshape(N_ODD * 3, SIDE_W),
                                jnp.zeros((SMALL_REP - SMALL_CONV - N_ODD * 3, SIDE_W), F32), flat])

    res = _adamw(small_pack(weights), small_pack(mom_m), small_pack(mom_v), g_tot, PK_SMALL[0], PK_SMALL[1])
    for name in ("meta_tokens", "conv_w") + REPLICATED:
        out[name] = []
    for r in res:
        out["meta_tokens"].append(r[SMALL_META:SMALL_META + N_META])
        out["conv_w"].append(r[SMALL_CONV:SMALL_CONV + N_ODD * 3].reshape(N_ODD, 3, SIDE_W))
        flat, off = r[SMALL_REP:].reshape(-1), 0
        for name in REPLICATED:
            n = weights[name].size
            out[name].append(flat[off:off + n].reshape(weights[name].shape))
            off += n
    return (loss, grad_x[None], *[out[n][0] for n in WEIGHT_ORDER], *[out[n][1] for n in WEIGHT_ORDER],
            *[out[n][2] for n in WEIGHT_ORDER], *[out[n][3] for n in WEIGHT_ORDER])
```

```python
import functools

import jax
import jax.numpy as jnp
import numpy as np
from jax import lax
from jax.experimental import pallas as pl
from jax.experimental.pallas import tpu as pltpu

F32 = jnp.float32
MXU_DTYPE = jnp.bfloat16

D_MODEL = 1024
N_META = 16
LANES = 128
PAD = LANES - N_META
HEADS = 8
Q_LORA = 384
KV_LORA = 256
MLA_NOPE = 64
MLA_ROPE = 32
MLA_QK = MLA_NOPE + MLA_ROPE
HEAD_V = 64
FOX_DIM = 64
ROPE_BASE = 10000.0
D_FF = 4 * D_MODEL
DEPTH = 4
EPS = 1e-6
NEG = -1e30
ATTN_IN = Q_LORA + KV_LORA + MLA_ROPE + 3 * HEADS * FOX_DIM + HEADS

ZA_CQ = 0
ZA_CKV = Q_LORA
ZA_FQ = ZA_CKV + KV_LORA
ZA_FK = ZA_FQ + HEADS * FOX_DIM
ZA_FV = ZA_FK + HEADS * FOX_DIM
ZA_LAST = ZA_FV + HEADS * FOX_DIM
ZA_W = ZA_LAST + LANES
KPE_LANE = MLA_NOPE

ADAM_LR = 0.001
ADAM_B1 = 0.9
ADAM_B2 = 0.999
ADAM_EPS = 1e-08
ADAM_WD = 0.01
ADAM_STEP = 10

VMEM_LIMIT_BYTES = 52 * 1024 * 1024
ROW_TILE = 1040
PREP_TILE = 320
ATTN_BLOCK = 640
LOSS_TILE = 128
MAX_TILE = 1536

MESH_AXES = ("x", "y", "c")
MESH = pl.DeviceIdType.MESH


def _params(*sem):
    return pltpu.CompilerParams(dimension_semantics=sem, vmem_limit_bytes=VMEM_LIMIT_BYTES)


def _tile(n, cap=None):
    cap = MAX_TILE if cap is None else cap
    if n <= cap:
        return n
    best = None
    for t in range(LANES, cap + 1, LANES):
        if n % t == 0:
            best = t
    assert best is not None, n
    return best


def _iota(shape, dim):
    return lax.broadcasted_iota(jnp.int32, shape, dim)


def _matmul(a, b, *, ta=False, tb=False, extras=(), epilogue=None, out_dtypes=(F32,), name, b_tiles=None,
            out_into=None, tm=None, tn=None):
    if ta:
        kdim, m = a.shape
    else:
        m, kdim = a.shape
    row_tile = min(ROW_TILE, m)
    if ta:
        tm_auto, tk = _tile(m), min(ATTN_BLOCK, kdim)
    else:
        tm_auto, tk = (row_tile if m % row_tile == 0 else _tile(m)), _tile(kdim)
    tm = tm_auto if tm is None else tm
    if b_tiles is None:
        n = b.shape[0] if tb else b.shape[1]
        assert (b.shape[1] if tb else b.shape[0]) == kdim, (a.shape, b.shape, ta, tb)
        tn = _tile(n) if tn is None else tn
        b_spec = pl.BlockSpec((tn, tk), lambda i, j, k: (j, k)) if tb else pl.BlockSpec((tk, tn), lambda i, j, k: (k, j))
    else:
        n, tn, tk, b_spec = b_tiles["n"], b_tiles["tn"], b_tiles["tk"], b_tiles["spec"]
    nm, nn, nk = m // tm, n // tn, kdim // tk
    assert nm * tm == m and nn * tn == n and nk * tk == kdim, (m, n, kdim, tm, tn, tk)
    n_ex, n_out = len(extras), len(out_dtypes)
    n_alias = 0 if out_into is None else 1
    assert n_out == 1 or out_into is None
    dims = (((0 if ta else 1,), (1 if tb else 0,)), ((), ()))
    if epilogue is None:
        epilogue = lambda acc: (acc,)

    def body(a_ref, b_ref, *rest):
        ex_refs, out_refs, acc_ref = rest[:n_ex], rest[n_ex + n_alias:n_ex + n_alias + n_out], rest[-1]
        k = pl.program_id(2)

        @pl.when(k == 0)
        def _():
            acc_ref[...] = jnp.zeros_like(acc_ref)

        acc_ref[...] += lax.dot_general(a_ref[...].astype(MXU_DTYPE), b_ref[...].astype(MXU_DTYPE), dims,
                                        preferred_element_type=F32)

        @pl.when(k == nk - 1)
        def _():
            res = epilogue(acc_ref[...], *[e[...] for e in ex_refs])
            for o_ref, r in zip(out_refs, res):
                o_ref[...] = r.astype(o_ref.dtype)

    a_spec = pl.BlockSpec((tk, tm), lambda i, j, k: (k, i)) if ta else pl.BlockSpec((tm, tk), lambda i, j, k: (i, k))
    mn_spec = pl.BlockSpec((tm, tn), lambda i, j, k: (i, j))
    row_spec = pl.BlockSpec((1, tn), lambda i, j, k: (0, j))
    ex_specs = [row_spec if e.shape[0] == 1 else mn_spec for e in extras]
    if out_into is None:
        outs = pl.pallas_call(
            body, name=name, grid=(nm, nn, nk),
            in_specs=[a_spec, b_spec] + ex_specs,
            out_specs=[mn_spec] * n_out,
            out_shape=[jax.ShapeDtypeStruct((m, n), dt) for dt in out_dtypes],
            scratch_shapes=[pltpu.VMEM((tm, tn), F32)],
            compiler_params=_params("parallel", "parallel", "arbitrary"),
        )(a, b, *extras)
        return outs[0] if n_out == 1 else outs
    buf, buf_spec = out_into
    return pl.pallas_call(
        body, name=name, grid=(nm, nn, nk),
        in_specs=[a_spec, b_spec] + ex_specs + [ANY],
        out_specs=buf_spec,
        out_shape=jax.ShapeDtypeStruct(buf.shape, buf.dtype),
        input_output_aliases={2 + n_ex: 0},
        scratch_shapes=[pltpu.VMEM((tm, tn), F32)],
        compiler_params=_params("parallel", "parallel", "arbitrary"),
    )(a, b, *extras, buf)


def _rmsnorm_fwd(x, g, *, name="rmsnorm_fwd"):
    t, d = x.shape
    bt = min(ROW_TILE, t)

    def body(x_ref, g_ref, o_ref):
        xv = x_ref[...]
        r = lax.rsqrt(jnp.mean(xv * xv, axis=-1, keepdims=True) + EPS)
        o_ref[...] = (xv * r * g_ref[...]).astype(o_ref.dtype)

    return pl.pallas_call(
        body, name=name, grid=(t // bt,),
        in_specs=[pl.BlockSpec((bt, d), lambda i: (i, 0)), pl.BlockSpec((1, d), lambda i: (0, 0))],
        out_specs=pl.BlockSpec((bt, d), lambda i: (i, 0)),
        out_shape=jax.ShapeDtypeStruct((t, d), MXU_DTYPE),
        compiler_params=_params("parallel"),
    )(x, g.reshape(1, d))


def _rmsnorm_bwd(x, g, dy, dres, *, name="rmsnorm_bwd"):
    t, d = x.shape
    bt = min(ROW_TILE, t)

    def body(x_ref, g_ref, dy_ref, dres_ref, dx_ref, dg_ref):
        @pl.when(pl.program_id(0) == 0)
        def _():
            dg_ref[...] = jnp.zeros_like(dg_ref)

        xv, dyv = x_ref[...], dy_ref[...].astype(F32)
        r = lax.rsqrt(jnp.mean(xv * xv, axis=-1, keepdims=True) + EPS)
        xhat = xv * r
        dxhat = dyv * g_ref[...]
        dx = r * (dxhat - xhat * jnp.mean(dxhat * xhat, axis=-1, keepdims=True))
        dx_ref[...] = dres_ref[...] + dx
        dg_ref[...] += jnp.sum(dyv * xhat, axis=0, keepdims=True)

    row = pl.BlockSpec((bt, d), lambda i: (i, 0))
    vec = pl.BlockSpec((1, d), lambda i: (0, 0))
    dx, dg = pl.pallas_call(
        body, name=name, grid=(t // bt,),
        in_specs=[row, vec, row, row], out_specs=[row, vec],
        out_shape=[jax.ShapeDtypeStruct((t, d), F32), jax.ShapeDtypeStruct((1, d), F32)],
        compiler_params=_params("arbitrary"),
    )(x, g.reshape(1, d), dy, dres)
    return dx, dg.reshape(d)


def _pair_rms(x, lo):
    x2 = x * x
    s_lo = jnp.sum(jnp.where(lo, x2, 0.0), axis=-1, keepdims=True)
    s_hi = jnp.sum(jnp.where(lo, 0.0, x2), axis=-1, keepdims=True)
    return jnp.where(lo, lax.rsqrt(s_lo / FOX_DIM + EPS), lax.rsqrt(s_hi / FOX_DIM + EPS))


def _pair_sum(x, lo):
    s_lo = jnp.sum(jnp.where(lo, x, 0.0), axis=-1, keepdims=True)
    s_hi = jnp.sum(jnp.where(lo, 0.0, x), axis=-1, keepdims=True)
    return jnp.where(lo, s_lo, s_hi)


def _prep_a_fwd(z, g_cq, g_ckv, g_qf, g_kf, b_f):
    t = z.shape[0]
    bt = min(PREP_TILE, t)
    hw = HEADS * FOX_DIM

    def body(z_ref, gcq_ref, gckv_ref, gqf_ref, gkf_ref, bf_ref, cqn_ref, ckvn_ref, qf_ref, kf_ref, vf_ref, logf_ref):
        i = pl.program_id(0)
        cq = z_ref[:, ZA_CQ:ZA_CQ + Q_LORA]
        cqn_ref[...] = (cq * lax.rsqrt(jnp.mean(cq * cq, axis=-1, keepdims=True) + EPS) * gcq_ref[...]).astype(cqn_ref.dtype)
        ckv = z_ref[:, ZA_CKV:ZA_CKV + KV_LORA]
        ckvn_ref[...] = (ckv * lax.rsqrt(jnp.mean(ckv * ckv, axis=-1, keepdims=True) + EPS) * gckv_ref[...]).astype(ckvn_ref.dtype)
        lo = _iota((1, LANES), 1) < FOX_DIM
        for p in range(HEADS // 2):
            sl = slice(p * LANES, (p + 1) * LANES)
            xq = z_ref[:, ZA_FQ + p * LANES:ZA_FQ + (p + 1) * LANES]
            qf_ref[:, sl] = (xq * _pair_rms(xq, lo) * gqf_ref[...]).astype(qf_ref.dtype)
            xk = z_ref[:, ZA_FK + p * LANES:ZA_FK + (p + 1) * LANES]
            kf_ref[:, sl] = (xk * _pair_rms(xk, lo) * gkf_ref[...]).astype(kf_ref.dtype)
        vf_ref[...] = z_ref[:, ZA_FV:ZA_FV + hw].astype(vf_ref.dtype)
        xl = z_ref[:, ZA_LAST:ZA_LAST + LANES] + bf_ref[...]
        logf = jnp.minimum(xl, 0.0) - jnp.log(1.0 + jnp.exp(-jnp.abs(xl)))
        row = i * bt + _iota((bt, LANES), 0)
        lane = _iota((bt, LANES), 1)
        logf_ref[...] = jnp.where((lane < HEADS) & (row >= PAD), logf, 0.0)

    def vec(w):
        return pl.BlockSpec((1, w), lambda i: (0, 0))

    def row(w):
        return pl.BlockSpec((bt, w), lambda i: (i, 0))

    return pl.pallas_call(
        body, name="attn_prep_a_fwd", grid=(t // bt,),
        in_specs=[row(ZA_W), vec(Q_LORA), vec(KV_LORA), vec(LANES), vec(LANES), vec(LANES)],
        out_specs=[row(Q_LORA), row(KV_LORA), row(hw), row(hw), row(hw), row(LANES)],
        out_shape=[jax.ShapeDtypeStruct((t, Q_LORA), MXU_DTYPE), jax.ShapeDtypeStruct((t, KV_LORA), MXU_DTYPE),
                   jax.ShapeDtypeStruct((t, hw), MXU_DTYPE), jax.ShapeDtypeStruct((t, hw), MXU_DTYPE),
                   jax.ShapeDtypeStruct((t, hw), MXU_DTYPE), jax.ShapeDtypeStruct((t, LANES), F32)],
        compiler_params=_params("parallel"),
    )(z, g_cq, g_ckv, g_qf, g_kf, b_f)


def _prep_a_bwd(z, g_cq, g_ckv, g_qf, g_kf, b_f, dcqn, dckvn, dqf, dkf, dvf, dlogf, dkpe):
    t = z.shape[0]
    bt = min(PREP_TILE, t)
    hw = HEADS * FOX_DIM

    def norm_bwd(x, g, dy):
        r = lax.rsqrt(jnp.mean(x * x, axis=-1, keepdims=True) + EPS)
        xhat = x * r
        dxhat = dy * g
        dx = r * (dxhat - xhat * jnp.mean(dxhat * xhat, axis=-1, keepdims=True))
        return dx, jnp.sum(dy * xhat, axis=0, keepdims=True)

    def body(z_ref, gcq_ref, gckv_ref, gqf_ref, gkf_ref, bf_ref, dcqn_ref, dckvn_ref, dqf_ref, dkf_ref, dvf_ref,
             dlogf_ref, dkpe_ref, dz_ref, dgcq_ref, dgckv_ref, dgqf_ref, dgkf_ref, dbf_ref):
        i = pl.program_id(0)

        @pl.when(i == 0)
        def _():
            for r in (dgcq_ref, dgckv_ref, dgqf_ref, dgkf_ref, dbf_ref):
                r[...] = jnp.zeros_like(r)

        dx, dg = norm_bwd(z_ref[:, ZA_CQ:ZA_CQ + Q_LORA], gcq_ref[...], dcqn_ref[...])
        dz_ref[:, ZA_CQ:ZA_CQ + Q_LORA] = dx.astype(dz_ref.dtype)
        dgcq_ref[...] += dg
        dx, dg = norm_bwd(z_ref[:, ZA_CKV:ZA_CKV + KV_LORA], gckv_ref[...], dckvn_ref[...])
        dz_ref[:, ZA_CKV:ZA_CKV + KV_LORA] = dx.astype(dz_ref.dtype)
        dgckv_ref[...] += dg
        lo = _iota((1, LANES), 1) < FOX_DIM
        for base, g_ref, dy_ref, dg_ref in ((ZA_FQ, gqf_ref, dqf_ref, dgqf_ref), (ZA_FK, gkf_ref, dkf_ref, dgkf_ref)):
            for p in range(HEADS // 2):
                x = z_ref[:, base + p * LANES:base + (p + 1) * LANES]
                dy = dy_ref[:, p * LANES:(p + 1) * LANES]
                r = _pair_rms(x, lo)
                xhat = x * r
                dxhat = dy * g_ref[...]
                dx = r * (dxhat - xhat * _pair_sum(dxhat * xhat, lo) / FOX_DIM)
                dz_ref[:, base + p * LANES:base + (p + 1) * LANES] = dx.astype(dz_ref.dtype)
                dg_ref[...] += jnp.sum(dy * xhat, axis=0, keepdims=True)
        dz_ref[:, ZA_FV:ZA_FV + hw] = dvf_ref[...].astype(dz_ref.dtype)
        xl = z_ref[:, ZA_LAST:ZA_LAST + LANES] + bf_ref[...]
        row = i * bt + _iota((bt, LANES), 0)
        lane = _iota((bt, LANES), 1)
        dfl = jnp.where((lane < HEADS) & (row >= PAD), dlogf_ref[...] / (1.0 + jnp.exp(xl)), 0.0)
        dbf_ref[...] += jnp.sum(dfl, axis=0, keepdims=True)
        dz_ref[:, ZA_LAST:ZA_LAST + LANES] = (dfl + dkpe_ref[...]).astype(dz_ref.dtype)

    def vec(w):
        return pl.BlockSpec((1, w), lambda i: (0, 0))

    def row(w):
        return pl.BlockSpec((bt, w), lambda i: (i, 0))

    return pl.pallas_call(
        body, name="attn_prep_a_bwd", grid=(t // bt,),
        in_specs=[row(ZA_W), vec(Q_LORA), vec(KV_LORA), vec(LANES), vec(LANES), vec(LANES),
                  row(Q_LORA), row(KV_LORA), row(hw), row(hw), row(hw), row(LANES), row(LANES)],
        out_specs=[row(ZA_W), vec(Q_LORA), vec(KV_LORA), vec(LANES), vec(LANES), vec(LANES)],
        out_shape=[jax.ShapeDtypeStruct((t, ZA_W), MXU_DTYPE), jax.ShapeDtypeStruct((1, Q_LORA), F32),
                   jax.ShapeDtypeStruct((1, KV_LORA), F32), jax.ShapeDtypeStruct((1, LANES), F32),
                   jax.ShapeDtypeStruct((1, LANES), F32), jax.ShapeDtypeStruct((1, LANES), F32)],
        compiler_params=_params("arbitrary"),
    )(z, g_cq, g_ckv, g_qf, g_kf, b_f, dcqn, dckvn, dqf, dkf, dvf, dlogf, dkpe)


def _cumsum_rows(x, *, reverse, name, out_scale=1.0):
    t = x.shape[0]
    nblk = t // LANES

    def body(x_ref, f_ref, ft_ref, carry_ref):
        r = _iota((LANES, LANES), 0)
        c = _iota((LANES, LANES), 1)
        tri = jnp.where((c >= r) if reverse else (c <= r), 1.0, 0.0).astype(F32)
        carry_ref[...] = jnp.zeros_like(carry_ref)

        def step(s, _):
            b = (nblk - 1 - s) if reverse else s
            start = pl.multiple_of(b * LANES, LANES)
            blk = x_ref[pl.ds(start, LANES), :]
            cs = jnp.dot(tri, blk, precision=lax.Precision.HIGHEST, preferred_element_type=F32) + carry_ref[0:1, :]
            scaled = cs if out_scale == 1.0 else cs * out_scale
            f_ref[pl.ds(start, LANES), :] = scaled
            ft_ref[:, pl.ds(start, LANES)] = scaled.T
            carry_ref[0:1, :] = cs[0:1, :] if reverse else cs[LANES - 1:LANES, :]
            return 0

        lax.fori_loop(0, nblk, step, 0)

    return pl.pallas_call(
        body, name=name, grid=(1,),
        in_specs=[pl.BlockSpec((t, LANES), lambda i: (0, 0))],
        out_specs=[pl.BlockSpec((t, LANES), lambda i: (0, 0)), pl.BlockSpec((LANES, t), lambda i: (0, 0))],
        out_shape=[jax.ShapeDtypeStruct((t, LANES), F32), jax.ShapeDtypeStruct((LANES, t), F32)],
        scratch_shapes=[pltpu.VMEM((8, LANES), F32)],
        compiler_params=_params("arbitrary"),
    )(x)


def _rope_partner(x, lane):
    half = MLA_ROPE // 2
    swapped = jnp.where(lane < KPE_LANE + half, pltpu.roll(x, LANES - half, axis=1), pltpu.roll(x, half, axis=1))
    return jnp.where((lane >= KPE_LANE) & (lane < KPE_LANE + MLA_ROPE), swapped, 0.0)


def _rope_tables(t):
    pos = (jnp.arange(t, dtype=jnp.int32) - PAD).astype(F32)
    inv_freq = ROPE_BASE ** (-jnp.arange(0, MLA_ROPE, 2, dtype=F32) / MLA_ROPE)
    ang = pos[:, None] * inv_freq[None, :]
    cos, sin = jnp.cos(ang), jnp.sin(ang)
    ones = jnp.ones((t, KPE_LANE), F32)
    tail = jnp.zeros((t, LANES - KPE_LANE - MLA_ROPE), F32)
    c_tab = jnp.concatenate([ones, cos, cos, tail + 1.0], axis=1)
    s_tab = jnp.concatenate([ones * 0.0, -sin, sin, tail], axis=1)
    return c_tab, s_tab


def _prep_b_fwd(q_raw, kv_raw, z, c_tab, s_tab, g_q, g_k):
    t = q_raw.shape[0]
    bt = min(PREP_TILE, t)
    qw = HEADS * LANES
    vw = HEADS * HEAD_V

    def body(q_ref, kv_ref, zl_ref, c_ref, s_ref, gq_ref, gk_ref, qn_ref, kn_ref, v_ref):
        lane = _iota((1, LANES), 1)
        kpe = jnp.where((lane >= KPE_LANE) & (lane < KPE_LANE + MLA_ROPE), zl_ref[...], 0.0)
        cv, sv = c_ref[...], s_ref[...]
        for h in range(HEADS):
            sl = slice(h * LANES, (h + 1) * LANES)
            for x, g_ref, o_ref in ((q_ref[:, sl], gq_ref, qn_ref), (kv_ref[:, sl] + kpe, gk_ref, kn_ref)):
                r = lax.rsqrt(jnp.sum(x * x, axis=-1, keepdims=True) / MLA_QK + EPS)
                xn = x * r * g_ref[...]
                o_ref[:, sl] = (xn * cv + _rope_partner(xn, lane) * sv).astype(o_ref.dtype)
        v_ref[...] = kv_ref[:, qw:qw + vw].astype(v_ref.dtype)

    def row(w):
        return pl.BlockSpec((bt, w), lambda i: (i, 0))

    vec = pl.BlockSpec((1, LANES), lambda i: (0, 0))
    return pl.pallas_call(
        body, name="attn_prep_b_fwd", grid=(t // bt,),
        in_specs=[row(qw), row(qw + vw), pl.BlockSpec((bt, LANES), lambda i: (i, ZA_LAST // LANES)),
                  row(LANES), row(LANES), vec, vec],
        out_specs=[row(qw), row(qw), row(vw)],
        out_shape=[jax.ShapeDtypeStruct((t, qw), MXU_DTYPE), jax.ShapeDtypeStruct((t, qw), MXU_DTYPE),
                   jax.ShapeDtypeStruct((t, vw), MXU_DTYPE)],
        compiler_params=_params("parallel"),
    )(q_raw, kv_raw, z, c_tab, s_tab, g_q, g_k)


def _prep_b_bwd(q_raw, kv_raw, z, c_tab, s_tab, g_q, g_k, dqn, dkn, dv):
    t = q_raw.shape[0]
    bt = min(PREP_TILE, t)
    qw = HEADS * LANES
    vw = HEADS * HEAD_V

    def body(q_ref, kv_ref, zl_ref, c_ref, s_ref, gq_ref, gk_ref, dqn_ref, dkn_ref, dv_ref,
             dq_ref, dkv_ref, dkpe_ref, dgq_ref, dgk_ref):
        @pl.when(pl.program_id(0) == 0)
        def _():
            dgq_ref[...] = jnp.zeros_like(dgq_ref)
            dgk_ref[...] = jnp.zeros_like(dgk_ref)

        lane = _iota((1, LANES), 1)
        rope_lanes = (lane >= KPE_LANE) & (lane < KPE_LANE + MLA_ROPE)
        kpe = jnp.where(rope_lanes, zl_ref[...], 0.0)
        cv, sv = c_ref[...], s_ref[...]
        dkpe = jnp.zeros((bt, LANES), F32)
        for h in range(HEADS):
            sl = slice(h * LANES, (h + 1) * LANES)
            for is_k, x, g_ref, dout, dg_ref in ((False, q_ref[:, sl], gq_ref, dqn_ref[:, sl], dgq_ref),
                                                  (True, kv_ref[:, sl] + kpe, gk_ref, dkn_ref[:, sl], dgk_ref)):
                r = lax.rsqrt(jnp.sum(x * x, axis=-1, keepdims=True) / MLA_QK + EPS)
                xhat = x * r
                dxn = dout * cv + _rope_partner(dout * sv, lane)
                dg_ref[...] += jnp.sum(dxn * xhat, axis=0, keepdims=True)
                dxhat = dxn * g_ref[...]
                dx = r * (dxhat - xhat * (jnp.sum(dxhat * xhat, axis=-1, keepdims=True) / MLA_QK))
                if is_k:
                    dkv_ref[:, sl] = jnp.where(lane < KPE_LANE, dx, 0.0).astype(dkv_ref.dtype)
                    dkpe = dkpe + jnp.where(rope_lanes, dx, 0.0)
                else:
                    dq_ref[:, sl] = dx.astype(dq_ref.dtype)
        dkv_ref[:, qw:qw + vw] = dv_ref[...].astype(dkv_ref.dtype)
        dkpe_ref[...] = dkpe

    def row(w):
        return pl.BlockSpec((bt, w), lambda i: (i, 0))

    vec = pl.BlockSpec((1, LANES), lambda i: (0, 0))
    return pl.pallas_call(
        body, name="attn_prep_b_bwd", grid=(t // bt,),
        in_specs=[row(qw), row(qw + vw), pl.BlockSpec((bt, LANES), lambda i: (i, ZA_LAST // LANES)),
                  row(LANES), row(LANES), vec, vec, row(qw), row(qw), row(vw)],
        out_specs=[row(qw), row(qw + vw), row(LANES), vec, vec],
        out_shape=[jax.ShapeDtypeStruct((t, qw), MXU_DTYPE), jax.ShapeDtypeStruct((t, qw + vw), MXU_DTYPE),
                   jax.ShapeDtypeStruct((t, LANES), F32), jax.ShapeDtypeStruct((1, LANES), F32),
                   jax.ShapeDtypeStruct((1, LANES), F32)],
        compiler_params=_params("arbitrary"),
    )(q_raw, kv_raw, z, c_tab, s_tab, g_q, g_k, dqn, dkn, dv)


NT_DIMS = (((1,), (1,)), ((), ()))
TN_DIMS = (((0,), (0,)), ((), ()))


def _head_qk(q_ref, k_ref, e, mla, lo):
    if mla:
        return q_ref[:, e * LANES:(e + 1) * LANES], k_ref[:, e * LANES:(e + 1) * LANES]
    q = q_ref[...]
    return jnp.where(lo if e == 0 else jnp.logical_not(lo), q, jnp.zeros_like(q)), k_ref[...]


def _attn_specs(mla, blk, q_map, k_map):
    w = 2 * LANES if mla else LANES
    q_spec = pl.BlockSpec((blk, w), lambda j, a, b: (q_map(a, b), j))
    k_spec = pl.BlockSpec((blk, w), lambda j, a, b: (k_map(a, b), j))
    qv_spec = pl.BlockSpec((blk, LANES), lambda j, a, b: (q_map(a, b), j))
    kv_spec = pl.BlockSpec((blk, LANES), lambda j, a, b: (k_map(a, b), j))
    fq_spec = pl.BlockSpec((blk, LANES), lambda j, a, b: (q_map(a, b), 0))
    fk_spec = pl.BlockSpec((8, blk), lambda j, a, b: (0, k_map(a, b)))
    return q_spec, k_spec, qv_spec, kv_spec, fq_spec, fk_spec


def _flash_fwd(q, k, v, f, f_t, *, mla, scale, name):
    t = q.shape[0]
    blk = min(ATTN_BLOCK, t)
    nb = t // blk
    pairs = HEADS // 2
    q_spec, k_spec, qv_spec, kv_spec, fq_spec, fk_spec = _attn_specs(
        mla, blk, lambda i, kk: i, lambda i, kk: jnp.minimum(kk, i))

    def body(*refs):
        if mla:
            q_ref, k_ref, v_ref, o_ref, lse_ref, m_s, l_s, acc_s = refs
            fq_ref = fk_ref = None
        else:
            q_ref, k_ref, v_ref, fq_ref, fk_ref, o_ref, lse_ref, m_s, l_s, acc_s = refs
        j, i, kk = pl.program_id(0), pl.program_id(1), pl.program_id(2)
        lo = _iota((1, LANES), 1) < HEAD_V

        @pl.when(kk == 0)
        def _():
            m_s[...] = jnp.full_like(m_s, NEG)
            l_s[...] = jnp.zeros_like(l_s)
            acc_s[...] = jnp.zeros_like(acc_s)

        def step(mask):
            vv = v_ref[...]
            for e in range(2):
                s, _, _ = _scores(q_ref, k_ref, fq_ref, fk_ref, e, j, mla, scale, lo, mask)
                m_prev = m_s[e]
                m_new = jnp.maximum(m_prev, jnp.max(s, axis=1, keepdims=True))
                alpha = jnp.exp(m_prev - m_new)
                p = jnp.exp(s - m_new)
                if mask is not None:
                    p = jnp.where(mask, p, 0.0)
                l_s[e] = alpha * l_s[e] + jnp.sum(p, axis=1, keepdims=True)
                acc_s[e] = alpha * acc_s[e] + jnp.dot(p.astype(MXU_DTYPE), vv, preferred_element_type=F32)
                m_s[e] = m_new

        _masked_and_plain(kk <= i, i, kk, blk, step)

        @pl.when(kk == nb - 1)
        def _():
            valid = (i * blk + _iota((blk, 1), 0)) >= PAD
            outs, lses = [], []
            for e in range(2):
                l = l_s[e]
                outs.append(acc_s[e] * jnp.where(l > 0.0, 1.0 / jnp.where(l > 0.0, l, 1.0), 0.0))
                lses.append(m_s[e] + jnp.log(jnp.where(l > 0.0, l, 1.0)))
            o = jnp.where(lo, outs[0], outs[1])
            o_ref[...] = jnp.where(valid, o, 0.0).astype(o_ref.dtype)
            lane = _iota((1, LANES), 1)
            lse_ref[...] = jnp.where(lane == 0, lses[0], jnp.where(lane == 1, lses[1], 0.0))

    in_specs = [q_spec, k_spec, kv_spec] + ([] if mla else [fq_spec, fk_spec])
    args = (q, k, v) + (() if mla else (f, f_t))
    hv = HEADS * HEAD_V
    return pl.pallas_call(
        body, name=name, grid=(pairs, nb, nb),
        in_specs=in_specs, out_specs=[qv_spec, qv_spec],
        out_shape=[jax.ShapeDtypeStruct((t, hv), F32), jax.ShapeDtypeStruct((t, hv), F32)],
        scratch_shapes=[pltpu.VMEM((2, blk, 1), F32), pltpu.VMEM((2, blk, 1), F32), pltpu.VMEM((2, blk, LANES), F32)],
        compiler_params=_params("parallel", "parallel", "arbitrary"),
    )(*args)


def _bwd_tile(q_ref, k_ref, v_ref, o_ref, do_ref, lse_ref, fq_ref, fk_ref, e, pair, mla, scale, lo, mask):
    s, qe, ke = _scores(q_ref, k_ref, fq_ref, fk_ref, e, pair, mla, scale, lo, mask)
    p = jnp.exp(s - lse_ref[:, e:e + 1])
    if mask is not None:
        p = jnp.where(mask, p, 0.0)
    do = do_ref[...]
    doe = jnp.where(lo if e == 0 else jnp.logical_not(lo), do, jnp.zeros_like(do))
    dp = lax.dot_general(doe, v_ref[...], NT_DIMS, preferred_element_type=F32)
    delta = jnp.sum(doe.astype(F32) * o_ref[...].astype(F32), axis=1, keepdims=True)
    return p, p * (dp - delta), qe, ke


def _flash_bwd_dq(q, k, v, o, do, lse, f, f_t, *, mla, scale, col0, name):
    t = q.shape[0]
    blk = min(ATTN_BLOCK, t)
    nb = t // blk
    pairs = HEADS // 2
    w = 2 * LANES if mla else LANES
    q_spec, k_spec, qv_spec, kv_spec, fq_spec, fk_spec = _attn_specs(
        mla, blk, lambda i, kk: i, lambda i, kk: jnp.minimum(kk, i))
    od_spec = pl.BlockSpec((blk, LANES), lambda j, i, kk: (i, col0 + j))

    def body(*refs):
        if mla:
            q_ref, k_ref, v_ref, o_ref, do_ref, lse_ref, dq_ref, dq_s = refs
            fq_ref = fk_ref = rs_ref = rs_s = None
        else:
            q_ref, k_ref, v_ref, o_ref, do_ref, lse_ref, fq_ref, fk_ref, dq_ref, rs_ref, dq_s, rs_s = refs
        j, i, kk = pl.program_id(0), pl.program_id(1), pl.program_id(2)
        lo = _iota((1, LANES), 1) < HEAD_V

        @pl.when(kk == 0)
        def _():
            dq_s[...] = jnp.zeros_like(dq_s)
            if not mla:
                rs_s[...] = jnp.zeros_like(rs_s)

        def step(mask):
            for e in range(2):
                _, ds, _, ke = _bwd_tile(q_ref, k_ref, v_ref, o_ref, do_ref, lse_ref, fq_ref, fk_ref, e, j, mla, scale,
                                         lo, mask)
                dq_s[e] += jnp.dot(ds.astype(MXU_DTYPE), ke, preferred_element_type=F32)
                if not mla:
                    rs_s[e] += jnp.sum(ds, axis=1, keepdims=True)

        _masked_and_plain(kk <= i, i, kk, blk, step)

        @pl.when(kk == nb - 1)
        def _():
            if mla:
                dq_ref[:, 0:LANES] = dq_s[0] * scale
                dq_ref[:, LANES:2 * LANES] = dq_s[1] * scale
            else:
                dq_ref[...] = jnp.where(lo, dq_s[0], dq_s[1]) * scale
                lane = _iota((1, LANES), 1)
                rs_ref[...] = jnp.where(lane == 0, rs_s[0], jnp.where(lane == 1, rs_s[1], 0.0))

    in_specs = [q_spec, k_spec, kv_spec, od_spec, od_spec, qv_spec] + ([] if mla else [fq_spec, fk_spec])
    args = (q, k, v, o, do, lse) + (() if mla else (f, f_t))
    out_specs = [q_spec] + ([] if mla else [qv_spec])
    out_shape = [jax.ShapeDtypeStruct((t, pairs * w), F32)]
    scratch = [pltpu.VMEM((2, blk, LANES), F32)]
    if not mla:
        out_shape.append(jax.ShapeDtypeStruct((t, pairs * LANES), F32))
        scratch.append(pltpu.VMEM((2, blk, 1), F32))
    outs = pl.pallas_call(
        body, name=name, grid=(pairs, nb, nb),
        in_specs=in_specs, out_specs=out_specs, out_shape=out_shape, scratch_shapes=scratch,
        compiler_params=_params("parallel", "parallel", "arbitrary"),
    )(*args)
    return outs[0] if mla else outs


def _flash_bwd_dkv(q, k, v, o, do, lse, f, f_t, *, mla, scale, col0, name):
    t = q.shape[0]
    blk = min(ATTN_BLOCK, t)
    nb = t // blk
    pairs = HEADS // 2
    w = 2 * LANES if mla else LANES
    q_spec, k_spec, qv_spec, kv_spec, fq_spec, fk_spec = _attn_specs(
        mla, blk, lambda a, b: jnp.maximum(a, b), lambda a, b: a)
    od_spec = pl.BlockSpec((blk, LANES), lambda j, a, b: (jnp.maximum(a, b), col0 + j))
    cs_spec = pl.BlockSpec((8, blk), lambda j, a, b: (j, a))

    def body(*refs):
        if mla:
            q_ref, k_ref, v_ref, o_ref, do_ref, lse_ref, dk_ref, dv_ref, dk_s, dv_s = refs
            fq_ref = fk_ref = cs_ref = cs_s = None
        else:
            q_ref, k_ref, v_ref, o_ref, do_ref, lse_ref, fq_ref, fk_ref, dk_ref, dv_ref, cs_ref, dk_s, dv_s, cs_s = refs
        j, kb, qb = pl.program_id(0), pl.program_id(1), pl.program_id(2)
        lo = _iota((1, LANES), 1) < HEAD_V

        @pl.when(qb == 0)
        def _():
            dk_s[...] = jnp.zeros_like(dk_s)
            dv_s[...] = jnp.zeros_like(dv_s)
            if not mla:
                cs_s[...] = jnp.zeros_like(cs_s)

        def step(mask):
            do = do_ref[...]
            for e in range(2):
                p, ds, _, _ = _bwd_tile(q_ref, k_ref, v_ref, o_ref, do_ref, lse_ref, fq_ref, fk_ref, e, j, mla, scale,
                                        lo, mask)
                dv_s[e] += lax.dot_general(p.astype(MXU_DTYPE), do, TN_DIMS, preferred_element_type=F32)
                q_src = q_ref[:, e * LANES:(e + 1) * LANES] if mla else q_ref[...]
                dk_s[e] += lax.dot_general(ds.astype(MXU_DTYPE), q_src, TN_DIMS, preferred_element_type=F32)
                if not mla:
                    cs_s[e] += jnp.sum(ds, axis=0, keepdims=True)

        _masked_and_plain(qb >= kb, qb, kb, blk, step)

        @pl.when(qb == nb - 1)
        def _():
            dv_ref[...] = jnp.where(lo, dv_s[0], dv_s[1])
            if mla:
                dk_ref[:, 0:LANES] = dk_s[0] * scale
                dk_ref[:, LANES:2 * LANES] = dk_s[1] * scale
            else:
                dk_ref[...] = jnp.where(lo, dk_s[0], dk_s[1]) * scale
                sub = _iota((8, 1), 0)
                cs_ref[...] = jnp.where(sub == 0, cs_s[0], jnp.where(sub == 1, cs_s[1], 0.0))

    in_specs = [q_spec, k_spec, kv_spec, od_spec, od_spec, qv_spec] + ([] if mla else [fq_spec, fk_spec])
    args = (q, k, v, o, do, lse) + (() if mla else (f, f_t))
    out_specs = [k_spec, kv_spec] + ([] if mla else [cs_spec])
    out_shape = [jax.ShapeDtypeStruct((t, pairs * w), F32), jax.ShapeDtypeStruct((t, HEADS * HEAD_V), F32)]
    scratch = [pltpu.VMEM((2, blk, LANES), F32), pltpu.VMEM((2, blk, LANES), F32)]
    if not mla:
        out_shape.append(jax.ShapeDtypeStruct((pairs * 8, t), F32))
        scratch.append(pltpu.VMEM((2, 1, blk), F32))
    return pl.pallas_call(
        body, name=name, grid=(pairs, nb, nb),
        in_specs=in_specs, out_specs=out_specs, out_shape=out_shape, scratch_shapes=scratch,
        compiler_params=_params("parallel", "parallel", "arbitrary"),
    )(*args)


ATTN_CHUNK = 640
LOG2E = 1.4426950408889634
LN2 = 0.6931471805599453


def _for_chunks(n, ch, body):
    for c in range(n):
        body(c * ch)


def _select_lane(x, idx):
    return jnp.sum(jnp.where(_iota(x.shape, 1) == idx, x, 0.0), axis=1, keepdims=True)


def _select_row(x, idx):
    return jnp.sum(jnp.where(_iota(x.shape, 0) == idx, x, 0.0), axis=0, keepdims=True)


def _flash_fwd_chunked(q, k, v, f, f_t, *, mla, name):
    t = q.shape[0]
    blk = min(ATTN_BLOCK, t)
    nb = t // blk
    pairs = HEADS // 2
    ch = min(ATTN_CHUNK, blk)
    assert blk % ch == 0 and blk >= PAD
    w = 2 * LANES if mla else LANES
    q_of, k_of = _causal_pairs(nb, by_key=False)
    q_spec = pl.BlockSpec((blk, w), lambda j, s, qt, kt: (qt[s], j))
    k_spec = pl.BlockSpec((blk, w), lambda j, s, qt, kt: (kt[s], j))
    kv_spec = pl.BlockSpec((blk, LANES), lambda j, s, qt, kt: (kt[s], j))
    qv_spec = pl.BlockSpec((blk, LANES), lambda j, s, qt, kt: (qt[s], j))
    fq_spec = pl.BlockSpec((blk, LANES), lambda j, s, qt, kt: (qt[s], 0))
    fk_spec = pl.BlockSpec((8, blk), lambda j, s, qt, kt: (0, kt[s]))
    lse_spec = pl.BlockSpec((8, blk), lambda j, s, qt, kt: (j, qt[s]))

    def body(qt_ref, kt_ref, *refs):
        if mla:
            q_ref, k_ref, v_ref, o_ref, lse_ref, m_s, l_s, a_s, acc_s, s_s, p_s = refs
            fq_ref = fk_ref = None
        else:
            q_ref, k_ref, v_ref, fq_ref, fk_ref, o_ref, lse_ref, m_s, l_s, a_s, acc_s, s_s, p_s = refs
        j, step_id = pl.program_id(0), pl.program_id(1)
        i, kk = qt_ref[step_id], kt_ref[step_id]
        lo = _iota((1, LANES), 1) < HEAD_V

        @pl.when(kk == 0)
        def _():
            m_s[...] = jnp.full_like(m_s, NEG)
            l_s[...] = jnp.zeros_like(l_s)
            acc_s[...] = jnp.zeros_like(acc_s)

        def step(masked):
            vv = v_ref[...]
            for e in range(2):
                qe, ke = _head_qk(q_ref, k_ref, e, mla, lo)
                s_s[...] = lax.dot_general(qe, ke, NT_DIMS, preferred_element_type=F32)
                fkr = None if mla else _select_row(fk_ref[...], 2 * j + e)

                def chunk(r0, e=e, fkr=fkr):
                    rows = pl.ds(r0, ch)
                    s = s_s[rows, :]
                    if not mla:
                        s = s + _select_lane(fq_ref[rows, :], 2 * j + e) - fkr
                    if masked:
                        rpos = i * blk + r0 + _iota((ch, blk), 0)
                        cpos = kk * blk + _iota((ch, blk), 1)
                        mask = (cpos <= rpos) & (cpos >= PAD)
                        s = jnp.where(mask, s, NEG)
                    m_prev = m_s[e, rows, :]
                    m_new = jnp.maximum(m_prev, jnp.max(s, axis=1, keepdims=True))
                    alpha = jnp.exp2(m_prev - m_new)
                    p = jnp.exp2(s - m_new)
                    if masked:
                        p = jnp.where(mask, p, 0.0)
                    l_s[e, rows, :] = alpha * l_s[e, rows, :] + jnp.sum(p, axis=1, keepdims=True)
                    m_s[e, rows, :] = m_new
                    a_s[rows, :] = alpha
                    p_s[rows, :] = p.astype(p_s.dtype)

                _for_chunks(blk // ch, ch, chunk)
                acc_s[e] = a_s[...] * acc_s[e] + jnp.dot(p_s[...], vv, preferred_element_type=F32)

        needs_mask = (kk == i) | (kk == 0)

        @pl.when(needs_mask)
        def _():
            step(True)

        @pl.when(jnp.logical_not(needs_mask))
        def _():
            step(False)

        @pl.when(kk == i)
        def _():
            valid = (i * blk + _iota((blk, 1), 0)) >= PAD
            outs, lses = [], []
            for e in range(2):
                l = l_s[e]
                outs.append(acc_s[e] * jnp.where(l > 0.0, 1.0 / jnp.where(l > 0.0, l, 1.0), 0.0))
                lses.append(m_s[e] + jnp.log(jnp.where(l > 0.0, l, 1.0)) * LOG2E)
            o = jnp.where(lo, outs[0], outs[1])
            o_ref[...] = jnp.where(valid, o, 0.0).astype(o_ref.dtype)
            lane = _iota((1, LANES), 1)
            lse_cols = jnp.where(lane == 0, lses[0], jnp.where(lane == 1, lses[1], 0.0))
            lse_ref[...] = lse_cols.T[0:8, :]

    in_specs = [q_spec, k_spec, kv_spec] + ([] if mla else [fq_spec, fk_spec])
    args = (q, k, v) + (() if mla else (f, f_t))
    hv = HEADS * HEAD_V
    grid_spec = pltpu.PrefetchScalarGridSpec(
        num_scalar_prefetch=2, grid=(pairs, len(q_of)), in_specs=in_specs, out_specs=[qv_spec, lse_spec],
        scratch_shapes=[pltpu.VMEM((2, blk, 1), F32), pltpu.VMEM((2, blk, 1), F32), pltpu.VMEM((blk, 1), F32),
                        pltpu.VMEM((2, blk, LANES), F32), pltpu.VMEM((blk, blk), F32), pltpu.VMEM((blk, blk), MXU_DTYPE)])
    return pl.pallas_call(
        body, name=name, grid_spec=grid_spec,
        out_shape=[jax.ShapeDtypeStruct((t, hv), F32), jax.ShapeDtypeStruct((pairs * 8, t), F32)],
        compiler_params=_params("parallel", "arbitrary"),
    )(jnp.asarray(q_of), jnp.asarray(k_of), *args)


def _causal_pairs(nb, *, by_key):
    if by_key:
        pairs = [(qb, kb) for kb in range(nb) for qb in range(kb, nb)]
    else:
        pairs = [(qb, kb) for qb in range(nb) for kb in range(qb + 1)]
    return (np.asarray([p[0] for p in pairs], np.int32), np.asarray([p[1] for p in pairs], np.int32))


def _delta_rows(do, o):
    t, width = o.shape
    bt = min(ATTN_BLOCK, t)
    n_heads = width // HEAD_V

    def body(do_ref, o_ref, d_ref):
        prod = do_ref[...].astype(F32) * o_ref[...]
        col = _iota((width, LANES), 0)
        first = _iota((width, LANES), 1) * HEAD_V
        sel = jnp.where((col >= first) & (col < first + HEAD_V), 1.0, 0.0).astype(F32)
        per_head = jnp.dot(prod, sel, precision=lax.Precision.HIGHEST, preferred_element_type=F32)
        d_ref[...] = per_head.T[0:n_heads, :]

    return pl.pallas_call(
        body, name="attn_delta", grid=(t // bt,),
        in_specs=[pl.BlockSpec((bt, width), lambda i: (i, 0)), pl.BlockSpec((bt, width), lambda i: (i, 0))],
        out_specs=pl.BlockSpec((n_heads, bt), lambda i: (0, i)),
        out_shape=jax.ShapeDtypeStruct((n_heads, t), F32),
        compiler_params=_params("parallel"),
    )(do, o)


def _flash_bwd_fused(q, k, v, do, lse_t, delta_t, f, f_t, *, mla, col0, name):
    t = q.shape[0]
    blk = min(ATTN_BLOCK, t)
    nb = t // blk
    pairs = HEADS // 2
    ch = min(ATTN_CHUNK, blk)
    assert blk % ch == 0
    w = 2 * LANES if mla else LANES
    last = nb - 1
    assert blk >= PAD
    q_of, k_of = _causal_pairs(nb, by_key=True)
    q_spec = pl.BlockSpec((blk, w), lambda j, s, qt, kt: (qt[s], j))
    k_spec = pl.BlockSpec((blk, w), lambda j, s, qt, kt: (kt[s], j))
    v_spec = pl.BlockSpec((blk, LANES), lambda j, s, qt, kt: (kt[s], j))
    do_spec = pl.BlockSpec((blk, LANES), lambda j, s, qt, kt: (qt[s], col0 + j))
    lse_spec = pl.BlockSpec((8, blk), lambda j, s, qt, kt: (j, qt[s]))
    delta_spec = pl.BlockSpec((8, blk), lambda j, s, qt, kt: (col0 // (HEADS // 2), qt[s]))
    fq_spec = pl.BlockSpec((8, blk), lambda j, s, qt, kt: (0, qt[s]))
    fk_spec = pl.BlockSpec((blk, LANES), lambda j, s, qt, kt: (kt[s], 0))
    dq_spec = pl.BlockSpec((blk, w), lambda j, s, qt, kt: (kt[s], j))
    rs_spec = pl.BlockSpec((8, blk), lambda j, s, qt, kt: (j, kt[s]))
    cs_spec = pl.BlockSpec((blk, LANES), lambda j, s, qt, kt: (kt[s], j))

    def body(qt_ref, kt_ref, *refs):
        if mla:
            (q_ref, k_ref, v_ref, do_ref, lse_ref, delta_ref, dq_ref, dk_ref, dv_ref,
             dq_s, dk_s, dv_s, st_s, dpt_s, pt_s, dst_s) = refs
            fq_ref = fk_ref = rs_ref = cs_ref = rs_s = cs_s = None
        else:
            (q_ref, k_ref, v_ref, do_ref, lse_ref, delta_ref, fq_ref, fk_ref, dq_ref, dk_ref, dv_ref, rs_ref, cs_ref,
             dq_s, dk_s, dv_s, st_s, dpt_s, pt_s, dst_s, rs_s, cs_s) = refs
        j, step_id = pl.program_id(0), pl.program_id(1)
        qb, kb = qt_ref[step_id], kt_ref[step_id]
        lo = _iota((1, LANES), 1) < HEAD_V

        @pl.when(step_id == 0)
        def _():
            dq_s[...] = jnp.zeros_like(dq_s)
            if not mla:
                rs_s[...] = jnp.zeros_like(rs_s)

        @pl.when(qb == kb)
        def _():
            dk_s[...] = jnp.zeros_like(dk_s)
            dv_s[...] = jnp.zeros_like(dv_s)
            if not mla:
                cs_s[...] = jnp.zeros_like(cs_s)

        def step(masked):
            do = do_ref[...]
            vv = v_ref[...]
            for e in range(2):
                half = lo if e == 0 else jnp.logical_not(lo)
                qe, ke = _head_qk(q_ref, k_ref, e, mla, lo)
                doe = jnp.where(half, do, jnp.zeros_like(do))
                st_s[...] = lax.dot_general(ke, qe, NT_DIMS, preferred_element_type=F32)
                dpt_s[...] = lax.dot_general(vv, doe, NT_DIMS, preferred_element_type=F32)
                head = 2 * j + e
                lse_row = _select_row(lse_ref[...], e)
                delta_row = _select_row(delta_ref[...], head)
                fq_row = None if mla else _select_row(fq_ref[...], head)

                def chunk(r0, e=e, lse_row=lse_row, delta_row=delta_row, fq_row=fq_row, head=head):
                    rows = pl.ds(r0, ch)
                    s = st_s[rows, :]
                    if not mla:
                        s = s + fq_row - _select_lane(fk_ref[rows, :], head)
                    p = jnp.exp2(s - lse_row)
                    if masked:
                        kpos = kb * blk + r0 + _iota((ch, blk), 0)
                        qpos = qb * blk + _iota((ch, blk), 1)
                        p = jnp.where((kpos <= qpos) & (kpos >= PAD), p, 0.0)
                    ds = p * (dpt_s[rows, :] - delta_row)
                    pt_s[rows, :] = p.astype(pt_s.dtype)
                    dst_s[rows, :] = ds.astype(dst_s.dtype)
                    if not mla:
                        cs_s[e, rows, :] += jnp.sum(ds, axis=1, keepdims=True)
                        rs_s[qb, e] += jnp.sum(ds, axis=0, keepdims=True)

                _for_chunks(blk // ch, ch, chunk)
                dv_s[e] += jnp.dot(pt_s[...], do, preferred_element_type=F32)
                q_src = qe if mla else q_ref[...]
                dk_s[e] += jnp.dot(dst_s[...], q_src, preferred_element_type=F32)
                dq_s[qb, e] += lax.dot_general(dst_s[...], ke, TN_DIMS, preferred_element_type=F32)

        needs_mask = (qb == kb) | (kb == 0)

        @pl.when(needs_mask)
        def _():
            step(True)

        @pl.when(jnp.logical_not(needs_mask))
        def _():
            step(False)

        @pl.when(qb == last)
        def _():
            dv_ref[...] = jnp.where(lo, dv_s[0], dv_s[1])
            if mla:
                dk_ref[:, 0:LANES] = dk_s[0] * LN2
                dk_ref[:, LANES:2 * LANES] = dk_s[1] * LN2
            else:
                dk_ref[...] = jnp.where(lo, dk_s[0], dk_s[1]) * LN2
                lane = _iota((1, LANES), 1)
                cs_ref[...] = jnp.where(lane == 0, cs_s[0], jnp.where(lane == 1, cs_s[1], 0.0))

        @pl.when(qb == kb)
        def _():
            if mla:
                dq_ref[:, 0:LANES] = dq_s[qb, 0] * LN2
                dq_ref[:, LANES:2 * LANES] = dq_s[qb, 1] * LN2
            else:
                dq_ref[...] = jnp.where(lo, dq_s[qb, 0], dq_s[qb, 1]) * LN2
                sub = _iota((8, 1), 0)
                rs_ref[...] = jnp.where(sub == 0, rs_s[qb, 0], jnp.where(sub == 1, rs_s[qb, 1], 0.0))

    in_specs = [q_spec, k_spec, v_spec, do_spec, lse_spec, delta_spec] + ([] if mla else [fq_spec, fk_spec])
    args = (q, k, v, do, lse_t, delta_t) + (() if mla else (f_t, f))
    hv = HEADS * HEAD_V
    out_specs = [dq_spec, k_spec, v_spec]
    out_shape = [jax.ShapeDtypeStruct((t, pairs * w), F32), jax.ShapeDtypeStruct((t, pairs * w), F32),
                 jax.ShapeDtypeStruct((t, hv), F32)]
    scratch = [pltpu.VMEM((nb, 2, blk, LANES), F32), pltpu.VMEM((2, blk, LANES), F32), pltpu.VMEM((2, blk, LANES), F32),
               pltpu.VMEM((blk, blk), F32), pltpu.VMEM((blk, blk), F32), pltpu.VMEM((blk, blk), MXU_DTYPE),
               pltpu.VMEM((blk, blk), MXU_DTYPE)]
    if not mla:
        out_specs += [rs_spec, cs_spec]
        out_shape += [jax.ShapeDtypeStruct((pairs * 8, t), F32), jax.ShapeDtypeStruct((t, hv), F32)]
        scratch += [pltpu.VMEM((nb, 2, 1, blk), F32), pltpu.VMEM((2, blk, 1), F32)]
    grid_spec = pltpu.PrefetchScalarGridSpec(
        num_scalar_prefetch=2, grid=(pairs, len(q_of)), in_specs=in_specs, out_specs=out_specs, scratch_shapes=scratch)
    return pl.pallas_call(
        body, name=name, grid_spec=grid_spec, out_shape=out_shape,
        compiler_params=_params("parallel", "arbitrary"),
    )(jnp.asarray(q_of), jnp.asarray(k_of), *args)


def _shift_down(x, halo, n):
    rows = x.shape[0]
    r = _iota((rows, 1), 0)
    out = pltpu.roll(x, n, axis=0)
    for s in range(n):
        out = jnp.where(r == s, halo[8 - n + s:8 - n + s + 1, :], out)
    return out


def _shift_up(x, halo, n):
    rows = x.shape[0]
    r = _iota((rows, 1), 0)
    out = pltpu.roll(x, rows - n, axis=0)
    for s in range(n):
        out = jnp.where(r == rows - n + s, halo[s:s + 1, :], out)
    return out


def _conv_specs(bt, nblk):
    d = D_MODEL
    per8 = bt // 8
    z_spec = pl.BlockSpec((bt, 3 * d), lambda i: (i, 0))
    prev_spec = pl.BlockSpec((8, 3 * d), lambda i: (jnp.maximum(i * per8 - 1, 0), 0))
    next_z = pl.BlockSpec((8, 3 * d), lambda i: (jnp.minimum((i + 1) * per8, nblk * per8 - 1), 0))
    next_d = pl.BlockSpec((8, d), lambda i: (jnp.minimum((i + 1) * per8, nblk * per8 - 1), 0))
    w_spec = pl.BlockSpec((8, d), lambda i: (0, 0))
    row_spec = pl.BlockSpec((bt, d), lambda i: (i, 0))
    return z_spec, prev_spec, next_z, next_d, w_spec, row_spec


def _conv_taps(z_ref, prev_ref, i):
    d = D_MODEL
    g = z_ref[:, d:2 * d] * z_ref[:, 2 * d:3 * d]
    gh = jnp.where(i > 0, prev_ref[:, d:2 * d] * prev_ref[:, 2 * d:3 * d], 0.0)
    return g, _shift_down(g, gh, 1), _shift_down(g, gh, 2)


def _conv_fwd(z, conv_w8):
    t = z.shape[0]
    bt = min(PREP_TILE, t)
    nblk = t // bt
    d = D_MODEL
    z_spec, prev_spec, _, _, w_spec, row_spec = _conv_specs(bt, nblk)

    def body(z_ref, prev_ref, w_ref, v_ref):
        g, g1, g2 = _conv_taps(z_ref, prev_ref, pl.program_id(0))
        y = w_ref[0:1, :] * g2 + w_ref[1:2, :] * g1 + w_ref[2:3, :] * g
        v_ref[...] = (z_ref[:, 0:d] * y).astype(v_ref.dtype)

    return pl.pallas_call(
        body, name="conv_fwd", grid=(nblk,),
        in_specs=[z_spec, prev_spec, w_spec], out_specs=row_spec,
        out_shape=jax.ShapeDtypeStruct((t, d), MXU_DTYPE),
        compiler_params=_params("parallel"),
    )(z, z, conv_w8)


def _conv_bwd(z, conv_w8, dv):
    t = z.shape[0]
    bt = min(PREP_TILE, t)
    nblk = t // bt
    d = D_MODEL
    z_spec, prev_spec, next_z, next_d, w_spec, row_spec = _conv_specs(bt, nblk)

    def body(z_ref, prev_ref, nz_ref, dv_ref, ndv_ref, w_ref, dz_ref, dw_ref):
        i = pl.program_id(0)

        @pl.when(i == 0)
        def _():
            dw_ref[...] = jnp.zeros_like(dw_ref)

        g, g1, g2 = _conv_taps(z_ref, prev_ref, i)
        w0, w1, w2 = w_ref[0:1, :], w_ref[1:2, :], w_ref[2:3, :]
        y = w0 * g2 + w1 * g1 + w2 * g
        dvv = dv_ref[...].astype(F32)
        gate_b = z_ref[:, 0:d]
        dy = dvv * gate_b
        dyn = jnp.where(i < nblk - 1, ndv_ref[...].astype(F32) * nz_ref[:, 0:d], 0.0)
        dg = w2 * dy + w1 * _shift_up(dy, dyn, 1) + w0 * _shift_up(dy, dyn, 2)
        dz_ref[:, 0:d] = (dvv * y).astype(dz_ref.dtype)
        dz_ref[:, d:2 * d] = (dg * z_ref[:, 2 * d:3 * d]).astype(dz_ref.dtype)
        dz_ref[:, 2 * d:3 * d] = (dg * z_ref[:, d:2 * d]).astype(dz_ref.dtype)
        sub = _iota((8, 1), 0)
        s0 = jnp.sum(dy * g2, axis=0, keepdims=True)
        s1 = jnp.sum(dy * g1, axis=0, keepdims=True)
        s2 = jnp.sum(dy * g, axis=0, keepdims=True)
        dw_ref[...] += jnp.where(sub == 0, s0, jnp.where(sub == 1, s1, jnp.where(sub == 2, s2, 0.0)))

    return pl.pallas_call(
        body, name="conv_bwd", grid=(nblk,),
        in_specs=[z_spec, prev_spec, next_z, row_spec, next_d, w_spec], out_specs=[z_spec, w_spec],
        out_shape=[jax.ShapeDtypeStruct((t, 3 * d), MXU_DTYPE), jax.ShapeDtypeStruct((8, d), F32)],
        compiler_params=_params("arbitrary"),
    )(z, z, z, dv, dv, conv_w8)


def _loss_head(h, target):
    t, d = h.shape
    bt = LOSS_TILE
    assert LANES % bt == 0 or bt == LANES
    off = LANES // bt

    def body(h_ref, y_ref, dh_ref, acc_ref):
        i = pl.program_id(0)

        @pl.when(i == 0)
        def _():
            acc_ref[...] = jnp.zeros_like(acc_ref)

        @pl.when(i < off)
        def _():
            dh_ref[...] = jnp.zeros_like(dh_ref)

        @pl.when(i >= off)
        def _():
            err = h_ref[...] - y_ref[...]
            dh_ref[...] = err / d
            acc_ref[...] += jnp.sum(err * err)

    dh, acc = pl.pallas_call(
        body, name="loss_head", grid=(t // bt,),
        in_specs=[pl.BlockSpec((bt, d), lambda i: (i, 0)), pl.BlockSpec((bt, d), lambda i: (jnp.maximum(i - off, 0), 0))],
        out_specs=[pl.BlockSpec((bt, d), lambda i: (i, 0)), pl.BlockSpec((8, LANES), lambda i: (0, 0))],
        out_shape=[jax.ShapeDtypeStruct((t, d), F32), jax.ShapeDtypeStruct((8, LANES), F32)],
        compiler_params=_params("arbitrary"),
    )(h, target)
    return dh, acc[0, 0] * (0.5 / d)


def _common_tile(rows, row_off, cap=512, align=8):
    for b in range(min(cap, rows) // align * align, 0, -align):
        if rows % b == 0 and row_off % b == 0:
            return b
    raise ValueError((rows, row_off))


def _round_up(n, m):
    return -(-n // m) * m


def _adamw(w, m, v, g_buf, row_off, col_off):
    rows, width = w.shape
    wpad = _round_up(width, LANES)
    assert col_off % wpad == 0
    bt = _common_tile(rows, row_off)

    def body(w_ref, m_ref, v_ref, g_ref, go_ref, d_ref, nm_ref, nv_ref):
        gv = g_ref[...]
        if wpad != width:
            gv = gv[:, :width]
        m_new = ADAM_B1 * m_ref[...] + (1.0 - ADAM_B1) * gv
        v_new = ADAM_B2 * v_ref[...] + (1.0 - ADAM_B2) * jnp.square(gv)
        m_hat = m_new / (1.0 - ADAM_B1 ** ADAM_STEP)
        v_hat = v_new / (1.0 - ADAM_B2 ** ADAM_STEP)
        go_ref[...] = gv
        d_ref[...] = -ADAM_LR * (m_hat / (jnp.sqrt(v_hat) + ADAM_EPS) + ADAM_WD * w_ref[...])
        nm_ref[...] = m_new
        nv_ref[...] = v_new

    spec = pl.BlockSpec((bt, width), lambda i: (i, 0))
    g_spec = pl.BlockSpec((bt, wpad), lambda i: (row_off // bt + i, col_off // wpad))
    return pl.pallas_call(
        body, name="adamw", grid=(rows // bt,),
        in_specs=[spec] * 3 + [g_spec], out_specs=[spec] * 4,
        out_shape=[jax.ShapeDtypeStruct((rows, width), F32)] * 4,
        compiler_params=_params("parallel"),
    )(w, m, v, g_buf)


def _add_half(g, got, c, *, out_dtype, name):
    slabs, half, width = got.shape
    bt = next(x for x in range(min(half, 640), 0, -16) if half % x == 0)
    per_half = half // bt

    def body(c_ref, a_ref, b_ref, o_ref):
        o_ref[...] = (a_ref[...] + b_ref[...]).astype(o_ref.dtype)

    grid_spec = pltpu.PrefetchScalarGridSpec(
        num_scalar_prefetch=1, grid=(slabs, per_half),
        in_specs=[pl.BlockSpec((None, bt, width), lambda s, i, cc: (s, cc[0] * per_half + i, 0)),
                  pl.BlockSpec((None, bt, width), lambda s, i, cc: (s, i, 0))],
        out_specs=pl.BlockSpec((None, bt, width), lambda s, i, cc: (s, i, 0)))
    return pl.pallas_call(
        body, name=name, grid_spec=grid_spec,
        out_shape=jax.ShapeDtypeStruct((slabs, half, width), out_dtype), compiler_params=_params("parallel", "parallel"),
    )(jnp.reshape(c, (1,)).astype(jnp.int32), g, got)


def _sum4(parts, slot, *, name):
    _, rows, width = parts.shape
    bt = next(x for x in range(min(rows, 640), 0, -16) if rows % x == 0)

    def body(slot_ref, p_ref, o_ref):
        p = [p_ref[n].astype(F32) for n in range(4)]
        o_ref[...] = ((p[0] + p[1]) + p[2]) + p[3]

    grid_spec = pltpu.PrefetchScalarGridSpec(
        num_scalar_prefetch=1, grid=(rows // bt,),
        in_specs=[pl.BlockSpec((4, bt, width), lambda i, s: (0, i, 0))],
        out_specs=pl.BlockSpec((None, bt, width), lambda i, s: (s[0], i, 0)))
    return pl.pallas_call(
        body, name=name, grid_spec=grid_spec,
        out_shape=jax.ShapeDtypeStruct((2, rows, width), F32), compiler_params=_params("parallel"),
    )(jnp.reshape(slot, (1,)).astype(jnp.int32), parts)


ANY = pl.BlockSpec(memory_space=pl.ANY)
CHIP_FLIPS = ((1, 0), (0, 1), (1, 1))


def _place():
    return lax.axis_index("x"), lax.axis_index("y"), lax.axis_index("c")


def _flip(v, f):
    return 1 - v if f else v


def _allgather_chips(slabs):
    _, rows, width = slabs.shape
    half = rows // 2

    def body(_, out_ref, send_sems, recv_sems):
        x, y, c = _place()
        me = 2 * x + y
        sibling = (x, y, 1 - c)
        my_rows = pl.ds(pl.multiple_of(c * half, 8), half)
        sib_rows = pl.ds(pl.multiple_of((1 - c) * half, 8), half)
        first, passed = [], []
        for n, (fx, fy) in enumerate(CHIP_FLIPS):
            px, py = _flip(x, fx), _flip(y, fy)
            peer = 2 * px + py
            first.append(pltpu.make_async_remote_copy(
                src_ref=out_ref.at[me, my_rows], dst_ref=out_ref.at[me, my_rows],
                send_sem=send_sems.at[n], recv_sem=recv_sems.at[n], device_id=(px, py, c), device_id_type=MESH))
            passed.append(pltpu.make_async_remote_copy(
                src_ref=out_ref.at[peer, my_rows], dst_ref=out_ref.at[peer, my_rows],
                send_sem=send_sems.at[3 + n], recv_sem=recv_sems.at[3 + n], device_id=sibling, device_id_type=MESH))
        for cp in first:
            cp.start()
        for n, (fx, fy) in enumerate(CHIP_FLIPS):
            peer = 2 * _flip(x, fx) + _flip(y, fy)
            pltpu.make_async_remote_copy(
                src_ref=out_ref.at[me, my_rows], dst_ref=out_ref.at[peer, my_rows],
                send_sem=send_sems.at[n], recv_sem=recv_sems.at[n], device_id=sibling, device_id_type=MESH).wait_recv()
            passed[n].start()
        for n, (fx, fy) in enumerate(CHIP_FLIPS):
            peer = 2 * _flip(x, fx) + _flip(y, fy)
            pltpu.make_async_remote_copy(
                src_ref=out_ref.at[me, sib_rows], dst_ref=out_ref.at[peer, sib_rows],
                send_sem=send_sems.at[3 + n], recv_sem=recv_sems.at[3 + n], device_id=sibling,
                device_id_type=MESH).wait_recv()
        for cp in first + passed:
            cp.wait_send()

    return pl.pallas_call(
        body, name="allgather_weights",
        in_specs=[ANY], out_specs=ANY,
        out_shape=jax.ShapeDtypeStruct(slabs.shape, slabs.dtype), input_output_aliases={0: 0},
        scratch_shapes=[pltpu.SemaphoreType.DMA((6,)), pltpu.SemaphoreType.DMA((6,))],
    )(slabs)


def _swap_halves(g):
    _, rows, width = g.shape
    half = rows // 2

    def body(g_ref, got_ref, send_sem, recv_sem):
        x, y, c = _place()
        away = pl.ds(pl.multiple_of((1 - c) * half, 8), half)
        cp = pltpu.make_async_remote_copy(
            src_ref=g_ref.at[:, away], dst_ref=got_ref, send_sem=send_sem, recv_sem=recv_sem,
            device_id=(x, y, 1 - c), device_id_type=MESH)
        cp.start()
        cp.wait()

    return pl.pallas_call(
        body, name="grad_swap_halves",
        in_specs=[ANY], out_specs=ANY,
        out_shape=jax.ShapeDtypeStruct((4, half, width), g.dtype),
        scratch_shapes=[pltpu.SemaphoreType.DMA, pltpu.SemaphoreType.DMA],
    )(g)


def _scatter_chips(s):
    _, rows, width = s.shape

    def body(s_ref, out_ref, send_sems, recv_sems, local_sem):
        x, y, c = _place()
        me = 2 * x + y
        mine = pltpu.make_async_copy(s_ref.at[me], out_ref.at[me], local_sem)
        mine.start()
        copies = []
        for n, (fx, fy) in enumerate(CHIP_FLIPS):
            px, py = _flip(x, fx), _flip(y, fy)
            copies.append(pltpu.make_async_remote_copy(
                src_ref=s_ref.at[2 * px + py], dst_ref=out_ref.at[me],
                send_sem=send_sems.at[n], recv_sem=recv_sems.at[n], device_id=(px, py, c), device_id_type=MESH))
        for cp in copies:
            cp.start()
        for n, (fx, fy) in enumerate(CHIP_FLIPS):
            peer = 2 * _flip(x, fx) + _flip(y, fy)
            pltpu.make_async_remote_copy(
                src_ref=s_ref.at[me], dst_ref=out_ref.at[peer],
                send_sem=send_sems.at[n], recv_sem=recv_sems.at[n], device_id=(x, y, c), device_id_type=MESH).wait_recv()
        for cp in copies:
            cp.wait_send()
        mine.wait()

    return pl.pallas_call(
        body, name="grad_scatter_chips",
        in_specs=[ANY], out_specs=ANY,
        out_shape=jax.ShapeDtypeStruct((4, rows, width), s.dtype),
        scratch_shapes=[pltpu.SemaphoreType.DMA((3,)), pltpu.SemaphoreType.DMA((3,)), pltpu.SemaphoreType.DMA],
    )(s)


def _join_halves(halves):
    def body(_, out_ref, send_sem, recv_sem):
        x, y, c = _place()
        cp = pltpu.make_async_remote_copy(
            src_ref=out_ref.at[c], dst_ref=out_ref.at[c], send_sem=send_sem, recv_sem=recv_sem,
            device_id=(x, y, 1 - c), device_id_type=MESH)
        cp.start()
        pltpu.make_async_remote_copy(
            src_ref=out_ref.at[c], dst_ref=out_ref.at[1 - c], send_sem=send_sem, recv_sem=recv_sem,
            device_id=(x, y, 1 - c), device_id_type=MESH).wait_recv()
        cp.wait_send()

    return pl.pallas_call(
        body, name="grad_join_halves",
        in_specs=[ANY], out_specs=ANY,
        out_shape=jax.ShapeDtypeStruct(halves.shape, halves.dtype), input_output_aliases={0: 0},
        scratch_shapes=[pltpu.SemaphoreType.DMA, pltpu.SemaphoreType.DMA],
    )(halves)


PACK_W = 1024
REPLICATED = ("g_mix", "g_mlp", "g_cq", "g_ckv", "g_q_mla", "g_k_mla", "g_q_fox", "g_k_fox", "b_forget")
WEIGHT_ORDER = ("meta_tokens", "g_mix", "g_mlp", "w_in_attn", "g_cq", "w_uq", "g_ckv", "w_ukv", "g_q_mla", "g_k_mla",
                "g_q_fox", "g_k_fox", "b_forget", "w_out_attn", "w_in_conv", "conv_w", "w_out_conv", "w_mlp_up",
                "w_mlp_down")
N_EVEN = 2
N_ODD = 2
SHARD_IN = ATTN_IN // 4
SHARD_MIX = D_MODEL // 4
SHARD_UQ = HEADS * MLA_QK // 4
SHARD_UKV = HEADS * (MLA_NOPE + HEAD_V) // 4
SHARD_CONV = 3 * D_MODEL // 4
SIDE_W = 256
PK_UP = (0, 0)
PK_DOWN = (4096, 0)
PK_CONV_IN = (8192, 0)
PK_ATTN_IN = (10240, 0)
PK_OUT_ATTN = (12288, 0)
PK_OUT_CONV = (12800, 0)
PK_SMALL = (8192, 768)
PK_UQ = (10240, 768)
PK_UKV = (11008, 768)
PK_ROWS = 13312
SMALL_ROWS = 64
SMALL_META = 0
SMALL_CONV = 16
SMALL_REP = 24
SMALL_BITS_ROWS = 48
MATRIX_PLACES = (("w_mlp_up", PK_UP), ("w_mlp_down", PK_DOWN), ("w_in_conv", PK_CONV_IN), ("w_in_attn", PK_ATTN_IN),
                 ("w_out_attn", PK_OUT_ATTN), ("w_out_conv", PK_OUT_CONV), ("w_uq", PK_UQ), ("w_ukv", PK_UKV))


def _put(buf, x, place, *, name, slab=None):
    row_off, col_off = place
    slabs = x.ndim == 3
    rows, w = x.shape[-2:]
    wpad = _round_up(w, LANES)
    assert col_off % wpad == 0
    bt = _common_tile(rows, row_off, align=16)

    def fill(x_ref, o_ref):
        v = x_ref[...].astype(o_ref.dtype)
        if wpad != w:
            v = jnp.concatenate([v, jnp.zeros((bt, wpad - w), o_ref.dtype)], axis=1)
        o_ref[...] = v

    def body(x_ref, _, o_ref):
        fill(x_ref, o_ref)

    if slab is not None:
        grid_spec = pltpu.PrefetchScalarGridSpec(
            num_scalar_prefetch=1, grid=(rows // bt,),
            in_specs=[pl.BlockSpec((bt, w), lambda i, s: (i, 0)), ANY],
            out_specs=pl.BlockSpec((None, bt, wpad), lambda i, s: (s[0], row_off // bt + i, col_off // wpad)))
        return pl.pallas_call(
            lambda s_ref, x_ref, _, o_ref: fill(x_ref, o_ref), name=name, grid_spec=grid_spec,
            out_shape=jax.ShapeDtypeStruct(buf.shape, buf.dtype), input_output_aliases={2: 0},
            compiler_params=_params("parallel"),
        )(jnp.reshape(slab, (1,)).astype(jnp.int32), x, buf)
    if slabs:
        grid = (4, rows // bt)
        x_spec = pl.BlockSpec((None, bt, w), lambda s, i: (s, i, 0))
        o_spec = pl.BlockSpec((None, bt, wpad), lambda s, i: (s, row_off // bt + i, col_off // wpad))
        sem = ("parallel", "parallel")
    else:
        grid = (rows // bt,)
        x_spec = pl.BlockSpec((bt, w), lambda i: (i, 0))
        o_spec = pl.BlockSpec((bt, wpad), lambda i: (row_off // bt + i, col_off // wpad))
        sem = ("parallel",)
    return pl.pallas_call(
        body, name=name, grid=grid, in_specs=[x_spec, ANY], out_specs=o_spec,
        out_shape=jax.ShapeDtypeStruct(buf.shape, buf.dtype), input_output_aliases={1: 0},
        compiler_params=_params(*sem),
    )(x, buf)


def _w_cols(place, layer, rows, width):
    base = (place[0] + layer * rows) // rows
    return dict(n=4 * width, tn=width, tk=rows, spec=pl.BlockSpec((None, rows, width), lambda i, j, k: (j, base, 0)))


def _w_cols_t(place, layer, rows, width):
    base = (place[0] + layer * rows) // rows
    return dict(n=rows, tn=rows, tk=width, spec=pl.BlockSpec((None, rows, width), lambda i, j, k: (k, base, 0)))


def _w_rows(place, layer, rows):
    base = (place[0] + layer * rows) // rows
    return dict(n=D_MODEL, tn=D_MODEL, tk=rows, spec=pl.BlockSpec((None, rows, D_MODEL), lambda i, j, k: (k, base, 0)))


def _w_rows_t(place, layer, rows):
    base = (place[0] + layer * rows) // rows
    return dict(n=4 * rows, tn=rows, tk=D_MODEL, spec=pl.BlockSpec((None, rows, D_MODEL), lambda i, j, k: (j, base, 0)))


def _g_cols(g, place, layer, rows, width):
    base = (place[0] + layer * rows) // rows
    return g, pl.BlockSpec((None, rows, width), lambda i, j, k: (j, base, 0))


def _g_rows(g, place, layer, rows):
    base = (place[0] + layer * rows) // rows
    return g, pl.BlockSpec((None, rows, D_MODEL), lambda i, j, k: (i, base, 0))


IN_PADW = _round_up(SHARD_IN, LANES)
IN_TAIL = ZA_FQ - SHARD_IN
IN_FL = SHARD_IN - HEADS
ZA_KPE = ZA_LAST + KPE_LANE


def _assemble_attn_in(gathered, layer):
    bt = 256
    base = (PK_ATTN_IN[0] + layer * D_MODEL) // bt
    assert 2 * SHARD_IN > ZA_FQ + MLA_ROPE and 3 * SHARD_IN < ATTN_IN - HEADS

    def body(s0, s1, s2, s3, o_ref):
        dt = o_ref.dtype
        z = lambda n: jnp.zeros((bt, n), dt)
        o_ref[...] = jnp.concatenate(
            [s0[:, :SHARD_IN], s1[:, :IN_TAIL], s1[:, IN_TAIL + MLA_ROPE:SHARD_IN], s2[:, :SHARD_IN], s3[:, :IN_FL],
             s3[:, IN_FL:SHARD_IN], z(KPE_LANE - HEADS), s1[:, IN_TAIL:IN_TAIL + MLA_ROPE],
             z(LANES - KPE_LANE - MLA_ROPE)], axis=1).astype(dt)

    def spec(s):
        return pl.BlockSpec((None, bt, IN_PADW), lambda i: (s, base + i, 0))

    return pl.pallas_call(
        body, name="assemble_attn_in", grid=(D_MODEL // bt,),
        in_specs=[spec(s) for s in range(4)], out_specs=pl.BlockSpec((bt, ZA_W), lambda i: (i, 0)),
        out_shape=jax.ShapeDtypeStruct((D_MODEL, ZA_W), MXU_DTYPE), compiler_params=_params("parallel"),
    )(gathered, gathered, gathered, gathered)


def _scatter_attn_in(g, dwa, layer):
    bt = 256
    base = (PK_ATTN_IN[0] + layer * D_MODEL) // bt
    fq1 = ZA_FQ + SHARD_IN - IN_TAIL - MLA_ROPE

    def body(d_ref, _, o_ref):
        pad = jnp.zeros((bt, IN_PADW - SHARD_IN), F32)
        pieces = (
            (d_ref[:, 0:SHARD_IN],),
            (d_ref[:, SHARD_IN:ZA_FQ], d_ref[:, ZA_KPE:ZA_KPE + MLA_ROPE], d_ref[:, ZA_FQ:fq1]),
            (d_ref[:, fq1:fq1 + SHARD_IN],),
            (d_ref[:, fq1 + SHARD_IN:ZA_LAST], d_ref[:, ZA_LAST:ZA_LAST + HEADS]),
        )
        for s in range(4):
            @pl.when(pl.program_id(0) == s)
            def _(s=s):
                o_ref[...] = jnp.concatenate(list(pieces[s]) + [pad], axis=1)

    return pl.pallas_call(
        body, name="scatter_attn_in", grid=(4, D_MODEL // bt),
        in_specs=[pl.BlockSpec((bt, ZA_W), lambda s, i: (i, 0)), ANY],
        out_specs=pl.BlockSpec((None, bt, IN_PADW), lambda s, i: (s, base + i, 0)),
        out_shape=jax.ShapeDtypeStruct(g.shape, g.dtype), input_output_aliases={1: 0},
        compiler_params=_params("parallel", "parallel"),
    )(dwa, g)


def _assemble_uq(gathered, layer):
    bt = 128
    base = (PK_UQ[0] + layer * Q_LORA) // bt
    col = PK_UQ[1] // SIDE_W

    def body(s0, s1, s2, s3, o_ref):
        dt = o_ref.dtype
        z = jnp.zeros((bt, LANES - MLA_QK), dt)
        parts = []
        for s_ref in (s0, s1, s2, s3):
            parts += [s_ref[:, 0:MLA_QK], z, s_ref[:, MLA_QK:2 * MLA_QK], z]
        o_ref[...] = jnp.concatenate(parts, axis=1).astype(dt)

    def spec(s):
        return pl.BlockSpec((None, bt, SIDE_W), lambda i: (s, base + i, col))

    return pl.pallas_call(
        body, name="assemble_uq", grid=(Q_LORA // bt,),
        in_specs=[spec(s) for s in range(4)], out_specs=pl.BlockSpec((bt, HEADS * LANES), lambda i: (i, 0)),
        out_shape=jax.ShapeDtypeStruct((Q_LORA, HEADS * LANES), MXU_DTYPE), compiler_params=_params("parallel"),
    )(gathered, gathered, gathered, gathered)


def _scatter_uq(g, dw, layer):
    bt = 128
    base = (PK_UQ[0] + layer * Q_LORA) // bt
    col = PK_UQ[1] // SIDE_W

    def body(d_ref, _, o_ref):
        o_ref[...] = jnp.concatenate([d_ref[:, 0:MLA_QK], d_ref[:, LANES:LANES + MLA_QK],
                                      jnp.zeros((bt, SIDE_W - 2 * MLA_QK), F32)], axis=1)

    return pl.pallas_call(
        body, name="scatter_uq", grid=(4, Q_LORA // bt),
        in_specs=[pl.BlockSpec((bt, 2 * LANES), lambda s, i: (i, s)), ANY],
        out_specs=pl.BlockSpec((None, bt, SIDE_W), lambda s, i: (s, base + i, col)),
        out_shape=jax.ShapeDtypeStruct(g.shape, g.dtype), input_output_aliases={1: 0},
        compiler_params=_params("parallel", "parallel"),
    )(dw, g)


def _assemble_ukv(gathered, layer):
    bt = KV_LORA
    base = (PK_UKV[0] + layer * KV_LORA) // bt
    col = PK_UKV[1] // SIDE_W
    hd = MLA_NOPE + HEAD_V

    def body(s0, s1, s2, s3, o_ref):
        dt = o_ref.dtype
        z = jnp.zeros((bt, LANES - MLA_NOPE), dt)
        keys, vals = [], []
        for s_ref in (s0, s1, s2, s3):
            for e in range(2):
                keys += [s_ref[:, e * hd:e * hd + MLA_NOPE], z]
                vals.append(s_ref[:, e * hd + MLA_NOPE:(e + 1) * hd])
        o_ref[...] = jnp.concatenate(keys + vals, axis=1).astype(dt)

    def spec(s):
        return pl.BlockSpec((None, bt, SIDE_W), lambda i: (s, base + i, col))

    return pl.pallas_call(
        body, name="assemble_ukv", grid=(1,),
        in_specs=[spec(s) for s in range(4)],
        out_specs=pl.BlockSpec((bt, HEADS * (LANES + HEAD_V)), lambda i: (i, 0)),
        out_shape=jax.ShapeDtypeStruct((KV_LORA, HEADS * (LANES + HEAD_V)), MXU_DTYPE), compiler_params=_params("parallel"),
    )(gathered, gathered, gathered, gathered)


def _scatter_ukv(g, dw, layer):
    bt = KV_LORA
    base = (PK_UKV[0] + layer * KV_LORA) // bt
    col = PK_UKV[1] // SIDE_W

    def body(k_ref, v_ref, _, o_ref):
        o_ref[...] = jnp.concatenate([k_ref[:, 0:MLA_NOPE], v_ref[:, 0:HEAD_V], k_ref[:, LANES:LANES + MLA_NOPE],
                                      v_ref[:, HEAD_V:2 * HEAD_V]], axis=1)

    return pl.pallas_call(
        body, name="scatter_ukv", grid=(4,),
        in_specs=[pl.BlockSpec((bt, 2 * LANES), lambda s: (0, s)),
                  pl.BlockSpec((bt, 2 * HEAD_V), lambda s: (0, HEADS * LANES // (2 * HEAD_V) + s)), ANY],
        out_specs=pl.BlockSpec((None, bt, SIDE_W), lambda s: (s, base, col)),
        out_shape=jax.ShapeDtypeStruct(g.shape, g.dtype), input_output_aliases={2: 0},
        compiler_params=_params("parallel"),
    )(dw, dw, g)


def _pad_lanes(v, n=LANES):
    return jnp.pad(v, (0, n - v.shape[0])).reshape(1, n)


def _relu2_up(acc):
    r = jnp.maximum(acc, 0.0)
    return acc, r * r


def _relu2_bwd(acc, u):
    return (acc * (2.0 * jnp.maximum(u, 0.0)),)


def _add_res(acc, res):
    return (acc + res,)


def _add_res_norm(acc, res, g):
    h = acc + res
    return h, h * lax.rsqrt(jnp.mean(h * h, axis=-1, keepdims=True) + EPS) * g


def _local_step(x, target, meta, small, gathered):
    seq = x.shape[0]
    t = seq + LANES
    d = D_MODEL
    h = jnp.concatenate([jnp.zeros((PAD, d), F32), meta.astype(F32), x], axis=0)
    c_tab, s_tab = _rope_tables(t)
    scale_mla, scale_fox = MLA_QK ** -0.5 * LOG2E, FOX_DIM ** -0.5 * LOG2E
    grads = {}
    saved = []
    g = jnp.zeros((4, PK_ROWS, PACK_W), F32)

    hn = _rmsnorm_fwd(h, small["g_mix"][0])
    for layer in range(DEPTH):
        j = layer // 2
        sv = {"h_in": h}
        sv["hn"] = hn
        g_mlp_row = small["g_mlp"][layer].reshape(1, d)
        if layer % 2 == 0:
            w_in = _assemble_attn_in(gathered, j)
            w_uq = _assemble_uq(gathered, j)
            w_ukv = _assemble_ukv(gathered, j)
            out_place = PK_OUT_ATTN
            vecs = dict(
                g_cq=small["g_cq"][j].reshape(1, Q_LORA), g_ckv=small["g_ckv"][j].reshape(1, KV_LORA),
                g_qf=jnp.tile(small["g_q_fox"][j] * scale_fox, 2).reshape(1, LANES),
                g_kf=jnp.tile(small["g_k_fox"][j], 2).reshape(1, LANES),
                b_f=_pad_lanes(small["b_forget"][j]), g_q=_pad_lanes(small["g_q_mla"][j] * scale_mla),
                g_k=_pad_lanes(small["g_k_mla"][j]))
            z = _matmul(hn, w_in, name="mm_attn_in")
            cqn, ckvn, qf, kf, vf, logf = _prep_a_fwd(z, vecs["g_cq"], vecs["g_ckv"], vecs["g_qf"], vecs["g_kf"], vecs["b_f"])
            f_cum, f_cum_t = _cumsum_rows(logf, reverse=False, name="cumsum_fwd", out_scale=LOG2E)
            q_raw = _matmul(cqn, w_uq, name="mm_uq")
            kv_raw = _matmul(ckvn, w_ukv, name="mm_ukv")
            qn, kn, v_mla = _prep_b_fwd(q_raw, kv_raw, z, c_tab, s_tab, vecs["g_q"], vecs["g_k"])
            o_mla, lse_mla = _flash_fwd_chunked(qn, kn, v_mla, None, None, mla=True, name="flash_fwd_mla")
            o_fox, lse_fox = _flash_fwd_chunked(qf, kf, vf, f_cum, f_cum_t, mla=False, name="flash_fwd_fox")
            o = jnp.concatenate([o_mla, o_fox], axis=1)
            h, hn2 = _matmul(o, gathered, b_tiles=_w_rows(out_place, j, SHARD_MIX), extras=(h, g_mlp_row),
                             epilogue=_add_res_norm, out_dtypes=(F32, MXU_DTYPE), name="mm_mix_out")
            sv.update(w_in=w_in, w_uq=w_uq, w_ukv=w_ukv, out_place=out_place, vecs=vecs, z=z, cqn=cqn, ckvn=ckvn, qf=qf, kf=kf,
                      vf=vf, f_cum=f_cum, f_cum_t=f_cum_t, q_raw=q_raw, kv_raw=kv_raw, qn=qn, kn=kn, v_mla=v_mla, o=o,
                      lse_mla=lse_mla, lse_fox=lse_fox)
        else:
            out_place = PK_OUT_CONV
            conv_w8 = jnp.pad(small["conv_w"][j], ((0, 5), (0, 0)))
            z = _matmul(hn, gathered, b_tiles=_w_cols(PK_CONV_IN, j, d, SHARD_CONV), name="mm_conv_in")
            vmix = _conv_fwd(z, conv_w8)
            h, hn2 = _matmul(vmix, gathered, b_tiles=_w_rows(out_place, j, SHARD_MIX), extras=(h, g_mlp_row),
                             epilogue=_add_res_norm, out_dtypes=(F32, MXU_DTYPE), name="mm_mix_out")
            sv.update(out_place=out_place, conv_w8=conv_w8, z=z, vmix=vmix)
        sv["h_mid"] = h
        u, a = _matmul(hn2, gathered, b_tiles=_w_cols(PK_UP, layer, d, d), epilogue=_relu2_up,
                       out_dtypes=(F32, MXU_DTYPE), name="mm_mlp_up")
        if layer + 1 < DEPTH:
            h, hn = _matmul(a, gathered, b_tiles=_w_rows(PK_DOWN, layer, d),
                            extras=(h, small["g_mix"][layer + 1].reshape(1, d)), epilogue=_add_res_norm,
                            out_dtypes=(F32, MXU_DTYPE), name="mm_mlp_down")
        else:
            h = _matmul(a, gathered, b_tiles=_w_rows(PK_DOWN, layer, d), extras=(h,), epilogue=_add_res,
                        name="mm_mlp_down")
        sv.update(hn2=hn2, u=u, a=a)
        saved.append(sv)

    dh, loss_local = _loss_head(h, target)

    dg_mix, dg_mlp = [None] * DEPTH, [None] * DEPTH
    per_even = {k: [None, None] for k in ("g_cq", "g_ckv", "g_q_mla", "g_k_mla", "g_q_fox", "g_k_fox", "b_forget")}
    per_odd = {"conv_w": [None, None]}
    for layer in reversed(range(DEPTH)):
        j = layer // 2
        sv = saved[layer]
        du = _matmul(dh, gathered, tb=True, b_tiles=_w_rows_t(PK_DOWN, layer, d), extras=(sv["u"],),
                     epilogue=_relu2_bwd, out_dtypes=(MXU_DTYPE,), name="mm_mlp_da")
        g = _matmul(sv["a"], dh, ta=True, out_into=_g_rows(g, PK_DOWN, layer, d), name="mm_dw_down")
        g = _matmul(sv["hn2"], du, ta=True, out_into=_g_cols(g, PK_UP, layer, d, d), name="mm_dw_up")
        dhn2 = _matmul(du, gathered, tb=True, b_tiles=_w_cols_t(PK_UP, layer, d, d), name="mm_mlp_dhn")
        dh, dg_mlp[layer] = _rmsnorm_bwd(sv["h_mid"], small["g_mlp"][layer], dhn2, dh)
        do = _matmul(dh, gathered, tb=True, b_tiles=_w_rows_t(sv["out_place"], j, SHARD_MIX), out_dtypes=(MXU_DTYPE,),
                     name="mm_mix_do")
        if layer % 2 == 0:
            vecs = sv["vecs"]
            g = _matmul(sv["o"], dh, ta=True, tm=SHARD_MIX, out_into=_g_rows(g, PK_OUT_ATTN, j, SHARD_MIX),
                        name="mm_dw_out")
            delta_t = _delta_rows(do, sv["o"])
            dqn, dkn, dv_mla = _flash_bwd_fused(sv["qn"], sv["kn"], sv["v_mla"], do, sv["lse_mla"], delta_t, None, None,
                                                mla=True, col0=0, name="flash_bwd_mla")
            dqf, dkf, dvf, rs_t, cs = _flash_bwd_fused(sv["qf"], sv["kf"], sv["vf"], do, sv["lse_fox"], delta_t,
                                                       sv["f_cum"], sv["f_cum_t"], mla=False, col0=HEADS // 2,
                                                       name="flash_bwd_fox")
            d_f = rs_t.reshape(HEADS // 2, 8, t)[:, :2, :].reshape(HEADS, t).T
            d_f = d_f - cs.reshape(t, HEADS // 2, LANES)[:, :, :2].reshape(t, HEADS)
            d_f = jnp.pad(d_f, ((0, 0), (0, LANES - HEADS)))
            dlogf, _ = _cumsum_rows(d_f, reverse=True, name="cumsum_bwd")
            dq_raw, dkv_raw, dkpe, dg_q, dg_k = _prep_b_bwd(sv["q_raw"], sv["kv_raw"], sv["z"], c_tab, s_tab, vecs["g_q"],
                                                            vecs["g_k"], dqn, dkn, dv_mla)
            g = _scatter_uq(g, _matmul(sv["cqn"], dq_raw, ta=True, name="mm_dw_uq"), j)
            g = _scatter_ukv(g, _matmul(sv["ckvn"], dkv_raw, ta=True, name="mm_dw_ukv"), j)
            dcqn = _matmul(dq_raw, sv["w_uq"], tb=True, name="mm_dcqn")
            dckvn = _matmul(dkv_raw, sv["w_ukv"], tb=True, name="mm_dckvn")
            dz, dg_cq, dg_ckv, dg_qf, dg_kf, db_f = _prep_a_bwd(
                sv["z"], vecs["g_cq"], vecs["g_ckv"], vecs["g_qf"], vecs["g_kf"], vecs["b_f"], dcqn, dckvn, dqf, dkf, dvf,
                dlogf, dkpe)
            g = _scatter_attn_in(g, _matmul(sv["hn"], dz, ta=True, name="mm_dw_attn_in"), j)
            per_even["g_cq"][j] = dg_cq[0]
            per_even["g_ckv"][j] = dg_ckv[0]
            per_even["g_q_mla"][j] = dg_q[0, :MLA_QK] * scale_mla
            per_even["g_k_mla"][j] = dg_k[0, :MLA_QK]
            per_even["g_q_fox"][j] = (dg_qf[0, :FOX_DIM] + dg_qf[0, FOX_DIM:]) * scale_fox
            per_even["g_k_fox"][j] = dg_kf[0, :FOX_DIM] + dg_kf[0, FOX_DIM:]
            per_even["b_forget"][j] = db_f[0, :HEADS]
            dhn = _matmul(dz, sv["w_in"], tb=True, name="mm_attn_dhn")
        else:
            g = _matmul(sv["vmix"], dh, ta=True, tm=SHARD_MIX, out_into=_g_rows(g, PK_OUT_CONV, j, SHARD_MIX),
                        name="mm_dw_out")
            dz, dcw = _conv_bwd(sv["z"], sv["conv_w8"], do)
            per_odd["conv_w"][j] = dcw[:3]
            g = _matmul(sv["hn"], dz, ta=True, tn=SHARD_CONV, out_into=_g_cols(g, PK_CONV_IN, j, d, SHARD_CONV),
                        name="mm_dw_conv_in")
            dhn = _matmul(dz, gathered, tb=True, b_tiles=_w_cols_t(PK_CONV_IN, j, d, SHARD_CONV), name="mm_conv_dhn")
        dh, dg_mix[layer] = _rmsnorm_bwd(sv["h_in"], small["g_mix"][layer], dhn, dh)

    grads["meta_tokens"] = dh[PAD:LANES]
    grads["g_mix"] = jnp.stack(dg_mix)
    grads["g_mlp"] = jnp.stack(dg_mlp)
    for k, v in list(per_even.items()) + list(per_odd.items()):
        grads[k] = jnp.stack(v)
    return loss_local, dh[LANES:], g, grads


def kernel(x, meta_tokens, g_mix, g_mlp, w_in_attn, g_cq, w_uq, g_ckv, w_ukv, g_q_mla, g_k_mla, g_q_fox, g_k_fox, b_forget, w_out_attn, w_in_conv, conv_w, w_out_conv, w_mlp_up, w_mlp_down, loss_target, m_meta_tokens, m_g_mix, m_g_mlp, m_w_in_attn, m_g_cq, m_w_uq, m_g_ckv, m_w_ukv, m_g_q_mla, m_g_k_mla, m_g_q_fox, m_g_k_fox, m_b_forget, m_w_out_attn, m_w_in_conv, m_conv_w, m_w_out_conv, m_w_mlp_up, m_w_mlp_down, v_meta_tokens, v_g_mix, v_g_mlp, v_w_in_attn, v_g_cq, v_w_uq, v_g_ckv, v_w_ukv, v_g_q_mla, v_g_k_mla, v_g_q_fox, v_g_k_fox, v_b_forget, v_w_out_attn, v_w_in_conv, v_conv_w, v_w_out_conv, v_w_mlp_up, v_w_mlp_down):
    args = dict(locals())
    weights = {n: args[n] for n in WEIGHT_ORDER}
    mom_m = {n: args["m_" + n] for n in WEIGHT_ORDER}
    mom_v = {n: args["v_" + n] for n in WEIGHT_ORDER}

    wire = jnp.bfloat16
    me = 2 * lax.axis_index("x") + lax.axis_index("y")
    buf = jnp.zeros((4, PK_ROWS, PACK_W), wire)
    for name, place in MATRIX_PLACES:
        w = weights[name]
        buf = _put(buf, w.reshape(-1, w.shape[-1]), place, name="pack_weights", slab=me)
    meta_bits = lax.bitcast_convert_type(meta_tokens, wire).reshape(2 * N_META, SIDE_W)
    conv_bits = lax.bitcast_convert_type(conv_w, wire).reshape(2 * N_ODD * 3, SIDE_W)
    bits = jnp.concatenate([meta_bits, conv_bits, jnp.zeros((SMALL_BITS_ROWS - 2 * N_META - 2 * N_ODD * 3, SIDE_W), wire)])
    buf = _put(buf, bits, PK_SMALL, name="pack_weights", slab=me)
    gathered = _allgather_chips(buf)
    got_bits = gathered[:, PK_SMALL[0]:PK_SMALL[0] + SMALL_BITS_ROWS, PK_SMALL[1]:PK_SMALL[1] + SIDE_W]
    meta_full = lax.bitcast_convert_type(got_bits[:, :2 * N_META].reshape(4, N_META, SIDE_W, 2), F32)
    meta_full = meta_full.transpose(1, 0, 2).reshape(N_META, D_MODEL)
    conv_full = lax.bitcast_convert_type(
        got_bits[:, 2 * N_META:2 * N_META + 2 * N_ODD * 3].reshape(4, N_ODD, 3, SIDE_W, 2), F32)
    small = {n: weights[n] for n in REPLICATED}
    small["conv_w"] = conv_full.transpose(1, 2, 0, 3).reshape(N_ODD, 3, D_MODEL)

    loss_local, grad_x, g, grads = _local_step(x[0], loss_target[0], meta_full, small, gathered)
    loss = lax.psum(loss_local, MESH_AXES)

    rep = jnp.concatenate([grads[n].reshape(-1) for n in REPLICATED])
    rep = jnp.pad(rep, (0, (SMALL_ROWS - SMALL_REP) * SIDE_W - rep.shape[0])).reshape(SMALL_ROWS - SMALL_REP, SIDE_W)
    g_meta = grads["meta_tokens"].reshape(N_META, 4, SIDE_W).transpose(1, 0, 2)
    g_conv = grads["conv_w"].reshape(N_ODD * 3, 4, SIDE_W).transpose(1, 0, 2)
    small4 = jnp.concatenate([g_meta, g_conv, jnp.zeros((4, SMALL_REP - SMALL_CONV - N_ODD * 3, SIDE_W), F32),
                              jnp.broadcast_to(rep[None], (4,) + rep.shape)], axis=1)
    g = _put(g, small4, PK_SMALL, name="pack_small_grads")
    half = PK_ROWS // 2
    c = lax.axis_index("c")
    got = _swap_halves(g)
    pair = _add_half(g, got, c, out_dtype=jnp.bfloat16, name="grad_pair_sum")
    total = _sum4(_scatter_chips(pair), c, name="grad_chip_sum")
    g_tot = _join_halves(total).reshape(PK_ROWS, PACK_W)

    out = {}
    for name, place in MATRIX_PLACES:
        shape = weights[name].shape
        two_d = lambda a: a.reshape(-1, shape[-1])
        res = _adamw(two_d(weights[name]), two_d(mom_m[name]), two_d(mom_v[name]), g_tot, place[0], place[1])
        out[name] = [r.reshape(shape) for r in res]

    def small_pack(src):
        flat = jnp.concatenate([src[n].reshape(-1) for n in REPLICATED])
        flat = jnp.pad(flat, (0, (SMALL_ROWS - SMALL_REP) * SIDE_W - flat.shape[0])).reshape(SMALL_ROWS - SMALL_REP, SIDE_W)
        return jnp.concatenate([src["meta_tokens"], src["conv_w"].reshape(N_ODD * 3, SIDE_W),
                                jnp.zeros((SMALL_REP - SMALL_CONV - N_ODD * 3, SIDE_W), F32), flat])

    res = _adamw(small_pack(weights), small_pack(mom_m), small_pack(mom_v), g_tot, PK_SMALL[0], PK_SMALL[1])
    for name in ("meta_tokens", "conv_w") + REPLICATED:
        out[name] = []
    for r in res:
        out["meta_tokens"].append(r[SMALL_META:SMALL_META + N_META])
        out["conv_w"].append(r[SMALL_CONV:SMALL_CONV + N_ODD * 3].reshape(N_ODD, 3, SIDE_W))
        flat, off = r[SMALL_REP:].reshape(-1), 0
        for name in REPLICATED:
            n = weights[name].size
            out[name].append(flat[off:off + n].reshape(weights[name].shape))
            off += n
    return (loss, grad_x[None], *[out[n][0] for n in WEIGHT_ORDER], *[out[n][1] for n in WEIGHT_ORDER],
            *[out[n][2] for n in WEIGHT_ORDER], *[out[n][3] for n in WEIGHT_ORDER])
```

```python
import functools

import jax
import jax.numpy as jnp
import numpy as np
from jax import lax
from jax.experimental import pallas as pl
from jax.experimental.pallas import tpu as pltpu

F32 = jnp.float32
MXU_DTYPE = jnp.bfloat16

D_MODEL = 1024
N_META = 16
LANES = 128
PAD = LANES - N_META
HEADS = 8
Q_LORA = 384
KV_LORA = 256
MLA_NOPE = 64
MLA_ROPE = 32
MLA_QK = MLA_NOPE + MLA_ROPE
HEAD_V = 64
FOX_DIM = 64
ROPE_BASE = 10000.0
D_FF = 4 * D_MODEL
DEPTH = 4
EPS = 1e-6
NEG = -1e30
ATTN_IN = Q_LORA + KV_LORA + MLA_ROPE + 3 * HEADS * FOX_DIM + HEADS

ZA_CQ = 0
ZA_CKV = Q_LORA
ZA_FQ = ZA_CKV + KV_LORA
ZA_FK = ZA_FQ + HEADS * FOX_DIM
ZA_FV = ZA_FK + HEADS * FOX_DIM
ZA_LAST = ZA_FV + HEADS * FOX_DIM
ZA_W = ZA_LAST + LANES
KPE_LANE = MLA_NOPE

ADAM_LR = 0.001
ADAM_B1 = 0.9
ADAM_B2 = 0.999
ADAM_EPS = 1e-08
ADAM_WD = 0.01
ADAM_STEP = 10

VMEM_LIMIT_BYTES = 52 * 1024 * 1024
ROW_TILE = 1040
PREP_TILE = 320
ATTN_BLOCK = 640
LOSS_TILE = 128
MAX_TILE = 1536

MESH_AXES = ("x", "y", "c")
MESH = pl.DeviceIdType.MESH


def _params(*sem):
    return pltpu.CompilerParams(dimension_semantics=sem, vmem_limit_bytes=VMEM_LIMIT_BYTES)


def _tile(n, cap=None):
    cap = MAX_TILE if cap is None else cap
    if n <= cap:
        return n
    best = None
    for t in range(LANES, cap + 1, LANES):
        if n % t == 0:
            best = t
    assert best is not None, n
    return best


def _iota(shape, dim):
    return lax.broadcasted_iota(jnp.int32, shape, dim)


def _matmul(a, b, *, ta=False, tb=False, extras=(), epilogue=None, out_dtypes=(F32,), name, b_tiles=None,
            out_into=None, tm=None, tn=None):
    if ta:
        kdim, m = a.shape
    else:
        m, kdim = a.shape
    row_tile = min(ROW_TILE, m)
    if ta:
        tm_auto, tk = _tile(m), min(ATTN_BLOCK, kdim)
    else:
        tm_auto, tk = (row_tile if m % row_tile == 0 else _tile(m)), _tile(kdim)
    tm = tm_auto if tm is None else tm
    if b_tiles is None:
        n = b.shape[0] if tb else b.shape[1]
        assert (b.shape[1] if tb else b.shape[0]) == kdim, (a.shape, b.shape, ta, tb)
        tn = _tile(n) if tn is None else tn
        b_spec = pl.BlockSpec((tn, tk), lambda i, j, k: (j, k)) if tb else pl.BlockSpec((tk, tn), lambda i, j, k: (k, j))
    else:
        n, tn, tk, b_spec = b_tiles["n"], b_tiles["tn"], b_tiles["tk"], b_tiles["spec"]
    nm, nn, nk = m // tm, n // tn, kdim // tk
    assert nm * tm == m and nn * tn == n and nk * tk == kdim, (m, n, kdim, tm, tn, tk)
    n_ex, n_out = len(extras), len(out_dtypes)
    n_alias = 0 if out_into is None else 1
    assert n_out == 1 or out_into is None
    dims = (((0 if ta else 1,), (1 if tb else 0,)), ((), ()))
    if epilogue is None:
        epilogue = lambda acc: (acc,)

    def body(a_ref, b_ref, *rest):
        ex_refs, out_refs, acc_ref = rest[:n_ex], rest[n_ex + n_alias:n_ex + n_alias + n_out], rest[-1]
        k = pl.program_id(2)

        @pl.when(k == 0)
        def _():
            acc_ref[...] = jnp.zeros_like(acc_ref)

        acc_ref[...] += lax.dot_general(a_ref[...].astype(MXU_DTYPE), b_ref[...].astype(MXU_DTYPE), dims,
                                        preferred_element_type=F32)

        @pl.when(k == nk - 1)
        def _():
            res = epilogue(acc_ref[...], *[e[...] for e in ex_refs])
            for o_ref, r in zip(out_refs, res):
                o_ref[...] = r.reshape(o_ref.shape).astype(o_ref.dtype)

    a_spec = pl.BlockSpec((tk, tm), lambda i, j, k: (k, i)) if ta else pl.BlockSpec((tm, tk), lambda i, j, k: (i, k))
    mn_spec = pl.BlockSpec((tm, tn), lambda i, j, k: (i, j))
    row_spec = pl.BlockSpec((1, tn), lambda i, j, k: (0, j))
    ex_specs = [row_spec if e.shape[0] == 1 else mn_spec for e in extras]
    if out_into is None:
        outs = pl.pallas_call(
            body, name=name, grid=(nm, nn, nk),
            in_specs=[a_spec, b_spec] + ex_specs,
            out_specs=[mn_spec] * n_out,
            out_shape=[jax.ShapeDtypeStruct((m, n), dt) for dt in out_dtypes],
            scratch_shapes=[pltpu.VMEM((tm, tn), F32)],
            compiler_params=_params("parallel", "parallel", "arbitrary"),
        )(a, b, *extras)
        return outs[0] if n_out == 1 else outs
    buf, buf_spec = out_into
    return pl.pallas_call(
        body, name=name, grid=(nm, nn, nk),
        in_specs=[a_spec, b_spec] + ex_specs + [ANY],
        out_specs=buf_spec,
        out_shape=jax.ShapeDtypeStruct(buf.shape, buf.dtype),
        input_output_aliases={2 + n_ex: 0},
        scratch_shapes=[pltpu.VMEM((tm, tn), F32)],
        compiler_params=_params("parallel", "parallel", "arbitrary"),
    )(a, b, *extras, buf)


def _rmsnorm_fwd(x, g, *, name="rmsnorm_fwd"):
    t, d = x.shape
    bt = min(ROW_TILE, t)

    def body(x_ref, g_ref, o_ref):
        xv = x_ref[...]
        r = lax.rsqrt(jnp.mean(xv * xv, axis=-1, keepdims=True) + EPS)
        o_ref[...] = (xv * r * g_ref[...]).astype(o_ref.dtype)

    return pl.pallas_call(
        body, name=name, grid=(t // bt,),
        in_specs=[pl.BlockSpec((bt, d), lambda i: (i, 0)), pl.BlockSpec((1, d), lambda i: (0, 0))],
        out_specs=pl.BlockSpec((bt, d), lambda i: (i, 0)),
        out_shape=jax.ShapeDtypeStruct((t, d), MXU_DTYPE),
        compiler_params=_params("parallel"),
    )(x, g.reshape(1, d))


def _rmsnorm_bwd(x, g, dy, dres, *, name="rmsnorm_bwd"):
    t, d = x.shape
    bt = min(ROW_TILE, t)

    def body(x_ref, g_ref, dy_ref, dres_ref, dx_ref, dg_ref):
        @pl.when(pl.program_id(0) == 0)
        def _():
            dg_ref[...] = jnp.zeros_like(dg_ref)

        xv, dyv = x_ref[...], dy_ref[...].astype(F32)
        r = lax.rsqrt(jnp.mean(xv * xv, axis=-1, keepdims=True) + EPS)
        xhat = xv * r
        dxhat = dyv * g_ref[...]
        dx = r * (dxhat - xhat * jnp.mean(dxhat * xhat, axis=-1, keepdims=True))
        dx_ref[...] = dres_ref[...] + dx
        dg_ref[...] += jnp.sum(dyv * xhat, axis=0, keepdims=True)

    row = pl.BlockSpec((bt, d), lambda i: (i, 0))
    vec = pl.BlockSpec((1, d), lambda i: (0, 0))
    dx, dg = pl.pallas_call(
        body, name=name, grid=(t // bt,),
        in_specs=[row, vec, row, row], out_specs=[row, vec],
        out_shape=[jax.ShapeDtypeStruct((t, d), F32), jax.ShapeDtypeStruct((1, d), F32)],
        compiler_params=_params("arbitrary"),
    )(x, g.reshape(1, d), dy, dres)
    return dx, dg.reshape(d)


def _pair_rms(x, lo):
    x2 = x * x
    s_lo = jnp.sum(jnp.where(lo, x2, 0.0), axis=-1, keepdims=True)
    s_hi = jnp.sum(jnp.where(lo, 0.0, x2), axis=-1, keepdims=True)
    return jnp.where(lo, lax.rsqrt(s_lo / FOX_DIM + EPS), lax.rsqrt(s_hi / FOX_DIM + EPS))


def _pair_sum(x, lo):
    s_lo = jnp.sum(jnp.where(lo, x, 0.0), axis=-1, keepdims=True)
    s_hi = jnp.sum(jnp.where(lo, 0.0, x), axis=-1, keepdims=True)
    return jnp.where(lo, s_lo, s_hi)


def _prep_a_fwd(z, g_cq, g_ckv, g_qf, g_kf, b_f):
    t = z.shape[0]
    bt = min(PREP_TILE, t)
    hw = HEADS * FOX_DIM

    def body(z_ref, gcq_ref, gckv_ref, gqf_ref, gkf_ref, bf_ref, cqn_ref, ckvn_ref, qf_ref, kf_ref, vf_ref, logf_ref):
        i = pl.program_id(0)
        cq = z_ref[:, ZA_CQ:ZA_CQ + Q_LORA]
        cqn_ref[...] = (cq * lax.rsqrt(jnp.mean(cq * cq, axis=-1, keepdims=True) + EPS) * gcq_ref[...]).astype(cqn_ref.dtype)
        ckv = z_ref[:, ZA_CKV:ZA_CKV + KV_LORA]
        ckvn_ref[...] = (ckv * lax.rsqrt(jnp.mean(ckv * ckv, axis=-1, keepdims=True) + EPS) * gckv_ref[...]).astype(ckvn_ref.dtype)
        lo = _iota((1, LANES), 1) < FOX_DIM
        for p in range(HEADS // 2):
            sl = slice(p * LANES, (p + 1) * LANES)
            xq = z_ref[:, ZA_FQ + p * LANES:ZA_FQ + (p + 1) * LANES]
            qf_ref[:, sl] = (xq * _pair_rms(xq, lo) * gqf_ref[...]).astype(qf_ref.dtype)
            xk = z_ref[:, ZA_FK + p * LANES:ZA_FK + (p + 1) * LANES]
            kf_ref[:, sl] = (xk * _pair_rms(xk, lo) * gkf_ref[...]).astype(kf_ref.dtype)
        vf_ref[...] = z_ref[:, ZA_FV:ZA_FV + hw].astype(vf_ref.dtype)
        xl = z_ref[:, ZA_LAST:ZA_LAST + LANES] + bf_ref[...]
        logf = jnp.minimum(xl, 0.0) - jnp.log(1.0 + jnp.exp(-jnp.abs(xl)))
        row = i * bt + _iota((bt, LANES), 0)
        lane = _iota((bt, LANES), 1)
        logf_ref[...] = jnp.where((lane < HEADS) & (row >= PAD), logf, 0.0)

    def vec(w):
        return pl.BlockSpec((1, w), lambda i: (0, 0))

    def row(w):
        return pl.BlockSpec((bt, w), lambda i: (i, 0))

    return pl.pallas_call(
        body, name="attn_prep_a_fwd", grid=(t // bt,),
        in_specs=[row(ZA_W), vec(Q_LORA), vec(KV_LORA), vec(LANES), vec(LANES), vec(LANES)],
        out_specs=[row(Q_LORA), row(KV_LORA), row(hw), row(hw), row(hw), row(LANES)],
        out_shape=[jax.ShapeDtypeStruct((t, Q_LORA), MXU_DTYPE), jax.ShapeDtypeStruct((t, KV_LORA), MXU_DTYPE),
                   jax.ShapeDtypeStruct((t, hw), MXU_DTYPE), jax.ShapeDtypeStruct((t, hw), MXU_DTYPE),
                   jax.ShapeDtypeStruct((t, hw), MXU_DTYPE), jax.ShapeDtypeStruct((t, LANES), F32)],
        compiler_params=_params("parallel"),
    )(z, g_cq, g_ckv, g_qf, g_kf, b_f)


def _prep_a_bwd(z, g_cq, g_ckv, g_qf, g_kf, b_f, dcqn, dckvn, dqf, dkf, dvf, dlogf, dkpe):
    t = z.shape[0]
    bt = min(PREP_TILE, t)
    hw = HEADS * FOX_DIM

    def norm_bwd(x, g, dy):
        r = lax.rsqrt(jnp.mean(x * x, axis=-1, keepdims=True) + EPS)
        xhat = x * r
        dxhat = dy * g
        dx = r * (dxhat - xhat * jnp.mean(dxhat * xhat, axis=-1, keepdims=True))
        return dx, jnp.sum(dy * xhat, axis=0, keepdims=True)

    def body(z_ref, gcq_ref, gckv_ref, gqf_ref, gkf_ref, bf_ref, dcqn_ref, dckvn_ref, dqf_ref, dkf_ref, dvf_ref,
             dlogf_ref, dkpe_ref, dz_ref, dgcq_ref, dgckv_ref, dgqf_ref, dgkf_ref, dbf_ref):
        i = pl.program_id(0)

        @pl.when(i == 0)
        def _():
            for r in (dgcq_ref, dgckv_ref, dgqf_ref, dgkf_ref, dbf_ref):
                r[...] = jnp.zeros_like(r)

        dx, dg = norm_bwd(z_ref[:, ZA_CQ:ZA_CQ + Q_LORA], gcq_ref[...], dcqn_ref[...])
        dz_ref[:, ZA_CQ:ZA_CQ + Q_LORA] = dx.astype(dz_ref.dtype)
        dgcq_ref[...] += dg
        dx, dg = norm_bwd(z_ref[:, ZA_CKV:ZA_CKV + KV_LORA], gckv_ref[...], dckvn_ref[...])
        dz_ref[:, ZA_CKV:ZA_CKV + KV_LORA] = dx.astype(dz_ref.dtype)
        dgckv_ref[...] += dg
        lo = _iota((1, LANES), 1) < FOX_DIM
        for base, g_ref, dy_ref, dg_ref in ((ZA_FQ, gqf_ref, dqf_ref, dgqf_ref), (ZA_FK, gkf_ref, dkf_ref, dgkf_ref)):
            for p in range(HEADS // 2):
                x = z_ref[:, base + p * LANES:base + (p + 1) * LANES]
                dy = dy_ref[:, p * LANES:(p + 1) * LANES]
                r = _pair_rms(x, lo)
                xhat = x * r
                dxhat = dy * g_ref[...]
                dx = r * (dxhat - xhat * _pair_sum(dxhat * xhat, lo) / FOX_DIM)
                dz_ref[:, base + p * LANES:base + (p + 1) * LANES] = dx.astype(dz_ref.dtype)
                dg_ref[...] += jnp.sum(dy * xhat, axis=0, keepdims=True)
        dz_ref[:, ZA_FV:ZA_FV + hw] = dvf_ref[...].astype(dz_ref.dtype)
        xl = z_ref[:, ZA_LAST:ZA_LAST + LANES] + bf_ref[...]
        row = i * bt + _iota((bt, LANES), 0)
        lane = _iota((bt, LANES), 1)
        dfl = jnp.where((lane < HEADS) & (row >= PAD), dlogf_ref[...] / (1.0 + jnp.exp(xl)), 0.0)
        dbf_ref[...] += jnp.sum(dfl, axis=0, keepdims=True)
        dz_ref[:, ZA_LAST:ZA_LAST + LANES] = (dfl + dkpe_ref[...]).astype(dz_ref.dtype)

    def vec(w):
        return pl.BlockSpec((1, w), lambda i: (0, 0))

    def row(w):
        return pl.BlockSpec((bt, w), lambda i: (i, 0))

    return pl.pallas_call(
        body, name="attn_prep_a_bwd", grid=(t // bt,),
        in_specs=[row(ZA_W), vec(Q_LORA), vec(KV_LORA), vec(LANES), vec(LANES), vec(LANES),
                  row(Q_LORA), row(KV_LORA), row(hw), row(hw), row(hw), row(LANES), row(LANES)],
        out_specs=[row(ZA_W), vec(Q_LORA), vec(KV_LORA), vec(LANES), vec(LANES), vec(LANES)],
        out_shape=[jax.ShapeDtypeStruct((t, ZA_W), MXU_DTYPE), jax.ShapeDtypeStruct((1, Q_LORA), F32),
                   jax.ShapeDtypeStruct((1, KV_LORA), F32), jax.ShapeDtypeStruct((1, LANES), F32),
                   jax.ShapeDtypeStruct((1, LANES), F32), jax.ShapeDtypeStruct((1, LANES), F32)],
        compiler_params=_params("arbitrary"),
    )(z, g_cq, g_ckv, g_qf, g_kf, b_f, dcqn, dckvn, dqf, dkf, dvf, dlogf, dkpe)


def _cumsum_rows(x, *, reverse, name, out_scale=1.0):
    t = x.shape[0]
    nblk = t // LANES

    def body(x_ref, f_ref, ft_ref, carry_ref):
        r = _iota((LANES, LANES), 0)
        c = _iota((LANES, LANES), 1)
        tri = jnp.where((c >= r) if reverse else (c <= r), 1.0, 0.0).astype(F32)
        carry_ref[...] = jnp.zeros_like(carry_ref)

        def step(s, _):
            b = (nblk - 1 - s) if reverse else s
            start = pl.multiple_of(b * LANES, LANES)
            blk = x_ref[pl.ds(start, LANES), :]
            cs = jnp.dot(tri, blk, precision=lax.Precision.HIGHEST, preferred_element_type=F32) + carry_ref[0:1, :]
            scaled = cs if out_scale == 1.0 else cs * out_scale
            f_ref[pl.ds(start, LANES), :] = scaled
            ft_ref[:, pl.ds(start, LANES)] = scaled.T
            carry_ref[0:1, :] = cs[0:1, :] if reverse else cs[LANES - 1:LANES, :]
            return 0

        lax.fori_loop(0, nblk, step, 0)

    return pl.pallas_call(
        body, name=name, grid=(1,),
        in_specs=[pl.BlockSpec((t, LANES), lambda i: (0, 0))],
        out_specs=[pl.BlockSpec((t, LANES), lambda i: (0, 0)), pl.BlockSpec((LANES, t), lambda i: (0, 0))],
        out_shape=[jax.ShapeDtypeStruct((t, LANES), F32), jax.ShapeDtypeStruct((LANES, t), F32)],
        scratch_shapes=[pltpu.VMEM((8, LANES), F32)],
        compiler_params=_params("arbitrary"),
    )(x)


def _rope_partner(x, lane):
    half = MLA_ROPE // 2
    swapped = jnp.where(lane < KPE_LANE + half, pltpu.roll(x, LANES - half, axis=1), pltpu.roll(x, half, axis=1))
    return jnp.where((lane >= KPE_LANE) & (lane < KPE_LANE + MLA_ROPE), swapped, 0.0)


def _rope_tables(t):
    pos = (jnp.arange(t, dtype=jnp.int32) - PAD).astype(F32)
    inv_freq = ROPE_BASE ** (-jnp.arange(0, MLA_ROPE, 2, dtype=F32) / MLA_ROPE)
    ang = pos[:, None] * inv_freq[None, :]
    cos, sin = jnp.cos(ang), jnp.sin(ang)
    ones = jnp.ones((t, KPE_LANE), F32)
    tail = jnp.zeros((t, LANES - KPE_LANE - MLA_ROPE), F32)
    c_tab = jnp.concatenate([ones, cos, cos, tail + 1.0], axis=1)
    s_tab = jnp.concatenate([ones * 0.0, -sin, sin, tail], axis=1)
    return c_tab, s_tab


def _prep_b_fwd(q_raw, kv_raw, z, c_tab, s_tab, g_q, g_k):
    t = q_raw.shape[0]
    bt = min(PREP_TILE, t)
    qw = HEADS * LANES
    vw = HEADS * HEAD_V

    def body(q_ref, kv_ref, zl_ref, c_ref, s_ref, gq_ref, gk_ref, qn_ref, kn_ref, v_ref):
        lane = _iota((1, LANES), 1)
        kpe = jnp.where((lane >= KPE_LANE) & (lane < KPE_LANE + MLA_ROPE), zl_ref[...], 0.0)
        cv, sv = c_ref[...], s_ref[...]
        for h in range(HEADS):
            sl = slice(h * LANES, (h + 1) * LANES)
            for x, g_ref, o_ref in ((q_ref[:, sl], gq_ref, qn_ref), (kv_ref[:, sl] + kpe, gk_ref, kn_ref)):
                r = lax.rsqrt(jnp.sum(x * x, axis=-1, keepdims=True) / MLA_QK + EPS)
                xn = x * r * g_ref[...]
                o_ref[:, sl] = (xn * cv + _rope_partner(xn, lane) * sv).astype(o_ref.dtype)
        v_ref[...] = kv_ref[:, qw:qw + vw].astype(v_ref.dtype)

    def row(w):
        return pl.BlockSpec((bt, w), lambda i: (i, 0))

    vec = pl.BlockSpec((1, LANES), lambda i: (0, 0))
    return pl.pallas_call(
        body, name="attn_prep_b_fwd", grid=(t // bt,),
        in_specs=[row(qw), row(qw + vw), pl.BlockSpec((bt, LANES), lambda i: (i, ZA_LAST // LANES)),
                  row(LANES), row(LANES), vec, vec],
        out_specs=[row(qw), row(qw), row(vw)],
        out_shape=[jax.ShapeDtypeStruct((t, qw), MXU_DTYPE), jax.ShapeDtypeStruct((t, qw), MXU_DTYPE),
                   jax.ShapeDtypeStruct((t, vw), MXU_DTYPE)],
        compiler_params=_params("parallel"),
    )(q_raw, kv_raw, z, c_tab, s_tab, g_q, g_k)


def _prep_b_bwd(q_raw, kv_raw, z, c_tab, s_tab, g_q, g_k, dqn, dkn, dv):
    t = q_raw.shape[0]
    bt = min(PREP_TILE, t)
    qw = HEADS * LANES
    vw = HEADS * HEAD_V

    def body(q_ref, kv_ref, zl_ref, c_ref, s_ref, gq_ref, gk_ref, dqn_ref, dkn_ref, dv_ref,
             dq_ref, dkv_ref, dkpe_ref, dgq_ref, dgk_ref):
        @pl.when(pl.program_id(0) == 0)
        def _():
            dgq_ref[...] = jnp.zeros_like(dgq_ref)
            dgk_ref[...] = jnp.zeros_like(dgk_ref)

        lane = _iota((1, LANES), 1)
        rope_lanes = (lane >= KPE_LANE) & (lane < KPE_LANE + MLA_ROPE)
        kpe = jnp.where(rope_lanes, zl_ref[...], 0.0)
        cv, sv = c_ref[...], s_ref[...]
        dkpe = jnp.zeros((bt, LANES), F32)
        for h in range(HEADS):
            sl = slice(h * LANES, (h + 1) * LANES)
            for is_k, x, g_ref, dout, dg_ref in ((False, q_ref[:, sl], gq_ref, dqn_ref[:, sl], dgq_ref),
                                                  (True, kv_ref[:, sl] + kpe, gk_ref, dkn_ref[:, sl], dgk_ref)):
                r = lax.rsqrt(jnp.sum(x * x, axis=-1, keepdims=True) / MLA_QK + EPS)
                xhat = x * r
                dxn = dout * cv + _rope_partner(dout * sv, lane)
                dg_ref[...] += jnp.sum(dxn * xhat, axis=0, keepdims=True)
                dxhat = dxn * g_ref[...]
                dx = r * (dxhat - xhat * (jnp.sum(dxhat * xhat, axis=-1, keepdims=True) / MLA_QK))
                if is_k:
                    dkv_ref[:, sl] = jnp.where(lane < KPE_LANE, dx, 0.0).astype(dkv_ref.dtype)
                    dkpe = dkpe + jnp.where(rope_lanes, dx, 0.0)
                else:
                    dq_ref[:, sl] = dx.astype(dq_ref.dtype)
        dkv_ref[:, qw:qw + vw] = dv_ref[...].astype(dkv_ref.dtype)
        dkpe_ref[...] = dkpe

    def row(w):
        return pl.BlockSpec((bt, w), lambda i: (i, 0))

    vec = pl.BlockSpec((1, LANES), lambda i: (0, 0))
    return pl.pallas_call(
        body, name="attn_prep_b_bwd", grid=(t // bt,),
        in_specs=[row(qw), row(qw + vw), pl.BlockSpec((bt, LANES), lambda i: (i, ZA_LAST // LANES)),
                  row(LANES), row(LANES), vec, vec, row(qw), row(qw), row(vw)],
        out_specs=[row(qw), row(qw + vw), row(LANES), vec, vec],
        out_shape=[jax.ShapeDtypeStruct((t, qw), MXU_DTYPE), jax.ShapeDtypeStruct((t, qw + vw), MXU_DTYPE),
                   jax.ShapeDtypeStruct((t, LANES), F32), jax.ShapeDtypeStruct((1, LANES), F32),
                   jax.ShapeDtypeStruct((1, LANES), F32)],
        compiler_params=_params("arbitrary"),
    )(q_raw, kv_raw, z, c_tab, s_tab, g_q, g_k, dqn, dkn, dv)


NT_DIMS = (((1,), (1,)), ((), ()))
TN_DIMS = (((0,), (0,)), ((), ()))


def _head_qk(q_ref, k_ref, e, mla, lo):
    if mla:
        return q_ref[:, e * LANES:(e + 1) * LANES], k_ref[:, e * LANES:(e + 1) * LANES]
    q = q_ref[...]
    return jnp.where(lo if e == 0 else jnp.logical_not(lo), q, jnp.zeros_like(q)), k_ref[...]


def _attn_specs(mla, blk, q_map, k_map):
    w = 2 * LANES if mla else LANES
    q_spec = pl.BlockSpec((blk, w), lambda j, a, b: (q_map(a, b), j))
    k_spec = pl.BlockSpec((blk, w), lambda j, a, b: (k_map(a, b), j))
    qv_spec = pl.BlockSpec((blk, LANES), lambda j, a, b: (q_map(a, b), j))
    kv_spec = pl.BlockSpec((blk, LANES), lambda j, a, b: (k_map(a, b), j))
    fq_spec = pl.BlockSpec((blk, LANES), lambda j, a, b: (q_map(a, b), 0))
    fk_spec = pl.BlockSpec((8, blk), lambda j, a, b: (0, k_map(a, b)))
    return q_spec, k_spec, qv_spec, kv_spec, fq_spec, fk_spec


def _flash_fwd(q, k, v, f, f_t, *, mla, scale, name):
    t = q.shape[0]
    blk = min(ATTN_BLOCK, t)
    nb = t // blk
    pairs = HEADS // 2
    q_spec, k_spec, qv_spec, kv_spec, fq_spec, fk_spec = _attn_specs(
        mla, blk, lambda i, kk: i, lambda i, kk: jnp.minimum(kk, i))

    def body(*refs):
        if mla:
            q_ref, k_ref, v_ref, o_ref, lse_ref, m_s, l_s, acc_s = refs
            fq_ref = fk_ref = None
        else:
            q_ref, k_ref, v_ref, fq_ref, fk_ref, o_ref, lse_ref, m_s, l_s, acc_s = refs
        j, i, kk = pl.program_id(0), pl.program_id(1), pl.program_id(2)
        lo = _iota((1, LANES), 1) < HEAD_V

        @pl.when(kk == 0)
        def _():
            m_s[...] = jnp.full_like(m_s, NEG)
            l_s[...] = jnp.zeros_like(l_s)
            acc_s[...] = jnp.zeros_like(acc_s)

        def step(mask):
            vv = v_ref[...]
            for e in range(2):
                s, _, _ = _scores(q_ref, k_ref, fq_ref, fk_ref, e, j, mla, scale, lo, mask)
                m_prev = m_s[e]
                m_new = jnp.maximum(m_prev, jnp.max(s, axis=1, keepdims=True))
                alpha = jnp.exp(m_prev - m_new)
                p = jnp.exp(s - m_new)
                if mask is not None:
                    p = jnp.where(mask, p, 0.0)
                l_s[e] = alpha * l_s[e] + jnp.sum(p, axis=1, keepdims=True)
                acc_s[e] = alpha * acc_s[e] + jnp.dot(p.astype(MXU_DTYPE), vv, preferred_element_type=F32)
                m_s[e] = m_new

        _masked_and_plain(kk <= i, i, kk, blk, step)

        @pl.when(kk == nb - 1)
        def _():
            valid = (i * blk + _iota((blk, 1), 0)) >= PAD
            outs, lses = [], []
            for e in range(2):
                l = l_s[e]
                outs.append(acc_s[e] * jnp.where(l > 0.0, 1.0 / jnp.where(l > 0.0, l, 1.0), 0.0))
                lses.append(m_s[e] + jnp.log(jnp.where(l > 0.0, l, 1.0)))
            o = jnp.where(lo, outs[0], outs[1])
            o_ref[...] = jnp.where(valid, o, 0.0).astype(o_ref.dtype)
            lane = _iota((1, LANES), 1)
            lse_ref[...] = jnp.where(lane == 0, lses[0], jnp.where(lane == 1, lses[1], 0.0))

    in_specs = [q_spec, k_spec, kv_spec] + ([] if mla else [fq_spec, fk_spec])
    args = (q, k, v) + (() if mla else (f, f_t))
    hv = HEADS * HEAD_V
    return pl.pallas_call(
        body, name=name, grid=(pairs, nb, nb),
        in_specs=in_specs, out_specs=[qv_spec, qv_spec],
        out_shape=[jax.ShapeDtypeStruct((t, hv), F32), jax.ShapeDtypeStruct((t, hv), F32)],
        scratch_shapes=[pltpu.VMEM((2, blk, 1), F32), pltpu.VMEM((2, blk, 1), F32), pltpu.VMEM((2, blk, LANES), F32)],
        compiler_params=_params("parallel", "parallel", "arbitrary"),
    )(*args)


def _bwd_tile(q_ref, k_ref, v_ref, o_ref, do_ref, lse_ref, fq_ref, fk_ref, e, pair, mla, scale, lo, mask):
    s, qe, ke = _scores(q_ref, k_ref, fq_ref, fk_ref, e, pair, mla, scale, lo, mask)
    p = jnp.exp(s - lse_ref[:, e:e + 1])
    if mask is not None:
        p = jnp.where(mask, p, 0.0)
    do = do_ref[...]
    doe = jnp.where(lo if e == 0 else jnp.logical_not(lo), do, jnp.zeros_like(do))
    dp = lax.dot_general(doe, v_ref[...], NT_DIMS, preferred_element_type=F32)
    delta = jnp.sum(doe.astype(F32) * o_ref[...].astype(F32), axis=1, keepdims=True)
    return p, p * (dp - delta), qe, ke


def _flash_bwd_dq(q, k, v, o, do, lse, f, f_t, *, mla, scale, col0, name):
    t = q.shape[0]
    blk = min(ATTN_BLOCK, t)
    nb = t // blk
    pairs = HEADS // 2
    w = 2 * LANES if mla else LANES
    q_spec, k_spec, qv_spec, kv_spec, fq_spec, fk_spec = _attn_specs(
        mla, blk, lambda i, kk: i, lambda i, kk: jnp.minimum(kk, i))
    od_spec = pl.BlockSpec((blk, LANES), lambda j, i, kk: (i, col0 + j))

    def body(*refs):
        if mla:
            q_ref, k_ref, v_ref, o_ref, do_ref, lse_ref, dq_ref, dq_s = refs
            fq_ref = fk_ref = rs_ref = rs_s = None
        else:
            q_ref, k_ref, v_ref, o_ref, do_ref, lse_ref, fq_ref, fk_ref, dq_ref, rs_ref, dq_s, rs_s = refs
        j, i, kk = pl.program_id(0), pl.program_id(1), pl.program_id(2)
        lo = _iota((1, LANES), 1) < HEAD_V

        @pl.when(kk == 0)
        def _():
            dq_s[...] = jnp.zeros_like(dq_s)
            if not mla:
                rs_s[...] = jnp.zeros_like(rs_s)

        def step(mask):
            for e in range(2):
                _, ds, _, ke = _bwd_tile(q_ref, k_ref, v_ref, o_ref, do_ref, lse_ref, fq_ref, fk_ref, e, j, mla, scale,
                                         lo, mask)
                dq_s[e] += jnp.dot(ds.astype(MXU_DTYPE), ke, preferred_element_type=F32)
                if not mla:
                    rs_s[e] += jnp.sum(ds, axis=1, keepdims=True)

        _masked_and_plain(kk <= i, i, kk, blk, step)

        @pl.when(kk == nb - 1)
        def _():
            if mla:
                dq_ref[:, 0:LANES] = dq_s[0] * scale
                dq_ref[:, LANES:2 * LANES] = dq_s[1] * scale
            else:
                dq_ref[...] = jnp.where(lo, dq_s[0], dq_s[1]) * scale
                lane = _iota((1, LANES), 1)
                rs_ref[...] = jnp.where(lane == 0, rs_s[0], jnp.where(lane == 1, rs_s[1], 0.0))

    in_specs = [q_spec, k_spec, kv_spec, od_spec, od_spec, qv_spec] + ([] if mla else [fq_spec, fk_spec])
    args = (q, k, v, o, do, lse) + (() if mla else (f, f_t))
    out_specs = [q_spec] + ([] if mla else [qv_spec])
    out_shape = [jax.ShapeDtypeStruct((t, pairs * w), F32)]
    scratch = [pltpu.VMEM((2, blk, LANES), F32)]
    if not mla:
        out_shape.append(jax.ShapeDtypeStruct((t, pairs * LANES), F32))
        scratch.append(pltpu.VMEM((2, blk, 1), F32))
    outs = pl.pallas_call(
        body, name=name, grid=(pairs, nb, nb),
        in_specs=in_specs, out_specs=out_specs, out_shape=out_shape, scratch_shapes=scratch,
        compiler_params=_params("parallel", "parallel", "arbitrary"),
    )(*args)
    return outs[0] if mla else outs


def _flash_bwd_dkv(q, k, v, o, do, lse, f, f_t, *, mla, scale, col0, name):
    t = q.shape[0]
    blk = min(ATTN_BLOCK, t)
    nb = t // blk
    pairs = HEADS // 2
    w = 2 * LANES if mla else LANES
    q_spec, k_spec, qv_spec, kv_spec, fq_spec, fk_spec = _attn_specs(
        mla, blk, lambda a, b: jnp.maximum(a, b), lambda a, b: a)
    od_spec = pl.BlockSpec((blk, LANES), lambda j, a, b: (jnp.maximum(a, b), col0 + j))
    cs_spec = pl.BlockSpec((8, blk), lambda j, a, b: (j, a))

    def body(*refs):
        if mla:
            q_ref, k_ref, v_ref, o_ref, do_ref, lse_ref, dk_ref, dv_ref, dk_s, dv_s = refs
            fq_ref = fk_ref = cs_ref = cs_s = None
        else:
            q_ref, k_ref, v_ref, o_ref, do_ref, lse_ref, fq_ref, fk_ref, dk_ref, dv_ref, cs_ref, dk_s, dv_s, cs_s = refs
        j, kb, qb = pl.program_id(0), pl.program_id(1), pl.program_id(2)
        lo = _iota((1, LANES), 1) < HEAD_V

        @pl.when(qb == 0)
        def _():
            dk_s[...] = jnp.zeros_like(dk_s)
            dv_s[...] = jnp.zeros_like(dv_s)
            if not mla:
                cs_s[...] = jnp.zeros_like(cs_s)

        def step(mask):
            do = do_ref[...]
            for e in range(2):
                p, ds, _, _ = _bwd_tile(q_ref, k_ref, v_ref, o_ref, do_ref, lse_ref, fq_ref, fk_ref, e, j, mla, scale,
                                        lo, mask)
                dv_s[e] += lax.dot_general(p.astype(MXU_DTYPE), do, TN_DIMS, preferred_element_type=F32)
                q_src = q_ref[:, e * LANES:(e + 1) * LANES] if mla else q_ref[...]
                dk_s[e] += lax.dot_general(ds.astype(MXU_DTYPE), q_src, TN_DIMS, preferred_element_type=F32)
                if not mla:
                    cs_s[e] += jnp.sum(ds, axis=0, keepdims=True)

        _masked_and_plain(qb >= kb, qb, kb, blk, step)

        @pl.when(qb == nb - 1)
        def _():
            dv_ref[...] = jnp.where(lo, dv_s[0], dv_s[1])
            if mla:
                dk_ref[:, 0:LANES] = dk_s[0] * scale
                dk_ref[:, LANES:2 * LANES] = dk_s[1] * scale
            else:
                dk_ref[...] = jnp.where(lo, dk_s[0], dk_s[1]) * scale
                sub = _iota((8, 1), 0)
                cs_ref[...] = jnp.where(sub == 0, cs_s[0], jnp.where(sub == 1, cs_s[1], 0.0))

    in_specs = [q_spec, k_spec, kv_spec, od_spec, od_spec, qv_spec] + ([] if mla else [fq_spec, fk_spec])
    args = (q, k, v, o, do, lse) + (() if mla else (f, f_t))
    out_specs = [k_spec, kv_spec] + ([] if mla else [cs_spec])
    out_shape = [jax.ShapeDtypeStruct((t, pairs * w), F32), jax.ShapeDtypeStruct((t, HEADS * HEAD_V), F32)]
    scratch = [pltpu.VMEM((2, blk, LANES), F32), pltpu.VMEM((2, blk, LANES), F32)]
    if not mla:
        out_shape.append(jax.ShapeDtypeStruct((pairs * 8, t), F32))
        scratch.append(pltpu.VMEM((2, 1, blk), F32))
    return pl.pallas_call(
        body, name=name, grid=(pairs, nb, nb),
        in_specs=in_specs, out_specs=out_specs, out_shape=out_shape, scratch_shapes=scratch,
        compiler_params=_params("parallel", "parallel", "arbitrary"),
    )(*args)


ATTN_CHUNK = 640
LOG2E = 1.4426950408889634
LN2 = 0.6931471805599453


def _for_chunks(n, ch, body):
    for c in range(n):
        body(c * ch)


def _select_lane(x, idx):
    return jnp.sum(jnp.where(_iota(x.shape, 1) == idx, x, 0.0), axis=1, keepdims=True)


def _select_row(x, idx):
    return jnp.sum(jnp.where(_iota(x.shape, 0) == idx, x, 0.0), axis=0, keepdims=True)


def _flash_fwd_chunked(q, k, v, f, f_t, *, mla, name):
    t = q.shape[0]
    blk = min(ATTN_BLOCK, t)
    nb = t // blk
    pairs = HEADS // 2
    ch = min(ATTN_CHUNK, blk)
    assert blk % ch == 0 and blk >= PAD
    w = 2 * LANES if mla else LANES
    q_of, k_of = _causal_pairs(nb, by_key=False)
    q_spec = pl.BlockSpec((blk, w), lambda j, s, qt, kt: (qt[s], j))
    k_spec = pl.BlockSpec((blk, w), lambda j, s, qt, kt: (kt[s], j))
    kv_spec = pl.BlockSpec((blk, LANES), lambda j, s, qt, kt: (kt[s], j))
    qv_spec = pl.BlockSpec((blk, LANES), lambda j, s, qt, kt: (qt[s], j))
    fq_spec = pl.BlockSpec((blk, LANES), lambda j, s, qt, kt: (qt[s], 0))
    fk_spec = pl.BlockSpec((8, blk), lambda j, s, qt, kt: (0, kt[s]))
    lse_spec = pl.BlockSpec((8, blk), lambda j, s, qt, kt: (j, qt[s]))

    def body(qt_ref, kt_ref, *refs):
        if mla:
            q_ref, k_ref, v_ref, o_ref, lse_ref, m_s, l_s, a_s, acc_s, s_s, p_s = refs
            fq_ref = fk_ref = None
        else:
            q_ref, k_ref, v_ref, fq_ref, fk_ref, o_ref, lse_ref, m_s, l_s, a_s, acc_s, s_s, p_s = refs
        j, step_id = pl.program_id(0), pl.program_id(1)
        i, kk = qt_ref[step_id], kt_ref[step_id]
        lo = _iota((1, LANES), 1) < HEAD_V

        @pl.when(kk == 0)
        def _():
            m_s[...] = jnp.full_like(m_s, NEG)
            l_s[...] = jnp.zeros_like(l_s)
            acc_s[...] = jnp.zeros_like(acc_s)

        def step(masked):
            vv = v_ref[...]
            for e in range(2):
                qe, ke = _head_qk(q_ref, k_ref, e, mla, lo)
                s_s[...] = lax.dot_general(qe, ke, NT_DIMS, preferred_element_type=F32)
                fkr = None if mla else _select_row(fk_ref[...], 2 * j + e)

                def chunk(r0, e=e, fkr=fkr):
                    rows = pl.ds(r0, ch)
                    s = s_s[rows, :]
                    if not mla:
                        s = s + _select_lane(fq_ref[rows, :], 2 * j + e) - fkr
                    if masked:
                        rpos = i * blk + r0 + _iota((ch, blk), 0)
                        cpos = kk * blk + _iota((ch, blk), 1)
                        mask = (cpos <= rpos) & (cpos >= PAD)
                        s = jnp.where(mask, s, NEG)
                    m_prev = m_s[e, rows, :]
                    m_new = jnp.maximum(m_prev, jnp.max(s, axis=1, keepdims=True))
                    alpha = jnp.exp2(m_prev - m_new)
                    p = jnp.exp2(s - m_new)
                    if masked:
                        p = jnp.where(mask, p, 0.0)
                    l_s[e, rows, :] = alpha * l_s[e, rows, :] + jnp.sum(p, axis=1, keepdims=True)
                    m_s[e, rows, :] = m_new
                    a_s[rows, :] = alpha
                    p_s[rows, :] = p.astype(p_s.dtype)

                _for_chunks(blk // ch, ch, chunk)
                acc_s[e] = a_s[...] * acc_s[e] + jnp.dot(p_s[...], vv, preferred_element_type=F32)

        needs_mask = (kk == i) | (kk == 0)

        @pl.when(needs_mask)
        def _():
            step(True)

        @pl.when(jnp.logical_not(needs_mask))
        def _():
            step(False)

        @pl.when(kk == i)
        def _():
            valid = (i * blk + _iota((blk, 1), 0)) >= PAD
            outs, lses = [], []
            for e in range(2):
                l = l_s[e]
                outs.append(acc_s[e] * jnp.where(l > 0.0, 1.0 / jnp.where(l > 0.0, l, 1.0), 0.0))
                lses.append(m_s[e] + jnp.log(jnp.where(l > 0.0, l, 1.0)) * LOG2E)
            o = jnp.where(lo, outs[0], outs[1])
            o_ref[...] = jnp.where(valid, o, 0.0).astype(o_ref.dtype)
            lane = _iota((1, LANES), 1)
            lse_cols = jnp.where(lane == 0, lses[0], jnp.where(lane == 1, lses[1], 0.0))
            lse_ref[...] = lse_cols.T[0:8, :]

    in_specs = [q_spec, k_spec, kv_spec] + ([] if mla else [fq_spec, fk_spec])
    args = (q, k, v) + (() if mla else (f, f_t))
    hv = HEADS * HEAD_V
    grid_spec = pltpu.PrefetchScalarGridSpec(
        num_scalar_prefetch=2, grid=(pairs, len(q_of)), in_specs=in_specs, out_specs=[qv_spec, lse_spec],
        scratch_shapes=[pltpu.VMEM((2, blk, 1), F32), pltpu.VMEM((2, blk, 1), F32), pltpu.VMEM((blk, 1), F32),
                        pltpu.VMEM((2, blk, LANES), F32), pltpu.VMEM((blk, blk), F32), pltpu.VMEM((blk, blk), MXU_DTYPE)])
    return pl.pallas_call(
        body, name=name, grid_spec=grid_spec,
        out_shape=[jax.ShapeDtypeStruct((t, hv), F32), jax.ShapeDtypeStruct((pairs * 8, t), F32)],
        compiler_params=_params("parallel", "arbitrary"),
    )(jnp.asarray(q_of), jnp.asarray(k_of), *args)


def _causal_pairs(nb, *, by_key):
    if by_key:
        pairs = [(qb, kb) for kb in range(nb) for qb in range(kb, nb)]
    else:
        pairs = [(qb, kb) for qb in range(nb) for kb in range(qb + 1)]
    return (np.asarray([p[0] for p in pairs], np.int32), np.asarray([p[1] for p in pairs], np.int32))


def _delta_rows(do, o):
    t, width = o.shape
    bt = min(ATTN_BLOCK, t)
    n_heads = width // HEAD_V

    def body(do_ref, o_ref, d_ref):
        prod = do_ref[...].astype(F32) * o_ref[...]
        col = _iota((width, LANES), 0)
        first = _iota((width, LANES), 1) * HEAD_V
        sel = jnp.where((col >= first) & (col < first + HEAD_V), 1.0, 0.0).astype(F32)
        per_head = jnp.dot(prod, sel, precision=lax.Precision.HIGHEST, preferred_element_type=F32)
        d_ref[...] = per_head.T[0:n_heads, :]

    return pl.pallas_call(
        body, name="attn_delta", grid=(t // bt,),
        in_specs=[pl.BlockSpec((bt, width), lambda i: (i, 0)), pl.BlockSpec((bt, width), lambda i: (i, 0))],
        out_specs=pl.BlockSpec((n_heads, bt), lambda i: (0, i)),
        out_shape=jax.ShapeDtypeStruct((n_heads, t), F32),
        compiler_params=_params("parallel"),
    )(do, o)


def _flash_bwd_fused(q, k, v, do, lse_t, delta_t, f, f_t, *, mla, col0, name):
    t = q.shape[0]
    blk = min(ATTN_BLOCK, t)
    nb = t // blk
    pairs = HEADS // 2
    ch = min(ATTN_CHUNK, blk)
    assert blk % ch == 0
    w = 2 * LANES if mla else LANES
    last = nb - 1
    assert blk >= PAD
    q_of, k_of = _causal_pairs(nb, by_key=True)
    q_spec = pl.BlockSpec((blk, w), lambda j, s, qt, kt: (qt[s], j))
    k_spec = pl.BlockSpec((blk, w), lambda j, s, qt, kt: (kt[s], j))
    v_spec = pl.BlockSpec((blk, LANES), lambda j, s, qt, kt: (kt[s], j))
    do_spec = pl.BlockSpec((blk, LANES), lambda j, s, qt, kt: (qt[s], col0 + j))
    lse_spec = pl.BlockSpec((8, blk), lambda j, s, qt, kt: (j, qt[s]))
    delta_spec = pl.BlockSpec((8, blk), lambda j, s, qt, kt: (col0 // (HEADS // 2), qt[s]))
    fq_spec = pl.BlockSpec((8, blk), lambda j, s, qt, kt: (0, qt[s]))
    fk_spec = pl.BlockSpec((blk, LANES), lambda j, s, qt, kt: (kt[s], 0))
    dq_spec = pl.BlockSpec((blk, w), lambda j, s, qt, kt: (kt[s], j))
    rs_spec = pl.BlockSpec((8, blk), lambda j, s, qt, kt: (j, kt[s]))
    cs_spec = pl.BlockSpec((blk, LANES), lambda j, s, qt, kt: (kt[s], j))

    def body(qt_ref, kt_ref, *refs):
        if mla:
            (q_ref, k_ref, v_ref, do_ref, lse_ref, delta_ref, dq_ref, dk_ref, dv_ref,
             dq_s, dk_s, dv_s, st_s, dpt_s, pt_s, dst_s) = refs
            fq_ref = fk_ref = rs_ref = cs_ref = rs_s = cs_s = None
        else:
            (q_ref, k_ref, v_ref, do_ref, lse_ref, delta_ref, fq_ref, fk_ref, dq_ref, dk_ref, dv_ref, rs_ref, cs_ref,
             dq_s, dk_s, dv_s, st_s, dpt_s, pt_s, dst_s, rs_s, cs_s) = refs
        j, step_id = pl.program_id(0), pl.program_id(1)
        qb, kb = qt_ref[step_id], kt_ref[step_id]
        lo = _iota((1, LANES), 1) < HEAD_V

        @pl.when(step_id == 0)
        def _():
            dq_s[...] = jnp.zeros_like(dq_s)
            if not mla:
                rs_s[...] = jnp.zeros_like(rs_s)

        @pl.when(qb == kb)
        def _():
            dk_s[...] = jnp.zeros_like(dk_s)
            dv_s[...] = jnp.zeros_like(dv_s)
            if not mla:
                cs_s[...] = jnp.zeros_like(cs_s)

        def step(masked):
            do = do_ref[...]
            vv = v_ref[...]
            for e in range(2):
                half = lo if e == 0 else jnp.logical_not(lo)
                qe, ke = _head_qk(q_ref, k_ref, e, mla, lo)
                doe = jnp.where(half, do, jnp.zeros_like(do))
                st_s[...] = lax.dot_general(ke, qe, NT_DIMS, preferred_element_type=F32)
                dpt_s[...] = lax.dot_general(vv, doe, NT_DIMS, preferred_element_type=F32)
                head = 2 * j + e
                lse_row = _select_row(lse_ref[...], e)
                delta_row = _select_row(delta_ref[...], head)
                fq_row = None if mla else _select_row(fq_ref[...], head)

                def chunk(r0, e=e, lse_row=lse_row, delta_row=delta_row, fq_row=fq_row, head=head):
                    rows = pl.ds(r0, ch)
                    s = st_s[rows, :]
                    if not mla:
                        s = s + fq_row - _select_lane(fk_ref[rows, :], head)
                    p = jnp.exp2(s - lse_row)
                    if masked:
                        kpos = kb * blk + r0 + _iota((ch, blk), 0)
                        qpos = qb * blk + _iota((ch, blk), 1)
                        p = jnp.where((kpos <= qpos) & (kpos >= PAD), p, 0.0)
                    ds = p * (dpt_s[rows, :] - delta_row)
                    pt_s[rows, :] = p.astype(pt_s.dtype)
                    dst_s[rows, :] = ds.astype(dst_s.dtype)
                    if not mla:
                        cs_s[e, rows, :] += jnp.sum(ds, axis=1, keepdims=True)
                        rs_s[qb, e] += jnp.sum(ds, axis=0, keepdims=True)

                _for_chunks(blk // ch, ch, chunk)
                dv_s[e] += jnp.dot(pt_s[...], do, preferred_element_type=F32)
                q_src = qe if mla else q_ref[...]
                dk_s[e] += jnp.dot(dst_s[...], q_src, preferred_element_type=F32)
                dq_s[qb, e] += lax.dot_general(dst_s[...], ke, TN_DIMS, preferred_element_type=F32)

        needs_mask = (qb == kb) | (kb == 0)

        @pl.when(needs_mask)
        def _():
            step(True)

        @pl.when(jnp.logical_not(needs_mask))
        def _():
            step(False)

        @pl.when(qb == last)
        def _():
            dv_ref[...] = jnp.where(lo, dv_s[0], dv_s[1])
            if mla:
                dk_ref[:, 0:LANES] = dk_s[0] * LN2
                dk_ref[:, LANES:2 * LANES] = dk_s[1] * LN2
            else:
                dk_ref[...] = jnp.where(lo, dk_s[0], dk_s[1]) * LN2
                lane = _iota((1, LANES), 1)
                cs_ref[...] = jnp.where(lane == 0, cs_s[0], jnp.where(lane == 1, cs_s[1], 0.0))

        @pl.when(qb == kb)
        def _():
            if mla:
                dq_ref[:, 0:LANES] = dq_s[qb, 0] * LN2
                dq_ref[:, LANES:2 * LANES] = dq_s[qb, 1] * LN2
            else:
                dq_ref[...] = jnp.where(lo, dq_s[qb, 0], dq_s[qb, 1]) * LN2
                sub = _iota((8, 1), 0)
                rs_ref[...] = jnp.where(sub == 0, rs_s[qb, 0], jnp.where(sub == 1, rs_s[qb, 1], 0.0))

    in_specs = [q_spec, k_spec, v_spec, do_spec, lse_spec, delta_spec] + ([] if mla else [fq_spec, fk_spec])
    args = (q, k, v, do, lse_t, delta_t) + (() if mla else (f_t, f))
    hv = HEADS * HEAD_V
    out_specs = [dq_spec, k_spec, v_spec]
    out_shape = [jax.ShapeDtypeStruct((t, pairs * w), F32), jax.ShapeDtypeStruct((t, pairs * w), F32),
                 jax.ShapeDtypeStruct((t, hv), F32)]
    scratch = [pltpu.VMEM((nb, 2, blk, LANES), F32), pltpu.VMEM((2, blk, LANES), F32), pltpu.VMEM((2, blk, LANES), F32),
               pltpu.VMEM((blk, blk), F32), pltpu.VMEM((blk, blk), F32), pltpu.VMEM((blk, blk), MXU_DTYPE),
               pltpu.VMEM((blk, blk), MXU_DTYPE)]
    if not mla:
        out_specs += [rs_spec, cs_spec]
        out_shape += [jax.ShapeDtypeStruct((pairs * 8, t), F32), jax.ShapeDtypeStruct((t, hv), F32)]
        scratch += [pltpu.VMEM((nb, 2, 1, blk), F32), pltpu.VMEM((2, blk, 1), F32)]
    grid_spec = pltpu.PrefetchScalarGridSpec(
        num_scalar_prefetch=2, grid=(pairs, len(q_of)), in_specs=in_specs, out_specs=out_specs, scratch_shapes=scratch)
    return pl.pallas_call(
        body, name=name, grid_spec=grid_spec, out_shape=out_shape,
        compiler_params=_params("parallel", "arbitrary"),
    )(jnp.asarray(q_of), jnp.asarray(k_of), *args)


def _shift_down(x, halo, n):
    rows = x.shape[0]
    r = _iota((rows, 1), 0)
    out = pltpu.roll(x, n, axis=0)
    for s in range(n):
        out = jnp.where(r == s, halo[8 - n + s:8 - n + s + 1, :], out)
    return out


def _shift_up(x, halo, n):
    rows = x.shape[0]
    r = _iota((rows, 1), 0)
    out = pltpu.roll(x, rows - n, axis=0)
    for s in range(n):
        out = jnp.where(r == rows - n + s, halo[s:s + 1, :], out)
    return out


def _conv_specs(bt, nblk):
    d = D_MODEL
    per8 = bt // 8
    z_spec = pl.BlockSpec((bt, 3 * d), lambda i: (i, 0))
    prev_spec = pl.BlockSpec((8, 3 * d), lambda i: (jnp.maximum(i * per8 - 1, 0), 0))
    next_z = pl.BlockSpec((8, 3 * d), lambda i: (jnp.minimum((i + 1) * per8, nblk * per8 - 1), 0))
    next_d = pl.BlockSpec((8, d), lambda i: (jnp.minimum((i + 1) * per8, nblk * per8 - 1), 0))
    w_spec = pl.BlockSpec((8, d), lambda i: (0, 0))
    row_spec = pl.BlockSpec((bt, d), lambda i: (i, 0))
    return z_spec, prev_spec, next_z, next_d, w_spec, row_spec


def _conv_taps(z_ref, prev_ref, i):
    d = D_MODEL
    g = z_ref[:, d:2 * d] * z_ref[:, 2 * d:3 * d]
    gh = jnp.where(i > 0, prev_ref[:, d:2 * d] * prev_ref[:, 2 * d:3 * d], 0.0)
    return g, _shift_down(g, gh, 1), _shift_down(g, gh, 2)


def _conv_fwd(z, conv_w8):
    t = z.shape[0]
    bt = min(PREP_TILE, t)
    nblk = t // bt
    d = D_MODEL
    z_spec, prev_spec, _, _, w_spec, row_spec = _conv_specs(bt, nblk)

    def body(z_ref, prev_ref, w_ref, v_ref):
        g, g1, g2 = _conv_taps(z_ref, prev_ref, pl.program_id(0))
        y = w_ref[0:1, :] * g2 + w_ref[1:2, :] * g1 + w_ref[2:3, :] * g
        v_ref[...] = (z_ref[:, 0:d] * y).astype(v_ref.dtype)

    return pl.pallas_call(
        body, name="conv_fwd", grid=(nblk,),
        in_specs=[z_spec, prev_spec, w_spec], out_specs=row_spec,
        out_shape=jax.ShapeDtypeStruct((t, d), MXU_DTYPE),
        compiler_params=_params("parallel"),
    )(z, z, conv_w8)


def _conv_bwd(z, conv_w8, dv):
    t = z.shape[0]
    bt = min(PREP_TILE, t)
    nblk = t // bt
    d = D_MODEL
    z_spec, prev_spec, next_z, next_d, w_spec, row_spec = _conv_specs(bt, nblk)

    def body(z_ref, prev_ref, nz_ref, dv_ref, ndv_ref, w_ref, dz_ref, dw_ref):
        i = pl.program_id(0)

        @pl.when(i == 0)
        def _():
            dw_ref[...] = jnp.zeros_like(dw_ref)

        g, g1, g2 = _conv_taps(z_ref, prev_ref, i)
        w0, w1, w2 = w_ref[0:1, :], w_ref[1:2, :], w_ref[2:3, :]
        y = w0 * g2 + w1 * g1 + w2 * g
        dvv = dv_ref[...].astype(F32)
        gate_b = z_ref[:, 0:d]
        dy = dvv * gate_b
        dyn = jnp.where(i < nblk - 1, ndv_ref[...].astype(F32) * nz_ref[:, 0:d], 0.0)
        dg = w2 * dy + w1 * _shift_up(dy, dyn, 1) + w0 * _shift_up(dy, dyn, 2)
        dz_ref[:, 0:d] = (dvv * y).astype(dz_ref.dtype)
        dz_ref[:, d:2 * d] = (dg * z_ref[:, 2 * d:3 * d]).astype(dz_ref.dtype)
        dz_ref[:, 2 * d:3 * d] = (dg * z_ref[:, d:2 * d]).astype(dz_ref.dtype)
        sub = _iota((8, 1), 0)
        s0 = jnp.sum(dy * g2, axis=0, keepdims=True)
        s1 = jnp.sum(dy * g1, axis=0, keepdims=True)
        s2 = jnp.sum(dy * g, axis=0, keepdims=True)
        dw_ref[...] += jnp.where(sub == 0, s0, jnp.where(sub == 1, s1, jnp.where(sub == 2, s2, 0.0)))

    return pl.pallas_call(
        body, name="conv_bwd", grid=(nblk,),
        in_specs=[z_spec, prev_spec, next_z, row_spec, next_d, w_spec], out_specs=[z_spec, w_spec],
        out_shape=[jax.ShapeDtypeStruct((t, 3 * d), MXU_DTYPE), jax.ShapeDtypeStruct((8, d), F32)],
        compiler_params=_params("arbitrary"),
    )(z, z, z, dv, dv, conv_w8)


def _loss_head(h, target):
    t, d = h.shape
    bt = LOSS_TILE
    assert LANES % bt == 0 or bt == LANES
    off = LANES // bt

    def body(h_ref, y_ref, dh_ref, acc_ref):
        i = pl.program_id(0)

        @pl.when(i == 0)
        def _():
            acc_ref[...] = jnp.zeros_like(acc_ref)

        @pl.when(i < off)
        def _():
            dh_ref[...] = jnp.zeros_like(dh_ref)

        @pl.when(i >= off)
        def _():
            err = h_ref[...] - y_ref[...]
            dh_ref[...] = err / d
            acc_ref[...] += jnp.sum(err * err)

    dh, acc = pl.pallas_call(
        body, name="loss_head", grid=(t // bt,),
        in_specs=[pl.BlockSpec((bt, d), lambda i: (i, 0)), pl.BlockSpec((bt, d), lambda i: (jnp.maximum(i - off, 0), 0))],
        out_specs=[pl.BlockSpec((bt, d), lambda i: (i, 0)), pl.BlockSpec((8, LANES), lambda i: (0, 0))],
        out_shape=[jax.ShapeDtypeStruct((t, d), F32), jax.ShapeDtypeStruct((8, LANES), F32)],
        compiler_params=_params("arbitrary"),
    )(h, target)
    return dh, acc[0, 0] * (0.5 / d)


def _common_tile(rows, row_off, cap=512, align=8):
    for b in range(min(cap, rows) // align * align, 0, -align):
        if rows % b == 0 and row_off % b == 0:
            return b
    raise ValueError((rows, row_off))


def _round_up(n, m):
    return -(-n // m) * m


def _adamw(w, m, v, g_buf, row_off, col_off):
    rows, width = w.shape
    wpad = _round_up(width, LANES)
    assert col_off % wpad == 0
    bt = _common_tile(rows, row_off)

    def body(w_ref, m_ref, v_ref, g_ref, go_ref, d_ref, nm_ref, nv_ref):
        gv = g_ref[...]
        if wpad != width:
            gv = gv[:, :width]
        m_new = ADAM_B1 * m_ref[...] + (1.0 - ADAM_B1) * gv
        v_new = ADAM_B2 * v_ref[...] + (1.0 - ADAM_B2) * jnp.square(gv)
        m_hat = m_new / (1.0 - ADAM_B1 ** ADAM_STEP)
        v_hat = v_new / (1.0 - ADAM_B2 ** ADAM_STEP)
        go_ref[...] = gv
        d_ref[...] = -ADAM_LR * (m_hat / (jnp.sqrt(v_hat) + ADAM_EPS) + ADAM_WD * w_ref[...])
        nm_ref[...] = m_new
        nv_ref[...] = v_new

    spec = pl.BlockSpec((bt, width), lambda i: (i, 0))
    g_spec = pl.BlockSpec((bt, wpad), lambda i: (row_off // bt + i, col_off // wpad))
    return pl.pallas_call(
        body, name="adamw", grid=(rows // bt,),
        in_specs=[spec] * 3 + [g_spec], out_specs=[spec] * 4,
        out_shape=[jax.ShapeDtypeStruct((rows, width), F32)] * 4,
        compiler_params=_params("parallel"),
    )(w, m, v, g_buf)


def _add_half(g, got, c, *, out_dtype, name):
    slabs, half, width = got.shape
    bt = next(x for x in range(min(half, 640), 0, -16) if half % x == 0)
    per_half = half // bt

    def body(c_ref, a_ref, b_ref, o_ref):
        o_ref[...] = (a_ref[...] + b_ref[...]).astype(o_ref.dtype)

    grid_spec = pltpu.PrefetchScalarGridSpec(
        num_scalar_prefetch=1, grid=(slabs, per_half),
        in_specs=[pl.BlockSpec((None, bt, width), lambda s, i, cc: (s, cc[0] * per_half + i, 0)),
                  pl.BlockSpec((None, bt, width), lambda s, i, cc: (s, i, 0))],
        out_specs=pl.BlockSpec((None, bt, width), lambda s, i, cc: (s, i, 0)))
    return pl.pallas_call(
        body, name=name, grid_spec=grid_spec,
        out_shape=jax.ShapeDtypeStruct((slabs, half, width), out_dtype), compiler_params=_params("parallel", "parallel"),
    )(jnp.reshape(c, (1,)).astype(jnp.int32), g, got)


def _sum4(parts, slot, *, name):
    _, rows, width = parts.shape
    bt = next(x for x in range(min(rows, 640), 0, -16) if rows % x == 0)

    def body(slot_ref, p_ref, o_ref):
        p = [p_ref[n].astype(F32) for n in range(4)]
        o_ref[...] = ((p[0] + p[1]) + p[2]) + p[3]

    grid_spec = pltpu.PrefetchScalarGridSpec(
        num_scalar_prefetch=1, grid=(rows // bt,),
        in_specs=[pl.BlockSpec((4, bt, width), lambda i, s: (0, i, 0))],
        out_specs=pl.BlockSpec((None, bt, width), lambda i, s: (s[0], i, 0)))
    return pl.pallas_call(
        body, name=name, grid_spec=grid_spec,
        out_shape=jax.ShapeDtypeStruct((2, rows, width), F32), compiler_params=_params("parallel"),
    )(jnp.reshape(slot, (1,)).astype(jnp.int32), parts)


ANY = pl.BlockSpec(memory_space=pl.ANY)
CHIP_FLIPS = ((1, 0), (0, 1), (1, 1))


def _place():
    return lax.axis_index("x"), lax.axis_index("y"), lax.axis_index("c")


def _flip(v, f):
    return 1 - v if f else v


def _allgather_chips(slabs):
    _, rows, width = slabs.shape
    half = rows // 2

    def body(_, out_ref, send_sems, recv_sems):
        x, y, c = _place()
        me = 2 * x + y
        sibling = (x, y, 1 - c)
        my_rows = pl.ds(pl.multiple_of(c * half, 8), half)
        sib_rows = pl.ds(pl.multiple_of((1 - c) * half, 8), half)
        first, passed = [], []
        for n, (fx, fy) in enumerate(CHIP_FLIPS):
            px, py = _flip(x, fx), _flip(y, fy)
            peer = 2 * px + py
            first.append(pltpu.make_async_remote_copy(
                src_ref=out_ref.at[me, my_rows], dst_ref=out_ref.at[me, my_rows],
                send_sem=send_sems.at[n], recv_sem=recv_sems.at[n], device_id=(px, py, c), device_id_type=MESH))
            passed.append(pltpu.make_async_remote_copy(
                src_ref=out_ref.at[peer, my_rows], dst_ref=out_ref.at[peer, my_rows],
                send_sem=send_sems.at[3 + n], recv_sem=recv_sems.at[3 + n], device_id=sibling, device_id_type=MESH))
        for cp in first:
            cp.start()
        for n, (fx, fy) in enumerate(CHIP_FLIPS):
            peer = 2 * _flip(x, fx) + _flip(y, fy)
            pltpu.make_async_remote_copy(
                src_ref=out_ref.at[me, my_rows], dst_ref=out_ref.at[peer, my_rows],
                send_sem=send_sems.at[n], recv_sem=recv_sems.at[n], device_id=sibling, device_id_type=MESH).wait_recv()
            passed[n].start()
        for n, (fx, fy) in enumerate(CHIP_FLIPS):
            peer = 2 * _flip(x, fx) + _flip(y, fy)
            pltpu.make_async_remote_copy(
                src_ref=out_ref.at[me, sib_rows], dst_ref=out_ref.at[peer, sib_rows],
                send_sem=send_sems.at[3 + n], recv_sem=recv_sems.at[3 + n], device_id=sibling,
                device_id_type=MESH).wait_recv()
        for cp in first + passed:
            cp.wait_send()

    return pl.pallas_call(
        body, name="allgather_weights",
        in_specs=[ANY], out_specs=ANY,
        out_shape=jax.ShapeDtypeStruct(slabs.shape, slabs.dtype), input_output_aliases={0: 0},
        scratch_shapes=[pltpu.SemaphoreType.DMA((6,)), pltpu.SemaphoreType.DMA((6,))],
    )(slabs)


def _swap_halves(g):
    _, rows, width = g.shape
    half = rows // 2

    def body(g_ref, got_ref, send_sem, recv_sem):
        x, y, c = _place()
        away = pl.ds(pl.multiple_of((1 - c) * half, 8), half)
        cp = pltpu.make_async_remote_copy(
            src_ref=g_ref.at[:, away], dst_ref=got_ref, send_sem=send_sem, recv_sem=recv_sem,
            device_id=(x, y, 1 - c), device_id_type=MESH)
        cp.start()
        cp.wait()

    return pl.pallas_call(
        body, name="grad_swap_halves",
        in_specs=[ANY], out_specs=ANY,
        out_shape=jax.ShapeDtypeStruct((4, half, width), g.dtype),
        scratch_shapes=[pltpu.SemaphoreType.DMA, pltpu.SemaphoreType.DMA],
    )(g)


def _scatter_chips(s):
    _, rows, width = s.shape

    def body(s_ref, out_ref, send_sems, recv_sems, local_sem):
        x, y, c = _place()
        me = 2 * x + y
        mine = pltpu.make_async_copy(s_ref.at[me], out_ref.at[me], local_sem)
        mine.start()
        copies = []
        for n, (fx, fy) in enumerate(CHIP_FLIPS):
            px, py = _flip(x, fx), _flip(y, fy)
            copies.append(pltpu.make_async_remote_copy(
                src_ref=s_ref.at[2 * px + py], dst_ref=out_ref.at[me],
                send_sem=send_sems.at[n], recv_sem=recv_sems.at[n], device_id=(px, py, c), device_id_type=MESH))
        for cp in copies:
            cp.start()
        for n, (fx, fy) in enumerate(CHIP_FLIPS):
            peer = 2 * _flip(x, fx) + _flip(y, fy)
            pltpu.make_async_remote_copy(
                src_ref=s_ref.at[me], dst_ref=out_ref.at[peer],
                send_sem=send_sems.at[n], recv_sem=recv_sems.at[n], device_id=(x, y, c), device_id_type=MESH).wait_recv()
        for cp in copies:
            cp.wait_send()
        mine.wait()

    return pl.pallas_call(
        body, name="grad_scatter_chips",
        in_specs=[ANY], out_specs=ANY,
        out_shape=jax.ShapeDtypeStruct((4, rows, width), s.dtype),
        scratch_shapes=[pltpu.SemaphoreType.DMA((3,)), pltpu.SemaphoreType.DMA((3,)), pltpu.SemaphoreType.DMA],
    )(s)


def _join_halves(halves):
    def body(_, out_ref, send_sem, recv_sem):
        x, y, c = _place()
        cp = pltpu.make_async_remote_copy(
            src_ref=out_ref.at[c], dst_ref=out_ref.at[c], send_sem=send_sem, recv_sem=recv_sem,
            device_id=(x, y, 1 - c), device_id_type=MESH)
        cp.start()
        pltpu.make_async_remote_copy(
            src_ref=out_ref.at[c], dst_ref=out_ref.at[1 - c], send_sem=send_sem, recv_sem=recv_sem,
            device_id=(x, y, 1 - c), device_id_type=MESH).wait_recv()
        cp.wait_send()

    return pl.pallas_call(
        body, name="grad_join_halves",
        in_specs=[ANY], out_specs=ANY,
        out_shape=jax.ShapeDtypeStruct(halves.shape, halves.dtype), input_output_aliases={0: 0},
        scratch_shapes=[pltpu.SemaphoreType.DMA, pltpu.SemaphoreType.DMA],
    )(halves)


PACK_W = 1024
REPLICATED = ("g_mix", "g_mlp", "g_cq", "g_ckv", "g_q_mla", "g_k_mla", "g_q_fox", "g_k_fox", "b_forget")
WEIGHT_ORDER = ("meta_tokens", "g_mix", "g_mlp", "w_in_attn", "g_cq", "w_uq", "g_ckv", "w_ukv", "g_q_mla", "g_k_mla",
                "g_q_fox", "g_k_fox", "b_forget", "w_out_attn", "w_in_conv", "conv_w", "w_out_conv", "w_mlp_up",
                "w_mlp_down")
N_EVEN = 2
N_ODD = 2
SHARD_IN = ATTN_IN // 4
SHARD_MIX = D_MODEL // 4
SHARD_UQ = HEADS * MLA_QK // 4
SHARD_UKV = HEADS * (MLA_NOPE + HEAD_V) // 4
SHARD_CONV = 3 * D_MODEL // 4
SIDE_W = 256
PK_UP = (0, 0)
PK_DOWN = (4096, 0)
PK_CONV_IN = (8192, 0)
PK_ATTN_IN = (10240, 0)
PK_OUT_ATTN = (12288, 0)
PK_OUT_CONV = (12800, 0)
PK_SMALL = (8192, 768)
PK_UQ = (10240, 768)
PK_UKV = (11008, 768)
PK_ROWS = 13312
SMALL_ROWS = 64
SMALL_META = 0
SMALL_CONV = 16
SMALL_REP = 24
SMALL_BITS_ROWS = 48
MATRIX_PLACES = (("w_mlp_up", PK_UP), ("w_mlp_down", PK_DOWN), ("w_in_conv", PK_CONV_IN), ("w_in_attn", PK_ATTN_IN),
                 ("w_out_attn", PK_OUT_ATTN), ("w_out_conv", PK_OUT_CONV), ("w_uq", PK_UQ), ("w_ukv", PK_UKV))


def _put(buf, x, place, *, name, slab=None):
    row_off, col_off = place
    slabs = x.ndim == 3
    rows, w = x.shape[-2:]
    wpad = _round_up(w, LANES)
    assert col_off % wpad == 0
    bt = _common_tile(rows, row_off, align=16)

    def fill(x_ref, o_ref):
        v = x_ref[...].astype(o_ref.dtype)
        if wpad != w:
            v = jnp.concatenate([v, jnp.zeros((bt, wpad - w), o_ref.dtype)], axis=1)
        o_ref[...] = v

    def body(x_ref, _, o_ref):
        fill(x_ref, o_ref)

    if slab is not None:
        grid_spec = pltpu.PrefetchScalarGridSpec(
            num_scalar_prefetch=1, grid=(rows // bt,),
            in_specs=[pl.BlockSpec((bt, w), lambda i, s: (i, 0)), ANY],
            out_specs=pl.BlockSpec((None, bt, wpad), lambda i, s: (s[0], row_off // bt + i, col_off // wpad)))
        return pl.pallas_call(
            lambda s_ref, x_ref, _, o_ref: fill(x_ref, o_ref), name=name, grid_spec=grid_spec,
            out_shape=jax.ShapeDtypeStruct(buf.shape, buf.dtype), input_output_aliases={2: 0},
            compiler_params=_params("parallel"),
        )(jnp.reshape(slab, (1,)).astype(jnp.int32), x, buf)
    if slabs:
        grid = (4, rows // bt)
        x_spec = pl.BlockSpec((None, bt, w), lambda s, i: (s, i, 0))
        o_spec = pl.BlockSpec((None, bt, wpad), lambda s, i: (s, row_off // bt + i, col_off // wpad))
        sem = ("parallel", "parallel")
    else:
        grid = (rows // bt,)
        x_spec = pl.BlockSpec((bt, w), lambda i: (i, 0))
        o_spec = pl.BlockSpec((bt, wpad), lambda i: (row_off // bt + i, col_off // wpad))
        sem = ("parallel",)
    return pl.pallas_call(
        body, name=name, grid=grid, in_specs=[x_spec, ANY], out_specs=o_spec,
        out_shape=jax.ShapeDtypeStruct(buf.shape, buf.dtype), input_output_aliases={1: 0},
        compiler_params=_params(*sem),
    )(x, buf)


def _w_cols(place, layer, rows, width):
    base = (place[0] + layer * rows) // rows
    return dict(n=4 * width, tn=width, tk=rows, spec=pl.BlockSpec((None, rows, width), lambda i, j, k: (j, base, 0)))


def _w_cols_t(place, layer, rows, width):
    base = (place[0] + layer * rows) // rows
    return dict(n=rows, tn=rows, tk=width, spec=pl.BlockSpec((None, rows, width), lambda i, j, k: (k, base, 0)))


def _w_rows(place, layer, rows):
    base = (place[0] + layer * rows) // rows
    return dict(n=D_MODEL, tn=D_MODEL, tk=rows, spec=pl.BlockSpec((None, rows, D_MODEL), lambda i, j, k: (k, base, 0)))


def _w_rows_t(place, layer, rows):
    base = (place[0] + layer * rows) // rows
    return dict(n=4 * rows, tn=rows, tk=D_MODEL, spec=pl.BlockSpec((None, rows, D_MODEL), lambda i, j, k: (j, base, 0)))


def _g_cols(g, place, layer, rows, width):
    base = (place[0] + layer * rows) // rows
    return g, pl.BlockSpec((None, rows, width), lambda i, j, k: (j, base, 0))


def _g_rows(g, place, layer, rows):
    base = (place[0] + layer * rows) // rows
    return g, pl.BlockSpec((None, rows, D_MODEL), lambda i, j, k: (i, base, 0))


def _g_rows_whole(g, place, layer, rows):
    base = (place[0] + layer * rows) // rows
    return g, pl.BlockSpec((4, rows, D_MODEL), lambda i, j, k: (0, base, 0))


IN_PADW = _round_up(SHARD_IN, LANES)
IN_TAIL = ZA_FQ - SHARD_IN
IN_FL = SHARD_IN - HEADS
ZA_KPE = ZA_LAST + KPE_LANE


def _assemble_attn_in(gathered, layer):
    bt = 256
    base = (PK_ATTN_IN[0] + layer * D_MODEL) // bt
    assert 2 * SHARD_IN > ZA_FQ + MLA_ROPE and 3 * SHARD_IN < ATTN_IN - HEADS

    def body(s0, s1, s2, s3, o_ref):
        dt = o_ref.dtype
        z = lambda n: jnp.zeros((bt, n), dt)
        o_ref[...] = jnp.concatenate(
            [s0[:, :SHARD_IN], s1[:, :IN_TAIL], s1[:, IN_TAIL + MLA_ROPE:SHARD_IN], s2[:, :SHARD_IN], s3[:, :IN_FL],
             s3[:, IN_FL:SHARD_IN], z(KPE_LANE - HEADS), s1[:, IN_TAIL:IN_TAIL + MLA_ROPE],
             z(LANES - KPE_LANE - MLA_ROPE)], axis=1).astype(dt)

    def spec(s):
        return pl.BlockSpec((None, bt, IN_PADW), lambda i: (s, base + i, 0))

    return pl.pallas_call(
        body, name="assemble_attn_in", grid=(D_MODEL // bt,),
        in_specs=[spec(s) for s in range(4)], out_specs=pl.BlockSpec((bt, ZA_W), lambda i: (i, 0)),
        out_shape=jax.ShapeDtypeStruct((D_MODEL, ZA_W), MXU_DTYPE), compiler_params=_params("parallel"),
    )(gathered, gathered, gathered, gathered)


def _scatter_attn_in(g, dwa, layer):
    bt = 256
    base = (PK_ATTN_IN[0] + layer * D_MODEL) // bt
    fq1 = ZA_FQ + SHARD_IN - IN_TAIL - MLA_ROPE

    def body(d_ref, _, o_ref):
        pad = jnp.zeros((bt, IN_PADW - SHARD_IN), F32)
        pieces = (
            (d_ref[:, 0:SHARD_IN],),
            (d_ref[:, SHARD_IN:ZA_FQ], d_ref[:, ZA_KPE:ZA_KPE + MLA_ROPE], d_ref[:, ZA_FQ:fq1]),
            (d_ref[:, fq1:fq1 + SHARD_IN],),
            (d_ref[:, fq1 + SHARD_IN:ZA_LAST], d_ref[:, ZA_LAST:ZA_LAST + HEADS]),
        )
        for s in range(4):
            @pl.when(pl.program_id(0) == s)
            def _(s=s):
                o_ref[...] = jnp.concatenate(list(pieces[s]) + [pad], axis=1)

    return pl.pallas_call(
        body, name="scatter_attn_in", grid=(4, D_MODEL // bt),
        in_specs=[pl.BlockSpec((bt, ZA_W), lambda s, i: (i, 0)), ANY],
        out_specs=pl.BlockSpec((None, bt, IN_PADW), lambda s, i: (s, base + i, 0)),
        out_shape=jax.ShapeDtypeStruct(g.shape, g.dtype), input_output_aliases={1: 0},
        compiler_params=_params("parallel", "parallel"),
    )(dwa, g)


def _assemble_uq(gathered, layer):
    bt = 128
    base = (PK_UQ[0] + layer * Q_LORA) // bt
    col = PK_UQ[1] // SIDE_W

    def body(s0, s1, s2, s3, o_ref):
        dt = o_ref.dtype
        z = jnp.zeros((bt, LANES - MLA_QK), dt)
        parts = []
        for s_ref in (s0, s1, s2, s3):
            parts += [s_ref[:, 0:MLA_QK], z, s_ref[:, MLA_QK:2 * MLA_QK], z]
        o_ref[...] = jnp.concatenate(parts, axis=1).astype(dt)

    def spec(s):
        return pl.BlockSpec((None, bt, SIDE_W), lambda i: (s, base + i, col))

    return pl.pallas_call(
        body, name="assemble_uq", grid=(Q_LORA // bt,),
        in_specs=[spec(s) for s in range(4)], out_specs=pl.BlockSpec((bt, HEADS * LANES), lambda i: (i, 0)),
        out_shape=jax.ShapeDtypeStruct((Q_LORA, HEADS * LANES), MXU_DTYPE), compiler_params=_params("parallel"),
    )(gathered, gathered, gathered, gathered)


def _scatter_uq(g, dw, layer):
    bt = 128
    base = (PK_UQ[0] + layer * Q_LORA) // bt
    col = PK_UQ[1] // SIDE_W

    def body(d_ref, _, o_ref):
        o_ref[...] = jnp.concatenate([d_ref[:, 0:MLA_QK], d_ref[:, LANES:LANES + MLA_QK],
                                      jnp.zeros((bt, SIDE_W - 2 * MLA_QK), F32)], axis=1)

    return pl.pallas_call(
        body, name="scatter_uq", grid=(4, Q_LORA // bt),
        in_specs=[pl.BlockSpec((bt, 2 * LANES), lambda s, i: (i, s)), ANY],
        out_specs=pl.BlockSpec((None, bt, SIDE_W), lambda s, i: (s, base + i, col)),
        out_shape=jax.ShapeDtypeStruct(g.shape, g.dtype), input_output_aliases={1: 0},
        compiler_params=_params("parallel", "parallel"),
    )(dw, g)


def _assemble_ukv(gathered, layer):
    bt = KV_LORA
    base = (PK_UKV[0] + layer * KV_LORA) // bt
    col = PK_UKV[1] // SIDE_W
    hd = MLA_NOPE + HEAD_V

    def body(s0, s1, s2, s3, o_ref):
        dt = o_ref.dtype
        z = jnp.zeros((bt, LANES - MLA_NOPE), dt)
        keys, vals = [], []
        for s_ref in (s0, s1, s2, s3):
            for e in range(2):
                keys += [s_ref[:, e * hd:e * hd + MLA_NOPE], z]
                vals.append(s_ref[:, e * hd + MLA_NOPE:(e + 1) * hd])
        o_ref[...] = jnp.concatenate(keys + vals, axis=1).astype(dt)

    def spec(s):
        return pl.BlockSpec((None, bt, SIDE_W), lambda i: (s, base + i, col))

    return pl.pallas_call(
        body, name="assemble_ukv", grid=(1,),
        in_specs=[spec(s) for s in range(4)],
        out_specs=pl.BlockSpec((bt, HEADS * (LANES + HEAD_V)), lambda i: (i, 0)),
        out_shape=jax.ShapeDtypeStruct((KV_LORA, HEADS * (LANES + HEAD_V)), MXU_DTYPE), compiler_params=_params("parallel"),
    )(gathered, gathered, gathered, gathered)


def _scatter_ukv(g, dw, layer):
    bt = KV_LORA
    base = (PK_UKV[0] + layer * KV_LORA) // bt
    col = PK_UKV[1] // SIDE_W

    def body(k_ref, v_ref, _, o_ref):
        o_ref[...] = jnp.concatenate([k_ref[:, 0:MLA_NOPE], v_ref[:, 0:HEAD_V], k_ref[:, LANES:LANES + MLA_NOPE],
                                      v_ref[:, HEAD_V:2 * HEAD_V]], axis=1)

    return pl.pallas_call(
        body, name="scatter_ukv", grid=(4,),
        in_specs=[pl.BlockSpec((bt, 2 * LANES), lambda s: (0, s)),
                  pl.BlockSpec((bt, 2 * HEAD_V), lambda s: (0, HEADS * LANES // (2 * HEAD_V) + s)), ANY],
        out_specs=pl.BlockSpec((None, bt, SIDE_W), lambda s: (s, base, col)),
        out_shape=jax.ShapeDtypeStruct(g.shape, g.dtype), input_output_aliases={2: 0},
        compiler_params=_params("parallel"),
    )(dw, dw, g)


def _pad_lanes(v, n=LANES):
    return jnp.pad(v, (0, n - v.shape[0])).reshape(1, n)


def _relu2_up(acc):
    r = jnp.maximum(acc, 0.0)
    return acc, r * r


def _relu2_bwd(acc, u):
    return (acc * (2.0 * jnp.maximum(u, 0.0)),)


def _add_res(acc, res):
    return (acc + res,)


def _add_res_norm(acc, res, g):
    h = acc + res
    return h, h * lax.rsqrt(jnp.mean(h * h, axis=-1, keepdims=True) + EPS) * g


def _local_step(x, target, meta, small, gathered):
    seq = x.shape[0]
    t = seq + LANES
    d = D_MODEL
    h = jnp.concatenate([jnp.zeros((PAD, d), F32), meta.astype(F32), x], axis=0)
    c_tab, s_tab = _rope_tables(t)
    scale_mla, scale_fox = MLA_QK ** -0.5 * LOG2E, FOX_DIM ** -0.5 * LOG2E
    grads = {}
    saved = []
    g = jnp.zeros((4, PK_ROWS, PACK_W), F32)

    hn = _rmsnorm_fwd(h, small["g_mix"][0])
    for layer in range(DEPTH):
        j = layer // 2
        sv = {"h_in": h}
        sv["hn"] = hn
        g_mlp_row = small["g_mlp"][layer].reshape(1, d)
        if layer % 2 == 0:
            w_in = _assemble_attn_in(gathered, j)
            w_uq = _assemble_uq(gathered, j)
            w_ukv = _assemble_ukv(gathered, j)
            out_place = PK_OUT_ATTN
            vecs = dict(
                g_cq=small["g_cq"][j].reshape(1, Q_LORA), g_ckv=small["g_ckv"][j].reshape(1, KV_LORA),
                g_qf=jnp.tile(small["g_q_fox"][j] * scale_fox, 2).reshape(1, LANES),
                g_kf=jnp.tile(small["g_k_fox"][j], 2).reshape(1, LANES),
                b_f=_pad_lanes(small["b_forget"][j]), g_q=_pad_lanes(small["g_q_mla"][j] * scale_mla),
                g_k=_pad_lanes(small["g_k_mla"][j]))
            z = _matmul(hn, w_in, name="mm_attn_in")
            cqn, ckvn, qf, kf, vf, logf = _prep_a_fwd(z, vecs["g_cq"], vecs["g_ckv"], vecs["g_qf"], vecs["g_kf"], vecs["b_f"])
            f_cum, f_cum_t = _cumsum_rows(logf, reverse=False, name="cumsum_fwd", out_scale=LOG2E)
            q_raw = _matmul(cqn, w_uq, name="mm_uq")
            kv_raw = _matmul(ckvn, w_ukv, name="mm_ukv")
            qn, kn, v_mla = _prep_b_fwd(q_raw, kv_raw, z, c_tab, s_tab, vecs["g_q"], vecs["g_k"])
            o_mla, lse_mla = _flash_fwd_chunked(qn, kn, v_mla, None, None, mla=True, name="flash_fwd_mla")
            o_fox, lse_fox = _flash_fwd_chunked(qf, kf, vf, f_cum, f_cum_t, mla=False, name="flash_fwd_fox")
            o = jnp.concatenate([o_mla, o_fox], axis=1)
            h, hn2 = _matmul(o, gathered, b_tiles=_w_rows(out_place, j, SHARD_MIX), extras=(h, g_mlp_row),
                             epilogue=_add_res_norm, out_dtypes=(F32, MXU_DTYPE), name="mm_mix_out")
            sv.update(w_in=w_in, w_uq=w_uq, w_ukv=w_ukv, out_place=out_place, vecs=vecs, z=z, cqn=cqn, ckvn=ckvn, qf=qf, kf=kf,
                      vf=vf, f_cum=f_cum, f_cum_t=f_cum_t, q_raw=q_raw, kv_raw=kv_raw, qn=qn, kn=kn, v_mla=v_mla, o=o,
                      lse_mla=lse_mla, lse_fox=lse_fox)
        else:
            out_place = PK_OUT_CONV
            conv_w8 = jnp.pad(small["conv_w"][j], ((0, 5), (0, 0)))
            z = _matmul(hn, gathered, b_tiles=_w_cols(PK_CONV_IN, j, d, SHARD_CONV), name="mm_conv_in")
            vmix = _conv_fwd(z, conv_w8)
            h, hn2 = _matmul(vmix, gathered, b_tiles=_w_rows(out_place, j, SHARD_MIX), extras=(h, g_mlp_row),
                             epilogue=_add_res_norm, out_dtypes=(F32, MXU_DTYPE), name="mm_mix_out")
            sv.update(out_place=out_place, conv_w8=conv_w8, z=z, vmix=vmix)
        sv["h_mid"] = h
        u, a = _matmul(hn2, gathered, b_tiles=_w_cols(PK_UP, layer, d, d), epilogue=_relu2_up,
                       out_dtypes=(F32, MXU_DTYPE), name="mm_mlp_up")
        if layer + 1 < DEPTH:
            h, hn = _matmul(a, gathered, b_tiles=_w_rows(PK_DOWN, layer, d),
                            extras=(h, small["g_mix"][layer + 1].reshape(1, d)), epilogue=_add_res_norm,
                            out_dtypes=(F32, MXU_DTYPE), name="mm_mlp_down")
        else:
            h = _matmul(a, gathered, b_tiles=_w_rows(PK_DOWN, layer, d), extras=(h,), epilogue=_add_res,
                        name="mm_mlp_down")
        sv.update(hn2=hn2, u=u, a=a)
        saved.append(sv)

    dh, loss_local = _loss_head(h, target)

    dg_mix, dg_mlp = [None] * DEPTH, [None] * DEPTH
    per_even = {k: [None, None] for k in ("g_cq", "g_ckv", "g_q_mla", "g_k_mla", "g_q_fox", "g_k_fox", "b_forget")}
    per_odd = {"conv_w": [None, None]}
    for layer in reversed(range(DEPTH)):
        j = layer // 2
        sv = saved[layer]
        du = _matmul(dh, gathered, tb=True, b_tiles=_w_rows_t(PK_DOWN, layer, d), extras=(sv["u"],),
                     epilogue=_relu2_bwd, out_dtypes=(MXU_DTYPE,), name="mm_mlp_da")
        g = _matmul(sv["a"], dh, ta=True, out_into=_g_rows(g, PK_DOWN, layer, d), name="mm_dw_down")
        g = _matmul(sv["hn2"], du, ta=True, out_into=_g_cols(g, PK_UP, layer, d, d), name="mm_dw_up")
        dhn2 = _matmul(du, gathered, tb=True, b_tiles=_w_cols_t(PK_UP, layer, d, d), name="mm_mlp_dhn")
        dh, dg_mlp[layer] = _rmsnorm_bwd(sv["h_mid"], small["g_mlp"][layer], dhn2, dh)
        do = _matmul(dh, gathered, tb=True, b_tiles=_w_rows_t(sv["out_place"], j, SHARD_MIX), out_dtypes=(MXU_DTYPE,),
                     name="mm_mix_do")
        if layer % 2 == 0:
            vecs = sv["vecs"]
            g = _matmul(sv["o"], dh, ta=True, out_into=_g_rows_whole(g, PK_OUT_ATTN, j, SHARD_MIX),
                        name="mm_dw_out")
            delta_t = _delta_rows(do, sv["o"])
            dqn, dkn, dv_mla = _flash_bwd_fused(sv["qn"], sv["kn"], sv["v_mla"], do, sv["lse_mla"], delta_t, None, None,
                                                mla=True, col0=0, name="flash_bwd_mla")
            dqf, dkf, dvf, rs_t, cs = _flash_bwd_fused(sv["qf"], sv["kf"], sv["vf"], do, sv["lse_fox"], delta_t,
                                                       sv["f_cum"], sv["f_cum_t"], mla=False, col0=HEADS // 2,
                                                       name="flash_bwd_fox")
            d_f = rs_t.reshape(HEADS // 2, 8, t)[:, :2, :].reshape(HEADS, t).T
            d_f = d_f - cs.reshape(t, HEADS // 2, LANES)[:, :, :2].reshape(t, HEADS)
            d_f = jnp.pad(d_f, ((0, 0), (0, LANES - HEADS)))
            dlogf, _ = _cumsum_rows(d_f, reverse=True, name="cumsum_bwd")
            dq_raw, dkv_raw, dkpe, dg_q, dg_k = _prep_b_bwd(sv["q_raw"], sv["kv_raw"], sv["z"], c_tab, s_tab, vecs["g_q"],
                                                            vecs["g_k"], dqn, dkn, dv_mla)
            g = _scatter_uq(g, _matmul(sv["cqn"], dq_raw, ta=True, name="mm_dw_uq"), j)
            g = _scatter_ukv(g, _matmul(sv["ckvn"], dkv_raw, ta=True, name="mm_dw_ukv"), j)
            dcqn = _matmul(dq_raw, sv["w_uq"], tb=True, name="mm_dcqn")
            dckvn = _matmul(dkv_raw, sv["w_ukv"], tb=True, name="mm_dckvn")
            dz, dg_cq, dg_ckv, dg_qf, dg_kf, db_f = _prep_a_bwd(
                sv["z"], vecs["g_cq"], vecs["g_ckv"], vecs["g_qf"], vecs["g_kf"], vecs["b_f"], dcqn, dckvn, dqf, dkf, dvf,
                dlogf, dkpe)
            g = _scatter_attn_in(g, _matmul(sv["hn"], dz, ta=True, name="mm_dw_attn_in"), j)
            per_even["g_cq"][j] = dg_cq[0]
            per_even["g_ckv"][j] = dg_ckv[0]
            per_even["g_q_mla"][j] = dg_q[0, :MLA_QK] * scale_mla
            per_even["g_k_mla"][j] = dg_k[0, :MLA_QK]
            per_even["g_q_fox"][j] = (dg_qf[0, :FOX_DIM] + dg_qf[0, FOX_DIM:]) * scale_fox
            per_even["g_k_fox"][j] = dg_kf[0, :FOX_DIM] + dg_kf[0, FOX_DIM:]
            per_even["b_forget"][j] = db_f[0, :HEADS]
            dhn = _matmul(dz, sv["w_in"], tb=True, name="mm_attn_dhn")
        else:
            g = _matmul(sv["vmix"], dh, ta=True, out_into=_g_rows_whole(g, PK_OUT_CONV, j, SHARD_MIX),
                        name="mm_dw_out")
            dz, dcw = _conv_bwd(sv["z"], sv["conv_w8"], do)
            per_odd["conv_w"][j] = dcw[:3]
            g = _matmul(sv["hn"], dz, ta=True, tn=SHARD_CONV, out_into=_g_cols(g, PK_CONV_IN, j, d, SHARD_CONV),
                        name="mm_dw_conv_in")
            dhn = _matmul(dz, gathered, tb=True, b_tiles=_w_cols_t(PK_CONV_IN, j, d, SHARD_CONV), name="mm_conv_dhn")
        dh, dg_mix[layer] = _rmsnorm_bwd(sv["h_in"], small["g_mix"][layer], dhn, dh)

    grads["meta_tokens"] = dh[PAD:LANES]
    grads["g_mix"] = jnp.stack(dg_mix)
    grads["g_mlp"] = jnp.stack(dg_mlp)
    for k, v in list(per_even.items()) + list(per_odd.items()):
        grads[k] = jnp.stack(v)
    return loss_local, dh[LANES:], g, grads


def kernel(x, meta_tokens, g_mix, g_mlp, w_in_attn, g_cq, w_uq, g_ckv, w_ukv, g_q_mla, g_k_mla, g_q_fox, g_k_fox, b_forget, w_out_attn, w_in_conv, conv_w, w_out_conv, w_mlp_up, w_mlp_down, loss_target, m_meta_tokens, m_g_mix, m_g_mlp, m_w_in_attn, m_g_cq, m_w_uq, m_g_ckv, m_w_ukv, m_g_q_mla, m_g_k_mla, m_g_q_fox, m_g_k_fox, m_b_forget, m_w_out_attn, m_w_in_conv, m_conv_w, m_w_out_conv, m_w_mlp_up, m_w_mlp_down, v_meta_tokens, v_g_mix, v_g_mlp, v_w_in_attn, v_g_cq, v_w_uq, v_g_ckv, v_w_ukv, v_g_q_mla, v_g_k_mla, v_g_q_fox, v_g_k_fox, v_b_forget, v_w_out_attn, v_w_in_conv, v_conv_w, v_w_out_conv, v_w_mlp_up, v_w_mlp_down):
    args = dict(locals())
    weights = {n: args[n] for n in WEIGHT_ORDER}
    mom_m = {n: args["m_" + n] for n in WEIGHT_ORDER}
    mom_v = {n: args["v_" + n] for n in WEIGHT_ORDER}

    wire = jnp.bfloat16
    me = 2 * lax.axis_index("x") + lax.axis_index("y")
    buf = jnp.zeros((4, PK_ROWS, PACK_W), wire)
    for name, place in MATRIX_PLACES:
        w = weights[name]
        buf = _put(buf, w.reshape(-1, w.shape[-1]), place, name="pack_weights", slab=me)
    meta_bits = lax.bitcast_convert_type(meta_tokens, wire).reshape(2 * N_META, SIDE_W)
    conv_bits = lax.bitcast_convert_type(conv_w, wire).reshape(2 * N_ODD * 3, SIDE_W)
    bits = jnp.concatenate([meta_bits, conv_bits, jnp.zeros((SMALL_BITS_ROWS - 2 * N_META - 2 * N_ODD * 3, SIDE_W), wire)])
    buf = _put(buf, bits, PK_SMALL, name="pack_weights", slab=me)
    gathered = _allgather_chips(buf)
    got_bits = gathered[:, PK_SMALL[0]:PK_SMALL[0] + SMALL_BITS_ROWS, PK_SMALL[1]:PK_SMALL[1] + SIDE_W]
    meta_full = lax.bitcast_convert_type(got_bits[:, :2 * N_META].reshape(4, N_META, SIDE_W, 2), F32)
    meta_full = meta_full.transpose(1, 0, 2).reshape(N_META, D_MODEL)
    conv_full = lax.bitcast_convert_type(
        got_bits[:, 2 * N_META:2 * N_META + 2 * N_ODD * 3].reshape(4, N_ODD, 3, SIDE_W, 2), F32)
    small = {n: weights[n] for n in REPLICATED}
    small["conv_w"] = conv_full.transpose(1, 2, 0, 3).reshape(N_ODD, 3, D_MODEL)

    loss_local, grad_x, g, grads = _local_step(x[0], loss_target[0], meta_full, small, gathered)
    loss = lax.psum(loss_local, MESH_AXES)

    rep = jnp.concatenate([grads[n].reshape(-1) for n in REPLICATED])
    rep = jnp.pad(rep, (0, (SMALL_ROWS - SMALL_REP) * SIDE_W - rep.shape[0])).reshape(SMALL_ROWS - SMALL_REP, SIDE_W)
    g_meta = grads["meta_tokens"].reshape(N_META, 4, SIDE_W).transpose(1, 0, 2)
    g_conv = grads["conv_w"].reshape(N_ODD * 3, 4, SIDE_W).transpose(1, 0, 2)
    small4 = jnp.concatenate([g_meta, g_conv, jnp.zeros((4, SMALL_REP - SMALL_CONV - N_ODD * 3, SIDE_W), F32),
                              jnp.broadcast_to(rep[None], (4,) + rep.shape)], axis=1)
    g = _put(g, small4, PK_SMALL, name="pack_small_grads")
    half = PK_ROWS // 2
    c = lax.axis_index("c")
    got = _swap_halves(g)
    pair = _add_half(g, got, c, out_dtype=jnp.bfloat16, name="grad_pair_sum")
    total = _sum4(_scatter_chips(pair), c, name="grad_chip_sum")
    g_tot = _join_halves(total).reshape(PK_ROWS, PACK_W)

    out = {}
    for name, place in MATRIX_PLACES:
        shape = weights[name].shape
        two_d = lambda a: a.reshape(-1, shape[-1])
        res = _adamw(two_d(weights[name]), two_d(mom_m[name]), two_d(mom_v[name]), g_tot, place[0], place[1])
        out[name] = [r.reshape(shape) for r in res]

    def small_pack(src):
        flat = jnp.concatenate([src[n].reshape(-1) for n in REPLICATED])
        flat = jnp.pad(flat, (0, (SMALL_ROWS - SMALL_REP) * SIDE_W - flat.shape[0])).reshape(SMALL_ROWS - SMALL_REP, SIDE_W)
        return jnp.concatenate([src["meta_tokens"], src["conv_w"].reshape(N_ODD * 3, SIDE_W),
                                jnp.zeros((SMALL_REP - SMALL_CONV - N_ODD * 3, SIDE_W), F32), flat])

    res = _adamw(small_pack(weights), small_pack(mom_m), small_pack(mom_v), g_tot, PK_SMALL[0], PK_SMALL[1])
    for name in ("meta_tokens", "conv_w") + REPLICATED:
        out[name] = []
    for r in res:
        out["meta_tokens"].append(r[SMALL_META:SMALL_META + N_META])
        out["conv_w"].append(r[SMALL_CONV:SMALL_CONV + N_ODD * 3].reshape(N_ODD, 3, SIDE_W))
        flat, off = r[SMALL_REP:].reshape(-1), 0
        for name in REPLICATED:
            n = weights[name].size
            out[name].append(flat[off:off + n].reshape(weights[name].shape))
            off += n
    return (loss, grad_x[None], *[out[n][0] for n in WEIGHT_ORDER], *[out[n][1] for n in WEIGHT_ORDER],
            *[out[n][2] for n in WEIGHT_ORDER], *[out[n][3] for n in WEIGHT_ORDER])
```

```python
import functools

import jax
import jax.numpy as jnp
import numpy as np
from jax import lax
from jax.experimental import pallas as pl
from jax.experimental.pallas import tpu as pltpu

F32 = jnp.float32
MXU_DTYPE = jnp.bfloat16

D_MODEL = 1024
N_META = 16
LANES = 128
PAD = LANES - N_META
HEADS = 8
Q_LORA = 384
KV_LORA = 256
MLA_NOPE = 64
MLA_ROPE = 32
MLA_QK = MLA_NOPE + MLA_ROPE
HEAD_V = 64
FOX_DIM = 64
ROPE_BASE = 10000.0
D_FF = 4 * D_MODEL
DEPTH = 4
EPS = 1e-6
NEG = -1e30
ATTN_IN = Q_LORA + KV_LORA + MLA_ROPE + 3 * HEADS * FOX_DIM + HEADS

ZA_CQ = 0
ZA_CKV = Q_LORA
ZA_FQ = ZA_CKV + KV_LORA
ZA_FK = ZA_FQ + HEADS * FOX_DIM
ZA_FV = ZA_FK + HEADS * FOX_DIM
ZA_LAST = ZA_FV + HEADS * FOX_DIM
ZA_W = ZA_LAST + LANES
KPE_LANE = MLA_NOPE

ADAM_LR = 0.001
ADAM_B1 = 0.9
ADAM_B2 = 0.999
ADAM_EPS = 1e-08
ADAM_WD = 0.01
ADAM_STEP = 10

VMEM_LIMIT_BYTES = 52 * 1024 * 1024
ROW_TILE = 1040
PREP_TILE = 320
ATTN_BLOCK = 640
LOSS_TILE = 128
MAX_TILE = 1536

MESH_AXES = ("x", "y", "c")
MESH = pl.DeviceIdType.MESH


def _params(*sem):
    return pltpu.CompilerParams(dimension_semantics=sem, vmem_limit_bytes=VMEM_LIMIT_BYTES)


def _tile(n, cap=None):
    cap = MAX_TILE if cap is None else cap
    if n <= cap:
        return n
    best = None
    for t in range(LANES, cap + 1, LANES):
        if n % t == 0:
            best = t
    assert best is not None, n
    return best


def _iota(shape, dim):
    return lax.broadcasted_iota(jnp.int32, shape, dim)


def _matmul(a, b, *, ta=False, tb=False, extras=(), epilogue=None, out_dtypes=(F32,), name, b_tiles=None,
            out_into=None, tm=None, tn=None):
    if ta:
        kdim, m = a.shape
    else:
        m, kdim = a.shape
    row_tile = min(ROW_TILE, m)
    if ta:
        tm_auto, tk = _tile(m), min(ATTN_BLOCK, kdim)
    else:
        tm_auto, tk = (row_tile if m % row_tile == 0 else _tile(m)), _tile(kdim)
    tm = tm_auto if tm is None else tm
    if b_tiles is None:
        n = b.shape[0] if tb else b.shape[1]
        assert (b.shape[1] if tb else b.shape[0]) == kdim, (a.shape, b.shape, ta, tb)
        tn = _tile(n) if tn is None else tn
        b_spec = pl.BlockSpec((tn, tk), lambda i, j, k: (j, k)) if tb else pl.BlockSpec((tk, tn), lambda i, j, k: (k, j))
    else:
        n, tn, tk, b_spec = b_tiles["n"], b_tiles["tn"], b_tiles["tk"], b_tiles["spec"]
    nm, nn, nk = m // tm, n // tn, kdim // tk
    assert nm * tm == m and nn * tn == n and nk * tk == kdim, (m, n, kdim, tm, tn, tk)
    n_ex, n_out = len(extras), len(out_dtypes)
    n_alias = 0 if out_into is None else 1
    assert n_out == 1 or out_into is None
    dims = (((0 if ta else 1,), (1 if tb else 0,)), ((), ()))
    if epilogue is None:
        epilogue = lambda acc: (acc,)

    def body(a_ref, b_ref, *rest):
        ex_refs, out_refs, acc_ref = rest[:n_ex], rest[n_ex + n_alias:n_ex + n_alias + n_out], rest[-1]
        k = pl.program_id(2)

        @pl.when(k == 0)
        def _():
            acc_ref[...] = jnp.zeros_like(acc_ref)

        acc_ref[...] += lax.dot_general(a_ref[...].astype(MXU_DTYPE), b_ref[...].astype(MXU_DTYPE), dims,
                                        preferred_element_type=F32)

        @pl.when(k == nk - 1)
        def _():
            res = epilogue(acc_ref[...], *[e[...] for e in ex_refs])
            for o_ref, r in zip(out_refs, res):
                o_ref[...] = r.reshape(o_ref.shape).astype(o_ref.dtype)

    a_spec = pl.BlockSpec((tk, tm), lambda i, j, k: (k, i)) if ta else pl.BlockSpec((tm, tk), lambda i, j, k: (i, k))
    mn_spec = pl.BlockSpec((tm, tn), lambda i, j, k: (i, j))
    row_spec = pl.BlockSpec((1, tn), lambda i, j, k: (0, j))
    ex_specs = [row_spec if e.shape[0] == 1 else mn_spec for e in extras]
    if out_into is None:
        outs = pl.pallas_call(
            body, name=name, grid=(nm, nn, nk),
            in_specs=[a_spec, b_spec] + ex_specs,
            out_specs=[mn_spec] * n_out,
            out_shape=[jax.ShapeDtypeStruct((m, n), dt) for dt in out_dtypes],
            scratch_shapes=[pltpu.VMEM((tm, tn), F32)],
            compiler_params=_params("parallel", "parallel", "arbitrary"),
        )(a, b, *extras)
        return outs[0] if n_out == 1 else outs
    buf, buf_spec = out_into
    return pl.pallas_call(
        body, name=name, grid=(nm, nn, nk),
        in_specs=[a_spec, b_spec] + ex_specs + [ANY],
        out_specs=buf_spec,
        out_shape=jax.ShapeDtypeStruct(buf.shape, buf.dtype),
        input_output_aliases={2 + n_ex: 0},
        scratch_shapes=[pltpu.VMEM((tm, tn), F32)],
        compiler_params=_params("parallel", "parallel", "arbitrary"),
    )(a, b, *extras, buf)


def _rmsnorm_fwd(x, g, *, name="rmsnorm_fwd"):
    t, d = x.shape
    bt = min(ROW_TILE, t)

    def body(x_ref, g_ref, o_ref):
        xv = x_ref[...]
        r = lax.rsqrt(jnp.mean(xv * xv, axis=-1, keepdims=True) + EPS)
        o_ref[...] = (xv * r * g_ref[...]).astype(o_ref.dtype)

    return pl.pallas_call(
        body, name=name, grid=(t // bt,),
        in_specs=[pl.BlockSpec((bt, d), lambda i: (i, 0)), pl.BlockSpec((1, d), lambda i: (0, 0))],
        out_specs=pl.BlockSpec((bt, d), lambda i: (i, 0)),
        out_shape=jax.ShapeDtypeStruct((t, d), MXU_DTYPE),
        compiler_params=_params("parallel"),
    )(x, g.reshape(1, d))


def _rmsnorm_bwd(x, g, dy, dres, *, name="rmsnorm_bwd"):
    t, d = x.shape
    bt = min(ROW_TILE, t)

    def body(x_ref, g_ref, dy_ref, dres_ref, dx_ref, dg_ref):
        @pl.when(pl.program_id(0) == 0)
        def _():
            dg_ref[...] = jnp.zeros_like(dg_ref)

        xv, dyv = x_ref[...], dy_ref[...].astype(F32)
        r = lax.rsqrt(jnp.mean(xv * xv, axis=-1, keepdims=True) + EPS)
        xhat = xv * r
        dxhat = dyv * g_ref[...]
        dx = r * (dxhat - xhat * jnp.mean(dxhat * xhat, axis=-1, keepdims=True))
        dx_ref[...] = dres_ref[...] + dx
        dg_ref[...] += jnp.sum(dyv * xhat, axis=0, keepdims=True)

    row = pl.BlockSpec((bt, d), lambda i: (i, 0))
    vec = pl.BlockSpec((1, d), lambda i: (0, 0))
    dx, dg = pl.pallas_call(
        body, name=name, grid=(t // bt,),
        in_specs=[row, vec, row, row], out_specs=[row, vec],
        out_shape=[jax.ShapeDtypeStruct((t, d), F32), jax.ShapeDtypeStruct((1, d), F32)],
        compiler_params=_params("arbitrary"),
    )(x, g.reshape(1, d), dy, dres)
    return dx, dg.reshape(d)


def _pair_rms(x, lo):
    x2 = x * x
    s_lo = jnp.sum(jnp.where(lo, x2, 0.0), axis=-1, keepdims=True)
    s_hi = jnp.sum(jnp.where(lo, 0.0, x2), axis=-1, keepdims=True)
    return jnp.where(lo, lax.rsqrt(s_lo / FOX_DIM + EPS), lax.rsqrt(s_hi / FOX_DIM + EPS))


def _pair_sum(x, lo):
    s_lo = jnp.sum(jnp.where(lo, x, 0.0), axis=-1, keepdims=True)
    s_hi = jnp.sum(jnp.where(lo, 0.0, x), axis=-1, keepdims=True)
    return jnp.where(lo, s_lo, s_hi)


def _prep_a_fwd(z, g_cq, g_ckv, g_qf, g_kf, b_f):
    t = z.shape[0]
    bt = min(PREP_TILE, t)
    hw = HEADS * FOX_DIM

    def body(z_ref, gcq_ref, gckv_ref, gqf_ref, gkf_ref, bf_ref, cqn_ref, ckvn_ref, qf_ref, kf_ref, vf_ref, logf_ref):
        i = pl.program_id(0)
        cq = z_ref[:, ZA_CQ:ZA_CQ + Q_LORA]
        cqn_ref[...] = (cq * lax.rsqrt(jnp.mean(cq * cq, axis=-1, keepdims=True) + EPS) * gcq_ref[...]).astype(cqn_ref.dtype)
        ckv = z_ref[:, ZA_CKV:ZA_CKV + KV_LORA]
        ckvn_ref[...] = (ckv * lax.rsqrt(jnp.mean(ckv * ckv, axis=-1, keepdims=True) + EPS) * gckv_ref[...]).astype(ckvn_ref.dtype)
        lo = _iota((1, LANES), 1) < FOX_DIM
        for p in range(HEADS // 2):
            sl = slice(p * LANES, (p + 1) * LANES)
            xq = z_ref[:, ZA_FQ + p * LANES:ZA_FQ + (p + 1) * LANES]
            qf_ref[:, sl] = (xq * _pair_rms(xq, lo) * gqf_ref[...]).astype(qf_ref.dtype)
            xk = z_ref[:, ZA_FK + p * LANES:ZA_FK + (p + 1) * LANES]
            kf_ref[:, sl] = (xk * _pair_rms(xk, lo) * gkf_ref[...]).astype(kf_ref.dtype)
        vf_ref[...] = z_ref[:, ZA_FV:ZA_FV + hw].astype(vf_ref.dtype)
        xl = z_ref[:, ZA_LAST:ZA_LAST + LANES] + bf_ref[...]
        logf = jnp.minimum(xl, 0.0) - jnp.log(1.0 + jnp.exp(-jnp.abs(xl)))
        row = i * bt + _iota((bt, LANES), 0)
        lane = _iota((bt, LANES), 1)
        logf_ref[...] = jnp.where((lane < HEADS) & (row >= PAD), logf, 0.0)

    def vec(w):
        return pl.BlockSpec((1, w), lambda i: (0, 0))

    def row(w):
        return pl.BlockSpec((bt, w), lambda i: (i, 0))

    return pl.pallas_call(
        body, name="attn_prep_a_fwd", grid=(t // bt,),
        in_specs=[row(ZA_W), vec(Q_LORA), vec(KV_LORA), vec(LANES), vec(LANES), vec(LANES)],
        out_specs=[row(Q_LORA), row(KV_LORA), row(hw), row(hw), row(hw), row(LANES)],
        out_shape=[jax.ShapeDtypeStruct((t, Q_LORA), MXU_DTYPE), jax.ShapeDtypeStruct((t, KV_LORA), MXU_DTYPE),
                   jax.ShapeDtypeStruct((t, hw), MXU_DTYPE), jax.ShapeDtypeStruct((t, hw), MXU_DTYPE),
                   jax.ShapeDtypeStruct((t, hw), MXU_DTYPE), jax.ShapeDtypeStruct((t, LANES), F32)],
        compiler_params=_params("parallel"),
    )(z, g_cq, g_ckv, g_qf, g_kf, b_f)


def _prep_a_bwd(z, g_cq, g_ckv, g_qf, g_kf, b_f, dcqn, dckvn, dqf, dkf, dvf, dlogf, dkpe):
    t = z.shape[0]
    bt = min(PREP_TILE, t)
    hw = HEADS * FOX_DIM

    def norm_bwd(x, g, dy):
        r = lax.rsqrt(jnp.mean(x * x, axis=-1, keepdims=True) + EPS)
        xhat = x * r
        dxhat = dy * g
        dx = r * (dxhat - xhat * jnp.mean(dxhat * xhat, axis=-1, keepdims=True))
        return dx, jnp.sum(dy * xhat, axis=0, keepdims=True)

    def body(z_ref, gcq_ref, gckv_ref, gqf_ref, gkf_ref, bf_ref, dcqn_ref, dckvn_ref, dqf_ref, dkf_ref, dvf_ref,
             dlogf_ref, dkpe_ref, dz_ref, dgcq_ref, dgckv_ref, dgqf_ref, dgkf_ref, dbf_ref):
        i = pl.program_id(0)

        @pl.when(i == 0)
        def _():
            for r in (dgcq_ref, dgckv_ref, dgqf_ref, dgkf_ref, dbf_ref):
                r[...] = jnp.zeros_like(r)

        dx, dg = norm_bwd(z_ref[:, ZA_CQ:ZA_CQ + Q_LORA], gcq_ref[...], dcqn_ref[...])
        dz_ref[:, ZA_CQ:ZA_CQ + Q_LORA] = dx.astype(dz_ref.dtype)
        dgcq_ref[...] += dg
        dx, dg = norm_bwd(z_ref[:, ZA_CKV:ZA_CKV + KV_LORA], gckv_ref[...], dckvn_ref[...])
        dz_ref[:, ZA_CKV:ZA_CKV + KV_LORA] = dx.astype(dz_ref.dtype)
        dgckv_ref[...] += dg
        lo = _iota((1, LANES), 1) < FOX_DIM
        for base, g_ref, dy_ref, dg_ref in ((ZA_FQ, gqf_ref, dqf_ref, dgqf_ref), (ZA_FK, gkf_ref, dkf_ref, dgkf_ref)):
            for p in range(HEADS // 2):
                x = z_ref[:, base + p * LANES:base + (p + 1) * LANES]
                dy = dy_ref[:, p * LANES:(p + 1) * LANES]
                r = _pair_rms(x, lo)
                xhat = x * r
                dxhat = dy * g_ref[...]
                dx = r * (dxhat - xhat * _pair_sum(dxhat * xhat, lo) / FOX_DIM)
                dz_ref[:, base + p * LANES:base + (p + 1) * LANES] = dx.astype(dz_ref.dtype)
                dg_ref[...] += jnp.sum(dy * xhat, axis=0, keepdims=True)
        dz_ref[:, ZA_FV:ZA_FV + hw] = dvf_ref[...].astype(dz_ref.dtype)
        xl = z_ref[:, ZA_LAST:ZA_LAST + LANES] + bf_ref[...]
        row = i * bt + _iota((bt, LANES), 0)
        lane = _iota((bt, LANES), 1)
        dfl = jnp.where((lane < HEADS) & (row >= PAD), dlogf_ref[...] / (1.0 + jnp.exp(xl)), 0.0)
        dbf_ref[...] += jnp.sum(dfl, axis=0, keepdims=True)
        dz_ref[:, ZA_LAST:ZA_LAST + LANES] = (dfl + dkpe_ref[...]).astype(dz_ref.dtype)

    def vec(w):
        return pl.BlockSpec((1, w), lambda i: (0, 0))

    def row(w):
        return pl.BlockSpec((bt, w), lambda i: (i, 0))

    return pl.pallas_call(
        body, name="attn_prep_a_bwd", grid=(t // bt,),
        in_specs=[row(ZA_W), vec(Q_LORA), vec(KV_LORA), vec(LANES), vec(LANES), vec(LANES),
                  row(Q_LORA), row(KV_LORA), row(hw), row(hw), row(hw), row(LANES), row(LANES)],
        out_specs=[row(ZA_W), vec(Q_LORA), vec(KV_LORA), vec(LANES), vec(LANES), vec(LANES)],
        out_shape=[jax.ShapeDtypeStruct((t, ZA_W), MXU_DTYPE), jax.ShapeDtypeStruct((1, Q_LORA), F32),
                   jax.ShapeDtypeStruct((1, KV_LORA), F32), jax.ShapeDtypeStruct((1, LANES), F32),
                   jax.ShapeDtypeStruct((1, LANES), F32), jax.ShapeDtypeStruct((1, LANES), F32)],
        compiler_params=_params("arbitrary"),
    )(z, g_cq, g_ckv, g_qf, g_kf, b_f, dcqn, dckvn, dqf, dkf, dvf, dlogf, dkpe)


def _cumsum_rows(x, *, reverse, name, out_scale=1.0):
    t = x.shape[0]
    nblk = t // LANES

    def body(x_ref, f_ref, ft_ref, carry_ref):
        r = _iota((LANES, LANES), 0)
        c = _iota((LANES, LANES), 1)
        tri = jnp.where((c >= r) if reverse else (c <= r), 1.0, 0.0).astype(F32)
        carry_ref[...] = jnp.zeros_like(carry_ref)

        def step(s, _):
            b = (nblk - 1 - s) if reverse else s
            start = pl.multiple_of(b * LANES, LANES)
            blk = x_ref[pl.ds(start, LANES), :]
            cs = jnp.dot(tri, blk, precision=lax.Precision.HIGHEST, preferred_element_type=F32) + carry_ref[0:1, :]
            scaled = cs if out_scale == 1.0 else cs * out_scale
            f_ref[pl.ds(start, LANES), :] = scaled
            ft_ref[:, pl.ds(start, LANES)] = scaled.T
            carry_ref[0:1, :] = cs[0:1, :] if reverse else cs[LANES - 1:LANES, :]
            return 0

        lax.fori_loop(0, nblk, step, 0)

    return pl.pallas_call(
        body, name=name, grid=(1,),
        in_specs=[pl.BlockSpec((t, LANES), lambda i: (0, 0))],
        out_specs=[pl.BlockSpec((t, LANES), lambda i: (0, 0)), pl.BlockSpec((LANES, t), lambda i: (0, 0))],
        out_shape=[jax.ShapeDtypeStruct((t, LANES), F32), jax.ShapeDtypeStruct((LANES, t), F32)],
        scratch_shapes=[pltpu.VMEM((8, LANES), F32)],
        compiler_params=_params("arbitrary"),
    )(x)


def _rope_partner(x, lane):
    half = MLA_ROPE // 2
    swapped = jnp.where(lane < KPE_LANE + half, pltpu.roll(x, LANES - half, axis=1), pltpu.roll(x, half, axis=1))
    return jnp.where((lane >= KPE_LANE) & (lane < KPE_LANE + MLA_ROPE), swapped, 0.0)


def _rope_tables(t):
    pos = (jnp.arange(t, dtype=jnp.int32) - PAD).astype(F32)
    inv_freq = ROPE_BASE ** (-jnp.arange(0, MLA_ROPE, 2, dtype=F32) / MLA_ROPE)
    ang = pos[:, None] * inv_freq[None, :]
    cos, sin = jnp.cos(ang), jnp.sin(ang)
    ones = jnp.ones((t, KPE_LANE), F32)
    tail = jnp.zeros((t, LANES - KPE_LANE - MLA_ROPE), F32)
    c_tab = jnp.concatenate([ones, cos, cos, tail + 1.0], axis=1)
    s_tab = jnp.concatenate([ones * 0.0, -sin, sin, tail], axis=1)
    return c_tab, s_tab


def _prep_b_fwd(q_raw, kv_raw, z, c_tab, s_tab, g_q, g_k):
    t = q_raw.shape[0]
    bt = min(PREP_TILE, t)
    qw = HEADS * LANES
    vw = HEADS * HEAD_V

    def body(q_ref, kv_ref, zl_ref, c_ref, s_ref, gq_ref, gk_ref, qn_ref, kn_ref, v_ref):
        lane = _iota((1, LANES), 1)
        kpe = jnp.where((lane >= KPE_LANE) & (lane < KPE_LANE + MLA_ROPE), zl_ref[...], 0.0)
        cv, sv = c_ref[...], s_ref[...]
        for h in range(HEADS):
            sl = slice(h * LANES, (h + 1) * LANES)
            for x, g_ref, o_ref in ((q_ref[:, sl], gq_ref, qn_ref), (kv_ref[:, sl] + kpe, gk_ref, kn_ref)):
                r = lax.rsqrt(jnp.sum(x * x, axis=-1, keepdims=True) / MLA_QK + EPS)
                xn = x * r * g_ref[...]
                o_ref[:, sl] = (xn * cv + _rope_partner(xn, lane) * sv).astype(o_ref.dtype)
        v_ref[...] = kv_ref[:, qw:qw + vw].astype(v_ref.dtype)

    def row(w):
        return pl.BlockSpec((bt, w), lambda i: (i, 0))

    vec = pl.BlockSpec((1, LANES), lambda i: (0, 0))
    return pl.pallas_call(
        body, name="attn_prep_b_fwd", grid=(t // bt,),
        in_specs=[row(qw), row(qw + vw), pl.BlockSpec((bt, LANES), lambda i: (i, ZA_LAST // LANES)),
                  row(LANES), row(LANES), vec, vec],
        out_specs=[row(qw), row(qw), row(vw)],
        out_shape=[jax.ShapeDtypeStruct((t, qw), MXU_DTYPE), jax.ShapeDtypeStruct((t, qw), MXU_DTYPE),
                   jax.ShapeDtypeStruct((t, vw), MXU_DTYPE)],
        compiler_params=_params("parallel"),
    )(q_raw, kv_raw, z, c_tab, s_tab, g_q, g_k)


def _prep_b_bwd(q_raw, kv_raw, z, c_tab, s_tab, g_q, g_k, dqn, dkn, dv):
    t = q_raw.shape[0]
    bt = min(PREP_TILE, t)
    qw = HEADS * LANES
    vw = HEADS * HEAD_V

    def body(q_ref, kv_ref, zl_ref, c_ref, s_ref, gq_ref, gk_ref, dqn_ref, dkn_ref, dv_ref,
             dq_ref, dkv_ref, dkpe_ref, dgq_ref, dgk_ref):
        @pl.when(pl.program_id(0) == 0)
        def _():
            dgq_ref[...] = jnp.zeros_like(dgq_ref)
            dgk_ref[...] = jnp.zeros_like(dgk_ref)

        lane = _iota((1, LANES), 1)
        rope_lanes = (lane >= KPE_LANE) & (lane < KPE_LANE + MLA_ROPE)
        kpe = jnp.where(rope_lanes, zl_ref[...], 0.0)
        cv, sv = c_ref[...], s_ref[...]
        dkpe = jnp.zeros((bt, LANES), F32)
        for h in range(HEADS):
            sl = slice(h * LANES, (h + 1) * LANES)
            for is_k, x, g_ref, dout, dg_ref in ((False, q_ref[:, sl], gq_ref, dqn_ref[:, sl], dgq_ref),
                                                  (True, kv_ref[:, sl] + kpe, gk_ref, dkn_ref[:, sl], dgk_ref)):
                r = lax.rsqrt(jnp.sum(x * x, axis=-1, keepdims=True) / MLA_QK + EPS)
                xhat = x * r
                dxn = dout * cv + _rope_partner(dout * sv, lane)
                dg_ref[...] += jnp.sum(dxn * xhat, axis=0, keepdims=True)
                dxhat = dxn * g_ref[...]
                dx = r * (dxhat - xhat * (jnp.sum(dxhat * xhat, axis=-1, keepdims=True) / MLA_QK))
                if is_k:
                    dkv_ref[:, sl] = jnp.where(lane < KPE_LANE, dx, 0.0).astype(dkv_ref.dtype)
                    dkpe = dkpe + jnp.where(rope_lanes, dx, 0.0)
                else:
                    dq_ref[:, sl] = dx.astype(dq_ref.dtype)
        dkv_ref[:, qw:qw + vw] = dv_ref[...].astype(dkv_ref.dtype)
        dkpe_ref[...] = dkpe

    def row(w):
        return pl.BlockSpec((bt, w), lambda i: (i, 0))

    vec = pl.BlockSpec((1, LANES), lambda i: (0, 0))
    return pl.pallas_call(
        body, name="attn_prep_b_bwd", grid=(t // bt,),
        in_specs=[row(qw), row(qw + vw), pl.BlockSpec((bt, LANES), lambda i: (i, ZA_LAST // LANES)),
                  row(LANES), row(LANES), vec, vec, row(qw), row(qw), row(vw)],
        out_specs=[row(qw), row(qw + vw), row(LANES), vec, vec],
        out_shape=[jax.ShapeDtypeStruct((t, qw), MXU_DTYPE), jax.ShapeDtypeStruct((t, qw + vw), MXU_DTYPE),
                   jax.ShapeDtypeStruct((t, LANES), F32), jax.ShapeDtypeStruct((1, LANES), F32),
                   jax.ShapeDtypeStruct((1, LANES), F32)],
        compiler_params=_params("arbitrary"),
    )(q_raw, kv_raw, z, c_tab, s_tab, g_q, g_k, dqn, dkn, dv)


NT_DIMS = (((1,), (1,)), ((), ()))
TN_DIMS = (((0,), (0,)), ((), ()))


def _head_qk(q_ref, k_ref, e, mla, lo):
    if mla:
        return q_ref[:, e * LANES:(e + 1) * LANES], k_ref[:, e * LANES:(e + 1) * LANES]
    q = q_ref[...]
    return jnp.where(lo if e == 0 else jnp.logical_not(lo), q, jnp.zeros_like(q)), k_ref[...]


def _attn_specs(mla, blk, q_map, k_map):
    w = 2 * LANES if mla else LANES
    q_spec = pl.BlockSpec((blk, w), lambda j, a, b: (q_map(a, b), j))
    k_spec = pl.BlockSpec((blk, w), lambda j, a, b: (k_map(a, b), j))
    qv_spec = pl.BlockSpec((blk, LANES), lambda j, a, b: (q_map(a, b), j))
    kv_spec = pl.BlockSpec((blk, LANES), lambda j, a, b: (k_map(a, b), j))
    fq_spec = pl.BlockSpec((blk, LANES), lambda j, a, b: (q_map(a, b), 0))
    fk_spec = pl.BlockSpec((8, blk), lambda j, a, b: (0, k_map(a, b)))
    return q_spec, k_spec, qv_spec, kv_spec, fq_spec, fk_spec


def _flash_fwd(q, k, v, f, f_t, *, mla, scale, name):
    t = q.shape[0]
    blk = min(ATTN_BLOCK, t)
    nb = t // blk
    pairs = HEADS // 2
    q_spec, k_spec, qv_spec, kv_spec, fq_spec, fk_spec = _attn_specs(
        mla, blk, lambda i, kk: i, lambda i, kk: jnp.minimum(kk, i))

    def body(*refs):
        if mla:
            q_ref, k_ref, v_ref, o_ref, lse_ref, m_s, l_s, acc_s = refs
            fq_ref = fk_ref = None
        else:
            q_ref, k_ref, v_ref, fq_ref, fk_ref, o_ref, lse_ref, m_s, l_s, acc_s = refs
        j, i, kk = pl.program_id(0), pl.program_id(1), pl.program_id(2)
        lo = _iota((1, LANES), 1) < HEAD_V

        @pl.when(kk == 0)
        def _():
            m_s[...] = jnp.full_like(m_s, NEG)
            l_s[...] = jnp.zeros_like(l_s)
            acc_s[...] = jnp.zeros_like(acc_s)

        def step(mask):
            vv = v_ref[...]
            for e in range(2):
                s, _, _ = _scores(q_ref, k_ref, fq_ref, fk_ref, e, j, mla, scale, lo, mask)
                m_prev = m_s[e]
                m_new = jnp.maximum(m_prev, jnp.max(s, axis=1, keepdims=True))
                alpha = jnp.exp(m_prev - m_new)
                p = jnp.exp(s - m_new)
                if mask is not None:
                    p = jnp.where(mask, p, 0.0)
                l_s[e] = alpha * l_s[e] + jnp.sum(p, axis=1, keepdims=True)
                acc_s[e] = alpha * acc_s[e] + jnp.dot(p.astype(MXU_DTYPE), vv, preferred_element_type=F32)
                m_s[e] = m_new

        _masked_and_plain(kk <= i, i, kk, blk, step)

        @pl.when(kk == nb - 1)
        def _():
            valid = (i * blk + _iota((blk, 1), 0)) >= PAD
            outs, lses = [], []
            for e in range(2):
                l = l_s[e]
                outs.append(acc_s[e] * jnp.where(l > 0.0, 1.0 / jnp.where(l > 0.0, l, 1.0), 0.0))
                lses.append(m_s[e] + jnp.log(jnp.where(l > 0.0, l, 1.0)))
            o = jnp.where(lo, outs[0], outs[1])
            o_ref[...] = jnp.where(valid, o, 0.0).astype(o_ref.dtype)
            lane = _iota((1, LANES), 1)
            lse_ref[...] = jnp.where(lane == 0, lses[0], jnp.where(lane == 1, lses[1], 0.0))

    in_specs = [q_spec, k_spec, kv_spec] + ([] if mla else [fq_spec, fk_spec])
    args = (q, k, v) + (() if mla else (f, f_t))
    hv = HEADS * HEAD_V
    return pl.pallas_call(
        body, name=name, grid=(pairs, nb, nb),
        in_specs=in_specs, out_specs=[qv_spec, qv_spec],
        out_shape=[jax.ShapeDtypeStruct((t, hv), F32), jax.ShapeDtypeStruct((t, hv), F32)],
        scratch_shapes=[pltpu.VMEM((2, blk, 1), F32), pltpu.VMEM((2, blk, 1), F32), pltpu.VMEM((2, blk, LANES), F32)],
        compiler_params=_params("parallel", "parallel", "arbitrary"),
    )(*args)


def _bwd_tile(q_ref, k_ref, v_ref, o_ref, do_ref, lse_ref, fq_ref, fk_ref, e, pair, mla, scale, lo, mask):
    s, qe, ke = _scores(q_ref, k_ref, fq_ref, fk_ref, e, pair, mla, scale, lo, mask)
    p = jnp.exp(s - lse_ref[:, e:e + 1])
    if mask is not None:
        p = jnp.where(mask, p, 0.0)
    do = do_ref[...]
    doe = jnp.where(lo if e == 0 else jnp.logical_not(lo), do, jnp.zeros_like(do))
    dp = lax.dot_general(doe, v_ref[...], NT_DIMS, preferred_element_type=F32)
    delta = jnp.sum(doe.astype(F32) * o_ref[...].astype(F32), axis=1, keepdims=True)
    return p, p * (dp - delta), qe, ke


def _flash_bwd_dq(q, k, v, o, do, lse, f, f_t, *, mla, scale, col0, name):
    t = q.shape[0]
    blk = min(ATTN_BLOCK, t)
    nb = t // blk
    pairs = HEADS // 2
    w = 2 * LANES if mla else LANES
    q_spec, k_spec, qv_spec, kv_spec, fq_spec, fk_spec = _attn_specs(
        mla, blk, lambda i, kk: i, lambda i, kk: jnp.minimum(kk, i))
    od_spec = pl.BlockSpec((blk, LANES), lambda j, i, kk: (i, col0 + j))

    def body(*refs):
        if mla:
            q_ref, k_ref, v_ref, o_ref, do_ref, lse_ref, dq_ref, dq_s = refs
            fq_ref = fk_ref = rs_ref = rs_s = None
        else:
            q_ref, k_ref, v_ref, o_ref, do_ref, lse_ref, fq_ref, fk_ref, dq_ref, rs_ref, dq_s, rs_s = refs
        j, i, kk = pl.program_id(0), pl.program_id(1), pl.program_id(2)
        lo = _iota((1, LANES), 1) < HEAD_V

        @pl.when(kk == 0)
        def _():
            dq_s[...] = jnp.zeros_like(dq_s)
            if not mla:
                rs_s[...] = jnp.zeros_like(rs_s)

        def step(mask):
            for e in range(2):
                _, ds, _, ke = _bwd_tile(q_ref, k_ref, v_ref, o_ref, do_ref, lse_ref, fq_ref, fk_ref, e, j, mla, scale,
                                         lo, mask)
                dq_s[e] += jnp.dot(ds.astype(MXU_DTYPE), ke, preferred_element_type=F32)
                if not mla:
                    rs_s[e] += jnp.sum(ds, axis=1, keepdims=True)

        _masked_and_plain(kk <= i, i, kk, blk, step)

        @pl.when(kk == nb - 1)
        def _():
            if mla:
                dq_ref[:, 0:LANES] = dq_s[0] * scale
                dq_ref[:, LANES:2 * LANES] = dq_s[1] * scale
            else:
                dq_ref[...] = jnp.where(lo, dq_s[0], dq_s[1]) * scale
                lane = _iota((1, LANES), 1)
                rs_ref[...] = jnp.where(lane == 0, rs_s[0], jnp.where(lane == 1, rs_s[1], 0.0))

    in_specs = [q_spec, k_spec, kv_spec, od_spec, od_spec, qv_spec] + ([] if mla else [fq_spec, fk_spec])
    args = (q, k, v, o, do, lse) + (() if mla else (f, f_t))
    out_specs = [q_spec] + ([] if mla else [qv_spec])
    out_shape = [jax.ShapeDtypeStruct((t, pairs * w), F32)]
    scratch = [pltpu.VMEM((2, blk, LANES), F32)]
    if not mla:
        out_shape.append(jax.ShapeDtypeStruct((t, pairs * LANES), F32))
        scratch.append(pltpu.VMEM((2, blk, 1), F32))
    outs = pl.pallas_call(
        body, name=name, grid=(pairs, nb, nb),
        in_specs=in_specs, out_specs=out_specs, out_shape=out_shape, scratch_shapes=scratch,
        compiler_params=_params("parallel", "parallel", "arbitrary"),
    )(*args)
    return outs[0] if mla else outs


def _flash_bwd_dkv(q, k, v, o, do, lse, f, f_t, *, mla, scale, col0, name):
    t = q.shape[0]
    blk = min(ATTN_BLOCK, t)
    nb = t // blk
    pairs = HEADS // 2
    w = 2 * LANES if mla else LANES
    q_spec, k_spec, qv_spec, kv_spec, fq_spec, fk_spec = _attn_specs(
        mla, blk, lambda a, b: jnp.maximum(a, b), lambda a, b: a)
    od_spec = pl.BlockSpec((blk, LANES), lambda j, a, b: (jnp.maximum(a, b), col0 + j))
    cs_spec = pl.BlockSpec((8, blk), lambda j, a, b: (j, a))

    def body(*refs):
        if mla:
            q_ref, k_ref, v_ref, o_ref, do_ref, lse_ref, dk_ref, dv_ref, dk_s, dv_s = refs
            fq_ref = fk_ref = cs_ref = cs_s = None
        else:
            q_ref, k_ref, v_ref, o_ref, do_ref, lse_ref, fq_ref, fk_ref, dk_ref, dv_ref, cs_ref, dk_s, dv_s, cs_s = refs
        j, kb, qb = pl.program_id(0), pl.program_id(1), pl.program_id(2)
        lo = _iota((1, LANES), 1) < HEAD_V

        @pl.when(qb == 0)
        def _():
            dk_s[...] = jnp.zeros_like(dk_s)
            dv_s[...] = jnp.zeros_like(dv_s)
            if not mla:
                cs_s[...] = jnp.zeros_like(cs_s)

        def step(mask):
            do = do_ref[...]
            for e in range(2):
                p, ds, _, _ = _bwd_tile(q_ref, k_ref, v_ref, o_ref, do_ref, lse_ref, fq_ref, fk_ref, e, j, mla, scale,
                                        lo, mask)
                dv_s[e] += lax.dot_general(p.astype(MXU_DTYPE), do, TN_DIMS, preferred_element_type=F32)
                q_src = q_ref[:, e * LANES:(e + 1) * LANES] if mla else q_ref[...]
                dk_s[e] += lax.dot_general(ds.astype(MXU_DTYPE), q_src, TN_DIMS, preferred_element_type=F32)
                if not mla:
                    cs_s[e] += jnp.sum(ds, axis=0, keepdims=True)

        _masked_and_plain(qb >= kb, qb, kb, blk, step)

        @pl.when(qb == nb - 1)
        def _():
            dv_ref[...] = jnp.where(lo, dv_s[0], dv_s[1])
            if mla:
                dk_ref[:, 0:LANES] = dk_s[0] * scale
                dk_ref[:, LANES:2 * LANES] = dk_s[1] * scale
            else:
                dk_ref[...] = jnp.where(lo, dk_s[0], dk_s[1]) * scale
                sub = _iota((8, 1), 0)
                cs_ref[...] = jnp.where(sub == 0, cs_s[0], jnp.where(sub == 1, cs_s[1], 0.0))

    in_specs = [q_spec, k_spec, kv_spec, od_spec, od_spec, qv_spec] + ([] if mla else [fq_spec, fk_spec])
    args = (q, k, v, o, do, lse) + (() if mla else (f, f_t))
    out_specs = [k_spec, kv_spec] + ([] if mla else [cs_spec])
    out_shape = [jax.ShapeDtypeStruct((t, pairs * w), F32), jax.ShapeDtypeStruct((t, HEADS * HEAD_V), F32)]
    scratch = [pltpu.VMEM((2, blk, LANES), F32), pltpu.VMEM((2, blk, LANES), F32)]
    if not mla:
        out_shape.append(jax.ShapeDtypeStruct((pairs * 8, t), F32))
        scratch.append(pltpu.VMEM((2, 1, blk), F32))
    return pl.pallas_call(
        body, name=name, grid=(pairs, nb, nb),
        in_specs=in_specs, out_specs=out_specs, out_shape=out_shape, scratch_shapes=scratch,
        compiler_params=_params("parallel", "parallel", "arbitrary"),
    )(*args)


ATTN_CHUNK = 640
LOG2E = 1.4426950408889634
LN2 = 0.6931471805599453


def _for_chunks(n, ch, body):
    for c in range(n):
        body(c * ch)


def _select_lane(x, idx):
    return jnp.sum(jnp.where(_iota(x.shape, 1) == idx, x, 0.0), axis=1, keepdims=True)


def _select_row(x, idx):
    return jnp.sum(jnp.where(_iota(x.shape, 0) == idx, x, 0.0), axis=0, keepdims=True)


def _flash_fwd_chunked(q, k, v, f, f_t, *, mla, name):
    t = q.shape[0]
    blk = min(ATTN_BLOCK, t)
    nb = t // blk
    pairs = HEADS // 2
    ch = min(ATTN_CHUNK, blk)
    assert blk % ch == 0 and blk >= PAD
    w = 2 * LANES if mla else LANES
    q_of, k_of = _causal_pairs(nb, by_key=False)
    q_spec = pl.BlockSpec((blk, w), lambda j, s, qt, kt: (qt[s], j))
    k_spec = pl.BlockSpec((blk, w), lambda j, s, qt, kt: (kt[s], j))
    kv_spec = pl.BlockSpec((blk, LANES), lambda j, s, qt, kt: (kt[s], j))
    qv_spec = pl.BlockSpec((blk, LANES), lambda j, s, qt, kt: (qt[s], j))
    fq_spec = pl.BlockSpec((blk, LANES), lambda j, s, qt, kt: (qt[s], 0))
    fk_spec = pl.BlockSpec((8, blk), lambda j, s, qt, kt: (0, kt[s]))
    lse_spec = pl.BlockSpec((8, blk), lambda j, s, qt, kt: (j, qt[s]))

    def body(qt_ref, kt_ref, *refs):
        if mla:
            q_ref, k_ref, v_ref, o_ref, lse_ref, m_s, l_s, a_s, acc_s, s_s, p_s = refs
            fq_ref = fk_ref = None
        else:
            q_ref, k_ref, v_ref, fq_ref, fk_ref, o_ref, lse_ref, m_s, l_s, a_s, acc_s, s_s, p_s = refs
        j, step_id = pl.program_id(0), pl.program_id(1)
        i, kk = qt_ref[step_id], kt_ref[step_id]
        lo = _iota((1, LANES), 1) < HEAD_V

        @pl.when(kk == 0)
        def _():
            m_s[...] = jnp.full_like(m_s, NEG)
            l_s[...] = jnp.zeros_like(l_s)
            acc_s[...] = jnp.zeros_like(acc_s)

        def step(masked):
            vv = v_ref[...]
            for e in range(2):
                qe, ke = _head_qk(q_ref, k_ref, e, mla, lo)
                s_s[...] = lax.dot_general(qe, ke, NT_DIMS, preferred_element_type=F32)
                fkr = None if mla else _select_row(fk_ref[...], 2 * j + e)

                def chunk(r0, e=e, fkr=fkr):
                    rows = pl.ds(r0, ch)
                    s = s_s[rows, :]
                    if not mla:
                        s = s + _select_lane(fq_ref[rows, :], 2 * j + e) - fkr
                    if masked:
                        rpos = i * blk + r0 + _iota((ch, blk), 0)
                        cpos = kk * blk + _iota((ch, blk), 1)
                        mask = (cpos <= rpos) & (cpos >= PAD)
                        s = jnp.where(mask, s, NEG)
                    m_prev = m_s[e, rows, :]
                    m_new = jnp.maximum(m_prev, jnp.max(s, axis=1, keepdims=True))
                    alpha = jnp.exp2(m_prev - m_new)
                    p = jnp.exp2(s - m_new)
                    if masked:
                        p = jnp.where(mask, p, 0.0)
                    l_s[e, rows, :] = alpha * l_s[e, rows, :] + jnp.sum(p, axis=1, keepdims=True)
                    m_s[e, rows, :] = m_new
                    a_s[rows, :] = alpha
                    p_s[rows, :] = p.astype(p_s.dtype)

                _for_chunks(blk // ch, ch, chunk)
                acc_s[e] = a_s[...] * acc_s[e] + jnp.dot(p_s[...], vv, preferred_element_type=F32)

        needs_mask = (kk == i) | (kk == 0)

        @pl.when(needs_mask)
        def _():
            step(True)

        @pl.when(jnp.logical_not(needs_mask))
        def _():
            step(False)

        @pl.when(kk == i)
        def _():
            valid = (i * blk + _iota((blk, 1), 0)) >= PAD
            outs, lses = [], []
            for e in range(2):
                l = l_s[e]
                outs.append(acc_s[e] * jnp.where(l > 0.0, 1.0 / jnp.where(l > 0.0, l, 1.0), 0.0))
                lses.append(m_s[e] + jnp.log(jnp.where(l > 0.0, l, 1.0)) * LOG2E)
            o = jnp.where(lo, outs[0], outs[1])
            o_ref[...] = jnp.where(valid, o, 0.0).astype(o_ref.dtype)
            lane = _iota((1, LANES), 1)
            lse_cols = jnp.where(lane == 0, lses[0], jnp.where(lane == 1, lses[1], 0.0))
            lse_ref[...] = lse_cols.T[0:8, :]

    in_specs = [q_spec, k_spec, kv_spec] + ([] if mla else [fq_spec, fk_spec])
    args = (q, k, v) + (() if mla else (f, f_t))
    hv = HEADS * HEAD_V
    grid_spec = pltpu.PrefetchScalarGridSpec(
        num_scalar_prefetch=2, grid=(pairs, len(q_of)), in_specs=in_specs, out_specs=[qv_spec, lse_spec],
        scratch_shapes=[pltpu.VMEM((2, blk, 1), F32), pltpu.VMEM((2, blk, 1), F32), pltpu.VMEM((blk, 1), F32),
                        pltpu.VMEM((2, blk, LANES), F32), pltpu.VMEM((blk, blk), F32), pltpu.VMEM((blk, blk), MXU_DTYPE)])
    return pl.pallas_call(
        body, name=name, grid_spec=grid_spec,
        out_shape=[jax.ShapeDtypeStruct((t, hv), F32), jax.ShapeDtypeStruct((pairs * 8, t), F32)],
        compiler_params=_params("parallel", "arbitrary"),
    )(jnp.asarray(q_of), jnp.asarray(k_of), *args)


def _causal_pairs(nb, *, by_key):
    if by_key:
        pairs = [(qb, kb) for kb in range(nb) for qb in range(kb, nb)]
    else:
        pairs = [(qb, kb) for qb in range(nb) for kb in range(qb + 1)]
    return (np.asarray([p[0] for p in pairs], np.int32), np.asarray([p[1] for p in pairs], np.int32))


def _delta_rows(do, o):
    t, width = o.shape
    bt = min(ATTN_BLOCK, t)
    n_heads = width // HEAD_V

    def body(do_ref, o_ref, d_ref):
        prod = do_ref[...].astype(F32) * o_ref[...]
        col = _iota((width, LANES), 0)
        first = _iota((width, LANES), 1) * HEAD_V
        sel = jnp.where((col >= first) & (col < first + HEAD_V), 1.0, 0.0).astype(F32)
        per_head = jnp.dot(prod, sel, precision=lax.Precision.HIGHEST, preferred_element_type=F32)
        d_ref[...] = per_head.T[0:n_heads, :]

    return pl.pallas_call(
        body, name="attn_delta", grid=(t // bt,),
        in_specs=[pl.BlockSpec((bt, width), lambda i: (i, 0)), pl.BlockSpec((bt, width), lambda i: (i, 0))],
        out_specs=pl.BlockSpec((n_heads, bt), lambda i: (0, i)),
        out_shape=jax.ShapeDtypeStruct((n_heads, t), F32),
        compiler_params=_params("parallel"),
    )(do, o)


def _flash_bwd_fused(q, k, v, do, lse_t, delta_t, f, f_t, *, mla, col0, name):
    t = q.shape[0]
    blk = min(ATTN_BLOCK, t)
    nb = t // blk
    pairs = HEADS // 2
    ch = min(ATTN_CHUNK, blk)
    assert blk % ch == 0
    w = 2 * LANES if mla else LANES
    last = nb - 1
    assert blk >= PAD
    q_of, k_of = _causal_pairs(nb, by_key=True)
    q_spec = pl.BlockSpec((blk, w), lambda j, s, qt, kt: (qt[s], j))
    k_spec = pl.BlockSpec((blk, w), lambda j, s, qt, kt: (kt[s], j))
    v_spec = pl.BlockSpec((blk, LANES), lambda j, s, qt, kt: (kt[s], j))
    do_spec = pl.BlockSpec((blk, LANES), lambda j, s, qt, kt: (qt[s], col0 + j))
    lse_spec = pl.BlockSpec((8, blk), lambda j, s, qt, kt: (j, qt[s]))
    delta_spec = pl.BlockSpec((8, blk), lambda j, s, qt, kt: (col0 // (HEADS // 2), qt[s]))
    fq_spec = pl.BlockSpec((8, blk), lambda j, s, qt, kt: (0, qt[s]))
    fk_spec = pl.BlockSpec((blk, LANES), lambda j, s, qt, kt: (kt[s], 0))
    dq_spec = pl.BlockSpec((blk, w), lambda j, s, qt, kt: (kt[s], j))
    rs_spec = pl.BlockSpec((8, blk), lambda j, s, qt, kt: (j, kt[s]))
    cs_spec = pl.BlockSpec((blk, LANES), lambda j, s, qt, kt: (kt[s], j))

    def body(qt_ref, kt_ref, *refs):
        if mla:
            (q_ref, k_ref, v_ref, do_ref, lse_ref, delta_ref, dq_ref, dk_ref, dv_ref,
             dq_s, dk_s, dv_s, st_s, dpt_s, pt_s, dst_s) = refs
            fq_ref = fk_ref = rs_ref = cs_ref = rs_s = cs_s = None
        else:
            (q_ref, k_ref, v_ref, do_ref, lse_ref, delta_ref, fq_ref, fk_ref, dq_ref, dk_ref, dv_ref, rs_ref, cs_ref,
             dq_s, dk_s, dv_s, st_s, dpt_s, pt_s, dst_s, rs_s, cs_s) = refs
        j, step_id = pl.program_id(0), pl.program_id(1)
        qb, kb = qt_ref[step_id], kt_ref[step_id]
        lo = _iota((1, LANES), 1) < HEAD_V

        @pl.when(step_id == 0)
        def _():
            dq_s[...] = jnp.zeros_like(dq_s)
            if not mla:
                rs_s[...] = jnp.zeros_like(rs_s)

        @pl.when(qb == kb)
        def _():
            dk_s[...] = jnp.zeros_like(dk_s)
            dv_s[...] = jnp.zeros_like(dv_s)
            if not mla:
                cs_s[...] = jnp.zeros_like(cs_s)

        def step(masked):
            do = do_ref[...]
            vv = v_ref[...]
            for e in range(2):
                half = lo if e == 0 else jnp.logical_not(lo)
                qe, ke = _head_qk(q_ref, k_ref, e, mla, lo)
                doe = jnp.where(half, do, jnp.zeros_like(do))
                st_s[...] = lax.dot_general(ke, qe, NT_DIMS, preferred_element_type=F32)
                dpt_s[...] = lax.dot_general(vv, doe, NT_DIMS, preferred_element_type=F32)
                head = 2 * j + e
                lse_row = _select_row(lse_ref[...], e)
                delta_row = _select_row(delta_ref[...], head)
                fq_row = None if mla else _select_row(fq_ref[...], head)

                def chunk(r0, e=e, lse_row=lse_row, delta_row=delta_row, fq_row=fq_row, head=head):
                    rows = pl.ds(r0, ch)
                    s = st_s[rows, :]
                    if not mla:
                        s = s + fq_row - _select_lane(fk_ref[rows, :], head)
                    p = jnp.exp2(s - lse_row)
                    if masked:
                        kpos = kb * blk + r0 + _iota((ch, blk), 0)
                        qpos = qb * blk + _iota((ch, blk), 1)
                        p = jnp.where((kpos <= qpos) & (kpos >= PAD), p, 0.0)
                    ds = p * (dpt_s[rows, :] - delta_row)
                    pt_s[rows, :] = p.astype(pt_s.dtype)
                    dst_s[rows, :] = ds.astype(dst_s.dtype)
                    if not mla:
                        cs_s[e, rows, :] += jnp.sum(ds, axis=1, keepdims=True)
                        rs_s[qb, e] += jnp.sum(ds, axis=0, keepdims=True)

                _for_chunks(blk // ch, ch, chunk)
                dv_s[e] += jnp.dot(pt_s[...], do, preferred_element_type=F32)
                q_src = qe if mla else q_ref[...]
                dk_s[e] += jnp.dot(dst_s[...], q_src, preferred_element_type=F32)
                dq_s[qb, e] += lax.dot_general(dst_s[...], ke, TN_DIMS, preferred_element_type=F32)

        needs_mask = (qb == kb) | (kb == 0)

        @pl.when(needs_mask)
        def _():
            step(True)

        @pl.when(jnp.logical_not(needs_mask))
        def _():
            step(False)

        @pl.when(qb == last)
        def _():
            dv_ref[...] = jnp.where(lo, dv_s[0], dv_s[1])
            if mla:
                dk_ref[:, 0:LANES] = dk_s[0] * LN2
                dk_ref[:, LANES:2 * LANES] = dk_s[1] * LN2
            else:
                dk_ref[...] = jnp.where(lo, dk_s[0], dk_s[1]) * LN2
                lane = _iota((1, LANES), 1)
                cs_ref[...] = jnp.where(lane == 0, cs_s[0], jnp.where(lane == 1, cs_s[1], 0.0))

        @pl.when(qb == kb)
        def _():
            if mla:
                dq_ref[:, 0:LANES] = dq_s[qb, 0] * LN2
                dq_ref[:, LANES:2 * LANES] = dq_s[qb, 1] * LN2
            else:
                dq_ref[...] = jnp.where(lo, dq_s[qb, 0], dq_s[qb, 1]) * LN2
                sub = _iota((8, 1), 0)
                rs_ref[...] = jnp.where(sub == 0, rs_s[qb, 0], jnp.where(sub == 1, rs_s[qb, 1], 0.0))

    in_specs = [q_spec, k_spec, v_spec, do_spec, lse_spec, delta_spec] + ([] if mla else [fq_spec, fk_spec])
    args = (q, k, v, do, lse_t, delta_t) + (() if mla else (f_t, f))
    hv = HEADS * HEAD_V
    out_specs = [dq_spec, k_spec, v_spec]
    out_shape = [jax.ShapeDtypeStruct((t, pairs * w), F32), jax.ShapeDtypeStruct((t, pairs * w), F32),
                 jax.ShapeDtypeStruct((t, hv), F32)]
    scratch = [pltpu.VMEM((nb, 2, blk, LANES), F32), pltpu.VMEM((2, blk, LANES), F32), pltpu.VMEM((2, blk, LANES), F32),
               pltpu.VMEM((blk, blk), F32), pltpu.VMEM((blk, blk), F32), pltpu.VMEM((blk, blk), MXU_DTYPE),
               pltpu.VMEM((blk, blk), MXU_DTYPE)]
    if not mla:
        out_specs += [rs_spec, cs_spec]
        out_shape += [jax.ShapeDtypeStruct((pairs * 8, t), F32), jax.ShapeDtypeStruct((t, hv), F32)]
        scratch += [pltpu.VMEM((nb, 2, 1, blk), F32), pltpu.VMEM((2, blk, 1), F32)]
    grid_spec = pltpu.PrefetchScalarGridSpec(
        num_scalar_prefetch=2, grid=(pairs, len(q_of)), in_specs=in_specs, out_specs=out_specs, scratch_shapes=scratch)
    return pl.pallas_call(
        body, name=name, grid_spec=grid_spec, out_shape=out_shape,
        compiler_params=_params("parallel", "arbitrary"),
    )(jnp.asarray(q_of), jnp.asarray(k_of), *args)


def _shift_down(x, halo, n):
    rows = x.shape[0]
    r = _iota((rows, 1), 0)
    out = pltpu.roll(x, n, axis=0)
    for s in range(n):
        out = jnp.where(r == s, halo[8 - n + s:8 - n + s + 1, :], out)
    return out


def _shift_up(x, halo, n):
    rows = x.shape[0]
    r = _iota((rows, 1), 0)
    out = pltpu.roll(x, rows - n, axis=0)
    for s in range(n):
        out = jnp.where(r == rows - n + s, halo[s:s + 1, :], out)
    return out


def _conv_specs(bt, nblk):
    d = D_MODEL
    per8 = bt // 8
    z_spec = pl.BlockSpec((bt, 3 * d), lambda i: (i, 0))
    prev_spec = pl.BlockSpec((8, 3 * d), lambda i: (jnp.maximum(i * per8 - 1, 0), 0))
    next_z = pl.BlockSpec((8, 3 * d), lambda i: (jnp.minimum((i + 1) * per8, nblk * per8 - 1), 0))
    next_d = pl.BlockSpec((8, d), lambda i: (jnp.minimum((i + 1) * per8, nblk * per8 - 1), 0))
    w_spec = pl.BlockSpec((8, d), lambda i: (0, 0))
    row_spec = pl.BlockSpec((bt, d), lambda i: (i, 0))
    return z_spec, prev_spec, next_z, next_d, w_spec, row_spec


def _conv_taps(z_ref, prev_ref, i):
    d = D_MODEL
    g = z_ref[:, d:2 * d] * z_ref[:, 2 * d:3 * d]
    gh = jnp.where(i > 0, prev_ref[:, d:2 * d] * prev_ref[:, 2 * d:3 * d], 0.0)
    return g, _shift_down(g, gh, 1), _shift_down(g, gh, 2)


def _conv_fwd(z, conv_w8):
    t = z.shape[0]
    bt = min(PREP_TILE, t)
    nblk = t // bt
    d = D_MODEL
    z_spec, prev_spec, _, _, w_spec, row_spec = _conv_specs(bt, nblk)

    def body(z_ref, prev_ref, w_ref, v_ref):
        g, g1, g2 = _conv_taps(z_ref, prev_ref, pl.program_id(0))
        y = w_ref[0:1, :] * g2 + w_ref[1:2, :] * g1 + w_ref[2:3, :] * g
        v_ref[...] = (z_ref[:, 0:d] * y).astype(v_ref.dtype)

    return pl.pallas_call(
        body, name="conv_fwd", grid=(nblk,),
        in_specs=[z_spec, prev_spec, w_spec], out_specs=row_spec,
        out_shape=jax.ShapeDtypeStruct((t, d), MXU_DTYPE),
        compiler_params=_params("parallel"),
    )(z, z, conv_w8)


def _conv_bwd(z, conv_w8, dv):
    t = z.shape[0]
    bt = min(PREP_TILE, t)
    nblk = t // bt
    d = D_MODEL
    z_spec, prev_spec, next_z, next_d, w_spec, row_spec = _conv_specs(bt, nblk)

    def body(z_ref, prev_ref, nz_ref, dv_ref, ndv_ref, w_ref, dz_ref, dw_ref):
        i = pl.program_id(0)

        @pl.when(i == 0)
        def _():
            dw_ref[...] = jnp.zeros_like(dw_ref)

        g, g1, g2 = _conv_taps(z_ref, prev_ref, i)
        w0, w1, w2 = w_ref[0:1, :], w_ref[1:2, :], w_ref[2:3, :]
        y = w0 * g2 + w1 * g1 + w2 * g
        dvv = dv_ref[...].astype(F32)
        gate_b = z_ref[:, 0:d]
        dy = dvv * gate_b
        dyn = jnp.where(i < nblk - 1, ndv_ref[...].astype(F32) * nz_ref[:, 0:d], 0.0)
        dg = w2 * dy + w1 * _shift_up(dy, dyn, 1) + w0 * _shift_up(dy, dyn, 2)
        dz_ref[:, 0:d] = (dvv * y).astype(dz_ref.dtype)
        dz_ref[:, d:2 * d] = (dg * z_ref[:, 2 * d:3 * d]).astype(dz_ref.dtype)
        dz_ref[:, 2 * d:3 * d] = (dg * z_ref[:, d:2 * d]).astype(dz_ref.dtype)
        sub = _iota((8, 1), 0)
        s0 = jnp.sum(dy * g2, axis=0, keepdims=True)
        s1 = jnp.sum(dy * g1, axis=0, keepdims=True)
        s2 = jnp.sum(dy * g, axis=0, keepdims=True)
        dw_ref[...] += jnp.where(sub == 0, s0, jnp.where(sub == 1, s1, jnp.where(sub == 2, s2, 0.0)))

    return pl.pallas_call(
        body, name="conv_bwd", grid=(nblk,),
        in_specs=[z_spec, prev_spec, next_z, row_spec, next_d, w_spec], out_specs=[z_spec, w_spec],
        out_shape=[jax.ShapeDtypeStruct((t, 3 * d), MXU_DTYPE), jax.ShapeDtypeStruct((8, d), F32)],
        compiler_params=_params("arbitrary"),
    )(z, z, z, dv, dv, conv_w8)


def _loss_head(h, target):
    t, d = h.shape
    bt = LOSS_TILE
    assert LANES % bt == 0 or bt == LANES
    off = LANES // bt

    def body(h_ref, y_ref, dh_ref, acc_ref):
        i = pl.program_id(0)

        @pl.when(i == 0)
        def _():
            acc_ref[...] = jnp.zeros_like(acc_ref)

        @pl.when(i < off)
        def _():
            dh_ref[...] = jnp.zeros_like(dh_ref)

        @pl.when(i >= off)
        def _():
            err = h_ref[...] - y_ref[...]
            dh_ref[...] = err / d
            acc_ref[...] += jnp.sum(err * err)

    dh, acc = pl.pallas_call(
        body, name="loss_head", grid=(t // bt,),
        in_specs=[pl.BlockSpec((bt, d), lambda i: (i, 0)), pl.BlockSpec((bt, d), lambda i: (jnp.maximum(i - off, 0), 0))],
        out_specs=[pl.BlockSpec((bt, d), lambda i: (i, 0)), pl.BlockSpec((8, LANES), lambda i: (0, 0))],
        out_shape=[jax.ShapeDtypeStruct((t, d), F32), jax.ShapeDtypeStruct((8, LANES), F32)],
        compiler_params=_params("arbitrary"),
    )(h, target)
    return dh, acc[0, 0] * (0.5 / d)


def _common_tile(rows, row_off, cap=512, align=8):
    for b in range(min(cap, rows) // align * align, 0, -align):
        if rows % b == 0 and row_off % b == 0:
            return b
    raise ValueError((rows, row_off))


def _round_up(n, m):
    return -(-n // m) * m


def _adamw(w, m, v, g_buf, row_off, col_off):
    rows, width = w.shape
    wpad = _round_up(width, LANES)
    assert col_off % wpad == 0
    bt = _common_tile(rows, row_off)

    def body(w_ref, m_ref, v_ref, g_ref, go_ref, d_ref, nm_ref, nv_ref):
        gv = g_ref[...]
        if wpad != width:
            gv = gv[:, :width]
        m_new = ADAM_B1 * m_ref[...] + (1.0 - ADAM_B1) * gv
        v_new = ADAM_B2 * v_ref[...] + (1.0 - ADAM_B2) * jnp.square(gv)
        m_hat = m_new / (1.0 - ADAM_B1 ** ADAM_STEP)
        v_hat = v_new / (1.0 - ADAM_B2 ** ADAM_STEP)
        go_ref[...] = gv
        d_ref[...] = -ADAM_LR * (m_hat / (jnp.sqrt(v_hat) + ADAM_EPS) + ADAM_WD * w_ref[...])
        nm_ref[...] = m_new
        nv_ref[...] = v_new

    spec = pl.BlockSpec((bt, width), lambda i: (i, 0))
    g_spec = pl.BlockSpec((bt, wpad), lambda i: (row_off // bt + i, col_off // wpad))
    return pl.pallas_call(
        body, name="adamw", grid=(rows // bt,),
        in_specs=[spec] * 3 + [g_spec], out_specs=[spec] * 4,
        out_shape=[jax.ShapeDtypeStruct((rows, width), F32)] * 4,
        compiler_params=_params("parallel"),
    )(w, m, v, g_buf)


def _add_half(g, got, c, *, out_dtype, name):
    slabs, half, width = got.shape
    bt = next(x for x in range(min(half, 640), 0, -16) if half % x == 0)
    per_half = half // bt

    def body(c_ref, a_ref, b_ref, o_ref):
        o_ref[...] = (a_ref[...] + b_ref[...]).astype(o_ref.dtype)

    grid_spec = pltpu.PrefetchScalarGridSpec(
        num_scalar_prefetch=1, grid=(slabs, per_half),
        in_specs=[pl.BlockSpec((None, bt, width), lambda s, i, cc: (s, cc[0] * per_half + i, 0)),
                  pl.BlockSpec((None, bt, width), lambda s, i, cc: (s, i, 0))],
        out_specs=pl.BlockSpec((None, bt, width), lambda s, i, cc: (s, i, 0)))
    return pl.pallas_call(
        body, name=name, grid_spec=grid_spec,
        out_shape=jax.ShapeDtypeStruct((slabs, half, width), out_dtype), compiler_params=_params("parallel", "parallel"),
    )(jnp.reshape(c, (1,)).astype(jnp.int32), g, got)


def _sum4(parts, slot, *, name):
    _, rows, width = parts.shape
    bt = next(x for x in range(min(rows, 640), 0, -16) if rows % x == 0)

    def body(slot_ref, p_ref, o_ref):
        p = [p_ref[n].astype(F32) for n in range(4)]
        o_ref[...] = ((p[0] + p[1]) + p[2]) + p[3]

    grid_spec = pltpu.PrefetchScalarGridSpec(
        num_scalar_prefetch=1, grid=(rows // bt,),
        in_specs=[pl.BlockSpec((4, bt, width), lambda i, s: (0, i, 0))],
        out_specs=pl.BlockSpec((None, bt, width), lambda i, s: (s[0], i, 0)))
    return pl.pallas_call(
        body, name=name, grid_spec=grid_spec,
        out_shape=jax.ShapeDtypeStruct((2, rows, width), F32), compiler_params=_params("parallel"),
    )(jnp.reshape(slot, (1,)).astype(jnp.int32), parts)


ANY = pl.BlockSpec(memory_space=pl.ANY)
CHIP_FLIPS = ((1, 0), (0, 1), (1, 1))


def _place():
    return lax.axis_index("x"), lax.axis_index("y"), lax.axis_index("c")


def _flip(v, f):
    return 1 - v if f else v


def _allgather_chips(slabs):
    _, rows, width = slabs.shape
    half = rows // 2

    def body(_, out_ref, send_sems, recv_sems):
        x, y, c = _place()
        me = 2 * x + y
        sibling = (x, y, 1 - c)
        my_rows = pl.ds(pl.multiple_of(c * half, 8), half)
        sib_rows = pl.ds(pl.multiple_of((1 - c) * half, 8), half)
        first, passed = [], []
        for n, (fx, fy) in enumerate(CHIP_FLIPS):
            px, py = _flip(x, fx), _flip(y, fy)
            peer = 2 * px + py
            first.append(pltpu.make_async_remote_copy(
                src_ref=out_ref.at[me, my_rows], dst_ref=out_ref.at[me, my_rows],
                send_sem=send_sems.at[n], recv_sem=recv_sems.at[n], device_id=(px, py, c), device_id_type=MESH))
            passed.append(pltpu.make_async_remote_copy(
                src_ref=out_ref.at[peer, my_rows], dst_ref=out_ref.at[peer, my_rows],
                send_sem=send_sems.at[3 + n], recv_sem=recv_sems.at[3 + n], device_id=sibling, device_id_type=MESH))
        for cp in first:
            cp.start()
        for n, (fx, fy) in enumerate(CHIP_FLIPS):
            peer = 2 * _flip(x, fx) + _flip(y, fy)
            pltpu.make_async_remote_copy(
                src_ref=out_ref.at[me, my_rows], dst_ref=out_ref.at[peer, my_rows],
                send_sem=send_sems.at[n], recv_sem=recv_sems.at[n], device_id=sibling, device_id_type=MESH).wait_recv()
            passed[n].start()
        for n, (fx, fy) in enumerate(CHIP_FLIPS):
            peer = 2 * _flip(x, fx) + _flip(y, fy)
            pltpu.make_async_remote_copy(
                src_ref=out_ref.at[me, sib_rows], dst_ref=out_ref.at[peer, sib_rows],
                send_sem=send_sems.at[3 + n], recv_sem=recv_sems.at[3 + n], device_id=sibling,
                device_id_type=MESH).wait_recv()
        for cp in first + passed:
            cp.wait_send()

    return pl.pallas_call(
        body, name="allgather_weights",
        in_specs=[ANY], out_specs=ANY,
        out_shape=jax.ShapeDtypeStruct(slabs.shape, slabs.dtype), input_output_aliases={0: 0},
        scratch_shapes=[pltpu.SemaphoreType.DMA((6,)), pltpu.SemaphoreType.DMA((6,))],
    )(slabs)


def _swap_halves(g):
    _, rows, width = g.shape
    half = rows // 2

    def body(g_ref, got_ref, send_sem, recv_sem):
        x, y, c = _place()
        away = pl.ds(pl.multiple_of((1 - c) * half, 8), half)
        cp = pltpu.make_async_remote_copy(
            src_ref=g_ref.at[:, away], dst_ref=got_ref, send_sem=send_sem, recv_sem=recv_sem,
            device_id=(x, y, 1 - c), device_id_type=MESH)
        cp.start()
        cp.wait()

    return pl.pallas_call(
        body, name="grad_swap_halves",
        in_specs=[ANY], out_specs=ANY,
        out_shape=jax.ShapeDtypeStruct((4, half, width), g.dtype),
        scratch_shapes=[pltpu.SemaphoreType.DMA, pltpu.SemaphoreType.DMA],
    )(g)


def _scatter_chips(s):
    _, rows, width = s.shape

    def body(s_ref, out_ref, send_sems, recv_sems, local_sem):
        x, y, c = _place()
        me = 2 * x + y
        mine = pltpu.make_async_copy(s_ref.at[me], out_ref.at[me], local_sem)
        mine.start()
        copies = []
        for n, (fx, fy) in enumerate(CHIP_FLIPS):
            px, py = _flip(x, fx), _flip(y, fy)
            copies.append(pltpu.make_async_remote_copy(
                src_ref=s_ref.at[2 * px + py], dst_ref=out_ref.at[me],
                send_sem=send_sems.at[n], recv_sem=recv_sems.at[n], device_id=(px, py, c), device_id_type=MESH))
        for cp in copies:
            cp.start()
        for n, (fx, fy) in enumerate(CHIP_FLIPS):
            peer = 2 * _flip(x, fx) + _flip(y, fy)
            pltpu.make_async_remote_copy(
                src_ref=s_ref.at[me], dst_ref=out_ref.at[peer],
                send_sem=send_sems.at[n], recv_sem=recv_sems.at[n], device_id=(x, y, c), device_id_type=MESH).wait_recv()
        for cp in copies:
            cp.wait_send()
        mine.wait()

    return pl.pallas_call(
        body, name="grad_scatter_chips",
        in_specs=[ANY], out_specs=ANY,
        out_shape=jax.ShapeDtypeStruct((4, rows, width), s.dtype),
        scratch_shapes=[pltpu.SemaphoreType.DMA((3,)), pltpu.SemaphoreType.DMA((3,)), pltpu.SemaphoreType.DMA],
    )(s)


def _join_halves(halves):
    def body(_, out_ref, send_sem, recv_sem):
        x, y, c = _place()
        cp = pltpu.make_async_remote_copy(
            src_ref=out_ref.at[c], dst_ref=out_ref.at[c], send_sem=send_sem, recv_sem=recv_sem,
            device_id=(x, y, 1 - c), device_id_type=MESH)
        cp.start()
        pltpu.make_async_remote_copy(
            src_ref=out_ref.at[c], dst_ref=out_ref.at[1 - c], send_sem=send_sem, recv_sem=recv_sem,
            device_id=(x, y, 1 - c), device_id_type=MESH).wait_recv()
        cp.wait_send()

    return pl.pallas_call(
        body, name="grad_join_halves",
        in_specs=[ANY], out_specs=ANY,
        out_shape=jax.ShapeDtypeStruct(halves.shape, halves.dtype), input_output_aliases={0: 0},
        scratch_shapes=[pltpu.SemaphoreType.DMA, pltpu.SemaphoreType.DMA],
    )(halves)


PACK_W = 1024
REPLICATED = ("g_mix", "g_mlp", "g_cq", "g_ckv", "g_q_mla", "g_k_mla", "g_q_fox", "g_k_fox", "b_forget")
WEIGHT_ORDER = ("meta_tokens", "g_mix", "g_mlp", "w_in_attn", "g_cq", "w_uq", "g_ckv", "w_ukv", "g_q_mla", "g_k_mla",
                "g_q_fox", "g_k_fox", "b_forget", "w_out_attn", "w_in_conv", "conv_w", "w_out_conv", "w_mlp_up",
                "w_mlp_down")
N_EVEN = 2
N_ODD = 2
SHARD_IN = ATTN_IN // 4
SHARD_MIX = D_MODEL // 4
SHARD_UQ = HEADS * MLA_QK // 4
SHARD_UKV = HEADS * (MLA_NOPE + HEAD_V) // 4
SHARD_CONV = 3 * D_MODEL // 4
SIDE_W = 256
PK_UP = (0, 0)
PK_DOWN = (4096, 0)
PK_CONV_IN = (8192, 0)
PK_ATTN_IN = (10240, 0)
PK_OUT_ATTN = (12288, 0)
PK_OUT_CONV = (12800, 0)
PK_SMALL = (8192, 768)
PK_UQ = (10240, 768)
PK_UKV = (11008, 768)
PK_ROWS = 13312
SMALL_ROWS = 64
SMALL_META = 0
SMALL_CONV = 16
SMALL_REP = 24
SMALL_BITS_ROWS = 48
MATRIX_PLACES = (("w_mlp_up", PK_UP), ("w_mlp_down", PK_DOWN), ("w_in_conv", PK_CONV_IN), ("w_in_attn", PK_ATTN_IN),
                 ("w_out_attn", PK_OUT_ATTN), ("w_out_conv", PK_OUT_CONV), ("w_uq", PK_UQ), ("w_ukv", PK_UKV))


def _put(buf, x, place, *, name, slab=None):
    row_off, col_off = place
    slabs = x.ndim == 3
    rows, w = x.shape[-2:]
    wpad = _round_up(w, LANES)
    assert col_off % wpad == 0
    bt = _common_tile(rows, row_off, align=16)

    def fill(x_ref, o_ref):
        v = x_ref[...].astype(o_ref.dtype)
        if wpad != w:
            v = jnp.concatenate([v, jnp.zeros((bt, wpad - w), o_ref.dtype)], axis=1)
        o_ref[...] = v

    def body(x_ref, _, o_ref):
        fill(x_ref, o_ref)

    if slab is not None:
        grid_spec = pltpu.PrefetchScalarGridSpec(
            num_scalar_prefetch=1, grid=(rows // bt,),
            in_specs=[pl.BlockSpec((bt, w), lambda i, s: (i, 0)), ANY],
            out_specs=pl.BlockSpec((None, bt, wpad), lambda i, s: (s[0], row_off // bt + i, col_off // wpad)))
        return pl.pallas_call(
            lambda s_ref, x_ref, _, o_ref: fill(x_ref, o_ref), name=name, grid_spec=grid_spec,
            out_shape=jax.ShapeDtypeStruct(buf.shape, buf.dtype), input_output_aliases={2: 0},
            compiler_params=_params("parallel"),
        )(jnp.reshape(slab, (1,)).astype(jnp.int32), x, buf)
    if slabs:
        grid = (4, rows // bt)
        x_spec = pl.BlockSpec((None, bt, w), lambda s, i: (s, i, 0))
        o_spec = pl.BlockSpec((None, bt, wpad), lambda s, i: (s, row_off // bt + i, col_off // wpad))
        sem = ("parallel", "parallel")
    else:
        grid = (rows // bt,)
        x_spec = pl.BlockSpec((bt, w), lambda i: (i, 0))
        o_spec = pl.BlockSpec((bt, wpad), lambda i: (row_off // bt + i, col_off // wpad))
        sem = ("parallel",)
    return pl.pallas_call(
        body, name=name, grid=grid, in_specs=[x_spec, ANY], out_specs=o_spec,
        out_shape=jax.ShapeDtypeStruct(buf.shape, buf.dtype), input_output_aliases={1: 0},
        compiler_params=_params(*sem),
    )(x, buf)


def _w_cols(place, layer, rows, width):
    base = (place[0] + layer * rows) // rows
    return dict(n=4 * width, tn=width, tk=rows, spec=pl.BlockSpec((None, rows, width), lambda i, j, k: (j, base, 0)))


def _w_cols_t(place, layer, rows, width):
    base = (place[0] + layer * rows) // rows
    return dict(n=rows, tn=rows, tk=width, spec=pl.BlockSpec((None, rows, width), lambda i, j, k: (k, base, 0)))


def _w_rows(place, layer, rows):
    base = (place[0] + layer * rows) // rows
    return dict(n=D_MODEL, tn=D_MODEL, tk=rows, spec=pl.BlockSpec((None, rows, D_MODEL), lambda i, j, k: (k, base, 0)))


def _w_rows_t(place, layer, rows):
    base = (place[0] + layer * rows) // rows
    return dict(n=4 * rows, tn=rows, tk=D_MODEL, spec=pl.BlockSpec((None, rows, D_MODEL), lambda i, j, k: (j, base, 0)))


def _g_cols(g, place, layer, rows, width):
    base = (place[0] + layer * rows) // rows
    return g, pl.BlockSpec((None, rows, width), lambda i, j, k: (j, base, 0))


def _g_rows(g, place, layer, rows):
    base = (place[0] + layer * rows) // rows
    return g, pl.BlockSpec((None, rows, D_MODEL), lambda i, j, k: (i, base, 0))


def _g_rows_pairs(g, place, layer, rows):
    base = (place[0] + layer * rows) // rows
    return g, pl.BlockSpec((2, rows, D_MODEL), lambda i, j, k: (i, base, 0))


def _g_rows_whole(g, place, layer, rows):
    base = (place[0] + layer * rows) // rows
    return g, pl.BlockSpec((4, rows, D_MODEL), lambda i, j, k: (0, base, 0))


IN_PADW = _round_up(SHARD_IN, LANES)
IN_TAIL = ZA_FQ - SHARD_IN
IN_FL = SHARD_IN - HEADS
ZA_KPE = ZA_LAST + KPE_LANE


def _assemble_attn_in(gathered, layer):
    bt = 256
    base = (PK_ATTN_IN[0] + layer * D_MODEL) // bt
    assert 2 * SHARD_IN > ZA_FQ + MLA_ROPE and 3 * SHARD_IN < ATTN_IN - HEADS

    def body(s0, s1, s2, s3, o_ref):
        dt = o_ref.dtype
        z = lambda n: jnp.zeros((bt, n), dt)
        o_ref[...] = jnp.concatenate(
            [s0[:, :SHARD_IN], s1[:, :IN_TAIL], s1[:, IN_TAIL + MLA_ROPE:SHARD_IN], s2[:, :SHARD_IN], s3[:, :IN_FL],
             s3[:, IN_FL:SHARD_IN], z(KPE_LANE - HEADS), s1[:, IN_TAIL:IN_TAIL + MLA_ROPE],
             z(LANES - KPE_LANE - MLA_ROPE)], axis=1).astype(dt)

    def spec(s):
        return pl.BlockSpec((None, bt, IN_PADW), lambda i: (s, base + i, 0))

    return pl.pallas_call(
        body, name="assemble_attn_in", grid=(D_MODEL // bt,),
        in_specs=[spec(s) for s in range(4)], out_specs=pl.BlockSpec((bt, ZA_W), lambda i: (i, 0)),
        out_shape=jax.ShapeDtypeStruct((D_MODEL, ZA_W), MXU_DTYPE), compiler_params=_params("parallel"),
    )(gathered, gathered, gathered, gathered)


def _scatter_attn_in(g, dwa, layer):
    bt = 256
    base = (PK_ATTN_IN[0] + layer * D_MODEL) // bt
    fq1 = ZA_FQ + SHARD_IN - IN_TAIL - MLA_ROPE

    def body(d_ref, _, o_ref):
        pad = jnp.zeros((bt, IN_PADW - SHARD_IN), F32)
        pieces = (
            (d_ref[:, 0:SHARD_IN],),
            (d_ref[:, SHARD_IN:ZA_FQ], d_ref[:, ZA_KPE:ZA_KPE + MLA_ROPE], d_ref[:, ZA_FQ:fq1]),
            (d_ref[:, fq1:fq1 + SHARD_IN],),
            (d_ref[:, fq1 + SHARD_IN:ZA_LAST], d_ref[:, ZA_LAST:ZA_LAST + HEADS]),
        )
        for s in range(4):
            @pl.when(pl.program_id(0) == s)
            def _(s=s):
                o_ref[...] = jnp.concatenate(list(pieces[s]) + [pad], axis=1)

    return pl.pallas_call(
        body, name="scatter_attn_in", grid=(4, D_MODEL // bt),
        in_specs=[pl.BlockSpec((bt, ZA_W), lambda s, i: (i, 0)), ANY],
        out_specs=pl.BlockSpec((None, bt, IN_PADW), lambda s, i: (s, base + i, 0)),
        out_shape=jax.ShapeDtypeStruct(g.shape, g.dtype), input_output_aliases={1: 0},
        compiler_params=_params("parallel", "parallel"),
    )(dwa, g)


def _assemble_uq(gathered, layer):
    bt = 128
    base = (PK_UQ[0] + layer * Q_LORA) // bt
    col = PK_UQ[1] // SIDE_W

    def body(s0, s1, s2, s3, o_ref):
        dt = o_ref.dtype
        z = jnp.zeros((bt, LANES - MLA_QK), dt)
        parts = []
        for s_ref in (s0, s1, s2, s3):
            parts += [s_ref[:, 0:MLA_QK], z, s_ref[:, MLA_QK:2 * MLA_QK], z]
        o_ref[...] = jnp.concatenate(parts, axis=1).astype(dt)

    def spec(s):
        return pl.BlockSpec((None, bt, SIDE_W), lambda i: (s, base + i, col))

    return pl.pallas_call(
        body, name="assemble_uq", grid=(Q_LORA // bt,),
        in_specs=[spec(s) for s in range(4)], out_specs=pl.BlockSpec((bt, HEADS * LANES), lambda i: (i, 0)),
        out_shape=jax.ShapeDtypeStruct((Q_LORA, HEADS * LANES), MXU_DTYPE), compiler_params=_params("parallel"),
    )(gathered, gathered, gathered, gathered)


def _scatter_uq(g, dw, layer):
    bt = 128
    base = (PK_UQ[0] + layer * Q_LORA) // bt
    col = PK_UQ[1] // SIDE_W

    def body(d_ref, _, o_ref):
        o_ref[...] = jnp.concatenate([d_ref[:, 0:MLA_QK], d_ref[:, LANES:LANES + MLA_QK],
                                      jnp.zeros((bt, SIDE_W - 2 * MLA_QK), F32)], axis=1)

    return pl.pallas_call(
        body, name="scatter_uq", grid=(4, Q_LORA // bt),
        in_specs=[pl.BlockSpec((bt, 2 * LANES), lambda s, i: (i, s)), ANY],
        out_specs=pl.BlockSpec((None, bt, SIDE_W), lambda s, i: (s, base + i, col)),
        out_shape=jax.ShapeDtypeStruct(g.shape, g.dtype), input_output_aliases={1: 0},
        compiler_params=_params("parallel", "parallel"),
    )(dw, g)


def _assemble_ukv(gathered, layer):
    bt = KV_LORA
    base = (PK_UKV[0] + layer * KV_LORA) // bt
    col = PK_UKV[1] // SIDE_W
    hd = MLA_NOPE + HEAD_V

    def body(s0, s1, s2, s3, o_ref):
        dt = o_ref.dtype
        z = jnp.zeros((bt, LANES - MLA_NOPE), dt)
        keys, vals = [], []
        for s_ref in (s0, s1, s2, s3):
            for e in range(2):
                keys += [s_ref[:, e * hd:e * hd + MLA_NOPE], z]
                vals.append(s_ref[:, e * hd + MLA_NOPE:(e + 1) * hd])
        o_ref[...] = jnp.concatenate(keys + vals, axis=1).astype(dt)

    def spec(s):
        return pl.BlockSpec((None, bt, SIDE_W), lambda i: (s, base + i, col))

    return pl.pallas_call(
        body, name="assemble_ukv", grid=(1,),
        in_specs=[spec(s) for s in range(4)],
        out_specs=pl.BlockSpec((bt, HEADS * (LANES + HEAD_V)), lambda i: (i, 0)),
        out_shape=jax.ShapeDtypeStruct((KV_LORA, HEADS * (LANES + HEAD_V)), MXU_DTYPE), compiler_params=_params("parallel"),
    )(gathered, gathered, gathered, gathered)


def _scatter_ukv(g, dw, layer):
    bt = KV_LORA
    base = (PK_UKV[0] + layer * KV_LORA) // bt
    col = PK_UKV[1] // SIDE_W

    def body(k_ref, v_ref, _, o_ref):
        o_ref[...] = jnp.concatenate([k_ref[:, 0:MLA_NOPE], v_ref[:, 0:HEAD_V], k_ref[:, LANES:LANES + MLA_NOPE],
                                      v_ref[:, HEAD_V:2 * HEAD_V]], axis=1)

    return pl.pallas_call(
        body, name="scatter_ukv", grid=(4,),
        in_specs=[pl.BlockSpec((bt, 2 * LANES), lambda s: (0, s)),
                  pl.BlockSpec((bt, 2 * HEAD_V), lambda s: (0, HEADS * LANES // (2 * HEAD_V) + s)), ANY],
        out_specs=pl.BlockSpec((None, bt, SIDE_W), lambda s: (s, base, col)),
        out_shape=jax.ShapeDtypeStruct(g.shape, g.dtype), input_output_aliases={2: 0},
        compiler_params=_params("parallel"),
    )(dw, dw, g)


def _pad_lanes(v, n=LANES):
    return jnp.pad(v, (0, n - v.shape[0])).reshape(1, n)


def _relu2_up(acc):
    r = jnp.maximum(acc, 0.0)
    return acc, r * r


def _relu2_bwd(acc, u):
    return (acc * (2.0 * jnp.maximum(u, 0.0)),)


def _add_res(acc, res):
    return (acc + res,)


def _add_res_norm(acc, res, g):
    h = acc + res
    return h, h * lax.rsqrt(jnp.mean(h * h, axis=-1, keepdims=True) + EPS) * g


def _local_step(x, target, meta, small, gathered):
    seq = x.shape[0]
    t = seq + LANES
    d = D_MODEL
    h = jnp.concatenate([jnp.zeros((PAD, d), F32), meta.astype(F32), x], axis=0)
    c_tab, s_tab = _rope_tables(t)
    scale_mla, scale_fox = MLA_QK ** -0.5 * LOG2E, FOX_DIM ** -0.5 * LOG2E
    grads = {}
    saved = []
    g = jnp.zeros((4, PK_ROWS, PACK_W), F32)

    hn = _rmsnorm_fwd(h, small["g_mix"][0])
    for layer in range(DEPTH):
        j = layer // 2
        sv = {"h_in": h}
        sv["hn"] = hn
        g_mlp_row = small["g_mlp"][layer].reshape(1, d)
        if layer % 2 == 0:
            w_in = _assemble_attn_in(gathered, j)
            w_uq = _assemble_uq(gathered, j)
            w_ukv = _assemble_ukv(gathered, j)
            out_place = PK_OUT_ATTN
            vecs = dict(
                g_cq=small["g_cq"][j].reshape(1, Q_LORA), g_ckv=small["g_ckv"][j].reshape(1, KV_LORA),
                g_qf=jnp.tile(small["g_q_fox"][j] * scale_fox, 2).reshape(1, LANES),
                g_kf=jnp.tile(small["g_k_fox"][j], 2).reshape(1, LANES),
                b_f=_pad_lanes(small["b_forget"][j]), g_q=_pad_lanes(small["g_q_mla"][j] * scale_mla),
                g_k=_pad_lanes(small["g_k_mla"][j]))
            z = _matmul(hn, w_in, name="mm_attn_in")
            cqn, ckvn, qf, kf, vf, logf = _prep_a_fwd(z, vecs["g_cq"], vecs["g_ckv"], vecs["g_qf"], vecs["g_kf"], vecs["b_f"])
            f_cum, f_cum_t = _cumsum_rows(logf, reverse=False, name="cumsum_fwd", out_scale=LOG2E)
            q_raw = _matmul(cqn, w_uq, name="mm_uq")
            kv_raw = _matmul(ckvn, w_ukv, name="mm_ukv")
            qn, kn, v_mla = _prep_b_fwd(q_raw, kv_raw, z, c_tab, s_tab, vecs["g_q"], vecs["g_k"])
            o_mla, lse_mla = _flash_fwd_chunked(qn, kn, v_mla, None, None, mla=True, name="flash_fwd_mla")
            o_fox, lse_fox = _flash_fwd_chunked(qf, kf, vf, f_cum, f_cum_t, mla=False, name="flash_fwd_fox")
            o = jnp.concatenate([o_mla, o_fox], axis=1)
            h, hn2 = _matmul(o, gathered, b_tiles=_w_rows(out_place, j, SHARD_MIX), extras=(h, g_mlp_row),
                             epilogue=_add_res_norm, out_dtypes=(F32, MXU_DTYPE), name="mm_mix_out")
            sv.update(w_in=w_in, w_uq=w_uq, w_ukv=w_ukv, out_place=out_place, vecs=vecs, z=z, cqn=cqn, ckvn=ckvn, qf=qf, kf=kf,
                      vf=vf, f_cum=f_cum, f_cum_t=f_cum_t, q_raw=q_raw, kv_raw=kv_raw, qn=qn, kn=kn, v_mla=v_mla, o=o,
                      lse_mla=lse_mla, lse_fox=lse_fox)
        else:
            out_place = PK_OUT_CONV
            conv_w8 = jnp.pad(small["conv_w"][j], ((0, 5), (0, 0)))
            z = _matmul(hn, gathered, b_tiles=_w_cols(PK_CONV_IN, j, d, SHARD_CONV), name="mm_conv_in")
            vmix = _conv_fwd(z, conv_w8)
            h, hn2 = _matmul(vmix, gathered, b_tiles=_w_rows(out_place, j, SHARD_MIX), extras=(h, g_mlp_row),
                             epilogue=_add_res_norm, out_dtypes=(F32, MXU_DTYPE), name="mm_mix_out")
            sv.update(out_place=out_place, conv_w8=conv_w8, z=z, vmix=vmix)
        sv["h_mid"] = h
        u, a = _matmul(hn2, gathered, b_tiles=_w_cols(PK_UP, layer, d, d), epilogue=_relu2_up,
                       out_dtypes=(F32, MXU_DTYPE), name="mm_mlp_up")
        if layer + 1 < DEPTH:
            h, hn = _matmul(a, gathered, b_tiles=_w_rows(PK_DOWN, layer, d),
                            extras=(h, small["g_mix"][layer + 1].reshape(1, d)), epilogue=_add_res_norm,
                            out_dtypes=(F32, MXU_DTYPE), name="mm_mlp_down")
        else:
            h = _matmul(a, gathered, b_tiles=_w_rows(PK_DOWN, layer, d), extras=(h,), epilogue=_add_res,
                        name="mm_mlp_down")
        sv.update(hn2=hn2, u=u, a=a)
        saved.append(sv)

    dh, loss_local = _loss_head(h, target)

    dg_mix, dg_mlp = [None] * DEPTH, [None] * DEPTH
    per_even = {k: [None, None] for k in ("g_cq", "g_ckv", "g_q_mla", "g_k_mla", "g_q_fox", "g_k_fox", "b_forget")}
    per_odd = {"conv_w": [None, None]}
    for layer in reversed(range(DEPTH)):
        j = layer // 2
        sv = saved[layer]
        du = _matmul(dh, gathered, tb=True, b_tiles=_w_rows_t(PK_DOWN, layer, d), extras=(sv["u"],),
                     epilogue=_relu2_bwd, out_dtypes=(MXU_DTYPE,), name="mm_mlp_da")
        g = _matmul(sv["a"], dh, ta=True, tm=2 * d, out_into=_g_rows_pairs(g, PK_DOWN, layer, d), name="mm_dw_down")
        g = _matmul(sv["hn2"], du, ta=True, out_into=_g_cols(g, PK_UP, layer, d, d), name="mm_dw_up")
        dhn2 = _matmul(du, gathered, tb=True, b_tiles=_w_cols_t(PK_UP, layer, d, d), name="mm_mlp_dhn")
        dh, dg_mlp[layer] = _rmsnorm_bwd(sv["h_mid"], small["g_mlp"][layer], dhn2, dh)
        do = _matmul(dh, gathered, tb=True, b_tiles=_w_rows_t(sv["out_place"], j, SHARD_MIX), out_dtypes=(MXU_DTYPE,),
                     name="mm_mix_do")
        if layer % 2 == 0:
            vecs = sv["vecs"]
            g = _matmul(sv["o"], dh, ta=True, out_into=_g_rows_whole(g, PK_OUT_ATTN, j, SHARD_MIX),
                        name="mm_dw_out")
            delta_t = _delta_rows(do, sv["o"])
            dqn, dkn, dv_mla = _flash_bwd_fused(sv["qn"], sv["kn"], sv["v_mla"], do, sv["lse_mla"], delta_t, None, None,
                                                mla=True, col0=0, name="flash_bwd_mla")
            dqf, dkf, dvf, rs_t, cs = _flash_bwd_fused(sv["qf"], sv["kf"], sv["vf"], do, sv["lse_fox"], delta_t,
                                                       sv["f_cum"], sv["f_cum_t"], mla=False, col0=HEADS // 2,
                                                       name="flash_bwd_fox")
            d_f = rs_t.reshape(HEADS // 2, 8, t)[:, :2, :].reshape(HEADS, t).T
            d_f = d_f - cs.reshape(t, HEADS // 2, LANES)[:, :, :2].reshape(t, HEADS)
            d_f = jnp.pad(d_f, ((0, 0), (0, LANES - HEADS)))
            dlogf, _ = _cumsum_rows(d_f, reverse=True, name="cumsum_bwd")
            dq_raw, dkv_raw, dkpe, dg_q, dg_k = _prep_b_bwd(sv["q_raw"], sv["kv_raw"], sv["z"], c_tab, s_tab, vecs["g_q"],
                                                            vecs["g_k"], dqn, dkn, dv_mla)
            g = _scatter_uq(g, _matmul(sv["cqn"], dq_raw, ta=True, name="mm_dw_uq"), j)
            g = _scatter_ukv(g, _matmul(sv["ckvn"], dkv_raw, ta=True, name="mm_dw_ukv"), j)
            dcqn = _matmul(dq_raw, sv["w_uq"], tb=True, name="mm_dcqn")
            dckvn = _matmul(dkv_raw, sv["w_ukv"], tb=True, name="mm_dckvn")
            dz, dg_cq, dg_ckv, dg_qf, dg_kf, db_f = _prep_a_bwd(
                sv["z"], vecs["g_cq"], vecs["g_ckv"], vecs["g_qf"], vecs["g_kf"], vecs["b_f"], dcqn, dckvn, dqf, dkf, dvf,
                dlogf, dkpe)
            g = _scatter_attn_in(g, _matmul(sv["hn"], dz, ta=True, name="mm_dw_attn_in"), j)
            per_even["g_cq"][j] = dg_cq[0]
            per_even["g_ckv"][j] = dg_ckv[0]
            per_even["g_q_mla"][j] = dg_q[0, :MLA_QK] * scale_mla
            per_even["g_k_mla"][j] = dg_k[0, :MLA_QK]
            per_even["g_q_fox"][j] = (dg_qf[0, :FOX_DIM] + dg_qf[0, FOX_DIM:]) * scale_fox
            per_even["g_k_fox"][j] = dg_kf[0, :FOX_DIM] + dg_kf[0, FOX_DIM:]
            per_even["b_forget"][j] = db_f[0, :HEADS]
            dhn = _matmul(dz, sv["w_in"], tb=True, name="mm_attn_dhn")
        else:
            g = _matmul(sv["vmix"], dh, ta=True, out_into=_g_rows_whole(g, PK_OUT_CONV, j, SHARD_MIX),
                        name="mm_dw_out")
            dz, dcw = _conv_bwd(sv["z"], sv["conv_w8"], do)
            per_odd["conv_w"][j] = dcw[:3]
            g = _matmul(sv["hn"], dz, ta=True, tn=SHARD_CONV, out_into=_g_cols(g, PK_CONV_IN, j, d, SHARD_CONV),
                        name="mm_dw_conv_in")
            dhn = _matmul(dz, gathered, tb=True, b_tiles=_w_cols_t(PK_CONV_IN, j, d, SHARD_CONV), name="mm_conv_dhn")
        dh, dg_mix[layer] = _rmsnorm_bwd(sv["h_in"], small["g_mix"][layer], dhn, dh)

    grads["meta_tokens"] = dh[PAD:LANES]
    grads["g_mix"] = jnp.stack(dg_mix)
    grads["g_mlp"] = jnp.stack(dg_mlp)
    for k, v in list(per_even.items()) + list(per_odd.items()):
        grads[k] = jnp.stack(v)
    return loss_local, dh[LANES:], g, grads


def kernel(x, meta_tokens, g_mix, g_mlp, w_in_attn, g_cq, w_uq, g_ckv, w_ukv, g_q_mla, g_k_mla, g_q_fox, g_k_fox, b_forget, w_out_attn, w_in_conv, conv_w, w_out_conv, w_mlp_up, w_mlp_down, loss_target, m_meta_tokens, m_g_mix, m_g_mlp, m_w_in_attn, m_g_cq, m_w_uq, m_g_ckv, m_w_ukv, m_g_q_mla, m_g_k_mla, m_g_q_fox, m_g_k_fox, m_b_forget, m_w_out_attn, m_w_in_conv, m_conv_w, m_w_out_conv, m_w_mlp_up, m_w_mlp_down, v_meta_tokens, v_g_mix, v_g_mlp, v_w_in_attn, v_g_cq, v_w_uq, v_g_ckv, v_w_ukv, v_g_q_mla, v_g_k_mla, v_g_q_fox, v_g_k_fox, v_b_forget, v_w_out_attn, v_w_in_conv, v_conv_w, v_w_out_conv, v_w_mlp_up, v_w_mlp_down):
    args = dict(locals())
    weights = {n: args[n] for n in WEIGHT_ORDER}
    mom_m = {n: args["m_" + n] for n in WEIGHT_ORDER}
    mom_v = {n: args["v_" + n] for n in WEIGHT_ORDER}

    wire = jnp.bfloat16
    me = 2 * lax.axis_index("x") + lax.axis_index("y")
    buf = jnp.zeros((4, PK_ROWS, PACK_W), wire)
    for name, place in MATRIX_PLACES:
        w = weights[name]
        buf = _put(buf, w.reshape(-1, w.shape[-1]), place, name="pack_weights", slab=me)
    meta_bits = lax.bitcast_convert_type(meta_tokens, wire).reshape(2 * N_META, SIDE_W)
    conv_bits = lax.bitcast_convert_type(conv_w, wire).reshape(2 * N_ODD * 3, SIDE_W)
    bits = jnp.concatenate([meta_bits, conv_bits, jnp.zeros((SMALL_BITS_ROWS - 2 * N_META - 2 * N_ODD * 3, SIDE_W), wire)])
    buf = _put(buf, bits, PK_SMALL, name="pack_weights", slab=me)
    gathered = _allgather_chips(buf)
    got_bits = gathered[:, PK_SMALL[0]:PK_SMALL[0] + SMALL_BITS_ROWS, PK_SMALL[1]:PK_SMALL[1] + SIDE_W]
    meta_full = lax.bitcast_convert_type(got_bits[:, :2 * N_META].reshape(4, N_META, SIDE_W, 2), F32)
    meta_full = meta_full.transpose(1, 0, 2).reshape(N_META, D_MODEL)
    conv_full = lax.bitcast_convert_type(
        got_bits[:, 2 * N_META:2 * N_META + 2 * N_ODD * 3].reshape(4, N_ODD, 3, SIDE_W, 2), F32)
    small = {n: weights[n] for n in REPLICATED}
    small["conv_w"] = conv_full.transpose(1, 2, 0, 3).reshape(N_ODD, 3, D_MODEL)

    loss_local, grad_x, g, grads = _local_step(x[0], loss_target[0], meta_full, small, gathered)
    loss = lax.psum(loss_local, MESH_AXES)

    rep = jnp.concatenate([grads[n].reshape(-1) for n in REPLICATED])
    rep = jnp.pad(rep, (0, (SMALL_ROWS - SMALL_REP) * SIDE_W - rep.shape[0])).reshape(SMALL_ROWS - SMALL_REP, SIDE_W)
    g_meta = grads["meta_tokens"].reshape(N_META, 4, SIDE_W).transpose(1, 0, 2)
    g_conv = grads["conv_w"].reshape(N_ODD * 3, 4, SIDE_W).transpose(1, 0, 2)
    small4 = jnp.concatenate([g_meta, g_conv, jnp.zeros((4, SMALL_REP - SMALL_CONV - N_ODD * 3, SIDE_W), F32),
                              jnp.broadcast_to(rep[None], (4,) + rep.shape)], axis=1)
    g = _put(g, small4, PK_SMALL, name="pack_small_grads")
    half = PK_ROWS // 2
    c = lax.axis_index("c")
    got = _swap_halves(g)
    pair = _add_half(g, got, c, out_dtype=jnp.bfloat16, name="grad_pair_sum")
    total = _sum4(_scatter_chips(pair), c, name="grad_chip_sum")
    g_tot = _join_halves(total).reshape(PK_ROWS, PACK_W)

    out = {}
    for name, place in MATRIX_PLACES:
        shape = weights[name].shape
        two_d = lambda a: a.reshape(-1, shape[-1])
        res = _adamw(two_d(weights[name]), two_d(mom_m[name]), two_d(mom_v[name]), g_tot, place[0], place[1])
        out[name] = [r.reshape(shape) for r in res]

    def small_pack(src):
        flat = jnp.concatenate([src[n].reshape(-1) for n in REPLICATED])
        flat = jnp.pad(flat, (0, (SMALL_ROWS - SMALL_REP) * SIDE_W - flat.shape[0])).reshape(SMALL_ROWS - SMALL_REP, SIDE_W)
        return jnp.concatenate([src["meta_tokens"], src["conv_w"].reshape(N_ODD * 3, SIDE_W),
                                jnp.zeros((SMALL_REP - SMALL_CONV - N_ODD * 3, SIDE_W), F32), flat])

    res = _adamw(small_pack(weights), small_pack(mom_m), small_pack(mom_v), g_tot, PK_SMALL[0], PK_SMALL[1])
    for name in ("meta_tokens", "conv_w") + REPLICATED:
        out[name] = []
    for r in res:
        out["meta_tokens"].append(r[SMALL_META:SMALL_META + N_META])
        out["conv_w"].append(r[SMALL_CONV:SMALL_CONV + N_ODD * 3].reshape(N_ODD, 3, SIDE_W))
        flat, off = r[SMALL_REP:].reshape(-1), 0
        for name in REPLICATED:
            n = weights[name].size
            out[name].append(flat[off:off + n].reshape(weights[name].shape))
            off += n
    return (loss, grad_x[None], *[out[n][0] for n in WEIGHT_ORDER], *[out[n][1] for n in WEIGHT_ORDER],
            *[out[n][2] for n in WEIGHT_ORDER], *[out[n][3] for n in WEIGHT_ORDER])
```
